```python
import math
import jax, jax.numpy as jnp
from jax import lax
import numpy as np

D_MODEL = 1024
BATCH = 16
SEQ = 4096
DEPTH = 1

LRU_WIDTH = D_MODEL
LRU_HEADS = 16
LRU_HEAD_DIM = LRU_WIDTH // LRU_HEADS
LRU_C = 8.0
CONV_WIDTH = 4
SSD_INNER = 2 * D_MODEL
SSD_HEAD_DIM = 64
SSD_HEADS = SSD_INNER // SSD_HEAD_DIM
SSD_GROUPS = 8
SSD_HPG = SSD_HEADS // SSD_GROUPS
SSD_STATE = 128
SSD_CHUNK = 128
SSD_CONV_DIM = SSD_INNER + 2 * SSD_GROUPS * SSD_STATE
N_BRANCH = 2
D_FF = 4 * D_MODEL
N_MOD = 6
EPS = 1e-6
IN_SIZES = (LRU_WIDTH, LRU_WIDTH, SSD_INNER, SSD_CONV_DIM, SSD_HEADS, N_BRANCH * D_MODEL)
IN_DIM = sum(IN_SIZES)

kernel_name = "hybrid_rglru_ssd_gated_merge_block"


def rmsnorm(x, w):
    xf = x.astype(jnp.float32)
    y = xf * lax.rsqrt(jnp.mean(xf * xf, axis=-1, keepdims=True) + EPS)
    return (y * w.astype(jnp.float32)).astype(x.dtype)


def grouped_rmsnorm(y, w):
    b, s, d = y.shape
    yf = y.astype(jnp.float32).reshape(b, s, SSD_GROUPS, d // SSD_GROUPS)
    yf = yf * lax.rsqrt(jnp.mean(yf * yf, axis=-1, keepdims=True) + EPS)
    return (yf.reshape(b, s, d) * w.astype(jnp.float32)).astype(y.dtype)


def causal_dwconv(x, w, b):
    s = x.shape[1]
    xp = jnp.pad(x, ((0, 0), (CONV_WIDTH - 1, 0), (0, 0)))
    return sum((xp[:, k:k + s] * w[k] for k in range(CONV_WIDTH)), b)


def rglru(x, w_a, b_a, w_x, b_x, lam):
    bsz, s, _ = x.shape
    xh = x.reshape(bsz, s, LRU_HEADS, LRU_HEAD_DIM)
    r = jax.nn.sigmoid(jnp.einsum('bshi,hij->bshj', xh, w_a).reshape(bsz, s, LRU_WIDTH) + b_a)
    i = jax.nn.sigmoid(jnp.einsum('bshi,hij->bshj', xh, w_x).reshape(bsz, s, LRU_WIDTH) + b_x)
    log_a = (-LRU_C * r.astype(jnp.float32)) * jax.nn.softplus(-lam.astype(jnp.float32))
    a = jnp.exp(log_a)
    u = jnp.sqrt(-jnp.expm1(2.0 * log_a)) * (i * x).astype(jnp.float32)

    def combine(left, right):
        a1, b1 = left
        a2, b2 = right
        return a1 * a2, a2 * b1 + b2

    _, h = lax.associative_scan(combine, (a, u), axis=1)
    return h.astype(x.dtype)


def ssd_chunked_scan(xs, dt, A, Bm, Cm):
    bsz, s = xs.shape[:2]
    nc, L = s // SSD_CHUNK, SSD_CHUNK

    def to_chunks(t):
        return jnp.moveaxis(t.reshape((bsz, nc, L) + t.shape[2:]), 1, 0)

    xdt = (xs * dt[..., None]).reshape(bsz, s, SSD_GROUPS, SSD_HPG, SSD_HEAD_DIM)
    dA = (dt * A).reshape(bsz, s, SSD_GROUPS, SSD_HPG)
    causal = jnp.tril(jnp.ones((L, L), dtype=bool))[None, :, :, None, None]

    def step(state, inp):
        x_c, dA_c, B_c, C_c = inp
        cs = jnp.cumsum(dA_c, axis=1)
        seg = cs[:, :, None] - cs[:, None, :]
        decay = jnp.exp(jnp.where(causal, seg, -jnp.inf))
        cb = jnp.einsum('blgn,bsgn->blsg', C_c, B_c)
        y_diag = jnp.einsum('blsge,bsgep->blgep', cb[..., None] * decay, x_c)
        y_off = jnp.einsum('blgn,bgepn->blgep', C_c, state) * jnp.exp(cs)[..., None]
        decay_to_end = jnp.exp(cs[:, -1:] - cs)
        new_state = state * jnp.exp(cs[:, -1])[..., None, None] + jnp.einsum(
            'blgn,blgep->bgepn', B_c, x_c * decay_to_end[..., None])
        return new_state, y_diag + y_off

    init = jnp.zeros((bsz, SSD_GROUPS, SSD_HPG, SSD_HEAD_DIM, SSD_STATE), jnp.float32)
    _, y = lax.scan(step, init, (to_chunks(xdt), to_chunks(dA), to_chunks(Bm), to_chunks(Cm)))
    return jnp.moveaxis(y, 0, 1).reshape(bsz, s, SSD_HEADS, SSD_HEAD_DIM)


def ssd_branch(z, xbc, dt_raw, conv_w, conv_b, dt_bias, a_log, d_skip, norm_w):
    bsz, s, _ = z.shape
    xbc = jax.nn.silu(causal_dwconv(xbc, conv_w, conv_b))
    xs, Bm, Cm = jnp.split(xbc, [SSD_INNER, SSD_INNER + SSD_GROUPS * SSD_STATE], axis=-1)
    xs = xs.astype(jnp.float32).reshape(bsz, s, SSD_HEADS, SSD_HEAD_DIM)
    Bm = Bm.astype(jnp.float32).reshape(bsz, s, SSD_GROUPS, SSD_STATE)
    Cm = Cm.astype(jnp.float32).reshape(bsz, s, SSD_GROUPS, SSD_STATE)
    dt = jax.nn.softplus(dt_raw.astype(jnp.float32) + dt_bias.astype(jnp.float32))
    A = -jnp.exp(a_log.astype(jnp.float32))
    y = ssd_chunked_scan(xs, dt, A, Bm, Cm)
    y = y + d_skip.astype(jnp.float32)[:, None] * xs
    y = y.reshape(bsz, s, SSD_INNER) * jax.nn.silu(z.astype(jnp.float32))
    return grouped_rmsnorm(y, norm_w).astype(z.dtype)


def temporal_mixer(h, w_in, b_gate, lru_conv_w, lru_conv_b, lru_wa, lru_ba, lru_wx, lru_bx,
                   lru_lambda, w_pa, ssd_conv_w, ssd_conv_b, ssd_dt_bias, ssd_a_log, ssd_d,
                   ssd_norm_w, w_pb, w_out):
    bsz, s, _ = h.shape
    proj = h @ w_in
    lru_x, lru_g, ssd_z, ssd_xbc, ssd_dt, gates = jnp.split(
        proj, [int(v) for v in np.cumsum(IN_SIZES)[:-1]], axis=-1)
    ra = rglru(causal_dwconv(lru_x, lru_conv_w, lru_conv_b), lru_wa, lru_ba, lru_wx, lru_bx, lru_lambda)
    y_a = (ra * jax.nn.gelu(lru_g)) @ w_pa
    y_b = ssd_branch(ssd_z, ssd_xbc, ssd_dt, ssd_conv_w, ssd_conv_b, ssd_dt_bias, ssd_a_log,
                     ssd_d, ssd_norm_w) @ w_pb
    g = jax.nn.sigmoid(gates + b_gate).reshape(bsz, s, N_BRANCH, D_MODEL)
    merged = g[:, :, 0] * y_a + g[:, :, 1] * y_b
    return merged @ w_out


def _fwd_setup_inputs(seed: int = 0) -> dict:
    key = jax.random.key(seed)
    ks = jax.random.split(key, 32)
    f32 = jnp.float32

    def nrm(k, shape, scale):
        return jax.random.normal(k, shape, f32) * scale

    def gain(k, shape):
        return 1.0 + 0.1 * jax.random.normal(k, shape, f32)

    L = DEPTH
    u = jax.random.uniform(ks[12], (L, LRU_WIDTH), f32, minval=0.9, maxval=0.999)
    sroot = u ** (1.0 / LRU_C)
    lru_lambda = jnp.log(sroot) - jnp.log1p(-sroot)
    dt0 = jnp.exp(jax.random.uniform(ks[16], (L, SSD_HEADS), f32,
                                     minval=math.log(1e-3), maxval=math.log(1e-1)))
    ssd_dt_bias = dt0 + jnp.log(-jnp.expm1(-dt0))
    ssd_a_log = jnp.log(jax.random.uniform(ks[17], (L, SSD_HEADS), f32, minval=1.0, maxval=16.0))
    return {
        "x": jax.random.normal(ks[0], (BATCH, SEQ, D_MODEL), f32),
        "c": jax.random.normal(ks[1], (BATCH, D_MODEL), f32),
        "w_ada": nrm(ks[2], (L, D_MODEL, N_MOD * D_MODEL), 0.5 * D_MODEL ** -0.5),
        "b_ada": nrm(ks[3], (L, N_MOD * D_MODEL), 0.01),
        "pre_norm1": gain(ks[4], (L, D_MODEL)),
        "post_norm1": gain(ks[5], (L, D_MODEL)),
        "w_in": nrm(ks[6], (L, D_MODEL, IN_DIM), D_MODEL ** -0.5),
        "b_gate": nrm(ks[7], (L, N_BRANCH * D_MODEL), 0.01),
        "lru_conv_w": nrm(ks[8], (L, CONV_WIDTH, LRU_WIDTH), CONV_WIDTH ** -0.5),
        "lru_conv_b": nrm(ks[9], (L, LRU_WIDTH), 0.01),
        "lru_wa": nrm(ks[10], (L, LRU_HEADS, LRU_HEAD_DIM, LRU_HEAD_DIM), LRU_HEAD_DIM ** -0.5),
        "lru_ba": nrm(ks[11], (L, LRU_WIDTH), 0.01),
        "lru_wx": nrm(ks[13], (L, LRU_HEADS, LRU_HEAD_DIM, LRU_HEAD_DIM), LRU_HEAD_DIM ** -0.5),
        "lru_bx": nrm(ks[14], (L, LRU_WIDTH), 0.01),
        "lru_lambda": lru_lambda,
        "w_pa": nrm(ks[15], (L, LRU_WIDTH, D_MODEL), LRU_WIDTH ** -0.5),
        "ssd_conv_w": nrm(ks[18], (L, CONV_WIDTH, SSD_CONV_DIM), CONV_WIDTH ** -0.5),
        "ssd_conv_b": nrm(ks[19], (L, SSD_CONV_DIM), 0.01),
        "ssd_dt_bias": ssd_dt_bias,
        "ssd_a_log": ssd_a_log,
        "ssd_d": gain(ks[20], (L, SSD_HEADS)),
        "ssd_norm_w": gain(ks[21], (L, SSD_INNER)),
        "w_pb": nrm(ks[22], (L, SSD_INNER, D_MODEL), SSD_INNER ** -0.5),
        "w_out": nrm(ks[23], (L, D_MODEL, D_MODEL), D_MODEL ** -0.5),
        "pre_norm2": gain(ks[24], (L, D_MODEL)),
        "post_norm2": gain(ks[25], (L, D_MODEL)),
        "w_ff1": nrm(ks[26], (L, D_MODEL, D_FF), D_MODEL ** -0.5),
        "w_ff2": nrm(ks[27], (L, D_FF, D_MODEL), D_FF ** -0.5),
    }


def _fwd_reference(x, c, w_ada, b_ada, pre_norm1, post_norm1, w_in, b_gate, lru_conv_w, lru_conv_b,
              lru_wa, lru_ba, lru_wx, lru_bx, lru_lambda, w_pa, ssd_conv_w, ssd_conv_b,
              ssd_dt_bias, ssd_a_log, ssd_d, ssd_norm_w, w_pb, w_out, pre_norm2, post_norm2,
              w_ff1, w_ff2):
    c_act = jax.nn.silu(c)
    for l in range(DEPTH):
        mod = (c_act @ w_ada[l] + b_ada[l])[:, None, :]
        sh1, sc1, g1, sh2, sc2, g2 = jnp.split(mod, N_MOD, axis=-1)
        h = rmsnorm(x, pre_norm1[l]) * (1.0 + sc1) + sh1
        y = temporal_mixer(h, w_in[l], b_gate[l], lru_conv_w[l], lru_conv_b[l], lru_wa[l],
                           lru_ba[l], lru_wx[l], lru_bx[l], lru_lambda[l], w_pa[l],
                           ssd_conv_w[l], ssd_conv_b[l], ssd_dt_bias[l], ssd_a_log[l], ssd_d[l],
                           ssd_norm_w[l], w_pb[l], w_out[l])
        x = x + g1 * rmsnorm(y, post_norm1[l])
        h = rmsnorm(x, pre_norm2[l]) * (1.0 + sc2) + sh2
        y = jnp.square(jax.nn.relu(h @ w_ff1[l])) @ w_ff2[l]
        x = x + g2 * rmsnorm(y, post_norm2[l])
    return x


import jax as _jax
import jax.numpy as _jnp

TWIN_FORMAT = 'train_step'
FWD_PARAMS = ['x', 'c', 'w_ada', 'b_ada', 'pre_norm1', 'post_norm1', 'w_in', 'b_gate', 'lru_conv_w', 'lru_conv_b', 'lru_wa', 'lru_ba', 'lru_wx', 'lru_bx', 'lru_lambda', 'w_pa', 'ssd_conv_w', 'ssd_conv_b', 'ssd_dt_bias', 'ssd_a_log', 'ssd_d', 'ssd_norm_w', 'w_pb', 'w_out', 'pre_norm2', 'post_norm2', 'w_ff1', 'w_ff2']
TWIN_WEIGHTS = ['w_ada', 'b_ada', 'pre_norm1', 'post_norm1', 'w_in', 'b_gate', 'lru_conv_w', 'lru_conv_b', 'lru_wa', 'lru_ba', 'lru_wx', 'lru_bx', 'lru_lambda', 'w_pa', 'ssd_conv_w', 'ssd_conv_b', 'ssd_dt_bias', 'ssd_a_log', 'ssd_d', 'ssd_norm_w', 'w_pb', 'w_out', 'pre_norm2', 'post_norm2', 'w_ff1', 'w_ff2']
TWIN_DIFF_INPUT = 'x'
TWIN_INPUTS = ['x', 'c', 'w_ada', 'b_ada', 'pre_norm1', 'post_norm1', 'w_in', 'b_gate', 'lru_conv_w', 'lru_conv_b', 'lru_wa', 'lru_ba', 'lru_wx', 'lru_bx', 'lru_lambda', 'w_pa', 'ssd_conv_w', 'ssd_conv_b', 'ssd_dt_bias', 'ssd_a_log', 'ssd_d', 'ssd_norm_w', 'w_pb', 'w_out', 'pre_norm2', 'post_norm2', 'w_ff1', 'w_ff2', 'loss_target', 'm_w_ada', 'm_b_ada', 'm_pre_norm1', 'm_post_norm1', 'm_w_in', 'm_b_gate', 'm_lru_conv_w', 'm_lru_conv_b', 'm_lru_wa', 'm_lru_ba', 'm_lru_wx', 'm_lru_bx', 'm_lru_lambda', 'm_w_pa', 'm_ssd_conv_w', 'm_ssd_conv_b', 'm_ssd_dt_bias', 'm_ssd_a_log', 'm_ssd_d', 'm_ssd_norm_w', 'm_w_pb', 'm_w_out', 'm_pre_norm2', 'm_post_norm2', 'm_w_ff1', 'm_w_ff2', 'v_w_ada', 'v_b_ada', 'v_pre_norm1', 'v_post_norm1', 'v_w_in', 'v_b_gate', 'v_lru_conv_w', 'v_lru_conv_b', 'v_lru_wa', 'v_lru_ba', 'v_lru_wx', 'v_lru_bx', 'v_lru_lambda', 'v_w_pa', 'v_ssd_conv_w', 'v_ssd_conv_b', 'v_ssd_dt_bias', 'v_ssd_a_log', 'v_ssd_d', 'v_ssd_norm_w', 'v_w_pb', 'v_w_out', 'v_pre_norm2', 'v_post_norm2', 'v_w_ff1', 'v_w_ff2']
TWIN_OUTPUTS = ['loss', 'grad_x', 'grad_w_ada', 'grad_b_ada', 'grad_pre_norm1', 'grad_post_norm1', 'grad_w_in', 'grad_b_gate', 'grad_lru_conv_w', 'grad_lru_conv_b', 'grad_lru_wa', 'grad_lru_ba', 'grad_lru_wx', 'grad_lru_bx', 'grad_lru_lambda', 'grad_w_pa', 'grad_ssd_conv_w', 'grad_ssd_conv_b', 'grad_ssd_dt_bias', 'grad_ssd_a_log', 'grad_ssd_d', 'grad_ssd_norm_w', 'grad_w_pb', 'grad_w_out', 'grad_pre_norm2', 'grad_post_norm2', 'grad_w_ff1', 'grad_w_ff2', 'delta_w_ada', 'delta_b_ada', 'delta_pre_norm1', 'delta_post_norm1', 'delta_w_in', 'delta_b_gate', 'delta_lru_conv_w', 'delta_lru_conv_b', 'delta_lru_wa', 'delta_lru_ba', 'delta_lru_wx', 'delta_lru_bx', 'delta_lru_lambda', 'delta_w_pa', 'delta_ssd_conv_w', 'delta_ssd_conv_b', 'delta_ssd_dt_bias', 'delta_ssd_a_log', 'delta_ssd_d', 'delta_ssd_norm_w', 'delta_w_pb', 'delta_w_out', 'delta_pre_norm2', 'delta_post_norm2', 'delta_w_ff1', 'delta_w_ff2', 'new_m_w_ada', 'new_m_b_ada', 'new_m_pre_norm1', 'new_m_post_norm1', 'new_m_w_in', 'new_m_b_gate', 'new_m_lru_conv_w', 'new_m_lru_conv_b', 'new_m_lru_wa', 'new_m_lru_ba', 'new_m_lru_wx', 'new_m_lru_bx', 'new_m_lru_lambda', 'new_m_w_pa', 'new_m_ssd_conv_w', 'new_m_ssd_conv_b', 'new_m_ssd_dt_bias', 'new_m_ssd_a_log', 'new_m_ssd_d', 'new_m_ssd_norm_w', 'new_m_w_pb', 'new_m_w_out', 'new_m_pre_norm2', 'new_m_post_norm2', 'new_m_w_ff1', 'new_m_w_ff2', 'new_v_w_ada', 'new_v_b_ada', 'new_v_pre_norm1', 'new_v_post_norm1', 'new_v_w_in', 'new_v_b_gate', 'new_v_lru_conv_w', 'new_v_lru_conv_b', 'new_v_lru_wa', 'new_v_lru_ba', 'new_v_lru_wx', 'new_v_lru_bx', 'new_v_lru_lambda', 'new_v_w_pa', 'new_v_ssd_conv_w', 'new_v_ssd_conv_b', 'new_v_ssd_dt_bias', 'new_v_ssd_a_log', 'new_v_ssd_d', 'new_v_ssd_norm_w', 'new_v_w_pb', 'new_v_w_out', 'new_v_pre_norm2', 'new_v_post_norm2', 'new_v_w_ff1', 'new_v_w_ff2']
TWIN_LEAF_KINDS = {'loss': 'loss', 'grad_x': 'grad_x', 'grad_w_ada': 'grad_w', 'grad_b_ada': 'grad_w', 'grad_pre_norm1': 'grad_w', 'grad_post_norm1': 'grad_w', 'grad_w_in': 'grad_w', 'grad_b_gate': 'grad_w', 'grad_lru_conv_w': 'grad_w', 'grad_lru_conv_b': 'grad_w', 'grad_lru_wa': 'grad_w', 'grad_lru_ba': 'grad_w', 'grad_lru_wx': 'grad_w', 'grad_lru_bx': 'grad_w', 'grad_lru_lambda': 'grad_w', 'grad_w_pa': 'grad_w', 'grad_ssd_conv_w': 'grad_w', 'grad_ssd_conv_b': 'grad_w', 'grad_ssd_dt_bias': 'grad_w', 'grad_ssd_a_log': 'grad_w', 'grad_ssd_d': 'grad_w', 'grad_ssd_norm_w': 'grad_w', 'grad_w_pb': 'grad_w', 'grad_w_out': 'grad_w', 'grad_pre_norm2': 'grad_w', 'grad_post_norm2': 'grad_w', 'grad_w_ff1': 'grad_w', 'grad_w_ff2': 'grad_w', 'delta_w_ada': 'delta_w', 'delta_b_ada': 'delta_w', 'delta_pre_norm1': 'delta_w', 'delta_post_norm1': 'delta_w', 'delta_w_in': 'delta_w', 'delta_b_gate': 'delta_w', 'delta_lru_conv_w': 'delta_w', 'delta_lru_conv_b': 'delta_w', 'delta_lru_wa': 'delta_w', 'delta_lru_ba': 'delta_w', 'delta_lru_wx': 'delta_w', 'delta_lru_bx': 'delta_w', 'delta_lru_lambda': 'delta_w', 'delta_w_pa': 'delta_w', 'delta_ssd_conv_w': 'delta_w', 'delta_ssd_conv_b': 'delta_w', 'delta_ssd_dt_bias': 'delta_w', 'delta_ssd_a_log': 'delta_w', 'delta_ssd_d': 'delta_w', 'delta_ssd_norm_w': 'delta_w', 'delta_w_pb': 'delta_w', 'delta_w_out': 'delta_w', 'delta_pre_norm2': 'delta_w', 'delta_post_norm2': 'delta_w', 'delta_w_ff1': 'delta_w', 'delta_w_ff2': 'delta_w', 'new_m_w_ada': 'new_m', 'new_m_b_ada': 'new_m', 'new_m_pre_norm1': 'new_m', 'new_m_post_norm1': 'new_m', 'new_m_w_in': 'new_m', 'new_m_b_gate': 'new_m', 'new_m_lru_conv_w': 'new_m', 'new_m_lru_conv_b': 'new_m', 'new_m_lru_wa': 'new_m', 'new_m_lru_ba': 'new_m', 'new_m_lru_wx': 'new_m', 'new_m_lru_bx': 'new_m', 'new_m_lru_lambda': 'new_m', 'new_m_w_pa': 'new_m', 'new_m_ssd_conv_w': 'new_m', 'new_m_ssd_conv_b': 'new_m', 'new_m_ssd_dt_bias': 'new_m', 'new_m_ssd_a_log': 'new_m', 'new_m_ssd_d': 'new_m', 'new_m_ssd_norm_w': 'new_m', 'new_m_w_pb': 'new_m', 'new_m_w_out': 'new_m', 'new_m_pre_norm2': 'new_m', 'new_m_post_norm2': 'new_m', 'new_m_w_ff1': 'new_m', 'new_m_w_ff2': 'new_m', 'new_v_w_ada': 'new_v', 'new_v_b_ada': 'new_v', 'new_v_pre_norm1': 'new_v', 'new_v_post_norm1': 'new_v', 'new_v_w_in': 'new_v', 'new_v_b_gate': 'new_v', 'new_v_lru_conv_w': 'new_v', 'new_v_lru_conv_b': 'new_v', 'new_v_lru_wa': 'new_v', 'new_v_lru_ba': 'new_v', 'new_v_lru_wx': 'new_v', 'new_v_lru_bx': 'new_v', 'new_v_lru_lambda': 'new_v', 'new_v_w_pa': 'new_v', 'new_v_ssd_conv_w': 'new_v', 'new_v_ssd_conv_b': 'new_v', 'new_v_ssd_dt_bias': 'new_v', 'new_v_ssd_a_log': 'new_v', 'new_v_ssd_d': 'new_v', 'new_v_ssd_norm_w': 'new_v', 'new_v_w_pb': 'new_v', 'new_v_w_out': 'new_v', 'new_v_pre_norm2': 'new_v', 'new_v_post_norm2': 'new_v', 'new_v_w_ff1': 'new_v', 'new_v_w_ff2': 'new_v'}


def _forward(args):
    return _fwd_reference(*[args[k] for k in FWD_PARAMS])


def _output_shape():
    out = _jax.eval_shape(lambda: _forward(_fwd_setup_inputs(0)))
    return out.shape, out.dtype

N_MICROBATCH = 1
ADAM_LR = 0.001
ADAM_B1 = 0.9
ADAM_B2 = 0.999
ADAM_EPS = 1e-08
ADAM_WD = 0.01
ADAM_STEP = 10
PER_EXAMPLE_BATCH_AXIS = {'x': 0, 'c': 0, 'loss_target': 0}
SHARED_INPUTS = []
_WEIGHT_DTYPES = {'w_ada': _jnp.float32, 'b_ada': _jnp.float32, 'pre_norm1': _jnp.float32, 'post_norm1': _jnp.float32, 'w_in': _jnp.float32, 'b_gate': _jnp.float32, 'lru_conv_w': _jnp.float32, 'lru_conv_b': _jnp.float32, 'lru_wa': _jnp.float32, 'lru_ba': _jnp.float32, 'lru_wx': _jnp.float32, 'lru_bx': _jnp.float32, 'lru_lambda': _jnp.float32, 'w_pa': _jnp.float32, 'ssd_conv_w': _jnp.float32, 'ssd_conv_b': _jnp.float32, 'ssd_dt_bias': _jnp.float32, 'ssd_a_log': _jnp.float32, 'ssd_d': _jnp.float32, 'ssd_norm_w': _jnp.float32, 'w_pb': _jnp.float32, 'w_out': _jnp.float32, 'pre_norm2': _jnp.float32, 'post_norm2': _jnp.float32, 'w_ff1': _jnp.float32, 'w_ff2': _jnp.float32}
MOMENT_SCALE = {'w_ada': 3.775916e+00, 'b_ada': 7.054629e+00, 'pre_norm1': 1.779329e-01, 'post_norm1': 7.592230e+00, 'w_in': 1.917411e-01, 'b_gate': 1.907433e-01, 'lru_conv_w': 7.468987e-01, 'lru_conv_b': 2.598418e+00, 'lru_wa': 7.557950e-02, 'lru_ba': 1.250382e-01, 'lru_wx': 1.858841e-01, 'lru_bx': 2.662810e-01, 'lru_lambda': 2.996617e-01, 'w_pa': 7.857701e-01, 'ssd_conv_w': 6.391345e-02, 'ssd_conv_b': 1.578449e-01, 'ssd_dt_bias': 1.158064e-01, 'ssd_a_log': 5.222633e-01, 'ssd_d': 3.145548e-01, 'ssd_norm_w': 1.345043e-01, 'w_pb': 1.801664e-01, 'w_out': 6.601359e-01, 'pre_norm2': 2.061128e-01, 'post_norm2': 7.749385e+00, 'w_ff1': 1.585389e-01, 'w_ff2': 7.904445e-01}


def _to_microbatches(a, axis):
    t = _jnp.moveaxis(a, axis, 0)
    t = t.reshape((N_MICROBATCH, t.shape[0] // N_MICROBATCH) + t.shape[1:])
    return _jnp.moveaxis(t, 1, axis + 1)


def setup_inputs(seed: int = 0) -> dict:
    inp = _fwd_setup_inputs(seed)
    key = _jax.random.fold_in(_jax.random.key(seed), 7919)
    shape, _ = _output_shape()
    out = dict(inp)
    out["loss_target"] = _jax.random.normal(_jax.random.fold_in(key, 0), shape, _jnp.float32)
    for i, name in enumerate(TWIN_WEIGHTS):
        w = inp[name].astype(_jnp.float32)
        if MOMENT_SCALE is None:
            s = _jnp.sqrt(_jnp.mean(_jnp.square(w)) + 1e-30)
        else:
            s = MOMENT_SCALE[name]
        km, kv = _jax.random.split(_jax.random.fold_in(key, i + 1))
        out[name] = w
        out["m_" + name] = s * _jax.random.normal(km, w.shape, _jnp.float32)
        out["v_" + name] = (s * s) * _jax.random.uniform(kv, w.shape, _jnp.float32, 0.5, 1.5)
    if N_MICROBATCH > 1:
        for name, axis in PER_EXAMPLE_BATCH_AXIS.items():
            out[name] = _to_microbatches(out[name], axis)
    return {'x': out['x'], 'c': out['c'], 'w_ada': out['w_ada'], 'b_ada': out['b_ada'], 'pre_norm1': out['pre_norm1'], 'post_norm1': out['post_norm1'], 'w_in': out['w_in'], 'b_gate': out['b_gate'], 'lru_conv_w': out['lru_conv_w'], 'lru_conv_b': out['lru_conv_b'], 'lru_wa': out['lru_wa'], 'lru_ba': out['lru_ba'], 'lru_wx': out['lru_wx'], 'lru_bx': out['lru_bx'], 'lru_lambda': out['lru_lambda'], 'w_pa': out['w_pa'], 'ssd_conv_w': out['ssd_conv_w'], 'ssd_conv_b': out['ssd_conv_b'], 'ssd_dt_bias': out['ssd_dt_bias'], 'ssd_a_log': out['ssd_a_log'], 'ssd_d': out['ssd_d'], 'ssd_norm_w': out['ssd_norm_w'], 'w_pb': out['w_pb'], 'w_out': out['w_out'], 'pre_norm2': out['pre_norm2'], 'post_norm2': out['post_norm2'], 'w_ff1': out['w_ff1'], 'w_ff2': out['w_ff2'], 'loss_target': out['loss_target'], 'm_w_ada': out['m_w_ada'], 'm_b_ada': out['m_b_ada'], 'm_pre_norm1': out['m_pre_norm1'], 'm_post_norm1': out['m_post_norm1'], 'm_w_in': out['m_w_in'], 'm_b_gate': out['m_b_gate'], 'm_lru_conv_w': out['m_lru_conv_w'], 'm_lru_conv_b': out['m_lru_conv_b'], 'm_lru_wa': out['m_lru_wa'], 'm_lru_ba': out['m_lru_ba'], 'm_lru_wx': out['m_lru_wx'], 'm_lru_bx': out['m_lru_bx'], 'm_lru_lambda': out['m_lru_lambda'], 'm_w_pa': out['m_w_pa'], 'm_ssd_conv_w': out['m_ssd_conv_w'], 'm_ssd_conv_b': out['m_ssd_conv_b'], 'm_ssd_dt_bias': out['m_ssd_dt_bias'], 'm_ssd_a_log': out['m_ssd_a_log'], 'm_ssd_d': out['m_ssd_d'], 'm_ssd_norm_w': out['m_ssd_norm_w'], 'm_w_pb': out['m_w_pb'], 'm_w_out': out['m_w_out'], 'm_pre_norm2': out['m_pre_norm2'], 'm_post_norm2': out['m_post_norm2'], 'm_w_ff1': out['m_w_ff1'], 'm_w_ff2': out['m_w_ff2'], 'v_w_ada': out['v_w_ada'], 'v_b_ada': out['v_b_ada'], 'v_pre_norm1': out['v_pre_norm1'], 'v_post_norm1': out['v_post_norm1'], 'v_w_in': out['v_w_in'], 'v_b_gate': out['v_b_gate'], 'v_lru_conv_w': out['v_lru_conv_w'], 'v_lru_conv_b': out['v_lru_conv_b'], 'v_lru_wa': out['v_lru_wa'], 'v_lru_ba': out['v_lru_ba'], 'v_lru_wx': out['v_lru_wx'], 'v_lru_bx': out['v_lru_bx'], 'v_lru_lambda': out['v_lru_lambda'], 'v_w_pa': out['v_w_pa'], 'v_ssd_conv_w': out['v_ssd_conv_w'], 'v_ssd_conv_b': out['v_ssd_conv_b'], 'v_ssd_dt_bias': out['v_ssd_dt_bias'], 'v_ssd_a_log': out['v_ssd_a_log'], 'v_ssd_d': out['v_ssd_d'], 'v_ssd_norm_w': out['v_ssd_norm_w'], 'v_w_pb': out['v_w_pb'], 'v_w_out': out['v_w_out'], 'v_pre_norm2': out['v_pre_norm2'], 'v_post_norm2': out['v_post_norm2'], 'v_w_ff1': out['v_w_ff1'], 'v_w_ff2': out['v_w_ff2']}


def _loss(weights, diff, rest, loss_target):
    with _jax.named_scope("forward"):
        args = {**rest, TWIN_DIFF_INPUT: diff, **{k: w.astype(_WEIGHT_DTYPES[k]) for k, w in weights.items()}}
        y = _forward(args)
    with _jax.named_scope("loss_head"):
        err = _jnp.square(y.astype(_jnp.float32) - loss_target)
        return 0.5 * _jnp.sum(_jnp.mean(err, axis=-1)) if err.ndim else 0.5 * err


def _adamw(w, g, m, v):
    m = ADAM_B1 * m + (1.0 - ADAM_B1) * g
    v = ADAM_B2 * v + (1.0 - ADAM_B2) * _jnp.square(g)
    m_hat = m / (1.0 - ADAM_B1 ** ADAM_STEP)
    v_hat = v / (1.0 - ADAM_B2 ** ADAM_STEP)
    delta = -ADAM_LR * (m_hat / (_jnp.sqrt(v_hat) + ADAM_EPS) + ADAM_WD * w)
    return delta, m, v


def reference(x, c, w_ada, b_ada, pre_norm1, post_norm1, w_in, b_gate, lru_conv_w, lru_conv_b, lru_wa, lru_ba, lru_wx, lru_bx, lru_lambda, w_pa, ssd_conv_w, ssd_conv_b, ssd_dt_bias, ssd_a_log, ssd_d, ssd_norm_w, w_pb, w_out, pre_norm2, post_norm2, w_ff1, w_ff2, loss_target, m_w_ada, m_b_ada, m_pre_norm1, m_post_norm1, m_w_in, m_b_gate, m_lru_conv_w, m_lru_conv_b, m_lru_wa, m_lru_ba, m_lru_wx, m_lru_bx, m_lru_lambda, m_w_pa, m_ssd_conv_w, m_ssd_conv_b, m_ssd_dt_bias, m_ssd_a_log, m_ssd_d, m_ssd_norm_w, m_w_pb, m_w_out, m_pre_norm2, m_post_norm2, m_w_ff1, m_w_ff2, v_w_ada, v_b_ada, v_pre_norm1, v_post_norm1, v_w_in, v_b_gate, v_lru_conv_w, v_lru_conv_b, v_lru_wa, v_lru_ba, v_lru_wx, v_lru_bx, v_lru_lambda, v_w_pa, v_ssd_conv_w, v_ssd_conv_b, v_ssd_dt_bias, v_ssd_a_log, v_ssd_d, v_ssd_norm_w, v_w_pb, v_w_out, v_pre_norm2, v_post_norm2, v_w_ff1, v_w_ff2):
    given = dict(x=x, c=c, w_ada=w_ada, b_ada=b_ada, pre_norm1=pre_norm1, post_norm1=post_norm1, w_in=w_in, b_gate=b_gate, lru_conv_w=lru_conv_w, lru_conv_b=lru_conv_b, lru_wa=lru_wa, lru_ba=lru_ba, lru_wx=lru_wx, lru_bx=lru_bx, lru_lambda=lru_lambda, w_pa=w_pa, ssd_conv_w=ssd_conv_w, ssd_conv_b=ssd_conv_b, ssd_dt_bias=ssd_dt_bias, ssd_a_log=ssd_a_log, ssd_d=ssd_d, ssd_norm_w=ssd_norm_w, w_pb=w_pb, w_out=w_out, pre_norm2=pre_norm2, post_norm2=post_norm2, w_ff1=w_ff1, w_ff2=w_ff2, loss_target=loss_target, m_w_ada=m_w_ada, m_b_ada=m_b_ada, m_pre_norm1=m_pre_norm1, m_post_norm1=m_post_norm1, m_w_in=m_w_in, m_b_gate=m_b_gate, m_lru_conv_w=m_lru_conv_w, m_lru_conv_b=m_lru_conv_b, m_lru_wa=m_lru_wa, m_lru_ba=m_lru_ba, m_lru_wx=m_lru_wx, m_lru_bx=m_lru_bx, m_lru_lambda=m_lru_lambda, m_w_pa=m_w_pa, m_ssd_conv_w=m_ssd_conv_w, m_ssd_conv_b=m_ssd_conv_b, m_ssd_dt_bias=m_ssd_dt_bias, m_ssd_a_log=m_ssd_a_log, m_ssd_d=m_ssd_d, m_ssd_norm_w=m_ssd_norm_w, m_w_pb=m_w_pb, m_w_out=m_w_out, m_pre_norm2=m_pre_norm2, m_post_norm2=m_post_norm2, m_w_ff1=m_w_ff1, m_w_ff2=m_w_ff2, v_w_ada=v_w_ada, v_b_ada=v_b_ada, v_pre_norm1=v_pre_norm1, v_post_norm1=v_post_norm1, v_w_in=v_w_in, v_b_gate=v_b_gate, v_lru_conv_w=v_lru_conv_w, v_lru_conv_b=v_lru_conv_b, v_lru_wa=v_lru_wa, v_lru_ba=v_lru_ba, v_lru_wx=v_lru_wx, v_lru_bx=v_lru_bx, v_lru_lambda=v_lru_lambda, v_w_pa=v_w_pa, v_ssd_conv_w=v_ssd_conv_w, v_ssd_conv_b=v_ssd_conv_b, v_ssd_dt_bias=v_ssd_dt_bias, v_ssd_a_log=v_ssd_a_log, v_ssd_d=v_ssd_d, v_ssd_norm_w=v_ssd_norm_w, v_w_pb=v_w_pb, v_w_out=v_w_out, v_pre_norm2=v_pre_norm2, v_post_norm2=v_post_norm2, v_w_ff1=v_w_ff1, v_w_ff2=v_w_ff2)
    weights = {n: given[n] for n in TWIN_WEIGHTS}
    shared = {n: given[n] for n in SHARED_INPUTS}
    per_example = {n: given[n] for n in ['x', 'c']}
    grad_fn = _jax.value_and_grad(_loss, argnums=(0, 1))

    def one_microbatch(ex, loss_target):
        ex = dict(ex)
        diff = ex.pop(TWIN_DIFF_INPUT)
        return grad_fn(weights, diff, {**shared, **ex}, loss_target)

    if N_MICROBATCH == 1:
        loss, (grad_w, grad_x) = one_microbatch(per_example, given["loss_target"])
    else:
        def body(carry, xs):
            loss_sum, grad_sum = carry
            l_k, (gw_k, gx_k) = one_microbatch(xs[0], xs[1])
            with _jax.named_scope("update"):
                return (loss_sum + l_k, _jax.tree.map(_jnp.add, grad_sum, gw_k)), gx_k

        init = (_jnp.zeros((), _jnp.float32), _jax.tree.map(_jnp.zeros_like, weights))
        (loss, grad_w), grad_x = _jax.lax.scan(body, init, (per_example, given["loss_target"]))
    with _jax.named_scope("update"):
        delta_w, new_m, new_v = {}, {}, {}
        for n in TWIN_WEIGHTS:
            delta_w[n], new_m[n], new_v[n] = _adamw(weights[n], grad_w[n], given["m_" + n], given["v_" + n])
    return (loss, grad_x, *[grad_w[n] for n in TWIN_WEIGHTS], *[delta_w[n] for n in TWIN_WEIGHTS],
            *[new_m[n] for n in TWIN_WEIGHTS], *[new_v[n] for n in TWIN_WEIGHTS])
```

```python
import functools
import math

import numpy as np
import jax
import jax.numpy as jnp
from jax import lax
from jax.experimental import pallas as pl
from jax.experimental.pallas import tpu as pltpu

F32 = jnp.float32
BF16 = jnp.bfloat16
HI = lax.Precision.HIGHEST
MESH = pl.DeviceIdType.MESH

D_MODEL = 1024
LRU_HEADS = 16
LRU_HEAD_DIM = 64
LRU_C = 8.0
SSD_INNER = 2048
SSD_HEADS = 32
SSD_HEAD_DIM = 64
SSD_GROUPS = 8
SSD_STATE = 128
SSD_CHUNK = 128
SSD_CONV_DIM = 4096
D_FF = 4096
EPS = 1e-6
N_CHIPS = 4
N_DEV = 8
LANES = 128
SUBLANES = 8

ADAM_LR = 0.001
ADAM_B1 = 0.9
ADAM_B2 = 0.999
ADAM_EPS = 1e-08
ADAM_WD = 0.01
ADAM_STEP = 10


def _pcall(body, **kw):
    return pl.pallas_call(body, **kw)


def _sigmoid(v):
    return 1.0 / (1.0 + jnp.exp(-v))


def _log1p(u):
    return jnp.where(u < 1e-3, u * (1.0 - u * (0.5 - u * (1.0 / 3.0))), jnp.log(1.0 + u))


def _softplus(v):
    return jnp.maximum(v, 0.0) + _log1p(jnp.exp(-jnp.abs(v)))


def _neg_expm1(v):
    small = -v * (1.0 + v * (0.5 + v * (1.0 / 6.0 + v * (1.0 / 24.0))))
    return jnp.where(v > -0.05, small, 1.0 - jnp.exp(v))


_GELU_K = math.sqrt(2.0 / math.pi)


def _gelu(v):
    t = jnp.tanh(_GELU_K * (v + 0.044715 * v * v * v))
    return 0.5 * v * (1.0 + t)


def _gelu_grad(v):
    t = jnp.tanh(_GELU_K * (v + 0.044715 * v * v * v))
    return 0.5 * (1.0 + t) + 0.5 * v * (1.0 - t * t) * _GELU_K * (1.0 + 3.0 * 0.044715 * v * v)


def _colsum(v):
    return jnp.sum(v, axis=0, keepdims=True)


def _dot(a, b, precision=None):
    return lax.dot_general(a, b, (((1,), (0,)), ((), ())), preferred_element_type=F32, precision=precision)


def _dot_nt(a, b):
    return lax.dot_general(a, b, (((1,), (1,)), ((), ())), preferred_element_type=F32)


def _dot_tn(a, b):
    return lax.dot_general(a, b, (((0,), (0,)), ((), ())), preferred_element_type=F32)


def _shift_down(xt, prev8, j):
    if j == 0:
        return xt
    n = xt.shape[0]
    r = pltpu.roll(xt, j, 0)
    p = pltpu.roll(prev8, j, 0)
    rows = lax.broadcasted_iota(jnp.int32, (SUBLANES, xt.shape[1]), 0)
    top = jnp.where(rows < j, p, r[0:SUBLANES])
    if n == SUBLANES:
        return top
    return jnp.concatenate([top, r[SUBLANES:]], axis=0)


def _shift_up(xt, next8, j):
    if j == 0:
        return xt
    n = xt.shape[0]
    r = pltpu.roll(xt, n - j, 0)
    p = pltpu.roll(next8, SUBLANES - j, 0)
    rows = lax.broadcasted_iota(jnp.int32, (SUBLANES, xt.shape[1]), 0)
    bot = jnp.where(rows >= SUBLANES - j, p, r[n - SUBLANES:])
    if n == SUBLANES:
        return bot
    return jnp.concatenate([r[:n - SUBLANES], bot], axis=0)


def _conv4(xt, prev8, w, b):
    out = b + w[3:4] * xt
    for k in range(3):
        out = out + w[k:k + 1] * _shift_down(xt, prev8, 3 - k)
    return out


def _conv4_bwd(d_out, next8, xt, prev8, w):
    d_x = w[3:4] * d_out
    for k in range(3):
        d_x = d_x + w[k:k + 1] * _shift_up(d_out, next8, 3 - k)
    d_w = [_colsum(d_out * _shift_down(xt, prev8, 3 - k)) for k in range(3)] + [_colsum(d_out * xt)]
    return d_x, d_w, _colsum(d_out)


def _stack_rows(rows, width):
    rows = list(rows) + [jnp.zeros((1, width), F32)] * (SUBLANES - len(rows))
    return jnp.concatenate(rows, axis=0)


def _pick(n, cands):
    for c in cands:
        if n % c == 0:
            return c
    raise ValueError(f"no tile for {n}")


def mm_nn(a, b, *, name, col0=0, n=None, out_dtype=F32, a_fn=None, tm=512):
    m, kdim = a.shape
    n = b.shape[1] if n is None else n
    tm = _pick(m, (tm, 256, 128, 64, 32, 16, 8))
    tn = _pick(n, (512, 256, 128))
    tk = _pick(kdim, (1024, 512, 256, 128))
    assert col0 % tn == 0
    nk = kdim // tk
    c0 = col0 // tn

    def body(a_ref, b_ref, o_ref, acc):
        k = pl.program_id(2)

        @pl.when(k == 0)
        def _():
            acc[...] = jnp.zeros_like(acc)

        av = a_ref[...]
        if a_fn is not None:
            av = a_fn(av)
        acc[...] += _dot(av.astype(BF16), b_ref[...].astype(BF16))

        @pl.when(k == nk - 1)
        def _():
            o_ref[...] = acc[...].astype(out_dtype)

    return _pcall(
        body, name=name,
        grid=(m // tm, n // tn, nk),
        in_specs=[pl.BlockSpec((tm, tk), lambda i, j, k: (i, k)),
                  pl.BlockSpec((tk, tn), lambda i, j, k: (k, c0 + j))],
        out_specs=pl.BlockSpec((tm, tn), lambda i, j, k: (i, j)),
        out_shape=jax.ShapeDtypeStruct((m, n), out_dtype),
        scratch_shapes=[pltpu.VMEM((tm, tn), F32)],
    )(a, b)


def mm_nt(a, b, *, name, col0=0, out_dtype=F32, add=None, epi=None, extra=None, tm=512):
    m, n = a.shape
    kdim = b.shape[0]
    tm = _pick(m, (tm, 256, 128, 64, 32, 16, 8))
    tko = _pick(kdim, (512, 256, 128))
    tc = _pick(n, (1024, 512, 256, 128))
    assert col0 % tc == 0
    nc = n // tc
    c0 = col0 // tc
    has_add = add is not None
    has_extra = extra is not None

    def body(*refs):
        a_ref, b_ref = refs[0], refs[1]
        pos = 2
        add_ref = extra_ref = None
        if has_add:
            add_ref = refs[pos]
            pos += 1
        if has_extra:
            extra_ref = refs[pos]
            pos += 1
        o_ref, acc = refs[pos], refs[pos + 1]
        k = pl.program_id(2)

        @pl.when(k == 0)
        def _():
            acc[...] = jnp.zeros_like(acc)

        acc[...] += _dot_nt(a_ref[...].astype(BF16), b_ref[...].astype(BF16))

        @pl.when(k == nc - 1)
        def _():
            r = acc[...]
            if has_add:
                r = r + add_ref[...]
            if epi is not None:
                r = epi(r, extra_ref[...]) if has_extra else epi(r)
            o_ref[...] = r.astype(out_dtype)

    in_specs = [pl.BlockSpec((tm, tc), lambda i, j, k: (i, k)),
                pl.BlockSpec((tko, tc), lambda i, j, k: (j, c0 + k))]
    args = [a, b]
    if has_add:
        in_specs.append(pl.BlockSpec((tm, tko), lambda i, j, k: (i, j)))
        args.append(add)
    if has_extra:
        in_specs.append(pl.BlockSpec((tm, tko), lambda i, j, k: (i, j)))
        args.append(extra)
    return _pcall(
        body, name=name,
        grid=(m // tm, kdim // tko, nc),
        in_specs=in_specs,
        out_specs=pl.BlockSpec((tm, tko), lambda i, j, k: (i, j)),
        out_shape=jax.ShapeDtypeStruct((m, kdim), out_dtype),
        scratch_shapes=[pltpu.VMEM((tm, tko), F32)],
    )(*args)


def mm_tn(a, b, *, name, a_fn=None):
    m, ka = a.shape
    nb = b.shape[1]
    ta = _pick(ka, (512, 256, 128))
    tb = _pick(nb, (512, 256, 128))
    tmk = _pick(m, (1024, 512, 256, 128, 64, 32, 16))
    nk = m // tmk

    def body(a_ref, b_ref, o_ref):
        k = pl.program_id(2)

        @pl.when(k == 0)
        def _():
            o_ref[...] = jnp.zeros_like(o_ref)

        av = a_ref[...]
        if a_fn is not None:
            av = a_fn(av)
        o_ref[...] += _dot_tn(av.astype(BF16), b_ref[...].astype(BF16))

    return _pcall(
        body, name=name,
        grid=(ka // ta, nb // tb, nk),
        in_specs=[pl.BlockSpec((tmk, ta), lambda i, j, k: (k, i)),
                  pl.BlockSpec((tmk, tb), lambda i, j, k: (k, j))],
        out_specs=pl.BlockSpec((ta, tb), lambda i, j, k: (i, j)),
        out_shape=jax.ShapeDtypeStruct((ka, nb), F32),
    )(a, b)


def _relu_sq(v):
    r = jnp.maximum(v, 0.0)
    return r * r


ROW_TILE = 512


def _row_specs(bsz, seq, width, ts):
    return pl.BlockSpec((None, ts, width), lambda b, i: (b, i, 0))


def _vec_spec(width):
    return pl.BlockSpec((1, width), lambda b, i: (0, 0))


def _mod_spec():
    return pl.BlockSpec((None, SUBLANES, D_MODEL), lambda b, i: (b, 0, 0))


def _rstd(v):
    return lax.rsqrt(jnp.mean(v * v, axis=-1, keepdims=True) + EPS)


def prenorm(x, w, mod, *, name):
    bsz, seq, d = x.shape
    ts = _pick(seq, (ROW_TILE, 256, 128))

    def body(x_ref, w_ref, mod_ref, h_ref):
        xv = x_ref[...]
        m = mod_ref[...]
        xh = xv * _rstd(xv)
        h_ref[...] = ((xh * w_ref[...]) * (1.0 + m[1:2]) + m[0:1]).astype(BF16)

    return _pcall(
        body, name=name, grid=(bsz, seq // ts),
        in_specs=[_row_specs(bsz, seq, d, ts), _vec_spec(d), _mod_spec()],
        out_specs=_row_specs(bsz, seq, d, ts),
        out_shape=jax.ShapeDtypeStruct((bsz, seq, d), BF16),
    )(x, w, mod)


def post1_pre2(x, out1, mod, post1, pre2, *, name):
    bsz, seq, d = x.shape
    ts = _pick(seq, (ROW_TILE, 256, 128))

    def body(x_ref, o_ref, mod_ref, p1_ref, p2_ref, x1_ref, h2_ref):
        m = mod_ref[...]
        ov = o_ref[...]
        x1 = x_ref[...] + m[2:3] * ((ov * _rstd(ov)) * p1_ref[...])
        x1_ref[...] = x1
        xh = x1 * _rstd(x1)
        h2_ref[...] = ((xh * p2_ref[...]) * (1.0 + m[4:5]) + m[3:4]).astype(BF16)

    return _pcall(
        body, name=name, grid=(bsz, seq // ts),
        in_specs=[_row_specs(bsz, seq, d, ts), _row_specs(bsz, seq, d, ts), _mod_spec(), _vec_spec(d), _vec_spec(d)],
        out_specs=[_row_specs(bsz, seq, d, ts), _row_specs(bsz, seq, d, ts)],
        out_shape=[jax.ShapeDtypeStruct((bsz, seq, d), F32), jax.ShapeDtypeStruct((bsz, seq, d), BF16)],
    )(x, out1, mod, post1, pre2)


def _acc_specs(d):
    per_batch = pl.BlockSpec((None, SUBLANES, d), lambda b, i: (b, 0, 0))
    glob = pl.BlockSpec((SUBLANES, d), lambda b, i: (0, 0))
    return per_batch, glob


def _accumulate(pb_ref, gl_ref, pb_rows, gl_rows, width):
    b, i = pl.program_id(0), pl.program_id(1)

    @pl.when(i == 0)
    def _():
        pb_ref[...] = jnp.zeros_like(pb_ref)

    @pl.when((b == 0) & (i == 0))
    def _():
        gl_ref[...] = jnp.zeros_like(gl_ref)

    pb_ref[...] += _stack_rows(pb_rows, width)
    gl_ref[...] += _stack_rows(gl_rows, width)


def _rms_bwd(d_n, n, r):
    return r * (d_n - n * jnp.mean(d_n * n, axis=-1, keepdims=True))


def final_bwd(x1, y2, target, mod, post2, *, name):
    bsz, seq, d = x1.shape
    ts = _pick(seq, (ROW_TILE, 256, 128))

    def body(x1_ref, y_ref, t_ref, mod_ref, p_ref, dx_ref, dy_ref, pb_ref, gl_ref):
        m = mod_ref[...]
        g2 = m[5:6]
        yv = y_ref[...]
        r = _rstd(yv)
        n = yv * r
        o = n * p_ref[...]
        diff = (x1_ref[...] + g2 * o) - t_ref[...]
        dx = diff * (1.0 / d)
        dx_ref[...] = dx
        d_o = dx * g2
        dy_ref[...] = _rms_bwd(d_o * p_ref[...], n, r).astype(BF16)
        _accumulate(pb_ref, gl_ref, [_colsum(dx * o)], [_colsum(d_o * n), _colsum(diff * diff) * (0.5 / d)], d)

    pb, gl = _acc_specs(d)
    rs = _row_specs(bsz, seq, d, ts)
    return _pcall(
        body, name=name, grid=(bsz, seq // ts),
        in_specs=[rs, rs, rs, _mod_spec(), _vec_spec(d)],
        out_specs=[rs, rs, pb, gl],
        out_shape=[jax.ShapeDtypeStruct((bsz, seq, d), F32), jax.ShapeDtypeStruct((bsz, seq, d), BF16),
                   jax.ShapeDtypeStruct((bsz, SUBLANES, d), F32), jax.ShapeDtypeStruct((SUBLANES, d), F32)],
    )(x1, y2, target, mod, post2)


def mid_bwd(d_h2, dx2, x1, out1, mod, pre2, post1, *, name):
    bsz, seq, d = x1.shape
    ts = _pick(seq, (ROW_TILE, 256, 128))

    def body(dh_ref, dx2_ref, x1_ref, o_ref, mod_ref, p2_ref, p1_ref, dx1_ref, do_ref, pb_ref, gl_ref):
        m = mod_ref[...]
        dh = dh_ref[...]
        x1 = x1_ref[...]
        r2 = _rstd(x1)
        xh = x1 * r2
        xw = xh * p2_ref[...]
        d_xw = dh * (1.0 + m[4:5])
        dx1 = dx2_ref[...] + _rms_bwd(d_xw * p2_ref[...], xh, r2)
        dx1_ref[...] = dx1
        ov = o_ref[...]
        r1 = _rstd(ov)
        n1 = ov * r1
        o1 = n1 * p1_ref[...]
        d_o1 = dx1 * m[2:3]
        do_ref[...] = _rms_bwd(d_o1 * p1_ref[...], n1, r1).astype(BF16)
        _accumulate(pb_ref, gl_ref, [_colsum(dh), _colsum(dh * xw), _colsum(dx1 * o1)],
                    [_colsum(d_xw * xh), _colsum(d_o1 * n1)], d)

    pb, gl = _acc_specs(d)
    rs = _row_specs(bsz, seq, d, ts)
    return _pcall(
        body, name=name, grid=(bsz, seq // ts),
        in_specs=[rs, rs, rs, rs, _mod_spec(), _vec_spec(d), _vec_spec(d)],
        out_specs=[rs, rs, pb, gl],
        out_shape=[jax.ShapeDtypeStruct((bsz, seq, d), F32), jax.ShapeDtypeStruct((bsz, seq, d), BF16),
                   jax.ShapeDtypeStruct((bsz, SUBLANES, d), F32), jax.ShapeDtypeStruct((SUBLANES, d), F32)],
    )(d_h2, dx2, x1, out1, mod, pre2, post1)


def first_bwd(d_h1, dx1, x, mod, pre1, *, name):
    bsz, seq, d = x.shape
    ts = _pick(seq, (ROW_TILE, 256, 128))

    def body(dh_ref, dx1_ref, x_ref, mod_ref, p_ref, gx_ref, pb_ref, gl_ref):
        m = mod_ref[...]
        dh = dh_ref[...]
        xv = x_ref[...]
        r = _rstd(xv)
        xh = xv * r
        xw = xh * p_ref[...]
        d_xw = dh * (1.0 + m[1:2])
        gx_ref[...] = dx1_ref[...] + _rms_bwd(d_xw * p_ref[...], xh, r)
        _accumulate(pb_ref, gl_ref, [_colsum(dh), _colsum(dh * xw)], [_colsum(d_xw * xh)], d)

    pb, gl = _acc_specs(d)
    rs = _row_specs(bsz, seq, d, ts)
    return _pcall(
        body, name=name, grid=(bsz, seq // ts),
        in_specs=[rs, rs, rs, _mod_spec(), _vec_spec(d)],
        out_specs=[rs, pb, gl],
        out_shape=[jax.ShapeDtypeStruct((bsz, seq, d), F32),
                   jax.ShapeDtypeStruct((bsz, SUBLANES, d), F32), jax.ShapeDtypeStruct((SUBLANES, d), F32)],
    )(d_h1, dx1, x, mod, pre1)


def merge_fwd(ya, yb, gates, b_gate, *, name):
    bsz, seq, d = ya.shape
    ts = _pick(seq, (ROW_TILE, 256, 128))

    def body(ya_ref, yb_ref, g_ref, b_ref, o_ref):
        g = _sigmoid(g_ref[...] + b_ref[...])
        o_ref[...] = (g[:, :d] * ya_ref[...] + g[:, d:] * yb_ref[...]).astype(BF16)

    rs = _row_specs(bsz, seq, d, ts)
    return _pcall(
        body, name=name, grid=(bsz, seq // ts),
        in_specs=[rs, rs, _row_specs(bsz, seq, 2 * d, ts), _vec_spec(2 * d)],
        out_specs=rs,
        out_shape=jax.ShapeDtypeStruct((bsz, seq, d), BF16),
    )(ya, yb, gates, b_gate)


def merge_bwd(d_merged, ya, yb, gates, b_gate, *, name):
    bsz, seq, d = ya.shape
    ts = _pick(seq, (ROW_TILE, 256, 128))

    def body(dm_ref, ya_ref, yb_ref, g_ref, b_ref, dya_ref, dyb_ref, dg_ref, gl_ref):
        b, i = pl.program_id(0), pl.program_id(1)
        g = _sigmoid(g_ref[...] + b_ref[...])
        dm = dm_ref[...]
        ga, gb = g[:, :d], g[:, d:]
        dya_ref[...] = (dm * ga).astype(BF16)
        dyb_ref[...] = (dm * gb).astype(BF16)
        dg = jnp.concatenate([dm * ya_ref[...] * ga * (1.0 - ga), dm * yb_ref[...] * gb * (1.0 - gb)], axis=1)
        dg_ref[...] = dg.astype(BF16)

        @pl.when((b == 0) & (i == 0))
        def _():
            gl_ref[...] = jnp.zeros_like(gl_ref)

        gl_ref[...] += _stack_rows([_colsum(dg)], 2 * d)

    rs = _row_specs(bsz, seq, d, ts)
    rs2 = _row_specs(bsz, seq, 2 * d, ts)
    return _pcall(
        body, name=name, grid=(bsz, seq // ts),
        in_specs=[rs, rs, rs, rs2, _vec_spec(2 * d)],
        out_specs=[rs, rs, rs2, pl.BlockSpec((SUBLANES, 2 * d), lambda b, i: (0, 0))],
        out_shape=[jax.ShapeDtypeStruct((bsz, seq, d), BF16), jax.ShapeDtypeStruct((bsz, seq, d), BF16),
                   jax.ShapeDtypeStruct((bsz, seq, 2 * d), BF16), jax.ShapeDtypeStruct((SUBLANES, 2 * d), F32)],
    )(d_merged, ya, yb, gates, b_gate)


LRU_TILE = 256
N_LRU_BLOCKS = D_MODEL // LANES


def _block_mm(v, w_ref, transpose=False):
    vb = v.astype(BF16)
    outs = []
    for j in range(N_LRU_BLOCKS):
        blk = vb[:, LANES * j:LANES * (j + 1)]
        outs.append(_dot_nt(blk, w_ref[j]) if transpose else _dot(blk, w_ref[j]))
    return jnp.concatenate(outs, axis=1)


def _lru_gates(xc, wa_ref, ba, wx_ref, bx, sp):
    r = _sigmoid(_block_mm(xc, wa_ref) + ba)
    i = _sigmoid(_block_mm(xc, wx_ref) + bx)
    la = (-LRU_C * r) * sp
    a = jnp.exp(la)
    sq = jnp.sqrt(_neg_expm1(2.0 * la))
    return r, i, a, sq


def _prev8_spec(width, col_block, tile_rows):
    per = tile_rows // SUBLANES
    return pl.BlockSpec((None, SUBLANES, width), lambda b, i: (b, jnp.maximum(i * per - 1, 0), col_block))


def lru_fwd(pm, cw, cb, wa, ba, wx, bx, lam, *, name):
    bsz, seq, _ = pm.shape
    d = D_MODEL
    ts = _pick(seq, (LRU_TILE, 128))

    def body(lx_ref, lxp_ref, lg_ref, cw_ref, cb_ref, wa_ref, ba_ref, wx_ref, bx_ref, lam_ref,
             h_ref, pa_ref, hc, a_s, u_s):
        i = pl.program_id(1)

        @pl.when(i == 0)
        def _():
            hc[...] = jnp.zeros_like(hc)

        lx = lx_ref[...]
        prev8 = jnp.where(i == 0, 0.0, lxp_ref[...])
        xc = _conv4(lx, prev8, cw_ref[...], cb_ref[...])
        sp = _softplus(-lam_ref[...])
        r, ig, a, sq = _lru_gates(xc, wa_ref, ba_ref[...], wx_ref, bx_ref[...], sp)
        a_s[...] = a
        u_s[...] = sq * (ig * xc)

        def step(g, h):
            r0 = pl.multiple_of(g * SUBLANES, SUBLANES)
            a8 = a_s[pl.ds(r0, SUBLANES), :]
            u8 = u_s[pl.ds(r0, SUBLANES), :]
            rows = []
            for j in range(SUBLANES):
                h = a8[j:j + 1] * h + u8[j:j + 1]
                rows.append(h)
            h_ref[pl.ds(r0, SUBLANES), :] = jnp.concatenate(rows, axis=0)
            return h

        hc[...] = lax.fori_loop(0, ts // SUBLANES, step, hc[...])
        pa_ref[...] = (h_ref[...] * _gelu(lg_ref[...])).astype(BF16)

    vec = _vec_spec(d)
    wspec = pl.BlockSpec((N_LRU_BLOCKS, LANES, LANES), lambda b, i: (0, 0, 0))
    return _pcall(
        body, name=name, grid=(bsz, seq // ts),
        in_specs=[pl.BlockSpec((None, ts, d), lambda b, i: (b, i, 0)), _prev8_spec(d, 0, ts),
                  pl.BlockSpec((None, ts, d), lambda b, i: (b, i, 1)),
                  pl.BlockSpec((4, d), lambda b, i: (0, 0)), vec, wspec, vec, wspec, vec, vec],
        out_specs=[_row_specs(bsz, seq, d, ts), _row_specs(bsz, seq, d, ts)],
        out_shape=[jax.ShapeDtypeStruct((bsz, seq, d), F32), jax.ShapeDtypeStruct((bsz, seq, d), BF16)],
        scratch_shapes=[pltpu.VMEM((1, d), F32), pltpu.VMEM((ts, d), F32), pltpu.VMEM((ts, d), F32)],
    )(pm, pm, pm, cw, cb, wa, ba, wx, bx, lam)


def lru_bwd(pm, h, d_pa, cw, cb, wa, ba, wx, bx, lam, *, name):
    bsz, seq, _ = pm.shape
    d = D_MODEL
    ts = _pick(seq, (LRU_TILE, 128))
    nt = seq // ts
    per = ts // SUBLANES

    def rev(i):
        return nt - 1 - i

    def body(lx_ref, lxp_ref, lg_ref, h_ref, hp_ref, dpa_ref, cw_ref, cb_ref, wa_ref, ba_ref, wx_ref, bx_ref,
             lam_ref, dl_ref, dwa_ref, dwx_ref, rows_ref, carry, dxc_next, a_s, dh_s, acc_s):
        b, i = pl.program_id(0), pl.program_id(1)
        t = rev(i)

        @pl.when(i == 0)
        def _():
            carry[...] = jnp.zeros_like(carry)
            dxc_next[...] = jnp.zeros_like(dxc_next)

        @pl.when((b == 0) & (i == 0))
        def _():
            dwa_ref[...] = jnp.zeros_like(dwa_ref)
            dwx_ref[...] = jnp.zeros_like(dwx_ref)
            rows_ref[...] = jnp.zeros_like(rows_ref)

        lx = lx_ref[...]
        lg = lg_ref[...]
        prev8 = jnp.where(t == 0, 0.0, lxp_ref[...])
        cwv = cw_ref[...]
        xc = _conv4(lx, prev8, cwv, cb_ref[...])
        lam_v = lam_ref[...]
        sp = _softplus(-lam_v)
        r, ig, a, sq = _lru_gates(xc, wa_ref, ba_ref[...], wx_ref, bx_ref[...], sp)
        hv = h_ref[...]
        d_pa = dpa_ref[...]
        a_s[...] = a
        dh_s[...] = d_pa * _gelu(lg)

        def step(g, c):
            r0 = pl.multiple_of((per - 1 - g) * SUBLANES, SUBLANES)
            a8 = a_s[pl.ds(r0, SUBLANES), :]
            d8 = dh_s[pl.ds(r0, SUBLANES), :]
            rows = [None] * SUBLANES
            for j in range(SUBLANES - 1, -1, -1):
                acc = d8[j:j + 1] + c
                rows[j] = acc
                c = a8[j:j + 1] * acc
            acc_s[pl.ds(r0, SUBLANES), :] = jnp.concatenate(rows, axis=0)
            return c

        carry[...] = lax.fori_loop(0, per, step, carry[...])
        d_u = acc_s[...]
        hprev8 = jnp.where(t == 0, 0.0, hp_ref[...])
        d_a = d_u * _shift_down(hv, hprev8, 1)
        d_sq = d_u * (ig * xc)
        d_i = d_u * (sq * xc)
        d_xc = d_u * (sq * ig)
        d_la = d_a * a - d_sq * (a * a) / sq
        d_pre_r = (d_la * (-LRU_C * sp)) * (r * (1.0 - r))
        d_pre_i = d_i * (ig * (1.0 - ig))
        d_xc = d_xc + _block_mm(d_pre_r, wa_ref, transpose=True) + _block_mm(d_pre_i, wx_ref, transpose=True)
        xcb = xc.astype(BF16)
        drb = d_pre_r.astype(BF16)
        dib = d_pre_i.astype(BF16)
        for j in range(N_LRU_BLOCKS):
            sl = slice(LANES * j, LANES * (j + 1))
            dwa_ref[j] += _dot_tn(xcb[:, sl], drb[:, sl])
            dwx_ref[j] += _dot_tn(xcb[:, sl], dib[:, sl])
        d_lx, d_cw, d_cb = _conv4_bwd(d_xc, dxc_next[...], lx, prev8, cwv)
        dxc_next[...] = d_xc[0:SUBLANES]
        d_lam = _colsum(d_la * (-LRU_C * r)) * (-_sigmoid(-lam_v))
        rows_ref[...] += _stack_rows([_colsum(d_pre_r), _colsum(d_pre_i), d_lam, d_cb] + d_cw, d)
        dl_ref[:, :d] = d_lx.astype(BF16)
        dl_ref[:, d:] = (d_pa * hv * _gelu_grad(lg)).astype(BF16)

    vec = _vec_spec(d)
    wspec = pl.BlockSpec((N_LRU_BLOCKS, LANES, LANES), lambda b, i: (0, 0, 0))
    tile = lambda col: pl.BlockSpec((None, ts, d), lambda b, i: (b, rev(i), col))
    prev8 = lambda col: pl.BlockSpec((None, SUBLANES, d), lambda b, i: (b, jnp.maximum(rev(i) * per - 1, 0), col))
    return _pcall(
        body, name=name, grid=(bsz, nt),
        in_specs=[tile(0), prev8(0), tile(1), tile(0), prev8(0), tile(0),
                  pl.BlockSpec((4, d), lambda b, i: (0, 0)), vec, wspec, vec, wspec, vec, vec],
        out_specs=[pl.BlockSpec((None, ts, 2 * d), lambda b, i: (b, rev(i), 0)), wspec, wspec,
                   pl.BlockSpec((SUBLANES, d), lambda b, i: (0, 0))],
        out_shape=[jax.ShapeDtypeStruct((bsz, seq, 2 * d), BF16),
                   jax.ShapeDtypeStruct((N_LRU_BLOCKS, LANES, LANES), F32),
                   jax.ShapeDtypeStruct((N_LRU_BLOCKS, LANES, LANES), F32),
                   jax.ShapeDtypeStruct((SUBLANES, d), F32)],
        scratch_shapes=[pltpu.VMEM((1, d), F32), pltpu.VMEM((SUBLANES, d), F32),
                        pltpu.VMEM((ts, d), F32), pltpu.VMEM((ts, d), F32), pltpu.VMEM((ts, d), F32)],
    )(pm, pm, pm, h, h, d_pa, cw, cb, wa, ba, wx, bx, lam)


L = SSD_CHUNK
N_PAIRS = SSD_HEADS // 2


def _ssd_common(xbc, prev8, dt_raw, cw, cb, dtb, alog, selt):
    conv = _conv4(xbc, prev8, cw, cb)
    sg = _sigmoid(conv)
    xa = conv * sg
    dtv = _softplus(dt_raw + dtb)
    a_neg = -jnp.exp(alog)
    rowi = lax.broadcasted_iota(jnp.int32, (L, L), 0)
    coli = lax.broadcasted_iota(jnp.int32, (L, L), 1)
    tril = (rowi >= coli).astype(F32)
    cs = _dot(tril, dtv * a_neg, precision=HI)
    cs_l = _dot(cs, selt, precision=HI)
    dt_l = _dot(dtv, selt, precision=HI)
    return conv, sg, xa, dtv, a_neg, cs, cs_l, dt_l, rowi, coli


def _head_masks():
    lane = lax.broadcasted_iota(jnp.int32, (L, LANES), 1)
    return lane < SSD_HEAD_DIM


def _stack_heads(v, first):
    return jnp.concatenate([jnp.where(first, v, 0.0), jnp.where(first, 0.0, v)], axis=0).astype(BF16)


def ssd_fwd(pm, dtr, cw, cb, dtb, alog, d_lanes, nw, selt, *, name):
    bsz, seq, _ = pm.shape
    nc = seq // L
    inner, cdim = SSD_INNER, SSD_CONV_DIM

    def body(xbc_ref, xp_ref, z_ref, dt_ref, cw_ref, cb_ref, dtb_ref, alog_ref, dl_ref, nw_ref, selt_ref,
             y_ref, yn_ref, st_ref, state):
        i = pl.program_id(1)

        @pl.when(i == 0)
        def _():
            state[...] = jnp.zeros_like(state)

        prev8 = jnp.where(i == 0, 0.0, xp_ref[...])
        _, _, xa, _, _, cs, cs_l, dt_l, rowi, coli = _ssd_common(
            xbc_ref[...], prev8, dt_ref[...], cw_ref[...], cb_ref[...], dtb_ref[...], alog_ref[...], selt_ref[...])
        xs = xa[:, :inner]
        xt = xs * dt_l
        e_l = jnp.exp(cs_l)
        cs_last = cs_l[L - 1:L, :]
        w_l = jnp.exp(cs_last - cs_l)
        e_last = jnp.exp(cs_last)
        cst = cs.T
        causal = rowi >= coli
        first = _head_masks()
        for g in range(SSD_GROUPS):
            bg = xa[:, inner + SSD_STATE * g:inner + SSD_STATE * (g + 1)].astype(BF16)
            cg = xa[:, inner + SSD_GROUPS * SSD_STATE + SSD_STATE * g:
                    inner + SSD_GROUPS * SSD_STATE + SSD_STATE * (g + 1)].astype(BF16)
            cbm = _dot_nt(cg, bg)
            for pp in range(2):
                p = 2 * g + pp
                sl = slice(LANES * p, LANES * (p + 1))
                ms = []
                for hh in (2 * p, 2 * p + 1):
                    seg = cs[:, hh:hh + 1] - cst[hh:hh + 1, :]
                    ms.append((cbm * jnp.exp(jnp.where(causal, seg, -jnp.inf))).astype(BF16))
                xp = xt[:, sl]
                y_diag = _dot(jnp.concatenate(ms, axis=1), _stack_heads(xp, first))
                st = state[p]
                st_ref[p] = st
                y_off = _dot(cg, st.astype(BF16)) * e_l[:, sl]
                y_ref[:, sl] = y_diag + y_off + dl_ref[:, sl] * xs[:, sl]
                state[p] = st * e_last[:, sl] + _dot_tn(bg, (xp * w_l[:, sl]).astype(BF16))
        zv = z_ref[...]
        yz = y_ref[...] * (zv * _sigmoid(zv))
        gw = inner // SSD_GROUPS
        for g in range(SSD_GROUPS):
            sl = slice(gw * g, gw * (g + 1))
            seg = yz[:, sl]
            yn_ref[:, sl] = ((seg * _rstd(seg)) * nw_ref[:, sl]).astype(BF16)

    cvec = lambda w: pl.BlockSpec((1, w), lambda b, i: (0, 0))
    return _pcall(
        body, name=name, grid=(bsz, nc),
        in_specs=[pl.BlockSpec((None, L, cdim), lambda b, i: (b, i, 1)), _prev8_spec(cdim, 1, L),
                  pl.BlockSpec((None, L, inner), lambda b, i: (b, i, 1)),
                  pl.BlockSpec((None, L, LANES), lambda b, i: (b, i, 0)),
                  pl.BlockSpec((4, cdim), lambda b, i: (0, 0)), cvec(cdim), cvec(LANES), cvec(LANES),
                  cvec(inner), cvec(inner), pl.BlockSpec((LANES, inner), lambda b, i: (0, 0))],
        out_specs=[pl.BlockSpec((None, L, inner), lambda b, i: (b, i, 0)),
                   pl.BlockSpec((None, L, inner), lambda b, i: (b, i, 0)),
                   pl.BlockSpec((None, None, N_PAIRS, SSD_STATE, LANES), lambda b, i: (b, i, 0, 0, 0))],
        out_shape=[jax.ShapeDtypeStruct((bsz, seq, inner), F32), jax.ShapeDtypeStruct((bsz, seq, inner), BF16),
                   jax.ShapeDtypeStruct((bsz, nc, N_PAIRS, SSD_STATE, LANES), F32)],
        scratch_shapes=[pltpu.VMEM((N_PAIRS, SSD_STATE, LANES), F32)],
    )(pm, pm, pm, dtr, cw, cb, dtb, alog, d_lanes, nw, selt)


def ssd_bwd(pm, dtr, y, states, d_yn, cw, cb, dtb, alog, d_lanes, nw, selt, sel, *, name):
    bsz, seq, _ = pm.shape
    nc = seq // L
    inner, cdim = SSD_INNER, SSD_CONV_DIM
    per = L // SUBLANES

    def rev(i):
        return nc - 1 - i

    def body(xbc_ref, xp_ref, z_ref, dt_ref, y_ref, st_ref, dyn_ref, cw_ref, cb_ref, dtb_ref, alog_ref,
             dl_ref, nw_ref, selt_ref, sel_ref, ds_ref, ddt_ref, r4_ref, r2_ref, r1_ref,
             dstate, dconv_next, dx_s, dbc_s, o_s, v_s, c0_s):
        b, i = pl.program_id(0), pl.program_id(1)
        t = rev(i)

        @pl.when(i == 0)
        def _():
            dstate[...] = jnp.zeros_like(dstate)
            dconv_next[...] = jnp.zeros_like(dconv_next)

        @pl.when((b == 0) & (i == 0))
        def _():
            r4_ref[...] = jnp.zeros_like(r4_ref)
            r2_ref[...] = jnp.zeros_like(r2_ref)
            r1_ref[...] = jnp.zeros_like(r1_ref)

        xbc = xbc_ref[...]
        prev8 = jnp.where(t == 0, 0.0, xp_ref[...])
        cwv = cw_ref[...]
        dt_in = dt_ref[...] + dtb_ref[...]
        conv, sg, xa, dtv, a_neg, cs, cs_l, dt_l, rowi, coli = _ssd_common(
            xbc, prev8, dt_ref[...], cwv, cb_ref[...], dtb_ref[...], alog_ref[...], selt_ref[...])
        xs = xa[:, :inner]
        xt = xs * dt_l
        e_l = jnp.exp(cs_l)
        cs_last = cs_l[L - 1:L, :]
        w_l = jnp.exp(cs_last - cs_l)
        e_last = jnp.exp(cs_last)
        cst = cs.T
        causal = rowi >= coli
        anti = coli >= rowi
        first = _head_masks()
        lane1 = lax.broadcasted_iota(jnp.int32, (L, LANES), 1)

        yv = y_ref[...]
        zv = z_ref[...]
        sz = _sigmoid(zv)
        zs = zv * sz
        yz = yv * zs
        dyn = dyn_ref[...]
        gw = inner // SSD_GROUPS
        d_yz_parts, d_nw_parts = [], []
        for g in range(SSD_GROUPS):
            sl = slice(gw * g, gw * (g + 1))
            seg = yz[:, sl]
            r = _rstd(seg)
            n = seg * r
            d_nw_parts.append(_colsum(dyn[:, sl] * n))
            d_yz_parts.append(_rms_bwd(dyn[:, sl] * nw_ref[:, sl], n, r))
        d_yz = jnp.concatenate(d_yz_parts, axis=1)
        d_y = d_yz * zs
        ds_ref[:, :inner] = (d_yz * yv * (sz * (1.0 + zv * (1.0 - sz)))).astype(BF16)
        dlv = dl_ref[...]
        d_dl = _colsum(d_y * xs)

        d_cs_q = jnp.zeros((L, LANES), F32)
        for g in range(SSD_GROUPS):
            bsl = slice(inner + SSD_STATE * g, inner + SSD_STATE * (g + 1))
            csl = slice(inner + SSD_GROUPS * SSD_STATE + SSD_STATE * g,
                        inner + SSD_GROUPS * SSD_STATE + SSD_STATE * (g + 1))
            bg = xa[:, bsl].astype(BF16)
            cg = xa[:, csl].astype(BF16)
            cbm = _dot_nt(cg, bg)
            cbt = _dot_nt(bg, cg)
            d_cb = jnp.zeros((L, L), F32)
            d_bg = jnp.zeros((L, SSD_STATE), F32)
            d_cg = jnp.zeros((L, SSD_STATE), F32)
            for pp in range(2):
                p = 2 * g + pp
                sl = slice(LANES * p, LANES * (p + 1))
                xp = xt[:, sl]
                xpb = xp.astype(BF16)
                dyp = d_y[:, sl]
                dypb = dyp.astype(BF16)
                dy_heads = (jnp.where(first, dyp, 0.0).astype(BF16), jnp.where(first, 0.0, dyp).astype(BF16))
                x_heads = (jnp.where(first, xp, 0.0).astype(BF16), jnp.where(first, 0.0, xp).astype(BF16))
                mts = []
                for k, hh in enumerate((2 * p, 2 * p + 1)):
                    col = cs[:, hh:hh + 1]
                    row = cst[hh:hh + 1, :]
                    dec = jnp.exp(jnp.where(causal, col - row, -jnp.inf))
                    dec_t = jnp.exp(jnp.where(anti, row - col, -jnp.inf))
                    gd = _dot_nt(dy_heads[k], xpb) * dec
                    d_cb = d_cb + gd
                    mt = cbt * dec_t
                    qd = gd * cbm - _dot_nt(x_heads[k], dypb) * mt
                    d_cs_q = jnp.where(lane1 == hh, jnp.sum(qd, axis=1, keepdims=True), d_cs_q)
                    mts.append(mt.astype(BF16))
                dst = dstate[p]
                dstb = dst.astype(BF16)
                st = st_ref[p]
                stb = st.astype(BF16)
                dye = (dyp * e_l[:, sl]).astype(BF16)
                xw = (xp * w_l[:, sl]).astype(BF16)
                dx_off = w_l[:, sl] * _dot(bg, dstb)
                dx_s[:, sl] = _dot(jnp.concatenate(mts, axis=1), jnp.concatenate(dy_heads, axis=0)) + dx_off
                o_s[:, sl] = dyp * (_dot(cg, stb) * e_l[:, sl])
                v_s[:, sl] = xp * dx_off
                c0_s[:, sl] = jnp.broadcast_to(_colsum(dst * st) * e_last[:, sl], (SUBLANES, LANES))
                d_cg = d_cg + _dot_nt(dye, stb)
                d_bg = d_bg + _dot_nt(xw, dstb)
                dstate[p] = dst * e_last[:, sl] + _dot_tn(cg, dye)
            d_cbb = d_cb.astype(BF16)
            dbc_s[:, SSD_STATE * g:SSD_STATE * (g + 1)] = d_bg + _dot_tn(d_cbb, cg)
            dbc_s[:, SSD_GROUPS * SSD_STATE + SSD_STATE * g:SSD_GROUPS * SSD_STATE + SSD_STATE * (g + 1)] = (
                d_cg + _dot(d_cbb, bg))

        d_xt = dx_s[...]
        selv = sel_ref[...]
        a1 = d_cs_q + _dot(o_s[...], selv, precision=HI)
        a2 = _dot(v_s[...], selv, precision=HI)
        c0 = _dot(c0_s[...], selv, precision=HI)[0:1]
        d_da = (_dot(anti.astype(F32), a1, precision=HI) + _dot((rowi > coli).astype(F32), a2, precision=HI) + c0)
        d_dt = d_da * a_neg + _dot(d_xt * xs, selv, precision=HI)
        d_alog = _colsum(d_da * dtv) * a_neg
        d_dtr = jnp.where(lane1 < SSD_HEADS, d_dt * _sigmoid(dt_in), 0.0)
        ddt_ref[...] = d_dtr.astype(BF16)
        d_xs = d_xt * dt_l + d_y * dlv
        d_xa = jnp.concatenate([d_xs, dbc_s[...]], axis=1)
        d_conv = d_xa * (sg * (1.0 + conv * (1.0 - sg)))
        d_xbc, d_cw, d_cbias = _conv4_bwd(d_conv, dconv_next[...], xbc, prev8, cwv)
        dconv_next[...] = d_conv[0:SUBLANES]
        ds_ref[:, inner:] = d_xbc.astype(BF16)
        r4_ref[...] += _stack_rows([d_cbias] + d_cw, cdim)
        r2_ref[...] += _stack_rows([jnp.concatenate(d_nw_parts, axis=1)], inner)
        r1_ref[...] += _stack_rows([_colsum(d_dtr), d_alog, _dot(jnp.broadcast_to(d_dl, (SUBLANES, inner)), selv,
                                                                   precision=HI)[0:1]], LANES)

    cvec = lambda w: pl.BlockSpec((1, w), lambda b, i: (0, 0))
    return _pcall(
        body, name=name, grid=(bsz, nc),
        in_specs=[pl.BlockSpec((None, L, cdim), lambda b, i: (b, rev(i), 1)),
                  pl.BlockSpec((None, SUBLANES, cdim), lambda b, i: (b, jnp.maximum(rev(i) * per - 1, 0), 1)),
                  pl.BlockSpec((None, L, inner), lambda b, i: (b, rev(i), 1)),
                  pl.BlockSpec((None, L, LANES), lambda b, i: (b, rev(i), 0)),
                  pl.BlockSpec((None, L, inner), lambda b, i: (b, rev(i), 0)),
                  pl.BlockSpec((None, None, N_PAIRS, SSD_STATE, LANES), lambda b, i: (b, rev(i), 0, 0, 0)),
                  pl.BlockSpec((None, L, inner), lambda b, i: (b, rev(i), 0)),
                  pl.BlockSpec((4, cdim), lambda b, i: (0, 0)), cvec(cdim), cvec(LANES), cvec(LANES),
                  cvec(inner), cvec(inner), pl.BlockSpec((LANES, inner), lambda b, i: (0, 0)),
                  pl.BlockSpec((inner, LANES), lambda b, i: (0, 0))],
        out_specs=[pl.BlockSpec((None, L, inner + cdim), lambda b, i: (b, rev(i), 0)),
                   pl.BlockSpec((None, L, LANES), lambda b, i: (b, rev(i), 0)),
                   pl.BlockSpec((SUBLANES, cdim), lambda b, i: (0, 0)),
                   pl.BlockSpec((SUBLANES, inner), lambda b, i: (0, 0)),
                   pl.BlockSpec((SUBLANES, LANES), lambda b, i: (0, 0))],
        out_shape=[jax.ShapeDtypeStruct((bsz, seq, inner + cdim), BF16),
                   jax.ShapeDtypeStruct((bsz, seq, LANES), BF16),
                   jax.ShapeDtypeStruct((SUBLANES, cdim), F32),
                   jax.ShapeDtypeStruct((SUBLANES, inner), F32),
                   jax.ShapeDtypeStruct((SUBLANES, LANES), F32)],
        scratch_shapes=[pltpu.VMEM((N_PAIRS, SSD_STATE, LANES), F32), pltpu.VMEM((SUBLANES, cdim), F32),
                        pltpu.VMEM((L, inner), F32), pltpu.VMEM((L, 2 * SSD_GROUPS * SSD_STATE), F32),
                        pltpu.VMEM((L, inner), F32), pltpu.VMEM((L, inner), F32), pltpu.VMEM((SUBLANES, inner), F32)],
    )(pm, pm, pm, dtr, y, states, d_yn, cw, cb, dtb, alog, d_lanes, nw, selt, sel)


def _lru_block_weights(w):
    w = w.reshape(N_LRU_BLOCKS, 2, LRU_HEAD_DIM, LRU_HEAD_DIM)
    z = jnp.zeros((N_LRU_BLOCKS, LRU_HEAD_DIM, LRU_HEAD_DIM), w.dtype)
    top = jnp.concatenate([w[:, 0], z], axis=2)
    bot = jnp.concatenate([z, w[:, 1]], axis=2)
    return jnp.concatenate([top, bot], axis=1).astype(BF16)


def _lru_block_grads(g):
    h = LRU_HEAD_DIM
    return jnp.stack([g[:, :h, :h], g[:, h:, h:]], axis=1).reshape(LRU_HEADS, h, h)


def _pad_lanes(v, width=LANES):
    return jnp.pad(v, ((0, 0), (0, width - v.shape[1])))


def local_step(x, target, mod, big, small):
    bsz, seq, d = x.shape
    t = bsz * seq
    flat = lambda v: v.reshape(t, v.shape[-1])
    unflat = lambda v: v.reshape(bsz, seq, v.shape[-1])

    sel = (jnp.arange(SSD_INNER)[:, None] // SSD_HEAD_DIM == jnp.arange(LANES)[None, :]).astype(F32)
    selt = sel.T
    wa_b = _lru_block_weights(small["lru_wa"])
    wx_b = _lru_block_weights(small["lru_wx"])
    dtb = _pad_lanes(small["ssd_dt_bias"])
    alog = _pad_lanes(small["ssd_a_log"])
    d_lanes = jnp.repeat(small["ssd_d"], SSD_HEAD_DIM, axis=1)

    h1 = prenorm(x, small["pre_norm1"], mod, name="prenorm1")
    h1f = flat(h1)
    pm = unflat(mm_nn(h1f, big["w_main"], name="in_proj_main"))
    gates = unflat(mm_nn(h1f, big["w_gates"], name="in_proj_gates"))
    dtr = unflat(mm_nn(h1f, big["w_dt"], name="in_proj_dt"))
    lru_args = (small["lru_conv_w"], small["lru_conv_b"], wa_b, small["lru_ba"], wx_b, small["lru_bx"],
                small["lru_lambda"])
    h_lru, pa_in = lru_fwd(pm, *lru_args, name="lru_fwd")
    ssd_args = (small["ssd_conv_w"], small["ssd_conv_b"], dtb, alog, d_lanes, small["ssd_norm_w"], selt)
    y_ssd, ynorm, states = ssd_fwd(pm, dtr, *ssd_args, name="ssd_fwd")
    ya = unflat(mm_nn(flat(pa_in), big["w_pa"], name="proj_a"))
    yb = unflat(mm_nn(flat(ynorm), big["w_pb"], name="proj_b"))
    merged = merge_fwd(ya, yb, gates, small["b_gate"], name="merge_fwd")
    out1 = unflat(mm_nn(flat(merged), big["w_out"], name="proj_out"))
    x1, h2 = post1_pre2(x, out1, mod, small["post_norm1"], small["pre_norm2"], name="post1_pre2")
    f = mm_nn(flat(h2), big["w_ff1"], name="ff1")
    y2 = unflat(mm_nn(f, big["w_ff2"], a_fn=_relu_sq, name="ff2"))

    dx2, d_y2, pb_a, gl_a = final_bwd(x1, y2, target, mod, small["post_norm2"], name="final_bwd")
    d_y2f = flat(d_y2)
    d_f = mm_nt(d_y2f, big["w_ff2"], out_dtype=BF16, extra=f,
                epi=lambda r, fv: r * (2.0 * jnp.maximum(fv, 0.0)), name="ff2_dx")
    g_ff2 = mm_tn(f, d_y2f, a_fn=_relu_sq, name="ff2_dw")
    d_h2 = unflat(mm_nt(d_f, big["w_ff1"], name="ff1_dx"))
    g_ff1 = mm_tn(flat(h2), d_f, name="ff1_dw")
    dx1, d_out1, pb_b, gl_b = mid_bwd(d_h2, dx2, x1, out1, mod, small["pre_norm2"], small["post_norm1"],
                                      name="mid_bwd")
    d_out1f = flat(d_out1)
    d_merged = unflat(mm_nt(d_out1f, big["w_out"], name="out_dx"))
    g_out = mm_tn(flat(merged), d_out1f, name="out_dw")
    d_ya, d_yb, d_gates, gl_c = merge_bwd(d_merged, ya, yb, gates, small["b_gate"], name="merge_bwd")
    d_pa = unflat(mm_nt(flat(d_ya), big["w_pa"], name="pa_dx"))
    g_pa = mm_tn(flat(pa_in), flat(d_ya), name="pa_dw")
    d_yn = unflat(mm_nt(flat(d_yb), big["w_pb"], name="pb_dx"))
    g_pb = mm_tn(flat(ynorm), flat(d_yb), name="pb_dw")
    d_l, g_wa_b, g_wx_b, lru_rows = lru_bwd(pm, h_lru, d_pa, *lru_args, name="lru_bwd")
    d_s, d_dt, r4, r2, r1 = ssd_bwd(pm, dtr, y_ssd, states, d_yn, *ssd_args, sel, name="ssd_bwd")
    d_lf, d_sf, d_gf, d_dtf = flat(d_l), flat(d_s), flat(d_gates), flat(d_dt)
    d_h1 = mm_nt(d_lf, big["w_main"], col0=0, name="in_dx_lru")
    d_h1 = mm_nt(d_sf, big["w_main"], col0=2 * D_MODEL, add=d_h1, name="in_dx_ssd")
    d_h1 = mm_nt(d_gf, big["w_gates"], add=d_h1, name="in_dx_gates")
    d_h1 = mm_nt(d_dtf, big["w_dt"], add=d_h1, name="in_dx_dt")
    g_in = jnp.concatenate([
        mm_tn(h1f, d_lf, name="in_dw_lru"), mm_tn(h1f, d_sf, name="in_dw_ssd"),
        mm_tn(h1f, d_dtf, name="in_dw_dt")[:, :SSD_HEADS], mm_tn(h1f, d_gf, name="in_dw_gates")], axis=1)
    grad_x, pb_c, gl_d = first_bwd(unflat(d_h1), dx1, x, mod, small["pre_norm1"], name="first_bwd")

    d_mod = jnp.stack([pb_c[:, 0], pb_c[:, 1], pb_b[:, 2], pb_b[:, 0], pb_b[:, 1], pb_a[:, 0]], axis=1)
    loss_cols = gl_a[1:2]
    big_grads = {"w_in": g_in, "w_pa": g_pa, "w_pb": g_pb, "w_out": g_out, "w_ff1": g_ff1, "w_ff2": g_ff2}
    nh = SSD_HEADS
    small_grads = {
        "pre_norm1": gl_d[0:1], "post_norm1": gl_b[1:2], "b_gate": gl_c[0:1],
        "lru_conv_w": lru_rows[4:8], "lru_conv_b": lru_rows[3:4],
        "lru_wa": _lru_block_grads(g_wa_b), "lru_ba": lru_rows[0:1],
        "lru_wx": _lru_block_grads(g_wx_b), "lru_bx": lru_rows[1:2], "lru_lambda": lru_rows[2:3],
        "ssd_conv_w": r4[1:5], "ssd_conv_b": r4[0:1],
        "ssd_dt_bias": r1[0:1, :nh], "ssd_a_log": r1[1:2, :nh], "ssd_d": r1[2:3, :nh],
        "ssd_norm_w": r2[0:1], "pre_norm2": gl_b[0:1], "post_norm2": gl_a[0:1],
    }
    return loss_cols, grad_x, d_mod, big_grads, small_grads


ANY = pl.BlockSpec(memory_space=pl.ANY)


def _position():
    return lax.axis_index("x"), lax.axis_index("y"), lax.axis_index("c")


def _other_chips(x, y):
    return [(1 - x, y), (x, 1 - y), (1 - x, 1 - y)]


def allgather8(v, *, name):
    m_per, n = v.shape

    def body(x_ref, out_ref, send_sems, recv_sems, local_sem):
        x, y, c = _position()
        me, sibling = (x, y, c), (x, y, 1 - c)
        chips = _other_chips(x, y)

        def rows(px, py, pc):
            return out_ref.at[pl.ds((4 * px + 2 * py + pc) * m_per, m_per), :]

        def copy(k, block, to, src=None):
            return pltpu.make_async_remote_copy(
                src_ref=rows(*block) if src is None else src, dst_ref=rows(*block),
                send_sem=send_sems.at[k], recv_sem=recv_sems.at[k], device_id=to, device_id_type=MESH)

        mine = pltpu.make_async_copy(x_ref, rows(*me), local_sem)
        mine.start()
        first = [copy(0, me, sibling, src=x_ref)]
        first += [copy(1 + j, me, (*chip, c), src=x_ref) for j, chip in enumerate(chips)]
        for cp in first:
            cp.start()
        passed = [copy(4 + j, (*chip, c), sibling) for j, chip in enumerate(chips)]
        for j, chip in enumerate(chips):
            copy(1 + j, (*chip, c), me).wait_recv()
            passed[j].start()
        copy(0, sibling, me).wait_recv()
        for j, chip in enumerate(chips):
            copy(4 + j, (*chip, 1 - c), me).wait_recv()
        for cp in first + passed:
            cp.wait_send()
        mine.wait()

    return _pcall(
        body, name=name,
        out_shape=jax.ShapeDtypeStruct((N_DEV * m_per, n), v.dtype),
        in_specs=[pl.BlockSpec(memory_space=pltpu.VMEM)],
        out_specs=pl.BlockSpec(memory_space=pltpu.VMEM),
        scratch_shapes=[pltpu.SemaphoreType.DMA((7,)), pltpu.SemaphoreType.DMA((7,)), pltpu.SemaphoreType.DMA],
    )(v)


def gather_weights(shards, *, name):
    n = len(shards)
    half = [s.shape[0] // 2 for s in shards]

    def body(*refs):
        ins, outs = refs[:n], refs[n:2 * n]
        send_sems, recv_sems, local_sems = refs[2 * n:]
        x, y, c = _position()
        me_chip = 2 * x + y
        chips = _other_chips(x, y)

        def piece(w, chip, core):
            return outs[w].at[chip, pl.ds(core * half[w], half[w]), :]

        def copy(w, k, chip, core, to, src=None):
            dst = piece(w, chip, core)
            return pltpu.make_async_remote_copy(
                src_ref=dst if src is None else src, dst_ref=dst,
                send_sem=send_sems.at[6 * w + k], recv_sem=recv_sems.at[6 * w + k], device_id=to, device_id_type=MESH)

        local = [pltpu.make_async_copy(ins[w], outs[w].at[me_chip], local_sems.at[w]) for w in range(n)]
        for cp in local:
            cp.start()
        sent = []
        for w in range(n):
            for j, (px, py) in enumerate(chips):
                cp = copy(w, j, me_chip, c, (px, py, c), src=ins[w].at[pl.ds(c * half[w], half[w]), :])
                cp.start()
                sent.append(cp)
        for w in range(n):
            for j, (px, py) in enumerate(chips):
                copy(w, j, 2 * px + py, c, (px, py, c)).wait_recv()
                cp = copy(w, 3 + j, 2 * px + py, c, (x, y, 1 - c))
                cp.start()
                sent.append(cp)
        for w in range(n):
            for j, (px, py) in enumerate(chips):
                copy(w, 3 + j, 2 * px + py, 1 - c, (x, y, 1 - c)).wait_recv()
        for cp in sent:
            cp.wait_send()
        for cp in local:
            cp.wait()

    return _pcall(
        body, name=name,
        out_shape=[jax.ShapeDtypeStruct((N_CHIPS,) + s.shape, s.dtype) for s in shards],
        in_specs=[ANY] * n, out_specs=[ANY] * n,
        scratch_shapes=[pltpu.SemaphoreType.DMA((6 * n,)), pltpu.SemaphoreType.DMA((6 * n,)),
                        pltpu.SemaphoreType.DMA((n,))],
    )(*shards)


def split_to_sibling(grads, *, name):
    n = len(grads)
    half = [g.shape[1] // 2 for g in grads]

    def body(*refs):
        ins, mine, theirs = refs[:n], refs[n:2 * n], refs[2 * n:3 * n]
        send_sems, recv_sems, local_sems = refs[3 * n:]
        x, y, c = _position()
        local, sent = [], []
        for w in range(n):
            cp = pltpu.make_async_copy(ins[w].at[:, pl.ds(c * half[w], half[w]), :], mine[w], local_sems.at[w])
            cp.start()
            local.append(cp)
            cp = pltpu.make_async_remote_copy(
                src_ref=ins[w].at[:, pl.ds((1 - c) * half[w], half[w]), :], dst_ref=theirs[w],
                send_sem=send_sems.at[w], recv_sem=recv_sems.at[w], device_id=(x, y, 1 - c), device_id_type=MESH)
            cp.start()
            sent.append(cp)
        for cp in sent:
            cp.wait()
        for cp in local:
            cp.wait()

    shapes = [jax.ShapeDtypeStruct((N_CHIPS, h, g.shape[2]), g.dtype) for g, h in zip(grads, half)]
    res = _pcall(
        body, name=name, out_shape=shapes + shapes,
        in_specs=[ANY] * n, out_specs=[ANY] * (2 * n),
        scratch_shapes=[pltpu.SemaphoreType.DMA((n,)), pltpu.SemaphoreType.DMA((n,)), pltpu.SemaphoreType.DMA((n,))],
    )(*grads)
    return res[:n], res[n:]


def scatter_to_chips(parts, *, name):
    n = len(parts)

    def body(*refs):
        ins, outs = refs[:n], refs[n:2 * n]
        send_sems, recv_sems, local_sems = refs[2 * n:]
        x, y, c = _position()
        me_chip = 2 * x + y
        chips = _other_chips(x, y)
        local, sent = [], []
        for w in range(n):
            cp = pltpu.make_async_copy(ins[w].at[me_chip], outs[w].at[me_chip], local_sems.at[w])
            cp.start()
            local.append(cp)
            for j, (px, py) in enumerate(chips):
                cp = pltpu.make_async_remote_copy(
                    src_ref=ins[w].at[2 * px + py], dst_ref=outs[w].at[me_chip],
                    send_sem=send_sems.at[3 * w + j], recv_sem=recv_sems.at[3 * w + j],
                    device_id=(px, py, c), device_id_type=MESH)
                cp.start()
                sent.append(cp)
        for w in range(n):
            for j, (px, py) in enumerate(chips):
                slot = outs[w].at[2 * px + py]
                pltpu.make_async_remote_copy(
                    src_ref=slot, dst_ref=slot, send_sem=send_sems.at[3 * w + j], recv_sem=recv_sems.at[3 * w + j],
                    device_id=(px, py, c), device_id_type=MESH).wait_recv()
        for cp in sent:
            cp.wait_send()
        for cp in local:
            cp.wait()

    return _pcall(
        body, name=name, out_shape=[jax.ShapeDtypeStruct(p.shape, p.dtype) for p in parts],
        in_specs=[ANY] * n, out_specs=[ANY] * n,
        scratch_shapes=[pltpu.SemaphoreType.DMA((3 * n,)), pltpu.SemaphoreType.DMA((3 * n,)),
                        pltpu.SemaphoreType.DMA((n,))],
    )(*parts)


def join_with_sibling(halves, *, name):
    n = len(halves)

    def body(*refs):
        ins, outs = refs[:n], refs[n:2 * n]
        send_sems, recv_sems, local_sems = refs[2 * n:]
        x, y, c = _position()
        local, sent = [], []
        for w in range(n):
            h = halves[w].shape[0]
            dst = outs[w].at[pl.ds(c * h, h), :]
            cp = pltpu.make_async_copy(ins[w], dst, local_sems.at[w])
            cp.start()
            local.append(cp)
            cp = pltpu.make_async_remote_copy(
                src_ref=ins[w], dst_ref=dst, send_sem=send_sems.at[w], recv_sem=recv_sems.at[w],
                device_id=(x, y, 1 - c), device_id_type=MESH)
            cp.start()
            sent.append(cp)
        for w in range(n):
            h = halves[w].shape[0]
            theirs = outs[w].at[pl.ds((1 - c) * h, h), :]
            pltpu.make_async_remote_copy(
                src_ref=theirs, dst_ref=theirs, send_sem=send_sems.at[w], recv_sem=recv_sems.at[w],
                device_id=(x, y, 1 - c), device_id_type=MESH).wait_recv()
        for cp in sent:
            cp.wait_send()
        for cp in local:
            cp.wait()

    return _pcall(
        body, name=name,
        out_shape=[jax.ShapeDtypeStruct((2 * h.shape[0], h.shape[1]), h.dtype) for h in halves],
        in_specs=[ANY] * n, out_specs=[ANY] * n,
        scratch_shapes=[pltpu.SemaphoreType.DMA((n,)), pltpu.SemaphoreType.DMA((n,)), pltpu.SemaphoreType.DMA((n,))],
    )(*halves)


def _row_tile(rows, cols, itemsize=4, budget=2 << 20):
    for t in (1024, 512, 256, 128, 64, 32, 16, 8):
        if rows % t == 0 and t * cols * itemsize <= budget:
            return t
    return rows


def add_to_bf16(a, b, *, name):
    k, r, c = a.shape
    tr = _row_tile(r, c)

    def body(a_ref, b_ref, o_ref):
        o_ref[...] = (a_ref[...] + b_ref[...]).astype(BF16)

    spec = pl.BlockSpec((None, tr, c), lambda i, j: (i, j, 0))
    return _pcall(body, name=name, grid=(k, r // tr), in_specs=[spec, spec], out_specs=spec,
                  out_shape=jax.ShapeDtypeStruct(a.shape, BF16))(a, b)


def sum_blocks(v, *, name):
    k, r, c = v.shape
    tr = _row_tile(r, c * k)

    def body(v_ref, o_ref):
        acc = v_ref[0].astype(F32)
        for j in range(1, k):
            acc = acc + v_ref[j].astype(F32)
        o_ref[...] = acc

    return _pcall(body, name=name, grid=(r // tr,),
                  in_specs=[pl.BlockSpec((k, tr, c), lambda i: (0, i, 0))],
                  out_specs=pl.BlockSpec((tr, c), lambda i: (i, 0)),
                  out_shape=jax.ShapeDtypeStruct((r, c), F32))(v)


def adamw(w, g, m, v, *, name):
    r, c = w.shape
    tr = _row_tile(r, c, budget=1 << 20)
    m_scale = 1.0 / (1.0 - ADAM_B1 ** ADAM_STEP)
    v_scale = 1.0 / (1.0 - ADAM_B2 ** ADAM_STEP)

    def body(w_ref, g_ref, m_ref, v_ref, d_ref, nm_ref, nv_ref):
        gv = g_ref[...]
        nm = ADAM_B1 * m_ref[...] + (1.0 - ADAM_B1) * gv
        nv = ADAM_B2 * v_ref[...] + (1.0 - ADAM_B2) * (gv * gv)
        nm_ref[...] = nm
        nv_ref[...] = nv
        d_ref[...] = -ADAM_LR * ((nm * m_scale) / (jnp.sqrt(nv * v_scale) + ADAM_EPS) + ADAM_WD * w_ref[...])

    spec = pl.BlockSpec((tr, c), lambda i: (i, 0))
    return _pcall(body, name=name, grid=(r // tr,), in_specs=[spec] * 4, out_specs=[spec] * 3,
                  out_shape=[jax.ShapeDtypeStruct((r, c), F32)] * 3)(w, g, m, v)


def ada_fwd(c_all, w_shard, b_shard, *, name):
    bsz, d = c_all.shape
    ncol = w_shard.shape[1]

    def body(c_ref, w_ref, b_ref, o_ref):
        cv = c_ref[...]
        act = (cv * _sigmoid(cv)).astype(BF16)
        o_ref[...] = _dot(act, w_ref[...].astype(BF16)) + b_ref[...]

    tn = _pick(ncol, (512, 256, 128))
    return _pcall(body, name=name, grid=(ncol // tn,),
                  in_specs=[pl.BlockSpec((bsz, d), lambda j: (0, 0)), pl.BlockSpec((d, tn), lambda j: (0, j)),
                            pl.BlockSpec((1, tn), lambda j: (0, j))],
                  out_specs=pl.BlockSpec((bsz, tn), lambda j: (0, j)),
                  out_shape=jax.ShapeDtypeStruct((bsz, ncol), F32))(c_all, w_shard, b_shard)


def ada_bwd(c_all, d_mod_all, d_mod_cols, *, name):
    bsz, d = c_all.shape
    ncol = d_mod_cols.shape[1]
    nall = d_mod_all.shape[1]

    def body(c_ref, da_ref, dc_ref, gw_ref, gb_ref):
        cv = c_ref[...]
        act = (cv * _sigmoid(cv)).astype(BF16)
        gw_ref[...] = _dot_tn(act, dc_ref[...].astype(BF16))
        gb_ref[...] = _colsum(da_ref[...])

    full = lambda s: pl.BlockSpec(s, lambda: (0,) * len(s))
    return _pcall(body, name=name,
                  in_specs=[full((bsz, d)), full((bsz, nall)), full((bsz, ncol))],
                  out_specs=[full((d, ncol)), full((1, nall))],
                  out_shape=[jax.ShapeDtypeStruct((d, ncol), F32), jax.ShapeDtypeStruct((1, nall), F32)],
                  )(c_all, d_mod_all, d_mod_cols)


WEIGHT_NAMES = ['w_ada', 'b_ada', 'pre_norm1', 'post_norm1', 'w_in', 'b_gate', 'lru_conv_w', 'lru_conv_b', 'lru_wa',
                'lru_ba', 'lru_wx', 'lru_bx', 'lru_lambda', 'w_pa', 'ssd_conv_w', 'ssd_conv_b', 'ssd_dt_bias',
                'ssd_a_log', 'ssd_d', 'ssd_norm_w', 'w_pb', 'w_out', 'pre_norm2', 'post_norm2', 'w_ff1', 'w_ff2']
BIG_NAMES = ['w_in', 'w_pa', 'w_pb', 'w_out', 'w_ff1', 'w_ff2']
COLUMN_SHARDED = ('w_in', 'w_ff1')
SMALL_NAMES = [n for n in WEIGHT_NAMES if n not in BIG_NAMES + ['w_ada', 'b_ada']]
SHARDED_SMALL = ('lru_conv_w', 'ssd_conv_w')
PACK_WIDTH = 1024


def _pack(parts):
    flat = jnp.concatenate([p.reshape(-1).astype(F32) for p in parts])
    rows = -(-flat.shape[0] // (PACK_WIDTH * SUBLANES)) * SUBLANES
    return jnp.pad(flat, (0, rows * PACK_WIDTH - flat.shape[0])).reshape(rows, PACK_WIDTH)


def _unpack(packed, shapes):
    flat = packed.reshape(-1)
    out, pos = [], 0
    for s in shapes:
        size = int(np.prod(s))
        out.append(flat[pos:pos + size].reshape(s))
        pos += size
    return out


def kernel(x, c, w_ada, b_ada, pre_norm1, post_norm1, w_in, b_gate, lru_conv_w, lru_conv_b, lru_wa, lru_ba, lru_wx, lru_bx, lru_lambda, w_pa, ssd_conv_w, ssd_conv_b, ssd_dt_bias, ssd_a_log, ssd_d, ssd_norm_w, w_pb, w_out, pre_norm2, post_norm2, w_ff1, w_ff2, loss_target, m_w_ada, m_b_ada, m_pre_norm1, m_post_norm1, m_w_in, m_b_gate, m_lru_conv_w, m_lru_conv_b, m_lru_wa, m_lru_ba, m_lru_wx, m_lru_bx, m_lru_lambda, m_w_pa, m_ssd_conv_w, m_ssd_conv_b, m_ssd_dt_bias, m_ssd_a_log, m_ssd_d, m_ssd_norm_w, m_w_pb, m_w_out, m_pre_norm2, m_post_norm2, m_w_ff1, m_w_ff2, v_w_ada, v_b_ada, v_pre_norm1, v_post_norm1, v_w_in, v_b_gate, v_lru_conv_w, v_lru_conv_b, v_lru_wa, v_lru_ba, v_lru_wx, v_lru_bx, v_lru_lambda, v_w_pa, v_ssd_conv_w, v_ssd_conv_b, v_ssd_dt_bias, v_ssd_a_log, v_ssd_d, v_ssd_norm_w, v_w_pb, v_w_out, v_pre_norm2, v_post_norm2, v_w_ff1, v_w_ff2):
    given = dict(locals())
    bsz, seq, d = x.shape
    my_x, my_y, my_c = lax.axis_index("x"), lax.axis_index("y"), lax.axis_index("c")
    chip = 2 * my_x + my_y
    dev = 2 * chip + my_c
    strip = lambda a: a if a.ndim == 2 else a[0]
    w = {n: strip(given[n]) for n in WEIGHT_NAMES}
    m = {n: strip(given["m_" + n]) for n in WEIGHT_NAMES}
    v = {n: strip(given["v_" + n]) for n in WEIGHT_NAMES}

    first_shapes = [c.shape] + [w[n].shape for n in SHARDED_SMALL]
    first = allgather8(_pack([c] + [w[n] for n in SHARDED_SMALL]), name="gather_c_conv")
    first = first.reshape(N_DEV, -1, PACK_WIDTH)
    per_dev = [_unpack(first[k], first_shapes) for k in range(N_DEV)]
    c_all = jnp.concatenate([p[0] for p in per_dev], axis=0)
    conv_full = {n: jnp.concatenate([per_dev[2 * k][1 + i] for k in range(N_CHIPS)], axis=1)
                 for i, n in enumerate(SHARDED_SMALL)}

    ncol = w["w_ada"].shape[1]
    b_cols = lax.dynamic_slice(b_ada, (0, chip * ncol), (1, ncol))
    mod_cols = ada_fwd(c_all, w["w_ada"], b_cols, name="ada_fwd")
    mod_all = allgather8(mod_cols, name="gather_mod").reshape(N_CHIPS, 2, N_DEV * bsz, ncol)[:, 0]
    mod_all = jnp.transpose(mod_all, (1, 0, 2)).reshape(N_DEV * bsz, N_CHIPS * ncol)
    mod = lax.dynamic_slice(mod_all, (dev * bsz, 0), (bsz, 6 * d)).reshape(bsz, 6, d)
    mod = jnp.pad(mod, ((0, 0), (0, 2), (0, 0)))

    gathered = gather_weights([w[n].astype(BF16) for n in BIG_NAMES], name="gather_weights")
    full = {}
    for n, g in zip(BIG_NAMES, gathered):
        if n in COLUMN_SHARDED:
            full[n] = jnp.transpose(g, (1, 0, 2)).reshape(g.shape[1], N_CHIPS * g.shape[2])
        else:
            full[n] = g.reshape(N_CHIPS * g.shape[1], g.shape[2])
    w_in_full = full["w_in"]
    big = {"w_main": w_in_full[:, :8192],
           "w_dt": jnp.pad(w_in_full[:, 8192:8192 + SSD_HEADS], ((0, 0), (0, LANES - SSD_HEADS))),
           "w_gates": w_in_full[:, 8192 + SSD_HEADS:]}
    for n in ("w_pa", "w_pb", "w_out", "w_ff1", "w_ff2"):
        big[n] = full[n]
    small = {n: w[n] for n in SMALL_NAMES}
    small.update(conv_full)

    loss_cols, grad_x, d_mod, big_grads, small_grads = local_step(x, loss_target, mod, big, small)

    packed = _pack([d_mod, loss_cols] + [small_grads[n] for n in SMALL_NAMES])
    rows = packed.shape[0]
    everyone = allgather8(packed, name="gather_small").reshape(N_DEV, rows, PACK_WIDTH)
    d_mod_all = everyone[:, :bsz * 6].reshape(N_DEV * bsz, 6 * d)
    summed = sum_blocks(everyone, name="sum_small")
    shapes = [d_mod.shape, loss_cols.shape] + [small_grads[n].shape for n in SMALL_NAMES]
    parts = _unpack(summed, shapes)
    loss = jnp.sum(parts[1])
    grads = dict(zip(SMALL_NAMES, parts[2:]))
    for n in SHARDED_SMALL:
        cols = w[n].shape[1]
        grads[n] = lax.dynamic_slice(grads[n], (0, chip * cols), (grads[n].shape[0], cols))
    d_mod_cols = lax.dynamic_slice(d_mod_all, (0, chip * ncol), (N_DEV * bsz, ncol))
    grads["w_ada"], grads["b_ada"] = ada_bwd(c_all, d_mod_all, d_mod_cols, name="ada_bwd")

    by_chip = []
    for n in BIG_NAMES:
        g = big_grads[n]
        if n in COLUMN_SHARDED:
            by_chip.append(jnp.transpose(g.reshape(g.shape[0], N_CHIPS, g.shape[1] // N_CHIPS), (1, 0, 2)))
        else:
            by_chip.append(g.reshape(N_CHIPS, g.shape[0] // N_CHIPS, g.shape[1]))
    mine, theirs = split_to_sibling(by_chip, name="grads_to_sibling")
    chip_sums = [add_to_bf16(a, b, name="add_cores_" + n) for n, a, b in zip(BIG_NAMES, mine, theirs)]
    landed = scatter_to_chips(chip_sums, name="grads_to_chips")
    halves = [sum_blocks(p, name="add_chips_" + n) for n, p in zip(BIG_NAMES, landed)]
    for n, g in zip(BIG_NAMES, join_with_sibling(halves, name="grads_join")):
        grads[n] = g

    delta, new_m, new_v = {}, {}, {}
    for n in BIG_NAMES + ["w_ada", "b_ada"]:
        delta[n], new_m[n], new_v[n] = adamw(w[n], grads[n], m[n], v[n], name="adamw_" + n)
    shapes = [w[n].shape for n in SMALL_NAMES]
    pk = lambda src: _pack([src[n] for n in SMALL_NAMES])
    upd = adamw(pk(w), pk(grads), pk(m), pk(v), name="adamw_small")
    for out, packed_out in zip((delta, new_m, new_v), upd):
        out.update(zip(SMALL_NAMES, _unpack(packed_out, shapes)))

    shaped = lambda src: [src[n].reshape(given[n].shape) for n in WEIGHT_NAMES]
    return (loss, grad_x, *shaped(grads), *shaped(delta), *shaped(new_m), *shaped(new_v))
```

```python
import functools
import math

import numpy as np
import jax
import jax.numpy as jnp
from jax import lax
from jax.experimental import pallas as pl
from jax.experimental.pallas import tpu as pltpu

F32 = jnp.float32
BF16 = jnp.bfloat16
HI = lax.Precision.HIGHEST
MESH = pl.DeviceIdType.MESH

D_MODEL = 1024
LRU_HEADS = 16
LRU_HEAD_DIM = 64
LRU_C = 8.0
SSD_INNER = 2048
SSD_HEADS = 32
SSD_HEAD_DIM = 64
SSD_GROUPS = 8
SSD_STATE = 128
SSD_CHUNK = 128
SSD_CONV_DIM = 4096
D_FF = 4096
EPS = 1e-6
N_CHIPS = 4
N_DEV = 8
LANES = 128
SUBLANES = 8

ADAM_LR = 0.001
ADAM_B1 = 0.9
ADAM_B2 = 0.999
ADAM_EPS = 1e-08
ADAM_WD = 0.01
ADAM_STEP = 10


def _pcall(body, **kw):
    return pl.pallas_call(body, **kw)


def _sigmoid(v):
    return 1.0 / (1.0 + jnp.exp(-v))


def _log1p(u):
    return jnp.where(u < 1e-3, u * (1.0 - u * (0.5 - u * (1.0 / 3.0))), jnp.log(1.0 + u))


def _softplus(v):
    return jnp.maximum(v, 0.0) + _log1p(jnp.exp(-jnp.abs(v)))


def _neg_expm1(v):
    small = -v * (1.0 + v * (0.5 + v * (1.0 / 6.0 + v * (1.0 / 24.0))))
    return jnp.where(v > -0.05, small, 1.0 - jnp.exp(v))


_GELU_K = math.sqrt(2.0 / math.pi)


def _gelu(v):
    t = jnp.tanh(_GELU_K * (v + 0.044715 * v * v * v))
    return 0.5 * v * (1.0 + t)


def _gelu_grad(v):
    t = jnp.tanh(_GELU_K * (v + 0.044715 * v * v * v))
    return 0.5 * (1.0 + t) + 0.5 * v * (1.0 - t * t) * _GELU_K * (1.0 + 3.0 * 0.044715 * v * v)


def _colsum(v):
    return jnp.sum(v, axis=0, keepdims=True)


def _dot(a, b, precision=None):
    return lax.dot_general(a, b, (((1,), (0,)), ((), ())), preferred_element_type=F32, precision=precision)


def _dot_nt(a, b):
    return lax.dot_general(a, b, (((1,), (1,)), ((), ())), preferred_element_type=F32)


def _dot_tn(a, b):
    return lax.dot_general(a, b, (((0,), (0,)), ((), ())), preferred_element_type=F32)


def _shift_down(xt, prev8, j):
    if j == 0:
        return xt
    n = xt.shape[0]
    r = pltpu.roll(xt, j, 0)
    p = pltpu.roll(prev8, j, 0)
    rows = lax.broadcasted_iota(jnp.int32, (SUBLANES, xt.shape[1]), 0)
    top = jnp.where(rows < j, p, r[0:SUBLANES])
    if n == SUBLANES:
        return top
    return jnp.concatenate([top, r[SUBLANES:]], axis=0)


def _shift_up(xt, next8, j):
    if j == 0:
        return xt
    n = xt.shape[0]
    r = pltpu.roll(xt, n - j, 0)
    p = pltpu.roll(next8, SUBLANES - j, 0)
    rows = lax.broadcasted_iota(jnp.int32, (SUBLANES, xt.shape[1]), 0)
    bot = jnp.where(rows >= SUBLANES - j, p, r[n - SUBLANES:])
    if n == SUBLANES:
        return bot
    return jnp.concatenate([r[:n - SUBLANES], bot], axis=0)


def _conv4(xt, prev8, w, b):
    out = b + w[3:4] * xt
    for k in range(3):
        out = out + w[k:k + 1] * _shift_down(xt, prev8, 3 - k)
    return out


def _conv4_bwd(d_out, next8, xt, prev8, w):
    d_x = w[3:4] * d_out
    for k in range(3):
        d_x = d_x + w[k:k + 1] * _shift_up(d_out, next8, 3 - k)
    d_w = [_colsum(d_out * _shift_down(xt, prev8, 3 - k)) for k in range(3)] + [_colsum(d_out * xt)]
    return d_x, d_w, _colsum(d_out)


def _stack_rows(rows, width):
    rows = list(rows) + [jnp.zeros((1, width), F32)] * (SUBLANES - len(rows))
    return jnp.concatenate(rows, axis=0)


def _pick(n, cands):
    for c in cands:
        if n % c == 0:
            return c
    raise ValueError(f"no tile for {n}")


MM_ROWS = 512
MM_PANEL_COLS = 2048
MM_SUB = 512


def mm_nn(pairs, *, name, out_dtype=F32, a_fn=None, add=None, epi=None, extra=None):
    np_ = len(pairs)
    m, n = pairs[0][0].shape[0], pairs[0][1].shape[1]
    tm = _pick(m, (MM_ROWS, 256, 128, 64, 32, 16, 8))
    pn = n if n <= MM_PANEL_COLS else _pick(n, (MM_PANEL_COLS, 1024, 512, 256, 128))
    ns = _pick(pn, (MM_SUB, 256, 128))
    has_add, has_extra = add is not None, extra is not None
    stage0 = a_fn is not None or pairs[0][0].dtype != BF16

    def body(*refs):
        a_refs, b_refs = refs[:np_], refs[np_:2 * np_]
        pos = 2 * np_
        add_ref = extra_ref = None
        if has_add:
            add_ref = refs[pos]
            pos += 1
        if has_extra:
            extra_ref = refs[pos]
            pos += 1
        o_ref = refs[pos]
        lhs = list(a_refs)
        if stage0:
            av = a_refs[0][...]
            if a_fn is not None:
                av = a_fn(av)
            refs[pos + 1][...] = av.astype(BF16)
            lhs[0] = refs[pos + 1]
        for n0 in range(0, pn, ns):
            sl = slice(n0, n0 + ns)
            acc = None
            for a_ref, b_ref in zip(lhs, b_refs):
                part = _dot(a_ref[...].astype(BF16), b_ref[:, sl])
                acc = part if acc is None else acc + part
            if has_add:
                acc = acc + add_ref[:, sl]
            if epi is not None:
                acc = epi(acc, extra_ref[:, sl]) if has_extra else epi(acc)
            o_ref[:, sl] = acc.astype(out_dtype)

    in_specs = [pl.BlockSpec((tm, a.shape[1]), lambda j, i: (i, 0)) for a, _ in pairs]
    in_specs += [pl.BlockSpec((b.shape[0], pn), lambda j, i: (0, j)) for _, b in pairs]
    args = [a for a, _ in pairs] + [b for _, b in pairs]
    tile = pl.BlockSpec((tm, pn), lambda j, i: (i, j))
    if has_add:
        in_specs.append(tile)
        args.append(add)
    if has_extra:
        in_specs.append(tile)
        args.append(extra)
    return _pcall(
        body, name=name, grid=(n // pn, m // tm), in_specs=in_specs, out_specs=tile,
        out_shape=jax.ShapeDtypeStruct((m, n), out_dtype),
        scratch_shapes=[pltpu.VMEM((tm, pairs[0][0].shape[1]), BF16)] if stage0 else [],
    )(*args)


MM_REDUCE_ROWS = 1024
MM_GRAD_ROWS = 1024
MM_GRAD_COLS = 2048


def mm_tn(a, b, *, name, a_fn=None):
    m, ka = a.shape
    nb = b.shape[1]
    pa = _pick(ka, (MM_GRAD_ROWS, 512, 256, 128))
    pb = nb if nb <= MM_GRAD_COLS else _pick(nb, (MM_GRAD_COLS, 1024, 512, 256, 128))
    ns = _pick(pb, (MM_SUB, 256, 128))
    tmk = _pick(m, (MM_REDUCE_ROWS, 512, 256, 128, 64, 32, 16))

    def body(a_ref, b_ref, o_ref, lhs):
        k = pl.program_id(2)

        @pl.when(k == 0)
        def _():
            o_ref[...] = jnp.zeros_like(o_ref)

        av = a_ref[...]
        if a_fn is not None:
            av = a_fn(av)
        lhs[...] = av.astype(BF16)
        for n0 in range(0, pb, ns):
            o_ref[:, n0:n0 + ns] += _dot_tn(lhs[...], b_ref[:, n0:n0 + ns].astype(BF16))

    return _pcall(
        body, name=name,
        grid=(ka // pa, nb // pb, m // tmk),
        in_specs=[pl.BlockSpec((tmk, pa), lambda i, j, k: (k, i)),
                  pl.BlockSpec((tmk, pb), lambda i, j, k: (k, j))],
        out_specs=pl.BlockSpec((pa, pb), lambda i, j, k: (i, j)),
        out_shape=jax.ShapeDtypeStruct((ka, nb), F32),
        scratch_shapes=[pltpu.VMEM((tmk, pa), BF16)],
    )(a, b)


def _relu_sq(v):
    r = jnp.maximum(v, 0.0)
    return r * r


ROW_TILE = 512


def _row_specs(bsz, seq, width, ts):
    return pl.BlockSpec((None, ts, width), lambda b, i: (b, i, 0))


def _vec_spec(width):
    return pl.BlockSpec((1, width), lambda b, i: (0, 0))


def _mod_spec():
    return pl.BlockSpec((None, SUBLANES, D_MODEL), lambda b, i: (b, 0, 0))


def _rstd(v):
    return lax.rsqrt(jnp.mean(v * v, axis=-1, keepdims=True) + EPS)


def prenorm(x, w, mod, *, name):
    bsz, seq, d = x.shape
    ts = _pick(seq, (ROW_TILE, 256, 128))

    def body(x_ref, w_ref, mod_ref, h_ref):
        xv = x_ref[...]
        m = mod_ref[...]
        xh = xv * _rstd(xv)
        h_ref[...] = ((xh * w_ref[...]) * (1.0 + m[1:2]) + m[0:1]).astype(BF16)

    return _pcall(
        body, name=name, grid=(bsz, seq // ts),
        in_specs=[_row_specs(bsz, seq, d, ts), _vec_spec(d), _mod_spec()],
        out_specs=_row_specs(bsz, seq, d, ts),
        out_shape=jax.ShapeDtypeStruct((bsz, seq, d), BF16),
    )(x, w, mod)


def post1_pre2(x, out1, mod, post1, pre2, *, name):
    bsz, seq, d = x.shape
    ts = _pick(seq, (ROW_TILE, 256, 128))

    def body(x_ref, o_ref, mod_ref, p1_ref, p2_ref, x1_ref, h2_ref):
        m = mod_ref[...]
        ov = o_ref[...]
        x1 = x_ref[...] + m[2:3] * ((ov * _rstd(ov)) * p1_ref[...])
        x1_ref[...] = x1
        xh = x1 * _rstd(x1)
        h2_ref[...] = ((xh * p2_ref[...]) * (1.0 + m[4:5]) + m[3:4]).astype(BF16)

    return _pcall(
        body, name=name, grid=(bsz, seq // ts),
        in_specs=[_row_specs(bsz, seq, d, ts), _row_specs(bsz, seq, d, ts), _mod_spec(), _vec_spec(d), _vec_spec(d)],
        out_specs=[_row_specs(bsz, seq, d, ts), _row_specs(bsz, seq, d, ts)],
        out_shape=[jax.ShapeDtypeStruct((bsz, seq, d), F32), jax.ShapeDtypeStruct((bsz, seq, d), BF16)],
    )(x, out1, mod, post1, pre2)


def _acc_specs(d):
    per_batch = pl.BlockSpec((None, SUBLANES, d), lambda b, i: (b, 0, 0))
    glob = pl.BlockSpec((SUBLANES, d), lambda b, i: (0, 0))
    return per_batch, glob


def _accumulate(pb_ref, gl_ref, pb_rows, gl_rows, width):
    b, i = pl.program_id(0), pl.program_id(1)

    @pl.when(i == 0)
    def _():
        pb_ref[...] = jnp.zeros_like(pb_ref)

    @pl.when((b == 0) & (i == 0))
    def _():
        gl_ref[...] = jnp.zeros_like(gl_ref)

    pb_ref[...] += _stack_rows(pb_rows, width)
    gl_ref[...] += _stack_rows(gl_rows, width)


def _rms_bwd(d_n, n, r):
    return r * (d_n - n * jnp.mean(d_n * n, axis=-1, keepdims=True))


def final_bwd(x1, y2, target, mod, post2, *, name):
    bsz, seq, d = x1.shape
    ts = _pick(seq, (ROW_TILE, 256, 128))

    def body(x1_ref, y_ref, t_ref, mod_ref, p_ref, dx_ref, dy_ref, pb_ref, gl_ref):
        m = mod_ref[...]
        g2 = m[5:6]
        yv = y_ref[...]
        r = _rstd(yv)
        n = yv * r
        o = n * p_ref[...]
        diff = (x1_ref[...] + g2 * o) - t_ref[...]
        dx = diff * (1.0 / d)
        dx_ref[...] = dx
        d_o = dx * g2
        dy_ref[...] = _rms_bwd(d_o * p_ref[...], n, r).astype(BF16)
        _accumulate(pb_ref, gl_ref, [_colsum(dx * o)], [_colsum(d_o * n), _colsum(diff * diff) * (0.5 / d)], d)

    pb, gl = _acc_specs(d)
    rs = _row_specs(bsz, seq, d, ts)
    return _pcall(
        body, name=name, grid=(bsz, seq // ts),
        in_specs=[rs, rs, rs, _mod_spec(), _vec_spec(d)],
        out_specs=[rs, rs, pb, gl],
        out_shape=[jax.ShapeDtypeStruct((bsz, seq, d), F32), jax.ShapeDtypeStruct((bsz, seq, d), BF16),
                   jax.ShapeDtypeStruct((bsz, SUBLANES, d), F32), jax.ShapeDtypeStruct((SUBLANES, d), F32)],
    )(x1, y2, target, mod, post2)


def mid_bwd(d_h2, dx2, x1, out1, mod, pre2, post1, *, name):
    bsz, seq, d = x1.shape
    ts = _pick(seq, (ROW_TILE, 256, 128))

    def body(dh_ref, dx2_ref, x1_ref, o_ref, mod_ref, p2_ref, p1_ref, dx1_ref, do_ref, pb_ref, gl_ref):
        m = mod_ref[...]
        dh = dh_ref[...]
        x1 = x1_ref[...]
        r2 = _rstd(x1)
        xh = x1 * r2
        xw = xh * p2_ref[...]
        d_xw = dh * (1.0 + m[4:5])
        dx1 = dx2_ref[...] + _rms_bwd(d_xw * p2_ref[...], xh, r2)
        dx1_ref[...] = dx1
        ov = o_ref[...]
        r1 = _rstd(ov)
        n1 = ov * r1
        o1 = n1 * p1_ref[...]
        d_o1 = dx1 * m[2:3]
        do_ref[...] = _rms_bwd(d_o1 * p1_ref[...], n1, r1).astype(BF16)
        _accumulate(pb_ref, gl_ref, [_colsum(dh), _colsum(dh * xw), _colsum(dx1 * o1)],
                    [_colsum(d_xw * xh), _colsum(d_o1 * n1)], d)

    pb, gl = _acc_specs(d)
    rs = _row_specs(bsz, seq, d, ts)
    return _pcall(
        body, name=name, grid=(bsz, seq // ts),
        in_specs=[rs, rs, rs, rs, _mod_spec(), _vec_spec(d), _vec_spec(d)],
        out_specs=[rs, rs, pb, gl],
        out_shape=[jax.ShapeDtypeStruct((bsz, seq, d), F32), jax.ShapeDtypeStruct((bsz, seq, d), BF16),
                   jax.ShapeDtypeStruct((bsz, SUBLANES, d), F32), jax.ShapeDtypeStruct((SUBLANES, d), F32)],
    )(d_h2, dx2, x1, out1, mod, pre2, post1)


def first_bwd(d_h1, dx1, x, mod, pre1, *, name):
    bsz, seq, d = x.shape
    ts = _pick(seq, (ROW_TILE, 256, 128))

    def body(dh_ref, dx1_ref, x_ref, mod_ref, p_ref, gx_ref, pb_ref, gl_ref):
        m = mod_ref[...]
        dh = dh_ref[...]
        xv = x_ref[...]
        r = _rstd(xv)
        xh = xv * r
        xw = xh * p_ref[...]
        d_xw = dh * (1.0 + m[1:2])
        gx_ref[...] = dx1_ref[...] + _rms_bwd(d_xw * p_ref[...], xh, r)
        _accumulate(pb_ref, gl_ref, [_colsum(dh), _colsum(dh * xw)], [_colsum(d_xw * xh)], d)

    pb, gl = _acc_specs(d)
    rs = _row_specs(bsz, seq, d, ts)
    return _pcall(
        body, name=name, grid=(bsz, seq // ts),
        in_specs=[rs, rs, rs, _mod_spec(), _vec_spec(d)],
        out_specs=[rs, pb, gl],
        out_shape=[jax.ShapeDtypeStruct((bsz, seq, d), F32),
                   jax.ShapeDtypeStruct((bsz, SUBLANES, d), F32), jax.ShapeDtypeStruct((SUBLANES, d), F32)],
    )(d_h1, dx1, x, mod, pre1)


def merge_fwd(ya, yb, gates, b_gate, *, name):
    bsz, seq, d = ya.shape
    ts = _pick(seq, (ROW_TILE, 256, 128))

    def body(ya_ref, yb_ref, g_ref, b_ref, o_ref):
        g = _sigmoid(g_ref[...] + b_ref[...])
        o_ref[...] = (g[:, :d] * ya_ref[...] + g[:, d:] * yb_ref[...]).astype(BF16)

    rs = _row_specs(bsz, seq, d, ts)
    return _pcall(
        body, name=name, grid=(bsz, seq // ts),
        in_specs=[rs, rs, _row_specs(bsz, seq, 2 * d, ts), _vec_spec(2 * d)],
        out_specs=rs,
        out_shape=jax.ShapeDtypeStruct((bsz, seq, d), BF16),
    )(ya, yb, gates, b_gate)


def merge_bwd(d_merged, ya, yb, gates, b_gate, *, name):
    bsz, seq, d = ya.shape
    ts = _pick(seq, (ROW_TILE, 256, 128))

    def body(dm_ref, ya_ref, yb_ref, g_ref, b_ref, dya_ref, dyb_ref, dg_ref, gl_ref):
        b, i = pl.program_id(0), pl.program_id(1)
        g = _sigmoid(g_ref[...] + b_ref[...])
        dm = dm_ref[...]
        ga, gb = g[:, :d], g[:, d:]
        dya_ref[...] = (dm * ga).astype(BF16)
        dyb_ref[...] = (dm * gb).astype(BF16)
        dg = jnp.concatenate([dm * ya_ref[...] * ga * (1.0 - ga), dm * yb_ref[...] * gb * (1.0 - gb)], axis=1)
        dg_ref[...] = dg.astype(BF16)

        @pl.when((b == 0) & (i == 0))
        def _():
            gl_ref[...] = jnp.zeros_like(gl_ref)

        gl_ref[...] += _stack_rows([_colsum(dg)], 2 * d)

    rs = _row_specs(bsz, seq, d, ts)
    rs2 = _row_specs(bsz, seq, 2 * d, ts)
    return _pcall(
        body, name=name, grid=(bsz, seq // ts),
        in_specs=[rs, rs, rs, rs2, _vec_spec(2 * d)],
        out_specs=[rs, rs, rs2, pl.BlockSpec((SUBLANES, 2 * d), lambda b, i: (0, 0))],
        out_shape=[jax.ShapeDtypeStruct((bsz, seq, d), BF16), jax.ShapeDtypeStruct((bsz, seq, d), BF16),
                   jax.ShapeDtypeStruct((bsz, seq, 2 * d), BF16), jax.ShapeDtypeStruct((SUBLANES, 2 * d), F32)],
    )(d_merged, ya, yb, gates, b_gate)


LRU_TILE = 256
N_LRU_BLOCKS = D_MODEL // LANES


def _block_mm(v, w_ref, transpose=False):
    vb = v.astype(BF16)
    outs = []
    for j in range(N_LRU_BLOCKS):
        blk = vb[:, LANES * j:LANES * (j + 1)]
        outs.append(_dot_nt(blk, w_ref[j]) if transpose else _dot(blk, w_ref[j]))
    return jnp.concatenate(outs, axis=1)


def _lru_gates(xc, wa_ref, ba, wx_ref, bx, sp):
    r = _sigmoid(_block_mm(xc, wa_ref) + ba)
    i = _sigmoid(_block_mm(xc, wx_ref) + bx)
    la = (-LRU_C * r) * sp
    a = jnp.exp(la)
    sq = jnp.sqrt(_neg_expm1(2.0 * la))
    return r, i, a, sq


def _prev8_spec(width, col_block, tile_rows):
    per = tile_rows // SUBLANES
    return pl.BlockSpec((None, SUBLANES, width), lambda b, i: (b, jnp.maximum(i * per - 1, 0), col_block))


def lru_fwd(pm, cw, cb, wa, ba, wx, bx, lam, *, name):
    bsz, seq, _ = pm.shape
    d = D_MODEL
    ts = _pick(seq, (LRU_TILE, 128))

    def body(lx_ref, lxp_ref, lg_ref, cw_ref, cb_ref, wa_ref, ba_ref, wx_ref, bx_ref, lam_ref,
             h_ref, pa_ref, hc, a_s, u_s):
        i = pl.program_id(1)

        @pl.when(i == 0)
        def _():
            hc[...] = jnp.zeros_like(hc)

        lx = lx_ref[...]
        prev8 = jnp.where(i == 0, 0.0, lxp_ref[...])
        xc = _conv4(lx, prev8, cw_ref[...], cb_ref[...])
        sp = _softplus(-lam_ref[...])
        r, ig, a, sq = _lru_gates(xc, wa_ref, ba_ref[...], wx_ref, bx_ref[...], sp)
        a_s[...] = a
        u_s[...] = sq * (ig * xc)

        def step(g, h):
            r0 = pl.multiple_of(g * SUBLANES, SUBLANES)
            a8 = a_s[pl.ds(r0, SUBLANES), :]
            u8 = u_s[pl.ds(r0, SUBLANES), :]
            rows = []
            for j in range(SUBLANES):
                h = a8[j:j + 1] * h + u8[j:j + 1]
                rows.append(h)
            h_ref[pl.ds(r0, SUBLANES), :] = jnp.concatenate(rows, axis=0)
            return h

        hc[...] = lax.fori_loop(0, ts // SUBLANES, step, hc[...])
        pa_ref[...] = (h_ref[...] * _gelu(lg_ref[...])).astype(BF16)

    vec = _vec_spec(d)
    wspec = pl.BlockSpec((N_LRU_BLOCKS, LANES, LANES), lambda b, i: (0, 0, 0))
    return _pcall(
        body, name=name, grid=(bsz, seq // ts),
        in_specs=[pl.BlockSpec((None, ts, d), lambda b, i: (b, i, 0)), _prev8_spec(d, 0, ts),
                  pl.BlockSpec((None, ts, d), lambda b, i: (b, i, 1)),
                  pl.BlockSpec((4, d), lambda b, i: (0, 0)), vec, wspec, vec, wspec, vec, vec],
        out_specs=[_row_specs(bsz, seq, d, ts), _row_specs(bsz, seq, d, ts)],
        out_shape=[jax.ShapeDtypeStruct((bsz, seq, d), F32), jax.ShapeDtypeStruct((bsz, seq, d), BF16)],
        scratch_shapes=[pltpu.VMEM((1, d), F32), pltpu.VMEM((ts, d), F32), pltpu.VMEM((ts, d), F32)],
    )(pm, pm, pm, cw, cb, wa, ba, wx, bx, lam)


def lru_bwd(pm, h, d_pa, cw, cb, wa, ba, wx, bx, lam, *, name):
    bsz, seq, _ = pm.shape
    d = D_MODEL
    ts = _pick(seq, (LRU_TILE, 128))
    nt = seq // ts
    per = ts // SUBLANES

    def rev(i):
        return nt - 1 - i

    def body(lx_ref, lxp_ref, lg_ref, h_ref, hp_ref, dpa_ref, cw_ref, cb_ref, wa_ref, ba_ref, wx_ref, bx_ref,
             lam_ref, dl_ref, dwa_ref, dwx_ref, rows_ref, carry, dxc_next, a_s, dh_s, acc_s):
        b, i = pl.program_id(0), pl.program_id(1)
        t = rev(i)

        @pl.when(i == 0)
        def _():
            carry[...] = jnp.zeros_like(carry)
            dxc_next[...] = jnp.zeros_like(dxc_next)

        @pl.when((b == 0) & (i == 0))
        def _():
            dwa_ref[...] = jnp.zeros_like(dwa_ref)
            dwx_ref[...] = jnp.zeros_like(dwx_ref)
            rows_ref[...] = jnp.zeros_like(rows_ref)

        lx = lx_ref[...]
        lg = lg_ref[...]
        prev8 = jnp.where(t == 0, 0.0, lxp_ref[...])
        cwv = cw_ref[...]
        xc = _conv4(lx, prev8, cwv, cb_ref[...])
        lam_v = lam_ref[...]
        sp = _softplus(-lam_v)
        r, ig, a, sq = _lru_gates(xc, wa_ref, ba_ref[...], wx_ref, bx_ref[...], sp)
        hv = h_ref[...]
        d_pa = dpa_ref[...]
        a_s[...] = a
        dh_s[...] = d_pa * _gelu(lg)

        def step(g, c):
            r0 = pl.multiple_of((per - 1 - g) * SUBLANES, SUBLANES)
            a8 = a_s[pl.ds(r0, SUBLANES), :]
            d8 = dh_s[pl.ds(r0, SUBLANES), :]
            rows = [None] * SUBLANES
            for j in range(SUBLANES - 1, -1, -1):
                acc = d8[j:j + 1] + c
                rows[j] = acc
                c = a8[j:j + 1] * acc
            acc_s[pl.ds(r0, SUBLANES), :] = jnp.concatenate(rows, axis=0)
            return c

        carry[...] = lax.fori_loop(0, per, step, carry[...])
        d_u = acc_s[...]
        hprev8 = jnp.where(t == 0, 0.0, hp_ref[...])
        d_a = d_u * _shift_down(hv, hprev8, 1)
        d_sq = d_u * (ig * xc)
        d_i = d_u * (sq * xc)
        d_xc = d_u * (sq * ig)
        d_la = d_a * a - d_sq * (a * a) / sq
        d_pre_r = (d_la * (-LRU_C * sp)) * (r * (1.0 - r))
        d_pre_i = d_i * (ig * (1.0 - ig))
        d_xc = d_xc + _block_mm(d_pre_r, wa_ref, transpose=True) + _block_mm(d_pre_i, wx_ref, transpose=True)
        xcb = xc.astype(BF16)
        drb = d_pre_r.astype(BF16)
        dib = d_pre_i.astype(BF16)
        for j in range(N_LRU_BLOCKS):
            sl = slice(LANES * j, LANES * (j + 1))
            dwa_ref[j] += _dot_tn(xcb[:, sl], drb[:, sl])
            dwx_ref[j] += _dot_tn(xcb[:, sl], dib[:, sl])
        d_lx, d_cw, d_cb = _conv4_bwd(d_xc, dxc_next[...], lx, prev8, cwv)
        dxc_next[...] = d_xc[0:SUBLANES]
        d_lam = _colsum(d_la * (-LRU_C * r)) * (-_sigmoid(-lam_v))
        rows_ref[...] += _stack_rows([_colsum(d_pre_r), _colsum(d_pre_i), d_lam, d_cb] + d_cw, d)
        dl_ref[:, :d] = d_lx.astype(BF16)
        dl_ref[:, d:] = (d_pa * hv * _gelu_grad(lg)).astype(BF16)

    vec = _vec_spec(d)
    wspec = pl.BlockSpec((N_LRU_BLOCKS, LANES, LANES), lambda b, i: (0, 0, 0))
    tile = lambda col: pl.BlockSpec((None, ts, d), lambda b, i: (b, rev(i), col))
    prev8 = lambda col: pl.BlockSpec((None, SUBLANES, d), lambda b, i: (b, jnp.maximum(rev(i) * per - 1, 0), col))
    return _pcall(
        body, name=name, grid=(bsz, nt),
        in_specs=[tile(0), prev8(0), tile(1), tile(0), prev8(0), tile(0),
                  pl.BlockSpec((4, d), lambda b, i: (0, 0)), vec, wspec, vec, wspec, vec, vec],
        out_specs=[pl.BlockSpec((None, ts, 2 * d), lambda b, i: (b, rev(i), 0)), wspec, wspec,
                   pl.BlockSpec((SUBLANES, d), lambda b, i: (0, 0))],
        out_shape=[jax.ShapeDtypeStruct((bsz, seq, 2 * d), BF16),
                   jax.ShapeDtypeStruct((N_LRU_BLOCKS, LANES, LANES), F32),
                   jax.ShapeDtypeStruct((N_LRU_BLOCKS, LANES, LANES), F32),
                   jax.ShapeDtypeStruct((SUBLANES, d), F32)],
        scratch_shapes=[pltpu.VMEM((1, d), F32), pltpu.VMEM((SUBLANES, d), F32),
                        pltpu.VMEM((ts, d), F32), pltpu.VMEM((ts, d), F32), pltpu.VMEM((ts, d), F32)],
    )(pm, pm, pm, h, h, d_pa, cw, cb, wa, ba, wx, bx, lam)


L = SSD_CHUNK
N_PAIRS = SSD_HEADS // 2


def _ssd_common(xbc, prev8, dt_raw, cw, cb, dtb, alog, selt):
    conv = _conv4(xbc, prev8, cw, cb)
    sg = _sigmoid(conv)
    xa = conv * sg
    dtv = _softplus(dt_raw + dtb)
    a_neg = -jnp.exp(alog)
    rowi = lax.broadcasted_iota(jnp.int32, (L, L), 0)
    coli = lax.broadcasted_iota(jnp.int32, (L, L), 1)
    tril = (rowi >= coli).astype(F32)
    cs = _dot(tril, dtv * a_neg, precision=HI)
    cs_l = _dot(cs, selt, precision=HI)
    dt_l = _dot(dtv, selt, precision=HI)
    return conv, sg, xa, dtv, a_neg, cs, cs_l, dt_l, rowi, coli


def _head_masks():
    lane = lax.broadcasted_iota(jnp.int32, (L, LANES), 1)
    return lane < SSD_HEAD_DIM


def _stack_heads(v, first):
    return jnp.concatenate([jnp.where(first, v, 0.0), jnp.where(first, 0.0, v)], axis=0).astype(BF16)


def ssd_fwd(pm, dtr, cw, cb, dtb, alog, d_lanes, nw, selt, *, name):
    bsz, seq, _ = pm.shape
    nc = seq // L
    inner, cdim = SSD_INNER, SSD_CONV_DIM

    def body(xbc_ref, xp_ref, z_ref, dt_ref, cw_ref, cb_ref, dtb_ref, alog_ref, dl_ref, nw_ref, selt_ref,
             y_ref, yn_ref, st_ref, state):
        i = pl.program_id(1)

        @pl.when(i == 0)
        def _():
            state[...] = jnp.zeros_like(state)

        prev8 = jnp.where(i == 0, 0.0, xp_ref[...])
        _, _, xa, _, _, cs, cs_l, dt_l, rowi, coli = _ssd_common(
            xbc_ref[...], prev8, dt_ref[...], cw_ref[...], cb_ref[...], dtb_ref[...], alog_ref[...], selt_ref[...])
        xs = xa[:, :inner]
        xt = xs * dt_l
        e_l = jnp.exp(cs_l)
        cs_last = cs_l[L - 1:L, :]
        w_l = jnp.exp(cs_last - cs_l)
        e_last = jnp.exp(cs_last)
        cst = cs.T
        causal = rowi >= coli
        first = _head_masks()
        for g in range(SSD_GROUPS):
            bg = xa[:, inner + SSD_STATE * g:inner + SSD_STATE * (g + 1)].astype(BF16)
            cg = xa[:, inner + SSD_GROUPS * SSD_STATE + SSD_STATE * g:
                    inner + SSD_GROUPS * SSD_STATE + SSD_STATE * (g + 1)].astype(BF16)
            cbm = _dot_nt(cg, bg)
            for pp in range(2):
                p = 2 * g + pp
                sl = slice(LANES * p, LANES * (p + 1))
                ms = []
                for hh in (2 * p, 2 * p + 1):
                    seg = cs[:, hh:hh + 1] - cst[hh:hh + 1, :]
                    ms.append((cbm * jnp.exp(jnp.where(causal, seg, -jnp.inf))).astype(BF16))
                xp = xt[:, sl]
                y_diag = _dot(jnp.concatenate(ms, axis=1), _stack_heads(xp, first))
                st = state[p]
                st_ref[p] = st
                y_off = _dot(cg, st.astype(BF16)) * e_l[:, sl]
                y_ref[:, sl] = y_diag + y_off + dl_ref[:, sl] * xs[:, sl]
                state[p] = st * e_last[:, sl] + _dot_tn(bg, (xp * w_l[:, sl]).astype(BF16))
        zv = z_ref[...]
        yz = y_ref[...] * (zv * _sigmoid(zv))
        gw = inner // SSD_GROUPS
        for g in range(SSD_GROUPS):
            sl = slice(gw * g, gw * (g + 1))
            seg = yz[:, sl]
            yn_ref[:, sl] = ((seg * _rstd(seg)) * nw_ref[:, sl]).astype(BF16)

    cvec = lambda w: pl.BlockSpec((1, w), lambda b, i: (0, 0))
    return _pcall(
        body, name=name, grid=(bsz, nc),
        in_specs=[pl.BlockSpec((None, L, cdim), lambda b, i: (b, i, 1)), _prev8_spec(cdim, 1, L),
                  pl.BlockSpec((None, L, inner), lambda b, i: (b, i, 1)),
                  pl.BlockSpec((None, L, LANES), lambda b, i: (b, i, 0)),
                  pl.BlockSpec((4, cdim), lambda b, i: (0, 0)), cvec(cdim), cvec(LANES), cvec(LANES),
                  cvec(inner), cvec(inner), pl.BlockSpec((LANES, inner), lambda b, i: (0, 0))],
        out_specs=[pl.BlockSpec((None, L, inner), lambda b, i: (b, i, 0)),
                   pl.BlockSpec((None, L, inner), lambda b, i: (b, i, 0)),
                   pl.BlockSpec((None, None, N_PAIRS, SSD_STATE, LANES), lambda b, i: (b, i, 0, 0, 0))],
        out_shape=[jax.ShapeDtypeStruct((bsz, seq, inner), F32), jax.ShapeDtypeStruct((bsz, seq, inner), BF16),
                   jax.ShapeDtypeStruct((bsz, nc, N_PAIRS, SSD_STATE, LANES), F32)],
        scratch_shapes=[pltpu.VMEM((N_PAIRS, SSD_STATE, LANES), F32)],
    )(pm, pm, pm, dtr, cw, cb, dtb, alog, d_lanes, nw, selt)


def ssd_bwd(pm, dtr, y, states, d_yn, cw, cb, dtb, alog, d_lanes, nw, selt, sel, *, name):
    bsz, seq, _ = pm.shape
    nc = seq // L
    inner, cdim = SSD_INNER, SSD_CONV_DIM
    per = L // SUBLANES

    def rev(i):
        return nc - 1 - i

    def body(xbc_ref, xp_ref, z_ref, dt_ref, y_ref, st_ref, dyn_ref, cw_ref, cb_ref, dtb_ref, alog_ref,
             dl_ref, nw_ref, selt_ref, sel_ref, ds_ref, ddt_ref, r4_ref, r2_ref, r1_ref,
             dstate, dconv_next, dx_s, dbc_s, o_s, v_s, c0_s):
        b, i = pl.program_id(0), pl.program_id(1)
        t = rev(i)

        @pl.when(i == 0)
        def _():
            dstate[...] = jnp.zeros_like(dstate)
            dconv_next[...] = jnp.zeros_like(dconv_next)

        @pl.when((b == 0) & (i == 0))
        def _():
            r4_ref[...] = jnp.zeros_like(r4_ref)
            r2_ref[...] = jnp.zeros_like(r2_ref)
            r1_ref[...] = jnp.zeros_like(r1_ref)

        xbc = xbc_ref[...]
        prev8 = jnp.where(t == 0, 0.0, xp_ref[...])
        cwv = cw_ref[...]
        dt_in = dt_ref[...] + dtb_ref[...]
        conv, sg, xa, dtv, a_neg, cs, cs_l, dt_l, rowi, coli = _ssd_common(
            xbc, prev8, dt_ref[...], cwv, cb_ref[...], dtb_ref[...], alog_ref[...], selt_ref[...])
        xs = xa[:, :inner]
        xt = xs * dt_l
        e_l = jnp.exp(cs_l)
        cs_last = cs_l[L - 1:L, :]
        w_l = jnp.exp(cs_last - cs_l)
        e_last = jnp.exp(cs_last)
        cst = cs.T
        causal = rowi >= coli
        anti = coli >= rowi
        first = _head_masks()
        lane1 = lax.broadcasted_iota(jnp.int32, (L, LANES), 1)

        yv = y_ref[...]
        zv = z_ref[...]
        sz = _sigmoid(zv)
        zs = zv * sz
        yz = yv * zs
        dyn = dyn_ref[...]
        gw = inner // SSD_GROUPS
        d_yz_parts, d_nw_parts = [], []
        for g in range(SSD_GROUPS):
            sl = slice(gw * g, gw * (g + 1))
            seg = yz[:, sl]
            r = _rstd(seg)
            n = seg * r
            d_nw_parts.append(_colsum(dyn[:, sl] * n))
            d_yz_parts.append(_rms_bwd(dyn[:, sl] * nw_ref[:, sl], n, r))
        d_yz = jnp.concatenate(d_yz_parts, axis=1)
        d_y = d_yz * zs
        ds_ref[:, :inner] = (d_yz * yv * (sz * (1.0 + zv * (1.0 - sz)))).astype(BF16)
        dlv = dl_ref[...]
        d_dl = _colsum(d_y * xs)

        d_cs_q = jnp.zeros((L, LANES), F32)
        for g in range(SSD_GROUPS):
            bsl = slice(inner + SSD_STATE * g, inner + SSD_STATE * (g + 1))
            csl = slice(inner + SSD_GROUPS * SSD_STATE + SSD_STATE * g,
                        inner + SSD_GROUPS * SSD_STATE + SSD_STATE * (g + 1))
            bg = xa[:, bsl].astype(BF16)
            cg = xa[:, csl].astype(BF16)
            cbm = _dot_nt(cg, bg)
            cbt = _dot_nt(bg, cg)
            d_cb = jnp.zeros((L, L), F32)
            d_bg = jnp.zeros((L, SSD_STATE), F32)
            d_cg = jnp.zeros((L, SSD_STATE), F32)
            for pp in range(2):
                p = 2 * g + pp
                sl = slice(LANES * p, LANES * (p + 1))
                xp = xt[:, sl]
                xpb = xp.astype(BF16)
                dyp = d_y[:, sl]
                dypb = dyp.astype(BF16)
                dy_heads = (jnp.where(first, dyp, 0.0).astype(BF16), jnp.where(first, 0.0, dyp).astype(BF16))
                x_heads = (jnp.where(first, xp, 0.0).astype(BF16), jnp.where(first, 0.0, xp).astype(BF16))
                mts = []
                for k, hh in enumerate((2 * p, 2 * p + 1)):
                    col = cs[:, hh:hh + 1]
                    row = cst[hh:hh + 1, :]
                    dec = jnp.exp(jnp.where(causal, col - row, -jnp.inf))
                    dec_t = jnp.exp(jnp.where(anti, row - col, -jnp.inf))
                    gd = _dot_nt(dy_heads[k], xpb) * dec
                    d_cb = d_cb + gd
                    mt = cbt * dec_t
                    qd = gd * cbm - _dot_nt(x_heads[k], dypb) * mt
                    d_cs_q = jnp.where(lane1 == hh, jnp.sum(qd, axis=1, keepdims=True), d_cs_q)
                    mts.append(mt.astype(BF16))
                dst = dstate[p]
                dstb = dst.astype(BF16)
                st = st_ref[p]
                stb = st.astype(BF16)
                dye = (dyp * e_l[:, sl]).astype(BF16)
                xw = (xp * w_l[:, sl]).astype(BF16)
                dx_off = w_l[:, sl] * _dot(bg, dstb)
                dx_s[:, sl] = _dot(jnp.concatenate(mts, axis=1), jnp.concatenate(dy_heads, axis=0)) + dx_off
                o_s[:, sl] = dyp * (_dot(cg, stb) * e_l[:, sl])
                v_s[:, sl] = xp * dx_off
                c0_s[:, sl] = jnp.broadcast_to(_colsum(dst * st) * e_last[:, sl], (SUBLANES, LANES))
                d_cg = d_cg + _dot_nt(dye, stb)
                d_bg = d_bg + _dot_nt(xw, dstb)
                dstate[p] = dst * e_last[:, sl] + _dot_tn(cg, dye)
            d_cbb = d_cb.astype(BF16)
            dbc_s[:, SSD_STATE * g:SSD_STATE * (g + 1)] = d_bg + _dot_tn(d_cbb, cg)
            dbc_s[:, SSD_GROUPS * SSD_STATE + SSD_STATE * g:SSD_GROUPS * SSD_STATE + SSD_STATE * (g + 1)] = (
                d_cg + _dot(d_cbb, bg))

        d_xt = dx_s[...]
        selv = sel_ref[...]
        a1 = d_cs_q + _dot(o_s[...], selv, precision=HI)
        a2 = _dot(v_s[...], selv, precision=HI)
        c0 = _dot(c0_s[...], selv, precision=HI)[0:1]
        d_da = (_dot(anti.astype(F32), a1, precision=HI) + _dot((rowi > coli).astype(F32), a2, precision=HI) + c0)
        d_dt = d_da * a_neg + _dot(d_xt * xs, selv, precision=HI)
        d_alog = _colsum(d_da * dtv) * a_neg
        d_dtr = jnp.where(lane1 < SSD_HEADS, d_dt * _sigmoid(dt_in), 0.0)
        ddt_ref[...] = d_dtr.astype(BF16)
        d_xs = d_xt * dt_l + d_y * dlv
        d_xa = jnp.concatenate([d_xs, dbc_s[...]], axis=1)
        d_conv = d_xa * (sg * (1.0 + conv * (1.0 - sg)))
        d_xbc, d_cw, d_cbias = _conv4_bwd(d_conv, dconv_next[...], xbc, prev8, cwv)
        dconv_next[...] = d_conv[0:SUBLANES]
        ds_ref[:, inner:] = d_xbc.astype(BF16)
        r4_ref[...] += _stack_rows([d_cbias] + d_cw, cdim)
        r2_ref[...] += _stack_rows([jnp.concatenate(d_nw_parts, axis=1)], inner)
        r1_ref[...] += _stack_rows([_colsum(d_dtr), d_alog, _dot(jnp.broadcast_to(d_dl, (SUBLANES, inner)), selv,
                                                                   precision=HI)[0:1]], LANES)

    cvec = lambda w: pl.BlockSpec((1, w), lambda b, i: (0, 0))
    return _pcall(
        body, name=name, grid=(bsz, nc),
        in_specs=[pl.BlockSpec((None, L, cdim), lambda b, i: (b, rev(i), 1)),
                  pl.BlockSpec((None, SUBLANES, cdim), lambda b, i: (b, jnp.maximum(rev(i) * per - 1, 0), 1)),
                  pl.BlockSpec((None, L, inner), lambda b, i: (b, rev(i), 1)),
                  pl.BlockSpec((None, L, LANES), lambda b, i: (b, rev(i), 0)),
                  pl.BlockSpec((None, L, inner), lambda b, i: (b, rev(i), 0)),
                  pl.BlockSpec((None, None, N_PAIRS, SSD_STATE, LANES), lambda b, i: (b, rev(i), 0, 0, 0)),
                  pl.BlockSpec((None, L, inner), lambda b, i: (b, rev(i), 0)),
                  pl.BlockSpec((4, cdim), lambda b, i: (0, 0)), cvec(cdim), cvec(LANES), cvec(LANES),
                  cvec(inner), cvec(inner), pl.BlockSpec((LANES, inner), lambda b, i: (0, 0)),
                  pl.BlockSpec((inner, LANES), lambda b, i: (0, 0))],
        out_specs=[pl.BlockSpec((None, L, inner + cdim), lambda b, i: (b, rev(i), 0)),
                   pl.BlockSpec((None, L, LANES), lambda b, i: (b, rev(i), 0)),
                   pl.BlockSpec((SUBLANES, cdim), lambda b, i: (0, 0)),
                   pl.BlockSpec((SUBLANES, inner), lambda b, i: (0, 0)),
                   pl.BlockSpec((SUBLANES, LANES), lambda b, i: (0, 0))],
        out_shape=[jax.ShapeDtypeStruct((bsz, seq, inner + cdim), BF16),
                   jax.ShapeDtypeStruct((bsz, seq, LANES), BF16),
                   jax.ShapeDtypeStruct((SUBLANES, cdim), F32),
                   jax.ShapeDtypeStruct((SUBLANES, inner), F32),
                   jax.ShapeDtypeStruct((SUBLANES, LANES), F32)],
        scratch_shapes=[pltpu.VMEM((N_PAIRS, SSD_STATE, LANES), F32), pltpu.VMEM((SUBLANES, cdim), F32),
                        pltpu.VMEM((L, inner), F32), pltpu.VMEM((L, 2 * SSD_GROUPS * SSD_STATE), F32),
                        pltpu.VMEM((L, inner), F32), pltpu.VMEM((L, inner), F32), pltpu.VMEM((SUBLANES, inner), F32)],
    )(pm, pm, pm, dtr, y, states, d_yn, cw, cb, dtb, alog, d_lanes, nw, selt, sel)


def _lru_block_weights(w):
    w = w.reshape(N_LRU_BLOCKS, 2, LRU_HEAD_DIM, LRU_HEAD_DIM)
    z = jnp.zeros((N_LRU_BLOCKS, LRU_HEAD_DIM, LRU_HEAD_DIM), w.dtype)
    top = jnp.concatenate([w[:, 0], z], axis=2)
    bot = jnp.concatenate([z, w[:, 1]], axis=2)
    return jnp.concatenate([top, bot], axis=1).astype(BF16)


def _lru_block_grads(g):
    h = LRU_HEAD_DIM
    return jnp.stack([g[:, :h, :h], g[:, h:, h:]], axis=1).reshape(LRU_HEADS, h, h)


def _pad_lanes(v, width=LANES):
    return jnp.pad(v, ((0, 0), (0, width - v.shape[1])))


def local_step(x, target, mod, big, small):
    bsz, seq, d = x.shape
    t = bsz * seq
    flat = lambda v: v.reshape(t, v.shape[-1])
    unflat = lambda v: v.reshape(bsz, seq, v.shape[-1])

    sel = (jnp.arange(SSD_INNER)[:, None] // SSD_HEAD_DIM == jnp.arange(LANES)[None, :]).astype(F32)
    selt = sel.T
    wa_b = _lru_block_weights(small["lru_wa"])
    wx_b = _lru_block_weights(small["lru_wx"])
    dtb = _pad_lanes(small["ssd_dt_bias"])
    alog = _pad_lanes(small["ssd_a_log"])
    d_lanes = jnp.repeat(small["ssd_d"], SSD_HEAD_DIM, axis=1)

    lru_cols = 2 * D_MODEL
    wt = {"lru": big["w_main"][:, :lru_cols].T, "ssd": big["w_main"][:, lru_cols:].T, "gates": big["w_gates"].T,
          "dt": big["w_dt"].T}
    for n in ("w_pa", "w_pb", "w_out", "w_ff1", "w_ff2"):
        wt[n] = big[n].T

    h1 = prenorm(x, small["pre_norm1"], mod, name="prenorm1")
    h1f = flat(h1)
    pm = unflat(mm_nn([(h1f, big["w_main"])], name="in_proj_main"))
    gates = unflat(mm_nn([(h1f, big["w_gates"])], name="in_proj_gates"))
    dtr = unflat(mm_nn([(h1f, big["w_dt"])], name="in_proj_dt"))
    lru_args = (small["lru_conv_w"], small["lru_conv_b"], wa_b, small["lru_ba"], wx_b, small["lru_bx"],
                small["lru_lambda"])
    h_lru, pa_in = lru_fwd(pm, *lru_args, name="lru_fwd")
    ssd_args = (small["ssd_conv_w"], small["ssd_conv_b"], dtb, alog, d_lanes, small["ssd_norm_w"], selt)
    y_ssd, ynorm, states = ssd_fwd(pm, dtr, *ssd_args, name="ssd_fwd")
    ya = unflat(mm_nn([(flat(pa_in), big["w_pa"])], name="proj_a"))
    yb = unflat(mm_nn([(flat(ynorm), big["w_pb"])], name="proj_b"))
    merged = merge_fwd(ya, yb, gates, small["b_gate"], name="merge_fwd")
    out1 = unflat(mm_nn([(flat(merged), big["w_out"])], name="proj_out"))
    x1, h2 = post1_pre2(x, out1, mod, small["post_norm1"], small["pre_norm2"], name="post1_pre2")
    f = mm_nn([(flat(h2), big["w_ff1"])], name="ff1")
    y2 = unflat(mm_nn([(f, big["w_ff2"])], a_fn=_relu_sq, name="ff2"))

    dx2, d_y2, pb_a, gl_a = final_bwd(x1, y2, target, mod, small["post_norm2"], name="final_bwd")
    d_y2f = flat(d_y2)
    d_f = mm_nn([(d_y2f, wt["w_ff2"])], out_dtype=BF16, extra=f,
                epi=lambda r, fv: r * (2.0 * jnp.maximum(fv, 0.0)), name="ff2_dx")
    g_ff2 = mm_tn(f, d_y2f, a_fn=_relu_sq, name="ff2_dw")
    d_h2 = unflat(mm_nn([(d_f, wt["w_ff1"])], name="ff1_dx"))
    g_ff1 = mm_tn(flat(h2), d_f, name="ff1_dw")
    dx1, d_out1, pb_b, gl_b = mid_bwd(d_h2, dx2, x1, out1, mod, small["pre_norm2"], small["post_norm1"],
                                      name="mid_bwd")
    d_out1f = flat(d_out1)
    d_merged = unflat(mm_nn([(d_out1f, wt["w_out"])], name="out_dx"))
    g_out = mm_tn(flat(merged), d_out1f, name="out_dw")
    d_ya, d_yb, d_gates, gl_c = merge_bwd(d_merged, ya, yb, gates, small["b_gate"], name="merge_bwd")
    d_pa = unflat(mm_nn([(flat(d_ya), wt["w_pa"])], name="pa_dx"))
    g_pa = mm_tn(flat(pa_in), flat(d_ya), name="pa_dw")
    d_yn = unflat(mm_nn([(flat(d_yb), wt["w_pb"])], name="pb_dx"))
    g_pb = mm_tn(flat(ynorm), flat(d_yb), name="pb_dw")
    d_l, g_wa_b, g_wx_b, lru_rows = lru_bwd(pm, h_lru, d_pa, *lru_args, name="lru_bwd")
    d_s, d_dt, r4, r2, r1 = ssd_bwd(pm, dtr, y_ssd, states, d_yn, *ssd_args, sel, name="ssd_bwd")
    d_lf, d_sf, d_gf, d_dtf = flat(d_l), flat(d_s), flat(d_gates), flat(d_dt)
    d_h1 = mm_nn([(d_lf, wt["lru"]), (d_gf, wt["gates"]), (d_dtf, wt["dt"])], name="in_dx_lru_gates")
    d_h1 = mm_nn([(d_sf, wt["ssd"])], add=d_h1, name="in_dx_ssd")
    g_in = jnp.concatenate([
        mm_tn(h1f, d_lf, name="in_dw_lru"), mm_tn(h1f, d_sf, name="in_dw_ssd"),
        mm_tn(h1f, d_dtf, name="in_dw_dt")[:, :SSD_HEADS], mm_tn(h1f, d_gf, name="in_dw_gates")], axis=1)
    grad_x, pb_c, gl_d = first_bwd(unflat(d_h1), dx1, x, mod, small["pre_norm1"], name="first_bwd")

    d_mod = jnp.stack([pb_c[:, 0], pb_c[:, 1], pb_b[:, 2], pb_b[:, 0], pb_b[:, 1], pb_a[:, 0]], axis=1)
    loss_cols = gl_a[1:2]
    big_grads = {"w_in": g_in, "w_pa": g_pa, "w_pb": g_pb, "w_out": g_out, "w_ff1": g_ff1, "w_ff2": g_ff2}
    nh = SSD_HEADS
    small_grads = {
        "pre_norm1": gl_d[0:1], "post_norm1": gl_b[1:2], "b_gate": gl_c[0:1],
        "lru_conv_w": lru_rows[4:8], "lru_conv_b": lru_rows[3:4],
        "lru_wa": _lru_block_grads(g_wa_b), "lru_ba": lru_rows[0:1],
        "lru_wx": _lru_block_grads(g_wx_b), "lru_bx": lru_rows[1:2], "lru_lambda": lru_rows[2:3],
        "ssd_conv_w": r4[1:5], "ssd_conv_b": r4[0:1],
        "ssd_dt_bias": r1[0:1, :nh], "ssd_a_log": r1[1:2, :nh], "ssd_d": r1[2:3, :nh],
        "ssd_norm_w": r2[0:1], "pre_norm2": gl_b[0:1], "post_norm2": gl_a[0:1],
    }
    return loss_cols, grad_x, d_mod, big_grads, small_grads


ANY = pl.BlockSpec(memory_space=pl.ANY)


def _position():
    return lax.axis_index("x"), lax.axis_index("y"), lax.axis_index("c")


def _other_chips(x, y):
    return [(1 - x, y), (x, 1 - y), (1 - x, 1 - y)]


def allgather8(v, *, name):
    m_per, n = v.shape

    def body(x_ref, out_ref, send_sems, recv_sems, local_sem):
        x, y, c = _position()
        me, sibling = (x, y, c), (x, y, 1 - c)
        chips = _other_chips(x, y)

        def rows(px, py, pc):
            return out_ref.at[pl.ds((4 * px + 2 * py + pc) * m_per, m_per), :]

        def copy(k, block, to, src=None):
            return pltpu.make_async_remote_copy(
                src_ref=rows(*block) if src is None else src, dst_ref=rows(*block),
                send_sem=send_sems.at[k], recv_sem=recv_sems.at[k], device_id=to, device_id_type=MESH)

        mine = pltpu.make_async_copy(x_ref, rows(*me), local_sem)
        mine.start()
        first = [copy(0, me, sibling, src=x_ref)]
        first += [copy(1 + j, me, (*chip, c), src=x_ref) for j, chip in enumerate(chips)]
        for cp in first:
            cp.start()
        passed = [copy(4 + j, (*chip, c), sibling) for j, chip in enumerate(chips)]
        for j, chip in enumerate(chips):
            copy(1 + j, (*chip, c), me).wait_recv()
            passed[j].start()
        copy(0, sibling, me).wait_recv()
        for j, chip in enumerate(chips):
            copy(4 + j, (*chip, 1 - c), me).wait_recv()
        for cp in first + passed:
            cp.wait_send()
        mine.wait()

    return _pcall(
        body, name=name,
        out_shape=jax.ShapeDtypeStruct((N_DEV * m_per, n), v.dtype),
        in_specs=[pl.BlockSpec(memory_space=pltpu.VMEM)],
        out_specs=pl.BlockSpec(memory_space=pltpu.VMEM),
        scratch_shapes=[pltpu.SemaphoreType.DMA((7,)), pltpu.SemaphoreType.DMA((7,)), pltpu.SemaphoreType.DMA],
    )(v)


def gather_weights(shards, *, name):
    n = len(shards)
    half = [s.shape[0] // 2 for s in shards]

    def body(*refs):
        ins, outs = refs[:n], refs[n:2 * n]
        send_sems, recv_sems, local_sems = refs[2 * n:]
        x, y, c = _position()
        me_chip = 2 * x + y
        chips = _other_chips(x, y)

        def piece(w, chip, core):
            return outs[w].at[chip, pl.ds(core * half[w], half[w]), :]

        def copy(w, k, chip, core, to, src=None):
            dst = piece(w, chip, core)
            return pltpu.make_async_remote_copy(
                src_ref=dst if src is None else src, dst_ref=dst,
                send_sem=send_sems.at[6 * w + k], recv_sem=recv_sems.at[6 * w + k], device_id=to, device_id_type=MESH)

        local = [pltpu.make_async_copy(ins[w], outs[w].at[me_chip], local_sems.at[w]) for w in range(n)]
        for cp in local:
            cp.start()
        sent = []
        for w in range(n):
            for j, (px, py) in enumerate(chips):
                cp = copy(w, j, me_chip, c, (px, py, c), src=ins[w].at[pl.ds(c * half[w], half[w]), :])
                cp.start()
                sent.append(cp)
        for w in range(n):
            for j, (px, py) in enumerate(chips):
                copy(w, j, 2 * px + py, c, (px, py, c)).wait_recv()
                cp = copy(w, 3 + j, 2 * px + py, c, (x, y, 1 - c))
                cp.start()
                sent.append(cp)
        for w in range(n):
            for j, (px, py) in enumerate(chips):
                copy(w, 3 + j, 2 * px + py, 1 - c, (x, y, 1 - c)).wait_recv()
        for cp in sent:
            cp.wait_send()
        for cp in local:
            cp.wait()

    return _pcall(
        body, name=name,
        out_shape=[jax.ShapeDtypeStruct((N_CHIPS,) + s.shape, s.dtype) for s in shards],
        in_specs=[ANY] * n, out_specs=[ANY] * n,
        scratch_shapes=[pltpu.SemaphoreType.DMA((6 * n,)), pltpu.SemaphoreType.DMA((6 * n,)),
                        pltpu.SemaphoreType.DMA((n,))],
    )(*shards)


STAGE_BYTES = 2 << 20


def _stage_rows(rows, width, itemsize=4):
    return _pick(rows, tuple(t for t in (1024, 512, 256, 128, 64, 32, 16, 8) if t * width * itemsize <= STAGE_BYTES * 3 // 2))


def _staged(chunks, bufs, load_sems):
    count, pending = {}, {}

    def load(i):
        cls, src, _ = chunks[i]
        slot = count.get(cls, 0) % 2
        count[cls] = count.get(cls, 0) + 1
        for cp, remote in pending.pop((cls, slot), []):
            if remote:
                cp.wait_send()
            else:
                cp.wait()
        staged = bufs[cls].at[slot, pl.ds(0, src.shape[0]), :]
        ld = pltpu.make_async_copy(src, staged, load_sems[cls].at[slot])
        ld.start()
        return ld, cls, slot, staged

    cur = load(0)
    for i in range(len(chunks)):
        nxt = load(i + 1) if i + 1 < len(chunks) else None
        ld, cls, slot, staged = cur
        ld.wait()
        started = []
        for make in chunks[i][2]:
            cp, remote = make(staged, slot)
            cp.start()
            started.append((cp, remote))
        pending[(cls, slot)] = started
        cur = nxt
    for started in pending.values():
        for cp, remote in started:
            if remote:
                cp.wait_send()
            else:
                cp.wait()


def _stage_scratch(widths_rows, dtype):
    scratch = []
    for width, rows in widths_rows:
        scratch += [pltpu.VMEM((2, rows, width), dtype), pltpu.SemaphoreType.DMA((2,)), pltpu.SemaphoreType.DMA((2,)),
                    pltpu.SemaphoreType.DMA((2,))]
    return scratch


def send_half_to_sibling(grads, *, name):
    n = len(grads)
    half = [g.shape[1] // 2 for g in grads]
    widths = sorted({g.shape[2] for g in grads})
    chunk_rows = [_stage_rows(h, g.shape[2]) for g, h in zip(grads, half)]
    plan = [(w, k, r0) for w in range(n) for k in range(N_CHIPS) for r0 in range(0, half[w], chunk_rows[w])]

    def body(*refs):
        ins, theirs = refs[:n], refs[n:2 * n]
        recv_sems = refs[2 * n]
        stage = refs[2 * n + 1:]
        bufs = {wd: stage[4 * i] for i, wd in enumerate(widths)}
        load_sems = {wd: stage[4 * i + 1] for i, wd in enumerate(widths)}
        send_sems = {wd: stage[4 * i + 2] for i, wd in enumerate(widths)}
        x, y, c = _position()
        chunks = []
        for idx, (w, k, r0) in enumerate(plan):
            wd = grads[w].shape[2]
            rb = chunk_rows[w]

            def make(staged, slot, idx=idx, w=w, k=k, r0=r0, wd=wd, rb=rb):
                return pltpu.make_async_remote_copy(
                    src_ref=staged, dst_ref=theirs[w].at[k, pl.ds(r0, rb), :], send_sem=send_sems[wd].at[slot],
                    recv_sem=recv_sems.at[idx], device_id=(x, y, 1 - c), device_id_type=MESH), True

            chunks.append((wd, ins[w].at[k, pl.ds((1 - c) * half[w] + r0, rb), :], [make]))
        _staged(chunks, bufs, load_sems)
        for idx, (w, k, r0) in enumerate(plan):
            wd = grads[w].shape[2]
            landed = theirs[w].at[k, pl.ds(r0, chunk_rows[w]), :]
            pltpu.make_async_remote_copy(
                src_ref=landed, dst_ref=landed, send_sem=send_sems[wd].at[0], recv_sem=recv_sems.at[idx],
                device_id=(x, y, 1 - c), device_id_type=MESH).wait_recv()

    stage_rows = [(wd, max(r for g, r in zip(grads, chunk_rows) if g.shape[2] == wd)) for wd in widths]
    return _pcall(
        body, name=name,
        out_shape=[jax.ShapeDtypeStruct((N_CHIPS, h, g.shape[2]), g.dtype) for g, h in zip(grads, half)],
        in_specs=[ANY] * n, out_specs=[ANY] * n,
        scratch_shapes=[pltpu.SemaphoreType.DMA((len(plan),))] + _stage_scratch(stage_rows, F32),
    )(*grads)


def scatter_to_chips(parts, *, name):
    n = len(parts)

    def body(*refs):
        ins, outs = refs[:n], refs[n:2 * n]
        send_sems, recv_sems, local_sems = refs[2 * n:]
        x, y, c = _position()
        me_chip = 2 * x + y
        chips = _other_chips(x, y)
        local, sent = [], []
        for w in range(n):
            cp = pltpu.make_async_copy(ins[w].at[me_chip], outs[w].at[me_chip], local_sems.at[w])
            cp.start()
            local.append(cp)
            for j, (px, py) in enumerate(chips):
                cp = pltpu.make_async_remote_copy(
                    src_ref=ins[w].at[2 * px + py], dst_ref=outs[w].at[me_chip],
                    send_sem=send_sems.at[3 * w + j], recv_sem=recv_sems.at[3 * w + j],
                    device_id=(px, py, c), device_id_type=MESH)
                cp.start()
                sent.append(cp)
        for w in range(n):
            for j, (px, py) in enumerate(chips):
                slot = outs[w].at[2 * px + py]
                pltpu.make_async_remote_copy(
                    src_ref=slot, dst_ref=slot, send_sem=send_sems.at[3 * w + j], recv_sem=recv_sems.at[3 * w + j],
                    device_id=(px, py, c), device_id_type=MESH).wait_recv()
        for cp in sent:
            cp.wait_send()
        for cp in local:
            cp.wait()

    return _pcall(
        body, name=name, out_shape=[jax.ShapeDtypeStruct(p.shape, p.dtype) for p in parts],
        in_specs=[ANY] * n, out_specs=[ANY] * n,
        scratch_shapes=[pltpu.SemaphoreType.DMA((3 * n,)), pltpu.SemaphoreType.DMA((3 * n,)),
                        pltpu.SemaphoreType.DMA((n,))],
    )(*parts)


def join_with_sibling(halves, *, name):
    n = len(halves)
    widths = sorted({h.shape[1] for h in halves})
    chunk_rows = [_stage_rows(h.shape[0], h.shape[1]) for h in halves]
    plan = [(w, r0) for w in range(n) for r0 in range(0, halves[w].shape[0], chunk_rows[w])]

    def body(*refs):
        ins, outs = refs[:n], refs[n:2 * n]
        recv_sems = refs[2 * n]
        stage = refs[2 * n + 1:]
        bufs = {wd: stage[4 * i] for i, wd in enumerate(widths)}
        load_sems = {wd: stage[4 * i + 1] for i, wd in enumerate(widths)}
        send_sems = {wd: stage[4 * i + 2] for i, wd in enumerate(widths)}
        store_sems = {wd: stage[4 * i + 3] for i, wd in enumerate(widths)}
        x, y, c = _position()
        chunks = []
        for idx, (w, r0) in enumerate(plan):
            h, wd = halves[w].shape
            rb = chunk_rows[w]

            def to_sibling(staged, slot, idx=idx, w=w, r0=r0, h=h, wd=wd, rb=rb):
                return pltpu.make_async_remote_copy(
                    src_ref=staged, dst_ref=outs[w].at[pl.ds(c * h + r0, rb), :], send_sem=send_sems[wd].at[slot],
                    recv_sem=recv_sems.at[idx], device_id=(x, y, 1 - c), device_id_type=MESH), True

            def to_mine(staged, slot, w=w, r0=r0, h=h, wd=wd, rb=rb):
                return pltpu.make_async_copy(staged, outs[w].at[pl.ds(c * h + r0, rb), :], store_sems[wd].at[slot]), False

            chunks.append((wd, ins[w].at[pl.ds(r0, rb), :], [to_sibling, to_mine]))
        _staged(chunks, bufs, load_sems)
        for idx, (w, r0) in enumerate(plan):
            h, wd = halves[w].shape
            landed = outs[w].at[pl.ds((1 - c) * h + r0, chunk_rows[w]), :]
            pltpu.make_async_remote_copy(
                src_ref=landed, dst_ref=landed, send_sem=send_sems[wd].at[0], recv_sem=recv_sems.at[idx],
                device_id=(x, y, 1 - c), device_id_type=MESH).wait_recv()

    stage_rows = [(wd, max(r for h, r in zip(halves, chunk_rows) if h.shape[1] == wd)) for wd in widths]
    return _pcall(
        body, name=name,
        out_shape=[jax.ShapeDtypeStruct((2 * h.shape[0], h.shape[1]), h.dtype) for h in halves],
        in_specs=[ANY] * n, out_specs=[ANY] * n,
        scratch_shapes=[pltpu.SemaphoreType.DMA((len(plan),))] + _stage_scratch(stage_rows, F32),
    )(*halves)


def _row_tile(rows, cols, itemsize=4, budget=2 << 20):
    for t in (1024, 512, 256, 128, 64, 32, 16, 8):
        if rows % t == 0 and t * cols * itemsize <= budget:
            return t
    return rows


def add_half_to_bf16(core, full, theirs, *, name):
    k, r, c = theirs.shape
    tr = _row_tile(r, c)
    nb = r // tr

    def body(core_ref, a_ref, b_ref, o_ref):
        o_ref[...] = (a_ref[...] + b_ref[...]).astype(BF16)

    spec = pl.BlockSpec((None, tr, c), lambda i, j, core_ref: (i, j, 0))
    grid_spec = pltpu.PrefetchScalarGridSpec(
        num_scalar_prefetch=1, grid=(k, nb),
        in_specs=[pl.BlockSpec((None, tr, c), lambda i, j, core_ref: (i, core_ref[0] * nb + j, 0)), spec],
        out_specs=spec)
    return _pcall(body, name=name, grid_spec=grid_spec,
                  out_shape=jax.ShapeDtypeStruct(theirs.shape, BF16))(core, full, theirs)


def sum_blocks(v, *, name):
    k, r, c = v.shape
    tr = _row_tile(r, c * k)

    def body(v_ref, o_ref):
        acc = v_ref[0].astype(F32)
        for j in range(1, k):
            acc = acc + v_ref[j].astype(F32)
        o_ref[...] = acc

    return _pcall(body, name=name, grid=(r // tr,),
                  in_specs=[pl.BlockSpec((k, tr, c), lambda i: (0, i, 0))],
                  out_specs=pl.BlockSpec((tr, c), lambda i: (i, 0)),
                  out_shape=jax.ShapeDtypeStruct((r, c), F32))(v)


def adamw(w, g, m, v, *, name):
    r, c = w.shape
    tr = _row_tile(r, c, budget=1 << 20)
    m_scale = 1.0 / (1.0 - ADAM_B1 ** ADAM_STEP)
    v_scale = 1.0 / (1.0 - ADAM_B2 ** ADAM_STEP)

    def body(w_ref, g_ref, m_ref, v_ref, d_ref, nm_ref, nv_ref):
        gv = g_ref[...]
        nm = ADAM_B1 * m_ref[...] + (1.0 - ADAM_B1) * gv
        nv = ADAM_B2 * v_ref[...] + (1.0 - ADAM_B2) * (gv * gv)
        nm_ref[...] = nm
        nv_ref[...] = nv
        d_ref[...] = -ADAM_LR * ((nm * m_scale) / (jnp.sqrt(nv * v_scale) + ADAM_EPS) + ADAM_WD * w_ref[...])

    spec = pl.BlockSpec((tr, c), lambda i: (i, 0))
    return _pcall(body, name=name, grid=(r // tr,), in_specs=[spec] * 4, out_specs=[spec] * 3,
                  out_shape=[jax.ShapeDtypeStruct((r, c), F32)] * 3)(w, g, m, v)


def ada_fwd(c_all, w_shard, b_shard, *, name):
    bsz, d = c_all.shape
    ncol = w_shard.shape[1]

    def body(c_ref, w_ref, b_ref, o_ref):
        cv = c_ref[...]
        act = (cv * _sigmoid(cv)).astype(BF16)
        o_ref[...] = _dot(act, w_ref[...].astype(BF16)) + b_ref[...]

    tn = _pick(ncol, (512, 256, 128))
    return _pcall(body, name=name, grid=(ncol // tn,),
                  in_specs=[pl.BlockSpec((bsz, d), lambda j: (0, 0)), pl.BlockSpec((d, tn), lambda j: (0, j)),
                            pl.BlockSpec((1, tn), lambda j: (0, j))],
                  out_specs=pl.BlockSpec((bsz, tn), lambda j: (0, j)),
                  out_shape=jax.ShapeDtypeStruct((bsz, ncol), F32))(c_all, w_shard, b_shard)


def ada_bwd(c_all, d_mod_all, d_mod_cols, *, name):
    bsz, d = c_all.shape
    ncol = d_mod_cols.shape[1]
    nall = d_mod_all.shape[1]

    def body(c_ref, da_ref, dc_ref, gw_ref, gb_ref):
        cv = c_ref[...]
        act = (cv * _sigmoid(cv)).astype(BF16)
        gw_ref[...] = _dot_tn(act, dc_ref[...].astype(BF16))
        gb_ref[...] = _colsum(da_ref[...])

    full = lambda s: pl.BlockSpec(s, lambda: (0,) * len(s))
    return _pcall(body, name=name,
                  in_specs=[full((bsz, d)), full((bsz, nall)), full((bsz, ncol))],
                  out_specs=[full((d, ncol)), full((1, nall))],
                  out_shape=[jax.ShapeDtypeStruct((d, ncol), F32), jax.ShapeDtypeStruct((1, nall), F32)],
                  )(c_all, d_mod_all, d_mod_cols)


WEIGHT_NAMES = ['w_ada', 'b_ada', 'pre_norm1', 'post_norm1', 'w_in', 'b_gate', 'lru_conv_w', 'lru_conv_b', 'lru_wa',
                'lru_ba', 'lru_wx', 'lru_bx', 'lru_lambda', 'w_pa', 'ssd_conv_w', 'ssd_conv_b', 'ssd_dt_bias',
                'ssd_a_log', 'ssd_d', 'ssd_norm_w', 'w_pb', 'w_out', 'pre_norm2', 'post_norm2', 'w_ff1', 'w_ff2']
BIG_NAMES = ['w_in', 'w_pa', 'w_pb', 'w_out', 'w_ff1', 'w_ff2']
COLUMN_SHARDED = ('w_in', 'w_ff1')
SMALL_NAMES = [n for n in WEIGHT_NAMES if n not in BIG_NAMES + ['w_ada', 'b_ada']]
SHARDED_SMALL = ('lru_conv_w', 'ssd_conv_w')
PACK_WIDTH = 1024


def _pack(parts):
    flat = jnp.concatenate([p.reshape(-1).astype(F32) for p in parts])
    rows = -(-flat.shape[0] // (PACK_WIDTH * SUBLANES)) * SUBLANES
    return jnp.pad(flat, (0, rows * PACK_WIDTH - flat.shape[0])).reshape(rows, PACK_WIDTH)


def _unpack(packed, shapes):
    flat = packed.reshape(-1)
    out, pos = [], 0
    for s in shapes:
        size = int(np.prod(s))
        out.append(flat[pos:pos + size].reshape(s))
        pos += size
    return out


def kernel(x, c, w_ada, b_ada, pre_norm1, post_norm1, w_in, b_gate, lru_conv_w, lru_conv_b, lru_wa, lru_ba, lru_wx, lru_bx, lru_lambda, w_pa, ssd_conv_w, ssd_conv_b, ssd_dt_bias, ssd_a_log, ssd_d, ssd_norm_w, w_pb, w_out, pre_norm2, post_norm2, w_ff1, w_ff2, loss_target, m_w_ada, m_b_ada, m_pre_norm1, m_post_norm1, m_w_in, m_b_gate, m_lru_conv_w, m_lru_conv_b, m_lru_wa, m_lru_ba, m_lru_wx, m_lru_bx, m_lru_lambda, m_w_pa, m_ssd_conv_w, m_ssd_conv_b, m_ssd_dt_bias, m_ssd_a_log, m_ssd_d, m_ssd_norm_w, m_w_pb, m_w_out, m_pre_norm2, m_post_norm2, m_w_ff1, m_w_ff2, v_w_ada, v_b_ada, v_pre_norm1, v_post_norm1, v_w_in, v_b_gate, v_lru_conv_w, v_lru_conv_b, v_lru_wa, v_lru_ba, v_lru_wx, v_lru_bx, v_lru_lambda, v_w_pa, v_ssd_conv_w, v_ssd_conv_b, v_ssd_dt_bias, v_ssd_a_log, v_ssd_d, v_ssd_norm_w, v_w_pb, v_w_out, v_pre_norm2, v_post_norm2, v_w_ff1, v_w_ff2):
    given = dict(locals())
    bsz, seq, d = x.shape
    my_x, my_y, my_c = lax.axis_index("x"), lax.axis_index("y"), lax.axis_index("c")
    chip = 2 * my_x + my_y
    dev = 2 * chip + my_c
    strip = lambda a: a if a.ndim == 2 else a[0]
    w = {n: strip(given[n]) for n in WEIGHT_NAMES}
    m = {n: strip(given["m_" + n]) for n in WEIGHT_NAMES}
    v = {n: strip(given["v_" + n]) for n in WEIGHT_NAMES}

    first_shapes = [c.shape] + [w[n].shape for n in SHARDED_SMALL]
    first = allgather8(_pack([c] + [w[n] for n in SHARDED_SMALL]), name="gather_c_conv")
    first = first.reshape(N_DEV, -1, PACK_WIDTH)
    per_dev = [_unpack(first[k], first_shapes) for k in range(N_DEV)]
    c_all = jnp.concatenate([p[0] for p in per_dev], axis=0)
    conv_full = {n: jnp.concatenate([per_dev[2 * k][1 + i] for k in range(N_CHIPS)], axis=1)
                 for i, n in enumerate(SHARDED_SMALL)}

    ncol = w["w_ada"].shape[1]
    b_cols = lax.dynamic_slice(b_ada, (0, chip * ncol), (1, ncol))
    mod_cols = ada_fwd(c_all, w["w_ada"], b_cols, name="ada_fwd")
    mod_all = allgather8(mod_cols, name="gather_mod").reshape(N_CHIPS, 2, N_DEV * bsz, ncol)[:, 0]
    mod_all = jnp.transpose(mod_all, (1, 0, 2)).reshape(N_DEV * bsz, N_CHIPS * ncol)
    mod = lax.dynamic_slice(mod_all, (dev * bsz, 0), (bsz, 6 * d)).reshape(bsz, 6, d)
    mod = jnp.pad(mod, ((0, 0), (0, 2), (0, 0)))

    gathered = gather_weights([w[n].astype(BF16) for n in BIG_NAMES], name="gather_weights")
    full = {}
    for n, g in zip(BIG_NAMES, gathered):
        if n in COLUMN_SHARDED:
            full[n] = jnp.transpose(g, (1, 0, 2)).reshape(g.shape[1], N_CHIPS * g.shape[2])
        else:
            full[n] = g.reshape(N_CHIPS * g.shape[1], g.shape[2])
    w_in_full = full["w_in"]
    big = {"w_main": w_in_full[:, :8192],
           "w_dt": jnp.pad(w_in_full[:, 8192:8192 + SSD_HEADS], ((0, 0), (0, LANES - SSD_HEADS))),
           "w_gates": w_in_full[:, 8192 + SSD_HEADS:]}
    for n in ("w_pa", "w_pb", "w_out", "w_ff1", "w_ff2"):
        big[n] = full[n]
    small = {n: w[n] for n in SMALL_NAMES}
    small.update(conv_full)

    loss_cols, grad_x, d_mod, big_grads, small_grads = local_step(x, loss_target, mod, big, small)

    packed = _pack([d_mod, loss_cols] + [small_grads[n] for n in SMALL_NAMES])
    rows = packed.shape[0]
    everyone = allgather8(packed, name="gather_small").reshape(N_DEV, rows, PACK_WIDTH)
    d_mod_all = everyone[:, :bsz * 6].reshape(N_DEV * bsz, 6 * d)
    summed = sum_blocks(everyone, name="sum_small")
    shapes = [d_mod.shape, loss_cols.shape] + [small_grads[n].shape for n in SMALL_NAMES]
    parts = _unpack(summed, shapes)
    loss = jnp.sum(parts[1])
    grads = dict(zip(SMALL_NAMES, parts[2:]))
    for n in SHARDED_SMALL:
        cols = w[n].shape[1]
        grads[n] = lax.dynamic_slice(grads[n], (0, chip * cols), (grads[n].shape[0], cols))
    d_mod_cols = lax.dynamic_slice(d_mod_all, (0, chip * ncol), (N_DEV * bsz, ncol))
    grads["w_ada"], grads["b_ada"] = ada_bwd(c_all, d_mod_all, d_mod_cols, name="ada_bwd")

    by_chip = []
    for n in BIG_NAMES:
        g = big_grads[n]
        if n in COLUMN_SHARDED:
            by_chip.append(jnp.transpose(g.reshape(g.shape[0], N_CHIPS, g.shape[1] // N_CHIPS), (1, 0, 2)))
        else:
            by_chip.append(g.reshape(N_CHIPS, g.shape[0] // N_CHIPS, g.shape[1]))
    theirs = send_half_to_sibling(by_chip, name="grads_to_sibling")
    core = my_c.astype(jnp.int32).reshape(1)
    chip_sums = [add_half_to_bf16(core, a, b, name="add_cores_" + n) for n, a, b in zip(BIG_NAMES, by_chip, theirs)]
    landed = scatter_to_chips(chip_sums, name="grads_to_chips")
    halves = [sum_blocks(p, name="add_chips_" + n) for n, p in zip(BIG_NAMES, landed)]
    for n, g in zip(BIG_NAMES, join_with_sibling(halves, name="grads_join")):
        grads[n] = g

    delta, new_m, new_v = {}, {}, {}
    for n in BIG_NAMES + ["w_ada", "b_ada"]:
        delta[n], new_m[n], new_v[n] = adamw(w[n], grads[n], m[n], v[n], name="adamw_" + n)
    shapes = [w[n].shape for n in SMALL_NAMES]
    pk = lambda src: _pack([src[n] for n in SMALL_NAMES])
    upd = adamw(pk(w), pk(grads), pk(m), pk(v), name="adamw_small")
    for out, packed_out in zip((delta, new_m, new_v), upd):
        out.update(zip(SMALL_NAMES, _unpack(packed_out, shapes)))

    shaped = lambda src: [src[n].reshape(given[n].shape) for n in WEIGHT_NAMES]
    return (loss, grad_x, *shaped(grads), *shaped(delta), *shaped(new_m), *shaped(new_v))
```

```python
import functools
import math

import numpy as np
import jax
import jax.numpy as jnp
from jax import lax
from jax.experimental import pallas as pl
from jax.experimental.pallas import tpu as pltpu

F32 = jnp.float32
BF16 = jnp.bfloat16
HI = lax.Precision.HIGHEST
MESH = pl.DeviceIdType.MESH

D_MODEL = 1024
LRU_HEADS = 16
LRU_HEAD_DIM = 64
LRU_C = 8.0
SSD_INNER = 2048
SSD_HEADS = 32
SSD_HEAD_DIM = 64
SSD_GROUPS = 8
SSD_STATE = 128
SSD_CHUNK = 128
SSD_CONV_DIM = 4096
D_FF = 4096
EPS = 1e-6
N_CHIPS = 4
N_DEV = 8
LANES = 128
SUBLANES = 8

ADAM_LR = 0.001
ADAM_B1 = 0.9
ADAM_B2 = 0.999
ADAM_EPS = 1e-08
ADAM_WD = 0.01
ADAM_STEP = 10


def _pcall(body, **kw):
    return pl.pallas_call(body, **kw)


def _sigmoid(v):
    return 1.0 / (1.0 + jnp.exp(-v))


def _log1p(u):
    return jnp.where(u < 1e-3, u * (1.0 - u * (0.5 - u * (1.0 / 3.0))), jnp.log(1.0 + u))


def _softplus(v):
    return jnp.maximum(v, 0.0) + _log1p(jnp.exp(-jnp.abs(v)))


def _neg_expm1(v):
    small = -v * (1.0 + v * (0.5 + v * (1.0 / 6.0 + v * (1.0 / 24.0))))
    return jnp.where(v > -0.05, small, 1.0 - jnp.exp(v))


_GELU_K = math.sqrt(2.0 / math.pi)


def _gelu(v):
    t = jnp.tanh(_GELU_K * (v + 0.044715 * v * v * v))
    return 0.5 * v * (1.0 + t)


def _gelu_grad(v):
    t = jnp.tanh(_GELU_K * (v + 0.044715 * v * v * v))
    return 0.5 * (1.0 + t) + 0.5 * v * (1.0 - t * t) * _GELU_K * (1.0 + 3.0 * 0.044715 * v * v)


def _colsum(v):
    return jnp.sum(v, axis=0, keepdims=True)


def _dot(a, b, precision=None):
    return lax.dot_general(a, b, (((1,), (0,)), ((), ())), preferred_element_type=F32, precision=precision)


def _dot_nt(a, b):
    return lax.dot_general(a, b, (((1,), (1,)), ((), ())), preferred_element_type=F32)


def _dot_tn(a, b):
    return lax.dot_general(a, b, (((0,), (0,)), ((), ())), preferred_element_type=F32)


def _shift_down(xt, prev8, j):
    if j == 0:
        return xt
    n = xt.shape[0]
    r = pltpu.roll(xt, j, 0)
    p = pltpu.roll(prev8, j, 0)
    rows = lax.broadcasted_iota(jnp.int32, (SUBLANES, xt.shape[1]), 0)
    top = jnp.where(rows < j, p, r[0:SUBLANES])
    if n == SUBLANES:
        return top
    return jnp.concatenate([top, r[SUBLANES:]], axis=0)


def _shift_up(xt, next8, j):
    if j == 0:
        return xt
    n = xt.shape[0]
    r = pltpu.roll(xt, n - j, 0)
    p = pltpu.roll(next8, SUBLANES - j, 0)
    rows = lax.broadcasted_iota(jnp.int32, (SUBLANES, xt.shape[1]), 0)
    bot = jnp.where(rows >= SUBLANES - j, p, r[n - SUBLANES:])
    if n == SUBLANES:
        return bot
    return jnp.concatenate([r[:n - SUBLANES], bot], axis=0)


def _conv4(xt, prev8, w, b):
    out = b + w[3:4] * xt
    for k in range(3):
        out = out + w[k:k + 1] * _shift_down(xt, prev8, 3 - k)
    return out


def _conv4_bwd(d_out, next8, xt, w):
    d_x = w[3:4] * d_out
    d_w = []
    for k in range(3):
        up = _shift_up(d_out, next8, 3 - k)
        d_x = d_x + w[k:k + 1] * up
        d_w.append(_colsum(up * xt))
    d_w.append(_colsum(d_out * xt))
    return d_x, d_w, _colsum(d_out)


def _stack_rows(rows, width):
    rows = list(rows) + [jnp.zeros((1, width), F32)] * (SUBLANES - len(rows))
    return jnp.concatenate(rows, axis=0)


def _pick(n, cands):
    for c in cands:
        if n % c == 0:
            return c
    raise ValueError(f"no tile for {n}")


MM_ROWS = 512
MM_PANEL_COLS = 2048
MM_SUB = 512


def mm_nn(pairs, *, name, out_dtype=F32, a_fn=None, add=None, epi=None, extra=None):
    np_ = len(pairs)
    m, n = pairs[0][0].shape[0], pairs[0][1].shape[1]
    tm = _pick(m, (MM_ROWS, 256, 128, 64, 32, 16, 8))
    pn = n if n <= MM_PANEL_COLS else _pick(n, (MM_PANEL_COLS, 1024, 512, 256, 128))
    ns = _pick(pn, (MM_SUB, 256, 128))
    has_add, has_extra = add is not None, extra is not None
    stage0 = a_fn is not None or pairs[0][0].dtype != BF16

    def body(*refs):
        a_refs, b_refs = refs[:np_], refs[np_:2 * np_]
        pos = 2 * np_
        add_ref = extra_ref = None
        if has_add:
            add_ref = refs[pos]
            pos += 1
        if has_extra:
            extra_ref = refs[pos]
            pos += 1
        o_ref = refs[pos]
        lhs = list(a_refs)
        if stage0:
            av = a_refs[0][...]
            if a_fn is not None:
                av = a_fn(av)
            refs[pos + 1][...] = av.astype(BF16)
            lhs[0] = refs[pos + 1]
        for n0 in range(0, pn, ns):
            sl = slice(n0, n0 + ns)
            acc = None
            for a_ref, b_ref in zip(lhs, b_refs):
                part = _dot(a_ref[...].astype(BF16), b_ref[:, sl])
                acc = part if acc is None else acc + part
            if has_add:
                acc = acc + add_ref[:, sl]
            if epi is not None:
                acc = epi(acc, extra_ref[:, sl]) if has_extra else epi(acc)
            o_ref[:, sl] = acc.astype(out_dtype)

    in_specs = [pl.BlockSpec((tm, a.shape[1]), lambda j, i: (i, 0)) for a, _ in pairs]
    in_specs += [pl.BlockSpec((b.shape[0], pn), lambda j, i: (0, j)) for _, b in pairs]
    args = [a for a, _ in pairs] + [b for _, b in pairs]
    tile = pl.BlockSpec((tm, pn), lambda j, i: (i, j))
    if has_add:
        in_specs.append(tile)
        args.append(add)
    if has_extra:
        in_specs.append(tile)
        args.append(extra)
    return _pcall(
        body, name=name, grid=(n // pn, m // tm), in_specs=in_specs, out_specs=tile,
        out_shape=jax.ShapeDtypeStruct((m, n), out_dtype),
        scratch_shapes=[pltpu.VMEM((tm, pairs[0][0].shape[1]), BF16)] if stage0 else [],
    )(*args)


MM_REDUCE_ROWS = 1024
MM_GRAD_ROWS = 1024
MM_GRAD_COLS = 2048


def mm_tn(a, b, *, name, a_fn=None):
    m, ka = a.shape
    nb = b.shape[1]
    pa = _pick(ka, (MM_GRAD_ROWS, 512, 256, 128))
    pb = nb if nb <= MM_GRAD_COLS else _pick(nb, (MM_GRAD_COLS, 1024, 512, 256, 128))
    ns = _pick(pb, (MM_SUB, 256, 128))
    tmk = _pick(m, (MM_REDUCE_ROWS, 512, 256, 128, 64, 32, 16))

    def body(a_ref, b_ref, o_ref, lhs):
        k = pl.program_id(2)

        @pl.when(k == 0)
        def _():
            o_ref[...] = jnp.zeros_like(o_ref)

        av = a_ref[...]
        if a_fn is not None:
            av = a_fn(av)
        lhs[...] = av.astype(BF16)
        for n0 in range(0, pb, ns):
            o_ref[:, n0:n0 + ns] += _dot_tn(lhs[...], b_ref[:, n0:n0 + ns].astype(BF16))

    return _pcall(
        body, name=name,
        grid=(ka // pa, nb // pb, m // tmk),
        in_specs=[pl.BlockSpec((tmk, pa), lambda i, j, k: (k, i)),
                  pl.BlockSpec((tmk, pb), lambda i, j, k: (k, j))],
        out_specs=pl.BlockSpec((pa, pb), lambda i, j, k: (i, j)),
        out_shape=jax.ShapeDtypeStruct((ka, nb), F32),
        scratch_shapes=[pltpu.VMEM((tmk, pa), BF16)],
    )(a, b)


def _relu_sq(v):
    r = jnp.maximum(v, 0.0)
    return r * r


ROW_TILE = 512


def _row_specs(bsz, seq, width, ts):
    return pl.BlockSpec((None, ts, width), lambda b, i: (b, i, 0))


def _vec_spec(width):
    return pl.BlockSpec((1, width), lambda b, i: (0, 0))


def _mod_spec():
    return pl.BlockSpec((None, SUBLANES, D_MODEL), lambda b, i: (b, 0, 0))


def _rstd(v):
    return lax.rsqrt(jnp.mean(v * v, axis=-1, keepdims=True) + EPS)


def prenorm(x, w, mod, *, name):
    bsz, seq, d = x.shape
    ts = _pick(seq, (ROW_TILE, 256, 128))

    def body(x_ref, w_ref, mod_ref, h_ref):
        xv = x_ref[...]
        m = mod_ref[...]
        xh = xv * _rstd(xv)
        h_ref[...] = ((xh * w_ref[...]) * (1.0 + m[1:2]) + m[0:1]).astype(BF16)

    return _pcall(
        body, name=name, grid=(bsz, seq // ts),
        in_specs=[_row_specs(bsz, seq, d, ts), _vec_spec(d), _mod_spec()],
        out_specs=_row_specs(bsz, seq, d, ts),
        out_shape=jax.ShapeDtypeStruct((bsz, seq, d), BF16),
    )(x, w, mod)


def post1_pre2(x, out1, mod, post1, pre2, *, name):
    bsz, seq, d = x.shape
    ts = _pick(seq, (ROW_TILE, 256, 128))

    def body(x_ref, o_ref, mod_ref, p1_ref, p2_ref, x1_ref, h2_ref):
        m = mod_ref[...]
        ov = o_ref[...]
        x1 = x_ref[...] + m[2:3] * ((ov * _rstd(ov)) * p1_ref[...])
        x1_ref[...] = x1
        xh = x1 * _rstd(x1)
        h2_ref[...] = ((xh * p2_ref[...]) * (1.0 + m[4:5]) + m[3:4]).astype(BF16)

    return _pcall(
        body, name=name, grid=(bsz, seq // ts),
        in_specs=[_row_specs(bsz, seq, d, ts), _row_specs(bsz, seq, d, ts), _mod_spec(), _vec_spec(d), _vec_spec(d)],
        out_specs=[_row_specs(bsz, seq, d, ts), _row_specs(bsz, seq, d, ts)],
        out_shape=[jax.ShapeDtypeStruct((bsz, seq, d), F32), jax.ShapeDtypeStruct((bsz, seq, d), BF16)],
    )(x, out1, mod, post1, pre2)


def _acc_specs(d):
    per_batch = pl.BlockSpec((None, SUBLANES, d), lambda b, i: (b, 0, 0))
    glob = pl.BlockSpec((SUBLANES, d), lambda b, i: (0, 0))
    return per_batch, glob


def _accumulate(pb_ref, gl_ref, pb_rows, gl_rows, width):
    b, i = pl.program_id(0), pl.program_id(1)

    @pl.when(i == 0)
    def _():
        pb_ref[...] = jnp.zeros_like(pb_ref)

    @pl.when((b == 0) & (i == 0))
    def _():
        gl_ref[...] = jnp.zeros_like(gl_ref)

    pb_ref[...] += _stack_rows(pb_rows, width)
    gl_ref[...] += _stack_rows(gl_rows, width)


def _rms_bwd(d_n, n, r):
    return r * (d_n - n * jnp.mean(d_n * n, axis=-1, keepdims=True))


def final_bwd(x1, y2, target, mod, post2, *, name):
    bsz, seq, d = x1.shape
    ts = _pick(seq, (ROW_TILE, 256, 128))

    def body(x1_ref, y_ref, t_ref, mod_ref, p_ref, dx_ref, dy_ref, pb_ref, gl_ref):
        m = mod_ref[...]
        g2 = m[5:6]
        yv = y_ref[...]
        r = _rstd(yv)
        n = yv * r
        o = n * p_ref[...]
        diff = (x1_ref[...] + g2 * o) - t_ref[...]
        dx = diff * (1.0 / d)
        dx_ref[...] = dx
        d_o = dx * g2
        dy_ref[...] = _rms_bwd(d_o * p_ref[...], n, r).astype(BF16)
        _accumulate(pb_ref, gl_ref, [_colsum(dx * o)], [_colsum(d_o * n), _colsum(diff * diff) * (0.5 / d)], d)

    pb, gl = _acc_specs(d)
    rs = _row_specs(bsz, seq, d, ts)
    return _pcall(
        body, name=name, grid=(bsz, seq // ts),
        in_specs=[rs, rs, rs, _mod_spec(), _vec_spec(d)],
        out_specs=[rs, rs, pb, gl],
        out_shape=[jax.ShapeDtypeStruct((bsz, seq, d), F32), jax.ShapeDtypeStruct((bsz, seq, d), BF16),
                   jax.ShapeDtypeStruct((bsz, SUBLANES, d), F32), jax.ShapeDtypeStruct((SUBLANES, d), F32)],
    )(x1, y2, target, mod, post2)


def mid_bwd(d_h2, dx2, x1, out1, mod, pre2, post1, *, name):
    bsz, seq, d = x1.shape
    ts = _pick(seq, (ROW_TILE, 256, 128))

    def body(dh_ref, dx2_ref, x1_ref, o_ref, mod_ref, p2_ref, p1_ref, dx1_ref, do_ref, pb_ref, gl_ref):
        m = mod_ref[...]
        dh = dh_ref[...]
        x1 = x1_ref[...]
        r2 = _rstd(x1)
        xh = x1 * r2
        xw = xh * p2_ref[...]
        d_xw = dh * (1.0 + m[4:5])
        dx1 = dx2_ref[...] + _rms_bwd(d_xw * p2_ref[...], xh, r2)
        dx1_ref[...] = dx1
        ov = o_ref[...]
        r1 = _rstd(ov)
        n1 = ov * r1
        o1 = n1 * p1_ref[...]
        d_o1 = dx1 * m[2:3]
        do_ref[...] = _rms_bwd(d_o1 * p1_ref[...], n1, r1).astype(BF16)
        _accumulate(pb_ref, gl_ref, [_colsum(dh), _colsum(dh * xw), _colsum(dx1 * o1)],
                    [_colsum(d_xw * xh), _colsum(d_o1 * n1)], d)

    pb, gl = _acc_specs(d)
    rs = _row_specs(bsz, seq, d, ts)
    return _pcall(
        body, name=name, grid=(bsz, seq // ts),
        in_specs=[rs, rs, rs, rs, _mod_spec(), _vec_spec(d), _vec_spec(d)],
        out_specs=[rs, rs, pb, gl],
        out_shape=[jax.ShapeDtypeStruct((bsz, seq, d), F32), jax.ShapeDtypeStruct((bsz, seq, d), BF16),
                   jax.ShapeDtypeStruct((bsz, SUBLANES, d), F32), jax.ShapeDtypeStruct((SUBLANES, d), F32)],
    )(d_h2, dx2, x1, out1, mod, pre2, post1)


def first_bwd(d_h1, dx1, x, mod, pre1, *, name):
    bsz, seq, d = x.shape
    ts = _pick(seq, (ROW_TILE, 256, 128))

    def body(dh_ref, dx1_ref, x_ref, mod_ref, p_ref, gx_ref, pb_ref, gl_ref):
        m = mod_ref[...]
        dh = dh_ref[...]
        xv = x_ref[...]
        r = _rstd(xv)
        xh = xv * r
        xw = xh * p_ref[...]
        d_xw = dh * (1.0 + m[1:2])
        gx_ref[...] = dx1_ref[...] + _rms_bwd(d_xw * p_ref[...], xh, r)
        _accumulate(pb_ref, gl_ref, [_colsum(dh), _colsum(dh * xw)], [_colsum(d_xw * xh)], d)

    pb, gl = _acc_specs(d)
    rs = _row_specs(bsz, seq, d, ts)
    return _pcall(
        body, name=name, grid=(bsz, seq // ts),
        in_specs=[rs, rs, rs, _mod_spec(), _vec_spec(d)],
        out_specs=[rs, pb, gl],
        out_shape=[jax.ShapeDtypeStruct((bsz, seq, d), F32),
                   jax.ShapeDtypeStruct((bsz, SUBLANES, d), F32), jax.ShapeDtypeStruct((SUBLANES, d), F32)],
    )(d_h1, dx1, x, mod, pre1)


def merge_fwd(ya, yb, gates, b_gate, *, name):
    bsz, seq, d = ya.shape
    ts = _pick(seq, (ROW_TILE, 256, 128))

    def body(ya_ref, yb_ref, g_ref, b_ref, o_ref):
        g = _sigmoid(g_ref[...] + b_ref[...])
        o_ref[...] = (g[:, :d] * ya_ref[...] + g[:, d:] * yb_ref[...]).astype(BF16)

    rs = _row_specs(bsz, seq, d, ts)
    return _pcall(
        body, name=name, grid=(bsz, seq // ts),
        in_specs=[rs, rs, _row_specs(bsz, seq, 2 * d, ts), _vec_spec(2 * d)],
        out_specs=rs,
        out_shape=jax.ShapeDtypeStruct((bsz, seq, d), BF16),
    )(ya, yb, gates, b_gate)


def merge_bwd(d_merged, ya, yb, gates, b_gate, *, name):
    bsz, seq, d = ya.shape
    ts = _pick(seq, (ROW_TILE, 256, 128))

    def body(dm_ref, ya_ref, yb_ref, g_ref, b_ref, dya_ref, dyb_ref, dg_ref, gl_ref):
        b, i = pl.program_id(0), pl.program_id(1)
        g = _sigmoid(g_ref[...] + b_ref[...])
        dm = dm_ref[...]
        ga, gb = g[:, :d], g[:, d:]
        dya_ref[...] = (dm * ga).astype(BF16)
        dyb_ref[...] = (dm * gb).astype(BF16)
        dg = jnp.concatenate([dm * ya_ref[...] * ga * (1.0 - ga), dm * yb_ref[...] * gb * (1.0 - gb)], axis=1)
        dg_ref[...] = dg.astype(BF16)

        @pl.when((b == 0) & (i == 0))
        def _():
            gl_ref[...] = jnp.zeros_like(gl_ref)

        gl_ref[...] += _stack_rows([_colsum(dg)], 2 * d)

    rs = _row_specs(bsz, seq, d, ts)
    rs2 = _row_specs(bsz, seq, 2 * d, ts)
    return _pcall(
        body, name=name, grid=(bsz, seq // ts),
        in_specs=[rs, rs, rs, rs2, _vec_spec(2 * d)],
        out_specs=[rs, rs, rs2, pl.BlockSpec((SUBLANES, 2 * d), lambda b, i: (0, 0))],
        out_shape=[jax.ShapeDtypeStruct((bsz, seq, d), BF16), jax.ShapeDtypeStruct((bsz, seq, d), BF16),
                   jax.ShapeDtypeStruct((bsz, seq, 2 * d), BF16), jax.ShapeDtypeStruct((SUBLANES, 2 * d), F32)],
    )(d_merged, ya, yb, gates, b_gate)


LRU_TILE = 256
N_LRU_BLOCKS = D_MODEL // LANES


def _block_mm(v, w_ref, transpose=False):
    vb = v.astype(BF16)
    outs = []
    for j in range(N_LRU_BLOCKS):
        blk = vb[:, LANES * j:LANES * (j + 1)]
        outs.append(_dot_nt(blk, w_ref[j]) if transpose else _dot(blk, w_ref[j]))
    return jnp.concatenate(outs, axis=1)


def _lru_gates(xc, wa_ref, ba, wx_ref, bx, sp):
    r = _sigmoid(_block_mm(xc, wa_ref) + ba)
    i = _sigmoid(_block_mm(xc, wx_ref) + bx)
    la = (-LRU_C * r) * sp
    a = jnp.exp(la)
    sq = jnp.sqrt(_neg_expm1(2.0 * la))
    return r, i, a, sq


def _prev8_spec(width, col_block, tile_rows):
    per = tile_rows // SUBLANES
    return pl.BlockSpec((None, SUBLANES, width), lambda b, i: (b, jnp.maximum(i * per - 1, 0), col_block))


def lru_fwd(pm, cw, cb, wa, ba, wx, bx, lam, *, name):
    bsz, seq, _ = pm.shape
    d = D_MODEL
    ts = _pick(seq, (LRU_TILE, 128))

    def body(lx_ref, lxp_ref, lg_ref, cw_ref, cb_ref, wa_ref, ba_ref, wx_ref, bx_ref, lam_ref,
             h_ref, pa_ref, hc, a_s, u_s):
        i = pl.program_id(1)

        @pl.when(i == 0)
        def _():
            hc[...] = jnp.zeros_like(hc)

        lx = lx_ref[...]
        prev8 = jnp.where(i == 0, 0.0, lxp_ref[...])
        xc = _conv4(lx, prev8, cw_ref[...], cb_ref[...])
        sp = _softplus(-lam_ref[...])
        r, ig, a, sq = _lru_gates(xc, wa_ref, ba_ref[...], wx_ref, bx_ref[...], sp)
        a_s[...] = a
        u_s[...] = sq * (ig * xc)

        def step(g, h):
            r0 = pl.multiple_of(g * SUBLANES, SUBLANES)
            a8 = a_s[pl.ds(r0, SUBLANES), :]
            u8 = u_s[pl.ds(r0, SUBLANES), :]
            rows = []
            for j in range(SUBLANES):
                h = a8[j:j + 1] * h + u8[j:j + 1]
                rows.append(h)
            h_ref[pl.ds(r0, SUBLANES), :] = jnp.concatenate(rows, axis=0)
            return h

        hc[...] = lax.fori_loop(0, ts // SUBLANES, step, hc[...])
        pa_ref[...] = (h_ref[...] * _gelu(lg_ref[...])).astype(BF16)

    vec = _vec_spec(d)
    wspec = pl.BlockSpec((N_LRU_BLOCKS, LANES, LANES), lambda b, i: (0, 0, 0))
    return _pcall(
        body, name=name, grid=(bsz, seq // ts),
        in_specs=[pl.BlockSpec((None, ts, d), lambda b, i: (b, i, 0)), _prev8_spec(d, 0, ts),
                  pl.BlockSpec((None, ts, d), lambda b, i: (b, i, 1)),
                  pl.BlockSpec((4, d), lambda b, i: (0, 0)), vec, wspec, vec, wspec, vec, vec],
        out_specs=[_row_specs(bsz, seq, d, ts), _row_specs(bsz, seq, d, ts)],
        out_shape=[jax.ShapeDtypeStruct((bsz, seq, d), F32), jax.ShapeDtypeStruct((bsz, seq, d), BF16)],
        scratch_shapes=[pltpu.VMEM((1, d), F32), pltpu.VMEM((ts, d), F32), pltpu.VMEM((ts, d), F32)],
    )(pm, pm, pm, cw, cb, wa, ba, wx, bx, lam)


def lru_bwd(pm, h, d_pa, cw, cb, wa, ba, wx, bx, lam, *, name):
    bsz, seq, _ = pm.shape
    d = D_MODEL
    ts = _pick(seq, (LRU_TILE, 128))
    nt = seq // ts
    per = ts // SUBLANES

    def rev(i):
        return nt - 1 - i

    def body(lx_ref, lxp_ref, lg_ref, h_ref, hp_ref, dpa_ref, cw_ref, cb_ref, wa_ref, ba_ref, wx_ref, bx_ref,
             lam_ref, dl_ref, dwa_ref, dwx_ref, rows_ref, carry, dxc_next, a_s, dh_s, acc_s):
        b, i = pl.program_id(0), pl.program_id(1)
        t = rev(i)

        @pl.when(i == 0)
        def _():
            carry[...] = jnp.zeros_like(carry)
            dxc_next[...] = jnp.zeros_like(dxc_next)

        @pl.when((b == 0) & (i == 0))
        def _():
            dwa_ref[...] = jnp.zeros_like(dwa_ref)
            dwx_ref[...] = jnp.zeros_like(dwx_ref)
            rows_ref[...] = jnp.zeros_like(rows_ref)

        lx = lx_ref[...]
        lg = lg_ref[...]
        prev8 = jnp.where(t == 0, 0.0, lxp_ref[...])
        cwv = cw_ref[...]
        xc = _conv4(lx, prev8, cwv, cb_ref[...])
        lam_v = lam_ref[...]
        sp = _softplus(-lam_v)
        r, ig, a, sq = _lru_gates(xc, wa_ref, ba_ref[...], wx_ref, bx_ref[...], sp)
        hv = h_ref[...]
        d_pa = dpa_ref[...]
        a_s[...] = a
        dh_s[...] = d_pa * _gelu(lg)

        def step(g, c):
            r0 = pl.multiple_of((per - 1 - g) * SUBLANES, SUBLANES)
            a8 = a_s[pl.ds(r0, SUBLANES), :]
            d8 = dh_s[pl.ds(r0, SUBLANES), :]
            rows = [None] * SUBLANES
            for j in range(SUBLANES - 1, -1, -1):
                acc = d8[j:j + 1] + c
                rows[j] = acc
                c = a8[j:j + 1] * acc
            acc_s[pl.ds(r0, SUBLANES), :] = jnp.concatenate(rows, axis=0)
            return c

        carry[...] = lax.fori_loop(0, per, step, carry[...])
        d_u = acc_s[...]
        hprev8 = jnp.where(t == 0, 0.0, hp_ref[...])
        d_a = d_u * _shift_down(hv, hprev8, 1)
        d_sq = d_u * (ig * xc)
        d_i = d_u * (sq * xc)
        d_xc = d_u * (sq * ig)
        d_la = d_a * a - d_sq * (a * a) / sq
        d_pre_r = (d_la * (-LRU_C * sp)) * (r * (1.0 - r))
        d_pre_i = d_i * (ig * (1.0 - ig))
        d_xc = d_xc + _block_mm(d_pre_r, wa_ref, transpose=True) + _block_mm(d_pre_i, wx_ref, transpose=True)
        xcb = xc.astype(BF16)
        drb = d_pre_r.astype(BF16)
        dib = d_pre_i.astype(BF16)
        for j in range(N_LRU_BLOCKS):
            sl = slice(LANES * j, LANES * (j + 1))
            dwa_ref[j] += _dot_tn(xcb[:, sl], drb[:, sl])
            dwx_ref[j] += _dot_tn(xcb[:, sl], dib[:, sl])
        d_lx, d_cw, d_cb = _conv4_bwd(d_xc, dxc_next[...], lx, cwv)
        dxc_next[...] = d_xc[0:SUBLANES]
        d_lam = _colsum(d_la * (-LRU_C * r)) * (-_sigmoid(-lam_v))
        rows_ref[...] += _stack_rows([_colsum(d_pre_r), _colsum(d_pre_i), d_lam, d_cb] + d_cw, d)
        dl_ref[:, :d] = d_lx.astype(BF16)
        dl_ref[:, d:] = (d_pa * hv * _gelu_grad(lg)).astype(BF16)

    vec = _vec_spec(d)
    wspec = pl.BlockSpec((N_LRU_BLOCKS, LANES, LANES), lambda b, i: (0, 0, 0))
    tile = lambda col: pl.BlockSpec((None, ts, d), lambda b, i: (b, rev(i), col))
    prev8 = lambda col: pl.BlockSpec((None, SUBLANES, d), lambda b, i: (b, jnp.maximum(rev(i) * per - 1, 0), col))
    return _pcall(
        body, name=name, grid=(bsz, nt),
        in_specs=[tile(0), prev8(0), tile(1), tile(0), prev8(0), tile(0),
                  pl.BlockSpec((4, d), lambda b, i: (0, 0)), vec, wspec, vec, wspec, vec, vec],
        out_specs=[pl.BlockSpec((None, ts, 2 * d), lambda b, i: (b, rev(i), 0)), wspec, wspec,
                   pl.BlockSpec((SUBLANES, d), lambda b, i: (0, 0))],
        out_shape=[jax.ShapeDtypeStruct((bsz, seq, 2 * d), BF16),
                   jax.ShapeDtypeStruct((N_LRU_BLOCKS, LANES, LANES), F32),
                   jax.ShapeDtypeStruct((N_LRU_BLOCKS, LANES, LANES), F32),
                   jax.ShapeDtypeStruct((SUBLANES, d), F32)],
        scratch_shapes=[pltpu.VMEM((1, d), F32), pltpu.VMEM((SUBLANES, d), F32),
                        pltpu.VMEM((ts, d), F32), pltpu.VMEM((ts, d), F32), pltpu.VMEM((ts, d), F32)],
    )(pm, pm, pm, h, h, d_pa, cw, cb, wa, ba, wx, bx, lam)


L = SSD_CHUNK
N_PAIRS = SSD_HEADS // 2


def _ssd_common(xbc, prev8, dt_raw, cw, cb, dtb, alog):
    conv = _conv4(xbc, prev8, cw, cb)
    sg = _sigmoid(conv)
    xa = conv * sg
    dtv = _softplus(dt_raw + dtb)
    a_neg = -jnp.exp(alog)
    rowi = lax.broadcasted_iota(jnp.int32, (L, L), 0)
    coli = lax.broadcasted_iota(jnp.int32, (L, L), 1)
    tril = (rowi >= coli).astype(F32)
    cs = _dot(tril, dtv * a_neg, precision=HI)
    return conv, sg, xa, dtv, a_neg, cs, rowi, coli


def _head_masks():
    lane = lax.broadcasted_iota(jnp.int32, (L, LANES), 1)
    return lane < SSD_HEAD_DIM


def _spread(v, p, first):
    return jnp.where(first[:v.shape[0]], v[:, 2 * p:2 * p + 1], v[:, 2 * p + 1:2 * p + 2])


def _place_head_sums(acc, z, p, first, lane1):
    rows = z.shape[0]
    s0 = jnp.sum(jnp.where(first[:rows], z, 0.0), axis=1, keepdims=True)
    s1 = jnp.sum(jnp.where(first[:rows], 0.0, z), axis=1, keepdims=True)
    lane = lane1[:rows]
    return acc + jnp.where(lane == 2 * p, s0, 0.0) + jnp.where(lane == 2 * p + 1, s1, 0.0)


def _stack_heads(v, first):
    return jnp.concatenate([jnp.where(first, v, 0.0), jnp.where(first, 0.0, v)], axis=0).astype(BF16)


def ssd_fwd(pm, dtr, cw, cb, dtb, alog, d_lanes, nw, *, name):
    bsz, seq, _ = pm.shape
    nc = seq // L
    inner, cdim = SSD_INNER, SSD_CONV_DIM

    def body(xbc_ref, xp_ref, z_ref, dt_ref, cw_ref, cb_ref, dtb_ref, alog_ref, dl_ref, nw_ref,
             y_ref, yn_ref, st_ref, state):
        i = pl.program_id(1)

        @pl.when(i == 0)
        def _():
            state[...] = jnp.zeros_like(state)

        prev8 = jnp.where(i == 0, 0.0, xp_ref[...])
        _, _, xa, dtv, _, cs, rowi, coli = _ssd_common(
            xbc_ref[...], prev8, dt_ref[...], cw_ref[...], cb_ref[...], dtb_ref[...], alog_ref[...])
        cst = cs.T
        causal = rowi >= coli
        first = _head_masks()
        for g in range(SSD_GROUPS):
            bg = xa[:, inner + SSD_STATE * g:inner + SSD_STATE * (g + 1)].astype(BF16)
            cg = xa[:, inner + SSD_GROUPS * SSD_STATE + SSD_STATE * g:
                    inner + SSD_GROUPS * SSD_STATE + SSD_STATE * (g + 1)].astype(BF16)
            cbm = _dot_nt(cg, bg)
            for pp in range(2):
                p = 2 * g + pp
                sl = slice(LANES * p, LANES * (p + 1))
                ms = []
                for hh in (2 * p, 2 * p + 1):
                    seg = cs[:, hh:hh + 1] - cst[hh:hh + 1, :]
                    ms.append((cbm * jnp.exp(jnp.where(causal, seg, -jnp.inf))).astype(BF16))
                xsp = xa[:, sl]
                cs_p = _spread(cs, p, first)
                cs_last = cs_p[L - 1:L]
                xp = xsp * _spread(dtv, p, first)
                y_diag = _dot(jnp.concatenate(ms, axis=1), _stack_heads(xp, first))
                st = state[p]
                st_ref[p] = st
                y_off = _dot(cg, st.astype(BF16)) * jnp.exp(cs_p)
                y_ref[:, sl] = y_diag + y_off + dl_ref[:, sl] * xsp
                state[p] = st * jnp.exp(cs_last) + _dot_tn(bg, (xp * jnp.exp(cs_last - cs_p)).astype(BF16))
        zv = z_ref[...]
        yz = y_ref[...] * (zv * _sigmoid(zv))
        gw = inner // SSD_GROUPS
        for g in range(SSD_GROUPS):
            sl = slice(gw * g, gw * (g + 1))
            seg = yz[:, sl]
            yn_ref[:, sl] = ((seg * _rstd(seg)) * nw_ref[:, sl]).astype(BF16)

    cvec = lambda w: pl.BlockSpec((1, w), lambda b, i: (0, 0))
    return _pcall(
        body, name=name, grid=(bsz, nc),
        in_specs=[pl.BlockSpec((None, L, cdim), lambda b, i: (b, i, 1)), _prev8_spec(cdim, 1, L),
                  pl.BlockSpec((None, L, inner), lambda b, i: (b, i, 1)),
                  pl.BlockSpec((None, L, LANES), lambda b, i: (b, i, 0)),
                  pl.BlockSpec((4, cdim), lambda b, i: (0, 0)), cvec(cdim), cvec(LANES), cvec(LANES),
                  cvec(inner), cvec(inner)],
        out_specs=[pl.BlockSpec((None, L, inner), lambda b, i: (b, i, 0)),
                   pl.BlockSpec((None, L, inner), lambda b, i: (b, i, 0)),
                   pl.BlockSpec((None, None, N_PAIRS, SSD_STATE, LANES), lambda b, i: (b, i, 0, 0, 0))],
        out_shape=[jax.ShapeDtypeStruct((bsz, seq, inner), F32), jax.ShapeDtypeStruct((bsz, seq, inner), BF16),
                   jax.ShapeDtypeStruct((bsz, nc, N_PAIRS, SSD_STATE, LANES), F32)],
        scratch_shapes=[pltpu.VMEM((N_PAIRS, SSD_STATE, LANES), F32)],
    )(pm, pm, pm, dtr, cw, cb, dtb, alog, d_lanes, nw)


def ssd_bwd(pm, dtr, y, states, d_yn, cw, cb, dtb, alog, d_lanes, nw, *, name):
    bsz, seq, _ = pm.shape
    nc = seq // L
    inner, cdim = SSD_INNER, SSD_CONV_DIM
    per = L // SUBLANES

    def rev(i):
        return nc - 1 - i

    def body(xbc_ref, xp_ref, z_ref, dt_ref, y_ref, st_ref, dyn_ref, cw_ref, cb_ref, dtb_ref, alog_ref,
             dl_ref, nw_ref, ds_ref, ddt_ref, r4_ref, r2_ref, r1_ref,
             dstate, dconv_next, dxs_s, dbc_s):
        b, i = pl.program_id(0), pl.program_id(1)
        t = rev(i)

        @pl.when(i == 0)
        def _():
            dstate[...] = jnp.zeros_like(dstate)
            dconv_next[...] = jnp.zeros_like(dconv_next)

        @pl.when((b == 0) & (i == 0))
        def _():
            r4_ref[...] = jnp.zeros_like(r4_ref)
            r2_ref[...] = jnp.zeros_like(r2_ref)
            r1_ref[...] = jnp.zeros_like(r1_ref)

        xbc = xbc_ref[...]
        prev8 = jnp.where(t == 0, 0.0, xp_ref[...])
        cwv = cw_ref[...]
        dt_in = dt_ref[...] + dtb_ref[...]
        conv, sg, xa, dtv, a_neg, cs, rowi, coli = _ssd_common(
            xbc, prev8, dt_ref[...], cwv, cb_ref[...], dtb_ref[...], alog_ref[...])
        cst = cs.T
        causal = rowi >= coli
        anti = coli >= rowi
        first = _head_masks()
        lane1 = lax.broadcasted_iota(jnp.int32, (L, LANES), 1)

        yv = y_ref[...]
        zv = z_ref[...]
        sz = _sigmoid(zv)
        zs = zv * sz
        yz = yv * zs
        dyn = dyn_ref[...]
        gw = inner // SSD_GROUPS
        d_yz_parts, d_nw_parts = [], []
        for g in range(SSD_GROUPS):
            sl = slice(gw * g, gw * (g + 1))
            seg = yz[:, sl]
            r = _rstd(seg)
            n = seg * r
            d_nw_parts.append(_colsum(dyn[:, sl] * n))
            d_yz_parts.append(_rms_bwd(dyn[:, sl] * nw_ref[:, sl], n, r))
        d_yz = jnp.concatenate(d_yz_parts, axis=1)
        d_y = d_yz * zs
        ds_ref[:, :inner] = (d_yz * yv * (sz * (1.0 + zv * (1.0 - sz)))).astype(BF16)

        a1 = jnp.zeros((L, LANES), F32)
        a2 = jnp.zeros((L, LANES), F32)
        xs_dxt = jnp.zeros((L, LANES), F32)
        c0 = jnp.zeros((1, LANES), F32)
        d_dl = jnp.zeros((1, LANES), F32)
        for g in range(SSD_GROUPS):
            bsl = slice(inner + SSD_STATE * g, inner + SSD_STATE * (g + 1))
            csl = slice(inner + SSD_GROUPS * SSD_STATE + SSD_STATE * g,
                        inner + SSD_GROUPS * SSD_STATE + SSD_STATE * (g + 1))
            bg = xa[:, bsl].astype(BF16)
            cg = xa[:, csl].astype(BF16)
            cbm = _dot_nt(cg, bg)
            cbt = _dot_nt(bg, cg)
            d_cb = jnp.zeros((L, L), F32)
            d_bg = jnp.zeros((L, SSD_STATE), F32)
            d_cg = jnp.zeros((L, SSD_STATE), F32)
            for pp in range(2):
                p = 2 * g + pp
                sl = slice(LANES * p, LANES * (p + 1))
                xsp = xa[:, sl]
                dt_p = _spread(dtv, p, first)
                cs_p = _spread(cs, p, first)
                cs_last = cs_p[L - 1:L]
                e_p = jnp.exp(cs_p)
                w_p = jnp.exp(cs_last - cs_p)
                e_last = jnp.exp(cs_last)
                xp = xsp * dt_p
                xpb = xp.astype(BF16)
                dyp = d_y[:, sl]
                dypb = dyp.astype(BF16)
                dy_heads = (jnp.where(first, dyp, 0.0).astype(BF16), jnp.where(first, 0.0, dyp).astype(BF16))
                x_heads = (jnp.where(first, xp, 0.0).astype(BF16), jnp.where(first, 0.0, xp).astype(BF16))
                mts = []
                for k, hh in enumerate((2 * p, 2 * p + 1)):
                    col = cs[:, hh:hh + 1]
                    row = cst[hh:hh + 1, :]
                    dec = jnp.exp(jnp.where(causal, col - row, -jnp.inf))
                    dec_t = jnp.exp(jnp.where(anti, row - col, -jnp.inf))
                    gd = _dot_nt(dy_heads[k], xpb) * dec
                    d_cb = d_cb + gd
                    mt = cbt * dec_t
                    qd = gd * cbm - _dot_nt(x_heads[k], dypb) * mt
                    a1 = a1 + jnp.where(lane1 == hh, jnp.sum(qd, axis=1, keepdims=True), 0.0)
                    mts.append(mt.astype(BF16))
                dst = dstate[p]
                dstb = dst.astype(BF16)
                st = st_ref[p]
                stb = st.astype(BF16)
                dye = (dyp * e_p).astype(BF16)
                xw = (xp * w_p).astype(BF16)
                dx_off = w_p * _dot(bg, dstb)
                d_xp = _dot(jnp.concatenate(mts, axis=1), jnp.concatenate(dy_heads, axis=0)) + dx_off
                dxs_s[:, sl] = d_xp * dt_p + dyp * dl_ref[:, sl]
                a1 = _place_head_sums(a1, dyp * (_dot(cg, stb) * e_p), p, first, lane1)
                a2 = _place_head_sums(a2, xp * dx_off, p, first, lane1)
                xs_dxt = _place_head_sums(xs_dxt, d_xp * xsp, p, first, lane1)
                c0 = _place_head_sums(c0, _colsum(dst * st) * e_last, p, first, lane1)
                d_dl = _place_head_sums(d_dl, _colsum(dyp * xsp), p, first, lane1)
                d_cg = d_cg + _dot_nt(dye, stb)
                d_bg = d_bg + _dot_nt(xw, dstb)
                dstate[p] = dst * e_last + _dot_tn(cg, dye)
            d_cbb = d_cb.astype(BF16)
            dbc_s[:, SSD_STATE * g:SSD_STATE * (g + 1)] = d_bg + _dot_tn(d_cbb, cg)
            dbc_s[:, SSD_GROUPS * SSD_STATE + SSD_STATE * g:SSD_GROUPS * SSD_STATE + SSD_STATE * (g + 1)] = (
                d_cg + _dot(d_cbb, bg))

        d_da = (_dot(anti.astype(F32), a1, precision=HI) + _dot((rowi > coli).astype(F32), a2, precision=HI) + c0)
        d_dt = d_da * a_neg + xs_dxt
        d_alog = _colsum(d_da * dtv) * a_neg
        d_dtr = jnp.where(lane1 < SSD_HEADS, d_dt * _sigmoid(dt_in), 0.0)
        ddt_ref[...] = d_dtr.astype(BF16)
        d_xa = jnp.concatenate([dxs_s[...], dbc_s[...]], axis=1)
        d_conv = d_xa * (sg * (1.0 + conv * (1.0 - sg)))
        d_xbc, d_cw, d_cbias = _conv4_bwd(d_conv, dconv_next[...], xbc, cwv)
        dconv_next[...] = d_conv[0:SUBLANES]
        ds_ref[:, inner:] = d_xbc.astype(BF16)
        r4_ref[...] += _stack_rows([d_cbias] + d_cw, cdim)
        r2_ref[...] += _stack_rows([jnp.concatenate(d_nw_parts, axis=1)], inner)
        r1_ref[...] += _stack_rows([_colsum(d_dtr), d_alog, d_dl], LANES)

    cvec = lambda w: pl.BlockSpec((1, w), lambda b, i: (0, 0))
    return _pcall(
        body, name=name, grid=(bsz, nc),
        in_specs=[pl.BlockSpec((None, L, cdim), lambda b, i: (b, rev(i), 1)),
                  pl.BlockSpec((None, SUBLANES, cdim), lambda b, i: (b, jnp.maximum(rev(i) * per - 1, 0), 1)),
                  pl.BlockSpec((None, L, inner), lambda b, i: (b, rev(i), 1)),
                  pl.BlockSpec((None, L, LANES), lambda b, i: (b, rev(i), 0)),
                  pl.BlockSpec((None, L, inner), lambda b, i: (b, rev(i), 0)),
                  pl.BlockSpec((None, None, N_PAIRS, SSD_STATE, LANES), lambda b, i: (b, rev(i), 0, 0, 0)),
                  pl.BlockSpec((None, L, inner), lambda b, i: (b, rev(i), 0)),
                  pl.BlockSpec((4, cdim), lambda b, i: (0, 0)), cvec(cdim), cvec(LANES), cvec(LANES),
                  cvec(inner), cvec(inner)],
        out_specs=[pl.BlockSpec((None, L, inner + cdim), lambda b, i: (b, rev(i), 0)),
                   pl.BlockSpec((None, L, LANES), lambda b, i: (b, rev(i), 0)),
                   pl.BlockSpec((SUBLANES, cdim), lambda b, i: (0, 0)),
                   pl.BlockSpec((SUBLANES, inner), lambda b, i: (0, 0)),
                   pl.BlockSpec((SUBLANES, LANES), lambda b, i: (0, 0))],
        out_shape=[jax.ShapeDtypeStruct((bsz, seq, inner + cdim), BF16),
                   jax.ShapeDtypeStruct((bsz, seq, LANES), BF16),
                   jax.ShapeDtypeStruct((SUBLANES, cdim), F32),
                   jax.ShapeDtypeStruct((SUBLANES, inner), F32),
                   jax.ShapeDtypeStruct((SUBLANES, LANES), F32)],
        scratch_shapes=[pltpu.VMEM((N_PAIRS, SSD_STATE, LANES), F32), pltpu.VMEM((SUBLANES, cdim), F32),
                        pltpu.VMEM((L, inner), F32), pltpu.VMEM((L, 2 * SSD_GROUPS * SSD_STATE), F32)],
    )(pm, pm, pm, dtr, y, states, d_yn, cw, cb, dtb, alog, d_lanes, nw)


def _lru_block_weights(w):
    w = w.reshape(N_LRU_BLOCKS, 2, LRU_HEAD_DIM, LRU_HEAD_DIM)
    z = jnp.zeros((N_LRU_BLOCKS, LRU_HEAD_DIM, LRU_HEAD_DIM), w.dtype)
    top = jnp.concatenate([w[:, 0], z], axis=2)
    bot = jnp.concatenate([z, w[:, 1]], axis=2)
    return jnp.concatenate([top, bot], axis=1).astype(BF16)


def _lru_block_grads(g):
    h = LRU_HEAD_DIM
    return jnp.stack([g[:, :h, :h], g[:, h:, h:]], axis=1).reshape(LRU_HEADS, h, h)


def _pad_lanes(v, width=LANES):
    return jnp.pad(v, ((0, 0), (0, width - v.shape[1])))


def local_step(x, target, mod, big, small):
    bsz, seq, d = x.shape
    t = bsz * seq
    flat = lambda v: v.reshape(t, v.shape[-1])
    unflat = lambda v: v.reshape(bsz, seq, v.shape[-1])

    wa_b = _lru_block_weights(small["lru_wa"])
    wx_b = _lru_block_weights(small["lru_wx"])
    dtb = _pad_lanes(small["ssd_dt_bias"])
    alog = _pad_lanes(small["ssd_a_log"])
    d_lanes = jnp.repeat(small["ssd_d"], SSD_HEAD_DIM, axis=1)

    lru_cols = 2 * D_MODEL
    wt = {"lru": big["w_main"][:, :lru_cols].T, "ssd": big["w_main"][:, lru_cols:].T, "gates": big["w_gates"].T,
          "dt": big["w_dt"].T}
    for n in ("w_pa", "w_pb", "w_out", "w_ff1", "w_ff2"):
        wt[n] = big[n].T

    h1 = prenorm(x, small["pre_norm1"], mod, name="prenorm1")
    h1f = flat(h1)
    pm = unflat(mm_nn([(h1f, big["w_main"])], name="in_proj_main"))
    gates = unflat(mm_nn([(h1f, big["w_gates"])], name="in_proj_gates"))
    dtr = unflat(mm_nn([(h1f, big["w_dt"])], name="in_proj_dt"))
    lru_args = (small["lru_conv_w"], small["lru_conv_b"], wa_b, small["lru_ba"], wx_b, small["lru_bx"],
                small["lru_lambda"])
    h_lru, pa_in = lru_fwd(pm, *lru_args, name="lru_fwd")
    ssd_args = (small["ssd_conv_w"], small["ssd_conv_b"], dtb, alog, d_lanes, small["ssd_norm_w"])
    y_ssd, ynorm, states = ssd_fwd(pm, dtr, *ssd_args, name="ssd_fwd")
    ya = unflat(mm_nn([(flat(pa_in), big["w_pa"])], name="proj_a"))
    yb = unflat(mm_nn([(flat(ynorm), big["w_pb"])], name="proj_b"))
    merged = merge_fwd(ya, yb, gates, small["b_gate"], name="merge_fwd")
    out1 = unflat(mm_nn([(flat(merged), big["w_out"])], name="proj_out"))
    x1, h2 = post1_pre2(x, out1, mod, small["post_norm1"], small["pre_norm2"], name="post1_pre2")
    f = mm_nn([(flat(h2), big["w_ff1"])], name="ff1")
    y2 = unflat(mm_nn([(f, big["w_ff2"])], a_fn=_relu_sq, name="ff2"))

    dx2, d_y2, pb_a, gl_a = final_bwd(x1, y2, target, mod, small["post_norm2"], name="final_bwd")
    d_y2f = flat(d_y2)
    d_f = mm_nn([(d_y2f, wt["w_ff2"])], out_dtype=BF16, extra=f,
                epi=lambda r, fv: r * (2.0 * jnp.maximum(fv, 0.0)), name="ff2_dx")
    g_ff2 = mm_tn(f, d_y2f, a_fn=_relu_sq, name="ff2_dw")
    d_h2 = unflat(mm_nn([(d_f, wt["w_ff1"])], name="ff1_dx"))
    g_ff1 = mm_tn(flat(h2), d_f, name="ff1_dw")
    dx1, d_out1, pb_b, gl_b = mid_bwd(d_h2, dx2, x1, out1, mod, small["pre_norm2"], small["post_norm1"],
                                      name="mid_bwd")
    d_out1f = flat(d_out1)
    d_merged = unflat(mm_nn([(d_out1f, wt["w_out"])], name="out_dx"))
    g_out = mm_tn(flat(merged), d_out1f, name="out_dw")
    d_ya, d_yb, d_gates, gl_c = merge_bwd(d_merged, ya, yb, gates, small["b_gate"], name="merge_bwd")
    d_pa = unflat(mm_nn([(flat(d_ya), wt["w_pa"])], name="pa_dx"))
    g_pa = mm_tn(flat(pa_in), flat(d_ya), name="pa_dw")
    d_yn = unflat(mm_nn([(flat(d_yb), wt["w_pb"])], name="pb_dx"))
    g_pb = mm_tn(flat(ynorm), flat(d_yb), name="pb_dw")
    d_l, g_wa_b, g_wx_b, lru_rows = lru_bwd(pm, h_lru, d_pa, *lru_args, name="lru_bwd")
    d_s, d_dt, r4, r2, r1 = ssd_bwd(pm, dtr, y_ssd, states, d_yn, *ssd_args, name="ssd_bwd")
    d_lf, d_sf, d_gf, d_dtf = flat(d_l), flat(d_s), flat(d_gates), flat(d_dt)
    d_h1 = mm_nn([(d_lf, wt["lru"]), (d_gf, wt["gates"]), (d_dtf, wt["dt"])], name="in_dx_lru_gates")
    d_h1 = mm_nn([(d_sf, wt["ssd"])], add=d_h1, name="in_dx_ssd")
    g_in = jnp.concatenate([
        mm_tn(h1f, d_lf, name="in_dw_lru"), mm_tn(h1f, d_sf, name="in_dw_ssd"),
        mm_tn(h1f, d_dtf, name="in_dw_dt")[:, :SSD_HEADS], mm_tn(h1f, d_gf, name="in_dw_gates")], axis=1)
    grad_x, pb_c, gl_d = first_bwd(unflat(d_h1), dx1, x, mod, small["pre_norm1"], name="first_bwd")

    d_mod = jnp.stack([pb_c[:, 0], pb_c[:, 1], pb_b[:, 2], pb_b[:, 0], pb_b[:, 1], pb_a[:, 0]], axis=1)
    loss_cols = gl_a[1:2]
    big_grads = {"w_in": g_in, "w_pa": g_pa, "w_pb": g_pb, "w_out": g_out, "w_ff1": g_ff1, "w_ff2": g_ff2}
    nh = SSD_HEADS
    small_grads = {
        "pre_norm1": gl_d[0:1], "post_norm1": gl_b[1:2], "b_gate": gl_c[0:1],
        "lru_conv_w": lru_rows[4:8], "lru_conv_b": lru_rows[3:4],
        "lru_wa": _lru_block_grads(g_wa_b), "lru_ba": lru_rows[0:1],
        "lru_wx": _lru_block_grads(g_wx_b), "lru_bx": lru_rows[1:2], "lru_lambda": lru_rows[2:3],
        "ssd_conv_w": r4[1:5], "ssd_conv_b": r4[0:1],
        "ssd_dt_bias": r1[0:1, :nh], "ssd_a_log": r1[1:2, :nh], "ssd_d": r1[2:3, :nh],
        "ssd_norm_w": r2[0:1], "pre_norm2": gl_b[0:1], "post_norm2": gl_a[0:1],
    }
    return loss_cols, grad_x, d_mod, big_grads, small_grads


ANY = pl.BlockSpec(memory_space=pl.ANY)


def _position():
    return lax.axis_index("x"), lax.axis_index("y"), lax.axis_index("c")


def _other_chips(x, y):
    return [(1 - x, y), (x, 1 - y), (1 - x, 1 - y)]


def allgather8(v, *, name):
    m_per, n = v.shape

    def body(x_ref, out_ref, send_sems, recv_sems, local_sem):
        x, y, c = _position()
        me, sibling = (x, y, c), (x, y, 1 - c)
        chips = _other_chips(x, y)

        def rows(px, py, pc):
            return out_ref.at[pl.ds((4 * px + 2 * py + pc) * m_per, m_per), :]

        def copy(k, block, to, src=None):
            return pltpu.make_async_remote_copy(
                src_ref=rows(*block) if src is None else src, dst_ref=rows(*block),
                send_sem=send_sems.at[k], recv_sem=recv_sems.at[k], device_id=to, device_id_type=MESH)

        mine = pltpu.make_async_copy(x_ref, rows(*me), local_sem)
        mine.start()
        first = [copy(0, me, sibling, src=x_ref)]
        first += [copy(1 + j, me, (*chip, c), src=x_ref) for j, chip in enumerate(chips)]
        for cp in first:
            cp.start()
        passed = [copy(4 + j, (*chip, c), sibling) for j, chip in enumerate(chips)]
        for j, chip in enumerate(chips):
            copy(1 + j, (*chip, c), me).wait_recv()
            passed[j].start()
        copy(0, sibling, me).wait_recv()
        for j, chip in enumerate(chips):
            copy(4 + j, (*chip, 1 - c), me).wait_recv()
        for cp in first + passed:
            cp.wait_send()
        mine.wait()

    return _pcall(
        body, name=name,
        out_shape=jax.ShapeDtypeStruct((N_DEV * m_per, n), v.dtype),
        in_specs=[pl.BlockSpec(memory_space=pltpu.VMEM)],
        out_specs=pl.BlockSpec(memory_space=pltpu.VMEM),
        scratch_shapes=[pltpu.SemaphoreType.DMA((7,)), pltpu.SemaphoreType.DMA((7,)), pltpu.SemaphoreType.DMA],
    )(v)


def gather_weights(shards, *, name):
    n = len(shards)
    half = [s.shape[0] // 2 for s in shards]

    def body(*refs):
        ins, outs = refs[:n], refs[n:2 * n]
        send_sems, recv_sems, local_sems = refs[2 * n:]
        x, y, c = _position()
        me_chip = 2 * x + y
        chips = _other_chips(x, y)

        def piece(w, chip, core):
            return outs[w].at[chip, pl.ds(core * half[w], half[w]), :]

        def copy(w, k, chip, core, to, src=None):
            dst = piece(w, chip, core)
            return pltpu.make_async_remote_copy(
                src_ref=dst if src is None else src, dst_ref=dst,
                send_sem=send_sems.at[6 * w + k], recv_sem=recv_sems.at[6 * w + k], device_id=to, device_id_type=MESH)

        local = [pltpu.make_async_copy(ins[w], outs[w].at[me_chip], local_sems.at[w]) for w in range(n)]
        for cp in local:
            cp.start()
        sent = []
        for w in range(n):
            for j, (px, py) in enumerate(chips):
                cp = copy(w, j, me_chip, c, (px, py, c), src=ins[w].at[pl.ds(c * half[w], half[w]), :])
                cp.start()
                sent.append(cp)
        for w in range(n):
            for j, (px, py) in enumerate(chips):
                copy(w, j, 2 * px + py, c, (px, py, c)).wait_recv()
                cp = copy(w, 3 + j, 2 * px + py, c, (x, y, 1 - c))
                cp.start()
                sent.append(cp)
        for w in range(n):
            for j, (px, py) in enumerate(chips):
                copy(w, 3 + j, 2 * px + py, 1 - c, (x, y, 1 - c)).wait_recv()
        for cp in sent:
            cp.wait_send()
        for cp in local:
            cp.wait()

    return _pcall(
        body, name=name,
        out_shape=[jax.ShapeDtypeStruct((N_CHIPS,) + s.shape, s.dtype) for s in shards],
        in_specs=[ANY] * n, out_specs=[ANY] * n,
        scratch_shapes=[pltpu.SemaphoreType.DMA((6 * n,)), pltpu.SemaphoreType.DMA((6 * n,)),
                        pltpu.SemaphoreType.DMA((n,))],
    )(*shards)


STAGE_BYTES = 2 << 20


def _stage_rows(rows, width, itemsize=4):
    return _pick(rows, tuple(t for t in (1024, 512, 256, 128, 64, 32, 16, 8) if t * width * itemsize <= STAGE_BYTES * 3 // 2))


def _staged(chunks, bufs, load_sems):
    count, pending = {}, {}

    def load(i):
        cls, src, _ = chunks[i]
        slot = count.get(cls, 0) % 2
        count[cls] = count.get(cls, 0) + 1
        for cp, remote in pending.pop((cls, slot), []):
            if remote:
                cp.wait_send()
            else:
                cp.wait()
        staged = bufs[cls].at[slot, pl.ds(0, src.shape[0]), :]
        ld = pltpu.make_async_copy(src, staged, load_sems[cls].at[slot])
        ld.start()
        return ld, cls, slot, staged

    cur = load(0)
    for i in range(len(chunks)):
        nxt = load(i + 1) if i + 1 < len(chunks) else None
        ld, cls, slot, staged = cur
        ld.wait()
        started = []
        for make in chunks[i][2]:
            cp, remote = make(staged, slot)
            cp.start()
            started.append((cp, remote))
        pending[(cls, slot)] = started
        cur = nxt
    for started in pending.values():
        for cp, remote in started:
            if remote:
                cp.wait_send()
            else:
                cp.wait()


def _stage_scratch(widths_rows, dtype):
    scratch = []
    for width, rows in widths_rows:
        scratch += [pltpu.VMEM((2, rows, width), dtype), pltpu.SemaphoreType.DMA((2,)), pltpu.SemaphoreType.DMA((2,)),
                    pltpu.SemaphoreType.DMA((2,))]
    return scratch


def send_half_to_sibling(grads, *, name):
    n = len(grads)
    half = [g.shape[1] // 2 for g in grads]
    widths = sorted({g.shape[2] for g in grads})
    chunk_rows = [_stage_rows(h, g.shape[2]) for g, h in zip(grads, half)]
    plan = [(w, k, r0) for w in range(n) for k in range(N_CHIPS) for r0 in range(0, half[w], chunk_rows[w])]

    def body(*refs):
        ins, theirs = refs[:n], refs[n:2 * n]
        recv_sems = refs[2 * n]
        stage = refs[2 * n + 1:]
        bufs = {wd: stage[4 * i] for i, wd in enumerate(widths)}
        load_sems = {wd: stage[4 * i + 1] for i, wd in enumerate(widths)}
        send_sems = {wd: stage[4 * i + 2] for i, wd in enumerate(widths)}
        x, y, c = _position()
        chunks = []
        for idx, (w, k, r0) in enumerate(plan):
            wd = grads[w].shape[2]
            rb = chunk_rows[w]

            def make(staged, slot, idx=idx, w=w, k=k, r0=r0, wd=wd, rb=rb):
                return pltpu.make_async_remote_copy(
                    src_ref=staged, dst_ref=theirs[w].at[k, pl.ds(r0, rb), :], send_sem=send_sems[wd].at[slot],
                    recv_sem=recv_sems.at[idx], device_id=(x, y, 1 - c), device_id_type=MESH), True

            chunks.append((wd, ins[w].at[k, pl.ds((1 - c) * half[w] + r0, rb), :], [make]))
        _staged(chunks, bufs, load_sems)
        for idx, (w, k, r0) in enumerate(plan):
            wd = grads[w].shape[2]
            landed = theirs[w].at[k, pl.ds(r0, chunk_rows[w]), :]
            pltpu.make_async_remote_copy(
                src_ref=landed, dst_ref=landed, send_sem=send_sems[wd].at[0], recv_sem=recv_sems.at[idx],
                device_id=(x, y, 1 - c), device_id_type=MESH).wait_recv()

    stage_rows = [(wd, max(r for g, r in zip(grads, chunk_rows) if g.shape[2] == wd)) for wd in widths]
    return _pcall(
        body, name=name,
        out_shape=[jax.ShapeDtypeStruct((N_CHIPS, h, g.shape[2]), g.dtype) for g, h in zip(grads, half)],
        in_specs=[ANY] * n, out_specs=[ANY] * n,
        scratch_shapes=[pltpu.SemaphoreType.DMA((len(plan),))] + _stage_scratch(stage_rows, F32),
    )(*grads)


def scatter_to_chips(parts, *, name):
    n = len(parts)

    def body(*refs):
        ins, outs = refs[:n], refs[n:2 * n]
        send_sems, recv_sems, local_sems = refs[2 * n:]
        x, y, c = _position()
        me_chip = 2 * x + y
        chips = _other_chips(x, y)
        local, sent = [], []
        for w in range(n):
            cp = pltpu.make_async_copy(ins[w].at[me_chip], outs[w].at[me_chip], local_sems.at[w])
            cp.start()
            local.append(cp)
            for j, (px, py) in enumerate(chips):
                cp = pltpu.make_async_remote_copy(
                    src_ref=ins[w].at[2 * px + py], dst_ref=outs[w].at[me_chip],
                    send_sem=send_sems.at[3 * w + j], recv_sem=recv_sems.at[3 * w + j],
                    device_id=(px, py, c), device_id_type=MESH)
                cp.start()
                sent.append(cp)
        for w in range(n):
            for j, (px, py) in enumerate(chips):
                slot = outs[w].at[2 * px + py]
                pltpu.make_async_remote_copy(
                    src_ref=slot, dst_ref=slot, send_sem=send_sems.at[3 * w + j], recv_sem=recv_sems.at[3 * w + j],
                    device_id=(px, py, c), device_id_type=MESH).wait_recv()
        for cp in sent:
            cp.wait_send()
        for cp in local:
            cp.wait()

    return _pcall(
        body, name=name, out_shape=[jax.ShapeDtypeStruct(p.shape, p.dtype) for p in parts],
        in_specs=[ANY] * n, out_specs=[ANY] * n,
        scratch_shapes=[pltpu.SemaphoreType.DMA((3 * n,)), pltpu.SemaphoreType.DMA((3 * n,)),
                        pltpu.SemaphoreType.DMA((n,))],
    )(*parts)


def join_with_sibling(halves, *, name):
    n = len(halves)
    widths = sorted({h.shape[1] for h in halves})
    chunk_rows = [_stage_rows(h.shape[0], h.shape[1]) for h in halves]
    plan = [(w, r0) for w in range(n) for r0 in range(0, halves[w].shape[0], chunk_rows[w])]

    def body(*refs):
        ins, outs = refs[:n], refs[n:2 * n]
        recv_sems = refs[2 * n]
        stage = refs[2 * n + 1:]
        bufs = {wd: stage[4 * i] for i, wd in enumerate(widths)}
        load_sems = {wd: stage[4 * i + 1] for i, wd in enumerate(widths)}
        send_sems = {wd: stage[4 * i + 2] for i, wd in enumerate(widths)}
        store_sems = {wd: stage[4 * i + 3] for i, wd in enumerate(widths)}
        x, y, c = _position()
        chunks = []
        for idx, (w, r0) in enumerate(plan):
            h, wd = halves[w].shape
            rb = chunk_rows[w]

            def to_sibling(staged, slot, idx=idx, w=w, r0=r0, h=h, wd=wd, rb=rb):
                return pltpu.make_async_remote_copy(
                    src_ref=staged, dst_ref=outs[w].at[pl.ds(c * h + r0, rb), :], send_sem=send_sems[wd].at[slot],
                    recv_sem=recv_sems.at[idx], device_id=(x, y, 1 - c), device_id_type=MESH), True

            def to_mine(staged, slot, w=w, r0=r0, h=h, wd=wd, rb=rb):
                return pltpu.make_async_copy(staged, outs[w].at[pl.ds(c * h + r0, rb), :], store_sems[wd].at[slot]), False

            chunks.append((wd, ins[w].at[pl.ds(r0, rb), :], [to_sibling, to_mine]))
        _staged(chunks, bufs, load_sems)
        for idx, (w, r0) in enumerate(plan):
            h, wd = halves[w].shape
            landed = outs[w].at[pl.ds((1 - c) * h + r0, chunk_rows[w]), :]
            pltpu.make_async_remote_copy(
                src_ref=landed, dst_ref=landed, send_sem=send_sems[wd].at[0], recv_sem=recv_sems.at[idx],
                device_id=(x, y, 1 - c), device_id_type=MESH).wait_recv()

    stage_rows = [(wd, max(r for h, r in zip(halves, chunk_rows) if h.shape[1] == wd)) for wd in widths]
    return _pcall(
        body, name=name,
        out_shape=[jax.ShapeDtypeStruct((2 * h.shape[0], h.shape[1]), h.dtype) for h in halves],
        in_specs=[ANY] * n, out_specs=[ANY] * n,
        scratch_shapes=[pltpu.SemaphoreType.DMA((len(plan),))] + _stage_scratch(stage_rows, F32),
    )(*halves)


def _row_tile(rows, cols, itemsize=4, budget=2 << 20):
    for t in (1024, 512, 256, 128, 64, 32, 16, 8):
        if rows % t == 0 and t * cols * itemsize <= budget:
            return t
    return rows


def add_half_to_bf16(core, full, theirs, *, name):
    k, r, c = theirs.shape
    tr = _row_tile(r, c)
    nb = r // tr

    def body(core_ref, a_ref, b_ref, o_ref):
        o_ref[...] = (a_ref[...] + b_ref[...]).astype(BF16)

    spec = pl.BlockSpec((None, tr, c), lambda i, j, core_ref: (i, j, 0))
    grid_spec = pltpu.PrefetchScalarGridSpec(
        num_scalar_prefetch=1, grid=(k, nb),
        in_specs=[pl.BlockSpec((None, tr, c), lambda i, j, core_ref: (i, core_ref[0] * nb + j, 0)), spec],
        out_specs=spec)
    return _pcall(body, name=name, grid_spec=grid_spec,
                  out_shape=jax.ShapeDtypeStruct(theirs.shape, BF16))(core, full, theirs)


def sum_blocks(v, *, name):
    k, r, c = v.shape
    tr = _row_tile(r, c * k)

    def body(v_ref, o_ref):
        acc = v_ref[0].astype(F32)
        for j in range(1, k):
            acc = acc + v_ref[j].astype(F32)
        o_ref[...] = acc

    return _pcall(body, name=name, grid=(r // tr,),
                  in_specs=[pl.BlockSpec((k, tr, c), lambda i: (0, i, 0))],
                  out_specs=pl.BlockSpec((tr, c), lambda i: (i, 0)),
                  out_shape=jax.ShapeDtypeStruct((r, c), F32))(v)


def adamw(w, g, m, v, *, name):
    r, c = w.shape
    tr = _row_tile(r, c, budget=1 << 20)
    m_scale = 1.0 / (1.0 - ADAM_B1 ** ADAM_STEP)
    v_scale = 1.0 / (1.0 - ADAM_B2 ** ADAM_STEP)

    def body(w_ref, g_ref, m_ref, v_ref, d_ref, nm_ref, nv_ref):
        gv = g_ref[...]
        nm = ADAM_B1 * m_ref[...] + (1.0 - ADAM_B1) * gv
        nv = ADAM_B2 * v_ref[...] + (1.0 - ADAM_B2) * (gv * gv)
        nm_ref[...] = nm
        nv_ref[...] = nv
        d_ref[...] = -ADAM_LR * ((nm * m_scale) / (jnp.sqrt(nv * v_scale) + ADAM_EPS) + ADAM_WD * w_ref[...])

    spec = pl.BlockSpec((tr, c), lambda i: (i, 0))
    return _pcall(body, name=name, grid=(r // tr,), in_specs=[spec] * 4, out_specs=[spec] * 3,
                  out_shape=[jax.ShapeDtypeStruct((r, c), F32)] * 3)(w, g, m, v)


def ada_fwd(c_all, w_shard, b_shard, *, name):
    bsz, d = c_all.shape
    ncol = w_shard.shape[1]

    def body(c_ref, w_ref, b_ref, o_ref):
        cv = c_ref[...]
        act = (cv * _sigmoid(cv)).astype(BF16)
        o_ref[...] = _dot(act, w_ref[...].astype(BF16)) + b_ref[...]

    tn = _pick(ncol, (512, 256, 128))
    return _pcall(body, name=name, grid=(ncol // tn,),
                  in_specs=[pl.BlockSpec((bsz, d), lambda j: (0, 0)), pl.BlockSpec((d, tn), lambda j: (0, j)),
                            pl.BlockSpec((1, tn), lambda j: (0, j))],
                  out_specs=pl.BlockSpec((bsz, tn), lambda j: (0, j)),
                  out_shape=jax.ShapeDtypeStruct((bsz, ncol), F32))(c_all, w_shard, b_shard)


def ada_bwd(c_all, d_mod_all, d_mod_cols, *, name):
    bsz, d = c_all.shape
    ncol = d_mod_cols.shape[1]
    nall = d_mod_all.shape[1]

    def body(c_ref, da_ref, dc_ref, gw_ref, gb_ref):
        cv = c_ref[...]
        act = (cv * _sigmoid(cv)).astype(BF16)
        gw_ref[...] = _dot_tn(act, dc_ref[...].astype(BF16))
        gb_ref[...] = _colsum(da_ref[...])

    full = lambda s: pl.BlockSpec(s, lambda: (0,) * len(s))
    return _pcall(body, name=name,
                  in_specs=[full((bsz, d)), full((bsz, nall)), full((bsz, ncol))],
                  out_specs=[full((d, ncol)), full((1, nall))],
                  out_shape=[jax.ShapeDtypeStruct((d, ncol), F32), jax.ShapeDtypeStruct((1, nall), F32)],
                  )(c_all, d_mod_all, d_mod_cols)


WEIGHT_NAMES = ['w_ada', 'b_ada', 'pre_norm1', 'post_norm1', 'w_in', 'b_gate', 'lru_conv_w', 'lru_conv_b', 'lru_wa',
                'lru_ba', 'lru_wx', 'lru_bx', 'lru_lambda', 'w_pa', 'ssd_conv_w', 'ssd_conv_b', 'ssd_dt_bias',
                'ssd_a_log', 'ssd_d', 'ssd_norm_w', 'w_pb', 'w_out', 'pre_norm2', 'post_norm2', 'w_ff1', 'w_ff2']
BIG_NAMES = ['w_in', 'w_pa', 'w_pb', 'w_out', 'w_ff1', 'w_ff2']
COLUMN_SHARDED = ('w_in', 'w_ff1')
SMALL_NAMES = [n for n in WEIGHT_NAMES if n not in BIG_NAMES + ['w_ada', 'b_ada']]
SHARDED_SMALL = ('lru_conv_w', 'ssd_conv_w')
PACK_WIDTH = 1024


def _pack(parts):
    flat = jnp.concatenate([p.reshape(-1).astype(F32) for p in parts])
    rows = -(-flat.shape[0] // (PACK_WIDTH * SUBLANES)) * SUBLANES
    return jnp.pad(flat, (0, rows * PACK_WIDTH - flat.shape[0])).reshape(rows, PACK_WIDTH)


def _unpack(packed, shapes):
    flat = packed.reshape(-1)
    out, pos = [], 0
    for s in shapes:
        size = int(np.prod(s))
        out.append(flat[pos:pos + size].reshape(s))
        pos += size
    return out


def kernel(x, c, w_ada, b_ada, pre_norm1, post_norm1, w_in, b_gate, lru_conv_w, lru_conv_b, lru_wa, lru_ba, lru_wx, lru_bx, lru_lambda, w_pa, ssd_conv_w, ssd_conv_b, ssd_dt_bias, ssd_a_log, ssd_d, ssd_norm_w, w_pb, w_out, pre_norm2, post_norm2, w_ff1, w_ff2, loss_target, m_w_ada, m_b_ada, m_pre_norm1, m_post_norm1, m_w_in, m_b_gate, m_lru_conv_w, m_lru_conv_b, m_lru_wa, m_lru_ba, m_lru_wx, m_lru_bx, m_lru_lambda, m_w_pa, m_ssd_conv_w, m_ssd_conv_b, m_ssd_dt_bias, m_ssd_a_log, m_ssd_d, m_ssd_norm_w, m_w_pb, m_w_out, m_pre_norm2, m_post_norm2, m_w_ff1, m_w_ff2, v_w_ada, v_b_ada, v_pre_norm1, v_post_norm1, v_w_in, v_b_gate, v_lru_conv_w, v_lru_conv_b, v_lru_wa, v_lru_ba, v_lru_wx, v_lru_bx, v_lru_lambda, v_w_pa, v_ssd_conv_w, v_ssd_conv_b, v_ssd_dt_bias, v_ssd_a_log, v_ssd_d, v_ssd_norm_w, v_w_pb, v_w_out, v_pre_norm2, v_post_norm2, v_w_ff1, v_w_ff2):
    given = dict(locals())
    bsz, seq, d = x.shape
    my_x, my_y, my_c = lax.axis_index("x"), lax.axis_index("y"), lax.axis_index("c")
    chip = 2 * my_x + my_y
    dev = 2 * chip + my_c
    strip = lambda a: a if a.ndim == 2 else a[0]
    w = {n: strip(given[n]) for n in WEIGHT_NAMES}
    m = {n: strip(given["m_" + n]) for n in WEIGHT_NAMES}
    v = {n: strip(given["v_" + n]) for n in WEIGHT_NAMES}

    first_shapes = [c.shape] + [w[n].shape for n in SHARDED_SMALL]
    first = allgather8(_pack([c] + [w[n] for n in SHARDED_SMALL]), name="gather_c_conv")
    first = first.reshape(N_DEV, -1, PACK_WIDTH)
    per_dev = [_unpack(first[k], first_shapes) for k in range(N_DEV)]
    c_all = jnp.concatenate([p[0] for p in per_dev], axis=0)
    conv_full = {n: jnp.concatenate([per_dev[2 * k][1 + i] for k in range(N_CHIPS)], axis=1)
                 for i, n in enumerate(SHARDED_SMALL)}

    ncol = w["w_ada"].shape[1]
    b_cols = lax.dynamic_slice(b_ada, (0, chip * ncol), (1, ncol))
    mod_cols = ada_fwd(c_all, w["w_ada"], b_cols, name="ada_fwd")
    mod_all = allgather8(mod_cols, name="gather_mod").reshape(N_CHIPS, 2, N_DEV * bsz, ncol)[:, 0]
    mod_all = jnp.transpose(mod_all, (1, 0, 2)).reshape(N_DEV * bsz, N_CHIPS * ncol)
    mod = lax.dynamic_slice(mod_all, (dev * bsz, 0), (bsz, 6 * d)).reshape(bsz, 6, d)
    mod = jnp.pad(mod, ((0, 0), (0, 2), (0, 0)))

    gathered = gather_weights([w[n].astype(BF16) for n in BIG_NAMES], name="gather_weights")
    full = {}
    for n, g in zip(BIG_NAMES, gathered):
        if n in COLUMN_SHARDED:
            full[n] = jnp.transpose(g, (1, 0, 2)).reshape(g.shape[1], N_CHIPS * g.shape[2])
        else:
            full[n] = g.reshape(N_CHIPS * g.shape[1], g.shape[2])
    w_in_full = full["w_in"]
    big = {"w_main": w_in_full[:, :8192],
           "w_dt": jnp.pad(w_in_full[:, 8192:8192 + SSD_HEADS], ((0, 0), (0, LANES - SSD_HEADS))),
           "w_gates": w_in_full[:, 8192 + SSD_HEADS:]}
    for n in ("w_pa", "w_pb", "w_out", "w_ff1", "w_ff2"):
        big[n] = full[n]
    small = {n: w[n] for n in SMALL_NAMES}
    small.update(conv_full)

    loss_cols, grad_x, d_mod, big_grads, small_grads = local_step(x, loss_target, mod, big, small)

    packed = _pack([d_mod, loss_cols] + [small_grads[n] for n in SMALL_NAMES])
    rows = packed.shape[0]
    everyone = allgather8(packed, name="gather_small").reshape(N_DEV, rows, PACK_WIDTH)
    d_mod_all = everyone[:, :bsz * 6].reshape(N_DEV * bsz, 6 * d)
    summed = sum_blocks(everyone, name="sum_small")
    shapes = [d_mod.shape, loss_cols.shape] + [small_grads[n].shape for n in SMALL_NAMES]
    parts = _unpack(summed, shapes)
    loss = jnp.sum(parts[1])
    grads = dict(zip(SMALL_NAMES, parts[2:]))
    for n in SHARDED_SMALL:
        cols = w[n].shape[1]
        grads[n] = lax.dynamic_slice(grads[n], (0, chip * cols), (grads[n].shape[0], cols))
    d_mod_cols = lax.dynamic_slice(d_mod_all, (0, chip * ncol), (N_DEV * bsz, ncol))
    grads["w_ada"], grads["b_ada"] = ada_bwd(c_all, d_mod_all, d_mod_cols, name="ada_bwd")

    by_chip = []
    for n in BIG_NAMES:
        g = big_grads[n]
        if n in COLUMN_SHARDED:
            by_chip.append(jnp.transpose(g.reshape(g.shape[0], N_CHIPS, g.shape[1] // N_CHIPS), (1, 0, 2)))
        else:
            by_chip.append(g.reshape(N_CHIPS, g.shape[0] // N_CHIPS, g.shape[1]))
    theirs = send_half_to_sibling(by_chip, name="grads_to_sibling")
    core = my_c.astype(jnp.int32).reshape(1)
    chip_sums = [add_half_to_bf16(core, a, b, name="add_cores_" + n) for n, a, b in zip(BIG_NAMES, by_chip, theirs)]
    landed = scatter_to_chips(chip_sums, name="grads_to_chips")
    halves = [sum_blocks(p, name="add_chips_" + n) for n, p in zip(BIG_NAMES, landed)]
    for n, g in zip(BIG_NAMES, join_with_sibling(halves, name="grads_join")):
        grads[n] = g

    delta, new_m, new_v = {}, {}, {}
    for n in BIG_NAMES + ["w_ada", "b_ada"]:
        delta[n], new_m[n], new_v[n] = adamw(w[n], grads[n], m[n], v[n], name="adamw_" + n)
    shapes = [w[n].shape for n in SMALL_NAMES]
    pk = lambda src: _pack([src[n] for n in SMALL_NAMES])
    upd = adamw(pk(w), pk(grads), pk(m), pk(v), name="adamw_small")
    for out, packed_out in zip((delta, new_m, new_v), upd):
        out.update(zip(SMALL_NAMES, _unpack(packed_out, shapes)))

    shaped = lambda src: [src[n].reshape(given[n].shape) for n in WEIGHT_NAMES]
    return (loss, grad_x, *shaped(grads), *shaped(delta), *shaped(new_m), *shaped(new_v))
```

```python
import functools
import math

import numpy as np
import jax
import jax.numpy as jnp
from jax import lax
from jax.experimental import pallas as pl
from jax.experimental.pallas import tpu as pltpu

F32 = jnp.float32
BF16 = jnp.bfloat16
HI = lax.Precision.HIGHEST
MESH = pl.DeviceIdType.MESH

D_MODEL = 1024
LRU_HEADS = 16
LRU_HEAD_DIM = 64
LRU_C = 8.0
SSD_INNER = 2048
SSD_HEADS = 32
SSD_HEAD_DIM = 64
SSD_GROUPS = 8
SSD_STATE = 128
SSD_CHUNK = 128
SSD_CONV_DIM = 4096
D_FF = 4096
EPS = 1e-6
N_CHIPS = 4
N_DEV = 8
LANES = 128
SUBLANES = 8

ADAM_LR = 0.001
ADAM_B1 = 0.9
ADAM_B2 = 0.999
ADAM_EPS = 1e-08
ADAM_WD = 0.01
ADAM_STEP = 10


ANY = pl.BlockSpec(memory_space=pl.ANY)


def _pcall(body, **kw):
    return pl.pallas_call(body, **kw)


class Background:
    def __init__(self, inputs, out_shapes, scratch, start, finish):
        self.inputs, self.out_shapes, self.scratch = list(inputs), list(out_shapes), list(scratch)
        self.start, self.finish = start, finish

    def wrap(self, body, kw):
        n_in, n_out = len(kw["in_specs"]), len(kw["out_specs"])
        n_scr = len(kw.get("scratch_shapes", []))
        b_in, b_out = len(self.inputs), len(self.out_shapes)
        grid = kw["grid"]

        def wrapped(*refs):
            ins, b_ins = refs[:n_in], refs[n_in:n_in + b_in]
            o0 = n_in + b_in
            outs, b_outs = refs[o0:o0 + n_out], refs[o0 + n_out:o0 + n_out + b_out]
            s0 = o0 + n_out + b_out
            scr, b_scr = refs[s0:s0 + n_scr], refs[s0 + n_scr:]
            ids = [pl.program_id(a) for a in range(len(grid))]
            first = functools.reduce(jnp.logical_and, [i == 0 for i in ids])
            last = functools.reduce(jnp.logical_and, [i == g - 1 for i, g in zip(ids, grid)])

            @pl.when(first)
            def _():
                self.start(b_ins, b_outs, b_scr)

            body(*ins, *outs, *scr)

            @pl.when(last)
            def _():
                self.finish(b_ins, b_outs, b_scr)

        kw = dict(kw, in_specs=list(kw["in_specs"]) + [ANY] * b_in, out_specs=list(kw["out_specs"]) + [ANY] * b_out,
                  out_shape=list(kw["out_shape"]) + self.out_shapes,
                  scratch_shapes=list(kw.get("scratch_shapes", [])) + self.scratch)
        return wrapped, kw


def _run(body, args, bg, **kw):
    n_out = len(kw["out_shape"])
    if bg is None:
        return list(_pcall(body, **kw)(*args)), []
    body, kw = bg.wrap(body, kw)
    outs = _pcall(body, **kw)(*args, *bg.inputs)
    return list(outs[:n_out]), list(outs[n_out:])


def _sigmoid(v):
    return 1.0 / (1.0 + jnp.exp(-v))


def _log1p(u):
    return jnp.where(u < 1e-3, u * (1.0 - u * (0.5 - u * (1.0 / 3.0))), jnp.log(1.0 + u))


def _softplus(v):
    return jnp.maximum(v, 0.0) + _log1p(jnp.exp(-jnp.abs(v)))


def _neg_expm1(v):
    small = -v * (1.0 + v * (0.5 + v * (1.0 / 6.0 + v * (1.0 / 24.0))))
    return jnp.where(v > -0.05, small, 1.0 - jnp.exp(v))


_GELU_K = math.sqrt(2.0 / math.pi)


def _gelu(v):
    t = jnp.tanh(_GELU_K * (v + 0.044715 * v * v * v))
    return 0.5 * v * (1.0 + t)


def _gelu_grad(v):
    t = jnp.tanh(_GELU_K * (v + 0.044715 * v * v * v))
    return 0.5 * (1.0 + t) + 0.5 * v * (1.0 - t * t) * _GELU_K * (1.0 + 3.0 * 0.044715 * v * v)


def _colsum(v):
    return jnp.sum(v, axis=0, keepdims=True)


def _dot(a, b, precision=None):
    return lax.dot_general(a, b, (((1,), (0,)), ((), ())), preferred_element_type=F32, precision=precision)


def _dot_nt(a, b):
    return lax.dot_general(a, b, (((1,), (1,)), ((), ())), preferred_element_type=F32)


def _dot_tn(a, b):
    return lax.dot_general(a, b, (((0,), (0,)), ((), ())), preferred_element_type=F32)


def _shift_down(xt, prev8, j):
    if j == 0:
        return xt
    n = xt.shape[0]
    r = pltpu.roll(xt, j, 0)
    p = pltpu.roll(prev8, j, 0)
    rows = lax.broadcasted_iota(jnp.int32, (SUBLANES, xt.shape[1]), 0)
    top = jnp.where(rows < j, p, r[0:SUBLANES])
    if n == SUBLANES:
        return top
    return jnp.concatenate([top, r[SUBLANES:]], axis=0)


def _shift_up(xt, next8, j):
    if j == 0:
        return xt
    n = xt.shape[0]
    r = pltpu.roll(xt, n - j, 0)
    p = pltpu.roll(next8, SUBLANES - j, 0)
    rows = lax.broadcasted_iota(jnp.int32, (SUBLANES, xt.shape[1]), 0)
    bot = jnp.where(rows >= SUBLANES - j, p, r[n - SUBLANES:])
    if n == SUBLANES:
        return bot
    return jnp.concatenate([r[:n - SUBLANES], bot], axis=0)


def _conv4(xt, prev8, w, b):
    out = b + w[3:4] * xt
    for k in range(3):
        out = out + w[k:k + 1] * _shift_down(xt, prev8, 3 - k)
    return out


def _conv4_bwd(d_out, next8, xt, w):
    d_x = w[3:4] * d_out
    d_w = []
    for k in range(3):
        up = _shift_up(d_out, next8, 3 - k)
        d_x = d_x + w[k:k + 1] * up
        d_w.append(_colsum(up * xt))
    d_w.append(_colsum(d_out * xt))
    return d_x, d_w, _colsum(d_out)


def _stack_rows(rows, width):
    rows = list(rows) + [jnp.zeros((1, width), F32)] * (SUBLANES - len(rows))
    return jnp.concatenate(rows, axis=0)


def _pick(n, cands):
    for c in cands:
        if n % c == 0:
            return c
    raise ValueError(f"no tile for {n}")


MM_ROWS = 512
MM_PANEL_COLS = 2048
MM_SUB = 512


def mm_nn(pairs, *, name, out_dtype=F32, a_fn=None, add=None, epi=None, extra=None, bg=None):
    np_ = len(pairs)
    m, n = pairs[0][0].shape[0], pairs[0][1].shape[1]
    tm = _pick(m, (MM_ROWS, 256, 128, 64, 32, 16, 8))
    pn = n if n <= MM_PANEL_COLS else _pick(n, (MM_PANEL_COLS, 1024, 512, 256, 128))
    ns = _pick(pn, (MM_SUB, 256, 128))
    has_add, has_extra = add is not None, extra is not None
    stage0 = a_fn is not None or pairs[0][0].dtype != BF16

    def body(*refs):
        a_refs, b_refs = refs[:np_], refs[np_:2 * np_]
        pos = 2 * np_
        add_ref = extra_ref = None
        if has_add:
            add_ref = refs[pos]
            pos += 1
        if has_extra:
            extra_ref = refs[pos]
            pos += 1
        o_ref = refs[pos]
        lhs = list(a_refs)
        if stage0:
            av = a_refs[0][...]
            if a_fn is not None:
                av = a_fn(av)
            refs[pos + 1][...] = av.astype(BF16)
            lhs[0] = refs[pos + 1]
        for n0 in range(0, pn, ns):
            sl = slice(n0, n0 + ns)
            acc = None
            for a_ref, b_ref in zip(lhs, b_refs):
                part = _dot(a_ref[...].astype(BF16), b_ref[:, sl])
                acc = part if acc is None else acc + part
            if has_add:
                acc = acc + add_ref[:, sl]
            if epi is not None:
                acc = epi(acc, extra_ref[:, sl]) if has_extra else epi(acc)
            o_ref[:, sl] = acc.astype(out_dtype)

    in_specs = [pl.BlockSpec((tm, a.shape[1]), lambda j, i: (i, 0)) for a, _ in pairs]
    in_specs += [pl.BlockSpec((b.shape[0], pn), lambda j, i: (0, j)) for _, b in pairs]
    args = [a for a, _ in pairs] + [b for _, b in pairs]
    tile = pl.BlockSpec((tm, pn), lambda j, i: (i, j))
    if has_add:
        in_specs.append(tile)
        args.append(add)
    if has_extra:
        in_specs.append(tile)
        args.append(extra)
    outs, bg_outs = _run(
        body, args, bg, name=name, grid=(n // pn, m // tm), in_specs=in_specs, out_specs=[tile],
        out_shape=[jax.ShapeDtypeStruct((m, n), out_dtype)],
        scratch_shapes=[pltpu.VMEM((tm, pairs[0][0].shape[1]), BF16)] if stage0 else [])
    return outs[0] if bg is None else (outs[0], bg_outs)


MM_REDUCE_ROWS = 1024
MM_GRAD_ROWS = 1024
MM_GRAD_COLS = 2048


def mm_tn(a, b, *, name, a_fn=None):
    m, ka = a.shape
    nb = b.shape[1]
    pa = _pick(ka, (MM_GRAD_ROWS, 512, 256, 128))
    pb = nb if nb <= MM_GRAD_COLS else _pick(nb, (MM_GRAD_COLS, 1024, 512, 256, 128))
    ns = _pick(pb, (MM_SUB, 256, 128))
    tmk = _pick(m, (MM_REDUCE_ROWS, 512, 256, 128, 64, 32, 16))

    def body(a_ref, b_ref, o_ref, lhs):
        k = pl.program_id(2)

        @pl.when(k == 0)
        def _():
            o_ref[...] = jnp.zeros_like(o_ref)

        av = a_ref[...]
        if a_fn is not None:
            av = a_fn(av)
        lhs[...] = av.astype(BF16)
        for n0 in range(0, pb, ns):
            o_ref[:, n0:n0 + ns] += _dot_tn(lhs[...], b_ref[:, n0:n0 + ns].astype(BF16))

    return _pcall(
        body, name=name,
        grid=(ka // pa, nb // pb, m // tmk),
        in_specs=[pl.BlockSpec((tmk, pa), lambda i, j, k: (k, i)),
                  pl.BlockSpec((tmk, pb), lambda i, j, k: (k, j))],
        out_specs=pl.BlockSpec((pa, pb), lambda i, j, k: (i, j)),
        out_shape=jax.ShapeDtypeStruct((ka, nb), F32),
        scratch_shapes=[pltpu.VMEM((tmk, pa), BF16)],
    )(a, b)


def _relu_sq(v):
    r = jnp.maximum(v, 0.0)
    return r * r


ROW_TILE = 512


def _row_specs(bsz, seq, width, ts):
    return pl.BlockSpec((None, ts, width), lambda b, i: (b, i, 0))


def _vec_spec(width):
    return pl.BlockSpec((1, width), lambda b, i: (0, 0))


def _mod_spec():
    return pl.BlockSpec((None, SUBLANES, D_MODEL), lambda b, i: (b, 0, 0))


def _rstd(v):
    return lax.rsqrt(jnp.mean(v * v, axis=-1, keepdims=True) + EPS)


def prenorm(x, w, mod, *, name):
    bsz, seq, d = x.shape
    ts = _pick(seq, (ROW_TILE, 256, 128))

    def body(x_ref, w_ref, mod_ref, h_ref):
        xv = x_ref[...]
        m = mod_ref[...]
        xh = xv * _rstd(xv)
        h_ref[...] = ((xh * w_ref[...]) * (1.0 + m[1:2]) + m[0:1]).astype(BF16)

    return _pcall(
        body, name=name, grid=(bsz, seq // ts),
        in_specs=[_row_specs(bsz, seq, d, ts), _vec_spec(d), _mod_spec()],
        out_specs=_row_specs(bsz, seq, d, ts),
        out_shape=jax.ShapeDtypeStruct((bsz, seq, d), BF16),
    )(x, w, mod)


def post1_pre2(x, out1, mod, post1, pre2, *, name):
    bsz, seq, d = x.shape
    ts = _pick(seq, (ROW_TILE, 256, 128))

    def body(x_ref, o_ref, mod_ref, p1_ref, p2_ref, x1_ref, h2_ref):
        m = mod_ref[...]
        ov = o_ref[...]
        x1 = x_ref[...] + m[2:3] * ((ov * _rstd(ov)) * p1_ref[...])
        x1_ref[...] = x1
        xh = x1 * _rstd(x1)
        h2_ref[...] = ((xh * p2_ref[...]) * (1.0 + m[4:5]) + m[3:4]).astype(BF16)

    return _pcall(
        body, name=name, grid=(bsz, seq // ts),
        in_specs=[_row_specs(bsz, seq, d, ts), _row_specs(bsz, seq, d, ts), _mod_spec(), _vec_spec(d), _vec_spec(d)],
        out_specs=[_row_specs(bsz, seq, d, ts), _row_specs(bsz, seq, d, ts)],
        out_shape=[jax.ShapeDtypeStruct((bsz, seq, d), F32), jax.ShapeDtypeStruct((bsz, seq, d), BF16)],
    )(x, out1, mod, post1, pre2)


def _acc_specs(d):
    per_batch = pl.BlockSpec((None, SUBLANES, d), lambda b, i: (b, 0, 0))
    glob = pl.BlockSpec((SUBLANES, d), lambda b, i: (0, 0))
    return per_batch, glob


def _accumulate(pb_ref, gl_ref, pb_rows, gl_rows, width):
    b, i = pl.program_id(0), pl.program_id(1)

    @pl.when(i == 0)
    def _():
        pb_ref[...] = jnp.zeros_like(pb_ref)

    @pl.when((b == 0) & (i == 0))
    def _():
        gl_ref[...] = jnp.zeros_like(gl_ref)

    pb_ref[...] += _stack_rows(pb_rows, width)
    gl_ref[...] += _stack_rows(gl_rows, width)


def _rms_bwd(d_n, n, r):
    return r * (d_n - n * jnp.mean(d_n * n, axis=-1, keepdims=True))


def final_bwd(x1, y2, target, mod, post2, *, name):
    bsz, seq, d = x1.shape
    ts = _pick(seq, (ROW_TILE, 256, 128))

    def body(x1_ref, y_ref, t_ref, mod_ref, p_ref, dx_ref, dy_ref, pb_ref, gl_ref):
        m = mod_ref[...]
        g2 = m[5:6]
        yv = y_ref[...]
        r = _rstd(yv)
        n = yv * r
        o = n * p_ref[...]
        diff = (x1_ref[...] + g2 * o) - t_ref[...]
        dx = diff * (1.0 / d)
        dx_ref[...] = dx
        d_o = dx * g2
        dy_ref[...] = _rms_bwd(d_o * p_ref[...], n, r).astype(BF16)
        _accumulate(pb_ref, gl_ref, [_colsum(dx * o)], [_colsum(d_o * n), _colsum(diff * diff) * (0.5 / d)], d)

    pb, gl = _acc_specs(d)
    rs = _row_specs(bsz, seq, d, ts)
    return _pcall(
        body, name=name, grid=(bsz, seq // ts),
        in_specs=[rs, rs, rs, _mod_spec(), _vec_spec(d)],
        out_specs=[rs, rs, pb, gl],
        out_shape=[jax.ShapeDtypeStruct((bsz, seq, d), F32), jax.ShapeDtypeStruct((bsz, seq, d), BF16),
                   jax.ShapeDtypeStruct((bsz, SUBLANES, d), F32), jax.ShapeDtypeStruct((SUBLANES, d), F32)],
    )(x1, y2, target, mod, post2)


def mid_bwd(d_h2, dx2, x1, out1, mod, pre2, post1, *, name):
    bsz, seq, d = x1.shape
    ts = _pick(seq, (ROW_TILE, 256, 128))

    def body(dh_ref, dx2_ref, x1_ref, o_ref, mod_ref, p2_ref, p1_ref, dx1_ref, do_ref, pb_ref, gl_ref):
        m = mod_ref[...]
        dh = dh_ref[...]
        x1 = x1_ref[...]
        r2 = _rstd(x1)
        xh = x1 * r2
        xw = xh * p2_ref[...]
        d_xw = dh * (1.0 + m[4:5])
        dx1 = dx2_ref[...] + _rms_bwd(d_xw * p2_ref[...], xh, r2)
        dx1_ref[...] = dx1
        ov = o_ref[...]
        r1 = _rstd(ov)
        n1 = ov * r1
        o1 = n1 * p1_ref[...]
        d_o1 = dx1 * m[2:3]
        do_ref[...] = _rms_bwd(d_o1 * p1_ref[...], n1, r1).astype(BF16)
        _accumulate(pb_ref, gl_ref, [_colsum(dh), _colsum(dh * xw), _colsum(dx1 * o1)],
                    [_colsum(d_xw * xh), _colsum(d_o1 * n1)], d)

    pb, gl = _acc_specs(d)
    rs = _row_specs(bsz, seq, d, ts)
    return _pcall(
        body, name=name, grid=(bsz, seq // ts),
        in_specs=[rs, rs, rs, rs, _mod_spec(), _vec_spec(d), _vec_spec(d)],
        out_specs=[rs, rs, pb, gl],
        out_shape=[jax.ShapeDtypeStruct((bsz, seq, d), F32), jax.ShapeDtypeStruct((bsz, seq, d), BF16),
                   jax.ShapeDtypeStruct((bsz, SUBLANES, d), F32), jax.ShapeDtypeStruct((SUBLANES, d), F32)],
    )(d_h2, dx2, x1, out1, mod, pre2, post1)


def first_bwd(d_h1, dx1, x, mod, pre1, *, name):
    bsz, seq, d = x.shape
    ts = _pick(seq, (ROW_TILE, 256, 128))

    def body(dh_ref, dx1_ref, x_ref, mod_ref, p_ref, gx_ref, pb_ref, gl_ref):
        m = mod_ref[...]
        dh = dh_ref[...]
        xv = x_ref[...]
        r = _rstd(xv)
        xh = xv * r
        xw = xh * p_ref[...]
        d_xw = dh * (1.0 + m[1:2])
        gx_ref[...] = dx1_ref[...] + _rms_bwd(d_xw * p_ref[...], xh, r)
        _accumulate(pb_ref, gl_ref, [_colsum(dh), _colsum(dh * xw)], [_colsum(d_xw * xh)], d)

    pb, gl = _acc_specs(d)
    rs = _row_specs(bsz, seq, d, ts)
    return _pcall(
        body, name=name, grid=(bsz, seq // ts),
        in_specs=[rs, rs, rs, _mod_spec(), _vec_spec(d)],
        out_specs=[rs, pb, gl],
        out_shape=[jax.ShapeDtypeStruct((bsz, seq, d), F32),
                   jax.ShapeDtypeStruct((bsz, SUBLANES, d), F32), jax.ShapeDtypeStruct((SUBLANES, d), F32)],
    )(d_h1, dx1, x, mod, pre1)


def merge_fwd(ya, yb, gates, b_gate, *, name):
    bsz, seq, d = ya.shape
    ts = _pick(seq, (ROW_TILE, 256, 128))

    def body(ya_ref, yb_ref, g_ref, b_ref, o_ref):
        g = _sigmoid(g_ref[...] + b_ref[...])
        o_ref[...] = (g[:, :d] * ya_ref[...] + g[:, d:] * yb_ref[...]).astype(BF16)

    rs = _row_specs(bsz, seq, d, ts)
    return _pcall(
        body, name=name, grid=(bsz, seq // ts),
        in_specs=[rs, rs, _row_specs(bsz, seq, 2 * d, ts), _vec_spec(2 * d)],
        out_specs=rs,
        out_shape=jax.ShapeDtypeStruct((bsz, seq, d), BF16),
    )(ya, yb, gates, b_gate)


def merge_bwd(d_merged, ya, yb, gates, b_gate, *, name):
    bsz, seq, d = ya.shape
    ts = _pick(seq, (ROW_TILE, 256, 128))

    def body(dm_ref, ya_ref, yb_ref, g_ref, b_ref, dya_ref, dyb_ref, dg_ref, gl_ref):
        b, i = pl.program_id(0), pl.program_id(1)
        g = _sigmoid(g_ref[...] + b_ref[...])
        dm = dm_ref[...]
        ga, gb = g[:, :d], g[:, d:]
        dya_ref[...] = (dm * ga).astype(BF16)
        dyb_ref[...] = (dm * gb).astype(BF16)
        dg = jnp.concatenate([dm * ya_ref[...] * ga * (1.0 - ga), dm * yb_ref[...] * gb * (1.0 - gb)], axis=1)
        dg_ref[...] = dg.astype(BF16)

        @pl.when((b == 0) & (i == 0))
        def _():
            gl_ref[...] = jnp.zeros_like(gl_ref)

        gl_ref[...] += _stack_rows([_colsum(dg)], 2 * d)

    rs = _row_specs(bsz, seq, d, ts)
    rs2 = _row_specs(bsz, seq, 2 * d, ts)
    return _pcall(
        body, name=name, grid=(bsz, seq // ts),
        in_specs=[rs, rs, rs, rs2, _vec_spec(2 * d)],
        out_specs=[rs, rs, rs2, pl.BlockSpec((SUBLANES, 2 * d), lambda b, i: (0, 0))],
        out_shape=[jax.ShapeDtypeStruct((bsz, seq, d), BF16), jax.ShapeDtypeStruct((bsz, seq, d), BF16),
                   jax.ShapeDtypeStruct((bsz, seq, 2 * d), BF16), jax.ShapeDtypeStruct((SUBLANES, 2 * d), F32)],
    )(d_merged, ya, yb, gates, b_gate)


LRU_TILE = 256
N_LRU_BLOCKS = D_MODEL // LANES


def _block_mm(v, w_ref, transpose=False):
    vb = v.astype(BF16)
    outs = []
    for j in range(N_LRU_BLOCKS):
        blk = vb[:, LANES * j:LANES * (j + 1)]
        outs.append(_dot_nt(blk, w_ref[j]) if transpose else _dot(blk, w_ref[j]))
    return jnp.concatenate(outs, axis=1)


def _lru_gates(xc, wa_ref, ba, wx_ref, bx, sp):
    r = _sigmoid(_block_mm(xc, wa_ref) + ba)
    i = _sigmoid(_block_mm(xc, wx_ref) + bx)
    la = (-LRU_C * r) * sp
    a = jnp.exp(la)
    sq = jnp.sqrt(_neg_expm1(2.0 * la))
    return r, i, a, sq


def _prev8_spec(width, col_block, tile_rows):
    per = tile_rows // SUBLANES
    return pl.BlockSpec((None, SUBLANES, width), lambda b, i: (b, jnp.maximum(i * per - 1, 0), col_block))


def lru_fwd(pm, cw, cb, wa, ba, wx, bx, lam, *, name):
    bsz, seq, _ = pm.shape
    d = D_MODEL
    ts = _pick(seq, (LRU_TILE, 128))

    def body(lx_ref, lxp_ref, lg_ref, cw_ref, cb_ref, wa_ref, ba_ref, wx_ref, bx_ref, lam_ref,
             h_ref, pa_ref, hc, a_s, u_s):
        i = pl.program_id(1)

        @pl.when(i == 0)
        def _():
            hc[...] = jnp.zeros_like(hc)

        lx = lx_ref[...]
        prev8 = jnp.where(i == 0, 0.0, lxp_ref[...])
        xc = _conv4(lx, prev8, cw_ref[...], cb_ref[...])
        sp = _softplus(-lam_ref[...])
        r, ig, a, sq = _lru_gates(xc, wa_ref, ba_ref[...], wx_ref, bx_ref[...], sp)
        a_s[...] = a
        u_s[...] = sq * (ig * xc)

        def step(g, h):
            r0 = pl.multiple_of(g * SUBLANES, SUBLANES)
            a8 = a_s[pl.ds(r0, SUBLANES), :]
            u8 = u_s[pl.ds(r0, SUBLANES), :]
            rows = []
            for j in range(SUBLANES):
                h = a8[j:j + 1] * h + u8[j:j + 1]
                rows.append(h)
            h_ref[pl.ds(r0, SUBLANES), :] = jnp.concatenate(rows, axis=0)
            return h

        hc[...] = lax.fori_loop(0, ts // SUBLANES, step, hc[...])
        pa_ref[...] = (h_ref[...] * _gelu(lg_ref[...])).astype(BF16)

    vec = _vec_spec(d)
    wspec = pl.BlockSpec((N_LRU_BLOCKS, LANES, LANES), lambda b, i: (0, 0, 0))
    return _pcall(
        body, name=name, grid=(bsz, seq // ts),
        in_specs=[pl.BlockSpec((None, ts, d), lambda b, i: (b, i, 0)), _prev8_spec(d, 0, ts),
                  pl.BlockSpec((None, ts, d), lambda b, i: (b, i, 1)),
                  pl.BlockSpec((4, d), lambda b, i: (0, 0)), vec, wspec, vec, wspec, vec, vec],
        out_specs=[_row_specs(bsz, seq, d, ts), _row_specs(bsz, seq, d, ts)],
        out_shape=[jax.ShapeDtypeStruct((bsz, seq, d), F32), jax.ShapeDtypeStruct((bsz, seq, d), BF16)],
        scratch_shapes=[pltpu.VMEM((1, d), F32), pltpu.VMEM((ts, d), F32), pltpu.VMEM((ts, d), F32)],
    )(pm, pm, pm, cw, cb, wa, ba, wx, bx, lam)


def lru_bwd(pm, h, d_pa, cw, cb, wa, ba, wx, bx, lam, *, name, bg=None):
    bsz, seq, _ = pm.shape
    d = D_MODEL
    ts = _pick(seq, (LRU_TILE, 128))
    nt = seq // ts
    per = ts // SUBLANES

    def rev(i):
        return nt - 1 - i

    def body(lx_ref, lxp_ref, lg_ref, h_ref, hp_ref, dpa_ref, cw_ref, cb_ref, wa_ref, ba_ref, wx_ref, bx_ref,
             lam_ref, dl_ref, dwa_ref, dwx_ref, rows_ref, carry, dxc_next, a_s, dh_s, acc_s):
        b, i = pl.program_id(0), pl.program_id(1)
        t = rev(i)

        @pl.when(i == 0)
        def _():
            carry[...] = jnp.zeros_like(carry)
            dxc_next[...] = jnp.zeros_like(dxc_next)

        @pl.when((b == 0) & (i == 0))
        def _():
            dwa_ref[...] = jnp.zeros_like(dwa_ref)
            dwx_ref[...] = jnp.zeros_like(dwx_ref)
            rows_ref[...] = jnp.zeros_like(rows_ref)

        lx = lx_ref[...]
        lg = lg_ref[...]
        prev8 = jnp.where(t == 0, 0.0, lxp_ref[...])
        cwv = cw_ref[...]
        xc = _conv4(lx, prev8, cwv, cb_ref[...])
        lam_v = lam_ref[...]
        sp = _softplus(-lam_v)
        r, ig, a, sq = _lru_gates(xc, wa_ref, ba_ref[...], wx_ref, bx_ref[...], sp)
        hv = h_ref[...]
        d_pa = dpa_ref[...]
        a_s[...] = a
        dh_s[...] = d_pa * _gelu(lg)

        def step(g, c):
            r0 = pl.multiple_of((per - 1 - g) * SUBLANES, SUBLANES)
            a8 = a_s[pl.ds(r0, SUBLANES), :]
            d8 = dh_s[pl.ds(r0, SUBLANES), :]
            rows = [None] * SUBLANES
            for j in range(SUBLANES - 1, -1, -1):
                acc = d8[j:j + 1] + c
                rows[j] = acc
                c = a8[j:j + 1] * acc
            acc_s[pl.ds(r0, SUBLANES), :] = jnp.concatenate(rows, axis=0)
            return c

        carry[...] = lax.fori_loop(0, per, step, carry[...])
        d_u = acc_s[...]
        hprev8 = jnp.where(t == 0, 0.0, hp_ref[...])
        d_a = d_u * _shift_down(hv, hprev8, 1)
        d_sq = d_u * (ig * xc)
        d_i = d_u * (sq * xc)
        d_xc = d_u * (sq * ig)
        d_la = d_a * a - d_sq * (a * a) / sq
        d_pre_r = (d_la * (-LRU_C * sp)) * (r * (1.0 - r))
        d_pre_i = d_i * (ig * (1.0 - ig))
        d_xc = d_xc + _block_mm(d_pre_r, wa_ref, transpose=True) + _block_mm(d_pre_i, wx_ref, transpose=True)
        xcb = xc.astype(BF16)
        drb = d_pre_r.astype(BF16)
        dib = d_pre_i.astype(BF16)
        for j in range(N_LRU_BLOCKS):
            sl = slice(LANES * j, LANES * (j + 1))
            dwa_ref[j] += _dot_tn(xcb[:, sl], drb[:, sl])
            dwx_ref[j] += _dot_tn(xcb[:, sl], dib[:, sl])
        d_lx, d_cw, d_cb = _conv4_bwd(d_xc, dxc_next[...], lx, cwv)
        dxc_next[...] = d_xc[0:SUBLANES]
        d_lam = _colsum(d_la * (-LRU_C * r)) * (-_sigmoid(-lam_v))
        rows_ref[...] += _stack_rows([_colsum(d_pre_r), _colsum(d_pre_i), d_lam, d_cb] + d_cw, d)
        dl_ref[:, :d] = d_lx.astype(BF16)
        dl_ref[:, d:] = (d_pa * hv * _gelu_grad(lg)).astype(BF16)

    vec = _vec_spec(d)
    wspec = pl.BlockSpec((N_LRU_BLOCKS, LANES, LANES), lambda b, i: (0, 0, 0))
    tile = lambda col: pl.BlockSpec((None, ts, d), lambda b, i: (b, rev(i), col))
    prev8 = lambda col: pl.BlockSpec((None, SUBLANES, d), lambda b, i: (b, jnp.maximum(rev(i) * per - 1, 0), col))
    return _run(
        body, (pm, pm, pm, h, h, d_pa, cw, cb, wa, ba, wx, bx, lam), bg, name=name, grid=(bsz, nt),
        in_specs=[tile(0), prev8(0), tile(1), tile(0), prev8(0), tile(0),
                  pl.BlockSpec((4, d), lambda b, i: (0, 0)), vec, wspec, vec, wspec, vec, vec],
        out_specs=[pl.BlockSpec((None, ts, 2 * d), lambda b, i: (b, rev(i), 0)), wspec, wspec,
                   pl.BlockSpec((SUBLANES, d), lambda b, i: (0, 0))],
        out_shape=[jax.ShapeDtypeStruct((bsz, seq, 2 * d), BF16),
                   jax.ShapeDtypeStruct((N_LRU_BLOCKS, LANES, LANES), F32),
                   jax.ShapeDtypeStruct((N_LRU_BLOCKS, LANES, LANES), F32),
                   jax.ShapeDtypeStruct((SUBLANES, d), F32)],
        scratch_shapes=[pltpu.VMEM((1, d), F32), pltpu.VMEM((SUBLANES, d), F32),
                        pltpu.VMEM((ts, d), F32), pltpu.VMEM((ts, d), F32), pltpu.VMEM((ts, d), F32)])


L = SSD_CHUNK
N_PAIRS = SSD_HEADS // 2


def _ssd_common(xbc, prev8, dt_raw, cw, cb, dtb, alog):
    conv = _conv4(xbc, prev8, cw, cb)
    sg = _sigmoid(conv)
    xa = conv * sg
    dtv = _softplus(dt_raw + dtb)
    a_neg = -jnp.exp(alog)
    rowi = lax.broadcasted_iota(jnp.int32, (L, L), 0)
    coli = lax.broadcasted_iota(jnp.int32, (L, L), 1)
    tril = (rowi >= coli).astype(F32)
    cs = _dot(tril, dtv * a_neg, precision=HI)
    return conv, sg, xa, dtv, a_neg, cs, rowi, coli


def _head_masks():
    lane = lax.broadcasted_iota(jnp.int32, (L, LANES), 1)
    return lane < SSD_HEAD_DIM


def _spread(v, p, first):
    return jnp.where(first[:v.shape[0]], v[:, 2 * p:2 * p + 1], v[:, 2 * p + 1:2 * p + 2])


def _place_head_sums(acc, z, p, first, lane1):
    rows = z.shape[0]
    s0 = jnp.sum(jnp.where(first[:rows], z, 0.0), axis=1, keepdims=True)
    s1 = jnp.sum(jnp.where(first[:rows], 0.0, z), axis=1, keepdims=True)
    lane = lane1[:rows]
    return acc + jnp.where(lane == 2 * p, s0, 0.0) + jnp.where(lane == 2 * p + 1, s1, 0.0)


def _stack_heads(v, first):
    return jnp.concatenate([jnp.where(first, v, 0.0), jnp.where(first, 0.0, v)], axis=0).astype(BF16)


def ssd_fwd(pm, dtr, cw, cb, dtb, alog, d_lanes, nw, *, name, bg=None):
    bsz, seq, _ = pm.shape
    nc = seq // L
    inner, cdim = SSD_INNER, SSD_CONV_DIM

    def body(xbc_ref, xp_ref, z_ref, dt_ref, cw_ref, cb_ref, dtb_ref, alog_ref, dl_ref, nw_ref,
             y_ref, yn_ref, st_ref, state):
        i = pl.program_id(1)

        @pl.when(i == 0)
        def _():
            state[...] = jnp.zeros_like(state)

        prev8 = jnp.where(i == 0, 0.0, xp_ref[...])
        _, _, xa, dtv, _, cs, rowi, coli = _ssd_common(
            xbc_ref[...], prev8, dt_ref[...], cw_ref[...], cb_ref[...], dtb_ref[...], alog_ref[...])
        cst = cs.T
        causal = rowi >= coli
        first = _head_masks()
        for g in range(SSD_GROUPS):
            bg = xa[:, inner + SSD_STATE * g:inner + SSD_STATE * (g + 1)].astype(BF16)
            cg = xa[:, inner + SSD_GROUPS * SSD_STATE + SSD_STATE * g:
                    inner + SSD_GROUPS * SSD_STATE + SSD_STATE * (g + 1)].astype(BF16)
            cbm = _dot_nt(cg, bg)
            for pp in range(2):
                p = 2 * g + pp
                sl = slice(LANES * p, LANES * (p + 1))
                ms = []
                for hh in (2 * p, 2 * p + 1):
                    seg = cs[:, hh:hh + 1] - cst[hh:hh + 1, :]
                    ms.append((cbm * jnp.exp(jnp.where(causal, seg, -jnp.inf))).astype(BF16))
                xsp = xa[:, sl]
                cs_p = _spread(cs, p, first)
                cs_last = cs_p[L - 1:L]
                xp = xsp * _spread(dtv, p, first)
                y_diag = _dot(jnp.concatenate(ms, axis=1), _stack_heads(xp, first))
                st = state[p]
                st_ref[p] = st
                y_off = _dot(cg, st.astype(BF16)) * jnp.exp(cs_p)
                y_ref[:, sl] = y_diag + y_off + dl_ref[:, sl] * xsp
                state[p] = st * jnp.exp(cs_last) + _dot_tn(bg, (xp * jnp.exp(cs_last - cs_p)).astype(BF16))
        zv = z_ref[...]
        yz = y_ref[...] * (zv * _sigmoid(zv))
        gw = inner // SSD_GROUPS
        for g in range(SSD_GROUPS):
            sl = slice(gw * g, gw * (g + 1))
            seg = yz[:, sl]
            yn_ref[:, sl] = ((seg * _rstd(seg)) * nw_ref[:, sl]).astype(BF16)

    cvec = lambda w: pl.BlockSpec((1, w), lambda b, i: (0, 0))
    outs, bg_outs = _run(
        body, (pm, pm, pm, dtr, cw, cb, dtb, alog, d_lanes, nw), bg, name=name, grid=(bsz, nc),
        in_specs=[pl.BlockSpec((None, L, cdim), lambda b, i: (b, i, 1)), _prev8_spec(cdim, 1, L),
                  pl.BlockSpec((None, L, inner), lambda b, i: (b, i, 1)),
                  pl.BlockSpec((None, L, LANES), lambda b, i: (b, i, 0)),
                  pl.BlockSpec((4, cdim), lambda b, i: (0, 0)), cvec(cdim), cvec(LANES), cvec(LANES),
                  cvec(inner), cvec(inner)],
        out_specs=[pl.BlockSpec((None, L, inner), lambda b, i: (b, i, 0)),
                   pl.BlockSpec((None, L, inner), lambda b, i: (b, i, 0)),
                   pl.BlockSpec((None, None, N_PAIRS, SSD_STATE, LANES), lambda b, i: (b, i, 0, 0, 0))],
        out_shape=[jax.ShapeDtypeStruct((bsz, seq, inner), F32), jax.ShapeDtypeStruct((bsz, seq, inner), BF16),
                   jax.ShapeDtypeStruct((bsz, nc, N_PAIRS, SSD_STATE, LANES), F32)],
        scratch_shapes=[pltpu.VMEM((N_PAIRS, SSD_STATE, LANES), F32)])
    return outs, bg_outs


def ssd_bwd(pm, dtr, y, states, d_yn, cw, cb, dtb, alog, d_lanes, nw, *, name):
    bsz, seq, _ = pm.shape
    nc = seq // L
    inner, cdim = SSD_INNER, SSD_CONV_DIM
    per = L // SUBLANES

    def rev(i):
        return nc - 1 - i

    def body(xbc_ref, xp_ref, z_ref, dt_ref, y_ref, st_ref, dyn_ref, cw_ref, cb_ref, dtb_ref, alog_ref,
             dl_ref, nw_ref, ds_ref, ddt_ref, r4_ref, r2_ref, r1_ref,
             dstate, dconv_next, dxs_s, dbc_s):
        b, i = pl.program_id(0), pl.program_id(1)
        t = rev(i)

        @pl.when(i == 0)
        def _():
            dstate[...] = jnp.zeros_like(dstate)
            dconv_next[...] = jnp.zeros_like(dconv_next)

        @pl.when((b == 0) & (i == 0))
        def _():
            r4_ref[...] = jnp.zeros_like(r4_ref)
            r2_ref[...] = jnp.zeros_like(r2_ref)
            r1_ref[...] = jnp.zeros_like(r1_ref)

        xbc = xbc_ref[...]
        prev8 = jnp.where(t == 0, 0.0, xp_ref[...])
        cwv = cw_ref[...]
        dt_in = dt_ref[...] + dtb_ref[...]
        conv, sg, xa, dtv, a_neg, cs, rowi, coli = _ssd_common(
            xbc, prev8, dt_ref[...], cwv, cb_ref[...], dtb_ref[...], alog_ref[...])
        cst = cs.T
        causal = rowi >= coli
        anti = coli >= rowi
        first = _head_masks()
        lane1 = lax.broadcasted_iota(jnp.int32, (L, LANES), 1)

        yv = y_ref[...]
        zv = z_ref[...]
        sz = _sigmoid(zv)
        zs = zv * sz
        yz = yv * zs
        dyn = dyn_ref[...]
        gw = inner // SSD_GROUPS
        d_yz_parts, d_nw_parts = [], []
        for g in range(SSD_GROUPS):
            sl = slice(gw * g, gw * (g + 1))
            seg = yz[:, sl]
            r = _rstd(seg)
            n = seg * r
            d_nw_parts.append(_colsum(dyn[:, sl] * n))
            d_yz_parts.append(_rms_bwd(dyn[:, sl] * nw_ref[:, sl], n, r))
        d_yz = jnp.concatenate(d_yz_parts, axis=1)
        d_y = d_yz * zs
        ds_ref[:, :inner] = (d_yz * yv * (sz * (1.0 + zv * (1.0 - sz)))).astype(BF16)

        a1 = jnp.zeros((L, LANES), F32)
        a2 = jnp.zeros((L, LANES), F32)
        xs_dxt = jnp.zeros((L, LANES), F32)
        c0 = jnp.zeros((1, LANES), F32)
        d_dl = jnp.zeros((1, LANES), F32)
        for g in range(SSD_GROUPS):
            bsl = slice(inner + SSD_STATE * g, inner + SSD_STATE * (g + 1))
            csl = slice(inner + SSD_GROUPS * SSD_STATE + SSD_STATE * g,
                        inner + SSD_GROUPS * SSD_STATE + SSD_STATE * (g + 1))
            bg = xa[:, bsl].astype(BF16)
            cg = xa[:, csl].astype(BF16)
            cbm = _dot_nt(cg, bg)
            cbt = _dot_nt(bg, cg)
            d_cb = jnp.zeros((L, L), F32)
            d_bg = jnp.zeros((L, SSD_STATE), F32)
            d_cg = jnp.zeros((L, SSD_STATE), F32)
            for pp in range(2):
                p = 2 * g + pp
                sl = slice(LANES * p, LANES * (p + 1))
                xsp = xa[:, sl]
                dt_p = _spread(dtv, p, first)
                cs_p = _spread(cs, p, first)
                cs_last = cs_p[L - 1:L]
                e_p = jnp.exp(cs_p)
                w_p = jnp.exp(cs_last - cs_p)
                e_last = jnp.exp(cs_last)
                xp = xsp * dt_p
                xpb = xp.astype(BF16)
                dyp = d_y[:, sl]
                dypb = dyp.astype(BF16)
                dy_heads = (jnp.where(first, dyp, 0.0).astype(BF16), jnp.where(first, 0.0, dyp).astype(BF16))
                x_heads = (jnp.where(first, xp, 0.0).astype(BF16), jnp.where(first, 0.0, xp).astype(BF16))
                mts = []
                for k, hh in enumerate((2 * p, 2 * p + 1)):
                    col = cs[:, hh:hh + 1]
                    row = cst[hh:hh + 1, :]
                    dec = jnp.exp(jnp.where(causal, col - row, -jnp.inf))
                    dec_t = jnp.exp(jnp.where(anti, row - col, -jnp.inf))
                    gd = _dot_nt(dy_heads[k], xpb) * dec
                    d_cb = d_cb + gd
                    mt = cbt * dec_t
                    qd = gd * cbm - _dot_nt(x_heads[k], dypb) * mt
                    a1 = a1 + jnp.where(lane1 == hh, jnp.sum(qd, axis=1, keepdims=True), 0.0)
                    mts.append(mt.astype(BF16))
                dst = dstate[p]
                dstb = dst.astype(BF16)
                st = st_ref[p]
                stb = st.astype(BF16)
                dye = (dyp * e_p).astype(BF16)
                xw = (xp * w_p).astype(BF16)
                dx_off = w_p * _dot(bg, dstb)
                d_xp = _dot(jnp.concatenate(mts, axis=1), jnp.concatenate(dy_heads, axis=0)) + dx_off
                dxs_s[:, sl] = d_xp * dt_p + dyp * dl_ref[:, sl]
                a1 = _place_head_sums(a1, dyp * (_dot(cg, stb) * e_p), p, first, lane1)
                a2 = _place_head_sums(a2, xp * dx_off, p, first, lane1)
                xs_dxt = _place_head_sums(xs_dxt, d_xp * xsp, p, first, lane1)
                c0 = _place_head_sums(c0, _colsum(dst * st) * e_last, p, first, lane1)
                d_dl = _place_head_sums(d_dl, _colsum(dyp * xsp), p, first, lane1)
                d_cg = d_cg + _dot_nt(dye, stb)
                d_bg = d_bg + _dot_nt(xw, dstb)
                dstate[p] = dst * e_last + _dot_tn(cg, dye)
            d_cbb = d_cb.astype(BF16)
            dbc_s[:, SSD_STATE * g:SSD_STATE * (g + 1)] = d_bg + _dot_tn(d_cbb, cg)
            dbc_s[:, SSD_GROUPS * SSD_STATE + SSD_STATE * g:SSD_GROUPS * SSD_STATE + SSD_STATE * (g + 1)] = (
                d_cg + _dot(d_cbb, bg))

        d_da = (_dot(anti.astype(F32), a1, precision=HI) + _dot((rowi > coli).astype(F32), a2, precision=HI) + c0)
        d_dt = d_da * a_neg + xs_dxt
        d_alog = _colsum(d_da * dtv) * a_neg
        d_dtr = jnp.where(lane1 < SSD_HEADS, d_dt * _sigmoid(dt_in), 0.0)
        ddt_ref[...] = d_dtr.astype(BF16)
        d_xa = jnp.concatenate([dxs_s[...], dbc_s[...]], axis=1)
        d_conv = d_xa * (sg * (1.0 + conv * (1.0 - sg)))
        d_xbc, d_cw, d_cbias = _conv4_bwd(d_conv, dconv_next[...], xbc, cwv)
        dconv_next[...] = d_conv[0:SUBLANES]
        ds_ref[:, inner:] = d_xbc.astype(BF16)
        r4_ref[...] += _stack_rows([d_cbias] + d_cw, cdim)
        r2_ref[...] += _stack_rows([jnp.concatenate(d_nw_parts, axis=1)], inner)
        r1_ref[...] += _stack_rows([_colsum(d_dtr), d_alog, d_dl], LANES)

    cvec = lambda w: pl.BlockSpec((1, w), lambda b, i: (0, 0))
    return _pcall(
        body, name=name, grid=(bsz, nc),
        in_specs=[pl.BlockSpec((None, L, cdim), lambda b, i: (b, rev(i), 1)),
                  pl.BlockSpec((None, SUBLANES, cdim), lambda b, i: (b, jnp.maximum(rev(i) * per - 1, 0), 1)),
                  pl.BlockSpec((None, L, inner), lambda b, i: (b, rev(i), 1)),
                  pl.BlockSpec((None, L, LANES), lambda b, i: (b, rev(i), 0)),
                  pl.BlockSpec((None, L, inner), lambda b, i: (b, rev(i), 0)),
                  pl.BlockSpec((None, None, N_PAIRS, SSD_STATE, LANES), lambda b, i: (b, rev(i), 0, 0, 0)),
                  pl.BlockSpec((None, L, inner), lambda b, i: (b, rev(i), 0)),
                  pl.BlockSpec((4, cdim), lambda b, i: (0, 0)), cvec(cdim), cvec(LANES), cvec(LANES),
                  cvec(inner), cvec(inner)],
        out_specs=[pl.BlockSpec((None, L, inner + cdim), lambda b, i: (b, rev(i), 0)),
                   pl.BlockSpec((None, L, LANES), lambda b, i: (b, rev(i), 0)),
                   pl.BlockSpec((SUBLANES, cdim), lambda b, i: (0, 0)),
                   pl.BlockSpec((SUBLANES, inner), lambda b, i: (0, 0)),
                   pl.BlockSpec((SUBLANES, LANES), lambda b, i: (0, 0))],
        out_shape=[jax.ShapeDtypeStruct((bsz, seq, inner + cdim), BF16),
                   jax.ShapeDtypeStruct((bsz, seq, LANES), BF16),
                   jax.ShapeDtypeStruct((SUBLANES, cdim), F32),
                   jax.ShapeDtypeStruct((SUBLANES, inner), F32),
                   jax.ShapeDtypeStruct((SUBLANES, LANES), F32)],
        scratch_shapes=[pltpu.VMEM((N_PAIRS, SSD_STATE, LANES), F32), pltpu.VMEM((SUBLANES, cdim), F32),
                        pltpu.VMEM((L, inner), F32), pltpu.VMEM((L, 2 * SSD_GROUPS * SSD_STATE), F32)],
    )(pm, pm, pm, dtr, y, states, d_yn, cw, cb, dtb, alog, d_lanes, nw)


def _lru_block_weights(w):
    w = w.reshape(N_LRU_BLOCKS, 2, LRU_HEAD_DIM, LRU_HEAD_DIM)
    z = jnp.zeros((N_LRU_BLOCKS, LRU_HEAD_DIM, LRU_HEAD_DIM), w.dtype)
    top = jnp.concatenate([w[:, 0], z], axis=2)
    bot = jnp.concatenate([z, w[:, 1]], axis=2)
    return jnp.concatenate([top, bot], axis=1).astype(BF16)


def _lru_block_grads(g):
    h = LRU_HEAD_DIM
    return jnp.stack([g[:, :h, :h], g[:, h:, h:]], axis=1).reshape(LRU_HEADS, h, h)


def _pad_lanes(v, width=LANES):
    return jnp.pad(v, ((0, 0), (0, width - v.shape[1])))


class NoExchange:
    def __init__(self, weights):
        self._weights, self.grads = weights, {}

    def weights_bg(self):
        return None

    def weights(self, bg_outs):
        return self._weights

    def grads_bg(self, grads):
        self.grads.update(grads)
        return None

    def grads_done(self, bg_outs):
        pass


def local_step(x, target, mod, big, small, plan):
    bsz, seq, d = x.shape
    t = bsz * seq
    flat = lambda v: v.reshape(t, v.shape[-1])
    unflat = lambda v: v.reshape(bsz, seq, v.shape[-1])

    wa_b = _lru_block_weights(small["lru_wa"])
    wx_b = _lru_block_weights(small["lru_wx"])
    dtb = _pad_lanes(small["ssd_dt_bias"])
    alog = _pad_lanes(small["ssd_a_log"])
    d_lanes = jnp.repeat(small["ssd_d"], SSD_HEAD_DIM, axis=1)

    lru_cols = 2 * D_MODEL
    wt = {"lru": big["w_main"][:, :lru_cols].T, "ssd": big["w_main"][:, lru_cols:].T, "gates": big["w_gates"].T,
          "dt": big["w_dt"].T}

    h1 = prenorm(x, small["pre_norm1"], mod, name="prenorm1")
    h1f = flat(h1)
    pm = unflat(mm_nn([(h1f, big["w_main"])], name="in_proj_main"))
    gates = unflat(mm_nn([(h1f, big["w_gates"])], name="in_proj_gates"))
    dtr = unflat(mm_nn([(h1f, big["w_dt"])], name="in_proj_dt"))
    lru_args = (small["lru_conv_w"], small["lru_conv_b"], wa_b, small["lru_ba"], wx_b, small["lru_bx"],
                small["lru_lambda"])
    h_lru, pa_in = lru_fwd(pm, *lru_args, name="lru_fwd")
    ssd_args = (small["ssd_conv_w"], small["ssd_conv_b"], dtb, alog, d_lanes, small["ssd_norm_w"])
    (y_ssd, ynorm, states), arrived = ssd_fwd(pm, dtr, *ssd_args, name="ssd_fwd", bg=plan.weights_bg())
    big = dict(big, **plan.weights(arrived))
    for n in ("w_pa", "w_pb", "w_out", "w_ff1", "w_ff2"):
        wt[n] = big[n].T
    ya = unflat(mm_nn([(flat(pa_in), big["w_pa"])], name="proj_a"))
    yb = unflat(mm_nn([(flat(ynorm), big["w_pb"])], name="proj_b"))
    merged = merge_fwd(ya, yb, gates, small["b_gate"], name="merge_fwd")
    out1 = unflat(mm_nn([(flat(merged), big["w_out"])], name="proj_out"))
    x1, h2 = post1_pre2(x, out1, mod, small["post_norm1"], small["pre_norm2"], name="post1_pre2")
    f = mm_nn([(flat(h2), big["w_ff1"])], name="ff1")
    y2 = unflat(mm_nn([(f, big["w_ff2"])], a_fn=_relu_sq, name="ff2"))

    dx2, d_y2, pb_a, gl_a = final_bwd(x1, y2, target, mod, small["post_norm2"], name="final_bwd")
    d_y2f = flat(d_y2)
    d_f = mm_nn([(d_y2f, wt["w_ff2"])], out_dtype=BF16, extra=f,
                epi=lambda r, fv: r * (2.0 * jnp.maximum(fv, 0.0)), name="ff2_dx")
    g_ff2 = mm_tn(f, d_y2f, a_fn=_relu_sq, name="ff2_dw")
    d_h2 = unflat(mm_nn([(d_f, wt["w_ff1"])], name="ff1_dx"))
    g_ff1 = mm_tn(flat(h2), d_f, name="ff1_dw")
    dx1, d_out1, pb_b, gl_b = mid_bwd(d_h2, dx2, x1, out1, mod, small["pre_norm2"], small["post_norm1"],
                                      name="mid_bwd")
    d_out1f = flat(d_out1)
    d_merged = unflat(mm_nn([(d_out1f, wt["w_out"])], name="out_dx"))
    g_out = mm_tn(flat(merged), d_out1f, name="out_dw")
    d_ya, d_yb, d_gates, gl_c = merge_bwd(d_merged, ya, yb, gates, small["b_gate"], name="merge_bwd")
    d_pa = unflat(mm_nn([(flat(d_ya), wt["w_pa"])], name="pa_dx"))
    g_pa = mm_tn(flat(pa_in), flat(d_ya), name="pa_dw")
    d_yn = unflat(mm_nn([(flat(d_yb), wt["w_pb"])], name="pb_dx"))
    g_pb = mm_tn(flat(ynorm), flat(d_yb), name="pb_dw")
    leaving = plan.grads_bg({"w_pa": g_pa, "w_pb": g_pb, "w_out": g_out, "w_ff1": g_ff1, "w_ff2": g_ff2})
    (d_l, g_wa_b, g_wx_b, lru_rows), landed = lru_bwd(pm, h_lru, d_pa, *lru_args, name="lru_bwd", bg=leaving)
    plan.grads_done(landed)
    d_s, d_dt, r4, r2, r1 = ssd_bwd(pm, dtr, y_ssd, states, d_yn, *ssd_args, name="ssd_bwd")
    d_lf, d_sf, d_gf, d_dtf = flat(d_l), flat(d_s), flat(d_gates), flat(d_dt)
    g_in = jnp.concatenate([
        mm_tn(h1f, d_lf, name="in_dw_lru"), mm_tn(h1f, d_sf, name="in_dw_ssd"),
        mm_tn(h1f, d_dtf, name="in_dw_dt")[:, :SSD_HEADS], mm_tn(h1f, d_gf, name="in_dw_gates")], axis=1)
    leaving = plan.grads_bg({"w_in": g_in})
    d_h1 = mm_nn([(d_lf, wt["lru"]), (d_gf, wt["gates"]), (d_dtf, wt["dt"])], name="in_dx_lru_gates")
    if leaving is None:
        d_h1 = mm_nn([(d_sf, wt["ssd"])], add=d_h1, name="in_dx_ssd")
    else:
        d_h1, landed = mm_nn([(d_sf, wt["ssd"])], add=d_h1, name="in_dx_ssd", bg=leaving)
        plan.grads_done(landed)
    grad_x, pb_c, gl_d = first_bwd(unflat(d_h1), dx1, x, mod, small["pre_norm1"], name="first_bwd")

    d_mod = jnp.stack([pb_c[:, 0], pb_c[:, 1], pb_b[:, 2], pb_b[:, 0], pb_b[:, 1], pb_a[:, 0]], axis=1)
    loss_cols = gl_a[1:2]
    nh = SSD_HEADS
    small_grads = {
        "pre_norm1": gl_d[0:1], "post_norm1": gl_b[1:2], "b_gate": gl_c[0:1],
        "lru_conv_w": lru_rows[4:8], "lru_conv_b": lru_rows[3:4],
        "lru_wa": _lru_block_grads(g_wa_b), "lru_ba": lru_rows[0:1],
        "lru_wx": _lru_block_grads(g_wx_b), "lru_bx": lru_rows[1:2], "lru_lambda": lru_rows[2:3],
        "ssd_conv_w": r4[1:5], "ssd_conv_b": r4[0:1],
        "ssd_dt_bias": r1[0:1, :nh], "ssd_a_log": r1[1:2, :nh], "ssd_d": r1[2:3, :nh],
        "ssd_norm_w": r2[0:1], "pre_norm2": gl_b[0:1], "post_norm2": gl_a[0:1],
    }
    return loss_cols, grad_x, d_mod, small_grads


def _position():
    return lax.axis_index("x"), lax.axis_index("y"), lax.axis_index("c")


def _other_chips(x, y):
    return [(1 - x, y), (x, 1 - y), (1 - x, 1 - y)]


def allgather8(v, *, name):
    m_per, n = v.shape

    def body(x_ref, out_ref, send_sems, recv_sems, local_sem):
        x, y, c = _position()
        me, sibling = (x, y, c), (x, y, 1 - c)
        chips = _other_chips(x, y)

        def rows(px, py, pc):
            return out_ref.at[pl.ds((4 * px + 2 * py + pc) * m_per, m_per), :]

        def copy(k, block, to, src=None):
            return pltpu.make_async_remote_copy(
                src_ref=rows(*block) if src is None else src, dst_ref=rows(*block),
                send_sem=send_sems.at[k], recv_sem=recv_sems.at[k], device_id=to, device_id_type=MESH)

        mine = pltpu.make_async_copy(x_ref, rows(*me), local_sem)
        mine.start()
        first = [copy(0, me, sibling, src=x_ref)]
        first += [copy(1 + j, me, (*chip, c), src=x_ref) for j, chip in enumerate(chips)]
        for cp in first:
            cp.start()
        passed = [copy(4 + j, (*chip, c), sibling) for j, chip in enumerate(chips)]
        for j, chip in enumerate(chips):
            copy(1 + j, (*chip, c), me).wait_recv()
            passed[j].start()
        copy(0, sibling, me).wait_recv()
        for j, chip in enumerate(chips):
            copy(4 + j, (*chip, 1 - c), me).wait_recv()
        for cp in first + passed:
            cp.wait_send()
        mine.wait()

    return _pcall(
        body, name=name,
        out_shape=jax.ShapeDtypeStruct((N_DEV * m_per, n), v.dtype),
        in_specs=[pl.BlockSpec(memory_space=pltpu.VMEM)],
        out_specs=pl.BlockSpec(memory_space=pltpu.VMEM),
        scratch_shapes=[pltpu.SemaphoreType.DMA((7,)), pltpu.SemaphoreType.DMA((7,)), pltpu.SemaphoreType.DMA],
    )(v)


def gather_weights(shards, *, name):
    n = len(shards)
    half = [s.shape[0] // 2 for s in shards]

    def body(*refs):
        ins, outs = refs[:n], refs[n:2 * n]
        send_sems, recv_sems, local_sems = refs[2 * n:]
        x, y, c = _position()
        me_chip = 2 * x + y
        chips = _other_chips(x, y)

        def piece(w, chip, core):
            return outs[w].at[chip, pl.ds(core * half[w], half[w]), :]

        def copy(w, k, chip, core, to, src=None):
            dst = piece(w, chip, core)
            return pltpu.make_async_remote_copy(
                src_ref=dst if src is None else src, dst_ref=dst,
                send_sem=send_sems.at[6 * w + k], recv_sem=recv_sems.at[6 * w + k], device_id=to, device_id_type=MESH)

        local = [pltpu.make_async_copy(ins[w], outs[w].at[me_chip], local_sems.at[w]) for w in range(n)]
        for cp in local:
            cp.start()
        sent = []
        for w in range(n):
            for j, (px, py) in enumerate(chips):
                cp = copy(w, j, me_chip, c, (px, py, c), src=ins[w].at[pl.ds(c * half[w], half[w]), :])
                cp.start()
                sent.append(cp)
        for w in range(n):
            for j, (px, py) in enumerate(chips):
                copy(w, j, 2 * px + py, c, (px, py, c)).wait_recv()
                cp = copy(w, 3 + j, 2 * px + py, c, (x, y, 1 - c))
                cp.start()
                sent.append(cp)
        for w in range(n):
            for j, (px, py) in enumerate(chips):
                copy(w, 3 + j, 2 * px + py, 1 - c, (x, y, 1 - c)).wait_recv()
        for cp in sent:
            cp.wait_send()
        for cp in local:
            cp.wait()

    return _pcall(
        body, name=name,
        out_shape=[jax.ShapeDtypeStruct((N_CHIPS,) + s.shape, s.dtype) for s in shards],
        in_specs=[ANY] * n, out_specs=[ANY] * n,
        scratch_shapes=[pltpu.SemaphoreType.DMA((6 * n,)), pltpu.SemaphoreType.DMA((6 * n,)),
                        pltpu.SemaphoreType.DMA((n,))],
    )(*shards)


STAGE_BYTES = 2 << 20


def _stage_rows(rows, width, itemsize=4):
    return _pick(rows, tuple(t for t in (1024, 512, 256, 128, 64, 32, 16, 8) if t * width * itemsize <= STAGE_BYTES * 3 // 2))


def _staged(chunks, bufs, load_sems):
    count, pending = {}, {}

    def load(i):
        cls, src, _ = chunks[i]
        slot = count.get(cls, 0) % 2
        count[cls] = count.get(cls, 0) + 1
        for cp, remote in pending.pop((cls, slot), []):
            if remote:
                cp.wait_send()
            else:
                cp.wait()
        staged = bufs[cls].at[slot, pl.ds(0, src.shape[0]), :]
        ld = pltpu.make_async_copy(src, staged, load_sems[cls].at[slot])
        ld.start()
        return ld, cls, slot, staged

    cur = load(0)
    for i in range(len(chunks)):
        nxt = load(i + 1) if i + 1 < len(chunks) else None
        ld, cls, slot, staged = cur
        ld.wait()
        started = []
        for make in chunks[i][2]:
            cp, remote = make(staged, slot)
            cp.start()
            started.append((cp, remote))
        pending[(cls, slot)] = started
        cur = nxt
    for started in pending.values():
        for cp, remote in started:
            if remote:
                cp.wait_send()
            else:
                cp.wait()


def _stage_scratch(widths_rows, dtype):
    scratch = []
    for width, rows in widths_rows:
        scratch += [pltpu.VMEM((2, rows, width), dtype), pltpu.SemaphoreType.DMA((2,)), pltpu.SemaphoreType.DMA((2,)),
                    pltpu.SemaphoreType.DMA((2,))]
    return scratch


def send_half_to_sibling(grads, *, name):
    n = len(grads)
    half = [g.shape[1] // 2 for g in grads]
    widths = sorted({g.shape[2] for g in grads})
    chunk_rows = [_stage_rows(h, g.shape[2]) for g, h in zip(grads, half)]
    plan = [(w, k, r0) for w in range(n) for k in range(N_CHIPS) for r0 in range(0, half[w], chunk_rows[w])]

    def body(*refs):
        ins, theirs = refs[:n], refs[n:2 * n]
        recv_sems = refs[2 * n]
        stage = refs[2 * n + 1:]
        bufs = {wd: stage[4 * i] for i, wd in enumerate(widths)}
        load_sems = {wd: stage[4 * i + 1] for i, wd in enumerate(widths)}
        send_sems = {wd: stage[4 * i + 2] for i, wd in enumerate(widths)}
        x, y, c = _position()
        chunks = []
        for idx, (w, k, r0) in enumerate(plan):
            wd = grads[w].shape[2]
            rb = chunk_rows[w]

            def make(staged, slot, idx=idx, w=w, k=k, r0=r0, wd=wd, rb=rb):
                return pltpu.make_async_remote_copy(
                    src_ref=staged, dst_ref=theirs[w].at[k, pl.ds(r0, rb), :], send_sem=send_sems[wd].at[slot],
                    recv_sem=recv_sems.at[idx], device_id=(x, y, 1 - c), device_id_type=MESH), True

            chunks.append((wd, ins[w].at[k, pl.ds((1 - c) * half[w] + r0, rb), :], [make]))
        _staged(chunks, bufs, load_sems)
        for idx, (w, k, r0) in enumerate(plan):
            wd = grads[w].shape[2]
            landed = theirs[w].at[k, pl.ds(r0, chunk_rows[w]), :]
            pltpu.make_async_remote_copy(
                src_ref=landed, dst_ref=landed, send_sem=send_sems[wd].at[0], recv_sem=recv_sems.at[idx],
                device_id=(x, y, 1 - c), device_id_type=MESH).wait_recv()

    stage_rows = [(wd, max(r for g, r in zip(grads, chunk_rows) if g.shape[2] == wd)) for wd in widths]
    return _pcall(
        body, name=name,
        out_shape=[jax.ShapeDtypeStruct((N_CHIPS, h, g.shape[2]), g.dtype) for g, h in zip(grads, half)],
        in_specs=[ANY] * n, out_specs=[ANY] * n,
        scratch_shapes=[pltpu.SemaphoreType.DMA((len(plan),))] + _stage_scratch(stage_rows, F32),
    )(*grads)


def _chip_exchange_background(arrays, out_shapes, src_of, dst_of, landed_of, own_of):
    n = len(arrays)

    def copies(ins, outs, scr):
        send_sems, recv_sems, local_sems = scr
        x, y, c = _position()
        me_chip = 2 * x + y
        local, sends, recvs = [], [], []
        for w in range(n):
            local.append(pltpu.make_async_copy(*own_of(ins[w], outs[w], w, me_chip), local_sems.at[w]))
            for j, (px, py) in enumerate(_other_chips(x, y)):
                sems = dict(send_sem=send_sems.at[3 * w + j], recv_sem=recv_sems.at[3 * w + j],
                            device_id=(px, py, c), device_id_type=MESH)
                sends.append(pltpu.make_async_remote_copy(
                    src_ref=src_of(ins[w], w, 2 * px + py, me_chip, c), dst_ref=dst_of(outs[w], w, me_chip, c), **sems))
                landed = landed_of(outs[w], w, 2 * px + py, c)
                recvs.append(pltpu.make_async_remote_copy(src_ref=landed, dst_ref=landed, **sems))
        return local, sends, recvs

    def start(ins, outs, scr):
        local, sends, _ = copies(ins, outs, scr)
        for cp in local + sends:
            cp.start()

    def finish(ins, outs, scr):
        local, sends, recvs = copies(ins, outs, scr)
        for cp in recvs:
            cp.wait_recv()
        for cp in sends:
            cp.wait_send()
        for cp in local:
            cp.wait()

    scratch = [pltpu.SemaphoreType.DMA((3 * n,)), pltpu.SemaphoreType.DMA((3 * n,)), pltpu.SemaphoreType.DMA((n,))]
    return Background(arrays, out_shapes, scratch, start, finish)


def scatter_background(parts):
    return _chip_exchange_background(
        parts, [jax.ShapeDtypeStruct(p.shape, p.dtype) for p in parts],
        src_of=lambda ref, w, peer, me, c: ref.at[peer], dst_of=lambda ref, w, me, c: ref.at[me],
        landed_of=lambda ref, w, peer, c: ref.at[peer], own_of=lambda i, o, w, me: (i.at[me], o.at[me]))


def gather_halves_background(shards):
    half = [s.shape[0] // 2 for s in shards]
    rows = lambda w, c: pl.ds(c * half[w], half[w])
    return _chip_exchange_background(
        shards, [jax.ShapeDtypeStruct((N_CHIPS,) + s.shape, s.dtype) for s in shards],
        src_of=lambda ref, w, peer, me, c: ref.at[rows(w, c), :], dst_of=lambda ref, w, me, c: ref.at[me, rows(w, c), :],
        landed_of=lambda ref, w, peer, c: ref.at[peer, rows(w, c), :], own_of=lambda i, o, w, me: (i, o.at[me]))


def fill_other_half(gathered, *, name):
    n = len(gathered)
    half = [g.shape[1] // 2 for g in gathered]
    widths = sorted({g.shape[2] for g in gathered})
    chunk_rows = [_stage_rows(h, g.shape[2], itemsize=2) for g, h in zip(gathered, half)]
    plan = [(w, j, r0) for w in range(n) for j in range(N_CHIPS - 1) for r0 in range(0, half[w], chunk_rows[w])]

    def body(*refs):
        ins, outs = refs[:n], refs[n:2 * n]
        recv_sems = refs[2 * n]
        stage = refs[2 * n + 1:]
        bufs = {wd: stage[4 * i] for i, wd in enumerate(widths)}
        load_sems = {wd: stage[4 * i + 1] for i, wd in enumerate(widths)}
        send_sems = {wd: stage[4 * i + 2] for i, wd in enumerate(widths)}
        x, y, c = _position()
        chips = _other_chips(x, y)
        chunks = []
        for idx, (w, j, r0) in enumerate(plan):
            wd, rb = gathered[w].shape[2], chunk_rows[w]
            k = 2 * chips[j][0] + chips[j][1]

            def make(staged, slot, idx=idx, w=w, k=k, r0=r0, wd=wd, rb=rb):
                return pltpu.make_async_remote_copy(
                    src_ref=staged, dst_ref=outs[w].at[k, pl.ds(c * half[w] + r0, rb), :],
                    send_sem=send_sems[wd].at[slot], recv_sem=recv_sems.at[idx],
                    device_id=(x, y, 1 - c), device_id_type=MESH), True

            chunks.append((wd, ins[w].at[k, pl.ds(c * half[w] + r0, rb), :], [make]))
        _staged(chunks, bufs, load_sems)
        for idx, (w, j, r0) in enumerate(plan):
            wd = gathered[w].shape[2]
            k = 2 * chips[j][0] + chips[j][1]
            landed = outs[w].at[k, pl.ds((1 - c) * half[w] + r0, chunk_rows[w]), :]
            pltpu.make_async_remote_copy(
                src_ref=landed, dst_ref=landed, send_sem=send_sems[wd].at[0], recv_sem=recv_sems.at[idx],
                device_id=(x, y, 1 - c), device_id_type=MESH).wait_recv()

    stage_rows = [(wd, max(r for g, r in zip(gathered, chunk_rows) if g.shape[2] == wd)) for wd in widths]
    return _pcall(
        body, name=name, out_shape=[jax.ShapeDtypeStruct(g.shape, g.dtype) for g in gathered],
        in_specs=[ANY] * n, out_specs=[ANY] * n, input_output_aliases={w: w for w in range(n)},
        scratch_shapes=[pltpu.SemaphoreType.DMA((len(plan),))] + _stage_scratch(stage_rows, gathered[0].dtype),
    )(*gathered)


def join_with_sibling(halves, *, name):
    n = len(halves)
    widths = sorted({h.shape[1] for h in halves})
    chunk_rows = [_stage_rows(h.shape[0], h.shape[1]) for h in halves]
    plan = [(w, r0) for w in range(n) for r0 in range(0, halves[w].shape[0], chunk_rows[w])]

    def body(*refs):
        ins, outs = refs[:n], refs[n:2 * n]
        recv_sems = refs[2 * n]
        stage = refs[2 * n + 1:]
        bufs = {wd: stage[4 * i] for i, wd in enumerate(widths)}
        load_sems = {wd: stage[4 * i + 1] for i, wd in enumerate(widths)}
        send_sems = {wd: stage[4 * i + 2] for i, wd in enumerate(widths)}
        store_sems = {wd: stage[4 * i + 3] for i, wd in enumerate(widths)}
        x, y, c = _position()
        chunks = []
        for idx, (w, r0) in enumerate(plan):
            h, wd = halves[w].shape
            rb = chunk_rows[w]

            def to_sibling(staged, slot, idx=idx, w=w, r0=r0, h=h, wd=wd, rb=rb):
                return pltpu.make_async_remote_copy(
                    src_ref=staged, dst_ref=outs[w].at[pl.ds(c * h + r0, rb), :], send_sem=send_sems[wd].at[slot],
                    recv_sem=recv_sems.at[idx], device_id=(x, y, 1 - c), device_id_type=MESH), True

            def to_mine(staged, slot, w=w, r0=r0, h=h, wd=wd, rb=rb):
                return pltpu.make_async_copy(staged, outs[w].at[pl.ds(c * h + r0, rb), :], store_sems[wd].at[slot]), False

            chunks.append((wd, ins[w].at[pl.ds(r0, rb), :], [to_sibling, to_mine]))
        _staged(chunks, bufs, load_sems)
        for idx, (w, r0) in enumerate(plan):
            h, wd = halves[w].shape
            landed = outs[w].at[pl.ds((1 - c) * h + r0, chunk_rows[w]), :]
            pltpu.make_async_remote_copy(
                src_ref=landed, dst_ref=landed, send_sem=send_sems[wd].at[0], recv_sem=recv_sems.at[idx],
                device_id=(x, y, 1 - c), device_id_type=MESH).wait_recv()

    stage_rows = [(wd, max(r for h, r in zip(halves, chunk_rows) if h.shape[1] == wd)) for wd in widths]
    return _pcall(
        body, name=name,
        out_shape=[jax.ShapeDtypeStruct((2 * h.shape[0], h.shape[1]), h.dtype) for h in halves],
        in_specs=[ANY] * n, out_specs=[ANY] * n,
        scratch_shapes=[pltpu.SemaphoreType.DMA((len(plan),))] + _stage_scratch(stage_rows, F32),
    )(*halves)


def _row_tile(rows, cols, itemsize=4, budget=2 << 20):
    for t in (1024, 512, 256, 128, 64, 32, 16, 8):
        if rows % t == 0 and t * cols * itemsize <= budget:
            return t
    return rows


def add_half_to_bf16(core, full, theirs, *, name):
    k, r, c = theirs.shape
    tr = _row_tile(r, c)
    nb = r // tr

    def body(core_ref, a_ref, b_ref, o_ref):
        o_ref[...] = (a_ref[...] + b_ref[...]).astype(BF16)

    spec = pl.BlockSpec((None, tr, c), lambda i, j, core_ref: (i, j, 0))
    grid_spec = pltpu.PrefetchScalarGridSpec(
        num_scalar_prefetch=1, grid=(k, nb),
        in_specs=[pl.BlockSpec((None, tr, c), lambda i, j, core_ref: (i, core_ref[0] * nb + j, 0)), spec],
        out_specs=spec)
    return _pcall(body, name=name, grid_spec=grid_spec,
                  out_shape=jax.ShapeDtypeStruct(theirs.shape, BF16))(core, full, theirs)


def sum_blocks(v, *, name):
    k, r, c = v.shape
    tr = _row_tile(r, c * k)

    def body(v_ref, o_ref):
        acc = v_ref[0].astype(F32)
        for j in range(1, k):
            acc = acc + v_ref[j].astype(F32)
        o_ref[...] = acc

    return _pcall(body, name=name, grid=(r // tr,),
                  in_specs=[pl.BlockSpec((k, tr, c), lambda i: (0, i, 0))],
                  out_specs=pl.BlockSpec((tr, c), lambda i: (i, 0)),
                  out_shape=jax.ShapeDtypeStruct((r, c), F32))(v)


def adamw(w, g, m, v, *, name):
    r, c = w.shape
    tr = _row_tile(r, c, budget=1 << 20)
    m_scale = 1.0 / (1.0 - ADAM_B1 ** ADAM_STEP)
    v_scale = 1.0 / (1.0 - ADAM_B2 ** ADAM_STEP)

    def body(w_ref, g_ref, m_ref, v_ref, d_ref, nm_ref, nv_ref):
        gv = g_ref[...]
        nm = ADAM_B1 * m_ref[...] + (1.0 - ADAM_B1) * gv
        nv = ADAM_B2 * v_ref[...] + (1.0 - ADAM_B2) * (gv * gv)
        nm_ref[...] = nm
        nv_ref[...] = nv
        d_ref[...] = -ADAM_LR * ((nm * m_scale) / (jnp.sqrt(nv * v_scale) + ADAM_EPS) + ADAM_WD * w_ref[...])

    spec = pl.BlockSpec((tr, c), lambda i: (i, 0))
    return _pcall(body, name=name, grid=(r // tr,), in_specs=[spec] * 4, out_specs=[spec] * 3,
                  out_shape=[jax.ShapeDtypeStruct((r, c), F32)] * 3)(w, g, m, v)


def ada_fwd(c_all, w_shard, b_shard, *, name):
    bsz, d = c_all.shape
    ncol = w_shard.shape[1]

    def body(c_ref, w_ref, b_ref, o_ref):
        cv = c_ref[...]
        act = (cv * _sigmoid(cv)).astype(BF16)
        o_ref[...] = _dot(act, w_ref[...].astype(BF16)) + b_ref[...]

    tn = _pick(ncol, (512, 256, 128))
    return _pcall(body, name=name, grid=(ncol // tn,),
                  in_specs=[pl.BlockSpec((bsz, d), lambda j: (0, 0)), pl.BlockSpec((d, tn), lambda j: (0, j)),
                            pl.BlockSpec((1, tn), lambda j: (0, j))],
                  out_specs=pl.BlockSpec((bsz, tn), lambda j: (0, j)),
                  out_shape=jax.ShapeDtypeStruct((bsz, ncol), F32))(c_all, w_shard, b_shard)


def ada_bwd(c_all, d_mod_all, d_mod_cols, *, name):
    bsz, d = c_all.shape
    ncol = d_mod_cols.shape[1]
    nall = d_mod_all.shape[1]

    def body(c_ref, da_ref, dc_ref, gw_ref, gb_ref):
        cv = c_ref[...]
        act = (cv * _sigmoid(cv)).astype(BF16)
        gw_ref[...] = _dot_tn(act, dc_ref[...].astype(BF16))
        gb_ref[...] = _colsum(da_ref[...])

    full = lambda s: pl.BlockSpec(s, lambda: (0,) * len(s))
    return _pcall(body, name=name,
                  in_specs=[full((bsz, d)), full((bsz, nall)), full((bsz, ncol))],
                  out_specs=[full((d, ncol)), full((1, nall))],
                  out_shape=[jax.ShapeDtypeStruct((d, ncol), F32), jax.ShapeDtypeStruct((1, nall), F32)],
                  )(c_all, d_mod_all, d_mod_cols)


WEIGHT_NAMES = ['w_ada', 'b_ada', 'pre_norm1', 'post_norm1', 'w_in', 'b_gate', 'lru_conv_w', 'lru_conv_b', 'lru_wa',
                'lru_ba', 'lru_wx', 'lru_bx', 'lru_lambda', 'w_pa', 'ssd_conv_w', 'ssd_conv_b', 'ssd_dt_bias',
                'ssd_a_log', 'ssd_d', 'ssd_norm_w', 'w_pb', 'w_out', 'pre_norm2', 'post_norm2', 'w_ff1', 'w_ff2']
BIG_NAMES = ['w_in', 'w_pa', 'w_pb', 'w_out', 'w_ff1', 'w_ff2']
COLUMN_SHARDED = ('w_in', 'w_ff1')
SMALL_NAMES = [n for n in WEIGHT_NAMES if n not in BIG_NAMES + ['w_ada', 'b_ada']]
SHARDED_SMALL = ('lru_conv_w', 'ssd_conv_w')
PACK_WIDTH = 1024


def _whole(name, gathered):
    if name in COLUMN_SHARDED:
        return jnp.transpose(gathered, (1, 0, 2)).reshape(gathered.shape[1], N_CHIPS * gathered.shape[2])
    return gathered.reshape(N_CHIPS * gathered.shape[1], gathered.shape[2])


def _by_chip(name, g):
    if name in COLUMN_SHARDED:
        return jnp.transpose(g.reshape(g.shape[0], N_CHIPS, g.shape[1] // N_CHIPS), (1, 0, 2))
    return g.reshape(N_CHIPS, g.shape[0] // N_CHIPS, g.shape[1])


class ChipExchange:
    def __init__(self, shards, core):
        self.shards, self.core = shards, core
        self.pending, self.halves = [], {}

    def weights_bg(self):
        return gather_halves_background(list(self.shards.values()))

    def weights(self, arrived):
        swapped = fill_other_half(arrived, name="weights_from_sibling")
        return {n: _whole(n, g) for n, g in zip(self.shards, swapped)}

    def grads_bg(self, grads):
        self.pending = list(grads)
        by_chip = [_by_chip(n, g) for n, g in grads.items()]
        theirs = send_half_to_sibling(by_chip, name="grads_to_sibling_" + self.pending[0])
        sums = [add_half_to_bf16(self.core, a, b, name="add_cores_" + n)
                for n, a, b in zip(self.pending, by_chip, theirs)]
        return scatter_background(sums)

    def grads_done(self, landed):
        for n, p in zip(self.pending, landed):
            self.halves[n] = sum_blocks(p, name="add_chips_" + n)

    def reduced(self):
        names = list(self.halves)
        return dict(zip(names, join_with_sibling([self.halves[n] for n in names], name="grads_join")))


def _pack(parts):
    flat = jnp.concatenate([p.reshape(-1).astype(F32) for p in parts])
    rows = -(-flat.shape[0] // (PACK_WIDTH * SUBLANES)) * SUBLANES
    return jnp.pad(flat, (0, rows * PACK_WIDTH - flat.shape[0])).reshape(rows, PACK_WIDTH)


def _unpack(packed, shapes):
    flat = packed.reshape(-1)
    out, pos = [], 0
    for s in shapes:
        size = int(np.prod(s))
        out.append(flat[pos:pos + size].reshape(s))
        pos += size
    return out


def kernel(x, c, w_ada, b_ada, pre_norm1, post_norm1, w_in, b_gate, lru_conv_w, lru_conv_b, lru_wa, lru_ba, lru_wx, lru_bx, lru_lambda, w_pa, ssd_conv_w, ssd_conv_b, ssd_dt_bias, ssd_a_log, ssd_d, ssd_norm_w, w_pb, w_out, pre_norm2, post_norm2, w_ff1, w_ff2, loss_target, m_w_ada, m_b_ada, m_pre_norm1, m_post_norm1, m_w_in, m_b_gate, m_lru_conv_w, m_lru_conv_b, m_lru_wa, m_lru_ba, m_lru_wx, m_lru_bx, m_lru_lambda, m_w_pa, m_ssd_conv_w, m_ssd_conv_b, m_ssd_dt_bias, m_ssd_a_log, m_ssd_d, m_ssd_norm_w, m_w_pb, m_w_out, m_pre_norm2, m_post_norm2, m_w_ff1, m_w_ff2, v_w_ada, v_b_ada, v_pre_norm1, v_post_norm1, v_w_in, v_b_gate, v_lru_conv_w, v_lru_conv_b, v_lru_wa, v_lru_ba, v_lru_wx, v_lru_bx, v_lru_lambda, v_w_pa, v_ssd_conv_w, v_ssd_conv_b, v_ssd_dt_bias, v_ssd_a_log, v_ssd_d, v_ssd_norm_w, v_w_pb, v_w_out, v_pre_norm2, v_post_norm2, v_w_ff1, v_w_ff2):
    given = dict(locals())
    bsz, seq, d = x.shape
    my_x, my_y, my_c = lax.axis_index("x"), lax.axis_index("y"), lax.axis_index("c")
    chip = 2 * my_x + my_y
    dev = 2 * chip + my_c
    strip = lambda a: a if a.ndim == 2 else a[0]
    w = {n: strip(given[n]) for n in WEIGHT_NAMES}
    m = {n: strip(given["m_" + n]) for n in WEIGHT_NAMES}
    v = {n: strip(given["v_" + n]) for n in WEIGHT_NAMES}

    first_shapes = [c.shape] + [w[n].shape for n in SHARDED_SMALL]
    first = allgather8(_pack([c] + [w[n] for n in SHARDED_SMALL]), name="gather_c_conv")
    first = first.reshape(N_DEV, -1, PACK_WIDTH)
    per_dev = [_unpack(first[k], first_shapes) for k in range(N_DEV)]
    c_all = jnp.concatenate([p[0] for p in per_dev], axis=0)
    conv_full = {n: jnp.concatenate([per_dev[2 * k][1 + i] for k in range(N_CHIPS)], axis=1)
                 for i, n in enumerate(SHARDED_SMALL)}

    ncol = w["w_ada"].shape[1]
    b_cols = lax.dynamic_slice(b_ada, (0, chip * ncol), (1, ncol))
    mod_cols = ada_fwd(c_all, w["w_ada"], b_cols, name="ada_fwd")
    mod_all = allgather8(mod_cols, name="gather_mod").reshape(N_CHIPS, 2, N_DEV * bsz, ncol)[:, 0]
    mod_all = jnp.transpose(mod_all, (1, 0, 2)).reshape(N_DEV * bsz, N_CHIPS * ncol)
    mod = lax.dynamic_slice(mod_all, (dev * bsz, 0), (bsz, 6 * d)).reshape(bsz, 6, d)
    mod = jnp.pad(mod, ((0, 0), (0, 2), (0, 0)))

    w_in_full = _whole("w_in", gather_weights([w["w_in"].astype(BF16)], name="gather_w_in")[0])
    big = {"w_main": w_in_full[:, :8192],
           "w_dt": jnp.pad(w_in_full[:, 8192:8192 + SSD_HEADS], ((0, 0), (0, LANES - SSD_HEADS))),
           "w_gates": w_in_full[:, 8192 + SSD_HEADS:]}
    small = {n: w[n] for n in SMALL_NAMES}
    small.update(conv_full)
    plan = ChipExchange({n: w[n].astype(BF16) for n in BIG_NAMES if n != "w_in"}, my_c.astype(jnp.int32).reshape(1))

    loss_cols, grad_x, d_mod, small_grads = local_step(x, loss_target, mod, big, small, plan)

    packed = _pack([d_mod, loss_cols] + [small_grads[n] for n in SMALL_NAMES])
    rows = packed.shape[0]
    everyone = allgather8(packed, name="gather_small").reshape(N_DEV, rows, PACK_WIDTH)
    d_mod_all = everyone[:, :bsz * 6].reshape(N_DEV * bsz, 6 * d)
    summed = sum_blocks(everyone, name="sum_small")
    shapes = [d_mod.shape, loss_cols.shape] + [small_grads[n].shape for n in SMALL_NAMES]
    parts = _unpack(summed, shapes)
    loss = jnp.sum(parts[1])
    grads = dict(zip(SMALL_NAMES, parts[2:]))
    for n in SHARDED_SMALL:
        cols = w[n].shape[1]
        grads[n] = lax.dynamic_slice(grads[n], (0, chip * cols), (grads[n].shape[0], cols))
    d_mod_cols = lax.dynamic_slice(d_mod_all, (0, chip * ncol), (N_DEV * bsz, ncol))
    grads["w_ada"], grads["b_ada"] = ada_bwd(c_all, d_mod_all, d_mod_cols, name="ada_bwd")

    grads.update(plan.reduced())

    delta, new_m, new_v = {}, {}, {}
    for n in BIG_NAMES + ["w_ada", "b_ada"]:
        delta[n], new_m[n], new_v[n] = adamw(w[n], grads[n], m[n], v[n], name="adamw_" + n)
    shapes = [w[n].shape for n in SMALL_NAMES]
    pk = lambda src: _pack([src[n] for n in SMALL_NAMES])
    upd = adamw(pk(w), pk(grads), pk(m), pk(v), name="adamw_small")
    for out, packed_out in zip((delta, new_m, new_v), upd):
        out.update(zip(SMALL_NAMES, _unpack(packed_out, shapes)))

    shaped = lambda src: [src[n].reshape(given[n].shape) for n in WEIGHT_NAMES]
    return (loss, grad_x, *shaped(grads), *shaped(delta), *shaped(new_m), *shaped(new_v))
```

```python
import functools
import math

import numpy as np
import jax
import jax.numpy as jnp
from jax import lax
from jax.experimental import pallas as pl
from jax.experimental.pallas import tpu as pltpu

F32 = jnp.float32
BF16 = jnp.bfloat16
HI = lax.Precision.HIGHEST
MESH = pl.DeviceIdType.MESH

D_MODEL = 1024
LRU_HEADS = 16
LRU_HEAD_DIM = 64
LRU_C = 8.0
SSD_INNER = 2048
SSD_HEADS = 32
SSD_HEAD_DIM = 64
SSD_GROUPS = 8
SSD_STATE = 128
SSD_CHUNK = 128
SSD_CONV_DIM = 4096
D_FF = 4096
EPS = 1e-6
N_CHIPS = 4
N_DEV = 8
LANES = 128
SUBLANES = 8

ADAM_LR = 0.001
ADAM_B1 = 0.9
ADAM_B2 = 0.999
ADAM_EPS = 1e-08
ADAM_WD = 0.01
ADAM_STEP = 10


ANY = pl.BlockSpec(memory_space=pl.ANY)


def _pcall(body, **kw):
    return pl.pallas_call(body, **kw)


class Background:
    def __init__(self, inputs, out_shapes, scratch, start, finish):
        self.inputs, self.out_shapes, self.scratch = list(inputs), list(out_shapes), list(scratch)
        self.start, self.finish = start, finish

    def wrap(self, body, kw):
        n_in, n_out = len(kw["in_specs"]), len(kw["out_specs"])
        n_scr = len(kw.get("scratch_shapes", []))
        b_in, b_out = len(self.inputs), len(self.out_shapes)
        grid = kw["grid"]

        def wrapped(*refs):
            ins, b_ins = refs[:n_in], refs[n_in:n_in + b_in]
            o0 = n_in + b_in
            outs, b_outs = refs[o0:o0 + n_out], refs[o0 + n_out:o0 + n_out + b_out]
            s0 = o0 + n_out + b_out
            scr, b_scr = refs[s0:s0 + n_scr], refs[s0 + n_scr:]
            ids = [pl.program_id(a) for a in range(len(grid))]
            first = functools.reduce(jnp.logical_and, [i == 0 for i in ids])
            last = functools.reduce(jnp.logical_and, [i == g - 1 for i, g in zip(ids, grid)])

            @pl.when(first)
            def _():
                self.start(b_ins, b_outs, b_scr)

            body(*ins, *outs, *scr)

            @pl.when(last)
            def _():
                self.finish(b_ins, b_outs, b_scr)

        kw = dict(kw, in_specs=list(kw["in_specs"]) + [ANY] * b_in, out_specs=list(kw["out_specs"]) + [ANY] * b_out,
                  out_shape=list(kw["out_shape"]) + self.out_shapes,
                  scratch_shapes=list(kw.get("scratch_shapes", [])) + self.scratch)
        return wrapped, kw


def _run(body, args, bg, **kw):
    n_out = len(kw["out_shape"])
    if bg is None:
        return list(_pcall(body, **kw)(*args)), []
    body, kw = bg.wrap(body, kw)
    outs = _pcall(body, **kw)(*args, *bg.inputs)
    return list(outs[:n_out]), list(outs[n_out:])


def _sigmoid(v):
    return 1.0 / (1.0 + jnp.exp(-v))


def _log1p(u):
    return jnp.where(u < 1e-3, u * (1.0 - u * (0.5 - u * (1.0 / 3.0))), jnp.log(1.0 + u))


def _softplus(v):
    return jnp.maximum(v, 0.0) + _log1p(jnp.exp(-jnp.abs(v)))


def _neg_expm1(v):
    small = -v * (1.0 + v * (0.5 + v * (1.0 / 6.0 + v * (1.0 / 24.0))))
    return jnp.where(v > -0.05, small, 1.0 - jnp.exp(v))


_GELU_K = math.sqrt(2.0 / math.pi)


def _gelu(v):
    t = jnp.tanh(_GELU_K * (v + 0.044715 * v * v * v))
    return 0.5 * v * (1.0 + t)


def _gelu_grad(v):
    t = jnp.tanh(_GELU_K * (v + 0.044715 * v * v * v))
    return 0.5 * (1.0 + t) + 0.5 * v * (1.0 - t * t) * _GELU_K * (1.0 + 3.0 * 0.044715 * v * v)


def _colsum(v):
    return jnp.sum(v, axis=0, keepdims=True)


def _dot(a, b, precision=None):
    return lax.dot_general(a, b, (((1,), (0,)), ((), ())), preferred_element_type=F32, precision=precision)


def _dot_nt(a, b):
    return lax.dot_general(a, b, (((1,), (1,)), ((), ())), preferred_element_type=F32)


def _dot_tn(a, b):
    return lax.dot_general(a, b, (((0,), (0,)), ((), ())), preferred_element_type=F32)


def _shift_down(xt, prev8, j):
    if j == 0:
        return xt
    n = xt.shape[0]
    r = pltpu.roll(xt, j, 0)
    p = pltpu.roll(prev8, j, 0)
    rows = lax.broadcasted_iota(jnp.int32, (SUBLANES, xt.shape[1]), 0)
    top = jnp.where(rows < j, p, r[0:SUBLANES])
    if n == SUBLANES:
        return top
    return jnp.concatenate([top, r[SUBLANES:]], axis=0)


def _shift_up(xt, next8, j):
    if j == 0:
        return xt
    n = xt.shape[0]
    r = pltpu.roll(xt, n - j, 0)
    p = pltpu.roll(next8, SUBLANES - j, 0)
    rows = lax.broadcasted_iota(jnp.int32, (SUBLANES, xt.shape[1]), 0)
    bot = jnp.where(rows >= SUBLANES - j, p, r[n - SUBLANES:])
    if n == SUBLANES:
        return bot
    return jnp.concatenate([r[:n - SUBLANES], bot], axis=0)


def _conv4(xt, prev8, w, b):
    out = b + w[3:4] * xt
    for k in range(3):
        out = out + w[k:k + 1] * _shift_down(xt, prev8, 3 - k)
    return out


def _conv4_bwd(d_out, next8, xt, w):
    d_x = w[3:4] * d_out
    d_w = []
    for k in range(3):
        up = _shift_up(d_out, next8, 3 - k)
        d_x = d_x + w[k:k + 1] * up
        d_w.append(_colsum(up * xt))
    d_w.append(_colsum(d_out * xt))
    return d_x, d_w, _colsum(d_out)


def _stack_rows(rows, width):
    rows = list(rows) + [jnp.zeros((1, width), F32)] * (SUBLANES - len(rows))
    return jnp.concatenate(rows, axis=0)


def _pick(n, cands):
    for c in cands:
        if n % c == 0:
            return c
    raise ValueError(f"no tile for {n}")


MM_ROWS = 512
MM_PANEL_COLS = 2048
MM_SUB = 512


def mm_nn(pairs, *, name, out_dtype=F32, a_fn=None, add=None, epi=None, extra=None, bg=None):
    np_ = len(pairs)
    m, n = pairs[0][0].shape[0], pairs[0][1].shape[1]
    tm = _pick(m, (MM_ROWS, 256, 128, 64, 32, 16, 8))
    pn = n if n <= MM_PANEL_COLS else _pick(n, (MM_PANEL_COLS, 1024, 512, 256, 128))
    ns = _pick(pn, (MM_SUB, 256, 128))
    adds = list(add or ())
    has_extra = extra is not None
    stage0 = a_fn is not None or pairs[0][0].dtype != BF16

    def body(*refs):
        a_refs, b_refs = refs[:np_], refs[np_:2 * np_]
        pos = 2 * np_
        extra_ref = None
        add_refs = refs[pos:pos + len(adds)]
        pos += len(adds)
        if has_extra:
            extra_ref = refs[pos]
            pos += 1
        o_ref = refs[pos]
        lhs = list(a_refs)
        if stage0:
            av = a_refs[0][...]
            if a_fn is not None:
                av = a_fn(av)
            refs[pos + 1][...] = av.astype(BF16)
            lhs[0] = refs[pos + 1]
        for n0 in range(0, pn, ns):
            sl = slice(n0, n0 + ns)
            acc = None
            for a_ref, b_ref in zip(lhs, b_refs):
                part = _dot(a_ref[...].astype(BF16), b_ref[:, sl])
                acc = part if acc is None else acc + part
            for add_ref in add_refs:
                acc = acc + add_ref[:, sl]
            if epi is not None:
                acc = epi(acc, extra_ref[:, sl]) if has_extra else epi(acc)
            o_ref[:, sl] = acc.astype(out_dtype)

    in_specs = [pl.BlockSpec((tm, a.shape[1]), lambda j, i: (i, 0)) for a, _ in pairs]
    in_specs += [pl.BlockSpec((b.shape[0], pn), lambda j, i: (0, j)) for _, b in pairs]
    args = [a for a, _ in pairs] + [b for _, b in pairs]
    tile = pl.BlockSpec((tm, pn), lambda j, i: (i, j))
    for extra_add in adds:
        in_specs.append(tile)
        args.append(extra_add)
    if has_extra:
        in_specs.append(tile)
        args.append(extra)
    outs, bg_outs = _run(
        body, args, bg, name=name, grid=(n // pn, m // tm), in_specs=in_specs, out_specs=[tile],
        out_shape=[jax.ShapeDtypeStruct((m, n), out_dtype)],
        scratch_shapes=[pltpu.VMEM((tm, pairs[0][0].shape[1]), BF16)] if stage0 else [])
    return outs[0] if bg is None else (outs[0], bg_outs)


MM_REDUCE_ROWS = 1024
MM_GRAD_ROWS = 1024
MM_GRAD_COLS = 2048


def mm_tn(a, b, *, name, a_fn=None):
    m, ka = a.shape
    nb = b.shape[1]
    pa = _pick(ka, (MM_GRAD_ROWS, 512, 256, 128))
    pb = nb if nb <= MM_GRAD_COLS else _pick(nb, (MM_GRAD_COLS, 1024, 512, 256, 128))
    ns = _pick(pb, (MM_SUB, 256, 128))
    tmk = _pick(m, (MM_REDUCE_ROWS, 512, 256, 128, 64, 32, 16))

    def body(a_ref, b_ref, o_ref, lhs):
        k = pl.program_id(2)

        @pl.when(k == 0)
        def _():
            o_ref[...] = jnp.zeros_like(o_ref)

        av = a_ref[...]
        if a_fn is not None:
            av = a_fn(av)
        lhs[...] = av.astype(BF16)
        for n0 in range(0, pb, ns):
            o_ref[:, n0:n0 + ns] += _dot_tn(lhs[...], b_ref[:, n0:n0 + ns].astype(BF16))

    return _pcall(
        body, name=name,
        grid=(ka // pa, nb // pb, m // tmk),
        in_specs=[pl.BlockSpec((tmk, pa), lambda i, j, k: (k, i)),
                  pl.BlockSpec((tmk, pb), lambda i, j, k: (k, j))],
        out_specs=pl.BlockSpec((pa, pb), lambda i, j, k: (i, j)),
        out_shape=jax.ShapeDtypeStruct((ka, nb), F32),
        scratch_shapes=[pltpu.VMEM((tmk, pa), BF16)],
    )(a, b)


def _relu_sq(v):
    r = jnp.maximum(v, 0.0)
    return r * r


ROW_TILE = 512


def _row_specs(bsz, seq, width, ts):
    return pl.BlockSpec((None, ts, width), lambda b, i: (b, i, 0))


def _vec_spec(width):
    return pl.BlockSpec((1, width), lambda b, i: (0, 0))


def _mod_spec():
    return pl.BlockSpec((None, SUBLANES, D_MODEL), lambda b, i: (b, 0, 0))


def _rstd(v):
    return lax.rsqrt(jnp.mean(v * v, axis=-1, keepdims=True) + EPS)


def prenorm(x, w, mod, *, name):
    bsz, seq, d = x.shape
    ts = _pick(seq, (ROW_TILE, 256, 128))

    def body(x_ref, w_ref, mod_ref, h_ref):
        xv = x_ref[...]
        m = mod_ref[...]
        xh = xv * _rstd(xv)
        h_ref[...] = ((xh * w_ref[...]) * (1.0 + m[1:2]) + m[0:1]).astype(BF16)

    return _pcall(
        body, name=name, grid=(bsz, seq // ts),
        in_specs=[_row_specs(bsz, seq, d, ts), _vec_spec(d), _mod_spec()],
        out_specs=_row_specs(bsz, seq, d, ts),
        out_shape=jax.ShapeDtypeStruct((bsz, seq, d), BF16),
    )(x, w, mod)


def post1_pre2(x, out1, mod, post1, pre2, *, name):
    bsz, seq, d = x.shape
    ts = _pick(seq, (ROW_TILE, 256, 128))

    def body(x_ref, o_ref, mod_ref, p1_ref, p2_ref, x1_ref, h2_ref):
        m = mod_ref[...]
        ov = o_ref[...]
        x1 = x_ref[...] + m[2:3] * ((ov * _rstd(ov)) * p1_ref[...])
        x1_ref[...] = x1
        xh = x1 * _rstd(x1)
        h2_ref[...] = ((xh * p2_ref[...]) * (1.0 + m[4:5]) + m[3:4]).astype(BF16)

    return _pcall(
        body, name=name, grid=(bsz, seq // ts),
        in_specs=[_row_specs(bsz, seq, d, ts), _row_specs(bsz, seq, d, ts), _mod_spec(), _vec_spec(d), _vec_spec(d)],
        out_specs=[_row_specs(bsz, seq, d, ts), _row_specs(bsz, seq, d, ts)],
        out_shape=[jax.ShapeDtypeStruct((bsz, seq, d), F32), jax.ShapeDtypeStruct((bsz, seq, d), BF16)],
    )(x, out1, mod, post1, pre2)


def _acc_specs(d):
    per_batch = pl.BlockSpec((None, SUBLANES, d), lambda b, i: (b, 0, 0))
    glob = pl.BlockSpec((SUBLANES, d), lambda b, i: (0, 0))
    return per_batch, glob


def _accumulate(pb_ref, gl_ref, pb_rows, gl_rows, width):
    b, i = pl.program_id(0), pl.program_id(1)

    @pl.when(i == 0)
    def _():
        pb_ref[...] = jnp.zeros_like(pb_ref)

    @pl.when((b == 0) & (i == 0))
    def _():
        gl_ref[...] = jnp.zeros_like(gl_ref)

    pb_ref[...] += _stack_rows(pb_rows, width)
    gl_ref[...] += _stack_rows(gl_rows, width)


def _rms_bwd(d_n, n, r):
    return r * (d_n - n * jnp.mean(d_n * n, axis=-1, keepdims=True))


def final_bwd(x1, y2, target, mod, post2, *, name):
    bsz, seq, d = x1.shape
    ts = _pick(seq, (ROW_TILE, 256, 128))

    def body(x1_ref, y_ref, t_ref, mod_ref, p_ref, dx_ref, dy_ref, pb_ref, gl_ref):
        m = mod_ref[...]
        g2 = m[5:6]
        yv = y_ref[...]
        r = _rstd(yv)
        n = yv * r
        o = n * p_ref[...]
        diff = (x1_ref[...] + g2 * o) - t_ref[...]
        dx = diff * (1.0 / d)
        dx_ref[...] = dx
        d_o = dx * g2
        dy_ref[...] = _rms_bwd(d_o * p_ref[...], n, r).astype(BF16)
        _accumulate(pb_ref, gl_ref, [_colsum(dx * o)], [_colsum(d_o * n), _colsum(diff * diff) * (0.5 / d)], d)

    pb, gl = _acc_specs(d)
    rs = _row_specs(bsz, seq, d, ts)
    return _pcall(
        body, name=name, grid=(bsz, seq // ts),
        in_specs=[rs, rs, rs, _mod_spec(), _vec_spec(d)],
        out_specs=[rs, rs, pb, gl],
        out_shape=[jax.ShapeDtypeStruct((bsz, seq, d), F32), jax.ShapeDtypeStruct((bsz, seq, d), BF16),
                   jax.ShapeDtypeStruct((bsz, SUBLANES, d), F32), jax.ShapeDtypeStruct((SUBLANES, d), F32)],
    )(x1, y2, target, mod, post2)


def mid_bwd(d_h2, dx2, x1, out1, mod, pre2, post1, *, name):
    bsz, seq, d = x1.shape
    ts = _pick(seq, (ROW_TILE, 256, 128))

    def body(dh_ref, dx2_ref, x1_ref, o_ref, mod_ref, p2_ref, p1_ref, dx1_ref, do_ref, pb_ref, gl_ref):
        m = mod_ref[...]
        dh = dh_ref[...]
        x1 = x1_ref[...]
        r2 = _rstd(x1)
        xh = x1 * r2
        xw = xh * p2_ref[...]
        d_xw = dh * (1.0 + m[4:5])
        dx1 = dx2_ref[...] + _rms_bwd(d_xw * p2_ref[...], xh, r2)
        dx1_ref[...] = dx1
        ov = o_ref[...]
        r1 = _rstd(ov)
        n1 = ov * r1
        o1 = n1 * p1_ref[...]
        d_o1 = dx1 * m[2:3]
        do_ref[...] = _rms_bwd(d_o1 * p1_ref[...], n1, r1).astype(BF16)
        _accumulate(pb_ref, gl_ref, [_colsum(dh), _colsum(dh * xw), _colsum(dx1 * o1)],
                    [_colsum(d_xw * xh), _colsum(d_o1 * n1)], d)

    pb, gl = _acc_specs(d)
    rs = _row_specs(bsz, seq, d, ts)
    return _pcall(
        body, name=name, grid=(bsz, seq // ts),
        in_specs=[rs, rs, rs, rs, _mod_spec(), _vec_spec(d), _vec_spec(d)],
        out_specs=[rs, rs, pb, gl],
        out_shape=[jax.ShapeDtypeStruct((bsz, seq, d), F32), jax.ShapeDtypeStruct((bsz, seq, d), BF16),
                   jax.ShapeDtypeStruct((bsz, SUBLANES, d), F32), jax.ShapeDtypeStruct((SUBLANES, d), F32)],
    )(d_h2, dx2, x1, out1, mod, pre2, post1)


def first_bwd(d_h1, dx1, x, mod, pre1, *, name):
    bsz, seq, d = x.shape
    ts = _pick(seq, (ROW_TILE, 256, 128))

    def body(dh_ref, dx1_ref, x_ref, mod_ref, p_ref, gx_ref, pb_ref, gl_ref):
        m = mod_ref[...]
        dh = dh_ref[...]
        xv = x_ref[...]
        r = _rstd(xv)
        xh = xv * r
        xw = xh * p_ref[...]
        d_xw = dh * (1.0 + m[1:2])
        gx_ref[...] = dx1_ref[...] + _rms_bwd(d_xw * p_ref[...], xh, r)
        _accumulate(pb_ref, gl_ref, [_colsum(dh), _colsum(dh * xw)], [_colsum(d_xw * xh)], d)

    pb, gl = _acc_specs(d)
    rs = _row_specs(bsz, seq, d, ts)
    return _pcall(
        body, name=name, grid=(bsz, seq // ts),
        in_specs=[rs, rs, rs, _mod_spec(), _vec_spec(d)],
        out_specs=[rs, pb, gl],
        out_shape=[jax.ShapeDtypeStruct((bsz, seq, d), F32),
                   jax.ShapeDtypeStruct((bsz, SUBLANES, d), F32), jax.ShapeDtypeStruct((SUBLANES, d), F32)],
    )(d_h1, dx1, x, mod, pre1)


def merge_fwd(ya, yb, gates, b_gate, *, name):
    bsz, seq, d = ya.shape
    ts = _pick(seq, (ROW_TILE, 256, 128))

    def body(ya_ref, yb_ref, g_ref, b_ref, o_ref):
        g = _sigmoid(g_ref[...] + b_ref[...])
        o_ref[...] = (g[:, :d] * ya_ref[...] + g[:, d:] * yb_ref[...]).astype(BF16)

    rs = _row_specs(bsz, seq, d, ts)
    return _pcall(
        body, name=name, grid=(bsz, seq // ts),
        in_specs=[rs, rs, _row_specs(bsz, seq, 2 * d, ts), _vec_spec(2 * d)],
        out_specs=rs,
        out_shape=jax.ShapeDtypeStruct((bsz, seq, d), BF16),
    )(ya, yb, gates, b_gate)


def merge_bwd(d_merged, ya, yb, gates, b_gate, *, name):
    bsz, seq, d = ya.shape
    ts = _pick(seq, (ROW_TILE, 256, 128))

    def body(dm_ref, ya_ref, yb_ref, g_ref, b_ref, dya_ref, dyb_ref, dg_ref, gl_ref):
        b, i = pl.program_id(0), pl.program_id(1)
        g = _sigmoid(g_ref[...] + b_ref[...])
        dm = dm_ref[...]
        ga, gb = g[:, :d], g[:, d:]
        dya_ref[...] = (dm * ga).astype(BF16)
        dyb_ref[...] = (dm * gb).astype(BF16)
        dg = jnp.concatenate([dm * ya_ref[...] * ga * (1.0 - ga), dm * yb_ref[...] * gb * (1.0 - gb)], axis=1)
        dg_ref[...] = dg.astype(BF16)

        @pl.when((b == 0) & (i == 0))
        def _():
            gl_ref[...] = jnp.zeros_like(gl_ref)

        gl_ref[...] += _stack_rows([_colsum(dg)], 2 * d)

    rs = _row_specs(bsz, seq, d, ts)
    rs2 = _row_specs(bsz, seq, 2 * d, ts)
    return _pcall(
        body, name=name, grid=(bsz, seq // ts),
        in_specs=[rs, rs, rs, rs2, _vec_spec(2 * d)],
        out_specs=[rs, rs, rs2, pl.BlockSpec((SUBLANES, 2 * d), lambda b, i: (0, 0))],
        out_shape=[jax.ShapeDtypeStruct((bsz, seq, d), BF16), jax.ShapeDtypeStruct((bsz, seq, d), BF16),
                   jax.ShapeDtypeStruct((bsz, seq, 2 * d), BF16), jax.ShapeDtypeStruct((SUBLANES, 2 * d), F32)],
    )(d_merged, ya, yb, gates, b_gate)


LRU_TILE = 256
N_LRU_BLOCKS = D_MODEL // LANES


def _block_mm(v, w_ref, transpose=False):
    vb = v.astype(BF16)
    outs = []
    for j in range(N_LRU_BLOCKS):
        blk = vb[:, LANES * j:LANES * (j + 1)]
        outs.append(_dot_nt(blk, w_ref[j]) if transpose else _dot(blk, w_ref[j]))
    return jnp.concatenate(outs, axis=1)


def _lru_gates(xc, wa_ref, ba, wx_ref, bx, sp):
    r = _sigmoid(_block_mm(xc, wa_ref) + ba)
    i = _sigmoid(_block_mm(xc, wx_ref) + bx)
    la = (-LRU_C * r) * sp
    a = jnp.exp(la)
    sq = jnp.sqrt(_neg_expm1(2.0 * la))
    return r, i, a, sq


def _prev8_spec(width, col_block, tile_rows):
    per = tile_rows // SUBLANES
    return pl.BlockSpec((None, SUBLANES, width), lambda b, i: (b, jnp.maximum(i * per - 1, 0), col_block))


def lru_fwd(pm, cw, cb, wa, ba, wx, bx, lam, w_pa, *, name):
    bsz, seq, _ = pm.shape
    d = D_MODEL
    ts = _pick(seq, (LRU_TILE, 128))

    def body(lx_ref, lxp_ref, lg_ref, cw_ref, cb_ref, wa_ref, ba_ref, wx_ref, bx_ref, lam_ref, wpa_ref,
             h_ref, pa_ref, ya_ref, hc, a_s, u_s):
        i = pl.program_id(1)

        @pl.when(i == 0)
        def _():
            hc[...] = jnp.zeros_like(hc)

        lx = lx_ref[...]
        prev8 = jnp.where(i == 0, 0.0, lxp_ref[...])
        xc = _conv4(lx, prev8, cw_ref[...], cb_ref[...])
        sp = _softplus(-lam_ref[...])
        r, ig, a, sq = _lru_gates(xc, wa_ref, ba_ref[...], wx_ref, bx_ref[...], sp)
        a_s[...] = a
        u_s[...] = sq * (ig * xc)

        def step(g, h):
            r0 = pl.multiple_of(g * SUBLANES, SUBLANES)
            a8 = a_s[pl.ds(r0, SUBLANES), :]
            u8 = u_s[pl.ds(r0, SUBLANES), :]
            rows = []
            for j in range(SUBLANES):
                h = a8[j:j + 1] * h + u8[j:j + 1]
                rows.append(h)
            h_ref[pl.ds(r0, SUBLANES), :] = jnp.concatenate(rows, axis=0)
            return h

        hc[...] = lax.fori_loop(0, ts // SUBLANES, step, hc[...])
        pa_ref[...] = (h_ref[...] * _gelu(lg_ref[...])).astype(BF16)
        ya_ref[...] = _dot(pa_ref[...], wpa_ref[...])

    vec = _vec_spec(d)
    wspec = pl.BlockSpec((N_LRU_BLOCKS, LANES, LANES), lambda b, i: (0, 0, 0))
    rs = _row_specs(bsz, seq, d, ts)
    return _pcall(
        body, name=name, grid=(bsz, seq // ts),
        in_specs=[pl.BlockSpec((None, ts, d), lambda b, i: (b, i, 0)), _prev8_spec(d, 0, ts),
                  pl.BlockSpec((None, ts, d), lambda b, i: (b, i, 1)),
                  pl.BlockSpec((4, d), lambda b, i: (0, 0)), vec, wspec, vec, wspec, vec, vec,
                  pl.BlockSpec(w_pa.shape, lambda b, i: (0, 0))],
        out_specs=[rs, rs, rs],
        out_shape=[jax.ShapeDtypeStruct((bsz, seq, d), F32), jax.ShapeDtypeStruct((bsz, seq, d), BF16),
                   jax.ShapeDtypeStruct((bsz, seq, d), F32)],
        scratch_shapes=[pltpu.VMEM((1, d), F32), pltpu.VMEM((ts, d), F32), pltpu.VMEM((ts, d), F32)],
    )(pm, pm, pm, cw, cb, wa, ba, wx, bx, lam, w_pa)


def lru_bwd(pm, h, d_ya, cw, cb, wa, ba, wx, bx, lam, wt_pa, wt_lru, *, name, bg=None):
    bsz, seq, _ = pm.shape
    d = D_MODEL
    ts = _pick(seq, (LRU_TILE, 128))
    nt = seq // ts
    per = ts // SUBLANES

    def rev(i):
        return nt - 1 - i

    def body(lx_ref, lxp_ref, lg_ref, h_ref, hp_ref, dya_ref, cw_ref, cb_ref, wa_ref, ba_ref, wx_ref, bx_ref,
             lam_ref, wtpa_ref, wtl_ref, dl_ref, dh1_ref, dwa_ref, dwx_ref, rows_ref,
             carry, dxc_next, a_s, dh_s, acc_s):
        b, i = pl.program_id(0), pl.program_id(1)
        t = rev(i)

        @pl.when(i == 0)
        def _():
            carry[...] = jnp.zeros_like(carry)
            dxc_next[...] = jnp.zeros_like(dxc_next)

        @pl.when((b == 0) & (i == 0))
        def _():
            dwa_ref[...] = jnp.zeros_like(dwa_ref)
            dwx_ref[...] = jnp.zeros_like(dwx_ref)
            rows_ref[...] = jnp.zeros_like(rows_ref)

        lx = lx_ref[...]
        lg = lg_ref[...]
        prev8 = jnp.where(t == 0, 0.0, lxp_ref[...])
        cwv = cw_ref[...]
        xc = _conv4(lx, prev8, cwv, cb_ref[...])
        lam_v = lam_ref[...]
        sp = _softplus(-lam_v)
        r, ig, a, sq = _lru_gates(xc, wa_ref, ba_ref[...], wx_ref, bx_ref[...], sp)
        hv = h_ref[...]
        d_pa = _dot(dya_ref[...], wtpa_ref[...])
        a_s[...] = a
        dh_s[...] = d_pa * _gelu(lg)

        def step(g, c):
            r0 = pl.multiple_of((per - 1 - g) * SUBLANES, SUBLANES)
            a8 = a_s[pl.ds(r0, SUBLANES), :]
            d8 = dh_s[pl.ds(r0, SUBLANES), :]
            rows = [None] * SUBLANES
            for j in range(SUBLANES - 1, -1, -1):
                acc = d8[j:j + 1] + c
                rows[j] = acc
                c = a8[j:j + 1] * acc
            acc_s[pl.ds(r0, SUBLANES), :] = jnp.concatenate(rows, axis=0)
            return c

        carry[...] = lax.fori_loop(0, per, step, carry[...])
        d_u = acc_s[...]
        hprev8 = jnp.where(t == 0, 0.0, hp_ref[...])
        d_a = d_u * _shift_down(hv, hprev8, 1)
        d_sq = d_u * (ig * xc)
        d_i = d_u * (sq * xc)
        d_xc = d_u * (sq * ig)
        d_la = d_a * a - d_sq * (a * a) / sq
        d_pre_r = (d_la * (-LRU_C * sp)) * (r * (1.0 - r))
        d_pre_i = d_i * (ig * (1.0 - ig))
        d_xc = d_xc + _block_mm(d_pre_r, wa_ref, transpose=True) + _block_mm(d_pre_i, wx_ref, transpose=True)
        xcb = xc.astype(BF16)
        drb = d_pre_r.astype(BF16)
        dib = d_pre_i.astype(BF16)
        for j in range(N_LRU_BLOCKS):
            sl = slice(LANES * j, LANES * (j + 1))
            dwa_ref[j] += _dot_tn(xcb[:, sl], drb[:, sl])
            dwx_ref[j] += _dot_tn(xcb[:, sl], dib[:, sl])
        d_lx, d_cw, d_cb = _conv4_bwd(d_xc, dxc_next[...], lx, cwv)
        dxc_next[...] = d_xc[0:SUBLANES]
        d_lam = _colsum(d_la * (-LRU_C * r)) * (-_sigmoid(-lam_v))
        rows_ref[...] += _stack_rows([_colsum(d_pre_r), _colsum(d_pre_i), d_lam, d_cb] + d_cw, d)
        dl_ref[:, :d] = d_lx.astype(BF16)
        dl_ref[:, d:] = (d_pa * hv * _gelu_grad(lg)).astype(BF16)
        dh1_ref[...] = _dot(dl_ref[...], wtl_ref[...])

    vec = _vec_spec(d)
    wspec = pl.BlockSpec((N_LRU_BLOCKS, LANES, LANES), lambda b, i: (0, 0, 0))
    tile = lambda col: pl.BlockSpec((None, ts, d), lambda b, i: (b, rev(i), col))
    prev8 = lambda col: pl.BlockSpec((None, SUBLANES, d), lambda b, i: (b, jnp.maximum(rev(i) * per - 1, 0), col))
    whole = lambda v: pl.BlockSpec(v.shape, lambda b, i: (0, 0))
    return _run(
        body, (pm, pm, pm, h, h, d_ya, cw, cb, wa, ba, wx, bx, lam, wt_pa, wt_lru), bg, name=name, grid=(bsz, nt),
        in_specs=[tile(0), prev8(0), tile(1), tile(0), prev8(0), tile(0),
                  pl.BlockSpec((4, d), lambda b, i: (0, 0)), vec, wspec, vec, wspec, vec, vec,
                  whole(wt_pa), whole(wt_lru)],
        out_specs=[pl.BlockSpec((None, ts, 2 * d), lambda b, i: (b, rev(i), 0)), tile(0), wspec, wspec,
                   pl.BlockSpec((SUBLANES, d), lambda b, i: (0, 0))],
        out_shape=[jax.ShapeDtypeStruct((bsz, seq, 2 * d), BF16), jax.ShapeDtypeStruct((bsz, seq, d), F32),
                   jax.ShapeDtypeStruct((N_LRU_BLOCKS, LANES, LANES), F32),
                   jax.ShapeDtypeStruct((N_LRU_BLOCKS, LANES, LANES), F32),
                   jax.ShapeDtypeStruct((SUBLANES, d), F32)],
        scratch_shapes=[pltpu.VMEM((1, d), F32), pltpu.VMEM((SUBLANES, d), F32),
                        pltpu.VMEM((ts, d), F32), pltpu.VMEM((ts, d), F32), pltpu.VMEM((ts, d), F32)])


L = SSD_CHUNK
N_PAIRS = SSD_HEADS // 2


def _ssd_common(xbc, prev8, dt_raw, cw, cb, dtb, alog):
    conv = _conv4(xbc, prev8, cw, cb)
    sg = _sigmoid(conv)
    xa = conv * sg
    dtv = _softplus(dt_raw + dtb)
    a_neg = -jnp.exp(alog)
    rowi = lax.broadcasted_iota(jnp.int32, (L, L), 0)
    coli = lax.broadcasted_iota(jnp.int32, (L, L), 1)
    tril = (rowi >= coli).astype(F32)
    cs = _dot(tril, dtv * a_neg, precision=HI)
    return conv, sg, xa, dtv, a_neg, cs, rowi, coli


def _head_masks():
    lane = lax.broadcasted_iota(jnp.int32, (L, LANES), 1)
    return lane < SSD_HEAD_DIM


def _spread(v, p, first):
    return jnp.where(first[:v.shape[0]], v[:, 2 * p:2 * p + 1], v[:, 2 * p + 1:2 * p + 2])


def _place_head_sums(acc, z, p, first, lane1):
    rows = z.shape[0]
    s0 = jnp.sum(jnp.where(first[:rows], z, 0.0), axis=1, keepdims=True)
    s1 = jnp.sum(jnp.where(first[:rows], 0.0, z), axis=1, keepdims=True)
    lane = lane1[:rows]
    return acc + jnp.where(lane == 2 * p, s0, 0.0) + jnp.where(lane == 2 * p + 1, s1, 0.0)


def _stack_heads(v, first):
    return jnp.concatenate([jnp.where(first, v, 0.0), jnp.where(first, 0.0, v)], axis=0).astype(BF16)


def ssd_fwd(pm, dtr, cw, cb, dtb, alog, d_lanes, nw, w_pb, *, name, bg=None):
    bsz, seq, _ = pm.shape
    nc = seq // L
    inner, cdim = SSD_INNER, SSD_CONV_DIM

    def body(xbc_ref, xp_ref, z_ref, dt_ref, cw_ref, cb_ref, dtb_ref, alog_ref, dl_ref, nw_ref, wpb_ref,
             y_ref, yn_ref, st_ref, yb_ref, state):
        i = pl.program_id(1)

        @pl.when(i == 0)
        def _():
            state[...] = jnp.zeros_like(state)

        prev8 = jnp.where(i == 0, 0.0, xp_ref[...])
        _, _, xa, dtv, _, cs, rowi, coli = _ssd_common(
            xbc_ref[...], prev8, dt_ref[...], cw_ref[...], cb_ref[...], dtb_ref[...], alog_ref[...])
        cst = cs.T
        causal = rowi >= coli
        first = _head_masks()
        for g in range(SSD_GROUPS):
            bg = xa[:, inner + SSD_STATE * g:inner + SSD_STATE * (g + 1)].astype(BF16)
            cg = xa[:, inner + SSD_GROUPS * SSD_STATE + SSD_STATE * g:
                    inner + SSD_GROUPS * SSD_STATE + SSD_STATE * (g + 1)].astype(BF16)
            cbm = _dot_nt(cg, bg)
            for pp in range(2):
                p = 2 * g + pp
                sl = slice(LANES * p, LANES * (p + 1))
                ms = []
                for hh in (2 * p, 2 * p + 1):
                    seg = cs[:, hh:hh + 1] - cst[hh:hh + 1, :]
                    ms.append((cbm * jnp.exp(jnp.where(causal, seg, -jnp.inf))).astype(BF16))
                xsp = xa[:, sl]
                cs_p = _spread(cs, p, first)
                cs_last = cs_p[L - 1:L]
                xp = xsp * _spread(dtv, p, first)
                y_diag = _dot(jnp.concatenate(ms, axis=1), _stack_heads(xp, first))
                st = state[p]
                st_ref[p] = st
                y_off = _dot(cg, st.astype(BF16)) * jnp.exp(cs_p)
                y_ref[:, sl] = y_diag + y_off + dl_ref[:, sl] * xsp
                state[p] = st * jnp.exp(cs_last) + _dot_tn(bg, (xp * jnp.exp(cs_last - cs_p)).astype(BF16))
        zv = z_ref[...]
        yz = y_ref[...] * (zv * _sigmoid(zv))
        gw = inner // SSD_GROUPS
        for g in range(SSD_GROUPS):
            sl = slice(gw * g, gw * (g + 1))
            seg = yz[:, sl]
            yn_ref[:, sl] = ((seg * _rstd(seg)) * nw_ref[:, sl]).astype(BF16)
        yb_ref[...] = _dot(yn_ref[...], wpb_ref[...])

    cvec = lambda w: pl.BlockSpec((1, w), lambda b, i: (0, 0))
    outs, bg_outs = _run(
        body, (pm, pm, pm, dtr, cw, cb, dtb, alog, d_lanes, nw, w_pb), bg, name=name, grid=(bsz, nc),
        in_specs=[pl.BlockSpec((None, L, cdim), lambda b, i: (b, i, 1)), _prev8_spec(cdim, 1, L),
                  pl.BlockSpec((None, L, inner), lambda b, i: (b, i, 1)),
                  pl.BlockSpec((None, L, LANES), lambda b, i: (b, i, 0)),
                  pl.BlockSpec((4, cdim), lambda b, i: (0, 0)), cvec(cdim), cvec(LANES), cvec(LANES),
                  cvec(inner), cvec(inner), pl.BlockSpec(w_pb.shape, lambda b, i: (0, 0))],
        out_specs=[pl.BlockSpec((None, L, inner), lambda b, i: (b, i, 0)),
                   pl.BlockSpec((None, L, inner), lambda b, i: (b, i, 0)),
                   pl.BlockSpec((None, None, N_PAIRS, SSD_STATE, LANES), lambda b, i: (b, i, 0, 0, 0)),
                   pl.BlockSpec((None, L, D_MODEL), lambda b, i: (b, i, 0))],
        out_shape=[jax.ShapeDtypeStruct((bsz, seq, inner), F32), jax.ShapeDtypeStruct((bsz, seq, inner), BF16),
                   jax.ShapeDtypeStruct((bsz, nc, N_PAIRS, SSD_STATE, LANES), F32),
                   jax.ShapeDtypeStruct((bsz, seq, D_MODEL), F32)],
        scratch_shapes=[pltpu.VMEM((N_PAIRS, SSD_STATE, LANES), F32)])
    return outs, bg_outs


def ssd_bwd(pm, dtr, y, states, d_yb, cw, cb, dtb, alog, d_lanes, nw, wt_pb, wt_ssd, *, name):
    bsz, seq, _ = pm.shape
    nc = seq // L
    inner, cdim = SSD_INNER, SSD_CONV_DIM
    per = L // SUBLANES

    def rev(i):
        return nc - 1 - i

    def body(xbc_ref, xp_ref, z_ref, dt_ref, y_ref, st_ref, dyb_ref, cw_ref, cb_ref, dtb_ref, alog_ref,
             dl_ref, nw_ref, wtpb_ref, wts_ref, ds_ref, dh1_ref, ddt_ref, r4_ref, r2_ref, r1_ref,
             dstate, dconv_next, dxs_s, dbc_s):
        b, i = pl.program_id(0), pl.program_id(1)
        t = rev(i)

        @pl.when(i == 0)
        def _():
            dstate[...] = jnp.zeros_like(dstate)
            dconv_next[...] = jnp.zeros_like(dconv_next)

        @pl.when((b == 0) & (i == 0))
        def _():
            r4_ref[...] = jnp.zeros_like(r4_ref)
            r2_ref[...] = jnp.zeros_like(r2_ref)
            r1_ref[...] = jnp.zeros_like(r1_ref)

        xbc = xbc_ref[...]
        prev8 = jnp.where(t == 0, 0.0, xp_ref[...])
        cwv = cw_ref[...]
        dt_in = dt_ref[...] + dtb_ref[...]
        conv, sg, xa, dtv, a_neg, cs, rowi, coli = _ssd_common(
            xbc, prev8, dt_ref[...], cwv, cb_ref[...], dtb_ref[...], alog_ref[...])
        cst = cs.T
        causal = rowi >= coli
        anti = coli >= rowi
        first = _head_masks()
        lane1 = lax.broadcasted_iota(jnp.int32, (L, LANES), 1)

        yv = y_ref[...]
        zv = z_ref[...]
        sz = _sigmoid(zv)
        zs = zv * sz
        yz = yv * zs
        dyn = _dot(dyb_ref[...], wtpb_ref[...])
        gw = inner // SSD_GROUPS
        d_yz_parts, d_nw_parts = [], []
        for g in range(SSD_GROUPS):
            sl = slice(gw * g, gw * (g + 1))
            seg = yz[:, sl]
            r = _rstd(seg)
            n = seg * r
            d_nw_parts.append(_colsum(dyn[:, sl] * n))
            d_yz_parts.append(_rms_bwd(dyn[:, sl] * nw_ref[:, sl], n, r))
        d_yz = jnp.concatenate(d_yz_parts, axis=1)
        d_y = d_yz * zs
        ds_ref[:, :inner] = (d_yz * yv * (sz * (1.0 + zv * (1.0 - sz)))).astype(BF16)

        a1 = jnp.zeros((L, LANES), F32)
        a2 = jnp.zeros((L, LANES), F32)
        xs_dxt = jnp.zeros((L, LANES), F32)
        c0 = jnp.zeros((1, LANES), F32)
        d_dl = jnp.zeros((1, LANES), F32)
        for g in range(SSD_GROUPS):
            bsl = slice(inner + SSD_STATE * g, inner + SSD_STATE * (g + 1))
            csl = slice(inner + SSD_GROUPS * SSD_STATE + SSD_STATE * g,
                        inner + SSD_GROUPS * SSD_STATE + SSD_STATE * (g + 1))
            bg = xa[:, bsl].astype(BF16)
            cg = xa[:, csl].astype(BF16)
            cbm = _dot_nt(cg, bg)
            cbt = _dot_nt(bg, cg)
            d_cb = jnp.zeros((L, L), F32)
            d_bg = jnp.zeros((L, SSD_STATE), F32)
            d_cg = jnp.zeros((L, SSD_STATE), F32)
            for pp in range(2):
                p = 2 * g + pp
                sl = slice(LANES * p, LANES * (p + 1))
                xsp = xa[:, sl]
                dt_p = _spread(dtv, p, first)
                cs_p = _spread(cs, p, first)
                cs_last = cs_p[L - 1:L]
                e_p = jnp.exp(cs_p)
                w_p = jnp.exp(cs_last - cs_p)
                e_last = jnp.exp(cs_last)
                xp = xsp * dt_p
                xpb = xp.astype(BF16)
                dyp = d_y[:, sl]
                dypb = dyp.astype(BF16)
                dy_heads = (jnp.where(first, dyp, 0.0).astype(BF16), jnp.where(first, 0.0, dyp).astype(BF16))
                x_heads = (jnp.where(first, xp, 0.0).astype(BF16), jnp.where(first, 0.0, xp).astype(BF16))
                mts = []
                for k, hh in enumerate((2 * p, 2 * p + 1)):
                    col = cs[:, hh:hh + 1]
                    row = cst[hh:hh + 1, :]
                    dec = jnp.exp(jnp.where(causal, col - row, -jnp.inf))
                    dec_t = jnp.exp(jnp.where(anti, row - col, -jnp.inf))
                    gd = _dot_nt(dy_heads[k], xpb) * dec
                    d_cb = d_cb + gd
                    mt = cbt * dec_t
                    qd = gd * cbm - _dot_nt(x_heads[k], dypb) * mt
                    a1 = a1 + jnp.where(lane1 == hh, jnp.sum(qd, axis=1, keepdims=True), 0.0)
                    mts.append(mt.astype(BF16))
                dst = dstate[p]
                dstb = dst.astype(BF16)
                st = st_ref[p]
                stb = st.astype(BF16)
                dye = (dyp * e_p).astype(BF16)
                xw = (xp * w_p).astype(BF16)
                dx_off = w_p * _dot(bg, dstb)
                d_xp = _dot(jnp.concatenate(mts, axis=1), jnp.concatenate(dy_heads, axis=0)) + dx_off
                dxs_s[:, sl] = d_xp * dt_p + dyp * dl_ref[:, sl]
                a1 = _place_head_sums(a1, dyp * (_dot(cg, stb) * e_p), p, first, lane1)
                a2 = _place_head_sums(a2, xp * dx_off, p, first, lane1)
                xs_dxt = _place_head_sums(xs_dxt, d_xp * xsp, p, first, lane1)
                c0 = _place_head_sums(c0, _colsum(dst * st) * e_last, p, first, lane1)
                d_dl = _place_head_sums(d_dl, _colsum(dyp * xsp), p, first, lane1)
                d_cg = d_cg + _dot_nt(dye, stb)
                d_bg = d_bg + _dot_nt(xw, dstb)
                dstate[p] = dst * e_last + _dot_tn(cg, dye)
            d_cbb = d_cb.astype(BF16)
            dbc_s[:, SSD_STATE * g:SSD_STATE * (g + 1)] = d_bg + _dot_tn(d_cbb, cg)
            dbc_s[:, SSD_GROUPS * SSD_STATE + SSD_STATE * g:SSD_GROUPS * SSD_STATE + SSD_STATE * (g + 1)] = (
                d_cg + _dot(d_cbb, bg))

        d_da = (_dot(anti.astype(F32), a1, precision=HI) + _dot((rowi > coli).astype(F32), a2, precision=HI) + c0)
        d_dt = d_da * a_neg + xs_dxt
        d_alog = _colsum(d_da * dtv) * a_neg
        d_dtr = jnp.where(lane1 < SSD_HEADS, d_dt * _sigmoid(dt_in), 0.0)
        ddt_ref[...] = d_dtr.astype(BF16)
        d_xa = jnp.concatenate([dxs_s[...], dbc_s[...]], axis=1)
        d_conv = d_xa * (sg * (1.0 + conv * (1.0 - sg)))
        d_xbc, d_cw, d_cbias = _conv4_bwd(d_conv, dconv_next[...], xbc, cwv)
        dconv_next[...] = d_conv[0:SUBLANES]
        ds_ref[:, inner:] = d_xbc.astype(BF16)
        dh1_ref[...] = _dot(ds_ref[...], wts_ref[...])
        r4_ref[...] += _stack_rows([d_cbias] + d_cw, cdim)
        r2_ref[...] += _stack_rows([jnp.concatenate(d_nw_parts, axis=1)], inner)
        r1_ref[...] += _stack_rows([_colsum(d_dtr), d_alog, d_dl], LANES)

    cvec = lambda w: pl.BlockSpec((1, w), lambda b, i: (0, 0))
    return _pcall(
        body, name=name, grid=(bsz, nc),
        in_specs=[pl.BlockSpec((None, L, cdim), lambda b, i: (b, rev(i), 1)),
                  pl.BlockSpec((None, SUBLANES, cdim), lambda b, i: (b, jnp.maximum(rev(i) * per - 1, 0), 1)),
                  pl.BlockSpec((None, L, inner), lambda b, i: (b, rev(i), 1)),
                  pl.BlockSpec((None, L, LANES), lambda b, i: (b, rev(i), 0)),
                  pl.BlockSpec((None, L, inner), lambda b, i: (b, rev(i), 0)),
                  pl.BlockSpec((None, None, N_PAIRS, SSD_STATE, LANES), lambda b, i: (b, rev(i), 0, 0, 0)),
                  pl.BlockSpec((None, L, D_MODEL), lambda b, i: (b, rev(i), 0)),
                  pl.BlockSpec((4, cdim), lambda b, i: (0, 0)), cvec(cdim), cvec(LANES), cvec(LANES),
                  cvec(inner), cvec(inner), pl.BlockSpec(wt_pb.shape, lambda b, i: (0, 0)),
                  pl.BlockSpec(wt_ssd.shape, lambda b, i: (0, 0))],
        out_specs=[pl.BlockSpec((None, L, inner + cdim), lambda b, i: (b, rev(i), 0)),
                   pl.BlockSpec((None, L, D_MODEL), lambda b, i: (b, rev(i), 0)),
                   pl.BlockSpec((None, L, LANES), lambda b, i: (b, rev(i), 0)),
                   pl.BlockSpec((SUBLANES, cdim), lambda b, i: (0, 0)),
                   pl.BlockSpec((SUBLANES, inner), lambda b, i: (0, 0)),
                   pl.BlockSpec((SUBLANES, LANES), lambda b, i: (0, 0))],
        out_shape=[jax.ShapeDtypeStruct((bsz, seq, inner + cdim), BF16),
                   jax.ShapeDtypeStruct((bsz, seq, D_MODEL), F32),
                   jax.ShapeDtypeStruct((bsz, seq, LANES), BF16),
                   jax.ShapeDtypeStruct((SUBLANES, cdim), F32),
                   jax.ShapeDtypeStruct((SUBLANES, inner), F32),
                   jax.ShapeDtypeStruct((SUBLANES, LANES), F32)],
        scratch_shapes=[pltpu.VMEM((N_PAIRS, SSD_STATE, LANES), F32), pltpu.VMEM((SUBLANES, cdim), F32),
                        pltpu.VMEM((L, inner), F32), pltpu.VMEM((L, 2 * SSD_GROUPS * SSD_STATE), F32)],
    )(pm, pm, pm, dtr, y, states, d_yb, cw, cb, dtb, alog, d_lanes, nw, wt_pb, wt_ssd)


def _lru_block_weights(w):
    w = w.reshape(N_LRU_BLOCKS, 2, LRU_HEAD_DIM, LRU_HEAD_DIM)
    z = jnp.zeros((N_LRU_BLOCKS, LRU_HEAD_DIM, LRU_HEAD_DIM), w.dtype)
    top = jnp.concatenate([w[:, 0], z], axis=2)
    bot = jnp.concatenate([z, w[:, 1]], axis=2)
    return jnp.concatenate([top, bot], axis=1).astype(BF16)


def _lru_block_grads(g):
    h = LRU_HEAD_DIM
    return jnp.stack([g[:, :h, :h], g[:, h:, h:]], axis=1).reshape(LRU_HEADS, h, h)


def _pad_lanes(v, width=LANES):
    return jnp.pad(v, ((0, 0), (0, width - v.shape[1])))


class NoExchange:
    def __init__(self, weights):
        self._weights, self.grads = weights, {}

    def weights_bg(self):
        return None

    def weights(self, bg_outs):
        return self._weights

    def grads_bg(self, grads):
        self.grads.update(grads)
        return None

    def grads_done(self, bg_outs):
        pass


def local_step(x, target, mod, big, small, plan):
    bsz, seq, d = x.shape
    t = bsz * seq
    flat = lambda v: v.reshape(t, v.shape[-1])
    unflat = lambda v: v.reshape(bsz, seq, v.shape[-1])

    wa_b = _lru_block_weights(small["lru_wa"])
    wx_b = _lru_block_weights(small["lru_wx"])
    dtb = _pad_lanes(small["ssd_dt_bias"])
    alog = _pad_lanes(small["ssd_a_log"])
    d_lanes = jnp.repeat(small["ssd_d"], SSD_HEAD_DIM, axis=1)

    lru_cols = 2 * D_MODEL
    wt = {"lru": big["w_main"][:, :lru_cols].T, "ssd": big["w_main"][:, lru_cols:].T, "gates": big["w_gates"].T,
          "dt": big["w_dt"].T}

    h1 = prenorm(x, small["pre_norm1"], mod, name="prenorm1")
    h1f = flat(h1)
    arriving = plan.weights_bg()
    if arriving is None:
        pm, arrived = mm_nn([(h1f, big["w_main"])], name="in_proj_main"), []
    else:
        pm, arrived = mm_nn([(h1f, big["w_main"])], name="in_proj_main", bg=arriving)
    pm = unflat(pm)
    big = dict(big, **plan.weights(arrived))
    for n in ("w_pa", "w_pb", "w_out", "w_ff1", "w_ff2"):
        wt[n] = big[n].T
    gates = unflat(mm_nn([(h1f, big["w_gates"])], name="in_proj_gates"))
    dtr = unflat(mm_nn([(h1f, big["w_dt"])], name="in_proj_dt"))
    lru_args = (small["lru_conv_w"], small["lru_conv_b"], wa_b, small["lru_ba"], wx_b, small["lru_bx"],
                small["lru_lambda"])
    h_lru, pa_in, ya = lru_fwd(pm, *lru_args, big["w_pa"], name="lru_fwd")
    ssd_args = (small["ssd_conv_w"], small["ssd_conv_b"], dtb, alog, d_lanes, small["ssd_norm_w"])
    (y_ssd, ynorm, states, yb), _ = ssd_fwd(pm, dtr, *ssd_args, big["w_pb"], name="ssd_fwd")
    merged = merge_fwd(ya, yb, gates, small["b_gate"], name="merge_fwd")
    out1 = unflat(mm_nn([(flat(merged), big["w_out"])], name="proj_out"))
    x1, h2 = post1_pre2(x, out1, mod, small["post_norm1"], small["pre_norm2"], name="post1_pre2")
    f = mm_nn([(flat(h2), big["w_ff1"])], name="ff1")
    y2 = unflat(mm_nn([(f, big["w_ff2"])], a_fn=_relu_sq, name="ff2"))

    dx2, d_y2, pb_a, gl_a = final_bwd(x1, y2, target, mod, small["post_norm2"], name="final_bwd")
    d_y2f = flat(d_y2)
    d_f = mm_nn([(d_y2f, wt["w_ff2"])], out_dtype=BF16, extra=f,
                epi=lambda r, fv: r * (2.0 * jnp.maximum(fv, 0.0)), name="ff2_dx")
    g_ff2 = mm_tn(f, d_y2f, a_fn=_relu_sq, name="ff2_dw")
    d_h2 = unflat(mm_nn([(d_f, wt["w_ff1"])], name="ff1_dx"))
    g_ff1 = mm_tn(flat(h2), d_f, name="ff1_dw")
    dx1, d_out1, pb_b, gl_b = mid_bwd(d_h2, dx2, x1, out1, mod, small["pre_norm2"], small["post_norm1"],
                                      name="mid_bwd")
    d_out1f = flat(d_out1)
    d_merged = unflat(mm_nn([(d_out1f, wt["w_out"])], name="out_dx"))
    g_out = mm_tn(flat(merged), d_out1f, name="out_dw")
    d_ya, d_yb, d_gates, gl_c = merge_bwd(d_merged, ya, yb, gates, small["b_gate"], name="merge_bwd")
    g_pa = mm_tn(flat(pa_in), flat(d_ya), name="pa_dw")
    g_pb = mm_tn(flat(ynorm), flat(d_yb), name="pb_dw")
    leaving = plan.grads_bg({"w_pa": g_pa, "w_pb": g_pb, "w_out": g_out, "w_ff1": g_ff1, "w_ff2": g_ff2})
    (d_l, dh_lru, g_wa_b, g_wx_b, lru_rows), landed = lru_bwd(
        pm, h_lru, d_ya, *lru_args, wt["w_pa"], wt["lru"], name="lru_bwd", bg=leaving)
    plan.grads_done(landed)
    d_s, dh_ssd, d_dt, r4, r2, r1 = ssd_bwd(pm, dtr, y_ssd, states, d_yb, *ssd_args, wt["w_pb"], wt["ssd"],
                                          name="ssd_bwd")
    d_lf, d_sf, d_gf, d_dtf = flat(d_l), flat(d_s), flat(d_gates), flat(d_dt)
    g_in = jnp.concatenate([
        mm_tn(h1f, d_lf, name="in_dw_lru"), mm_tn(h1f, d_sf, name="in_dw_ssd"),
        mm_tn(h1f, d_dtf, name="in_dw_dt")[:, :SSD_HEADS], mm_tn(h1f, d_gf, name="in_dw_gates")], axis=1)
    leaving = plan.grads_bg({"w_in": g_in})
    partial = [flat(dh_lru), flat(dh_ssd)]
    if leaving is None:
        d_h1 = mm_nn([(d_gf, wt["gates"]), (d_dtf, wt["dt"])], add=partial, name="in_dx_gates")
    else:
        d_h1, landed = mm_nn([(d_gf, wt["gates"]), (d_dtf, wt["dt"])], add=partial, name="in_dx_gates", bg=leaving)
        plan.grads_done(landed)
    grad_x, pb_c, gl_d = first_bwd(unflat(d_h1), dx1, x, mod, small["pre_norm1"], name="first_bwd")

    d_mod = jnp.stack([pb_c[:, 0], pb_c[:, 1], pb_b[:, 2], pb_b[:, 0], pb_b[:, 1], pb_a[:, 0]], axis=1)
    loss_cols = gl_a[1:2]
    nh = SSD_HEADS
    small_grads = {
        "pre_norm1": gl_d[0:1], "post_norm1": gl_b[1:2], "b_gate": gl_c[0:1],
        "lru_conv_w": lru_rows[4:8], "lru_conv_b": lru_rows[3:4],
        "lru_wa": _lru_block_grads(g_wa_b), "lru_ba": lru_rows[0:1],
        "lru_wx": _lru_block_grads(g_wx_b), "lru_bx": lru_rows[1:2], "lru_lambda": lru_rows[2:3],
        "ssd_conv_w": r4[1:5], "ssd_conv_b": r4[0:1],
        "ssd_dt_bias": r1[0:1, :nh], "ssd_a_log": r1[1:2, :nh], "ssd_d": r1[2:3, :nh],
        "ssd_norm_w": r2[0:1], "pre_norm2": gl_b[0:1], "post_norm2": gl_a[0:1],
    }
    return loss_cols, grad_x, d_mod, small_grads


def _position():
    return lax.axis_index("x"), lax.axis_index("y"), lax.axis_index("c")


def _other_chips(x, y):
    return [(1 - x, y), (x, 1 - y), (1 - x, 1 - y)]


def allgather8(v, *, name):
    m_per, n = v.shape

    def body(x_ref, out_ref, send_sems, recv_sems, local_sem):
        x, y, c = _position()
        me, sibling = (x, y, c), (x, y, 1 - c)
        chips = _other_chips(x, y)

        def rows(px, py, pc):
            return out_ref.at[pl.ds((4 * px + 2 * py + pc) * m_per, m_per), :]

        def copy(k, block, to, src=None):
            return pltpu.make_async_remote_copy(
                src_ref=rows(*block) if src is None else src, dst_ref=rows(*block),
                send_sem=send_sems.at[k], recv_sem=recv_sems.at[k], device_id=to, device_id_type=MESH)

        mine = pltpu.make_async_copy(x_ref, rows(*me), local_sem)
        mine.start()
        first = [copy(0, me, sibling, src=x_ref)]
        first += [copy(1 + j, me, (*chip, c), src=x_ref) for j, chip in enumerate(chips)]
        for cp in first:
            cp.start()
        passed = [copy(4 + j, (*chip, c), sibling) for j, chip in enumerate(chips)]
        for j, chip in enumerate(chips):
            copy(1 + j, (*chip, c), me).wait_recv()
            passed[j].start()
        copy(0, sibling, me).wait_recv()
        for j, chip in enumerate(chips):
            copy(4 + j, (*chip, 1 - c), me).wait_recv()
        for cp in first + passed:
            cp.wait_send()
        mine.wait()

    return _pcall(
        body, name=name,
        out_shape=jax.ShapeDtypeStruct((N_DEV * m_per, n), v.dtype),
        in_specs=[pl.BlockSpec(memory_space=pltpu.VMEM)],
        out_specs=pl.BlockSpec(memory_space=pltpu.VMEM),
        scratch_shapes=[pltpu.SemaphoreType.DMA((7,)), pltpu.SemaphoreType.DMA((7,)), pltpu.SemaphoreType.DMA],
    )(v)


def gather_weights(shards, *, name):
    n = len(shards)
    half = [s.shape[0] // 2 for s in shards]

    def body(*refs):
        ins, outs = refs[:n], refs[n:2 * n]
        send_sems, recv_sems, local_sems = refs[2 * n:]
        x, y, c = _position()
        me_chip = 2 * x + y
        chips = _other_chips(x, y)

        def piece(w, chip, core):
            return outs[w].at[chip, pl.ds(core * half[w], half[w]), :]

        def copy(w, k, chip, core, to, src=None):
            dst = piece(w, chip, core)
            return pltpu.make_async_remote_copy(
                src_ref=dst if src is None else src, dst_ref=dst,
                send_sem=send_sems.at[6 * w + k], recv_sem=recv_sems.at[6 * w + k], device_id=to, device_id_type=MESH)

        local = [pltpu.make_async_copy(ins[w], outs[w].at[me_chip], local_sems.at[w]) for w in range(n)]
        for cp in local:
            cp.start()
        sent = []
        for w in range(n):
            for j, (px, py) in enumerate(chips):
                cp = copy(w, j, me_chip, c, (px, py, c), src=ins[w].at[pl.ds(c * half[w], half[w]), :])
                cp.start()
                sent.append(cp)
        for w in range(n):
            for j, (px, py) in enumerate(chips):
                copy(w, j, 2 * px + py, c, (px, py, c)).wait_recv()
                cp = copy(w, 3 + j, 2 * px + py, c, (x, y, 1 - c))
                cp.start()
                sent.append(cp)
        for w in range(n):
            for j, (px, py) in enumerate(chips):
                copy(w, 3 + j, 2 * px + py, 1 - c, (x, y, 1 - c)).wait_recv()
        for cp in sent:
            cp.wait_send()
        for cp in local:
            cp.wait()

    return _pcall(
        body, name=name,
        out_shape=[jax.ShapeDtypeStruct((N_CHIPS,) + s.shape, s.dtype) for s in shards],
        in_specs=[ANY] * n, out_specs=[ANY] * n,
        scratch_shapes=[pltpu.SemaphoreType.DMA((6 * n,)), pltpu.SemaphoreType.DMA((6 * n,)),
                        pltpu.SemaphoreType.DMA((n,))],
    )(*shards)


STAGE_BYTES = 2 << 20


def _stage_rows(rows, width, itemsize=4):
    return _pick(rows, tuple(t for t in (1024, 512, 256, 128, 64, 32, 16, 8) if t * width * itemsize <= STAGE_BYTES * 3 // 2))


def _staged(chunks, bufs, load_sems):
    count, pending = {}, {}

    def load(i):
        cls, src, _ = chunks[i]
        slot = count.get(cls, 0) % 2
        count[cls] = count.get(cls, 0) + 1
        for cp, remote in pending.pop((cls, slot), []):
            if remote:
                cp.wait_send()
            else:
                cp.wait()
        staged = bufs[cls].at[slot, pl.ds(0, src.shape[0]), :]
        ld = pltpu.make_async_copy(src, staged, load_sems[cls].at[slot])
        ld.start()
        return ld, cls, slot, staged

    cur = load(0)
    for i in range(len(chunks)):
        nxt = load(i + 1) if i + 1 < len(chunks) else None
        ld, cls, slot, staged = cur
        ld.wait()
        started = []
        for make in chunks[i][2]:
            cp, remote = make(staged, slot)
            cp.start()
            started.append((cp, remote))
        pending[(cls, slot)] = started
        cur = nxt
    for started in pending.values():
        for cp, remote in started:
            if remote:
                cp.wait_send()
            else:
                cp.wait()


def _stage_scratch(widths_rows, dtype):
    scratch = []
    for width, rows in widths_rows:
        scratch += [pltpu.VMEM((2, rows, width), dtype), pltpu.SemaphoreType.DMA((2,)), pltpu.SemaphoreType.DMA((2,)),
                    pltpu.SemaphoreType.DMA((2,))]
    return scratch


def send_half_to_sibling(grads, *, name):
    n = len(grads)
    half = [g.shape[1] // 2 for g in grads]
    widths = sorted({g.shape[2] for g in grads})
    chunk_rows = [_stage_rows(h, g.shape[2]) for g, h in zip(grads, half)]
    plan = [(w, k, r0) for w in range(n) for k in range(N_CHIPS) for r0 in range(0, half[w], chunk_rows[w])]

    def body(*refs):
        ins, theirs = refs[:n], refs[n:2 * n]
        recv_sems = refs[2 * n]
        stage = refs[2 * n + 1:]
        bufs = {wd: stage[4 * i] for i, wd in enumerate(widths)}
        load_sems = {wd: stage[4 * i + 1] for i, wd in enumerate(widths)}
        send_sems = {wd: stage[4 * i + 2] for i, wd in enumerate(widths)}
        x, y, c = _position()
        chunks = []
        for idx, (w, k, r0) in enumerate(plan):
            wd = grads[w].shape[2]
            rb = chunk_rows[w]

            def make(staged, slot, idx=idx, w=w, k=k, r0=r0, wd=wd, rb=rb):
                return pltpu.make_async_remote_copy(
                    src_ref=staged, dst_ref=theirs[w].at[k, pl.ds(r0, rb), :], send_sem=send_sems[wd].at[slot],
                    recv_sem=recv_sems.at[idx], device_id=(x, y, 1 - c), device_id_type=MESH), True

            chunks.append((wd, ins[w].at[k, pl.ds((1 - c) * half[w] + r0, rb), :], [make]))
        _staged(chunks, bufs, load_sems)
        for idx, (w, k, r0) in enumerate(plan):
            wd = grads[w].shape[2]
            landed = theirs[w].at[k, pl.ds(r0, chunk_rows[w]), :]
            pltpu.make_async_remote_copy(
                src_ref=landed, dst_ref=landed, send_sem=send_sems[wd].at[0], recv_sem=recv_sems.at[idx],
                device_id=(x, y, 1 - c), device_id_type=MESH).wait_recv()

    stage_rows = [(wd, max(r for g, r in zip(grads, chunk_rows) if g.shape[2] == wd)) for wd in widths]
    return _pcall(
        body, name=name,
        out_shape=[jax.ShapeDtypeStruct((N_CHIPS, h, g.shape[2]), g.dtype) for g, h in zip(grads, half)],
        in_specs=[ANY] * n, out_specs=[ANY] * n,
        scratch_shapes=[pltpu.SemaphoreType.DMA((len(plan),))] + _stage_scratch(stage_rows, F32),
    )(*grads)


def _chip_exchange_background(arrays, out_shapes, src_of, dst_of, landed_of, own_of):
    n = len(arrays)

    def copies(ins, outs, scr):
        send_sems, recv_sems, local_sems = scr
        x, y, c = _position()
        me_chip = 2 * x + y
        local, sends, recvs = [], [], []
        for w in range(n):
            local.append(pltpu.make_async_copy(*own_of(ins[w], outs[w], w, me_chip), local_sems.at[w]))
            for j, (px, py) in enumerate(_other_chips(x, y)):
                sems = dict(send_sem=send_sems.at[3 * w + j], recv_sem=recv_sems.at[3 * w + j],
                            device_id=(px, py, c), device_id_type=MESH)
                sends.append(pltpu.make_async_remote_copy(
                    src_ref=src_of(ins[w], w, 2 * px + py, me_chip, c), dst_ref=dst_of(outs[w], w, me_chip, c), **sems))
                landed = landed_of(outs[w], w, 2 * px + py, c)
                recvs.append(pltpu.make_async_remote_copy(src_ref=landed, dst_ref=landed, **sems))
        return local, sends, recvs

    def start(ins, outs, scr):
        local, sends, _ = copies(ins, outs, scr)
        for cp in local + sends:
            cp.start()

    def finish(ins, outs, scr):
        local, sends, recvs = copies(ins, outs, scr)
        for cp in recvs:
            cp.wait_recv()
        for cp in sends:
            cp.wait_send()
        for cp in local:
            cp.wait()

    scratch = [pltpu.SemaphoreType.DMA((3 * n,)), pltpu.SemaphoreType.DMA((3 * n,)), pltpu.SemaphoreType.DMA((n,))]
    return Background(arrays, out_shapes, scratch, start, finish)


def scatter_background(parts):
    return _chip_exchange_background(
        parts, [jax.ShapeDtypeStruct(p.shape, p.dtype) for p in parts],
        src_of=lambda ref, w, peer, me, c: ref.at[peer], dst_of=lambda ref, w, me, c: ref.at[me],
        landed_of=lambda ref, w, peer, c: ref.at[peer], own_of=lambda i, o, w, me: (i.at[me], o.at[me]))


def gather_halves_background(shards):
    half = [s.shape[0] // 2 for s in shards]
    rows = lambda w, c: pl.ds(c * half[w], half[w])
    return _chip_exchange_background(
        shards, [jax.ShapeDtypeStruct((N_CHIPS,) + s.shape, s.dtype) for s in shards],
        src_of=lambda ref, w, peer, me, c: ref.at[rows(w, c), :], dst_of=lambda ref, w, me, c: ref.at[me, rows(w, c), :],
        landed_of=lambda ref, w, peer, c: ref.at[peer, rows(w, c), :], own_of=lambda i, o, w, me: (i, o.at[me]))


def fill_other_half(gathered, *, name):
    n = len(gathered)
    half = [g.shape[1] // 2 for g in gathered]
    widths = sorted({g.shape[2] for g in gathered})
    chunk_rows = [_stage_rows(h, g.shape[2], itemsize=2) for g, h in zip(gathered, half)]
    plan = [(w, j, r0) for w in range(n) for j in range(N_CHIPS - 1) for r0 in range(0, half[w], chunk_rows[w])]

    def body(*refs):
        ins, outs = refs[:n], refs[n:2 * n]
        recv_sems = refs[2 * n]
        stage = refs[2 * n + 1:]
        bufs = {wd: stage[4 * i] for i, wd in enumerate(widths)}
        load_sems = {wd: stage[4 * i + 1] for i, wd in enumerate(widths)}
        send_sems = {wd: stage[4 * i + 2] for i, wd in enumerate(widths)}
        x, y, c = _position()
        chips = _other_chips(x, y)
        chunks = []
        for idx, (w, j, r0) in enumerate(plan):
            wd, rb = gathered[w].shape[2], chunk_rows[w]
            k = 2 * chips[j][0] + chips[j][1]

            def make(staged, slot, idx=idx, w=w, k=k, r0=r0, wd=wd, rb=rb):
                return pltpu.make_async_remote_copy(
                    src_ref=staged, dst_ref=outs[w].at[k, pl.ds(c * half[w] + r0, rb), :],
                    send_sem=send_sems[wd].at[slot], recv_sem=recv_sems.at[idx],
                    device_id=(x, y, 1 - c), device_id_type=MESH), True

            chunks.append((wd, ins[w].at[k, pl.ds(c * half[w] + r0, rb), :], [make]))
        _staged(chunks, bufs, load_sems)
        for idx, (w, j, r0) in enumerate(plan):
            wd = gathered[w].shape[2]
            k = 2 * chips[j][0] + chips[j][1]
            landed = outs[w].at[k, pl.ds((1 - c) * half[w] + r0, chunk_rows[w]), :]
            pltpu.make_async_remote_copy(
                src_ref=landed, dst_ref=landed, send_sem=send_sems[wd].at[0], recv_sem=recv_sems.at[idx],
                device_id=(x, y, 1 - c), device_id_type=MESH).wait_recv()

    stage_rows = [(wd, max(r for g, r in zip(gathered, chunk_rows) if g.shape[2] == wd)) for wd in widths]
    return _pcall(
        body, name=name, out_shape=[jax.ShapeDtypeStruct(g.shape, g.dtype) for g in gathered],
        in_specs=[ANY] * n, out_specs=[ANY] * n, input_output_aliases={w: w for w in range(n)},
        scratch_shapes=[pltpu.SemaphoreType.DMA((len(plan),))] + _stage_scratch(stage_rows, gathered[0].dtype),
    )(*gathered)


def join_with_sibling(halves, *, name):
    n = len(halves)
    widths = sorted({h.shape[1] for h in halves})
    chunk_rows = [_stage_rows(h.shape[0], h.shape[1]) for h in halves]
    plan = [(w, r0) for w in range(n) for r0 in range(0, halves[w].shape[0], chunk_rows[w])]

    def body(*refs):
        ins, outs = refs[:n], refs[n:2 * n]
        recv_sems = refs[2 * n]
        stage = refs[2 * n + 1:]
        bufs = {wd: stage[4 * i] for i, wd in enumerate(widths)}
        load_sems = {wd: stage[4 * i + 1] for i, wd in enumerate(widths)}
        send_sems = {wd: stage[4 * i + 2] for i, wd in enumerate(widths)}
        store_sems = {wd: stage[4 * i + 3] for i, wd in enumerate(widths)}
        x, y, c = _position()
        chunks = []
        for idx, (w, r0) in enumerate(plan):
            h, wd = halves[w].shape
            rb = chunk_rows[w]

            def to_sibling(staged, slot, idx=idx, w=w, r0=r0, h=h, wd=wd, rb=rb):
                return pltpu.make_async_remote_copy(
                    src_ref=staged, dst_ref=outs[w].at[pl.ds(c * h + r0, rb), :], send_sem=send_sems[wd].at[slot],
                    recv_sem=recv_sems.at[idx], device_id=(x, y, 1 - c), device_id_type=MESH), True

            def to_mine(staged, slot, w=w, r0=r0, h=h, wd=wd, rb=rb):
                return pltpu.make_async_copy(staged, outs[w].at[pl.ds(c * h + r0, rb), :], store_sems[wd].at[slot]), False

            chunks.append((wd, ins[w].at[pl.ds(r0, rb), :], [to_sibling, to_mine]))
        _staged(chunks, bufs, load_sems)
        for idx, (w, r0) in enumerate(plan):
            h, wd = halves[w].shape
            landed = outs[w].at[pl.ds((1 - c) * h + r0, chunk_rows[w]), :]
            pltpu.make_async_remote_copy(
                src_ref=landed, dst_ref=landed, send_sem=send_sems[wd].at[0], recv_sem=recv_sems.at[idx],
                device_id=(x, y, 1 - c), device_id_type=MESH).wait_recv()

    stage_rows = [(wd, max(r for h, r in zip(halves, chunk_rows) if h.shape[1] == wd)) for wd in widths]
    return _pcall(
        body, name=name,
        out_shape=[jax.ShapeDtypeStruct((2 * h.shape[0], h.shape[1]), h.dtype) for h in halves],
        in_specs=[ANY] * n, out_specs=[ANY] * n,
        scratch_shapes=[pltpu.SemaphoreType.DMA((len(plan),))] + _stage_scratch(stage_rows, F32),
    )(*halves)


def _row_tile(rows, cols, itemsize=4, budget=2 << 20):
    for t in (1024, 512, 256, 128, 64, 32, 16, 8):
        if rows % t == 0 and t * cols * itemsize <= budget:
            return t
    return rows


def add_half_to_bf16(core, full, theirs, *, name):
    k, r, c = theirs.shape
    tr = _row_tile(r, c)
    nb = r // tr

    def body(core_ref, a_ref, b_ref, o_ref):
        o_ref[...] = (a_ref[...] + b_ref[...]).astype(BF16)

    spec = pl.BlockSpec((None, tr, c), lambda i, j, core_ref: (i, j, 0))
    grid_spec = pltpu.PrefetchScalarGridSpec(
        num_scalar_prefetch=1, grid=(k, nb),
        in_specs=[pl.BlockSpec((None, tr, c), lambda i, j, core_ref: (i, core_ref[0] * nb + j, 0)), spec],
        out_specs=spec)
    return _pcall(body, name=name, grid_spec=grid_spec,
                  out_shape=jax.ShapeDtypeStruct(theirs.shape, BF16))(core, full, theirs)


def sum_blocks(v, *, name):
    k, r, c = v.shape
    tr = _row_tile(r, c * k)

    def body(v_ref, o_ref):
        acc = v_ref[0].astype(F32)
        for j in range(1, k):
            acc = acc + v_ref[j].astype(F32)
        o_ref[...] = acc

    return _pcall(body, name=name, grid=(r // tr,),
                  in_specs=[pl.BlockSpec((k, tr, c), lambda i: (0, i, 0))],
                  out_specs=pl.BlockSpec((tr, c), lambda i: (i, 0)),
                  out_shape=jax.ShapeDtypeStruct((r, c), F32))(v)


def adamw(w, g, m, v, *, name):
    r, c = w.shape
    tr = _row_tile(r, c, budget=1 << 20)
    m_scale = 1.0 / (1.0 - ADAM_B1 ** ADAM_STEP)
    v_scale = 1.0 / (1.0 - ADAM_B2 ** ADAM_STEP)

    def body(w_ref, g_ref, m_ref, v_ref, d_ref, nm_ref, nv_ref):
        gv = g_ref[...]
        nm = ADAM_B1 * m_ref[...] + (1.0 - ADAM_B1) * gv
        nv = ADAM_B2 * v_ref[...] + (1.0 - ADAM_B2) * (gv * gv)
        nm_ref[...] = nm
        nv_ref[...] = nv
        d_ref[...] = -ADAM_LR * ((nm * m_scale) / (jnp.sqrt(nv * v_scale) + ADAM_EPS) + ADAM_WD * w_ref[...])

    spec = pl.BlockSpec((tr, c), lambda i: (i, 0))
    return _pcall(body, name=name, grid=(r // tr,), in_specs=[spec] * 4, out_specs=[spec] * 3,
                  out_shape=[jax.ShapeDtypeStruct((r, c), F32)] * 3)(w, g, m, v)


def ada_fwd(c_all, w_shard, b_shard, *, name):
    bsz, d = c_all.shape
    ncol = w_shard.shape[1]

    def body(c_ref, w_ref, b_ref, o_ref):
        cv = c_ref[...]
        act = (cv * _sigmoid(cv)).astype(BF16)
        o_ref[...] = _dot(act, w_ref[...].astype(BF16)) + b_ref[...]

    tn = _pick(ncol, (512, 256, 128))
    return _pcall(body, name=name, grid=(ncol // tn,),
                  in_specs=[pl.BlockSpec((bsz, d), lambda j: (0, 0)), pl.BlockSpec((d, tn), lambda j: (0, j)),
                            pl.BlockSpec((1, tn), lambda j: (0, j))],
                  out_specs=pl.BlockSpec((bsz, tn), lambda j: (0, j)),
                  out_shape=jax.ShapeDtypeStruct((bsz, ncol), F32))(c_all, w_shard, b_shard)


def ada_bwd(c_all, d_mod_all, d_mod_cols, *, name):
    bsz, d = c_all.shape
    ncol = d_mod_cols.shape[1]
    nall = d_mod_all.shape[1]

    def body(c_ref, da_ref, dc_ref, gw_ref, gb_ref):
        cv = c_ref[...]
        act = (cv * _sigmoid(cv)).astype(BF16)
        gw_ref[...] = _dot_tn(act, dc_ref[...].astype(BF16))
        gb_ref[...] = _colsum(da_ref[...])

    full = lambda s: pl.BlockSpec(s, lambda: (0,) * len(s))
    return _pcall(body, name=name,
                  in_specs=[full((bsz, d)), full((bsz, nall)), full((bsz, ncol))],
                  out_specs=[full((d, ncol)), full((1, nall))],
                  out_shape=[jax.ShapeDtypeStruct((d, ncol), F32), jax.ShapeDtypeStruct((1, nall), F32)],
                  )(c_all, d_mod_all, d_mod_cols)


WEIGHT_NAMES = ['w_ada', 'b_ada', 'pre_norm1', 'post_norm1', 'w_in', 'b_gate', 'lru_conv_w', 'lru_conv_b', 'lru_wa',
                'lru_ba', 'lru_wx', 'lru_bx', 'lru_lambda', 'w_pa', 'ssd_conv_w', 'ssd_conv_b', 'ssd_dt_bias',
                'ssd_a_log', 'ssd_d', 'ssd_norm_w', 'w_pb', 'w_out', 'pre_norm2', 'post_norm2', 'w_ff1', 'w_ff2']
BIG_NAMES = ['w_in', 'w_pa', 'w_pb', 'w_out', 'w_ff1', 'w_ff2']
COLUMN_SHARDED = ('w_in', 'w_ff1')
SMALL_NAMES = [n for n in WEIGHT_NAMES if n not in BIG_NAMES + ['w_ada', 'b_ada']]
SHARDED_SMALL = ('lru_conv_w', 'ssd_conv_w')
PACK_WIDTH = 1024


def _whole(name, gathered):
    if name in COLUMN_SHARDED:
        return jnp.transpose(gathered, (1, 0, 2)).reshape(gathered.shape[1], N_CHIPS * gathered.shape[2])
    return gathered.reshape(N_CHIPS * gathered.shape[1], gathered.shape[2])


def _by_chip(name, g):
    if name in COLUMN_SHARDED:
        return jnp.transpose(g.reshape(g.shape[0], N_CHIPS, g.shape[1] // N_CHIPS), (1, 0, 2))
    return g.reshape(N_CHIPS, g.shape[0] // N_CHIPS, g.shape[1])


class ChipExchange:
    def __init__(self, shards, core):
        self.shards, self.core = shards, core
        self.pending, self.halves = [], {}

    def weights_bg(self):
        return gather_halves_background(list(self.shards.values()))

    def weights(self, arrived):
        swapped = fill_other_half(arrived, name="weights_from_sibling")
        return {n: _whole(n, g) for n, g in zip(self.shards, swapped)}

    def grads_bg(self, grads):
        self.pending = list(grads)
        by_chip = [_by_chip(n, g) for n, g in grads.items()]
        theirs = send_half_to_sibling(by_chip, name="grads_to_sibling_" + self.pending[0])
        sums = [add_half_to_bf16(self.core, a, b, name="add_cores_" + n)
                for n, a, b in zip(self.pending, by_chip, theirs)]
        return scatter_background(sums)

    def grads_done(self, landed):
        for n, p in zip(self.pending, landed):
            self.halves[n] = sum_blocks(p, name="add_chips_" + n)

    def reduced(self):
        names = list(self.halves)
        return dict(zip(names, join_with_sibling([self.halves[n] for n in names], name="grads_join")))


def _pack(parts):
    flat = jnp.concatenate([p.reshape(-1).astype(F32) for p in parts])
    rows = -(-flat.shape[0] // (PACK_WIDTH * SUBLANES)) * SUBLANES
    return jnp.pad(flat, (0, rows * PACK_WIDTH - flat.shape[0])).reshape(rows, PACK_WIDTH)


def _unpack(packed, shapes):
    flat = packed.reshape(-1)
    out, pos = [], 0
    for s in shapes:
        size = int(np.prod(s))
        out.append(flat[pos:pos + size].reshape(s))
        pos += size
    return out


def kernel(x, c, w_ada, b_ada, pre_norm1, post_norm1, w_in, b_gate, lru_conv_w, lru_conv_b, lru_wa, lru_ba, lru_wx, lru_bx, lru_lambda, w_pa, ssd_conv_w, ssd_conv_b, ssd_dt_bias, ssd_a_log, ssd_d, ssd_norm_w, w_pb, w_out, pre_norm2, post_norm2, w_ff1, w_ff2, loss_target, m_w_ada, m_b_ada, m_pre_norm1, m_post_norm1, m_w_in, m_b_gate, m_lru_conv_w, m_lru_conv_b, m_lru_wa, m_lru_ba, m_lru_wx, m_lru_bx, m_lru_lambda, m_w_pa, m_ssd_conv_w, m_ssd_conv_b, m_ssd_dt_bias, m_ssd_a_log, m_ssd_d, m_ssd_norm_w, m_w_pb, m_w_out, m_pre_norm2, m_post_norm2, m_w_ff1, m_w_ff2, v_w_ada, v_b_ada, v_pre_norm1, v_post_norm1, v_w_in, v_b_gate, v_lru_conv_w, v_lru_conv_b, v_lru_wa, v_lru_ba, v_lru_wx, v_lru_bx, v_lru_lambda, v_w_pa, v_ssd_conv_w, v_ssd_conv_b, v_ssd_dt_bias, v_ssd_a_log, v_ssd_d, v_ssd_norm_w, v_w_pb, v_w_out, v_pre_norm2, v_post_norm2, v_w_ff1, v_w_ff2):
    given = dict(locals())
    bsz, seq, d = x.shape
    my_x, my_y, my_c = lax.axis_index("x"), lax.axis_index("y"), lax.axis_index("c")
    chip = 2 * my_x + my_y
    dev = 2 * chip + my_c
    strip = lambda a: a if a.ndim == 2 else a[0]
    w = {n: strip(given[n]) for n in WEIGHT_NAMES}
    m = {n: strip(given["m_" + n]) for n in WEIGHT_NAMES}
    v = {n: strip(given["v_" + n]) for n in WEIGHT_NAMES}

    first_shapes = [c.shape] + [w[n].shape for n in SHARDED_SMALL]
    first = allgather8(_pack([c] + [w[n] for n in SHARDED_SMALL]), name="gather_c_conv")
    first = first.reshape(N_DEV, -1, PACK_WIDTH)
    per_dev = [_unpack(first[k], first_shapes) for k in range(N_DEV)]
    c_all = jnp.concatenate([p[0] for p in per_dev], axis=0)
    conv_full = {n: jnp.concatenate([per_dev[2 * k][1 + i] for k in range(N_CHIPS)], axis=1)
                 for i, n in enumerate(SHARDED_SMALL)}

    ncol = w["w_ada"].shape[1]
    b_cols = lax.dynamic_slice(b_ada, (0, chip * ncol), (1, ncol))
    mod_cols = ada_fwd(c_all, w["w_ada"], b_cols, name="ada_fwd")
    mod_all = allgather8(mod_cols, name="gather_mod").reshape(N_CHIPS, 2, N_DEV * bsz, ncol)[:, 0]
    mod_all = jnp.transpose(mod_all, (1, 0, 2)).reshape(N_DEV * bsz, N_CHIPS * ncol)
    mod = lax.dynamic_slice(mod_all, (dev * bsz, 0), (bsz, 6 * d)).reshape(bsz, 6, d)
    mod = jnp.pad(mod, ((0, 0), (0, 2), (0, 0)))

    w_in_full = _whole("w_in", gather_weights([w["w_in"].astype(BF16)], name="gather_w_in")[0])
    big = {"w_main": w_in_full[:, :8192],
           "w_dt": jnp.pad(w_in_full[:, 8192:8192 + SSD_HEADS], ((0, 0), (0, LANES - SSD_HEADS))),
           "w_gates": w_in_full[:, 8192 + SSD_HEADS:]}
    small = {n: w[n] for n in SMALL_NAMES}
    small.update(conv_full)
    plan = ChipExchange({n: w[n].astype(BF16) for n in BIG_NAMES if n != "w_in"}, my_c.astype(jnp.int32).reshape(1))

    loss_cols, grad_x, d_mod, small_grads = local_step(x, loss_target, mod, big, small, plan)

    packed = _pack([d_mod, loss_cols] + [small_grads[n] for n in SMALL_NAMES])
    rows = packed.shape[0]
    everyone = allgather8(packed, name="gather_small").reshape(N_DEV, rows, PACK_WIDTH)
    d_mod_all = everyone[:, :bsz * 6].reshape(N_DEV * bsz, 6 * d)
    summed = sum_blocks(everyone, name="sum_small")
    shapes = [d_mod.shape, loss_cols.shape] + [small_grads[n].shape for n in SMALL_NAMES]
    parts = _unpack(summed, shapes)
    loss = jnp.sum(parts[1])
    grads = dict(zip(SMALL_NAMES, parts[2:]))
    for n in SHARDED_SMALL:
        cols = w[n].shape[1]
        grads[n] = lax.dynamic_slice(grads[n], (0, chip * cols), (grads[n].shape[0], cols))
    d_mod_cols = lax.dynamic_slice(d_mod_all, (0, chip * ncol), (N_DEV * bsz, ncol))
    grads["w_ada"], grads["b_ada"] = ada_bwd(c_all, d_mod_all, d_mod_cols, name="ada_bwd")

    grads.update(plan.reduced())

    delta, new_m, new_v = {}, {}, {}
    for n in BIG_NAMES + ["w_ada", "b_ada"]:
        delta[n], new_m[n], new_v[n] = adamw(w[n], grads[n], m[n], v[n], name="adamw_" + n)
    shapes = [w[n].shape for n in SMALL_NAMES]
    pk = lambda src: _pack([src[n] for n in SMALL_NAMES])
    upd = adamw(pk(w), pk(grads), pk(m), pk(v), name="adamw_small")
    for out, packed_out in zip((delta, new_m, new_v), upd):
        out.update(zip(SMALL_NAMES, _unpack(packed_out, shapes)))

    shaped = lambda src: [src[n].reshape(given[n].shape) for n in WEIGHT_NAMES]
    return (loss, grad_x, *shaped(grads), *shaped(delta), *shaped(new_m), *shaped(new_v))
```

```python
import functools
import math

import numpy as np
import jax
import jax.numpy as jnp
from jax import lax
from jax.experimental import pallas as pl
from jax.experimental.pallas import tpu as pltpu

F32 = jnp.float32
BF16 = jnp.bfloat16
HI = lax.Precision.HIGHEST
MESH = pl.DeviceIdType.MESH

D_MODEL = 1024
LRU_HEADS = 16
LRU_HEAD_DIM = 64
LRU_C = 8.0
SSD_INNER = 2048
SSD_HEADS = 32
SSD_HEAD_DIM = 64
SSD_GROUPS = 8
SSD_STATE = 128
SSD_CHUNK = 128
SSD_CONV_DIM = 4096
D_FF = 4096
EPS = 1e-6
N_CHIPS = 4
N_DEV = 8
LANES = 128
SUBLANES = 8

ADAM_LR = 0.001
ADAM_B1 = 0.9
ADAM_B2 = 0.999
ADAM_EPS = 1e-08
ADAM_WD = 0.01
ADAM_STEP = 10


ANY = pl.BlockSpec(memory_space=pl.ANY)


def _pcall(body, **kw):
    return pl.pallas_call(body, **kw)


class Background:
    def __init__(self, inputs, out_shapes, scratch, start, finish):
        self.inputs, self.out_shapes, self.scratch = list(inputs), list(out_shapes), list(scratch)
        self.start, self.finish = start, finish

    def wrap(self, body, kw):
        n_in, n_out = len(kw["in_specs"]), len(kw["out_specs"])
        n_scr = len(kw.get("scratch_shapes", []))
        b_in, b_out = len(self.inputs), len(self.out_shapes)
        grid = kw["grid"]

        def wrapped(*refs):
            ins, b_ins = refs[:n_in], refs[n_in:n_in + b_in]
            o0 = n_in + b_in
            outs, b_outs = refs[o0:o0 + n_out], refs[o0 + n_out:o0 + n_out + b_out]
            s0 = o0 + n_out + b_out
            scr, b_scr = refs[s0:s0 + n_scr], refs[s0 + n_scr:]
            ids = [pl.program_id(a) for a in range(len(grid))]
            first = functools.reduce(jnp.logical_and, [i == 0 for i in ids])
            last = functools.reduce(jnp.logical_and, [i == g - 1 for i, g in zip(ids, grid)])

            @pl.when(first)
            def _():
                self.start(b_ins, b_outs, b_scr)

            body(*ins, *outs, *scr)

            @pl.when(last)
            def _():
                self.finish(b_ins, b_outs, b_scr)

        kw = dict(kw, in_specs=list(kw["in_specs"]) + [ANY] * b_in, out_specs=list(kw["out_specs"]) + [ANY] * b_out,
                  out_shape=list(kw["out_shape"]) + self.out_shapes,
                  scratch_shapes=list(kw.get("scratch_shapes", [])) + self.scratch)
        return wrapped, kw


def _run(body, args, bg, **kw):
    n_out = len(kw["out_shape"])
    if bg is None:
        return list(_pcall(body, **kw)(*args)), []
    body, kw = bg.wrap(body, kw)
    outs = _pcall(body, **kw)(*args, *bg.inputs)
    return list(outs[:n_out]), list(outs[n_out:])


def _sigmoid(v):
    return 1.0 / (1.0 + jnp.exp(-v))


def _log1p(u):
    return jnp.where(u < 1e-3, u * (1.0 - u * (0.5 - u * (1.0 / 3.0))), jnp.log(1.0 + u))


def _softplus(v):
    return jnp.maximum(v, 0.0) + _log1p(jnp.exp(-jnp.abs(v)))


def _neg_expm1(v):
    small = -v * (1.0 + v * (0.5 + v * (1.0 / 6.0 + v * (1.0 / 24.0))))
    return jnp.where(v > -0.05, small, 1.0 - jnp.exp(v))


_GELU_K = math.sqrt(2.0 / math.pi)


def _gelu(v):
    t = jnp.tanh(_GELU_K * (v + 0.044715 * v * v * v))
    return 0.5 * v * (1.0 + t)


def _gelu_grad(v):
    t = jnp.tanh(_GELU_K * (v + 0.044715 * v * v * v))
    return 0.5 * (1.0 + t) + 0.5 * v * (1.0 - t * t) * _GELU_K * (1.0 + 3.0 * 0.044715 * v * v)


def _colsum(v):
    return jnp.sum(v, axis=0, keepdims=True)


def _dot(a, b, precision=None):
    return lax.dot_general(a, b, (((1,), (0,)), ((), ())), preferred_element_type=F32, precision=precision)


def _dot_nt(a, b):
    return lax.dot_general(a, b, (((1,), (1,)), ((), ())), preferred_element_type=F32)


def _dot_tn(a, b):
    return lax.dot_general(a, b, (((0,), (0,)), ((), ())), preferred_element_type=F32)


def _shift_down(xt, prev8, j):
    if j == 0:
        return xt
    n = xt.shape[0]
    r = pltpu.roll(xt, j, 0)
    p = pltpu.roll(prev8, j, 0)
    rows = lax.broadcasted_iota(jnp.int32, (SUBLANES, xt.shape[1]), 0)
    top = jnp.where(rows < j, p, r[0:SUBLANES])
    if n == SUBLANES:
        return top
    return jnp.concatenate([top, r[SUBLANES:]], axis=0)


def _shift_up(xt, next8, j):
    if j == 0:
        return xt
    n = xt.shape[0]
    r = pltpu.roll(xt, n - j, 0)
    p = pltpu.roll(next8, SUBLANES - j, 0)
    rows = lax.broadcasted_iota(jnp.int32, (SUBLANES, xt.shape[1]), 0)
    bot = jnp.where(rows >= SUBLANES - j, p, r[n - SUBLANES:])
    if n == SUBLANES:
        return bot
    return jnp.concatenate([r[:n - SUBLANES], bot], axis=0)


def _conv4(xt, prev8, w, b):
    out = b + w[3:4] * xt
    for k in range(3):
        out = out + w[k:k + 1] * _shift_down(xt, prev8, 3 - k)
    return out


def _conv4_bwd(d_out, next8, xt, w):
    d_x = w[3:4] * d_out
    d_w = []
    for k in range(3):
        up = _shift_up(d_out, next8, 3 - k)
        d_x = d_x + w[k:k + 1] * up
        d_w.append(_colsum(up * xt))
    d_w.append(_colsum(d_out * xt))
    return d_x, d_w, _colsum(d_out)


def _stack_rows(rows, width):
    rows = list(rows) + [jnp.zeros((1, width), F32)] * (SUBLANES - len(rows))
    return jnp.concatenate(rows, axis=0)


def _pick(n, cands):
    for c in cands:
        if n % c == 0:
            return c
    raise ValueError(f"no tile for {n}")


MM_ROWS = 512
MM_PANEL_COLS = 2048
MM_SUB = 512


def mm_nn(pairs, *, name, out_dtype=F32, a_fn=None, add=None, epi=None, extra=None, bg=None):
    np_ = len(pairs)
    m, n = pairs[0][0].shape[0], pairs[0][1].shape[1]
    tm = _pick(m, (MM_ROWS, 256, 128, 64, 32, 16, 8))
    pn = n if n <= MM_PANEL_COLS else _pick(n, (MM_PANEL_COLS, 1024, 512, 256, 128))
    ns = _pick(pn, (MM_SUB, 256, 128))
    adds = list(add or ())
    has_extra = extra is not None
    stage0 = a_fn is not None or pairs[0][0].dtype != BF16

    def body(*refs):
        a_refs, b_refs = refs[:np_], refs[np_:2 * np_]
        pos = 2 * np_
        extra_ref = None
        add_refs = refs[pos:pos + len(adds)]
        pos += len(adds)
        if has_extra:
            extra_ref = refs[pos]
            pos += 1
        o_ref = refs[pos]
        lhs = list(a_refs)
        if stage0:
            av = a_refs[0][...]
            if a_fn is not None:
                av = a_fn(av)
            refs[pos + 1][...] = av.astype(BF16)
            lhs[0] = refs[pos + 1]
        for n0 in range(0, pn, ns):
            sl = slice(n0, n0 + ns)
            acc = None
            for a_ref, b_ref in zip(lhs, b_refs):
                part = _dot(a_ref[...].astype(BF16), b_ref[:, sl])
                acc = part if acc is None else acc + part
            for add_ref in add_refs:
                acc = acc + add_ref[:, sl]
            if epi is not None:
                acc = epi(acc, extra_ref[:, sl]) if has_extra else epi(acc)
            o_ref[:, sl] = acc.astype(out_dtype)

    in_specs = [pl.BlockSpec((tm, a.shape[1]), lambda j, i: (i, 0)) for a, _ in pairs]
    in_specs += [pl.BlockSpec((b.shape[0], pn), lambda j, i: (0, j)) for _, b in pairs]
    args = [a for a, _ in pairs] + [b for _, b in pairs]
    tile = pl.BlockSpec((tm, pn), lambda j, i: (i, j))
    for extra_add in adds:
        in_specs.append(tile)
        args.append(extra_add)
    if has_extra:
        in_specs.append(tile)
        args.append(extra)
    outs, bg_outs = _run(
        body, args, bg, name=name, grid=(n // pn, m // tm), in_specs=in_specs, out_specs=[tile],
        out_shape=[jax.ShapeDtypeStruct((m, n), out_dtype)],
        scratch_shapes=[pltpu.VMEM((tm, pairs[0][0].shape[1]), BF16)] if stage0 else [])
    return outs[0] if bg is None else (outs[0], bg_outs)


MM_REDUCE_ROWS = 1024
MM_GRAD_ROWS = 1024
MM_GRAD_COLS = 2048


def mm_tn(a, b, *, name, a_fn=None):
    m, ka = a.shape
    nb = b.shape[1]
    pa = _pick(ka, (MM_GRAD_ROWS, 512, 256, 128))
    pb = nb if nb <= MM_GRAD_COLS else _pick(nb, (MM_GRAD_COLS, 1024, 512, 256, 128))
    ns = _pick(pb, (MM_SUB, 256, 128))
    tmk = _pick(m, (MM_REDUCE_ROWS, 512, 256, 128, 64, 32, 16))

    def body(a_ref, b_ref, o_ref, lhs):
        k = pl.program_id(2)

        @pl.when(k == 0)
        def _():
            o_ref[...] = jnp.zeros_like(o_ref)

        av = a_ref[...]
        if a_fn is not None:
            av = a_fn(av)
        lhs[...] = av.astype(BF16)
        for n0 in range(0, pb, ns):
            o_ref[:, n0:n0 + ns] += _dot_tn(lhs[...], b_ref[:, n0:n0 + ns].astype(BF16))

    return _pcall(
        body, name=name,
        grid=(ka // pa, nb // pb, m // tmk),
        in_specs=[pl.BlockSpec((tmk, pa), lambda i, j, k: (k, i)),
                  pl.BlockSpec((tmk, pb), lambda i, j, k: (k, j))],
        out_specs=pl.BlockSpec((pa, pb), lambda i, j, k: (i, j)),
        out_shape=jax.ShapeDtypeStruct((ka, nb), F32),
        scratch_shapes=[pltpu.VMEM((tmk, pa), BF16)],
    )(a, b)


def _relu_sq(v):
    r = jnp.maximum(v, 0.0)
    return r * r


ROW_TILE = 512


def _row_specs(bsz, seq, width, ts):
    return pl.BlockSpec((None, ts, width), lambda b, i: (b, i, 0))


def _vec_spec(width):
    return pl.BlockSpec((1, width), lambda b, i: (0, 0))


def _mod_spec():
    return pl.BlockSpec((None, SUBLANES, D_MODEL), lambda b, i: (b, 0, 0))


def _rstd(v):
    return lax.rsqrt(jnp.mean(v * v, axis=-1, keepdims=True) + EPS)


def prenorm(x, w, mod, *, name):
    bsz, seq, d = x.shape
    ts = _pick(seq, (ROW_TILE, 256, 128))

    def body(x_ref, w_ref, mod_ref, h_ref):
        xv = x_ref[...]
        m = mod_ref[...]
        xh = xv * _rstd(xv)
        h_ref[...] = ((xh * w_ref[...]) * (1.0 + m[1:2]) + m[0:1]).astype(BF16)

    return _pcall(
        body, name=name, grid=(bsz, seq // ts),
        in_specs=[_row_specs(bsz, seq, d, ts), _vec_spec(d), _mod_spec()],
        out_specs=_row_specs(bsz, seq, d, ts),
        out_shape=jax.ShapeDtypeStruct((bsz, seq, d), BF16),
    )(x, w, mod)


def post1_pre2(x, out1, mod, post1, pre2, *, name):
    bsz, seq, d = x.shape
    ts = _pick(seq, (ROW_TILE, 256, 128))

    def body(x_ref, o_ref, mod_ref, p1_ref, p2_ref, x1_ref, h2_ref):
        m = mod_ref[...]
        ov = o_ref[...]
        x1 = x_ref[...] + m[2:3] * ((ov * _rstd(ov)) * p1_ref[...])
        x1_ref[...] = x1
        xh = x1 * _rstd(x1)
        h2_ref[...] = ((xh * p2_ref[...]) * (1.0 + m[4:5]) + m[3:4]).astype(BF16)

    return _pcall(
        body, name=name, grid=(bsz, seq // ts),
        in_specs=[_row_specs(bsz, seq, d, ts), _row_specs(bsz, seq, d, ts), _mod_spec(), _vec_spec(d), _vec_spec(d)],
        out_specs=[_row_specs(bsz, seq, d, ts), _row_specs(bsz, seq, d, ts)],
        out_shape=[jax.ShapeDtypeStruct((bsz, seq, d), F32), jax.ShapeDtypeStruct((bsz, seq, d), BF16)],
    )(x, out1, mod, post1, pre2)


def _acc_specs(d):
    per_batch = pl.BlockSpec((None, SUBLANES, d), lambda b, i: (b, 0, 0))
    glob = pl.BlockSpec((SUBLANES, d), lambda b, i: (0, 0))
    return per_batch, glob


def _accumulate(pb_ref, gl_ref, pb_rows, gl_rows, width):
    b, i = pl.program_id(0), pl.program_id(1)

    @pl.when(i == 0)
    def _():
        pb_ref[...] = jnp.zeros_like(pb_ref)

    @pl.when((b == 0) & (i == 0))
    def _():
        gl_ref[...] = jnp.zeros_like(gl_ref)

    pb_ref[...] += _stack_rows(pb_rows, width)
    gl_ref[...] += _stack_rows(gl_rows, width)


def _rms_bwd(d_n, n, r):
    return r * (d_n - n * jnp.mean(d_n * n, axis=-1, keepdims=True))


def final_bwd(x1, y2, target, mod, post2, *, name):
    bsz, seq, d = x1.shape
    ts = _pick(seq, (ROW_TILE, 256, 128))

    def body(x1_ref, y_ref, t_ref, mod_ref, p_ref, dx_ref, dy_ref, pb_ref, gl_ref):
        m = mod_ref[...]
        g2 = m[5:6]
        yv = y_ref[...]
        r = _rstd(yv)
        n = yv * r
        o = n * p_ref[...]
        diff = (x1_ref[...] + g2 * o) - t_ref[...]
        dx = diff * (1.0 / d)
        dx_ref[...] = dx
        d_o = dx * g2
        dy_ref[...] = _rms_bwd(d_o * p_ref[...], n, r).astype(BF16)
        _accumulate(pb_ref, gl_ref, [_colsum(dx * o)], [_colsum(d_o * n), _colsum(diff * diff) * (0.5 / d)], d)

    pb, gl = _acc_specs(d)
    rs = _row_specs(bsz, seq, d, ts)
    return _pcall(
        body, name=name, grid=(bsz, seq // ts),
        in_specs=[rs, rs, rs, _mod_spec(), _vec_spec(d)],
        out_specs=[rs, rs, pb, gl],
        out_shape=[jax.ShapeDtypeStruct((bsz, seq, d), F32), jax.ShapeDtypeStruct((bsz, seq, d), BF16),
                   jax.ShapeDtypeStruct((bsz, SUBLANES, d), F32), jax.ShapeDtypeStruct((SUBLANES, d), F32)],
    )(x1, y2, target, mod, post2)


def mid_bwd(d_h2, dx2, x1, out1, mod, pre2, post1, *, name):
    bsz, seq, d = x1.shape
    ts = _pick(seq, (ROW_TILE, 256, 128))

    def body(dh_ref, dx2_ref, x1_ref, o_ref, mod_ref, p2_ref, p1_ref, dx1_ref, do_ref, pb_ref, gl_ref):
        m = mod_ref[...]
        dh = dh_ref[...]
        x1 = x1_ref[...]
        r2 = _rstd(x1)
        xh = x1 * r2
        xw = xh * p2_ref[...]
        d_xw = dh * (1.0 + m[4:5])
        dx1 = dx2_ref[...] + _rms_bwd(d_xw * p2_ref[...], xh, r2)
        dx1_ref[...] = dx1
        ov = o_ref[...]
        r1 = _rstd(ov)
        n1 = ov * r1
        o1 = n1 * p1_ref[...]
        d_o1 = dx1 * m[2:3]
        do_ref[...] = _rms_bwd(d_o1 * p1_ref[...], n1, r1).astype(BF16)
        _accumulate(pb_ref, gl_ref, [_colsum(dh), _colsum(dh * xw), _colsum(dx1 * o1)],
                    [_colsum(d_xw * xh), _colsum(d_o1 * n1)], d)

    pb, gl = _acc_specs(d)
    rs = _row_specs(bsz, seq, d, ts)
    return _pcall(
        body, name=name, grid=(bsz, seq // ts),
        in_specs=[rs, rs, rs, rs, _mod_spec(), _vec_spec(d), _vec_spec(d)],
        out_specs=[rs, rs, pb, gl],
        out_shape=[jax.ShapeDtypeStruct((bsz, seq, d), F32), jax.ShapeDtypeStruct((bsz, seq, d), BF16),
                   jax.ShapeDtypeStruct((bsz, SUBLANES, d), F32), jax.ShapeDtypeStruct((SUBLANES, d), F32)],
    )(d_h2, dx2, x1, out1, mod, pre2, post1)


def first_bwd(d_h1, dx1, x, mod, pre1, *, name):
    bsz, seq, d = x.shape
    ts = _pick(seq, (ROW_TILE, 256, 128))

    def body(dh_ref, dx1_ref, x_ref, mod_ref, p_ref, gx_ref, pb_ref, gl_ref):
        m = mod_ref[...]
        dh = dh_ref[...]
        xv = x_ref[...]
        r = _rstd(xv)
        xh = xv * r
        xw = xh * p_ref[...]
        d_xw = dh * (1.0 + m[1:2])
        gx_ref[...] = dx1_ref[...] + _rms_bwd(d_xw * p_ref[...], xh, r)
        _accumulate(pb_ref, gl_ref, [_colsum(dh), _colsum(dh * xw)], [_colsum(d_xw * xh)], d)

    pb, gl = _acc_specs(d)
    rs = _row_specs(bsz, seq, d, ts)
    return _pcall(
        body, name=name, grid=(bsz, seq // ts),
        in_specs=[rs, rs, rs, _mod_spec(), _vec_spec(d)],
        out_specs=[rs, pb, gl],
        out_shape=[jax.ShapeDtypeStruct((bsz, seq, d), F32),
                   jax.ShapeDtypeStruct((bsz, SUBLANES, d), F32), jax.ShapeDtypeStruct((SUBLANES, d), F32)],
    )(d_h1, dx1, x, mod, pre1)


def merge_fwd(ya, yb, gates, b_gate, *, name):
    bsz, seq, d = ya.shape
    ts = _pick(seq, (ROW_TILE, 256, 128))

    def body(ya_ref, yb_ref, g_ref, b_ref, o_ref):
        g = _sigmoid(g_ref[...] + b_ref[...])
        o_ref[...] = (g[:, :d] * ya_ref[...] + g[:, d:] * yb_ref[...]).astype(BF16)

    rs = _row_specs(bsz, seq, d, ts)
    return _pcall(
        body, name=name, grid=(bsz, seq // ts),
        in_specs=[rs, rs, _row_specs(bsz, seq, 2 * d, ts), _vec_spec(2 * d)],
        out_specs=rs,
        out_shape=jax.ShapeDtypeStruct((bsz, seq, d), BF16),
    )(ya, yb, gates, b_gate)


def merge_bwd(d_merged, ya, yb, gates, b_gate, *, name):
    bsz, seq, d = ya.shape
    ts = _pick(seq, (ROW_TILE, 256, 128))

    def body(dm_ref, ya_ref, yb_ref, g_ref, b_ref, dya_ref, dyb_ref, dg_ref, gl_ref):
        b, i = pl.program_id(0), pl.program_id(1)
        g = _sigmoid(g_ref[...] + b_ref[...])
        dm = dm_ref[...]
        ga, gb = g[:, :d], g[:, d:]
        dya_ref[...] = (dm * ga).astype(BF16)
        dyb_ref[...] = (dm * gb).astype(BF16)
        dg = jnp.concatenate([dm * ya_ref[...] * ga * (1.0 - ga), dm * yb_ref[...] * gb * (1.0 - gb)], axis=1)
        dg_ref[...] = dg.astype(BF16)

        @pl.when((b == 0) & (i == 0))
        def _():
            gl_ref[...] = jnp.zeros_like(gl_ref)

        gl_ref[...] += _stack_rows([_colsum(dg)], 2 * d)

    rs = _row_specs(bsz, seq, d, ts)
    rs2 = _row_specs(bsz, seq, 2 * d, ts)
    return _pcall(
        body, name=name, grid=(bsz, seq // ts),
        in_specs=[rs, rs, rs, rs2, _vec_spec(2 * d)],
        out_specs=[rs, rs, rs2, pl.BlockSpec((SUBLANES, 2 * d), lambda b, i: (0, 0))],
        out_shape=[jax.ShapeDtypeStruct((bsz, seq, d), BF16), jax.ShapeDtypeStruct((bsz, seq, d), BF16),
                   jax.ShapeDtypeStruct((bsz, seq, 2 * d), BF16), jax.ShapeDtypeStruct((SUBLANES, 2 * d), F32)],
    )(d_merged, ya, yb, gates, b_gate)


LRU_TILE = 256
N_LRU_BLOCKS = D_MODEL // LANES


def _block_mm(v, w_ref, transpose=False):
    vb = v.astype(BF16)
    outs = []
    for j in range(N_LRU_BLOCKS):
        blk = vb[:, LANES * j:LANES * (j + 1)]
        outs.append(_dot_nt(blk, w_ref[j]) if transpose else _dot(blk, w_ref[j]))
    return jnp.concatenate(outs, axis=1)


def _lru_gates(xc, wa_ref, ba, wx_ref, bx, sp):
    r = _sigmoid(_block_mm(xc, wa_ref) + ba)
    i = _sigmoid(_block_mm(xc, wx_ref) + bx)
    la = (-LRU_C * r) * sp
    a = jnp.exp(la)
    sq = jnp.sqrt(_neg_expm1(2.0 * la))
    return r, i, a, sq


def _prev8_spec(width, col_block, tile_rows):
    per = tile_rows // SUBLANES
    return pl.BlockSpec((None, SUBLANES, width), lambda b, i: (b, jnp.maximum(i * per - 1, 0), col_block))


def lru_fwd(pm, cw, cb, wa, ba, wx, bx, lam, w_pa, *, name):
    bsz, seq, _ = pm.shape
    d = D_MODEL
    ts = _pick(seq, (LRU_TILE, 128))

    def body(lx_ref, lxp_ref, lg_ref, cw_ref, cb_ref, wa_ref, ba_ref, wx_ref, bx_ref, lam_ref, wpa_ref,
             h_ref, pa_ref, ya_ref, kept_ref, hc, a_s, u_s):
        i = pl.program_id(1)

        @pl.when(i == 0)
        def _():
            hc[...] = jnp.zeros_like(hc)

        lx = lx_ref[...]
        prev8 = jnp.where(i == 0, 0.0, lxp_ref[...])
        xc = _conv4(lx, prev8, cw_ref[...], cb_ref[...])
        sp = _softplus(-lam_ref[...])
        r, ig, a, sq = _lru_gates(xc, wa_ref, ba_ref[...], wx_ref, bx_ref[...], sp)
        for k, kept in enumerate((xc, r, ig, a, sq)):
            kept_ref[:, k * d:(k + 1) * d] = kept
        a_s[...] = a
        u_s[...] = sq * (ig * xc)

        def step(g, h):
            r0 = pl.multiple_of(g * SUBLANES, SUBLANES)
            a8 = a_s[pl.ds(r0, SUBLANES), :]
            u8 = u_s[pl.ds(r0, SUBLANES), :]
            rows = []
            for j in range(SUBLANES):
                h = a8[j:j + 1] * h + u8[j:j + 1]
                rows.append(h)
            h_ref[pl.ds(r0, SUBLANES), :] = jnp.concatenate(rows, axis=0)
            return h

        hc[...] = lax.fori_loop(0, ts // SUBLANES, step, hc[...])
        pa_ref[...] = (h_ref[...] * _gelu(lg_ref[...])).astype(BF16)
        ya_ref[...] = _dot(pa_ref[...], wpa_ref[...])

    vec = _vec_spec(d)
    wspec = pl.BlockSpec((N_LRU_BLOCKS, LANES, LANES), lambda b, i: (0, 0, 0))
    rs = _row_specs(bsz, seq, d, ts)
    return _pcall(
        body, name=name, grid=(bsz, seq // ts),
        in_specs=[pl.BlockSpec((None, ts, d), lambda b, i: (b, i, 0)), _prev8_spec(d, 0, ts),
                  pl.BlockSpec((None, ts, d), lambda b, i: (b, i, 1)),
                  pl.BlockSpec((4, d), lambda b, i: (0, 0)), vec, wspec, vec, wspec, vec, vec,
                  pl.BlockSpec(w_pa.shape, lambda b, i: (0, 0))],
        out_specs=[rs, rs, rs, _row_specs(bsz, seq, 5 * d, ts)],
        out_shape=[jax.ShapeDtypeStruct((bsz, seq, d), F32), jax.ShapeDtypeStruct((bsz, seq, d), BF16),
                   jax.ShapeDtypeStruct((bsz, seq, d), F32), jax.ShapeDtypeStruct((bsz, seq, 5 * d), F32)],
        scratch_shapes=[pltpu.VMEM((1, d), F32), pltpu.VMEM((ts, d), F32), pltpu.VMEM((ts, d), F32)],
    )(pm, pm, pm, cw, cb, wa, ba, wx, bx, lam, w_pa)


def lru_bwd(pm, h, kept, d_ya, cw, wa, wx, lam, wt_pa, wt_lru, *, name, bg=None):
    bsz, seq, _ = pm.shape
    d = D_MODEL
    ts = _pick(seq, (LRU_TILE, 128))
    nt = seq // ts
    per = ts // SUBLANES

    def rev(i):
        return nt - 1 - i

    def body(lx_ref, lg_ref, h_ref, hp_ref, kept_ref, dya_ref, cw_ref, wa_ref, wx_ref,
             lam_ref, wtpa_ref, wtl_ref, dl_ref, dh1_ref, dwa_ref, dwx_ref, rows_ref,
             carry, dxc_next, a_s, dh_s, acc_s):
        b, i = pl.program_id(0), pl.program_id(1)
        t = rev(i)

        @pl.when(i == 0)
        def _():
            carry[...] = jnp.zeros_like(carry)
            dxc_next[...] = jnp.zeros_like(dxc_next)

        @pl.when((b == 0) & (i == 0))
        def _():
            dwa_ref[...] = jnp.zeros_like(dwa_ref)
            dwx_ref[...] = jnp.zeros_like(dwx_ref)
            rows_ref[...] = jnp.zeros_like(rows_ref)

        lx = lx_ref[...]
        lg = lg_ref[...]
        cwv = cw_ref[...]
        lam_v = lam_ref[...]
        sp = _softplus(-lam_v)
        xc, r, ig, a, sq = (kept_ref[:, k * d:(k + 1) * d] for k in range(5))
        hv = h_ref[...]
        d_pa = _dot(dya_ref[...], wtpa_ref[...])
        a_s[...] = a
        dh_s[...] = d_pa * _gelu(lg)

        def step(g, c):
            r0 = pl.multiple_of((per - 1 - g) * SUBLANES, SUBLANES)
            a8 = a_s[pl.ds(r0, SUBLANES), :]
            d8 = dh_s[pl.ds(r0, SUBLANES), :]
            rows = [None] * SUBLANES
            for j in range(SUBLANES - 1, -1, -1):
                acc = d8[j:j + 1] + c
                rows[j] = acc
                c = a8[j:j + 1] * acc
            acc_s[pl.ds(r0, SUBLANES), :] = jnp.concatenate(rows, axis=0)
            return c

        carry[...] = lax.fori_loop(0, per, step, carry[...])
        d_u = acc_s[...]
        hprev8 = jnp.where(t == 0, 0.0, hp_ref[...])
        d_a = d_u * _shift_down(hv, hprev8, 1)
        d_sq = d_u * (ig * xc)
        d_i = d_u * (sq * xc)
        d_xc = d_u * (sq * ig)
        d_la = d_a * a - d_sq * (a * a) / sq
        d_pre_r = (d_la * (-LRU_C * sp)) * (r * (1.0 - r))
        d_pre_i = d_i * (ig * (1.0 - ig))
        d_xc = d_xc + _block_mm(d_pre_r, wa_ref, transpose=True) + _block_mm(d_pre_i, wx_ref, transpose=True)
        xcb = xc.astype(BF16)
        drb = d_pre_r.astype(BF16)
        dib = d_pre_i.astype(BF16)
        for j in range(N_LRU_BLOCKS):
            sl = slice(LANES * j, LANES * (j + 1))
            dwa_ref[j] += _dot_tn(xcb[:, sl], drb[:, sl])
            dwx_ref[j] += _dot_tn(xcb[:, sl], dib[:, sl])
        d_lx, d_cw, d_cb = _conv4_bwd(d_xc, dxc_next[...], lx, cwv)
        dxc_next[...] = d_xc[0:SUBLANES]
        d_lam = _colsum(d_la * (-LRU_C * r)) * (-_sigmoid(-lam_v))
        rows_ref[...] += _stack_rows([_colsum(d_pre_r), _colsum(d_pre_i), d_lam, d_cb] + d_cw, d)
        dl_ref[:, :d] = d_lx.astype(BF16)
        dl_ref[:, d:] = (d_pa * hv * _gelu_grad(lg)).astype(BF16)
        dh1_ref[...] = _dot(dl_ref[...], wtl_ref[...])

    vec = _vec_spec(d)
    wspec = pl.BlockSpec((N_LRU_BLOCKS, LANES, LANES), lambda b, i: (0, 0, 0))
    tile = lambda col: pl.BlockSpec((None, ts, d), lambda b, i: (b, rev(i), col))
    prev8 = lambda col: pl.BlockSpec((None, SUBLANES, d), lambda b, i: (b, jnp.maximum(rev(i) * per - 1, 0), col))
    whole = lambda v: pl.BlockSpec(v.shape, lambda b, i: (0, 0))
    return _run(
        body, (pm, pm, h, h, kept, d_ya, cw, wa, wx, lam, wt_pa, wt_lru), bg, name=name, grid=(bsz, nt),
        in_specs=[tile(0), tile(1), tile(0), prev8(0), pl.BlockSpec((None, ts, 5 * d), lambda b, i: (b, rev(i), 0)),
                  tile(0), pl.BlockSpec((4, d), lambda b, i: (0, 0)), wspec, wspec, vec,
                  whole(wt_pa), whole(wt_lru)],
        out_specs=[pl.BlockSpec((None, ts, 2 * d), lambda b, i: (b, rev(i), 0)), tile(0), wspec, wspec,
                   pl.BlockSpec((SUBLANES, d), lambda b, i: (0, 0))],
        out_shape=[jax.ShapeDtypeStruct((bsz, seq, 2 * d), BF16), jax.ShapeDtypeStruct((bsz, seq, d), F32),
                   jax.ShapeDtypeStruct((N_LRU_BLOCKS, LANES, LANES), F32),
                   jax.ShapeDtypeStruct((N_LRU_BLOCKS, LANES, LANES), F32),
                   jax.ShapeDtypeStruct((SUBLANES, d), F32)],
        scratch_shapes=[pltpu.VMEM((1, d), F32), pltpu.VMEM((SUBLANES, d), F32),
                        pltpu.VMEM((ts, d), F32), pltpu.VMEM((ts, d), F32), pltpu.VMEM((ts, d), F32)])


L = SSD_CHUNK
N_PAIRS = SSD_HEADS // 2


def _ssd_common(conv, dt_raw, dtb, alog):
    sg = _sigmoid(conv)
    xa = conv * sg
    dtv = _softplus(dt_raw + dtb)
    a_neg = -jnp.exp(alog)
    rowi = lax.broadcasted_iota(jnp.int32, (L, L), 0)
    coli = lax.broadcasted_iota(jnp.int32, (L, L), 1)
    tril = (rowi >= coli).astype(F32)
    cs = _dot(tril, dtv * a_neg, precision=HI)
    return conv, sg, xa, dtv, a_neg, cs, rowi, coli


def _head_masks():
    lane = lax.broadcasted_iota(jnp.int32, (L, LANES), 1)
    return lane < SSD_HEAD_DIM


def _spread(v, p, first):
    return jnp.where(first[:v.shape[0]], v[:, 2 * p:2 * p + 1], v[:, 2 * p + 1:2 * p + 2])


def _place_head_sums(acc, z, p, first, lane1):
    rows = z.shape[0]
    s0 = jnp.sum(jnp.where(first[:rows], z, 0.0), axis=1, keepdims=True)
    s1 = jnp.sum(jnp.where(first[:rows], 0.0, z), axis=1, keepdims=True)
    lane = lane1[:rows]
    return acc + jnp.where(lane == 2 * p, s0, 0.0) + jnp.where(lane == 2 * p + 1, s1, 0.0)


def _stack_heads(v, first):
    return jnp.concatenate([jnp.where(first, v, 0.0), jnp.where(first, 0.0, v)], axis=0).astype(BF16)


def ssd_fwd(pm, dtr, cw, cb, dtb, alog, d_lanes, nw, w_pb, *, name, bg=None):
    bsz, seq, _ = pm.shape
    nc = seq // L
    inner, cdim = SSD_INNER, SSD_CONV_DIM

    def body(xbc_ref, xp_ref, z_ref, dt_ref, cw_ref, cb_ref, dtb_ref, alog_ref, dl_ref, nw_ref, wpb_ref,
             y_ref, yn_ref, st_ref, yb_ref, conv_ref, state):
        i = pl.program_id(1)

        @pl.when(i == 0)
        def _():
            state[...] = jnp.zeros_like(state)

        prev8 = jnp.where(i == 0, 0.0, xp_ref[...])
        conv = _conv4(xbc_ref[...], prev8, cw_ref[...], cb_ref[...])
        conv_ref[...] = conv
        _, _, xa, dtv, _, cs, rowi, coli = _ssd_common(conv, dt_ref[...], dtb_ref[...], alog_ref[...])
        cst = cs.T
        causal = rowi >= coli
        first = _head_masks()
        for g in range(SSD_GROUPS):
            bg = xa[:, inner + SSD_STATE * g:inner + SSD_STATE * (g + 1)].astype(BF16)
            cg = xa[:, inner + SSD_GROUPS * SSD_STATE + SSD_STATE * g:
                    inner + SSD_GROUPS * SSD_STATE + SSD_STATE * (g + 1)].astype(BF16)
            cbm = _dot_nt(cg, bg)
            for pp in range(2):
                p = 2 * g + pp
                sl = slice(LANES * p, LANES * (p + 1))
                ms = []
                for hh in (2 * p, 2 * p + 1):
                    seg = cs[:, hh:hh + 1] - cst[hh:hh + 1, :]
                    ms.append((cbm * jnp.exp(jnp.where(causal, seg, -jnp.inf))).astype(BF16))
                xsp = xa[:, sl]
                cs_p = _spread(cs, p, first)
                cs_last = cs_p[L - 1:L]
                xp = xsp * _spread(dtv, p, first)
                y_diag = _dot(jnp.concatenate(ms, axis=1), _stack_heads(xp, first))
                st = state[p]
                st_ref[p] = st
                y_off = _dot(cg, st.astype(BF16)) * jnp.exp(cs_p)
                y_ref[:, sl] = y_diag + y_off + dl_ref[:, sl] * xsp
                state[p] = st * jnp.exp(cs_last) + _dot_tn(bg, (xp * jnp.exp(cs_last - cs_p)).astype(BF16))
        zv = z_ref[...]
        yz = y_ref[...] * (zv * _sigmoid(zv))
        gw = inner // SSD_GROUPS
        for g in range(SSD_GROUPS):
            sl = slice(gw * g, gw * (g + 1))
            seg = yz[:, sl]
            yn_ref[:, sl] = ((seg * _rstd(seg)) * nw_ref[:, sl]).astype(BF16)
        yb_ref[...] = _dot(yn_ref[...], wpb_ref[...])

    cvec = lambda w: pl.BlockSpec((1, w), lambda b, i: (0, 0))
    outs, bg_outs = _run(
        body, (pm, pm, pm, dtr, cw, cb, dtb, alog, d_lanes, nw, w_pb), bg, name=name, grid=(bsz, nc),
        in_specs=[pl.BlockSpec((None, L, cdim), lambda b, i: (b, i, 1)), _prev8_spec(cdim, 1, L),
                  pl.BlockSpec((None, L, inner), lambda b, i: (b, i, 1)),
                  pl.BlockSpec((None, L, LANES), lambda b, i: (b, i, 0)),
                  pl.BlockSpec((4, cdim), lambda b, i: (0, 0)), cvec(cdim), cvec(LANES), cvec(LANES),
                  cvec(inner), cvec(inner), pl.BlockSpec(w_pb.shape, lambda b, i: (0, 0))],
        out_specs=[pl.BlockSpec((None, L, inner), lambda b, i: (b, i, 0)),
                   pl.BlockSpec((None, L, inner), lambda b, i: (b, i, 0)),
                   pl.BlockSpec((None, None, N_PAIRS, SSD_STATE, LANES), lambda b, i: (b, i, 0, 0, 0)),
                   pl.BlockSpec((None, L, D_MODEL), lambda b, i: (b, i, 0)),
                   pl.BlockSpec((None, L, cdim), lambda b, i: (b, i, 0))],
        out_shape=[jax.ShapeDtypeStruct((bsz, seq, inner), F32), jax.ShapeDtypeStruct((bsz, seq, inner), BF16),
                   jax.ShapeDtypeStruct((bsz, nc, N_PAIRS, SSD_STATE, LANES), F32),
                   jax.ShapeDtypeStruct((bsz, seq, D_MODEL), F32), jax.ShapeDtypeStruct((bsz, seq, cdim), F32)],
        scratch_shapes=[pltpu.VMEM((N_PAIRS, SSD_STATE, LANES), F32)])
    return outs, bg_outs


def ssd_bwd(pm, conv, dtr, y, states, d_yb, cw, dtb, alog, d_lanes, nw, wt_pb, wt_ssd, *, name):
    bsz, seq, _ = pm.shape
    nc = seq // L
    inner, cdim = SSD_INNER, SSD_CONV_DIM
    per = L // SUBLANES

    def rev(i):
        return nc - 1 - i

    def body(xbc_ref, conv_ref, z_ref, dt_ref, y_ref, st_ref, dyb_ref, cw_ref, dtb_ref, alog_ref,
             dl_ref, nw_ref, wtpb_ref, wts_ref, ds_ref, dh1_ref, ddt_ref, r4_ref, r2_ref, r1_ref,
             dstate, dconv_next, dxs_s, dbc_s):
        b, i = pl.program_id(0), pl.program_id(1)
        t = rev(i)

        @pl.when(i == 0)
        def _():
            dstate[...] = jnp.zeros_like(dstate)
            dconv_next[...] = jnp.zeros_like(dconv_next)

        @pl.when((b == 0) & (i == 0))
        def _():
            r4_ref[...] = jnp.zeros_like(r4_ref)
            r2_ref[...] = jnp.zeros_like(r2_ref)
            r1_ref[...] = jnp.zeros_like(r1_ref)

        xbc = xbc_ref[...]
        cwv = cw_ref[...]
        dt_in = dt_ref[...] + dtb_ref[...]
        conv = conv_ref[...]
        _, sg, xa, dtv, a_neg, cs, rowi, coli = _ssd_common(conv, dt_ref[...], dtb_ref[...], alog_ref[...])
        cst = cs.T
        causal = rowi >= coli
        anti = coli >= rowi
        first = _head_masks()
        lane1 = lax.broadcasted_iota(jnp.int32, (L, LANES), 1)

        yv = y_ref[...]
        zv = z_ref[...]
        sz = _sigmoid(zv)
        zs = zv * sz
        yz = yv * zs
        dyn = _dot(dyb_ref[...], wtpb_ref[...])
        gw = inner // SSD_GROUPS
        d_yz_parts, d_nw_parts = [], []
        for g in range(SSD_GROUPS):
            sl = slice(gw * g, gw * (g + 1))
            seg = yz[:, sl]
            r = _rstd(seg)
            n = seg * r
            d_nw_parts.append(_colsum(dyn[:, sl] * n))
            d_yz_parts.append(_rms_bwd(dyn[:, sl] * nw_ref[:, sl], n, r))
        d_yz = jnp.concatenate(d_yz_parts, axis=1)
        d_y = d_yz * zs
        ds_ref[:, :inner] = (d_yz * yv * (sz * (1.0 + zv * (1.0 - sz)))).astype(BF16)

        a1 = jnp.zeros((L, LANES), F32)
        a2 = jnp.zeros((L, LANES), F32)
        xs_dxt = jnp.zeros((L, LANES), F32)
        c0 = jnp.zeros((1, LANES), F32)
        d_dl = jnp.zeros((1, LANES), F32)
        for g in range(SSD_GROUPS):
            bsl = slice(inner + SSD_STATE * g, inner + SSD_STATE * (g + 1))
            csl = slice(inner + SSD_GROUPS * SSD_STATE + SSD_STATE * g,
                        inner + SSD_GROUPS * SSD_STATE + SSD_STATE * (g + 1))
            bg = xa[:, bsl].astype(BF16)
            cg = xa[:, csl].astype(BF16)
            cbm = _dot_nt(cg, bg)
            cbt = _dot_nt(bg, cg)
            d_cb = jnp.zeros((L, L), F32)
            d_bg = jnp.zeros((L, SSD_STATE), F32)
            d_cg = jnp.zeros((L, SSD_STATE), F32)
            for pp in range(2):
                p = 2 * g + pp
                sl = slice(LANES * p, LANES * (p + 1))
                xsp = xa[:, sl]
                dt_p = _spread(dtv, p, first)
                cs_p = _spread(cs, p, first)
                cs_last = cs_p[L - 1:L]
                e_p = jnp.exp(cs_p)
                w_p = jnp.exp(cs_last - cs_p)
                e_last = jnp.exp(cs_last)
                xp = xsp * dt_p
                xpb = xp.astype(BF16)
                dyp = d_y[:, sl]
                dypb = dyp.astype(BF16)
                dy_heads = (jnp.where(first, dyp, 0.0).astype(BF16), jnp.where(first, 0.0, dyp).astype(BF16))
                x_heads = (jnp.where(first, xp, 0.0).astype(BF16), jnp.where(first, 0.0, xp).astype(BF16))
                mts = []
                for k, hh in enumerate((2 * p, 2 * p + 1)):
                    col = cs[:, hh:hh + 1]
                    row = cst[hh:hh + 1, :]
                    dec = jnp.exp(jnp.where(causal, col - row, -jnp.inf))
                    dec_t = jnp.exp(jnp.where(anti, row - col, -jnp.inf))
                    gd = _dot_nt(dy_heads[k], xpb) * dec
                    d_cb = d_cb + gd
                    mt = cbt * dec_t
                    qd = gd * cbm - _dot_nt(x_heads[k], dypb) * mt
                    a1 = a1 + jnp.where(lane1 == hh, jnp.sum(qd, axis=1, keepdims=True), 0.0)
                    mts.append(mt.astype(BF16))
                dst = dstate[p]
                dstb = dst.astype(BF16)
                st = st_ref[p]
                stb = st.astype(BF16)
                dye = (dyp * e_p).astype(BF16)
                xw = (xp * w_p).astype(BF16)
                dx_off = w_p * _dot(bg, dstb)
                d_xp = _dot(jnp.concatenate(mts, axis=1), jnp.concatenate(dy_heads, axis=0)) + dx_off
                dxs_s[:, sl] = d_xp * dt_p + dyp * dl_ref[:, sl]
                a1 = _place_head_sums(a1, dyp * (_dot(cg, stb) * e_p), p, first, lane1)
                a2 = _place_head_sums(a2, xp * dx_off, p, first, lane1)
                xs_dxt = _place_head_sums(xs_dxt, d_xp * xsp, p, first, lane1)
                c0 = _place_head_sums(c0, _colsum(dst * st) * e_last, p, first, lane1)
                d_dl = _place_head_sums(d_dl, _colsum(dyp * xsp), p, first, lane1)
                d_cg = d_cg + _dot_nt(dye, stb)
                d_bg = d_bg + _dot_nt(xw, dstb)
                dstate[p] = dst * e_last + _dot_tn(cg, dye)
            d_cbb = d_cb.astype(BF16)
            dbc_s[:, SSD_STATE * g:SSD_STATE * (g + 1)] = d_bg + _dot_tn(d_cbb, cg)
            dbc_s[:, SSD_GROUPS * SSD_STATE + SSD_STATE * g:SSD_GROUPS * SSD_STATE + SSD_STATE * (g + 1)] = (
                d_cg + _dot(d_cbb, bg))

        d_da = (_dot(anti.astype(F32), a1, precision=HI) + _dot((rowi > coli).astype(F32), a2, precision=HI) + c0)
        d_dt = d_da * a_neg + xs_dxt
        d_alog = _colsum(d_da * dtv) * a_neg
        d_dtr = jnp.where(lane1 < SSD_HEADS, d_dt * _sigmoid(dt_in), 0.0)
        ddt_ref[...] = d_dtr.astype(BF16)
        d_xa = jnp.concatenate([dxs_s[...], dbc_s[...]], axis=1)
        d_conv = d_xa * (sg * (1.0 + conv * (1.0 - sg)))
        d_xbc, d_cw, d_cbias = _conv4_bwd(d_conv, dconv_next[...], xbc, cwv)
        dconv_next[...] = d_conv[0:SUBLANES]
        ds_ref[:, inner:] = d_xbc.astype(BF16)
        dh1_ref[...] = _dot(ds_ref[...], wts_ref[...])
        r4_ref[...] += _stack_rows([d_cbias] + d_cw, cdim)
        r2_ref[...] += _stack_rows([jnp.concatenate(d_nw_parts, axis=1)], inner)
        r1_ref[...] += _stack_rows([_colsum(d_dtr), d_alog, d_dl], LANES)

    cvec = lambda w: pl.BlockSpec((1, w), lambda b, i: (0, 0))
    return _pcall(
        body, name=name, grid=(bsz, nc),
        in_specs=[pl.BlockSpec((None, L, cdim), lambda b, i: (b, rev(i), 1)),
                  pl.BlockSpec((None, L, cdim), lambda b, i: (b, rev(i), 0)),
                  pl.BlockSpec((None, L, inner), lambda b, i: (b, rev(i), 1)),
                  pl.BlockSpec((None, L, LANES), lambda b, i: (b, rev(i), 0)),
                  pl.BlockSpec((None, L, inner), lambda b, i: (b, rev(i), 0)),
                  pl.BlockSpec((None, None, N_PAIRS, SSD_STATE, LANES), lambda b, i: (b, rev(i), 0, 0, 0)),
                  pl.BlockSpec((None, L, D_MODEL), lambda b, i: (b, rev(i), 0)),
                  pl.BlockSpec((4, cdim), lambda b, i: (0, 0)), cvec(LANES), cvec(LANES),
                  cvec(inner), cvec(inner), pl.BlockSpec(wt_pb.shape, lambda b, i: (0, 0)),
                  pl.BlockSpec(wt_ssd.shape, lambda b, i: (0, 0))],
        out_specs=[pl.BlockSpec((None, L, inner + cdim), lambda b, i: (b, rev(i), 0)),
                   pl.BlockSpec((None, L, D_MODEL), lambda b, i: (b, rev(i), 0)),
                   pl.BlockSpec((None, L, LANES), lambda b, i: (b, rev(i), 0)),
                   pl.BlockSpec((SUBLANES, cdim), lambda b, i: (0, 0)),
                   pl.BlockSpec((SUBLANES, inner), lambda b, i: (0, 0)),
                   pl.BlockSpec((SUBLANES, LANES), lambda b, i: (0, 0))],
        out_shape=[jax.ShapeDtypeStruct((bsz, seq, inner + cdim), BF16),
                   jax.ShapeDtypeStruct((bsz, seq, D_MODEL), F32),
                   jax.ShapeDtypeStruct((bsz, seq, LANES), BF16),
                   jax.ShapeDtypeStruct((SUBLANES, cdim), F32),
                   jax.ShapeDtypeStruct((SUBLANES, inner), F32),
                   jax.ShapeDtypeStruct((SUBLANES, LANES), F32)],
        scratch_shapes=[pltpu.VMEM((N_PAIRS, SSD_STATE, LANES), F32), pltpu.VMEM((SUBLANES, cdim), F32),
                        pltpu.VMEM((L, inner), F32), pltpu.VMEM((L, 2 * SSD_GROUPS * SSD_STATE), F32)],
    )(pm, conv, pm, dtr, y, states, d_yb, cw, dtb, alog, d_lanes, nw, wt_pb, wt_ssd)


def _lru_block_weights(w):
    w = w.reshape(N_LRU_BLOCKS, 2, LRU_HEAD_DIM, LRU_HEAD_DIM)
    z = jnp.zeros((N_LRU_BLOCKS, LRU_HEAD_DIM, LRU_HEAD_DIM), w.dtype)
    top = jnp.concatenate([w[:, 0], z], axis=2)
    bot = jnp.concatenate([z, w[:, 1]], axis=2)
    return jnp.concatenate([top, bot], axis=1).astype(BF16)


def _lru_block_grads(g):
    h = LRU_HEAD_DIM
    return jnp.stack([g[:, :h, :h], g[:, h:, h:]], axis=1).reshape(LRU_HEADS, h, h)


def _pad_lanes(v, width=LANES):
    return jnp.pad(v, ((0, 0), (0, width - v.shape[1])))


class NoExchange:
    def __init__(self, weights):
        self._weights, self.grads = weights, {}

    def weights_bg(self):
        return None

    def weights(self, bg_outs):
        return self._weights

    def grads_bg(self, grads):
        self.grads.update(grads)
        return None

    def grads_done(self, bg_outs):
        pass


def local_step(x, target, mod, big, small, plan):
    bsz, seq, d = x.shape
    t = bsz * seq
    flat = lambda v: v.reshape(t, v.shape[-1])
    unflat = lambda v: v.reshape(bsz, seq, v.shape[-1])

    wa_b = _lru_block_weights(small["lru_wa"])
    wx_b = _lru_block_weights(small["lru_wx"])
    dtb = _pad_lanes(small["ssd_dt_bias"])
    alog = _pad_lanes(small["ssd_a_log"])
    d_lanes = jnp.repeat(small["ssd_d"], SSD_HEAD_DIM, axis=1)

    lru_cols = 2 * D_MODEL
    wt = {"lru": big["w_main"][:, :lru_cols].T, "ssd": big["w_main"][:, lru_cols:].T, "gates": big["w_gates"].T,
          "dt": big["w_dt"].T}

    h1 = prenorm(x, small["pre_norm1"], mod, name="prenorm1")
    h1f = flat(h1)
    arriving = plan.weights_bg()
    if arriving is None:
        pm, arrived = mm_nn([(h1f, big["w_main"])], name="in_proj_main"), []
    else:
        pm, arrived = mm_nn([(h1f, big["w_main"])], name="in_proj_main", bg=arriving)
    pm = unflat(pm)
    big = dict(big, **plan.weights(arrived))
    for n in ("w_pa", "w_pb", "w_out", "w_ff1", "w_ff2"):
        wt[n] = big[n].T
    gates = unflat(mm_nn([(h1f, big["w_gates"])], name="in_proj_gates"))
    dtr = unflat(mm_nn([(h1f, big["w_dt"])], name="in_proj_dt"))
    lru_args = (small["lru_conv_w"], small["lru_conv_b"], wa_b, small["lru_ba"], wx_b, small["lru_bx"],
                small["lru_lambda"])
    h_lru, pa_in, ya, lru_kept = lru_fwd(pm, *lru_args, big["w_pa"], name="lru_fwd")
    ssd_args = (small["ssd_conv_w"], small["ssd_conv_b"], dtb, alog, d_lanes, small["ssd_norm_w"])
    (y_ssd, ynorm, states, yb, conv_ssd), _ = ssd_fwd(pm, dtr, *ssd_args, big["w_pb"], name="ssd_fwd")
    merged = merge_fwd(ya, yb, gates, small["b_gate"], name="merge_fwd")
    out1 = unflat(mm_nn([(flat(merged), big["w_out"])], name="proj_out"))
    x1, h2 = post1_pre2(x, out1, mod, small["post_norm1"], small["pre_norm2"], name="post1_pre2")
    f = mm_nn([(flat(h2), big["w_ff1"])], name="ff1")
    y2 = unflat(mm_nn([(f, big["w_ff2"])], a_fn=_relu_sq, name="ff2"))

    dx2, d_y2, pb_a, gl_a = final_bwd(x1, y2, target, mod, small["post_norm2"], name="final_bwd")
    d_y2f = flat(d_y2)
    d_f = mm_nn([(d_y2f, wt["w_ff2"])], out_dtype=BF16, extra=f,
                epi=lambda r, fv: r * (2.0 * jnp.maximum(fv, 0.0)), name="ff2_dx")
    g_ff2 = mm_tn(f, d_y2f, a_fn=_relu_sq, name="ff2_dw")
    d_h2 = unflat(mm_nn([(d_f, wt["w_ff1"])], name="ff1_dx"))
    g_ff1 = mm_tn(flat(h2), d_f, name="ff1_dw")
    dx1, d_out1, pb_b, gl_b = mid_bwd(d_h2, dx2, x1, out1, mod, small["pre_norm2"], small["post_norm1"],
                                      name="mid_bwd")
    d_out1f = flat(d_out1)
    d_merged = unflat(mm_nn([(d_out1f, wt["w_out"])], name="out_dx"))
    g_out = mm_tn(flat(merged), d_out1f, name="out_dw")
    d_ya, d_yb, d_gates, gl_c = merge_bwd(d_merged, ya, yb, gates, small["b_gate"], name="merge_bwd")
    g_pa = mm_tn(flat(pa_in), flat(d_ya), name="pa_dw")
    g_pb = mm_tn(flat(ynorm), flat(d_yb), name="pb_dw")
    leaving = plan.grads_bg({"w_pa": g_pa, "w_pb": g_pb, "w_out": g_out, "w_ff1": g_ff1, "w_ff2": g_ff2})
    (d_l, dh_lru, g_wa_b, g_wx_b, lru_rows), landed = lru_bwd(
        pm, h_lru, lru_kept, d_ya, small["lru_conv_w"], wa_b, wx_b, small["lru_lambda"], wt["w_pa"], wt["lru"],
        name="lru_bwd", bg=leaving)
    plan.grads_done(landed)
    d_s, dh_ssd, d_dt, r4, r2, r1 = ssd_bwd(pm, conv_ssd, dtr, y_ssd, states, d_yb, small["ssd_conv_w"], dtb, alog,
                                          d_lanes, small["ssd_norm_w"], wt["w_pb"], wt["ssd"], name="ssd_bwd")
    d_lf, d_sf, d_gf, d_dtf = flat(d_l), flat(d_s), flat(d_gates), flat(d_dt)
    g_in = jnp.concatenate([
        mm_tn(h1f, d_lf, name="in_dw_lru"), mm_tn(h1f, d_sf, name="in_dw_ssd"),
        mm_tn(h1f, d_dtf, name="in_dw_dt")[:, :SSD_HEADS], mm_tn(h1f, d_gf, name="in_dw_gates")], axis=1)
    leaving = plan.grads_bg({"w_in": g_in})
    partial = [flat(dh_lru), flat(dh_ssd)]
    if leaving is None:
        d_h1 = mm_nn([(d_gf, wt["gates"]), (d_dtf, wt["dt"])], add=partial, name="in_dx_gates")
    else:
        d_h1, landed = mm_nn([(d_gf, wt["gates"]), (d_dtf, wt["dt"])], add=partial, name="in_dx_gates", bg=leaving)
        plan.grads_done(landed)
    grad_x, pb_c, gl_d = first_bwd(unflat(d_h1), dx1, x, mod, small["pre_norm1"], name="first_bwd")

    d_mod = jnp.stack([pb_c[:, 0], pb_c[:, 1], pb_b[:, 2], pb_b[:, 0], pb_b[:, 1], pb_a[:, 0]], axis=1)
    loss_cols = gl_a[1:2]
    nh = SSD_HEADS
    small_grads = {
        "pre_norm1": gl_d[0:1], "post_norm1": gl_b[1:2], "b_gate": gl_c[0:1],
        "lru_conv_w": lru_rows[4:8], "lru_conv_b": lru_rows[3:4],
        "lru_wa": _lru_block_grads(g_wa_b), "lru_ba": lru_rows[0:1],
        "lru_wx": _lru_block_grads(g_wx_b), "lru_bx": lru_rows[1:2], "lru_lambda": lru_rows[2:3],
        "ssd_conv_w": r4[1:5], "ssd_conv_b": r4[0:1],
        "ssd_dt_bias": r1[0:1, :nh], "ssd_a_log": r1[1:2, :nh], "ssd_d": r1[2:3, :nh],
        "ssd_norm_w": r2[0:1], "pre_norm2": gl_b[0:1], "post_norm2": gl_a[0:1],
    }
    return loss_cols, grad_x, d_mod, small_grads


def _position():
    return lax.axis_index("x"), lax.axis_index("y"), lax.axis_index("c")


def _other_chips(x, y):
    return [(1 - x, y), (x, 1 - y), (1 - x, 1 - y)]


def allgather8(v, *, name):
    m_per, n = v.shape

    def body(x_ref, out_ref, send_sems, recv_sems, local_sem):
        x, y, c = _position()
        me, sibling = (x, y, c), (x, y, 1 - c)
        chips = _other_chips(x, y)

        def rows(px, py, pc):
            return out_ref.at[pl.ds((4 * px + 2 * py + pc) * m_per, m_per), :]

        def copy(k, block, to, src=None):
            return pltpu.make_async_remote_copy(
                src_ref=rows(*block) if src is None else src, dst_ref=rows(*block),
                send_sem=send_sems.at[k], recv_sem=recv_sems.at[k], device_id=to, device_id_type=MESH)

        mine = pltpu.make_async_copy(x_ref, rows(*me), local_sem)
        mine.start()
        first = [copy(0, me, sibling, src=x_ref)]
        first += [copy(1 + j, me, (*chip, c), src=x_ref) for j, chip in enumerate(chips)]
        for cp in first:
            cp.start()
        passed = [copy(4 + j, (*chip, c), sibling) for j, chip in enumerate(chips)]
        for j, chip in enumerate(chips):
            copy(1 + j, (*chip, c), me).wait_recv()
            passed[j].start()
        copy(0, sibling, me).wait_recv()
        for j, chip in enumerate(chips):
            copy(4 + j, (*chip, 1 - c), me).wait_recv()
        for cp in first + passed:
            cp.wait_send()
        mine.wait()

    return _pcall(
        body, name=name,
        out_shape=jax.ShapeDtypeStruct((N_DEV * m_per, n), v.dtype),
        in_specs=[pl.BlockSpec(memory_space=pltpu.VMEM)],
        out_specs=pl.BlockSpec(memory_space=pltpu.VMEM),
        scratch_shapes=[pltpu.SemaphoreType.DMA((7,)), pltpu.SemaphoreType.DMA((7,)), pltpu.SemaphoreType.DMA],
    )(v)


def gather_weights(shards, *, name):
    n = len(shards)
    half = [s.shape[0] // 2 for s in shards]

    def body(*refs):
        ins, outs = refs[:n], refs[n:2 * n]
        send_sems, recv_sems, local_sems = refs[2 * n:]
        x, y, c = _position()
        me_chip = 2 * x + y
        chips = _other_chips(x, y)

        def piece(w, chip, core):
            return outs[w].at[chip, pl.ds(core * half[w], half[w]), :]

        def copy(w, k, chip, core, to, src=None):
            dst = piece(w, chip, core)
            return pltpu.make_async_remote_copy(
                src_ref=dst if src is None else src, dst_ref=dst,
                send_sem=send_sems.at[6 * w + k], recv_sem=recv_sems.at[6 * w + k], device_id=to, device_id_type=MESH)

        local = [pltpu.make_async_copy(ins[w], outs[w].at[me_chip], local_sems.at[w]) for w in range(n)]
        for cp in local:
            cp.start()
        sent = []
        for w in range(n):
            for j, (px, py) in enumerate(chips):
                cp = copy(w, j, me_chip, c, (px, py, c), src=ins[w].at[pl.ds(c * half[w], half[w]), :])
                cp.start()
                sent.append(cp)
        for w in range(n):
            for j, (px, py) in enumerate(chips):
                copy(w, j, 2 * px + py, c, (px, py, c)).wait_recv()
                cp = copy(w, 3 + j, 2 * px + py, c, (x, y, 1 - c))
                cp.start()
                sent.append(cp)
        for w in range(n):
            for j, (px, py) in enumerate(chips):
                copy(w, 3 + j, 2 * px + py, 1 - c, (x, y, 1 - c)).wait_recv()
        for cp in sent:
            cp.wait_send()
        for cp in local:
            cp.wait()

    return _pcall(
        body, name=name,
        out_shape=[jax.ShapeDtypeStruct((N_CHIPS,) + s.shape, s.dtype) for s in shards],
        in_specs=[ANY] * n, out_specs=[ANY] * n,
        scratch_shapes=[pltpu.SemaphoreType.DMA((6 * n,)), pltpu.SemaphoreType.DMA((6 * n,)),
                        pltpu.SemaphoreType.DMA((n,))],
    )(*shards)


STAGE_BYTES = 2 << 20


def _stage_rows(rows, width, itemsize=4):
    return _pick(rows, tuple(t for t in (1024, 512, 256, 128, 64, 32, 16, 8) if t * width * itemsize <= STAGE_BYTES * 3 // 2))


def _staged(chunks, bufs, load_sems):
    count, pending = {}, {}

    def load(i):
        cls, src, _ = chunks[i]
        slot = count.get(cls, 0) % 2
        count[cls] = count.get(cls, 0) + 1
        for cp, remote in pending.pop((cls, slot), []):
            if remote:
                cp.wait_send()
            else:
                cp.wait()
        staged = bufs[cls].at[slot, pl.ds(0, src.shape[0]), :]
        ld = pltpu.make_async_copy(src, staged, load_sems[cls].at[slot])
        ld.start()
        return ld, cls, slot, staged

    cur = load(0)
    for i in range(len(chunks)):
        nxt = load(i + 1) if i + 1 < len(chunks) else None
        ld, cls, slot, staged = cur
        ld.wait()
        started = []
        for make in chunks[i][2]:
            cp, remote = make(staged, slot)
            cp.start()
            started.append((cp, remote))
        pending[(cls, slot)] = started
        cur = nxt
    for started in pending.values():
        for cp, remote in started:
            if remote:
                cp.wait_send()
            else:
                cp.wait()


def _stage_scratch(widths_rows, dtype):
    scratch = []
    for width, rows in widths_rows:
        scratch += [pltpu.VMEM((2, rows, width), dtype), pltpu.SemaphoreType.DMA((2,)), pltpu.SemaphoreType.DMA((2,)),
                    pltpu.SemaphoreType.DMA((2,))]
    return scratch


def send_half_to_sibling(grads, *, name):
    n = len(grads)
    half = [g.shape[1] // 2 for g in grads]
    widths = sorted({g.shape[2] for g in grads})
    chunk_rows = [_stage_rows(h, g.shape[2]) for g, h in zip(grads, half)]
    plan = [(w, k, r0) for w in range(n) for k in range(N_CHIPS) for r0 in range(0, half[w], chunk_rows[w])]

    def body(*refs):
        ins, theirs = refs[:n], refs[n:2 * n]
        recv_sems = refs[2 * n]
        stage = refs[2 * n + 1:]
        bufs = {wd: stage[4 * i] for i, wd in enumerate(widths)}
        load_sems = {wd: stage[4 * i + 1] for i, wd in enumerate(widths)}
        send_sems = {wd: stage[4 * i + 2] for i, wd in enumerate(widths)}
        x, y, c = _position()
        chunks = []
        for idx, (w, k, r0) in enumerate(plan):
            wd = grads[w].shape[2]
            rb = chunk_rows[w]

            def make(staged, slot, idx=idx, w=w, k=k, r0=r0, wd=wd, rb=rb):
                return pltpu.make_async_remote_copy(
                    src_ref=staged, dst_ref=theirs[w].at[k, pl.ds(r0, rb), :], send_sem=send_sems[wd].at[slot],
                    recv_sem=recv_sems.at[idx], device_id=(x, y, 1 - c), device_id_type=MESH), True

            chunks.append((wd, ins[w].at[k, pl.ds((1 - c) * half[w] + r0, rb), :], [make]))
        _staged(chunks, bufs, load_sems)
        for idx, (w, k, r0) in enumerate(plan):
            wd = grads[w].shape[2]
            landed = theirs[w].at[k, pl.ds(r0, chunk_rows[w]), :]
            pltpu.make_async_remote_copy(
                src_ref=landed, dst_ref=landed, send_sem=send_sems[wd].at[0], recv_sem=recv_sems.at[idx],
                device_id=(x, y, 1 - c), device_id_type=MESH).wait_recv()

    stage_rows = [(wd, max(r for g, r in zip(grads, chunk_rows) if g.shape[2] == wd)) for wd in widths]
    return _pcall(
        body, name=name,
        out_shape=[jax.ShapeDtypeStruct((N_CHIPS, h, g.shape[2]), g.dtype) for g, h in zip(grads, half)],
        in_specs=[ANY] * n, out_specs=[ANY] * n,
        scratch_shapes=[pltpu.SemaphoreType.DMA((len(plan),))] + _stage_scratch(stage_rows, F32),
    )(*grads)


def _chip_exchange_background(arrays, out_shapes, src_of, dst_of, landed_of, own_of):
    n = len(arrays)

    def copies(ins, outs, scr):
        send_sems, recv_sems, local_sems = scr
        x, y, c = _position()
        me_chip = 2 * x + y
        local, sends, recvs = [], [], []
        for w in range(n):
            local.append(pltpu.make_async_copy(*own_of(ins[w], outs[w], w, me_chip), local_sems.at[w]))
            for j, (px, py) in enumerate(_other_chips(x, y)):
                sems = dict(send_sem=send_sems.at[3 * w + j], recv_sem=recv_sems.at[3 * w + j],
                            device_id=(px, py, c), device_id_type=MESH)
                sends.append(pltpu.make_async_remote_copy(
                    src_ref=src_of(ins[w], w, 2 * px + py, me_chip, c), dst_ref=dst_of(outs[w], w, me_chip, c), **sems))
                landed = landed_of(outs[w], w, 2 * px + py, c)
                recvs.append(pltpu.make_async_remote_copy(src_ref=landed, dst_ref=landed, **sems))
        return local, sends, recvs

    def start(ins, outs, scr):
        local, sends, _ = copies(ins, outs, scr)
        for cp in local + sends:
            cp.start()

    def finish(ins, outs, scr):
        local, sends, recvs = copies(ins, outs, scr)
        for cp in recvs:
            cp.wait_recv()
        for cp in sends:
            cp.wait_send()
        for cp in local:
            cp.wait()

    scratch = [pltpu.SemaphoreType.DMA((3 * n,)), pltpu.SemaphoreType.DMA((3 * n,)), pltpu.SemaphoreType.DMA((n,))]
    return Background(arrays, out_shapes, scratch, start, finish)


def scatter_background(parts):
    return _chip_exchange_background(
        parts, [jax.ShapeDtypeStruct(p.shape, p.dtype) for p in parts],
        src_of=lambda ref, w, peer, me, c: ref.at[peer], dst_of=lambda ref, w, me, c: ref.at[me],
        landed_of=lambda ref, w, peer, c: ref.at[peer], own_of=lambda i, o, w, me: (i.at[me], o.at[me]))


def gather_halves_background(shards):
    half = [s.shape[0] // 2 for s in shards]
    rows = lambda w, c: pl.ds(c * half[w], half[w])
    return _chip_exchange_background(
        shards, [jax.ShapeDtypeStruct((N_CHIPS,) + s.shape, s.dtype) for s in shards],
        src_of=lambda ref, w, peer, me, c: ref.at[rows(w, c), :], dst_of=lambda ref, w, me, c: ref.at[me, rows(w, c), :],
        landed_of=lambda ref, w, peer, c: ref.at[peer, rows(w, c), :], own_of=lambda i, o, w, me: (i, o.at[me]))


def fill_other_half(gathered, *, name):
    n = len(gathered)
    half = [g.shape[1] // 2 for g in gathered]
    widths = sorted({g.shape[2] for g in gathered})
    chunk_rows = [_stage_rows(h, g.shape[2], itemsize=2) for g, h in zip(gathered, half)]
    plan = [(w, j, r0) for w in range(n) for j in range(N_CHIPS - 1) for r0 in range(0, half[w], chunk_rows[w])]

    def body(*refs):
        ins, outs = refs[:n], refs[n:2 * n]
        recv_sems = refs[2 * n]
        stage = refs[2 * n + 1:]
        bufs = {wd: stage[4 * i] for i, wd in enumerate(widths)}
        load_sems = {wd: stage[4 * i + 1] for i, wd in enumerate(widths)}
        send_sems = {wd: stage[4 * i + 2] for i, wd in enumerate(widths)}
        x, y, c = _position()
        chips = _other_chips(x, y)
        chunks = []
        for idx, (w, j, r0) in enumerate(plan):
            wd, rb = gathered[w].shape[2], chunk_rows[w]
            k = 2 * chips[j][0] + chips[j][1]

            def make(staged, slot, idx=idx, w=w, k=k, r0=r0, wd=wd, rb=rb):
                return pltpu.make_async_remote_copy(
                    src_ref=staged, dst_ref=outs[w].at[k, pl.ds(c * half[w] + r0, rb), :],
                    send_sem=send_sems[wd].at[slot], recv_sem=recv_sems.at[idx],
                    device_id=(x, y, 1 - c), device_id_type=MESH), True

            chunks.append((wd, ins[w].at[k, pl.ds(c * half[w] + r0, rb), :], [make]))
        _staged(chunks, bufs, load_sems)
        for idx, (w, j, r0) in enumerate(plan):
            wd = gathered[w].shape[2]
            k = 2 * chips[j][0] + chips[j][1]
            landed = outs[w].at[k, pl.ds((1 - c) * half[w] + r0, chunk_rows[w]), :]
            pltpu.make_async_remote_copy(
                src_ref=landed, dst_ref=landed, send_sem=send_sems[wd].at[0], recv_sem=recv_sems.at[idx],
                device_id=(x, y, 1 - c), device_id_type=MESH).wait_recv()

    stage_rows = [(wd, max(r for g, r in zip(gathered, chunk_rows) if g.shape[2] == wd)) for wd in widths]
    return _pcall(
        body, name=name, out_shape=[jax.ShapeDtypeStruct(g.shape, g.dtype) for g in gathered],
        in_specs=[ANY] * n, out_specs=[ANY] * n, input_output_aliases={w: w for w in range(n)},
        scratch_shapes=[pltpu.SemaphoreType.DMA((len(plan),))] + _stage_scratch(stage_rows, gathered[0].dtype),
    )(*gathered)


def join_with_sibling(halves, *, name):
    n = len(halves)
    widths = sorted({h.shape[1] for h in halves})
    chunk_rows = [_stage_rows(h.shape[0], h.shape[1]) for h in halves]
    plan = [(w, r0) for w in range(n) for r0 in range(0, halves[w].shape[0], chunk_rows[w])]

    def body(*refs):
        ins, outs = refs[:n], refs[n:2 * n]
        recv_sems = refs[2 * n]
        stage = refs[2 * n + 1:]
        bufs = {wd: stage[4 * i] for i, wd in enumerate(widths)}
        load_sems = {wd: stage[4 * i + 1] for i, wd in enumerate(widths)}
        send_sems = {wd: stage[4 * i + 2] for i, wd in enumerate(widths)}
        store_sems = {wd: stage[4 * i + 3] for i, wd in enumerate(widths)}
        x, y, c = _position()
        chunks = []
        for idx, (w, r0) in enumerate(plan):
            h, wd = halves[w].shape
            rb = chunk_rows[w]

            def to_sibling(staged, slot, idx=idx, w=w, r0=r0, h=h, wd=wd, rb=rb):
                return pltpu.make_async_remote_copy(
                    src_ref=staged, dst_ref=outs[w].at[pl.ds(c * h + r0, rb), :], send_sem=send_sems[wd].at[slot],
                    recv_sem=recv_sems.at[idx], device_id=(x, y, 1 - c), device_id_type=MESH), True

            def to_mine(staged, slot, w=w, r0=r0, h=h, wd=wd, rb=rb):
                return pltpu.make_async_copy(staged, outs[w].at[pl.ds(c * h + r0, rb), :], store_sems[wd].at[slot]), False

            chunks.append((wd, ins[w].at[pl.ds(r0, rb), :], [to_sibling, to_mine]))
        _staged(chunks, bufs, load_sems)
        for idx, (w, r0) in enumerate(plan):
            h, wd = halves[w].shape
            landed = outs[w].at[pl.ds((1 - c) * h + r0, chunk_rows[w]), :]
            pltpu.make_async_remote_copy(
                src_ref=landed, dst_ref=landed, send_sem=send_sems[wd].at[0], recv_sem=recv_sems.at[idx],
                device_id=(x, y, 1 - c), device_id_type=MESH).wait_recv()

    stage_rows = [(wd, max(r for h, r in zip(halves, chunk_rows) if h.shape[1] == wd)) for wd in widths]
    return _pcall(
        body, name=name,
        out_shape=[jax.ShapeDtypeStruct((2 * h.shape[0], h.shape[1]), h.dtype) for h in halves],
        in_specs=[ANY] * n, out_specs=[ANY] * n,
        scratch_shapes=[pltpu.SemaphoreType.DMA((len(plan),))] + _stage_scratch(stage_rows, F32),
    )(*halves)


def _row_tile(rows, cols, itemsize=4, budget=2 << 20):
    for t in (1024, 512, 256, 128, 64, 32, 16, 8):
        if rows % t == 0 and t * cols * itemsize <= budget:
            return t
    return rows


def add_half_to_bf16(core, full, theirs, *, name):
    k, r, c = theirs.shape
    tr = _row_tile(r, c)
    nb = r // tr

    def body(core_ref, a_ref, b_ref, o_ref):
        o_ref[...] = (a_ref[...] + b_ref[...]).astype(BF16)

    spec = pl.BlockSpec((None, tr, c), lambda i, j, core_ref: (i, j, 0))
    grid_spec = pltpu.PrefetchScalarGridSpec(
        num_scalar_prefetch=1, grid=(k, nb),
        in_specs=[pl.BlockSpec((None, tr, c), lambda i, j, core_ref: (i, core_ref[0] * nb + j, 0)), spec],
        out_specs=spec)
    return _pcall(body, name=name, grid_spec=grid_spec,
                  out_shape=jax.ShapeDtypeStruct(theirs.shape, BF16))(core, full, theirs)


def sum_blocks(v, *, name):
    k, r, c = v.shape
    tr = _row_tile(r, c * k)

    def body(v_ref, o_ref):
        acc = v_ref[0].astype(F32)
        for j in range(1, k):
            acc = acc + v_ref[j].astype(F32)
        o_ref[...] = acc

    return _pcall(body, name=name, grid=(r // tr,),
                  in_specs=[pl.BlockSpec((k, tr, c), lambda i: (0, i, 0))],
                  out_specs=pl.BlockSpec((tr, c), lambda i: (i, 0)),
                  out_shape=jax.ShapeDtypeStruct((r, c), F32))(v)


def adamw(w, g, m, v, *, name):
    r, c = w.shape
    tr = _row_tile(r, c, budget=1 << 20)
    m_scale = 1.0 / (1.0 - ADAM_B1 ** ADAM_STEP)
    v_scale = 1.0 / (1.0 - ADAM_B2 ** ADAM_STEP)

    def body(w_ref, g_ref, m_ref, v_ref, d_ref, nm_ref, nv_ref):
        gv = g_ref[...]
        nm = ADAM_B1 * m_ref[...] + (1.0 - ADAM_B1) * gv
        nv = ADAM_B2 * v_ref[...] + (1.0 - ADAM_B2) * (gv * gv)
        nm_ref[...] = nm
        nv_ref[...] = nv
        d_ref[...] = -ADAM_LR * ((nm * m_scale) / (jnp.sqrt(nv * v_scale) + ADAM_EPS) + ADAM_WD * w_ref[...])

    spec = pl.BlockSpec((tr, c), lambda i: (i, 0))
    return _pcall(body, name=name, grid=(r // tr,), in_specs=[spec] * 4, out_specs=[spec] * 3,
                  out_shape=[jax.ShapeDtypeStruct((r, c), F32)] * 3)(w, g, m, v)


def ada_fwd(c_all, w_shard, b_shard, *, name):
    bsz, d = c_all.shape
    ncol = w_shard.shape[1]

    def body(c_ref, w_ref, b_ref, o_ref):
        cv = c_ref[...]
        act = (cv * _sigmoid(cv)).astype(BF16)
        o_ref[...] = _dot(act, w_ref[...].astype(BF16)) + b_ref[...]

    tn = _pick(ncol, (512, 256, 128))
    return _pcall(body, name=name, grid=(ncol // tn,),
                  in_specs=[pl.BlockSpec((bsz, d), lambda j: (0, 0)), pl.BlockSpec((d, tn), lambda j: (0, j)),
                            pl.BlockSpec((1, tn), lambda j: (0, j))],
                  out_specs=pl.BlockSpec((bsz, tn), lambda j: (0, j)),
                  out_shape=jax.ShapeDtypeStruct((bsz, ncol), F32))(c_all, w_shard, b_shard)


def ada_bwd(c_all, d_mod_all, d_mod_cols, *, name):
    bsz, d = c_all.shape
    ncol = d_mod_cols.shape[1]
    nall = d_mod_all.shape[1]

    def body(c_ref, da_ref, dc_ref, gw_ref, gb_ref):
        cv = c_ref[...]
        act = (cv * _sigmoid(cv)).astype(BF16)
        gw_ref[...] = _dot_tn(act, dc_ref[...].astype(BF16))
        gb_ref[...] = _colsum(da_ref[...])

    full = lambda s: pl.BlockSpec(s, lambda: (0,) * len(s))
    return _pcall(body, name=name,
                  in_specs=[full((bsz, d)), full((bsz, nall)), full((bsz, ncol))],
                  out_specs=[full((d, ncol)), full((1, nall))],
                  out_shape=[jax.ShapeDtypeStruct((d, ncol), F32), jax.ShapeDtypeStruct((1, nall), F32)],
                  )(c_all, d_mod_all, d_mod_cols)


WEIGHT_NAMES = ['w_ada', 'b_ada', 'pre_norm1', 'post_norm1', 'w_in', 'b_gate', 'lru_conv_w', 'lru_conv_b', 'lru_wa',
                'lru_ba', 'lru_wx', 'lru_bx', 'lru_lambda', 'w_pa', 'ssd_conv_w', 'ssd_conv_b', 'ssd_dt_bias',
                'ssd_a_log', 'ssd_d', 'ssd_norm_w', 'w_pb', 'w_out', 'pre_norm2', 'post_norm2', 'w_ff1', 'w_ff2']
BIG_NAMES = ['w_in', 'w_pa', 'w_pb', 'w_out', 'w_ff1', 'w_ff2']
COLUMN_SHARDED = ('w_in', 'w_ff1')
SMALL_NAMES = [n for n in WEIGHT_NAMES if n not in BIG_NAMES + ['w_ada', 'b_ada']]
SHARDED_SMALL = ('lru_conv_w', 'ssd_conv_w')
PACK_WIDTH = 1024


def _whole(name, gathered):
    if name in COLUMN_SHARDED:
        return jnp.transpose(gathered, (1, 0, 2)).reshape(gathered.shape[1], N_CHIPS * gathered.shape[2])
    return gathered.reshape(N_CHIPS * gathered.shape[1], gathered.shape[2])


def _by_chip(name, g):
    if name in COLUMN_SHARDED:
        return jnp.transpose(g.reshape(g.shape[0], N_CHIPS, g.shape[1] // N_CHIPS), (1, 0, 2))
    return g.reshape(N_CHIPS, g.shape[0] // N_CHIPS, g.shape[1])


class ChipExchange:
    def __init__(self, shards, core):
        self.shards, self.core = shards, core
        self.pending, self.halves = [], {}

    def weights_bg(self):
        return gather_halves_background(list(self.shards.values()))

    def weights(self, arrived):
        swapped = fill_other_half(arrived, name="weights_from_sibling")
        return {n: _whole(n, g) for n, g in zip(self.shards, swapped)}

    def grads_bg(self, grads):
        self.pending = list(grads)
        by_chip = [_by_chip(n, g) for n, g in grads.items()]
        theirs = send_half_to_sibling(by_chip, name="grads_to_sibling_" + self.pending[0])
        sums = [add_half_to_bf16(self.core, a, b, name="add_cores_" + n)
                for n, a, b in zip(self.pending, by_chip, theirs)]
        return scatter_background(sums)

    def grads_done(self, landed):
        for n, p in zip(self.pending, landed):
            self.halves[n] = sum_blocks(p, name="add_chips_" + n)

    def reduced(self):
        names = list(self.halves)
        return dict(zip(names, join_with_sibling([self.halves[n] for n in names], name="grads_join")))


def _pack(parts):
    flat = jnp.concatenate([p.reshape(-1).astype(F32) for p in parts])
    rows = -(-flat.shape[0] // (PACK_WIDTH * SUBLANES)) * SUBLANES
    return jnp.pad(flat, (0, rows * PACK_WIDTH - flat.shape[0])).reshape(rows, PACK_WIDTH)


def _unpack(packed, shapes):
    flat = packed.reshape(-1)
    out, pos = [], 0
    for s in shapes:
        size = int(np.prod(s))
        out.append(flat[pos:pos + size].reshape(s))
        pos += size
    return out


def kernel(x, c, w_ada, b_ada, pre_norm1, post_norm1, w_in, b_gate, lru_conv_w, lru_conv_b, lru_wa, lru_ba, lru_wx, lru_bx, lru_lambda, w_pa, ssd_conv_w, ssd_conv_b, ssd_dt_bias, ssd_a_log, ssd_d, ssd_norm_w, w_pb, w_out, pre_norm2, post_norm2, w_ff1, w_ff2, loss_target, m_w_ada, m_b_ada, m_pre_norm1, m_post_norm1, m_w_in, m_b_gate, m_lru_conv_w, m_lru_conv_b, m_lru_wa, m_lru_ba, m_lru_wx, m_lru_bx, m_lru_lambda, m_w_pa, m_ssd_conv_w, m_ssd_conv_b, m_ssd_dt_bias, m_ssd_a_log, m_ssd_d, m_ssd_norm_w, m_w_pb, m_w_out, m_pre_norm2, m_post_norm2, m_w_ff1, m_w_ff2, v_w_ada, v_b_ada, v_pre_norm1, v_post_norm1, v_w_in, v_b_gate, v_lru_conv_w, v_lru_conv_b, v_lru_wa, v_lru_ba, v_lru_wx, v_lru_bx, v_lru_lambda, v_w_pa, v_ssd_conv_w, v_ssd_conv_b, v_ssd_dt_bias, v_ssd_a_log, v_ssd_d, v_ssd_norm_w, v_w_pb, v_w_out, v_pre_norm2, v_post_norm2, v_w_ff1, v_w_ff2):
    given = dict(locals())
    bsz, seq, d = x.shape
    my_x, my_y, my_c = lax.axis_index("x"), lax.axis_index("y"), lax.axis_index("c")
    chip = 2 * my_x + my_y
    dev = 2 * chip + my_c
    strip = lambda a: a if a.ndim == 2 else a[0]
    w = {n: strip(given[n]) for n in WEIGHT_NAMES}
    m = {n: strip(given["m_" + n]) for n in WEIGHT_NAMES}
    v = {n: strip(given["v_" + n]) for n in WEIGHT_NAMES}

    first_shapes = [c.shape] + [w[n].shape for n in SHARDED_SMALL]
    first = allgather8(_pack([c] + [w[n] for n in SHARDED_SMALL]), name="gather_c_conv")
    first = first.reshape(N_DEV, -1, PACK_WIDTH)
    per_dev = [_unpack(first[k], first_shapes) for k in range(N_DEV)]
    c_all = jnp.concatenate([p[0] for p in per_dev], axis=0)
    conv_full = {n: jnp.concatenate([per_dev[2 * k][1 + i] for k in range(N_CHIPS)], axis=1)
                 for i, n in enumerate(SHARDED_SMALL)}

    ncol = w["w_ada"].shape[1]
    b_cols = lax.dynamic_slice(b_ada, (0, chip * ncol), (1, ncol))
    mod_cols = ada_fwd(c_all, w["w_ada"], b_cols, name="ada_fwd")
    mod_all = allgather8(mod_cols, name="gather_mod").reshape(N_CHIPS, 2, N_DEV * bsz, ncol)[:, 0]
    mod_all = jnp.transpose(mod_all, (1, 0, 2)).reshape(N_DEV * bsz, N_CHIPS * ncol)
    mod = lax.dynamic_slice(mod_all, (dev * bsz, 0), (bsz, 6 * d)).reshape(bsz, 6, d)
    mod = jnp.pad(mod, ((0, 0), (0, 2), (0, 0)))

    w_in_full = _whole("w_in", gather_weights([w["w_in"].astype(BF16)], name="gather_w_in")[0])
    big = {"w_main": w_in_full[:, :8192],
           "w_dt": jnp.pad(w_in_full[:, 8192:8192 + SSD_HEADS], ((0, 0), (0, LANES - SSD_HEADS))),
           "w_gates": w_in_full[:, 8192 + SSD_HEADS:]}
    small = {n: w[n] for n in SMALL_NAMES}
    small.update(conv_full)
    plan = ChipExchange({n: w[n].astype(BF16) for n in BIG_NAMES if n != "w_in"}, my_c.astype(jnp.int32).reshape(1))

    loss_cols, grad_x, d_mod, small_grads = local_step(x, loss_target, mod, big, small, plan)

    packed = _pack([d_mod, loss_cols] + [small_grads[n] for n in SMALL_NAMES])
    rows = packed.shape[0]
    everyone = allgather8(packed, name="gather_small").reshape(N_DEV, rows, PACK_WIDTH)
    d_mod_all = everyone[:, :bsz * 6].reshape(N_DEV * bsz, 6 * d)
    summed = sum_blocks(everyone, name="sum_small")
    shapes = [d_mod.shape, loss_cols.shape] + [small_grads[n].shape for n in SMALL_NAMES]
    parts = _unpack(summed, shapes)
    loss = jnp.sum(parts[1])
    grads = dict(zip(SMALL_NAMES, parts[2:]))
    for n in SHARDED_SMALL:
        cols = w[n].shape[1]
        grads[n] = lax.dynamic_slice(grads[n], (0, chip * cols), (grads[n].shape[0], cols))
    d_mod_cols = lax.dynamic_slice(d_mod_all, (0, chip * ncol), (N_DEV * bsz, ncol))
    grads["w_ada"], grads["b_ada"] = ada_bwd(c_all, d_mod_all, d_mod_cols, name="ada_bwd")

    grads.update(plan.reduced())

    delta, new_m, new_v = {}, {}, {}
    for n in BIG_NAMES + ["w_ada", "b_ada"]:
        delta[n], new_m[n], new_v[n] = adamw(w[n], grads[n], m[n], v[n], name="adamw_" + n)
    shapes = [w[n].shape for n in SMALL_NAMES]
    pk = lambda src: _pack([src[n] for n in SMALL_NAMES])
    upd = adamw(pk(w), pk(grads), pk(m), pk(v), name="adamw_small")
    for out, packed_out in zip((delta, new_m, new_v), upd):
        out.update(zip(SMALL_NAMES, _unpack(packed_out, shapes)))

    shaped = lambda src: [src[n].reshape(given[n].shape) for n in WEIGHT_NAMES]
    return (loss, grad_x, *shaped(grads), *shaped(delta), *shaped(new_m), *shaped(new_v))
```

```python
import functools
import math

import numpy as np
import jax
import jax.numpy as jnp
from jax import lax
from jax.experimental import pallas as pl
from jax.experimental.pallas import tpu as pltpu

F32 = jnp.float32
BF16 = jnp.bfloat16
HI = lax.Precision.HIGHEST
MESH = pl.DeviceIdType.MESH

D_MODEL = 1024
LRU_HEADS = 16
LRU_HEAD_DIM = 64
LRU_C = 8.0
SSD_INNER = 2048
SSD_HEADS = 32
SSD_HEAD_DIM = 64
SSD_GROUPS = 8
SSD_STATE = 128
SSD_CHUNK = 128
SSD_CONV_DIM = 4096
D_FF = 4096
EPS = 1e-6
N_CHIPS = 4
N_DEV = 8
LANES = 128
SUBLANES = 8

ADAM_LR = 0.001
ADAM_B1 = 0.9
ADAM_B2 = 0.999
ADAM_EPS = 1e-08
ADAM_WD = 0.01
ADAM_STEP = 10


ANY = pl.BlockSpec(memory_space=pl.ANY)


def _pcall(body, **kw):
    return pl.pallas_call(body, **kw)


class Background:
    def __init__(self, inputs, out_shapes, scratch, start, finish):
        self.inputs, self.out_shapes, self.scratch = list(inputs), list(out_shapes), list(scratch)
        self.start, self.finish = start, finish

    def wrap(self, body, kw):
        n_in, n_out = len(kw["in_specs"]), len(kw["out_specs"])
        n_scr = len(kw.get("scratch_shapes", []))
        b_in, b_out = len(self.inputs), len(self.out_shapes)
        grid = kw["grid"]

        def wrapped(*refs):
            ins, b_ins = refs[:n_in], refs[n_in:n_in + b_in]
            o0 = n_in + b_in
            outs, b_outs = refs[o0:o0 + n_out], refs[o0 + n_out:o0 + n_out + b_out]
            s0 = o0 + n_out + b_out
            scr, b_scr = refs[s0:s0 + n_scr], refs[s0 + n_scr:]
            ids = [pl.program_id(a) for a in range(len(grid))]
            first = functools.reduce(jnp.logical_and, [i == 0 for i in ids])
            last = functools.reduce(jnp.logical_and, [i == g - 1 for i, g in zip(ids, grid)])

            @pl.when(first)
            def _():
                self.start(b_ins, b_outs, b_scr)

            body(*ins, *outs, *scr)

            @pl.when(last)
            def _():
                self.finish(b_ins, b_outs, b_scr)

        kw = dict(kw, in_specs=list(kw["in_specs"]) + [ANY] * b_in, out_specs=list(kw["out_specs"]) + [ANY] * b_out,
                  out_shape=list(kw["out_shape"]) + self.out_shapes,
                  scratch_shapes=list(kw.get("scratch_shapes", [])) + self.scratch)
        return wrapped, kw


def _run(body, args, bg, **kw):
    n_out = len(kw["out_shape"])
    if bg is None:
        return list(_pcall(body, **kw)(*args)), []
    body, kw = bg.wrap(body, kw)
    outs = _pcall(body, **kw)(*args, *bg.inputs)
    return list(outs[:n_out]), list(outs[n_out:])


def _sigmoid(v):
    return 1.0 / (1.0 + jnp.exp(-v))


def _log1p(u):
    return jnp.where(u < 1e-3, u * (1.0 - u * (0.5 - u * (1.0 / 3.0))), jnp.log(1.0 + u))


def _softplus(v):
    return jnp.maximum(v, 0.0) + _log1p(jnp.exp(-jnp.abs(v)))


def _neg_expm1(v):
    small = -v * (1.0 + v * (0.5 + v * (1.0 / 6.0 + v * (1.0 / 24.0))))
    return jnp.where(v > -0.05, small, 1.0 - jnp.exp(v))


_GELU_K = math.sqrt(2.0 / math.pi)


def _gelu(v):
    t = jnp.tanh(_GELU_K * (v + 0.044715 * v * v * v))
    return 0.5 * v * (1.0 + t)


def _gelu_grad(v):
    t = jnp.tanh(_GELU_K * (v + 0.044715 * v * v * v))
    return 0.5 * (1.0 + t) + 0.5 * v * (1.0 - t * t) * _GELU_K * (1.0 + 3.0 * 0.044715 * v * v)


def _colsum(v):
    return jnp.sum(v, axis=0, keepdims=True)


def _dot(a, b, precision=None):
    return lax.dot_general(a, b, (((1,), (0,)), ((), ())), preferred_element_type=F32, precision=precision)


def _dot_nt(a, b):
    return lax.dot_general(a, b, (((1,), (1,)), ((), ())), preferred_element_type=F32)


def _dot_tn(a, b):
    return lax.dot_general(a, b, (((0,), (0,)), ((), ())), preferred_element_type=F32)


def _shift_down(xt, prev8, j):
    if j == 0:
        return xt
    n = xt.shape[0]
    r = pltpu.roll(xt, j, 0)
    p = pltpu.roll(prev8, j, 0)
    rows = lax.broadcasted_iota(jnp.int32, (SUBLANES, xt.shape[1]), 0)
    top = jnp.where(rows < j, p, r[0:SUBLANES])
    if n == SUBLANES:
        return top
    return jnp.concatenate([top, r[SUBLANES:]], axis=0)


def _shift_up(xt, next8, j):
    if j == 0:
        return xt
    n = xt.shape[0]
    r = pltpu.roll(xt, n - j, 0)
    p = pltpu.roll(next8, SUBLANES - j, 0)
    rows = lax.broadcasted_iota(jnp.int32, (SUBLANES, xt.shape[1]), 0)
    bot = jnp.where(rows >= SUBLANES - j, p, r[n - SUBLANES:])
    if n == SUBLANES:
        return bot
    return jnp.concatenate([r[:n - SUBLANES], bot], axis=0)


def _conv4(xt, prev8, w, b):
    out = b + w[3:4] * xt
    for k in range(3):
        out = out + w[k:k + 1] * _shift_down(xt, prev8, 3 - k)
    return out


def _conv4_bwd(d_out, next8, xt, w):
    d_x = w[3:4] * d_out
    d_w = []
    for k in range(3):
        up = _shift_up(d_out, next8, 3 - k)
        d_x = d_x + w[k:k + 1] * up
        d_w.append(_colsum(up * xt))
    d_w.append(_colsum(d_out * xt))
    return d_x, d_w, _colsum(d_out)


def _stack_rows(rows, width):
    rows = list(rows) + [jnp.zeros((1, width), F32)] * (SUBLANES - len(rows))
    return jnp.concatenate(rows, axis=0)


def _pick(n, cands):
    for c in cands:
        if n % c == 0:
            return c
    raise ValueError(f"no tile for {n}")


MM_ROWS = 512
MM_PANEL_COLS = 2048
MM_SUB = 512


def mm_nn(pairs, *, name, out_dtype=F32, a_fn=None, add=None, epi=None, extra=None, bg=None):
    np_ = len(pairs)
    m, n = pairs[0][0].shape[0], pairs[0][1].shape[1]
    tm = _pick(m, (MM_ROWS, 256, 128, 64, 32, 16, 8))
    pn = n if n <= MM_PANEL_COLS else _pick(n, (MM_PANEL_COLS, 1024, 512, 256, 128))
    ns = _pick(pn, (MM_SUB, 256, 128))
    adds = list(add or ())
    has_extra = extra is not None
    stage0 = a_fn is not None or pairs[0][0].dtype != BF16

    def body(*refs):
        a_refs, b_refs = refs[:np_], refs[np_:2 * np_]
        pos = 2 * np_
        extra_ref = None
        add_refs = refs[pos:pos + len(adds)]
        pos += len(adds)
        if has_extra:
            extra_ref = refs[pos]
            pos += 1
        o_ref = refs[pos]
        lhs = list(a_refs)
        if stage0:
            av = a_refs[0][...]
            if a_fn is not None:
                av = a_fn(av)
            refs[pos + 1][...] = av.astype(BF16)
            lhs[0] = refs[pos + 1]
        for n0 in range(0, pn, ns):
            sl = slice(n0, n0 + ns)
            acc = None
            for a_ref, b_ref in zip(lhs, b_refs):
                part = _dot(a_ref[...].astype(BF16), b_ref[:, sl])
                acc = part if acc is None else acc + part
            for add_ref in add_refs:
                acc = acc + add_ref[:, sl]
            if epi is not None:
                acc = epi(acc, extra_ref[:, sl]) if has_extra else epi(acc)
            o_ref[:, sl] = acc.astype(out_dtype)

    in_specs = [pl.BlockSpec((tm, a.shape[1]), lambda j, i: (i, 0)) for a, _ in pairs]
    in_specs += [pl.BlockSpec((b.shape[0], pn), lambda j, i: (0, j)) for _, b in pairs]
    args = [a for a, _ in pairs] + [b for _, b in pairs]
    tile = pl.BlockSpec((tm, pn), lambda j, i: (i, j))
    for extra_add in adds:
        in_specs.append(tile)
        args.append(extra_add)
    if has_extra:
        in_specs.append(tile)
        args.append(extra)
    outs, bg_outs = _run(
        body, args, bg, name=name, grid=(n // pn, m // tm), in_specs=in_specs, out_specs=[tile],
        out_shape=[jax.ShapeDtypeStruct((m, n), out_dtype)],
        scratch_shapes=[pltpu.VMEM((tm, pairs[0][0].shape[1]), BF16)] if stage0 else [])
    return outs[0] if bg is None else (outs[0], bg_outs)


MM_REDUCE_ROWS = 1024
MM_GRAD_ROWS = 1024
MM_GRAD_COLS = 2048


def mm_tn(a, b, *, name, a_fn=None):
    m, ka = a.shape
    nb = b.shape[1]
    pa = _pick(ka, (MM_GRAD_ROWS, 512, 256, 128))
    pb = nb if nb <= MM_GRAD_COLS else _pick(nb, (MM_GRAD_COLS, 1024, 512, 256, 128))
    ns = _pick(pb, (MM_SUB, 256, 128))
    tmk = _pick(m, (MM_REDUCE_ROWS, 512, 256, 128, 64, 32, 16))

    def body(a_ref, b_ref, o_ref, lhs):
        k = pl.program_id(2)

        @pl.when(k == 0)
        def _():
            o_ref[...] = jnp.zeros_like(o_ref)

        av = a_ref[...]
        if a_fn is not None:
            av = a_fn(av)
        lhs[...] = av.astype(BF16)
        for n0 in range(0, pb, ns):
            o_ref[:, n0:n0 + ns] += _dot_tn(lhs[...], b_ref[:, n0:n0 + ns].astype(BF16))

    return _pcall(
        body, name=name,
        grid=(ka // pa, nb // pb, m // tmk),
        in_specs=[pl.BlockSpec((tmk, pa), lambda i, j, k: (k, i)),
                  pl.BlockSpec((tmk, pb), lambda i, j, k: (k, j))],
        out_specs=pl.BlockSpec((pa, pb), lambda i, j, k: (i, j)),
        out_shape=jax.ShapeDtypeStruct((ka, nb), F32),
        scratch_shapes=[pltpu.VMEM((tmk, pa), BF16)],
    )(a, b)


def _relu_sq(v):
    r = jnp.maximum(v, 0.0)
    return r * r


ROW_TILE = 512


def _row_specs(bsz, seq, width, ts):
    return pl.BlockSpec((None, ts, width), lambda b, i: (b, i, 0))


def _vec_spec(width):
    return pl.BlockSpec((1, width), lambda b, i: (0, 0))


def _mod_spec():
    return pl.BlockSpec((None, SUBLANES, D_MODEL), lambda b, i: (b, 0, 0))


def _rstd(v):
    return lax.rsqrt(jnp.mean(v * v, axis=-1, keepdims=True) + EPS)


def prenorm(x, w, mod, *, name):
    bsz, seq, d = x.shape
    ts = _pick(seq, (ROW_TILE, 256, 128))

    def body(x_ref, w_ref, mod_ref, h_ref):
        xv = x_ref[...]
        m = mod_ref[...]
        xh = xv * _rstd(xv)
        h_ref[...] = ((xh * w_ref[...]) * (1.0 + m[1:2]) + m[0:1]).astype(BF16)

    return _pcall(
        body, name=name, grid=(bsz, seq // ts),
        in_specs=[_row_specs(bsz, seq, d, ts), _vec_spec(d), _mod_spec()],
        out_specs=_row_specs(bsz, seq, d, ts),
        out_shape=jax.ShapeDtypeStruct((bsz, seq, d), BF16),
    )(x, w, mod)


def post1_pre2(x, out1, mod, post1, pre2, *, name):
    bsz, seq, d = x.shape
    ts = _pick(seq, (ROW_TILE, 256, 128))

    def body(x_ref, o_ref, mod_ref, p1_ref, p2_ref, x1_ref, h2_ref):
        m = mod_ref[...]
        ov = o_ref[...]
        x1 = x_ref[...] + m[2:3] * ((ov * _rstd(ov)) * p1_ref[...])
        x1_ref[...] = x1
        xh = x1 * _rstd(x1)
        h2_ref[...] = ((xh * p2_ref[...]) * (1.0 + m[4:5]) + m[3:4]).astype(BF16)

    return _pcall(
        body, name=name, grid=(bsz, seq // ts),
        in_specs=[_row_specs(bsz, seq, d, ts), _row_specs(bsz, seq, d, ts), _mod_spec(), _vec_spec(d), _vec_spec(d)],
        out_specs=[_row_specs(bsz, seq, d, ts), _row_specs(bsz, seq, d, ts)],
        out_shape=[jax.ShapeDtypeStruct((bsz, seq, d), F32), jax.ShapeDtypeStruct((bsz, seq, d), BF16)],
    )(x, out1, mod, post1, pre2)


def _acc_specs(d):
    per_batch = pl.BlockSpec((None, SUBLANES, d), lambda b, i: (b, 0, 0))
    glob = pl.BlockSpec((SUBLANES, d), lambda b, i: (0, 0))
    return per_batch, glob


def _accumulate(pb_ref, gl_ref, pb_rows, gl_rows, width):
    b, i = pl.program_id(0), pl.program_id(1)

    @pl.when(i == 0)
    def _():
        pb_ref[...] = jnp.zeros_like(pb_ref)

    @pl.when((b == 0) & (i == 0))
    def _():
        gl_ref[...] = jnp.zeros_like(gl_ref)

    pb_ref[...] += _stack_rows(pb_rows, width)
    gl_ref[...] += _stack_rows(gl_rows, width)


def _rms_bwd(d_n, n, r):
    return r * (d_n - n * jnp.mean(d_n * n, axis=-1, keepdims=True))


def final_bwd(x1, y2, target, mod, post2, *, name):
    bsz, seq, d = x1.shape
    ts = _pick(seq, (ROW_TILE, 256, 128))

    def body(x1_ref, y_ref, t_ref, mod_ref, p_ref, dx_ref, dy_ref, pb_ref, gl_ref):
        m = mod_ref[...]
        g2 = m[5:6]
        yv = y_ref[...]
        r = _rstd(yv)
        n = yv * r
        o = n * p_ref[...]
        diff = (x1_ref[...] + g2 * o) - t_ref[...]
        dx = diff * (1.0 / d)
        dx_ref[...] = dx
        d_o = dx * g2
        dy_ref[...] = _rms_bwd(d_o * p_ref[...], n, r).astype(BF16)
        _accumulate(pb_ref, gl_ref, [_colsum(dx * o)], [_colsum(d_o * n), _colsum(diff * diff) * (0.5 / d)], d)

    pb, gl = _acc_specs(d)
    rs = _row_specs(bsz, seq, d, ts)
    return _pcall(
        body, name=name, grid=(bsz, seq // ts),
        in_specs=[rs, rs, rs, _mod_spec(), _vec_spec(d)],
        out_specs=[rs, rs, pb, gl],
        out_shape=[jax.ShapeDtypeStruct((bsz, seq, d), F32), jax.ShapeDtypeStruct((bsz, seq, d), BF16),
                   jax.ShapeDtypeStruct((bsz, SUBLANES, d), F32), jax.ShapeDtypeStruct((SUBLANES, d), F32)],
    )(x1, y2, target, mod, post2)


def mid_bwd(d_h2, dx2, x1, out1, mod, pre2, post1, *, name):
    bsz, seq, d = x1.shape
    ts = _pick(seq, (ROW_TILE, 256, 128))

    def body(dh_ref, dx2_ref, x1_ref, o_ref, mod_ref, p2_ref, p1_ref, dx1_ref, do_ref, pb_ref, gl_ref):
        m = mod_ref[...]
        dh = dh_ref[...]
        x1 = x1_ref[...]
        r2 = _rstd(x1)
        xh = x1 * r2
        xw = xh * p2_ref[...]
        d_xw = dh * (1.0 + m[4:5])
        dx1 = dx2_ref[...] + _rms_bwd(d_xw * p2_ref[...], xh, r2)
        dx1_ref[...] = dx1
        ov = o_ref[...]
        r1 = _rstd(ov)
        n1 = ov * r1
        o1 = n1 * p1_ref[...]
        d_o1 = dx1 * m[2:3]
        do_ref[...] = _rms_bwd(d_o1 * p1_ref[...], n1, r1).astype(BF16)
        _accumulate(pb_ref, gl_ref, [_colsum(dh), _colsum(dh * xw), _colsum(dx1 * o1)],
                    [_colsum(d_xw * xh), _colsum(d_o1 * n1)], d)

    pb, gl = _acc_specs(d)
    rs = _row_specs(bsz, seq, d, ts)
    return _pcall(
        body, name=name, grid=(bsz, seq // ts),
        in_specs=[rs, rs, rs, rs, _mod_spec(), _vec_spec(d), _vec_spec(d)],
        out_specs=[rs, rs, pb, gl],
        out_shape=[jax.ShapeDtypeStruct((bsz, seq, d), F32), jax.ShapeDtypeStruct((bsz, seq, d), BF16),
                   jax.ShapeDtypeStruct((bsz, SUBLANES, d), F32), jax.ShapeDtypeStruct((SUBLANES, d), F32)],
    )(d_h2, dx2, x1, out1, mod, pre2, post1)


def first_bwd(d_h1, dx1, x, mod, pre1, *, name):
    bsz, seq, d = x.shape
    ts = _pick(seq, (ROW_TILE, 256, 128))

    def body(dh_ref, dx1_ref, x_ref, mod_ref, p_ref, gx_ref, pb_ref, gl_ref):
        m = mod_ref[...]
        dh = dh_ref[...]
        xv = x_ref[...]
        r = _rstd(xv)
        xh = xv * r
        xw = xh * p_ref[...]
        d_xw = dh * (1.0 + m[1:2])
        gx_ref[...] = dx1_ref[...] + _rms_bwd(d_xw * p_ref[...], xh, r)
        _accumulate(pb_ref, gl_ref, [_colsum(dh), _colsum(dh * xw)], [_colsum(d_xw * xh)], d)

    pb, gl = _acc_specs(d)
    rs = _row_specs(bsz, seq, d, ts)
    return _pcall(
        body, name=name, grid=(bsz, seq // ts),
        in_specs=[rs, rs, rs, _mod_spec(), _vec_spec(d)],
        out_specs=[rs, pb, gl],
        out_shape=[jax.ShapeDtypeStruct((bsz, seq, d), F32),
                   jax.ShapeDtypeStruct((bsz, SUBLANES, d), F32), jax.ShapeDtypeStruct((SUBLANES, d), F32)],
    )(d_h1, dx1, x, mod, pre1)


def merge_bwd(d_merged, ya, yb, gates, b_gate, *, name):
    bsz, seq, d = ya.shape
    ts = _pick(seq, (ROW_TILE, 256, 128))

    def body(dm_ref, ya_ref, yb_ref, g_ref, b_ref, dya_ref, dyb_ref, dg_ref, gl_ref):
        b, i = pl.program_id(0), pl.program_id(1)
        g = _sigmoid(g_ref[...] + b_ref[...])
        dm = dm_ref[...]
        ga, gb = g[:, :d], g[:, d:]
        dya_ref[...] = (dm * ga).astype(BF16)
        dyb_ref[...] = (dm * gb).astype(BF16)
        dg = jnp.concatenate([dm * ya_ref[...] * ga * (1.0 - ga), dm * yb_ref[...] * gb * (1.0 - gb)], axis=1)
        dg_ref[...] = dg.astype(BF16)

        @pl.when((b == 0) & (i == 0))
        def _():
            gl_ref[...] = jnp.zeros_like(gl_ref)

        gl_ref[...] += _stack_rows([_colsum(dg)], 2 * d)

    rs = _row_specs(bsz, seq, d, ts)
    rs2 = _row_specs(bsz, seq, 2 * d, ts)
    return _pcall(
        body, name=name, grid=(bsz, seq // ts),
        in_specs=[rs, rs, rs, rs2, _vec_spec(2 * d)],
        out_specs=[rs, rs, rs2, pl.BlockSpec((SUBLANES, 2 * d), lambda b, i: (0, 0))],
        out_shape=[jax.ShapeDtypeStruct((bsz, seq, d), BF16), jax.ShapeDtypeStruct((bsz, seq, d), BF16),
                   jax.ShapeDtypeStruct((bsz, seq, 2 * d), BF16), jax.ShapeDtypeStruct((SUBLANES, 2 * d), F32)],
    )(d_merged, ya, yb, gates, b_gate)


LRU_TILE = 256
N_LRU_BLOCKS = D_MODEL // LANES


def _block_mm(v, w_ref, transpose=False):
    vb = v.astype(BF16)
    outs = []
    for j in range(N_LRU_BLOCKS):
        blk = vb[:, LANES * j:LANES * (j + 1)]
        outs.append(_dot_nt(blk, w_ref[j]) if transpose else _dot(blk, w_ref[j]))
    return jnp.concatenate(outs, axis=1)


def _lru_gates(xc, wa_ref, ba, wx_ref, bx, sp):
    r = _sigmoid(_block_mm(xc, wa_ref) + ba)
    i = _sigmoid(_block_mm(xc, wx_ref) + bx)
    la = (-LRU_C * r) * sp
    a = jnp.exp(la)
    sq = jnp.sqrt(_neg_expm1(2.0 * la))
    return r, i, a, sq


def _prev8_spec(width, col_block, tile_rows):
    per = tile_rows // SUBLANES
    return pl.BlockSpec((None, SUBLANES, width), lambda b, i: (b, jnp.maximum(i * per - 1, 0), col_block))


def lru_fwd(pm, cw, cb, wa, ba, wx, bx, lam, w_pa, *, name):
    bsz, seq, _ = pm.shape
    d = D_MODEL
    ts = _pick(seq, (LRU_TILE, 128))

    def body(lx_ref, lxp_ref, lg_ref, cw_ref, cb_ref, wa_ref, ba_ref, wx_ref, bx_ref, lam_ref, wpa_ref,
             h_ref, pa_ref, ya_ref, kept_ref, hc, a_s, u_s):
        i = pl.program_id(1)

        @pl.when(i == 0)
        def _():
            hc[...] = jnp.zeros_like(hc)

        lx = lx_ref[...]
        prev8 = jnp.where(i == 0, 0.0, lxp_ref[...])
        xc = _conv4(lx, prev8, cw_ref[...], cb_ref[...])
        sp = _softplus(-lam_ref[...])
        r, ig, a, sq = _lru_gates(xc, wa_ref, ba_ref[...], wx_ref, bx_ref[...], sp)
        for k, kept in enumerate((xc, r, ig, a, sq)):
            kept_ref[:, k * d:(k + 1) * d] = kept
        a_s[...] = a
        u_s[...] = sq * (ig * xc)

        def step(g, h):
            r0 = pl.multiple_of(g * SUBLANES, SUBLANES)
            a8 = a_s[pl.ds(r0, SUBLANES), :]
            u8 = u_s[pl.ds(r0, SUBLANES), :]
            rows = []
            for j in range(SUBLANES):
                h = a8[j:j + 1] * h + u8[j:j + 1]
                rows.append(h)
            h_ref[pl.ds(r0, SUBLANES), :] = jnp.concatenate(rows, axis=0)
            return h

        hc[...] = lax.fori_loop(0, ts // SUBLANES, step, hc[...])
        pa_ref[...] = (h_ref[...] * _gelu(lg_ref[...])).astype(BF16)
        ya_ref[...] = _dot(pa_ref[...], wpa_ref[...])

    vec = _vec_spec(d)
    wspec = pl.BlockSpec((N_LRU_BLOCKS, LANES, LANES), lambda b, i: (0, 0, 0))
    rs = _row_specs(bsz, seq, d, ts)
    return _pcall(
        body, name=name, grid=(bsz, seq // ts),
        in_specs=[pl.BlockSpec((None, ts, d), lambda b, i: (b, i, 0)), _prev8_spec(d, 0, ts),
                  pl.BlockSpec((None, ts, d), lambda b, i: (b, i, 1)),
                  pl.BlockSpec((4, d), lambda b, i: (0, 0)), vec, wspec, vec, wspec, vec, vec,
                  pl.BlockSpec(w_pa.shape, lambda b, i: (0, 0))],
        out_specs=[rs, rs, rs, _row_specs(bsz, seq, 5 * d, ts)],
        out_shape=[jax.ShapeDtypeStruct((bsz, seq, d), F32), jax.ShapeDtypeStruct((bsz, seq, d), BF16),
                   jax.ShapeDtypeStruct((bsz, seq, d), F32), jax.ShapeDtypeStruct((bsz, seq, 5 * d), F32)],
        scratch_shapes=[pltpu.VMEM((1, d), F32), pltpu.VMEM((ts, d), F32), pltpu.VMEM((ts, d), F32)],
    )(pm, pm, pm, cw, cb, wa, ba, wx, bx, lam, w_pa)


def lru_bwd(pm, h, kept, d_ya, cw, wa, wx, lam, wt_pa, wt_lru, *, name, bg=None):
    bsz, seq, _ = pm.shape
    d = D_MODEL
    ts = _pick(seq, (LRU_TILE, 128))
    nt = seq // ts
    per = ts // SUBLANES

    def rev(i):
        return nt - 1 - i

    def body(lx_ref, lg_ref, h_ref, hp_ref, kept_ref, dya_ref, cw_ref, wa_ref, wx_ref,
             lam_ref, wtpa_ref, wtl_ref, dl_ref, dh1_ref, dwa_ref, dwx_ref, rows_ref,
             carry, dxc_next, a_s, dh_s, acc_s):
        b, i = pl.program_id(0), pl.program_id(1)
        t = rev(i)

        @pl.when(i == 0)
        def _():
            carry[...] = jnp.zeros_like(carry)
            dxc_next[...] = jnp.zeros_like(dxc_next)

        @pl.when((b == 0) & (i == 0))
        def _():
            dwa_ref[...] = jnp.zeros_like(dwa_ref)
            dwx_ref[...] = jnp.zeros_like(dwx_ref)
            rows_ref[...] = jnp.zeros_like(rows_ref)

        lx = lx_ref[...]
        lg = lg_ref[...]
        cwv = cw_ref[...]
        lam_v = lam_ref[...]
        sp = _softplus(-lam_v)
        xc, r, ig, a, sq = (kept_ref[:, k * d:(k + 1) * d] for k in range(5))
        hv = h_ref[...]
        d_pa = _dot(dya_ref[...], wtpa_ref[...])
        a_s[...] = a
        dh_s[...] = d_pa * _gelu(lg)

        def step(g, c):
            r0 = pl.multiple_of((per - 1 - g) * SUBLANES, SUBLANES)
            a8 = a_s[pl.ds(r0, SUBLANES), :]
            d8 = dh_s[pl.ds(r0, SUBLANES), :]
            rows = [None] * SUBLANES
            for j in range(SUBLANES - 1, -1, -1):
                acc = d8[j:j + 1] + c
                rows[j] = acc
                c = a8[j:j + 1] * acc
            acc_s[pl.ds(r0, SUBLANES), :] = jnp.concatenate(rows, axis=0)
            return c

        carry[...] = lax.fori_loop(0, per, step, carry[...])
        d_u = acc_s[...]
        hprev8 = jnp.where(t == 0, 0.0, hp_ref[...])
        d_a = d_u * _shift_down(hv, hprev8, 1)
        d_sq = d_u * (ig * xc)
        d_i = d_u * (sq * xc)
        d_xc = d_u * (sq * ig)
        d_la = d_a * a - d_sq * (a * a) / sq
        d_pre_r = (d_la * (-LRU_C * sp)) * (r * (1.0 - r))
        d_pre_i = d_i * (ig * (1.0 - ig))
        d_xc = d_xc + _block_mm(d_pre_r, wa_ref, transpose=True) + _block_mm(d_pre_i, wx_ref, transpose=True)
        xcb = xc.astype(BF16)
        drb = d_pre_r.astype(BF16)
        dib = d_pre_i.astype(BF16)
        for j in range(N_LRU_BLOCKS):
            sl = slice(LANES * j, LANES * (j + 1))
            dwa_ref[j] += _dot_tn(xcb[:, sl], drb[:, sl])
            dwx_ref[j] += _dot_tn(xcb[:, sl], dib[:, sl])
        d_lx, d_cw, d_cb = _conv4_bwd(d_xc, dxc_next[...], lx, cwv)
        dxc_next[...] = d_xc[0:SUBLANES]
        d_lam = _colsum(d_la * (-LRU_C * r)) * (-_sigmoid(-lam_v))
        rows_ref[...] += _stack_rows([_colsum(d_pre_r), _colsum(d_pre_i), d_lam, d_cb] + d_cw, d)
        dl_ref[:, :d] = d_lx.astype(BF16)
        dl_ref[:, d:] = (d_pa * hv * _gelu_grad(lg)).astype(BF16)
        dh1_ref[...] = _dot(dl_ref[...], wtl_ref[...])

    vec = _vec_spec(d)
    wspec = pl.BlockSpec((N_LRU_BLOCKS, LANES, LANES), lambda b, i: (0, 0, 0))
    tile = lambda col: pl.BlockSpec((None, ts, d), lambda b, i: (b, rev(i), col))
    prev8 = lambda col: pl.BlockSpec((None, SUBLANES, d), lambda b, i: (b, jnp.maximum(rev(i) * per - 1, 0), col))
    whole = lambda v: pl.BlockSpec(v.shape, lambda b, i: (0, 0))
    return _run(
        body, (pm, pm, h, h, kept, d_ya, cw, wa, wx, lam, wt_pa, wt_lru), bg, name=name, grid=(bsz, nt),
        in_specs=[tile(0), tile(1), tile(0), prev8(0), pl.BlockSpec((None, ts, 5 * d), lambda b, i: (b, rev(i), 0)),
                  tile(0), pl.BlockSpec((4, d), lambda b, i: (0, 0)), wspec, wspec, vec,
                  whole(wt_pa), whole(wt_lru)],
        out_specs=[pl.BlockSpec((None, ts, 2 * d), lambda b, i: (b, rev(i), 0)), tile(0), wspec, wspec,
                   pl.BlockSpec((SUBLANES, d), lambda b, i: (0, 0))],
        out_shape=[jax.ShapeDtypeStruct((bsz, seq, 2 * d), BF16), jax.ShapeDtypeStruct((bsz, seq, d), F32),
                   jax.ShapeDtypeStruct((N_LRU_BLOCKS, LANES, LANES), F32),
                   jax.ShapeDtypeStruct((N_LRU_BLOCKS, LANES, LANES), F32),
                   jax.ShapeDtypeStruct((SUBLANES, d), F32)],
        scratch_shapes=[pltpu.VMEM((1, d), F32), pltpu.VMEM((SUBLANES, d), F32),
                        pltpu.VMEM((ts, d), F32), pltpu.VMEM((ts, d), F32), pltpu.VMEM((ts, d), F32)])


L = SSD_CHUNK
N_PAIRS = SSD_HEADS // 2


def _ssd_common(conv, dt_raw, dtb, alog):
    sg = _sigmoid(conv)
    xa = conv * sg
    dtv = _softplus(dt_raw + dtb)
    a_neg = -jnp.exp(alog)
    rowi = lax.broadcasted_iota(jnp.int32, (L, L), 0)
    coli = lax.broadcasted_iota(jnp.int32, (L, L), 1)
    tril = (rowi >= coli).astype(F32)
    cs = _dot(tril, dtv * a_neg, precision=HI)
    return conv, sg, xa, dtv, a_neg, cs, rowi, coli


def _head_masks():
    lane = lax.broadcasted_iota(jnp.int32, (L, LANES), 1)
    return lane < SSD_HEAD_DIM


def _spread(v, p, first):
    return jnp.where(first[:v.shape[0]], v[:, 2 * p:2 * p + 1], v[:, 2 * p + 1:2 * p + 2])


def _place_head_sums(acc, z, p, first, lane1):
    rows = z.shape[0]
    s0 = jnp.sum(jnp.where(first[:rows], z, 0.0), axis=1, keepdims=True)
    s1 = jnp.sum(jnp.where(first[:rows], 0.0, z), axis=1, keepdims=True)
    lane = lane1[:rows]
    return acc + jnp.where(lane == 2 * p, s0, 0.0) + jnp.where(lane == 2 * p + 1, s1, 0.0)


def _stack_heads(v, first):
    return jnp.concatenate([jnp.where(first, v, 0.0), jnp.where(first, 0.0, v)], axis=0).astype(BF16)


def ssd_fwd(pm, dtr, cw, cb, dtb, alog, d_lanes, nw, w_pb, ya, gates, b_gate, w_out, *, name, bg=None):
    bsz, seq, _ = pm.shape
    nc = seq // L
    inner, cdim, d = SSD_INNER, SSD_CONV_DIM, D_MODEL

    def body(xbc_ref, xp_ref, z_ref, dt_ref, cw_ref, cb_ref, dtb_ref, alog_ref, dl_ref, nw_ref, wpb_ref,
             ya_ref, g_ref, bg_ref, wout_ref,
             y_ref, yn_ref, st_ref, yb_ref, conv_ref, mg_ref, out_ref, state):
        i = pl.program_id(1)

        @pl.when(i == 0)
        def _():
            state[...] = jnp.zeros_like(state)

        prev8 = jnp.where(i == 0, 0.0, xp_ref[...])
        conv = _conv4(xbc_ref[...], prev8, cw_ref[...], cb_ref[...])
        conv_ref[...] = conv
        _, _, xa, dtv, _, cs, rowi, coli = _ssd_common(conv, dt_ref[...], dtb_ref[...], alog_ref[...])
        cst = cs.T
        causal = rowi >= coli
        first = _head_masks()
        for g in range(SSD_GROUPS):
            bg = xa[:, inner + SSD_STATE * g:inner + SSD_STATE * (g + 1)].astype(BF16)
            cg = xa[:, inner + SSD_GROUPS * SSD_STATE + SSD_STATE * g:
                    inner + SSD_GROUPS * SSD_STATE + SSD_STATE * (g + 1)].astype(BF16)
            cbm = _dot_nt(cg, bg)
            for pp in range(2):
                p = 2 * g + pp
                sl = slice(LANES * p, LANES * (p + 1))
                ms = []
                for hh in (2 * p, 2 * p + 1):
                    seg = cs[:, hh:hh + 1] - cst[hh:hh + 1, :]
                    ms.append((cbm * jnp.exp(jnp.where(causal, seg, -jnp.inf))).astype(BF16))
                xsp = xa[:, sl]
                cs_p = _spread(cs, p, first)
                cs_last = cs_p[L - 1:L]
                xp = xsp * _spread(dtv, p, first)
                y_diag = _dot(jnp.concatenate(ms, axis=1), _stack_heads(xp, first))
                st = state[p]
                st_ref[p] = st
                y_off = _dot(cg, st.astype(BF16)) * jnp.exp(cs_p)
                y_ref[:, sl] = y_diag + y_off + dl_ref[:, sl] * xsp
                state[p] = st * jnp.exp(cs_last) + _dot_tn(bg, (xp * jnp.exp(cs_last - cs_p)).astype(BF16))
        zv = z_ref[...]
        yz = y_ref[...] * (zv * _sigmoid(zv))
        gw = inner // SSD_GROUPS
        for g in range(SSD_GROUPS):
            sl = slice(gw * g, gw * (g + 1))
            seg = yz[:, sl]
            yn_ref[:, sl] = ((seg * _rstd(seg)) * nw_ref[:, sl]).astype(BF16)
        yb = _dot(yn_ref[...], wpb_ref[...])
        yb_ref[...] = yb
        g = _sigmoid(g_ref[...] + bg_ref[...])
        mg_ref[...] = (g[:, :d] * ya_ref[...] + g[:, d:] * yb).astype(BF16)
        out_ref[...] = _dot(mg_ref[...], wout_ref[...])

    cvec = lambda w: pl.BlockSpec((1, w), lambda b, i: (0, 0))
    rows = lambda w: pl.BlockSpec((None, L, w), lambda b, i: (b, i, 0))
    outs, bg_outs = _run(
        body, (pm, pm, pm, dtr, cw, cb, dtb, alog, d_lanes, nw, w_pb, ya, gates, b_gate, w_out), bg, name=name,
        grid=(bsz, nc),
        in_specs=[pl.BlockSpec((None, L, cdim), lambda b, i: (b, i, 1)), _prev8_spec(cdim, 1, L),
                  pl.BlockSpec((None, L, inner), lambda b, i: (b, i, 1)),
                  pl.BlockSpec((None, L, LANES), lambda b, i: (b, i, 0)),
                  pl.BlockSpec((4, cdim), lambda b, i: (0, 0)), cvec(cdim), cvec(LANES), cvec(LANES),
                  cvec(inner), cvec(inner), pl.BlockSpec(w_pb.shape, lambda b, i: (0, 0)),
                  rows(d), rows(2 * d), cvec(2 * d), pl.BlockSpec(w_out.shape, lambda b, i: (0, 0))],
        out_specs=[rows(inner), rows(inner),
                   pl.BlockSpec((None, None, N_PAIRS, SSD_STATE, LANES), lambda b, i: (b, i, 0, 0, 0)),
                   rows(d), rows(cdim), rows(d), rows(d)],
        out_shape=[jax.ShapeDtypeStruct((bsz, seq, inner), F32), jax.ShapeDtypeStruct((bsz, seq, inner), BF16),
                   jax.ShapeDtypeStruct((bsz, nc, N_PAIRS, SSD_STATE, LANES), F32),
                   jax.ShapeDtypeStruct((bsz, seq, d), F32), jax.ShapeDtypeStruct((bsz, seq, cdim), F32),
                   jax.ShapeDtypeStruct((bsz, seq, d), BF16), jax.ShapeDtypeStruct((bsz, seq, d), F32)],
        scratch_shapes=[pltpu.VMEM((N_PAIRS, SSD_STATE, LANES), F32)])
    return outs, bg_outs


def ssd_bwd(pm, conv, dtr, y, states, d_yb, cw, dtb, alog, d_lanes, nw, wt_pb, wt_ssd, *, name):
    bsz, seq, _ = pm.shape
    nc = seq // L
    inner, cdim = SSD_INNER, SSD_CONV_DIM
    per = L // SUBLANES

    def rev(i):
        return nc - 1 - i

    def body(xbc_ref, conv_ref, z_ref, dt_ref, y_ref, st_ref, dyb_ref, cw_ref, dtb_ref, alog_ref,
             dl_ref, nw_ref, wtpb_ref, wts_ref, ds_ref, dh1_ref, ddt_ref, r4_ref, r2_ref, r1_ref,
             dstate, dconv_next, dxs_s, dbc_s):
        b, i = pl.program_id(0), pl.program_id(1)
        t = rev(i)

        @pl.when(i == 0)
        def _():
            dstate[...] = jnp.zeros_like(dstate)
            dconv_next[...] = jnp.zeros_like(dconv_next)

        @pl.when((b == 0) & (i == 0))
        def _():
            r4_ref[...] = jnp.zeros_like(r4_ref)
            r2_ref[...] = jnp.zeros_like(r2_ref)
            r1_ref[...] = jnp.zeros_like(r1_ref)

        xbc = xbc_ref[...]
        cwv = cw_ref[...]
        dt_in = dt_ref[...] + dtb_ref[...]
        conv = conv_ref[...]
        _, sg, xa, dtv, a_neg, cs, rowi, coli = _ssd_common(conv, dt_ref[...], dtb_ref[...], alog_ref[...])
        cst = cs.T
        causal = rowi >= coli
        anti = coli >= rowi
        first = _head_masks()
        lane1 = lax.broadcasted_iota(jnp.int32, (L, LANES), 1)

        yv = y_ref[...]
        zv = z_ref[...]
        sz = _sigmoid(zv)
        zs = zv * sz
        yz = yv * zs
        dyn = _dot(dyb_ref[...], wtpb_ref[...])
        gw = inner // SSD_GROUPS
        d_yz_parts, d_nw_parts = [], []
        for g in range(SSD_GROUPS):
            sl = slice(gw * g, gw * (g + 1))
            seg = yz[:, sl]
            r = _rstd(seg)
            n = seg * r
            d_nw_parts.append(_colsum(dyn[:, sl] * n))
            d_yz_parts.append(_rms_bwd(dyn[:, sl] * nw_ref[:, sl], n, r))
        d_yz = jnp.concatenate(d_yz_parts, axis=1)
        d_y = d_yz * zs
        ds_ref[:, :inner] = (d_yz * yv * (sz * (1.0 + zv * (1.0 - sz)))).astype(BF16)

        a1 = jnp.zeros((L, LANES), F32)
        a2 = jnp.zeros((L, LANES), F32)
        xs_dxt = jnp.zeros((L, LANES), F32)
        c0 = jnp.zeros((1, LANES), F32)
        d_dl = jnp.zeros((1, LANES), F32)
        for g in range(SSD_GROUPS):
            bsl = slice(inner + SSD_STATE * g, inner + SSD_STATE * (g + 1))
            csl = slice(inner + SSD_GROUPS * SSD_STATE + SSD_STATE * g,
                        inner + SSD_GROUPS * SSD_STATE + SSD_STATE * (g + 1))
            bg = xa[:, bsl].astype(BF16)
            cg = xa[:, csl].astype(BF16)
            cbm = _dot_nt(cg, bg)
            cbt = _dot_nt(bg, cg)
            d_cb = jnp.zeros((L, L), F32)
            d_bg = jnp.zeros((L, SSD_STATE), F32)
            d_cg = jnp.zeros((L, SSD_STATE), F32)
            for pp in range(2):
                p = 2 * g + pp
                sl = slice(LANES * p, LANES * (p + 1))
                xsp = xa[:, sl]
                dt_p = _spread(dtv, p, first)
                cs_p = _spread(cs, p, first)
                cs_last = cs_p[L - 1:L]
                e_p = jnp.exp(cs_p)
                w_p = jnp.exp(cs_last - cs_p)
                e_last = jnp.exp(cs_last)
                xp = xsp * dt_p
                xpb = xp.astype(BF16)
                dyp = d_y[:, sl]
                dypb = dyp.astype(BF16)
                dy_heads = (jnp.where(first, dyp, 0.0).astype(BF16), jnp.where(first, 0.0, dyp).astype(BF16))
                x_heads = (jnp.where(first, xp, 0.0).astype(BF16), jnp.where(first, 0.0, xp).astype(BF16))
                mts = []
                for k, hh in enumerate((2 * p, 2 * p + 1)):
                    col = cs[:, hh:hh + 1]
                    row = cst[hh:hh + 1, :]
                    dec = jnp.exp(jnp.where(causal, col - row, -jnp.inf))
                    dec_t = jnp.exp(jnp.where(anti, row - col, -jnp.inf))
                    gd = _dot_nt(dy_heads[k], xpb) * dec
                    d_cb = d_cb + gd
                    mt = cbt * dec_t
                    qd = gd * cbm - _dot_nt(x_heads[k], dypb) * mt
                    a1 = a1 + jnp.where(lane1 == hh, jnp.sum(qd, axis=1, keepdims=True), 0.0)
                    mts.append(mt.astype(BF16))
                dst = dstate[p]
                dstb = dst.astype(BF16)
                st = st_ref[p]
                stb = st.astype(BF16)
                dye = (dyp * e_p).astype(BF16)
                xw = (xp * w_p).astype(BF16)
                dx_off = w_p * _dot(bg, dstb)
                d_xp = _dot(jnp.concatenate(mts, axis=1), jnp.concatenate(dy_heads, axis=0)) + dx_off
                dxs_s[:, sl] = d_xp * dt_p + dyp * dl_ref[:, sl]
                a1 = _place_head_sums(a1, dyp * (_dot(cg, stb) * e_p), p, first, lane1)
                a2 = _place_head_sums(a2, xp * dx_off, p, first, lane1)
                xs_dxt = _place_head_sums(xs_dxt, d_xp * xsp, p, first, lane1)
                c0 = _place_head_sums(c0, _colsum(dst * st) * e_last, p, first, lane1)
                d_dl = _place_head_sums(d_dl, _colsum(dyp * xsp), p, first, lane1)
                d_cg = d_cg + _dot_nt(dye, stb)
                d_bg = d_bg + _dot_nt(xw, dstb)
                dstate[p] = dst * e_last + _dot_tn(cg, dye)
            d_cbb = d_cb.astype(BF16)
            dbc_s[:, SSD_STATE * g:SSD_STATE * (g + 1)] = d_bg + _dot_tn(d_cbb, cg)
            dbc_s[:, SSD_GROUPS * SSD_STATE + SSD_STATE * g:SSD_GROUPS * SSD_STATE + SSD_STATE * (g + 1)] = (
                d_cg + _dot(d_cbb, bg))

        d_da = (_dot(anti.astype(F32), a1, precision=HI) + _dot((rowi > coli).astype(F32), a2, precision=HI) + c0)
        d_dt = d_da * a_neg + xs_dxt
        d_alog = _colsum(d_da * dtv) * a_neg
        d_dtr = jnp.where(lane1 < SSD_HEADS, d_dt * _sigmoid(dt_in), 0.0)
        ddt_ref[...] = d_dtr.astype(BF16)
        d_xa = jnp.concatenate([dxs_s[...], dbc_s[...]], axis=1)
        d_conv = d_xa * (sg * (1.0 + conv * (1.0 - sg)))
        d_xbc, d_cw, d_cbias = _conv4_bwd(d_conv, dconv_next[...], xbc, cwv)
        dconv_next[...] = d_conv[0:SUBLANES]
        ds_ref[:, inner:] = d_xbc.astype(BF16)
        dh1_ref[...] = _dot(ds_ref[...], wts_ref[...])
        r4_ref[...] += _stack_rows([d_cbias] + d_cw, cdim)
        r2_ref[...] += _stack_rows([jnp.concatenate(d_nw_parts, axis=1)], inner)
        r1_ref[...] += _stack_rows([_colsum(d_dtr), d_alog, d_dl], LANES)

    cvec = lambda w: pl.BlockSpec((1, w), lambda b, i: (0, 0))
    return _pcall(
        body, name=name, grid=(bsz, nc),
        in_specs=[pl.BlockSpec((None, L, cdim), lambda b, i: (b, rev(i), 1)),
                  pl.BlockSpec((None, L, cdim), lambda b, i: (b, rev(i), 0)),
                  pl.BlockSpec((None, L, inner), lambda b, i: (b, rev(i), 1)),
                  pl.BlockSpec((None, L, LANES), lambda b, i: (b, rev(i), 0)),
                  pl.BlockSpec((None, L, inner), lambda b, i: (b, rev(i), 0)),
                  pl.BlockSpec((None, None, N_PAIRS, SSD_STATE, LANES), lambda b, i: (b, rev(i), 0, 0, 0)),
                  pl.BlockSpec((None, L, D_MODEL), lambda b, i: (b, rev(i), 0)),
                  pl.BlockSpec((4, cdim), lambda b, i: (0, 0)), cvec(LANES), cvec(LANES),
                  cvec(inner), cvec(inner), pl.BlockSpec(wt_pb.shape, lambda b, i: (0, 0)),
                  pl.BlockSpec(wt_ssd.shape, lambda b, i: (0, 0))],
        out_specs=[pl.BlockSpec((None, L, inner + cdim), lambda b, i: (b, rev(i), 0)),
                   pl.BlockSpec((None, L, D_MODEL), lambda b, i: (b, rev(i), 0)),
                   pl.BlockSpec((None, L, LANES), lambda b, i: (b, rev(i), 0)),
                   pl.BlockSpec((SUBLANES, cdim), lambda b, i: (0, 0)),
                   pl.BlockSpec((SUBLANES, inner), lambda b, i: (0, 0)),
                   pl.BlockSpec((SUBLANES, LANES), lambda b, i: (0, 0))],
        out_shape=[jax.ShapeDtypeStruct((bsz, seq, inner + cdim), BF16),
                   jax.ShapeDtypeStruct((bsz, seq, D_MODEL), F32),
                   jax.ShapeDtypeStruct((bsz, seq, LANES), BF16),
                   jax.ShapeDtypeStruct((SUBLANES, cdim), F32),
                   jax.ShapeDtypeStruct((SUBLANES, inner), F32),
                   jax.ShapeDtypeStruct((SUBLANES, LANES), F32)],
        scratch_shapes=[pltpu.VMEM((N_PAIRS, SSD_STATE, LANES), F32), pltpu.VMEM((SUBLANES, cdim), F32),
                        pltpu.VMEM((L, inner), F32), pltpu.VMEM((L, 2 * SSD_GROUPS * SSD_STATE), F32)],
    )(pm, conv, pm, dtr, y, states, d_yb, cw, dtb, alog, d_lanes, nw, wt_pb, wt_ssd)


def _lru_block_weights(w):
    w = w.reshape(N_LRU_BLOCKS, 2, LRU_HEAD_DIM, LRU_HEAD_DIM)
    z = jnp.zeros((N_LRU_BLOCKS, LRU_HEAD_DIM, LRU_HEAD_DIM), w.dtype)
    top = jnp.concatenate([w[:, 0], z], axis=2)
    bot = jnp.concatenate([z, w[:, 1]], axis=2)
    return jnp.concatenate([top, bot], axis=1).astype(BF16)


def _lru_block_grads(g):
    h = LRU_HEAD_DIM
    return jnp.stack([g[:, :h, :h], g[:, h:, h:]], axis=1).reshape(LRU_HEADS, h, h)


def _pad_lanes(v, width=LANES):
    return jnp.pad(v, ((0, 0), (0, width - v.shape[1])))


class NoExchange:
    def __init__(self, weights):
        self._weights, self.grads = weights, {}

    def weights_bg(self):
        return None

    def weights(self, bg_outs):
        return self._weights

    def grads_bg(self, grads):
        self.grads.update(grads)
        return None

    def grads_done(self, bg_outs):
        pass


def local_step(x, target, mod, big, small, plan):
    bsz, seq, d = x.shape
    t = bsz * seq
    flat = lambda v: v.reshape(t, v.shape[-1])
    unflat = lambda v: v.reshape(bsz, seq, v.shape[-1])

    wa_b = _lru_block_weights(small["lru_wa"])
    wx_b = _lru_block_weights(small["lru_wx"])
    dtb = _pad_lanes(small["ssd_dt_bias"])
    alog = _pad_lanes(small["ssd_a_log"])
    d_lanes = jnp.repeat(small["ssd_d"], SSD_HEAD_DIM, axis=1)

    lru_cols = 2 * D_MODEL
    wt = {"lru": big["w_main"][:, :lru_cols].T, "ssd": big["w_main"][:, lru_cols:].T, "gates": big["w_gates"].T,
          "dt": big["w_dt"].T}

    h1 = prenorm(x, small["pre_norm1"], mod, name="prenorm1")
    h1f = flat(h1)
    arriving = plan.weights_bg()
    if arriving is None:
        pm, arrived = mm_nn([(h1f, big["w_main"])], name="in_proj_main"), []
    else:
        pm, arrived = mm_nn([(h1f, big["w_main"])], name="in_proj_main", bg=arriving)
    pm = unflat(pm)
    big = dict(big, **plan.weights(arrived))
    for n in ("w_pa", "w_pb", "w_out", "w_ff1", "w_ff2"):
        wt[n] = big[n].T
    gates = unflat(mm_nn([(h1f, big["w_gates"])], name="in_proj_gates"))
    dtr = unflat(mm_nn([(h1f, big["w_dt"])], name="in_proj_dt"))
    lru_args = (small["lru_conv_w"], small["lru_conv_b"], wa_b, small["lru_ba"], wx_b, small["lru_bx"],
                small["lru_lambda"])
    h_lru, pa_in, ya, lru_kept = lru_fwd(pm, *lru_args, big["w_pa"], name="lru_fwd")
    ssd_args = (small["ssd_conv_w"], small["ssd_conv_b"], dtb, alog, d_lanes, small["ssd_norm_w"])
    (y_ssd, ynorm, states, yb, conv_ssd, merged, out1), _ = ssd_fwd(
        pm, dtr, *ssd_args, big["w_pb"], ya, gates, small["b_gate"], big["w_out"], name="ssd_fwd")
    x1, h2 = post1_pre2(x, out1, mod, small["post_norm1"], small["pre_norm2"], name="post1_pre2")
    f = mm_nn([(flat(h2), big["w_ff1"])], name="ff1")
    y2 = unflat(mm_nn([(f, big["w_ff2"])], a_fn=_relu_sq, name="ff2"))

    dx2, d_y2, pb_a, gl_a = final_bwd(x1, y2, target, mod, small["post_norm2"], name="final_bwd")
    d_y2f = flat(d_y2)
    d_f = mm_nn([(d_y2f, wt["w_ff2"])], out_dtype=BF16, extra=f,
                epi=lambda r, fv: r * (2.0 * jnp.maximum(fv, 0.0)), name="ff2_dx")
    g_ff2 = mm_tn(f, d_y2f, a_fn=_relu_sq, name="ff2_dw")
    d_h2 = unflat(mm_nn([(d_f, wt["w_ff1"])], name="ff1_dx"))
    g_ff1 = mm_tn(flat(h2), d_f, name="ff1_dw")
    dx1, d_out1, pb_b, gl_b = mid_bwd(d_h2, dx2, x1, out1, mod, small["pre_norm2"], small["post_norm1"],
                                      name="mid_bwd")
    d_out1f = flat(d_out1)
    d_merged = unflat(mm_nn([(d_out1f, wt["w_out"])], name="out_dx"))
    g_out = mm_tn(flat(merged), d_out1f, name="out_dw")
    d_ya, d_yb, d_gates, gl_c = merge_bwd(d_merged, ya, yb, gates, small["b_gate"], name="merge_bwd")
    g_pa = mm_tn(flat(pa_in), flat(d_ya), name="pa_dw")
    g_pb = mm_tn(flat(ynorm), flat(d_yb), name="pb_dw")
    leaving = plan.grads_bg({"w_pa": g_pa, "w_pb": g_pb, "w_out": g_out, "w_ff1": g_ff1, "w_ff2": g_ff2})
    (d_l, dh_lru, g_wa_b, g_wx_b, lru_rows), landed = lru_bwd(
        pm, h_lru, lru_kept, d_ya, small["lru_conv_w"], wa_b, wx_b, small["lru_lambda"], wt["w_pa"], wt["lru"],
        name="lru_bwd", bg=leaving)
    plan.grads_done(landed)
    d_s, dh_ssd, d_dt, r4, r2, r1 = ssd_bwd(pm, conv_ssd, dtr, y_ssd, states, d_yb, small["ssd_conv_w"], dtb, alog,
                                          d_lanes, small["ssd_norm_w"], wt["w_pb"], wt["ssd"], name="ssd_bwd")
    d_lf, d_sf, d_gf, d_dtf = flat(d_l), flat(d_s), flat(d_gates), flat(d_dt)
    g_in = jnp.concatenate([
        mm_tn(h1f, d_lf, name="in_dw_lru"), mm_tn(h1f, d_sf, name="in_dw_ssd"),
        mm_tn(h1f, d_dtf, name="in_dw_dt")[:, :SSD_HEADS], mm_tn(h1f, d_gf, name="in_dw_gates")], axis=1)
    leaving = plan.grads_bg({"w_in": g_in})
    partial = [flat(dh_lru), flat(dh_ssd)]
    if leaving is None:
        d_h1 = mm_nn([(d_gf, wt["gates"]), (d_dtf, wt["dt"])], add=partial, name="in_dx_gates")
    else:
        d_h1, landed = mm_nn([(d_gf, wt["gates"]), (d_dtf, wt["dt"])], add=partial, name="in_dx_gates", bg=leaving)
        plan.grads_done(landed)
    grad_x, pb_c, gl_d = first_bwd(unflat(d_h1), dx1, x, mod, small["pre_norm1"], name="first_bwd")

    d_mod = jnp.stack([pb_c[:, 0], pb_c[:, 1], pb_b[:, 2], pb_b[:, 0], pb_b[:, 1], pb_a[:, 0]], axis=1)
    loss_cols = gl_a[1:2]
    nh = SSD_HEADS
    small_grads = {
        "pre_norm1": gl_d[0:1], "post_norm1": gl_b[1:2], "b_gate": gl_c[0:1],
        "lru_conv_w": lru_rows[4:8], "lru_conv_b": lru_rows[3:4],
        "lru_wa": _lru_block_grads(g_wa_b), "lru_ba": lru_rows[0:1],
        "lru_wx": _lru_block_grads(g_wx_b), "lru_bx": lru_rows[1:2], "lru_lambda": lru_rows[2:3],
        "ssd_conv_w": r4[1:5], "ssd_conv_b": r4[0:1],
        "ssd_dt_bias": r1[0:1, :nh], "ssd_a_log": r1[1:2, :nh], "ssd_d": r1[2:3, :nh],
        "ssd_norm_w": r2[0:1], "pre_norm2": gl_b[0:1], "post_norm2": gl_a[0:1],
    }
    return loss_cols, grad_x, d_mod, small_grads


def _position():
    return lax.axis_index("x"), lax.axis_index("y"), lax.axis_index("c")


def _other_chips(x, y):
    return [(1 - x, y), (x, 1 - y), (1 - x, 1 - y)]


def allgather8(v, *, name):
    m_per, n = v.shape

    def body(x_ref, out_ref, send_sems, recv_sems, local_sem):
        x, y, c = _position()
        me, sibling = (x, y, c), (x, y, 1 - c)
        chips = _other_chips(x, y)

        def rows(px, py, pc):
            return out_ref.at[pl.ds((4 * px + 2 * py + pc) * m_per, m_per), :]

        def copy(k, block, to, src=None):
            return pltpu.make_async_remote_copy(
                src_ref=rows(*block) if src is None else src, dst_ref=rows(*block),
                send_sem=send_sems.at[k], recv_sem=recv_sems.at[k], device_id=to, device_id_type=MESH)

        mine = pltpu.make_async_copy(x_ref, rows(*me), local_sem)
        mine.start()
        first = [copy(0, me, sibling, src=x_ref)]
        first += [copy(1 + j, me, (*chip, c), src=x_ref) for j, chip in enumerate(chips)]
        for cp in first:
            cp.start()
        passed = [copy(4 + j, (*chip, c), sibling) for j, chip in enumerate(chips)]
        for j, chip in enumerate(chips):
            copy(1 + j, (*chip, c), me).wait_recv()
            passed[j].start()
        copy(0, sibling, me).wait_recv()
        for j, chip in enumerate(chips):
            copy(4 + j, (*chip, 1 - c), me).wait_recv()
        for cp in first + passed:
            cp.wait_send()
        mine.wait()

    return _pcall(
        body, name=name,
        out_shape=jax.ShapeDtypeStruct((N_DEV * m_per, n), v.dtype),
        in_specs=[pl.BlockSpec(memory_space=pltpu.VMEM)],
        out_specs=pl.BlockSpec(memory_space=pltpu.VMEM),
        scratch_shapes=[pltpu.SemaphoreType.DMA((7,)), pltpu.SemaphoreType.DMA((7,)), pltpu.SemaphoreType.DMA],
    )(v)


def gather_weights(shards, *, name):
    n = len(shards)
    half = [s.shape[0] // 2 for s in shards]
    widths = sorted({s.shape[1] for s in shards})
    chunk_rows = [_stage_rows(h, s.shape[1], itemsize=s.dtype.itemsize) for s, h in zip(shards, half)]
    plan = [(w, j, r0) for w in range(n) for j in range(N_CHIPS - 1) for r0 in range(0, half[w], chunk_rows[w])]

    def body(*refs):
        ins, outs = refs[:n], refs[n:2 * n]
        send_sems, recv_sems, local_sems, passed_sems = refs[2 * n:2 * n + 4]
        stage = refs[2 * n + 4:]
        bufs = {wd: stage[4 * i] for i, wd in enumerate(widths)}
        load_sems = {wd: stage[4 * i + 1] for i, wd in enumerate(widths)}
        stage_send = {wd: stage[4 * i + 2] for i, wd in enumerate(widths)}
        x, y, c = _position()
        me_chip = 2 * x + y
        chips = _other_chips(x, y)

        def piece(w, chip, core):
            return outs[w].at[chip, pl.ds(core * half[w], half[w]), :]

        def over_ici(w, j, chip, src=None):
            px, py = chips[j]
            dst = piece(w, chip, c)
            return pltpu.make_async_remote_copy(
                src_ref=dst if src is None else src, dst_ref=dst, send_sem=send_sems.at[3 * w + j],
                recv_sem=recv_sems.at[3 * w + j], device_id=(px, py, c), device_id_type=MESH)

        local = [pltpu.make_async_copy(ins[w], outs[w].at[me_chip], local_sems.at[w]) for w in range(n)]
        for cp in local:
            cp.start()
        sent = []
        for w in range(n):
            for j in range(N_CHIPS - 1):
                cp = over_ici(w, j, me_chip, src=ins[w].at[pl.ds(c * half[w], half[w]), :])
                cp.start()
                sent.append(cp)
        chunks = []
        for idx, (w, j, r0) in enumerate(plan):
            wd, rb = shards[w].shape[1], chunk_rows[w]
            k = 2 * chips[j][0] + chips[j][1]

            def make(staged, slot, idx=idx, w=w, k=k, r0=r0, wd=wd, rb=rb):
                return pltpu.make_async_remote_copy(
                    src_ref=staged, dst_ref=outs[w].at[k, pl.ds(c * half[w] + r0, rb), :],
                    send_sem=stage_send[wd].at[slot], recv_sem=passed_sems.at[idx],
                    device_id=(x, y, 1 - c), device_id_type=MESH), True

            chunk = (wd, outs[w].at[k, pl.ds(c * half[w] + r0, rb), :], [make])
            if r0 == 0:
                chunk += (lambda w=w, j=j, k=k: over_ici(w, j, k).wait_recv(),)
            chunks.append(chunk)
        _staged(chunks, bufs, load_sems)
        for idx, (w, j, r0) in enumerate(plan):
            wd = shards[w].shape[1]
            k = 2 * chips[j][0] + chips[j][1]
            landed = outs[w].at[k, pl.ds((1 - c) * half[w] + r0, chunk_rows[w]), :]
            pltpu.make_async_remote_copy(
                src_ref=landed, dst_ref=landed, send_sem=stage_send[wd].at[0], recv_sem=passed_sems.at[idx],
                device_id=(x, y, 1 - c), device_id_type=MESH).wait_recv()
        for cp in sent:
            cp.wait_send()
        for cp in local:
            cp.wait()

    stage_rows = [(wd, max(r for s, r in zip(shards, chunk_rows) if s.shape[1] == wd)) for wd in widths]
    return _pcall(
        body, name=name,
        out_shape=[jax.ShapeDtypeStruct((N_CHIPS,) + s.shape, s.dtype) for s in shards],
        in_specs=[ANY] * n, out_specs=[ANY] * n,
        scratch_shapes=[pltpu.SemaphoreType.DMA((3 * n,)), pltpu.SemaphoreType.DMA((3 * n,)),
                        pltpu.SemaphoreType.DMA((n,)), pltpu.SemaphoreType.DMA((len(plan),))]
        + _stage_scratch(stage_rows, shards[0].dtype),
    )(*shards)


STAGE_BYTES = 2 << 20


def _stage_rows(rows, width, itemsize=4):
    return _pick(rows, tuple(t for t in (1024, 512, 256, 128, 64, 32, 16, 8) if t * width * itemsize <= STAGE_BYTES * 3 // 2))


def _staged(chunks, bufs, load_sems):
    count, pending = {}, {}

    def load(i):
        cls, src = chunks[i][0], chunks[i][1]
        if len(chunks[i]) > 3:
            chunks[i][3]()
        slot = count.get(cls, 0) % 2
        count[cls] = count.get(cls, 0) + 1
        for cp, remote in pending.pop((cls, slot), []):
            if remote:
                cp.wait_send()
            else:
                cp.wait()
        staged = bufs[cls].at[slot, pl.ds(0, src.shape[0]), :]
        ld = pltpu.make_async_copy(src, staged, load_sems[cls].at[slot])
        ld.start()
        return ld, cls, slot, staged

    cur = load(0)
    for i in range(len(chunks)):
        nxt = load(i + 1) if i + 1 < len(chunks) else None
        ld, cls, slot, staged = cur
        ld.wait()
        started = []
        for make in chunks[i][2]:
            cp, remote = make(staged, slot)
            cp.start()
            started.append((cp, remote))
        pending[(cls, slot)] = started
        cur = nxt
    for started in pending.values():
        for cp, remote in started:
            if remote:
                cp.wait_send()
            else:
                cp.wait()


def _stage_scratch(widths_rows, dtype):
    scratch = []
    for width, rows in widths_rows:
        scratch += [pltpu.VMEM((2, rows, width), dtype), pltpu.SemaphoreType.DMA((2,)), pltpu.SemaphoreType.DMA((2,)),
                    pltpu.SemaphoreType.DMA((2,))]
    return scratch


def send_half_to_sibling(grads, *, name):
    n = len(grads)
    half = [g.shape[1] // 2 for g in grads]
    widths = sorted({g.shape[2] for g in grads})
    chunk_rows = [_stage_rows(h, g.shape[2]) for g, h in zip(grads, half)]
    plan = [(w, k, r0) for w in range(n) for k in range(N_CHIPS) for r0 in range(0, half[w], chunk_rows[w])]

    def body(*refs):
        ins, theirs = refs[:n], refs[n:2 * n]
        recv_sems = refs[2 * n]
        stage = refs[2 * n + 1:]
        bufs = {wd: stage[4 * i] for i, wd in enumerate(widths)}
        load_sems = {wd: stage[4 * i + 1] for i, wd in enumerate(widths)}
        send_sems = {wd: stage[4 * i + 2] for i, wd in enumerate(widths)}
        x, y, c = _position()
        chunks = []
        for idx, (w, k, r0) in enumerate(plan):
            wd = grads[w].shape[2]
            rb = chunk_rows[w]

            def make(staged, slot, idx=idx, w=w, k=k, r0=r0, wd=wd, rb=rb):
                return pltpu.make_async_remote_copy(
                    src_ref=staged, dst_ref=theirs[w].at[k, pl.ds(r0, rb), :], send_sem=send_sems[wd].at[slot],
                    recv_sem=recv_sems.at[idx], device_id=(x, y, 1 - c), device_id_type=MESH), True

            chunks.append((wd, ins[w].at[k, pl.ds((1 - c) * half[w] + r0, rb), :], [make]))
        _staged(chunks, bufs, load_sems)
        for idx, (w, k, r0) in enumerate(plan):
            wd = grads[w].shape[2]
            landed = theirs[w].at[k, pl.ds(r0, chunk_rows[w]), :]
            pltpu.make_async_remote_copy(
                src_ref=landed, dst_ref=landed, send_sem=send_sems[wd].at[0], recv_sem=recv_sems.at[idx],
                device_id=(x, y, 1 - c), device_id_type=MESH).wait_recv()

    stage_rows = [(wd, max(r for g, r in zip(grads, chunk_rows) if g.shape[2] == wd)) for wd in widths]
    return _pcall(
        body, name=name,
        out_shape=[jax.ShapeDtypeStruct((N_CHIPS, h, g.shape[2]), g.dtype) for g, h in zip(grads, half)],
        in_specs=[ANY] * n, out_specs=[ANY] * n,
        scratch_shapes=[pltpu.SemaphoreType.DMA((len(plan),))] + _stage_scratch(stage_rows, F32),
    )(*grads)


def _chip_exchange_background(arrays, out_shapes, src_of, dst_of, landed_of, own_of):
    n = len(arrays)

    def copies(ins, outs, scr):
        send_sems, recv_sems, local_sems = scr
        x, y, c = _position()
        me_chip = 2 * x + y
        local, sends, recvs = [], [], []
        for w in range(n):
            local.append(pltpu.make_async_copy(*own_of(ins[w], outs[w], w, me_chip), local_sems.at[w]))
            for j, (px, py) in enumerate(_other_chips(x, y)):
                sems = dict(send_sem=send_sems.at[3 * w + j], recv_sem=recv_sems.at[3 * w + j],
                            device_id=(px, py, c), device_id_type=MESH)
                sends.append(pltpu.make_async_remote_copy(
                    src_ref=src_of(ins[w], w, 2 * px + py, me_chip, c), dst_ref=dst_of(outs[w], w, me_chip, c), **sems))
                landed = landed_of(outs[w], w, 2 * px + py, c)
                recvs.append(pltpu.make_async_remote_copy(src_ref=landed, dst_ref=landed, **sems))
        return local, sends, recvs

    def start(ins, outs, scr):
        local, sends, _ = copies(ins, outs, scr)
        for cp in local + sends:
            cp.start()

    def finish(ins, outs, scr):
        local, sends, recvs = copies(ins, outs, scr)
        for cp in recvs:
            cp.wait_recv()
        for cp in sends:
            cp.wait_send()
        for cp in local:
            cp.wait()

    scratch = [pltpu.SemaphoreType.DMA((3 * n,)), pltpu.SemaphoreType.DMA((3 * n,)), pltpu.SemaphoreType.DMA((n,))]
    return Background(arrays, out_shapes, scratch, start, finish)


def scatter_background(parts):
    return _chip_exchange_background(
        parts, [jax.ShapeDtypeStruct(p.shape, p.dtype) for p in parts],
        src_of=lambda ref, w, peer, me, c: ref.at[peer], dst_of=lambda ref, w, me, c: ref.at[me],
        landed_of=lambda ref, w, peer, c: ref.at[peer], own_of=lambda i, o, w, me: (i.at[me], o.at[me]))


def gather_halves_background(shards):
    half = [s.shape[0] // 2 for s in shards]
    rows = lambda w, c: pl.ds(c * half[w], half[w])
    return _chip_exchange_background(
        shards, [jax.ShapeDtypeStruct((N_CHIPS,) + s.shape, s.dtype) for s in shards],
        src_of=lambda ref, w, peer, me, c: ref.at[rows(w, c), :], dst_of=lambda ref, w, me, c: ref.at[me, rows(w, c), :],
        landed_of=lambda ref, w, peer, c: ref.at[peer, rows(w, c), :], own_of=lambda i, o, w, me: (i, o.at[me]))


def fill_other_half(gathered, *, name):
    n = len(gathered)
    half = [g.shape[1] // 2 for g in gathered]
    widths = sorted({g.shape[2] for g in gathered})
    chunk_rows = [_stage_rows(h, g.shape[2], itemsize=2) for g, h in zip(gathered, half)]
    plan = [(w, j, r0) for w in range(n) for j in range(N_CHIPS - 1) for r0 in range(0, half[w], chunk_rows[w])]

    def body(*refs):
        ins, outs = refs[:n], refs[n:2 * n]
        recv_sems = refs[2 * n]
        stage = refs[2 * n + 1:]
        bufs = {wd: stage[4 * i] for i, wd in enumerate(widths)}
        load_sems = {wd: stage[4 * i + 1] for i, wd in enumerate(widths)}
        send_sems = {wd: stage[4 * i + 2] for i, wd in enumerate(widths)}
        x, y, c = _position()
        chips = _other_chips(x, y)
        chunks = []
        for idx, (w, j, r0) in enumerate(plan):
            wd, rb = gathered[w].shape[2], chunk_rows[w]
            k = 2 * chips[j][0] + chips[j][1]

            def make(staged, slot, idx=idx, w=w, k=k, r0=r0, wd=wd, rb=rb):
                return pltpu.make_async_remote_copy(
                    src_ref=staged, dst_ref=outs[w].at[k, pl.ds(c * half[w] + r0, rb), :],
                    send_sem=send_sems[wd].at[slot], recv_sem=recv_sems.at[idx],
                    device_id=(x, y, 1 - c), device_id_type=MESH), True

            chunks.append((wd, ins[w].at[k, pl.ds(c * half[w] + r0, rb), :], [make]))
        _staged(chunks, bufs, load_sems)
        for idx, (w, j, r0) in enumerate(plan):
            wd = gathered[w].shape[2]
            k = 2 * chips[j][0] + chips[j][1]
            landed = outs[w].at[k, pl.ds((1 - c) * half[w] + r0, chunk_rows[w]), :]
            pltpu.make_async_remote_copy(
                src_ref=landed, dst_ref=landed, send_sem=send_sems[wd].at[0], recv_sem=recv_sems.at[idx],
                device_id=(x, y, 1 - c), device_id_type=MESH).wait_recv()

    stage_rows = [(wd, max(r for g, r in zip(gathered, chunk_rows) if g.shape[2] == wd)) for wd in widths]
    return _pcall(
        body, name=name, out_shape=[jax.ShapeDtypeStruct(g.shape, g.dtype) for g in gathered],
        in_specs=[ANY] * n, out_specs=[ANY] * n, input_output_aliases={w: w for w in range(n)},
        scratch_shapes=[pltpu.SemaphoreType.DMA((len(plan),))] + _stage_scratch(stage_rows, gathered[0].dtype),
    )(*gathered)


def join_with_sibling(halves, *, name):
    n = len(halves)
    widths = sorted({h.shape[1] for h in halves})
    chunk_rows = [_stage_rows(h.shape[0], h.shape[1]) for h in halves]
    plan = [(w, r0) for w in range(n) for r0 in range(0, halves[w].shape[0], chunk_rows[w])]

    def body(*refs):
        ins, outs = refs[:n], refs[n:2 * n]
        recv_sems = refs[2 * n]
        stage = refs[2 * n + 1:]
        bufs = {wd: stage[4 * i] for i, wd in enumerate(widths)}
        load_sems = {wd: stage[4 * i + 1] for i, wd in enumerate(widths)}
        send_sems = {wd: stage[4 * i + 2] for i, wd in enumerate(widths)}
        store_sems = {wd: stage[4 * i + 3] for i, wd in enumerate(widths)}
        x, y, c = _position()
        chunks = []
        for idx, (w, r0) in enumerate(plan):
            h, wd = halves[w].shape
            rb = chunk_rows[w]

            def to_sibling(staged, slot, idx=idx, w=w, r0=r0, h=h, wd=wd, rb=rb):
                return pltpu.make_async_remote_copy(
                    src_ref=staged, dst_ref=outs[w].at[pl.ds(c * h + r0, rb), :], send_sem=send_sems[wd].at[slot],
                    recv_sem=recv_sems.at[idx], device_id=(x, y, 1 - c), device_id_type=MESH), True

            def to_mine(staged, slot, w=w, r0=r0, h=h, wd=wd, rb=rb):
                return pltpu.make_async_copy(staged, outs[w].at[pl.ds(c * h + r0, rb), :], store_sems[wd].at[slot]), False

            chunks.append((wd, ins[w].at[pl.ds(r0, rb), :], [to_sibling, to_mine]))
        _staged(chunks, bufs, load_sems)
        for idx, (w, r0) in enumerate(plan):
            h, wd = halves[w].shape
            landed = outs[w].at[pl.ds((1 - c) * h + r0, chunk_rows[w]), :]
            pltpu.make_async_remote_copy(
                src_ref=landed, dst_ref=landed, send_sem=send_sems[wd].at[0], recv_sem=recv_sems.at[idx],
                device_id=(x, y, 1 - c), device_id_type=MESH).wait_recv()

    stage_rows = [(wd, max(r for h, r in zip(halves, chunk_rows) if h.shape[1] == wd)) for wd in widths]
    return _pcall(
        body, name=name,
        out_shape=[jax.ShapeDtypeStruct((2 * h.shape[0], h.shape[1]), h.dtype) for h in halves],
        in_specs=[ANY] * n, out_specs=[ANY] * n,
        scratch_shapes=[pltpu.SemaphoreType.DMA((len(plan),))] + _stage_scratch(stage_rows, F32),
    )(*halves)


def _row_tile(rows, cols, itemsize=4, budget=2 << 20):
    for t in (1024, 512, 256, 128, 64, 32, 16, 8):
        if rows % t == 0 and t * cols * itemsize <= budget:
            return t
    return rows


def add_half_to_bf16(core, full, theirs, *, name):
    k, r, c = theirs.shape
    tr = _row_tile(r, c)
    nb = r // tr

    def body(core_ref, a_ref, b_ref, o_ref):
        o_ref[...] = (a_ref[...] + b_ref[...]).astype(BF16)

    spec = pl.BlockSpec((None, tr, c), lambda i, j, core_ref: (i, j, 0))
    grid_spec = pltpu.PrefetchScalarGridSpec(
        num_scalar_prefetch=1, grid=(k, nb),
        in_specs=[pl.BlockSpec((None, tr, c), lambda i, j, core_ref: (i, core_ref[0] * nb + j, 0)), spec],
        out_specs=spec)
    return _pcall(body, name=name, grid_spec=grid_spec,
                  out_shape=jax.ShapeDtypeStruct(theirs.shape, BF16))(core, full, theirs)


def sum_blocks(v, *, name):
    k, r, c = v.shape
    tr = _row_tile(r, c * k)

    def body(v_ref, o_ref):
        acc = v_ref[0].astype(F32)
        for j in range(1, k):
            acc = acc + v_ref[j].astype(F32)
        o_ref[...] = acc

    return _pcall(body, name=name, grid=(r // tr,),
                  in_specs=[pl.BlockSpec((k, tr, c), lambda i: (0, i, 0))],
                  out_specs=pl.BlockSpec((tr, c), lambda i: (i, 0)),
                  out_shape=jax.ShapeDtypeStruct((r, c), F32))(v)


def adamw(w, g, m, v, *, name):
    r, c = w.shape
    tr = _row_tile(r, c, budget=1 << 20)
    m_scale = 1.0 / (1.0 - ADAM_B1 ** ADAM_STEP)
    v_scale = 1.0 / (1.0 - ADAM_B2 ** ADAM_STEP)

    def body(w_ref, g_ref, m_ref, v_ref, d_ref, nm_ref, nv_ref):
        gv = g_ref[...]
        nm = ADAM_B1 * m_ref[...] + (1.0 - ADAM_B1) * gv
        nv = ADAM_B2 * v_ref[...] + (1.0 - ADAM_B2) * (gv * gv)
        nm_ref[...] = nm
        nv_ref[...] = nv
        d_ref[...] = -ADAM_LR * ((nm * m_scale) / (jnp.sqrt(nv * v_scale) + ADAM_EPS) + ADAM_WD * w_ref[...])

    spec = pl.BlockSpec((tr, c), lambda i: (i, 0))
    return _pcall(body, name=name, grid=(r // tr,), in_specs=[spec] * 4, out_specs=[spec] * 3,
                  out_shape=[jax.ShapeDtypeStruct((r, c), F32)] * 3)(w, g, m, v)


def ada_fwd(c_all, w_shard, b_shard, *, name):
    bsz, d = c_all.shape
    ncol = w_shard.shape[1]

    def body(c_ref, w_ref, b_ref, o_ref):
        cv = c_ref[...]
        act = (cv * _sigmoid(cv)).astype(BF16)
        o_ref[...] = _dot(act, w_ref[...].astype(BF16)) + b_ref[...]

    tn = _pick(ncol, (512, 256, 128))
    return _pcall(body, name=name, grid=(ncol // tn,),
                  in_specs=[pl.BlockSpec((bsz, d), lambda j: (0, 0)), pl.BlockSpec((d, tn), lambda j: (0, j)),
                            pl.BlockSpec((1, tn), lambda j: (0, j))],
                  out_specs=pl.BlockSpec((bsz, tn), lambda j: (0, j)),
                  out_shape=jax.ShapeDtypeStruct((bsz, ncol), F32))(c_all, w_shard, b_shard)


def ada_bwd(c_all, d_mod_all, d_mod_cols, *, name):
    bsz, d = c_all.shape
    ncol = d_mod_cols.shape[1]
    nall = d_mod_all.shape[1]

    def body(c_ref, da_ref, dc_ref, gw_ref, gb_ref):
        cv = c_ref[...]
        act = (cv * _sigmoid(cv)).astype(BF16)
        gw_ref[...] = _dot_tn(act, dc_ref[...].astype(BF16))
        gb_ref[...] = _colsum(da_ref[...])

    full = lambda s: pl.BlockSpec(s, lambda: (0,) * len(s))
    return _pcall(body, name=name,
                  in_specs=[full((bsz, d)), full((bsz, nall)), full((bsz, ncol))],
                  out_specs=[full((d, ncol)), full((1, nall))],
                  out_shape=[jax.ShapeDtypeStruct((d, ncol), F32), jax.ShapeDtypeStruct((1, nall), F32)],
                  )(c_all, d_mod_all, d_mod_cols)


WEIGHT_NAMES = ['w_ada', 'b_ada', 'pre_norm1', 'post_norm1', 'w_in', 'b_gate', 'lru_conv_w', 'lru_conv_b', 'lru_wa',
                'lru_ba', 'lru_wx', 'lru_bx', 'lru_lambda', 'w_pa', 'ssd_conv_w', 'ssd_conv_b', 'ssd_dt_bias',
                'ssd_a_log', 'ssd_d', 'ssd_norm_w', 'w_pb', 'w_out', 'pre_norm2', 'post_norm2', 'w_ff1', 'w_ff2']
BIG_NAMES = ['w_in', 'w_pa', 'w_pb', 'w_out', 'w_ff1', 'w_ff2']
COLUMN_SHARDED = ('w_in', 'w_ff1')
SMALL_NAMES = [n for n in WEIGHT_NAMES if n not in BIG_NAMES + ['w_ada', 'b_ada']]
SHARDED_SMALL = ('lru_conv_w', 'ssd_conv_w')
PACK_WIDTH = 1024


def _whole(name, gathered):
    if name in COLUMN_SHARDED:
        return jnp.transpose(gathered, (1, 0, 2)).reshape(gathered.shape[1], N_CHIPS * gathered.shape[2])
    return gathered.reshape(N_CHIPS * gathered.shape[1], gathered.shape[2])


def _by_chip(name, g):
    if name in COLUMN_SHARDED:
        return jnp.transpose(g.reshape(g.shape[0], N_CHIPS, g.shape[1] // N_CHIPS), (1, 0, 2))
    return g.reshape(N_CHIPS, g.shape[0] // N_CHIPS, g.shape[1])


class ChipExchange:
    def __init__(self, shards, core):
        self.shards, self.core = shards, core
        self.pending, self.halves = [], {}

    def weights_bg(self):
        return gather_halves_background(list(self.shards.values()))

    def weights(self, arrived):
        swapped = fill_other_half(arrived, name="weights_from_sibling")
        return {n: _whole(n, g) for n, g in zip(self.shards, swapped)}

    def grads_bg(self, grads):
        self.pending = list(grads)
        by_chip = [_by_chip(n, g) for n, g in grads.items()]
        theirs = send_half_to_sibling(by_chip, name="grads_to_sibling_" + self.pending[0])
        sums = [add_half_to_bf16(self.core, a, b, name="add_cores_" + n)
                for n, a, b in zip(self.pending, by_chip, theirs)]
        return scatter_background(sums)

    def grads_done(self, landed):
        for n, p in zip(self.pending, landed):
            self.halves[n] = sum_blocks(p, name="add_chips_" + n)

    def reduced(self):
        names = list(self.halves)
        return dict(zip(names, join_with_sibling([self.halves[n] for n in names], name="grads_join")))


def _pack(parts):
    flat = jnp.concatenate([p.reshape(-1).astype(F32) for p in parts])
    rows = -(-flat.shape[0] // (PACK_WIDTH * SUBLANES)) * SUBLANES
    return jnp.pad(flat, (0, rows * PACK_WIDTH - flat.shape[0])).reshape(rows, PACK_WIDTH)


def _unpack(packed, shapes):
    flat = packed.reshape(-1)
    out, pos = [], 0
    for s in shapes:
        size = int(np.prod(s))
        out.append(flat[pos:pos + size].reshape(s))
        pos += size
    return out


def kernel(x, c, w_ada, b_ada, pre_norm1, post_norm1, w_in, b_gate, lru_conv_w, lru_conv_b, lru_wa, lru_ba, lru_wx, lru_bx, lru_lambda, w_pa, ssd_conv_w, ssd_conv_b, ssd_dt_bias, ssd_a_log, ssd_d, ssd_norm_w, w_pb, w_out, pre_norm2, post_norm2, w_ff1, w_ff2, loss_target, m_w_ada, m_b_ada, m_pre_norm1, m_post_norm1, m_w_in, m_b_gate, m_lru_conv_w, m_lru_conv_b, m_lru_wa, m_lru_ba, m_lru_wx, m_lru_bx, m_lru_lambda, m_w_pa, m_ssd_conv_w, m_ssd_conv_b, m_ssd_dt_bias, m_ssd_a_log, m_ssd_d, m_ssd_norm_w, m_w_pb, m_w_out, m_pre_norm2, m_post_norm2, m_w_ff1, m_w_ff2, v_w_ada, v_b_ada, v_pre_norm1, v_post_norm1, v_w_in, v_b_gate, v_lru_conv_w, v_lru_conv_b, v_lru_wa, v_lru_ba, v_lru_wx, v_lru_bx, v_lru_lambda, v_w_pa, v_ssd_conv_w, v_ssd_conv_b, v_ssd_dt_bias, v_ssd_a_log, v_ssd_d, v_ssd_norm_w, v_w_pb, v_w_out, v_pre_norm2, v_post_norm2, v_w_ff1, v_w_ff2):
    given = dict(locals())
    bsz, seq, d = x.shape
    my_x, my_y, my_c = lax.axis_index("x"), lax.axis_index("y"), lax.axis_index("c")
    chip = 2 * my_x + my_y
    dev = 2 * chip + my_c
    strip = lambda a: a if a.ndim == 2 else a[0]
    w = {n: strip(given[n]) for n in WEIGHT_NAMES}
    m = {n: strip(given["m_" + n]) for n in WEIGHT_NAMES}
    v = {n: strip(given["v_" + n]) for n in WEIGHT_NAMES}

    first_shapes = [c.shape] + [w[n].shape for n in SHARDED_SMALL]
    first = allgather8(_pack([c] + [w[n] for n in SHARDED_SMALL]), name="gather_c_conv")
    first = first.reshape(N_DEV, -1, PACK_WIDTH)
    per_dev = [_unpack(first[k], first_shapes) for k in range(N_DEV)]
    c_all = jnp.concatenate([p[0] for p in per_dev], axis=0)
    conv_full = {n: jnp.concatenate([per_dev[2 * k][1 + i] for k in range(N_CHIPS)], axis=1)
                 for i, n in enumerate(SHARDED_SMALL)}

    ncol = w["w_ada"].shape[1]
    b_cols = lax.dynamic_slice(b_ada, (0, chip * ncol), (1, ncol))
    mod_cols = ada_fwd(c_all, w["w_ada"], b_cols, name="ada_fwd")
    mod_all = allgather8(mod_cols, name="gather_mod").reshape(N_CHIPS, 2, N_DEV * bsz, ncol)[:, 0]
    mod_all = jnp.transpose(mod_all, (1, 0, 2)).reshape(N_DEV * bsz, N_CHIPS * ncol)
    mod = lax.dynamic_slice(mod_all, (dev * bsz, 0), (bsz, 6 * d)).reshape(bsz, 6, d)
    mod = jnp.pad(mod, ((0, 0), (0, 2), (0, 0)))

    w_in_full = _whole("w_in", gather_weights([w["w_in"].astype(BF16)], name="gather_w_in")[0])
    big = {"w_main": w_in_full[:, :8192],
           "w_dt": jnp.pad(w_in_full[:, 8192:8192 + SSD_HEADS], ((0, 0), (0, LANES - SSD_HEADS))),
           "w_gates": w_in_full[:, 8192 + SSD_HEADS:]}
    small = {n: w[n] for n in SMALL_NAMES}
    small.update(conv_full)
    plan = ChipExchange({n: w[n].astype(BF16) for n in BIG_NAMES if n != "w_in"}, my_c.astype(jnp.int32).reshape(1))

    loss_cols, grad_x, d_mod, small_grads = local_step(x, loss_target, mod, big, small, plan)

    packed = _pack([d_mod, loss_cols] + [small_grads[n] for n in SMALL_NAMES])
    rows = packed.shape[0]
    everyone = allgather8(packed, name="gather_small").reshape(N_DEV, rows, PACK_WIDTH)
    d_mod_all = everyone[:, :bsz * 6].reshape(N_DEV * bsz, 6 * d)
    summed = sum_blocks(everyone, name="sum_small")
    shapes = [d_mod.shape, loss_cols.shape] + [small_grads[n].shape for n in SMALL_NAMES]
    parts = _unpack(summed, shapes)
    loss = jnp.sum(parts[1])
    grads = dict(zip(SMALL_NAMES, parts[2:]))
    for n in SHARDED_SMALL:
        cols = w[n].shape[1]
        grads[n] = lax.dynamic_slice(grads[n], (0, chip * cols), (grads[n].shape[0], cols))
    d_mod_cols = lax.dynamic_slice(d_mod_all, (0, chip * ncol), (N_DEV * bsz, ncol))
    grads["w_ada"], grads["b_ada"] = ada_bwd(c_all, d_mod_all, d_mod_cols, name="ada_bwd")

    grads.update(plan.reduced())

    delta, new_m, new_v = {}, {}, {}
    for n in BIG_NAMES + ["w_ada", "b_ada"]:
        delta[n], new_m[n], new_v[n] = adamw(w[n], grads[n], m[n], v[n], name="adamw_" + n)
    shapes = [w[n].shape for n in SMALL_NAMES]
    pk = lambda src: _pack([src[n] for n in SMALL_NAMES])
    upd = adamw(pk(w), pk(grads), pk(m), pk(v), name="adamw_small")
    for out, packed_out in zip((delta, new_m, new_v), upd):
        out.update(zip(SMALL_NAMES, _unpack(packed_out, shapes)))

    shaped = lambda src: [src[n].reshape(given[n].shape) for n in WEIGHT_NAMES]
    return (loss, grad_x, *shaped(grads), *shaped(delta), *shaped(new_m), *shaped(new_v))
```

```python
import functools
import math

import numpy as np
import jax
import jax.numpy as jnp
from jax import lax
from jax.experimental import pallas as pl
from jax.experimental.pallas import tpu as pltpu

F32 = jnp.float32
BF16 = jnp.bfloat16
HI = lax.Precision.HIGHEST
MESH = pl.DeviceIdType.MESH

D_MODEL = 1024
LRU_HEADS = 16
LRU_HEAD_DIM = 64
LRU_C = 8.0
SSD_INNER = 2048
SSD_HEADS = 32
SSD_HEAD_DIM = 64
SSD_GROUPS = 8
SSD_STATE = 128
SSD_CHUNK = 128
SSD_CONV_DIM = 4096
D_FF = 4096
EPS = 1e-6
N_CHIPS = 4
N_DEV = 8
LANES = 128
SUBLANES = 8

ADAM_LR = 0.001
ADAM_B1 = 0.9
ADAM_B2 = 0.999
ADAM_EPS = 1e-08
ADAM_WD = 0.01
ADAM_STEP = 10


ANY = pl.BlockSpec(memory_space=pl.ANY)


def _pcall(body, **kw):
    return pl.pallas_call(body, **kw)


class Background:
    def __init__(self, inputs, out_shapes, scratch, start, finish):
        self.inputs, self.out_shapes, self.scratch = list(inputs), list(out_shapes), list(scratch)
        self.start, self.finish = start, finish

    def wrap(self, body, kw):
        n_in, n_out = len(kw["in_specs"]), len(kw["out_specs"])
        n_scr = len(kw.get("scratch_shapes", []))
        b_in, b_out = len(self.inputs), len(self.out_shapes)
        grid = kw["grid"]

        def wrapped(*refs):
            ins, b_ins = refs[:n_in], refs[n_in:n_in + b_in]
            o0 = n_in + b_in
            outs, b_outs = refs[o0:o0 + n_out], refs[o0 + n_out:o0 + n_out + b_out]
            s0 = o0 + n_out + b_out
            scr, b_scr = refs[s0:s0 + n_scr], refs[s0 + n_scr:]
            ids = [pl.program_id(a) for a in range(len(grid))]
            first = functools.reduce(jnp.logical_and, [i == 0 for i in ids])
            last = functools.reduce(jnp.logical_and, [i == g - 1 for i, g in zip(ids, grid)])

            @pl.when(first)
            def _():
                self.start(b_ins, b_outs, b_scr)

            body(*ins, *outs, *scr)

            @pl.when(last)
            def _():
                self.finish(b_ins, b_outs, b_scr)

        kw = dict(kw, in_specs=list(kw["in_specs"]) + [ANY] * b_in, out_specs=list(kw["out_specs"]) + [ANY] * b_out,
                  out_shape=list(kw["out_shape"]) + self.out_shapes,
                  scratch_shapes=list(kw.get("scratch_shapes", [])) + self.scratch)
        return wrapped, kw


def _run(body, args, bg, **kw):
    n_out = len(kw["out_shape"])
    if bg is None:
        return list(_pcall(body, **kw)(*args)), []
    body, kw = bg.wrap(body, kw)
    outs = _pcall(body, **kw)(*args, *bg.inputs)
    return list(outs[:n_out]), list(outs[n_out:])


def _sigmoid(v):
    return 1.0 / (1.0 + jnp.exp(-v))


def _log1p(u):
    return jnp.where(u < 1e-3, u * (1.0 - u * (0.5 - u * (1.0 / 3.0))), jnp.log(1.0 + u))


def _softplus(v):
    return jnp.maximum(v, 0.0) + _log1p(jnp.exp(-jnp.abs(v)))


def _neg_expm1(v):
    small = -v * (1.0 + v * (0.5 + v * (1.0 / 6.0 + v * (1.0 / 24.0))))
    return jnp.where(v > -0.05, small, 1.0 - jnp.exp(v))


_GELU_K = math.sqrt(2.0 / math.pi)


def _gelu(v):
    t = jnp.tanh(_GELU_K * (v + 0.044715 * v * v * v))
    return 0.5 * v * (1.0 + t)


def _gelu_grad(v):
    t = jnp.tanh(_GELU_K * (v + 0.044715 * v * v * v))
    return 0.5 * (1.0 + t) + 0.5 * v * (1.0 - t * t) * _GELU_K * (1.0 + 3.0 * 0.044715 * v * v)


def _colsum(v):
    return jnp.sum(v, axis=0, keepdims=True)


def _dot(a, b, precision=None):
    return lax.dot_general(a, b, (((1,), (0,)), ((), ())), preferred_element_type=F32, precision=precision)


def _dot_nt(a, b):
    return lax.dot_general(a, b, (((1,), (1,)), ((), ())), preferred_element_type=F32)


def _dot_tn(a, b):
    return lax.dot_general(a, b, (((0,), (0,)), ((), ())), preferred_element_type=F32)


def _shift_down(xt, prev8, j):
    if j == 0:
        return xt
    n = xt.shape[0]
    r = pltpu.roll(xt, j, 0)
    p = pltpu.roll(prev8, j, 0)
    rows = lax.broadcasted_iota(jnp.int32, (SUBLANES, xt.shape[1]), 0)
    top = jnp.where(rows < j, p, r[0:SUBLANES])
    if n == SUBLANES:
        return top
    return jnp.concatenate([top, r[SUBLANES:]], axis=0)


def _shift_up(xt, next8, j):
    if j == 0:
        return xt
    n = xt.shape[0]
    r = pltpu.roll(xt, n - j, 0)
    p = pltpu.roll(next8, SUBLANES - j, 0)
    rows = lax.broadcasted_iota(jnp.int32, (SUBLANES, xt.shape[1]), 0)
    bot = jnp.where(rows >= SUBLANES - j, p, r[n - SUBLANES:])
    if n == SUBLANES:
        return bot
    return jnp.concatenate([r[:n - SUBLANES], bot], axis=0)


def _conv4(xt, prev8, w, b):
    out = b + w[3:4] * xt
    for k in range(3):
        out = out + w[k:k + 1] * _shift_down(xt, prev8, 3 - k)
    return out


def _conv4_bwd(d_out, next8, xt, w):
    d_x = w[3:4] * d_out
    d_w = []
    for k in range(3):
        up = _shift_up(d_out, next8, 3 - k)
        d_x = d_x + w[k:k + 1] * up
        d_w.append(_colsum(up * xt))
    d_w.append(_colsum(d_out * xt))
    return d_x, d_w, _colsum(d_out)


def _stack_rows(rows, width):
    rows = list(rows) + [jnp.zeros((1, width), F32)] * (SUBLANES - len(rows))
    return jnp.concatenate(rows, axis=0)


def _pick(n, cands):
    for c in cands:
        if n % c == 0:
            return c
    raise ValueError(f"no tile for {n}")


MM_ROWS = 512
MM_PANEL_COLS = 2048
MM_SUB = 512


def mm_nn(pairs, *, name, out_dtype=F32, a_fn=None, add=None, epi=None, extra=None, bg=None):
    np_ = len(pairs)
    m, n = pairs[0][0].shape[0], pairs[0][1].shape[1]
    tm = _pick(m, (MM_ROWS, 256, 128, 64, 32, 16, 8))
    pn = n if n <= MM_PANEL_COLS else _pick(n, (MM_PANEL_COLS, 1024, 512, 256, 128))
    ns = _pick(pn, (MM_SUB, 256, 128))
    adds = list(add or ())
    has_extra = extra is not None
    stage0 = a_fn is not None or pairs[0][0].dtype != BF16

    def body(*refs):
        a_refs, b_refs = refs[:np_], refs[np_:2 * np_]
        pos = 2 * np_
        extra_ref = None
        add_refs = refs[pos:pos + len(adds)]
        pos += len(adds)
        if has_extra:
            extra_ref = refs[pos]
            pos += 1
        o_ref = refs[pos]
        lhs = list(a_refs)
        if stage0:
            av = a_refs[0][...]
            if a_fn is not None:
                av = a_fn(av)
            refs[pos + 1][...] = av.astype(BF16)
            lhs[0] = refs[pos + 1]
        for n0 in range(0, pn, ns):
            sl = slice(n0, n0 + ns)
            acc = None
            for a_ref, b_ref in zip(lhs, b_refs):
                part = _dot(a_ref[...].astype(BF16), b_ref[:, sl])
                acc = part if acc is None else acc + part
            for add_ref in add_refs:
                acc = acc + add_ref[:, sl]
            if epi is not None:
                acc = epi(acc, extra_ref[:, sl]) if has_extra else epi(acc)
            o_ref[:, sl] = acc.astype(out_dtype)

    in_specs = [pl.BlockSpec((tm, a.shape[1]), lambda j, i: (i, 0)) for a, _ in pairs]
    in_specs += [pl.BlockSpec((b.shape[0], pn), lambda j, i: (0, j)) for _, b in pairs]
    args = [a for a, _ in pairs] + [b for _, b in pairs]
    tile = pl.BlockSpec((tm, pn), lambda j, i: (i, j))
    for extra_add in adds:
        in_specs.append(tile)
        args.append(extra_add)
    if has_extra:
        in_specs.append(tile)
        args.append(extra)
    outs, bg_outs = _run(
        body, args, bg, name=name, grid=(n // pn, m // tm), in_specs=in_specs, out_specs=[tile],
        out_shape=[jax.ShapeDtypeStruct((m, n), out_dtype)],
        scratch_shapes=[pltpu.VMEM((tm, pairs[0][0].shape[1]), BF16)] if stage0 else [])
    return outs[0] if bg is None else (outs[0], bg_outs)


MM_REDUCE_ROWS = 1024
MM_GRAD_ROWS = 1024
MM_GRAD_COLS = 2048


def mm_tn(a, b, *, name, a_fn=None):
    m, ka = a.shape
    nb = b.shape[1]
    pa = _pick(ka, (MM_GRAD_ROWS, 512, 256, 128))
    pb = nb if nb <= MM_GRAD_COLS else _pick(nb, (MM_GRAD_COLS, 1024, 512, 256, 128))
    ns = _pick(pb, (MM_SUB, 256, 128))
    tmk = _pick(m, (MM_REDUCE_ROWS, 512, 256, 128, 64, 32, 16))

    def body(a_ref, b_ref, o_ref, lhs):
        k = pl.program_id(2)

        @pl.when(k == 0)
        def _():
            o_ref[...] = jnp.zeros_like(o_ref)

        av = a_ref[...]
        if a_fn is not None:
            av = a_fn(av)
        lhs[...] = av.astype(BF16)
        for n0 in range(0, pb, ns):
            o_ref[:, n0:n0 + ns] += _dot_tn(lhs[...], b_ref[:, n0:n0 + ns].astype(BF16))

    return _pcall(
        body, name=name,
        grid=(ka // pa, nb // pb, m // tmk),
        in_specs=[pl.BlockSpec((tmk, pa), lambda i, j, k: (k, i)),
                  pl.BlockSpec((tmk, pb), lambda i, j, k: (k, j))],
        out_specs=pl.BlockSpec((pa, pb), lambda i, j, k: (i, j)),
        out_shape=jax.ShapeDtypeStruct((ka, nb), F32),
        scratch_shapes=[pltpu.VMEM((tmk, pa), BF16)],
    )(a, b)


def _relu_sq(v):
    r = jnp.maximum(v, 0.0)
    return r * r


ROW_TILE = 512


def _row_specs(bsz, seq, width, ts):
    return pl.BlockSpec((None, ts, width), lambda b, i: (b, i, 0))


def _vec_spec(width):
    return pl.BlockSpec((1, width), lambda b, i: (0, 0))


def _mod_spec():
    return pl.BlockSpec((None, SUBLANES, D_MODEL), lambda b, i: (b, 0, 0))


def _rstd(v):
    return lax.rsqrt(jnp.mean(v * v, axis=-1, keepdims=True) + EPS)


def prenorm(x, w, mod, *, name):
    bsz, seq, d = x.shape
    ts = _pick(seq, (ROW_TILE, 256, 128))

    def body(x_ref, w_ref, mod_ref, h_ref):
        xv = x_ref[...]
        m = mod_ref[...]
        xh = xv * _rstd(xv)
        h_ref[...] = ((xh * w_ref[...]) * (1.0 + m[1:2]) + m[0:1]).astype(BF16)

    return _pcall(
        body, name=name, grid=(bsz, seq // ts),
        in_specs=[_row_specs(bsz, seq, d, ts), _vec_spec(d), _mod_spec()],
        out_specs=_row_specs(bsz, seq, d, ts),
        out_shape=jax.ShapeDtypeStruct((bsz, seq, d), BF16),
    )(x, w, mod)


def post1_pre2(x, out1, mod, post1, pre2, *, name):
    bsz, seq, d = x.shape
    ts = _pick(seq, (ROW_TILE, 256, 128))

    def body(x_ref, o_ref, mod_ref, p1_ref, p2_ref, x1_ref, h2_ref):
        m = mod_ref[...]
        ov = o_ref[...]
        x1 = x_ref[...] + m[2:3] * ((ov * _rstd(ov)) * p1_ref[...])
        x1_ref[...] = x1
        xh = x1 * _rstd(x1)
        h2_ref[...] = ((xh * p2_ref[...]) * (1.0 + m[4:5]) + m[3:4]).astype(BF16)

    return _pcall(
        body, name=name, grid=(bsz, seq // ts),
        in_specs=[_row_specs(bsz, seq, d, ts), _row_specs(bsz, seq, d, ts), _mod_spec(), _vec_spec(d), _vec_spec(d)],
        out_specs=[_row_specs(bsz, seq, d, ts), _row_specs(bsz, seq, d, ts)],
        out_shape=[jax.ShapeDtypeStruct((bsz, seq, d), F32), jax.ShapeDtypeStruct((bsz, seq, d), BF16)],
    )(x, out1, mod, post1, pre2)


def _acc_specs(d):
    per_batch = pl.BlockSpec((None, SUBLANES, d), lambda b, i: (b, 0, 0))
    glob = pl.BlockSpec((SUBLANES, d), lambda b, i: (0, 0))
    return per_batch, glob


def _accumulate(pb_ref, gl_ref, pb_rows, gl_rows, width):
    b, i = pl.program_id(0), pl.program_id(1)

    @pl.when(i == 0)
    def _():
        pb_ref[...] = jnp.zeros_like(pb_ref)

    @pl.when((b == 0) & (i == 0))
    def _():
        gl_ref[...] = jnp.zeros_like(gl_ref)

    pb_ref[...] += _stack_rows(pb_rows, width)
    gl_ref[...] += _stack_rows(gl_rows, width)


def _rms_bwd(d_n, n, r):
    return r * (d_n - n * jnp.mean(d_n * n, axis=-1, keepdims=True))


def final_bwd(x1, y2, target, mod, post2, *, name):
    bsz, seq, d = x1.shape
    ts = _pick(seq, (ROW_TILE, 256, 128))

    def body(x1_ref, y_ref, t_ref, mod_ref, p_ref, dx_ref, dy_ref, pb_ref, gl_ref):
        m = mod_ref[...]
        g2 = m[5:6]
        yv = y_ref[...]
        r = _rstd(yv)
        n = yv * r
        o = n * p_ref[...]
        diff = (x1_ref[...] + g2 * o) - t_ref[...]
        dx = diff * (1.0 / d)
        dx_ref[...] = dx
        d_o = dx * g2
        dy_ref[...] = _rms_bwd(d_o * p_ref[...], n, r).astype(BF16)
        _accumulate(pb_ref, gl_ref, [_colsum(dx * o)], [_colsum(d_o * n), _colsum(diff * diff) * (0.5 / d)], d)

    pb, gl = _acc_specs(d)
    rs = _row_specs(bsz, seq, d, ts)
    return _pcall(
        body, name=name, grid=(bsz, seq // ts),
        in_specs=[rs, rs, rs, _mod_spec(), _vec_spec(d)],
        out_specs=[rs, rs, pb, gl],
        out_shape=[jax.ShapeDtypeStruct((bsz, seq, d), F32), jax.ShapeDtypeStruct((bsz, seq, d), BF16),
                   jax.ShapeDtypeStruct((bsz, SUBLANES, d), F32), jax.ShapeDtypeStruct((SUBLANES, d), F32)],
    )(x1, y2, target, mod, post2)


def mid_bwd(d_h2, dx2, x1, out1, mod, pre2, post1, *, name):
    bsz, seq, d = x1.shape
    ts = _pick(seq, (ROW_TILE, 256, 128))

    def body(dh_ref, dx2_ref, x1_ref, o_ref, mod_ref, p2_ref, p1_ref, dx1_ref, do_ref, pb_ref, gl_ref):
        m = mod_ref[...]
        dh = dh_ref[...]
        x1 = x1_ref[...]
        r2 = _rstd(x1)
        xh = x1 * r2
        xw = xh * p2_ref[...]
        d_xw = dh * (1.0 + m[4:5])
        dx1 = dx2_ref[...] + _rms_bwd(d_xw * p2_ref[...], xh, r2)
        dx1_ref[...] = dx1
        ov = o_ref[...]
        r1 = _rstd(ov)
        n1 = ov * r1
        o1 = n1 * p1_ref[...]
        d_o1 = dx1 * m[2:3]
        do_ref[...] = _rms_bwd(d_o1 * p1_ref[...], n1, r1).astype(BF16)
        _accumulate(pb_ref, gl_ref, [_colsum(dh), _colsum(dh * xw), _colsum(dx1 * o1)],
                    [_colsum(d_xw * xh), _colsum(d_o1 * n1)], d)

    pb, gl = _acc_specs(d)
    rs = _row_specs(bsz, seq, d, ts)
    return _pcall(
        body, name=name, grid=(bsz, seq // ts),
        in_specs=[rs, rs, rs, rs, _mod_spec(), _vec_spec(d), _vec_spec(d)],
        out_specs=[rs, rs, pb, gl],
        out_shape=[jax.ShapeDtypeStruct((bsz, seq, d), F32), jax.ShapeDtypeStruct((bsz, seq, d), BF16),
                   jax.ShapeDtypeStruct((bsz, SUBLANES, d), F32), jax.ShapeDtypeStruct((SUBLANES, d), F32)],
    )(d_h2, dx2, x1, out1, mod, pre2, post1)


def first_bwd(d_h1, dx1, x, mod, pre1, *, name):
    bsz, seq, d = x.shape
    ts = _pick(seq, (ROW_TILE, 256, 128))

    def body(dh_ref, dx1_ref, x_ref, mod_ref, p_ref, gx_ref, pb_ref, gl_ref):
        m = mod_ref[...]
        dh = dh_ref[...]
        xv = x_ref[...]
        r = _rstd(xv)
        xh = xv * r
        xw = xh * p_ref[...]
        d_xw = dh * (1.0 + m[1:2])
        gx_ref[...] = dx1_ref[...] + _rms_bwd(d_xw * p_ref[...], xh, r)
        _accumulate(pb_ref, gl_ref, [_colsum(dh), _colsum(dh * xw)], [_colsum(d_xw * xh)], d)

    pb, gl = _acc_specs(d)
    rs = _row_specs(bsz, seq, d, ts)
    return _pcall(
        body, name=name, grid=(bsz, seq // ts),
        in_specs=[rs, rs, rs, _mod_spec(), _vec_spec(d)],
        out_specs=[rs, pb, gl],
        out_shape=[jax.ShapeDtypeStruct((bsz, seq, d), F32),
                   jax.ShapeDtypeStruct((bsz, SUBLANES, d), F32), jax.ShapeDtypeStruct((SUBLANES, d), F32)],
    )(d_h1, dx1, x, mod, pre1)


def merge_bwd(d_merged, ya, yb, gates, b_gate, *, name):
    bsz, seq, d = ya.shape
    ts = _pick(seq, (ROW_TILE, 256, 128))

    def body(dm_ref, ya_ref, yb_ref, g_ref, b_ref, dya_ref, dyb_ref, dg_ref, gl_ref):
        b, i = pl.program_id(0), pl.program_id(1)
        g = _sigmoid(g_ref[...] + b_ref[...])
        dm = dm_ref[...]
        ga, gb = g[:, :d], g[:, d:]
        dya_ref[...] = (dm * ga).astype(BF16)
        dyb_ref[...] = (dm * gb).astype(BF16)
        dg = jnp.concatenate([dm * ya_ref[...] * ga * (1.0 - ga), dm * yb_ref[...] * gb * (1.0 - gb)], axis=1)
        dg_ref[...] = dg.astype(BF16)

        @pl.when((b == 0) & (i == 0))
        def _():
            gl_ref[...] = jnp.zeros_like(gl_ref)

        gl_ref[...] += _stack_rows([_colsum(dg)], 2 * d)

    rs = _row_specs(bsz, seq, d, ts)
    rs2 = _row_specs(bsz, seq, 2 * d, ts)
    return _pcall(
        body, name=name, grid=(bsz, seq // ts),
        in_specs=[rs, rs, rs, rs2, _vec_spec(2 * d)],
        out_specs=[rs, rs, rs2, pl.BlockSpec((SUBLANES, 2 * d), lambda b, i: (0, 0))],
        out_shape=[jax.ShapeDtypeStruct((bsz, seq, d), BF16), jax.ShapeDtypeStruct((bsz, seq, d), BF16),
                   jax.ShapeDtypeStruct((bsz, seq, 2 * d), BF16), jax.ShapeDtypeStruct((SUBLANES, 2 * d), F32)],
    )(d_merged, ya, yb, gates, b_gate)


LRU_TILE = 256
N_LRU_BLOCKS = D_MODEL // LANES


def _block_mm(v, w_ref, transpose=False):
    vb = v.astype(BF16)
    outs = []
    for j in range(N_LRU_BLOCKS):
        blk = vb[:, LANES * j:LANES * (j + 1)]
        outs.append(_dot_nt(blk, w_ref[j]) if transpose else _dot(blk, w_ref[j]))
    return jnp.concatenate(outs, axis=1)


def _lru_gates(xc, wa_ref, ba, wx_ref, bx, sp):
    r = _sigmoid(_block_mm(xc, wa_ref) + ba)
    i = _sigmoid(_block_mm(xc, wx_ref) + bx)
    la = (-LRU_C * r) * sp
    a = jnp.exp(la)
    sq = jnp.sqrt(_neg_expm1(2.0 * la))
    return r, i, a, sq


def _prev8_spec(width, col_block, tile_rows):
    per = tile_rows // SUBLANES
    return pl.BlockSpec((None, SUBLANES, width), lambda b, i: (b, jnp.maximum(i * per - 1, 0), col_block))


def lru_fwd(pm, cw, cb, wa, ba, wx, bx, lam, w_pa, *, name):
    bsz, seq, _ = pm.shape
    d = D_MODEL
    ts = _pick(seq, (LRU_TILE, 128))

    def body(lx_ref, lxp_ref, lg_ref, cw_ref, cb_ref, wa_ref, ba_ref, wx_ref, bx_ref, lam_ref, wpa_ref,
             h_ref, pa_ref, ya_ref, kept_ref, hc, a_s, u_s):
        i = pl.program_id(1)

        @pl.when(i == 0)
        def _():
            hc[...] = jnp.zeros_like(hc)

        lx = lx_ref[...]
        prev8 = jnp.where(i == 0, 0.0, lxp_ref[...])
        xc = _conv4(lx, prev8, cw_ref[...], cb_ref[...])
        sp = _softplus(-lam_ref[...])
        r, ig, a, sq = _lru_gates(xc, wa_ref, ba_ref[...], wx_ref, bx_ref[...], sp)
        for k, kept in enumerate((xc, r, ig, a, sq)):
            kept_ref[:, k * d:(k + 1) * d] = kept
        a_s[...] = a
        u_s[...] = sq * (ig * xc)

        def step(g, h):
            r0 = pl.multiple_of(g * SUBLANES, SUBLANES)
            a8 = a_s[pl.ds(r0, SUBLANES), :]
            u8 = u_s[pl.ds(r0, SUBLANES), :]
            rows = []
            for j in range(SUBLANES):
                h = a8[j:j + 1] * h + u8[j:j + 1]
                rows.append(h)
            h_ref[pl.ds(r0, SUBLANES), :] = jnp.concatenate(rows, axis=0)
            return h

        hc[...] = lax.fori_loop(0, ts // SUBLANES, step, hc[...])
        pa_ref[...] = (h_ref[...] * _gelu(lg_ref[...])).astype(BF16)
        ya_ref[...] = _dot(pa_ref[...], wpa_ref[...])

    vec = _vec_spec(d)
    wspec = pl.BlockSpec((N_LRU_BLOCKS, LANES, LANES), lambda b, i: (0, 0, 0))
    rs = _row_specs(bsz, seq, d, ts)
    return _pcall(
        body, name=name, grid=(bsz, seq // ts),
        in_specs=[pl.BlockSpec((None, ts, d), lambda b, i: (b, i, 0)), _prev8_spec(d, 0, ts),
                  pl.BlockSpec((None, ts, d), lambda b, i: (b, i, 1)),
                  pl.BlockSpec((4, d), lambda b, i: (0, 0)), vec, wspec, vec, wspec, vec, vec,
                  pl.BlockSpec(w_pa.shape, lambda b, i: (0, 0))],
        out_specs=[rs, rs, rs, _row_specs(bsz, seq, 5 * d, ts)],
        out_shape=[jax.ShapeDtypeStruct((bsz, seq, d), F32), jax.ShapeDtypeStruct((bsz, seq, d), BF16),
                   jax.ShapeDtypeStruct((bsz, seq, d), F32), jax.ShapeDtypeStruct((bsz, seq, 5 * d), F32)],
        scratch_shapes=[pltpu.VMEM((1, d), F32), pltpu.VMEM((ts, d), F32), pltpu.VMEM((ts, d), F32)],
    )(pm, pm, pm, cw, cb, wa, ba, wx, bx, lam, w_pa)


def lru_bwd(pm, h, kept, d_ya, cw, wa, wx, lam, wt_pa, wt_lru, *, name, bg=None):
    bsz, seq, _ = pm.shape
    d = D_MODEL
    ts = _pick(seq, (LRU_TILE, 128))
    nt = seq // ts
    per = ts // SUBLANES

    def rev(i):
        return nt - 1 - i

    def body(lx_ref, lg_ref, h_ref, hp_ref, kept_ref, dya_ref, cw_ref, wa_ref, wx_ref,
             lam_ref, wtpa_ref, wtl_ref, dl_ref, dh1_ref, dwa_ref, dwx_ref, rows_ref,
             carry, dxc_next, a_s, dh_s, acc_s):
        b, i = pl.program_id(0), pl.program_id(1)
        t = rev(i)

        @pl.when(i == 0)
        def _():
            carry[...] = jnp.zeros_like(carry)
            dxc_next[...] = jnp.zeros_like(dxc_next)

        @pl.when((b == 0) & (i == 0))
        def _():
            dwa_ref[...] = jnp.zeros_like(dwa_ref)
            dwx_ref[...] = jnp.zeros_like(dwx_ref)
            rows_ref[...] = jnp.zeros_like(rows_ref)

        lx = lx_ref[...]
        lg = lg_ref[...]
        cwv = cw_ref[...]
        lam_v = lam_ref[...]
        sp = _softplus(-lam_v)
        xc, r, ig, a, sq = (kept_ref[:, k * d:(k + 1) * d] for k in range(5))
        hv = h_ref[...]
        d_pa = _dot(dya_ref[...], wtpa_ref[...])
        a_s[...] = a
        dh_s[...] = d_pa * _gelu(lg)

        def step(g, c):
            r0 = pl.multiple_of((per - 1 - g) * SUBLANES, SUBLANES)
            a8 = a_s[pl.ds(r0, SUBLANES), :]
            d8 = dh_s[pl.ds(r0, SUBLANES), :]
            rows = [None] * SUBLANES
            for j in range(SUBLANES - 1, -1, -1):
                acc = d8[j:j + 1] + c
                rows[j] = acc
                c = a8[j:j + 1] * acc
            acc_s[pl.ds(r0, SUBLANES), :] = jnp.concatenate(rows, axis=0)
            return c

        carry[...] = lax.fori_loop(0, per, step, carry[...])
        d_u = acc_s[...]
        hprev8 = jnp.where(t == 0, 0.0, hp_ref[...])
        d_a = d_u * _shift_down(hv, hprev8, 1)
        d_sq = d_u * (ig * xc)
        d_i = d_u * (sq * xc)
        d_xc = d_u * (sq * ig)
        d_la = d_a * a - d_sq * (a * a) / sq
        d_pre_r = (d_la * (-LRU_C * sp)) * (r * (1.0 - r))
        d_pre_i = d_i * (ig * (1.0 - ig))
        d_xc = d_xc + _block_mm(d_pre_r, wa_ref, transpose=True) + _block_mm(d_pre_i, wx_ref, transpose=True)
        xcb = xc.astype(BF16)
        drb = d_pre_r.astype(BF16)
        dib = d_pre_i.astype(BF16)
        for j in range(N_LRU_BLOCKS):
            sl = slice(LANES * j, LANES * (j + 1))
            dwa_ref[j] += _dot_tn(xcb[:, sl], drb[:, sl])
            dwx_ref[j] += _dot_tn(xcb[:, sl], dib[:, sl])
        d_lx, d_cw, d_cb = _conv4_bwd(d_xc, dxc_next[...], lx, cwv)
        dxc_next[...] = d_xc[0:SUBLANES]
        d_lam = _colsum(d_la * (-LRU_C * r)) * (-_sigmoid(-lam_v))
        rows_ref[...] += _stack_rows([_colsum(d_pre_r), _colsum(d_pre_i), d_lam, d_cb] + d_cw, d)
        dl_ref[:, :d] = d_lx.astype(BF16)
        dl_ref[:, d:] = (d_pa * hv * _gelu_grad(lg)).astype(BF16)
        dh1_ref[...] = _dot(dl_ref[...], wtl_ref[...])

    vec = _vec_spec(d)
    wspec = pl.BlockSpec((N_LRU_BLOCKS, LANES, LANES), lambda b, i: (0, 0, 0))
    tile = lambda col: pl.BlockSpec((None, ts, d), lambda b, i: (b, rev(i), col))
    prev8 = lambda col: pl.BlockSpec((None, SUBLANES, d), lambda b, i: (b, jnp.maximum(rev(i) * per - 1, 0), col))
    whole = lambda v: pl.BlockSpec(v.shape, lambda b, i: (0, 0))
    return _run(
        body, (pm, pm, h, h, kept, d_ya, cw, wa, wx, lam, wt_pa, wt_lru), bg, name=name, grid=(bsz, nt),
        in_specs=[tile(0), tile(1), tile(0), prev8(0), pl.BlockSpec((None, ts, 5 * d), lambda b, i: (b, rev(i), 0)),
                  tile(0), pl.BlockSpec((4, d), lambda b, i: (0, 0)), wspec, wspec, vec,
                  whole(wt_pa), whole(wt_lru)],
        out_specs=[pl.BlockSpec((None, ts, 2 * d), lambda b, i: (b, rev(i), 0)), tile(0), wspec, wspec,
                   pl.BlockSpec((SUBLANES, d), lambda b, i: (0, 0))],
        out_shape=[jax.ShapeDtypeStruct((bsz, seq, 2 * d), BF16), jax.ShapeDtypeStruct((bsz, seq, d), F32),
                   jax.ShapeDtypeStruct((N_LRU_BLOCKS, LANES, LANES), F32),
                   jax.ShapeDtypeStruct((N_LRU_BLOCKS, LANES, LANES), F32),
                   jax.ShapeDtypeStruct((SUBLANES, d), F32)],
        scratch_shapes=[pltpu.VMEM((1, d), F32), pltpu.VMEM((SUBLANES, d), F32),
                        pltpu.VMEM((ts, d), F32), pltpu.VMEM((ts, d), F32), pltpu.VMEM((ts, d), F32)])


L = SSD_CHUNK
N_PAIRS = SSD_HEADS // 2


def _ssd_common(conv, dt_raw, dtb, alog):
    sg = _sigmoid(conv)
    xa = conv * sg
    dtv = _softplus(dt_raw + dtb)
    a_neg = -jnp.exp(alog)
    rowi = lax.broadcasted_iota(jnp.int32, (L, L), 0)
    coli = lax.broadcasted_iota(jnp.int32, (L, L), 1)
    tril = (rowi >= coli).astype(F32)
    cs = _dot(tril, dtv * a_neg, precision=HI)
    return conv, sg, xa, dtv, a_neg, cs, rowi, coli


def _head_masks():
    lane = lax.broadcasted_iota(jnp.int32, (L, LANES), 1)
    return lane < SSD_HEAD_DIM


def _spread(v, p, first):
    return jnp.where(first[:v.shape[0]], v[:, 2 * p:2 * p + 1], v[:, 2 * p + 1:2 * p + 2])


def _place_head_sums(acc, z, p, first, lane1):
    rows = z.shape[0]
    s0 = jnp.sum(jnp.where(first[:rows], z, 0.0), axis=1, keepdims=True)
    s1 = jnp.sum(jnp.where(first[:rows], 0.0, z), axis=1, keepdims=True)
    lane = lane1[:rows]
    return acc + jnp.where(lane == 2 * p, s0, 0.0) + jnp.where(lane == 2 * p + 1, s1, 0.0)


def _stack_heads(v, first):
    return jnp.concatenate([jnp.where(first, v, 0.0), jnp.where(first, 0.0, v)], axis=0).astype(BF16)


def ssd_fwd(pm, dtr, cw, cb, dtb, alog, d_lanes, nw, w_pb, ya, gates, b_gate, w_out, *, name, bg=None):
    bsz, seq, _ = pm.shape
    nc = seq // L
    inner, cdim, d = SSD_INNER, SSD_CONV_DIM, D_MODEL

    def body(xbc_ref, xp_ref, z_ref, dt_ref, cw_ref, cb_ref, dtb_ref, alog_ref, dl_ref, nw_ref, wpb_ref,
             ya_ref, g_ref, bg_ref, wout_ref,
             y_ref, yn_ref, st_ref, yb_ref, conv_ref, mg_ref, out_ref, state):
        i = pl.program_id(1)

        @pl.when(i == 0)
        def _():
            state[...] = jnp.zeros_like(state)

        prev8 = jnp.where(i == 0, 0.0, xp_ref[...])
        conv = _conv4(xbc_ref[...], prev8, cw_ref[...], cb_ref[...])
        conv_ref[...] = conv
        _, _, xa, dtv, _, cs, rowi, coli = _ssd_common(conv, dt_ref[...], dtb_ref[...], alog_ref[...])
        cst = cs.T
        causal = rowi >= coli
        first = _head_masks()
        for g in range(SSD_GROUPS):
            bg = xa[:, inner + SSD_STATE * g:inner + SSD_STATE * (g + 1)].astype(BF16)
            cg = xa[:, inner + SSD_GROUPS * SSD_STATE + SSD_STATE * g:
                    inner + SSD_GROUPS * SSD_STATE + SSD_STATE * (g + 1)].astype(BF16)
            cbm = _dot_nt(cg, bg)
            for pp in range(2):
                p = 2 * g + pp
                sl = slice(LANES * p, LANES * (p + 1))
                ms = []
                for hh in (2 * p, 2 * p + 1):
                    seg = cs[:, hh:hh + 1] - cst[hh:hh + 1, :]
                    ms.append((cbm * jnp.exp(jnp.where(causal, seg, -jnp.inf))).astype(BF16))
                xsp = xa[:, sl]
                cs_p = _spread(cs, p, first)
                cs_last = cs_p[L - 1:L]
                xp = xsp * _spread(dtv, p, first)
                y_diag = _dot(jnp.concatenate(ms, axis=1), _stack_heads(xp, first))
                st = state[p]
                st_ref[p] = st
                y_off = _dot(cg, st.astype(BF16)) * jnp.exp(cs_p)
                y_ref[:, sl] = y_diag + y_off + dl_ref[:, sl] * xsp
                state[p] = st * jnp.exp(cs_last) + _dot_tn(bg, (xp * jnp.exp(cs_last - cs_p)).astype(BF16))
        zv = z_ref[...]
        yz = y_ref[...] * (zv * _sigmoid(zv))
        gw = inner // SSD_GROUPS
        for g in range(SSD_GROUPS):
            sl = slice(gw * g, gw * (g + 1))
            seg = yz[:, sl]
            yn_ref[:, sl] = ((seg * _rstd(seg)) * nw_ref[:, sl]).astype(BF16)
        yb = _dot(yn_ref[...], wpb_ref[...])
        yb_ref[...] = yb
        g = _sigmoid(g_ref[...] + bg_ref[...])
        mg_ref[...] = (g[:, :d] * ya_ref[...] + g[:, d:] * yb).astype(BF16)
        out_ref[...] = _dot(mg_ref[...], wout_ref[...])

    cvec = lambda w: pl.BlockSpec((1, w), lambda b, i: (0, 0))
    rows = lambda w: pl.BlockSpec((None, L, w), lambda b, i: (b, i, 0))
    outs, bg_outs = _run(
        body, (pm, pm, pm, dtr, cw, cb, dtb, alog, d_lanes, nw, w_pb, ya, gates, b_gate, w_out), bg, name=name,
        grid=(bsz, nc),
        in_specs=[pl.BlockSpec((None, L, cdim), lambda b, i: (b, i, 1)), _prev8_spec(cdim, 1, L),
                  pl.BlockSpec((None, L, inner), lambda b, i: (b, i, 1)),
                  pl.BlockSpec((None, L, LANES), lambda b, i: (b, i, 0)),
                  pl.BlockSpec((4, cdim), lambda b, i: (0, 0)), cvec(cdim), cvec(LANES), cvec(LANES),
                  cvec(inner), cvec(inner), pl.BlockSpec(w_pb.shape, lambda b, i: (0, 0)),
                  rows(d), rows(2 * d), cvec(2 * d), pl.BlockSpec(w_out.shape, lambda b, i: (0, 0))],
        out_specs=[rows(inner), rows(inner),
                   pl.BlockSpec((None, None, N_PAIRS, SSD_STATE, LANES), lambda b, i: (b, i, 0, 0, 0)),
                   rows(d), rows(cdim), rows(d), rows(d)],
        out_shape=[jax.ShapeDtypeStruct((bsz, seq, inner), F32), jax.ShapeDtypeStruct((bsz, seq, inner), BF16),
                   jax.ShapeDtypeStruct((bsz, nc, N_PAIRS, SSD_STATE, LANES), F32),
                   jax.ShapeDtypeStruct((bsz, seq, d), F32), jax.ShapeDtypeStruct((bsz, seq, cdim), F32),
                   jax.ShapeDtypeStruct((bsz, seq, d), BF16), jax.ShapeDtypeStruct((bsz, seq, d), F32)],
        scratch_shapes=[pltpu.VMEM((N_PAIRS, SSD_STATE, LANES), F32)])
    return outs, bg_outs


def ssd_bwd(pm, conv, dtr, y, states, d_yb, cw, dtb, alog, d_lanes, nw, wt_pb, wt_ssd, *, name):
    bsz, seq, _ = pm.shape
    nc = seq // L
    inner, cdim = SSD_INNER, SSD_CONV_DIM
    per = L // SUBLANES

    def rev(i):
        return nc - 1 - i

    def body(xbc_ref, conv_ref, z_ref, dt_ref, y_ref, st_ref, dyb_ref, cw_ref, dtb_ref, alog_ref,
             dl_ref, nw_ref, wtpb_ref, wts_ref, ds_ref, dh1_ref, ddt_ref, r4_ref, r2_ref, r1_ref,
             dstate, dconv_next, dxs_s, dbc_s):
        b, i = pl.program_id(0), pl.program_id(1)
        t = rev(i)

        @pl.when(i == 0)
        def _():
            dstate[...] = jnp.zeros_like(dstate)
            dconv_next[...] = jnp.zeros_like(dconv_next)

        @pl.when((b == 0) & (i == 0))
        def _():
            r4_ref[...] = jnp.zeros_like(r4_ref)
            r2_ref[...] = jnp.zeros_like(r2_ref)
            r1_ref[...] = jnp.zeros_like(r1_ref)

        xbc = xbc_ref[...]
        cwv = cw_ref[...]
        dt_in = dt_ref[...] + dtb_ref[...]
        conv = conv_ref[...]
        _, sg, xa, dtv, a_neg, cs, rowi, coli = _ssd_common(conv, dt_ref[...], dtb_ref[...], alog_ref[...])
        cst = cs.T
        causal = rowi >= coli
        anti = coli >= rowi
        first = _head_masks()
        lane1 = lax.broadcasted_iota(jnp.int32, (L, LANES), 1)

        yv = y_ref[...]
        zv = z_ref[...]
        sz = _sigmoid(zv)
        zs = zv * sz
        yz = yv * zs
        dyn = _dot(dyb_ref[...], wtpb_ref[...])
        gw = inner // SSD_GROUPS
        d_yz_parts, d_nw_parts = [], []
        for g in range(SSD_GROUPS):
            sl = slice(gw * g, gw * (g + 1))
            seg = yz[:, sl]
            r = _rstd(seg)
            n = seg * r
            d_nw_parts.append(_colsum(dyn[:, sl] * n))
            d_yz_parts.append(_rms_bwd(dyn[:, sl] * nw_ref[:, sl], n, r))
        d_yz = jnp.concatenate(d_yz_parts, axis=1)
        d_y = d_yz * zs
        ds_ref[:, :inner] = (d_yz * yv * (sz * (1.0 + zv * (1.0 - sz)))).astype(BF16)

        a1 = jnp.zeros((L, LANES), F32)
        a2 = jnp.zeros((L, LANES), F32)
        xs_dxt = jnp.zeros((L, LANES), F32)
        c0 = jnp.zeros((1, LANES), F32)
        d_dl = jnp.zeros((1, LANES), F32)
        for g in range(SSD_GROUPS):
            bsl = slice(inner + SSD_STATE * g, inner + SSD_STATE * (g + 1))
            csl = slice(inner + SSD_GROUPS * SSD_STATE + SSD_STATE * g,
                        inner + SSD_GROUPS * SSD_STATE + SSD_STATE * (g + 1))
            bg = xa[:, bsl].astype(BF16)
            cg = xa[:, csl].astype(BF16)
            cbm = _dot_nt(cg, bg)
            cbt = _dot_nt(bg, cg)
            d_cb = jnp.zeros((L, L), F32)
            d_bg = jnp.zeros((L, SSD_STATE), F32)
            d_cg = jnp.zeros((L, SSD_STATE), F32)
            for pp in range(2):
                p = 2 * g + pp
                sl = slice(LANES * p, LANES * (p + 1))
                xsp = xa[:, sl]
                dt_p = _spread(dtv, p, first)
                cs_p = _spread(cs, p, first)
                cs_last = cs_p[L - 1:L]
                e_p = jnp.exp(cs_p)
                w_p = jnp.exp(cs_last - cs_p)
                e_last = jnp.exp(cs_last)
                xp = xsp * dt_p
                xpb = xp.astype(BF16)
                dyp = d_y[:, sl]
                dypb = dyp.astype(BF16)
                dy_heads = (jnp.where(first, dyp, 0.0).astype(BF16), jnp.where(first, 0.0, dyp).astype(BF16))
                x_heads = (jnp.where(first, xp, 0.0).astype(BF16), jnp.where(first, 0.0, xp).astype(BF16))
                mts = []
                for k, hh in enumerate((2 * p, 2 * p + 1)):
                    col = cs[:, hh:hh + 1]
                    row = cst[hh:hh + 1, :]
                    dec = jnp.exp(jnp.where(causal, col - row, -jnp.inf))
                    dec_t = jnp.exp(jnp.where(anti, row - col, -jnp.inf))
                    gd = _dot_nt(dy_heads[k], xpb) * dec
                    d_cb = d_cb + gd
                    mt = cbt * dec_t
                    qd = gd * cbm - _dot_nt(x_heads[k], dypb) * mt
                    a1 = a1 + jnp.where(lane1 == hh, jnp.sum(qd, axis=1, keepdims=True), 0.0)
                    mts.append(mt.astype(BF16))
                dst = dstate[p]
                dstb = dst.astype(BF16)
                st = st_ref[p]
                stb = st.astype(BF16)
                dye = (dyp * e_p).astype(BF16)
                xw = (xp * w_p).astype(BF16)
                dx_off = w_p * _dot(bg, dstb)
                d_xp = _dot(jnp.concatenate(mts, axis=1), jnp.concatenate(dy_heads, axis=0)) + dx_off
                dxs_s[:, sl] = d_xp * dt_p + dyp * dl_ref[:, sl]
                a1 = _place_head_sums(a1, dyp * (_dot(cg, stb) * e_p), p, first, lane1)
                a2 = _place_head_sums(a2, xp * dx_off, p, first, lane1)
                xs_dxt = _place_head_sums(xs_dxt, d_xp * xsp, p, first, lane1)
                c0 = _place_head_sums(c0, _colsum(dst * st) * e_last, p, first, lane1)
                d_dl = _place_head_sums(d_dl, _colsum(dyp * xsp), p, first, lane1)
                d_cg = d_cg + _dot_nt(dye, stb)
                d_bg = d_bg + _dot_nt(xw, dstb)
                dstate[p] = dst * e_last + _dot_tn(cg, dye)
            d_cbb = d_cb.astype(BF16)
            dbc_s[:, SSD_STATE * g:SSD_STATE * (g + 1)] = d_bg + _dot_tn(d_cbb, cg)
            dbc_s[:, SSD_GROUPS * SSD_STATE + SSD_STATE * g:SSD_GROUPS * SSD_STATE + SSD_STATE * (g + 1)] = (
                d_cg + _dot(d_cbb, bg))

        d_da = (_dot(anti.astype(F32), a1, precision=HI) + _dot((rowi > coli).astype(F32), a2, precision=HI) + c0)
        d_dt = d_da * a_neg + xs_dxt
        d_alog = _colsum(d_da * dtv) * a_neg
        d_dtr = jnp.where(lane1 < SSD_HEADS, d_dt * _sigmoid(dt_in), 0.0)
        ddt_ref[...] = d_dtr.astype(BF16)
        d_xa = jnp.concatenate([dxs_s[...], dbc_s[...]], axis=1)
        d_conv = d_xa * (sg * (1.0 + conv * (1.0 - sg)))
        d_xbc, d_cw, d_cbias = _conv4_bwd(d_conv, dconv_next[...], xbc, cwv)
        dconv_next[...] = d_conv[0:SUBLANES]
        ds_ref[:, inner:] = d_xbc.astype(BF16)
        dh1_ref[...] = _dot(ds_ref[...], wts_ref[...])
        r4_ref[...] += _stack_rows([d_cbias] + d_cw, cdim)
        r2_ref[...] += _stack_rows([jnp.concatenate(d_nw_parts, axis=1)], inner)
        r1_ref[...] += _stack_rows([_colsum(d_dtr), d_alog, d_dl], LANES)

    cvec = lambda w: pl.BlockSpec((1, w), lambda b, i: (0, 0))
    return _pcall(
        body, name=name, grid=(bsz, nc),
        in_specs=[pl.BlockSpec((None, L, cdim), lambda b, i: (b, rev(i), 1)),
                  pl.BlockSpec((None, L, cdim), lambda b, i: (b, rev(i), 0)),
                  pl.BlockSpec((None, L, inner), lambda b, i: (b, rev(i), 1)),
                  pl.BlockSpec((None, L, LANES), lambda b, i: (b, rev(i), 0)),
                  pl.BlockSpec((None, L, inner), lambda b, i: (b, rev(i), 0)),
                  pl.BlockSpec((None, None, N_PAIRS, SSD_STATE, LANES), lambda b, i: (b, rev(i), 0, 0, 0)),
                  pl.BlockSpec((None, L, D_MODEL), lambda b, i: (b, rev(i), 0)),
                  pl.BlockSpec((4, cdim), lambda b, i: (0, 0)), cvec(LANES), cvec(LANES),
                  cvec(inner), cvec(inner), pl.BlockSpec(wt_pb.shape, lambda b, i: (0, 0)),
                  pl.BlockSpec(wt_ssd.shape, lambda b, i: (0, 0))],
        out_specs=[pl.BlockSpec((None, L, inner + cdim), lambda b, i: (b, rev(i), 0)),
                   pl.BlockSpec((None, L, D_MODEL), lambda b, i: (b, rev(i), 0)),
                   pl.BlockSpec((None, L, LANES), lambda b, i: (b, rev(i), 0)),
                   pl.BlockSpec((SUBLANES, cdim), lambda b, i: (0, 0)),
                   pl.BlockSpec((SUBLANES, inner), lambda b, i: (0, 0)),
                   pl.BlockSpec((SUBLANES, LANES), lambda b, i: (0, 0))],
        out_shape=[jax.ShapeDtypeStruct((bsz, seq, inner + cdim), BF16),
                   jax.ShapeDtypeStruct((bsz, seq, D_MODEL), F32),
                   jax.ShapeDtypeStruct((bsz, seq, LANES), BF16),
                   jax.ShapeDtypeStruct((SUBLANES, cdim), F32),
                   jax.ShapeDtypeStruct((SUBLANES, inner), F32),
                   jax.ShapeDtypeStruct((SUBLANES, LANES), F32)],
        scratch_shapes=[pltpu.VMEM((N_PAIRS, SSD_STATE, LANES), F32), pltpu.VMEM((SUBLANES, cdim), F32),
                        pltpu.VMEM((L, inner), F32), pltpu.VMEM((L, 2 * SSD_GROUPS * SSD_STATE), F32)],
    )(pm, conv, pm, dtr, y, states, d_yb, cw, dtb, alog, d_lanes, nw, wt_pb, wt_ssd)


def _lru_block_weights(w):
    w = w.reshape(N_LRU_BLOCKS, 2, LRU_HEAD_DIM, LRU_HEAD_DIM)
    z = jnp.zeros((N_LRU_BLOCKS, LRU_HEAD_DIM, LRU_HEAD_DIM), w.dtype)
    top = jnp.concatenate([w[:, 0], z], axis=2)
    bot = jnp.concatenate([z, w[:, 1]], axis=2)
    return jnp.concatenate([top, bot], axis=1).astype(BF16)


def _lru_block_grads(g):
    h = LRU_HEAD_DIM
    return jnp.stack([g[:, :h, :h], g[:, h:, h:]], axis=1).reshape(LRU_HEADS, h, h)


def _pad_lanes(v, width=LANES):
    return jnp.pad(v, ((0, 0), (0, width - v.shape[1])))


class NoExchange:
    def __init__(self, weights):
        self._weights, self.grads = weights, {}

    def weights_bg(self):
        return None

    def weights(self, bg_outs):
        return self._weights

    def grads_bg(self, grads, short_host=False):
        self.grads.update(grads)
        return None

    def grads_done(self, bg_outs):
        pass


def local_step(x, target, mod, big, small, plan):
    bsz, seq, d = x.shape
    t = bsz * seq
    flat = lambda v: v.reshape(t, v.shape[-1])
    unflat = lambda v: v.reshape(bsz, seq, v.shape[-1])

    wa_b = _lru_block_weights(small["lru_wa"])
    wx_b = _lru_block_weights(small["lru_wx"])
    dtb = _pad_lanes(small["ssd_dt_bias"])
    alog = _pad_lanes(small["ssd_a_log"])
    d_lanes = jnp.repeat(small["ssd_d"], SSD_HEAD_DIM, axis=1)

    lru_cols = 2 * D_MODEL
    wt = {"lru": big["w_main"][:, :lru_cols].T, "ssd": big["w_main"][:, lru_cols:].T, "gates": big["w_gates"].T,
          "dt": big["w_dt"].T}

    h1 = prenorm(x, small["pre_norm1"], mod, name="prenorm1")
    h1f = flat(h1)
    arriving = plan.weights_bg()
    if arriving is None:
        pm, arrived = mm_nn([(h1f, big["w_main"])], name="in_proj_main"), []
    else:
        pm, arrived = mm_nn([(h1f, big["w_main"])], name="in_proj_main", bg=arriving)
    pm = unflat(pm)
    big = dict(big, **plan.weights(arrived))
    for n in ("w_pa", "w_pb", "w_out", "w_ff1", "w_ff2"):
        wt[n] = big[n].T
    gates = unflat(mm_nn([(h1f, big["w_gates"])], name="in_proj_gates"))
    dtr = unflat(mm_nn([(h1f, big["w_dt"])], name="in_proj_dt"))
    lru_args = (small["lru_conv_w"], small["lru_conv_b"], wa_b, small["lru_ba"], wx_b, small["lru_bx"],
                small["lru_lambda"])
    h_lru, pa_in, ya, lru_kept = lru_fwd(pm, *lru_args, big["w_pa"], name="lru_fwd")
    ssd_args = (small["ssd_conv_w"], small["ssd_conv_b"], dtb, alog, d_lanes, small["ssd_norm_w"])
    (y_ssd, ynorm, states, yb, conv_ssd, merged, out1), _ = ssd_fwd(
        pm, dtr, *ssd_args, big["w_pb"], ya, gates, small["b_gate"], big["w_out"], name="ssd_fwd")
    x1, h2 = post1_pre2(x, out1, mod, small["post_norm1"], small["pre_norm2"], name="post1_pre2")
    f = mm_nn([(flat(h2), big["w_ff1"])], name="ff1")
    y2 = unflat(mm_nn([(f, big["w_ff2"])], a_fn=_relu_sq, name="ff2"))

    dx2, d_y2, pb_a, gl_a = final_bwd(x1, y2, target, mod, small["post_norm2"], name="final_bwd")
    d_y2f = flat(d_y2)
    d_f = mm_nn([(d_y2f, wt["w_ff2"])], out_dtype=BF16, extra=f,
                epi=lambda r, fv: r * (2.0 * jnp.maximum(fv, 0.0)), name="ff2_dx")
    g_ff2 = mm_tn(f, d_y2f, a_fn=_relu_sq, name="ff2_dw")
    d_h2 = unflat(mm_nn([(d_f, wt["w_ff1"])], name="ff1_dx"))
    g_ff1 = mm_tn(flat(h2), d_f, name="ff1_dw")
    dx1, d_out1, pb_b, gl_b = mid_bwd(d_h2, dx2, x1, out1, mod, small["pre_norm2"], small["post_norm1"],
                                      name="mid_bwd")
    d_out1f = flat(d_out1)
    d_merged = unflat(mm_nn([(d_out1f, wt["w_out"])], name="out_dx"))
    g_out = mm_tn(flat(merged), d_out1f, name="out_dw")
    d_ya, d_yb, d_gates, gl_c = merge_bwd(d_merged, ya, yb, gates, small["b_gate"], name="merge_bwd")
    g_pa = mm_tn(flat(pa_in), flat(d_ya), name="pa_dw")
    g_pb = mm_tn(flat(ynorm), flat(d_yb), name="pb_dw")
    leaving = plan.grads_bg({"w_pa": g_pa, "w_pb": g_pb, "w_out": g_out, "w_ff1": g_ff1, "w_ff2": g_ff2})
    (d_l, dh_lru, g_wa_b, g_wx_b, lru_rows), landed = lru_bwd(
        pm, h_lru, lru_kept, d_ya, small["lru_conv_w"], wa_b, wx_b, small["lru_lambda"], wt["w_pa"], wt["lru"],
        name="lru_bwd", bg=leaving)
    plan.grads_done(landed)
    d_s, dh_ssd, d_dt, r4, r2, r1 = ssd_bwd(pm, conv_ssd, dtr, y_ssd, states, d_yb, small["ssd_conv_w"], dtb, alog,
                                          d_lanes, small["ssd_norm_w"], wt["w_pb"], wt["ssd"], name="ssd_bwd")
    d_lf, d_sf, d_gf, d_dtf = flat(d_l), flat(d_s), flat(d_gates), flat(d_dt)
    g_in = jnp.concatenate([
        mm_tn(h1f, d_lf, name="in_dw_lru"), mm_tn(h1f, d_sf, name="in_dw_ssd"),
        mm_tn(h1f, d_dtf, name="in_dw_dt")[:, :SSD_HEADS], mm_tn(h1f, d_gf, name="in_dw_gates")], axis=1)
    leaving = plan.grads_bg({"w_in": g_in}, short_host=True)
    partial = [flat(dh_lru), flat(dh_ssd)]
    if leaving is None:
        d_h1 = mm_nn([(d_gf, wt["gates"]), (d_dtf, wt["dt"])], add=partial, name="in_dx_gates")
    else:
        d_h1, landed = mm_nn([(d_gf, wt["gates"]), (d_dtf, wt["dt"])], add=partial, name="in_dx_gates", bg=leaving)
        plan.grads_done(landed)
    grad_x, pb_c, gl_d = first_bwd(unflat(d_h1), dx1, x, mod, small["pre_norm1"], name="first_bwd")

    d_mod = jnp.stack([pb_c[:, 0], pb_c[:, 1], pb_b[:, 2], pb_b[:, 0], pb_b[:, 1], pb_a[:, 0]], axis=1)
    loss_cols = gl_a[1:2]
    nh = SSD_HEADS
    small_grads = {
        "pre_norm1": gl_d[0:1], "post_norm1": gl_b[1:2], "b_gate": gl_c[0:1],
        "lru_conv_w": lru_rows[4:8], "lru_conv_b": lru_rows[3:4],
        "lru_wa": _lru_block_grads(g_wa_b), "lru_ba": lru_rows[0:1],
        "lru_wx": _lru_block_grads(g_wx_b), "lru_bx": lru_rows[1:2], "lru_lambda": lru_rows[2:3],
        "ssd_conv_w": r4[1:5], "ssd_conv_b": r4[0:1],
        "ssd_dt_bias": r1[0:1, :nh], "ssd_a_log": r1[1:2, :nh], "ssd_d": r1[2:3, :nh],
        "ssd_norm_w": r2[0:1], "pre_norm2": gl_b[0:1], "post_norm2": gl_a[0:1],
    }
    return loss_cols, grad_x, d_mod, small_grads


def _position():
    return lax.axis_index("x"), lax.axis_index("y"), lax.axis_index("c")


def _other_chips(x, y):
    return [(1 - x, y), (x, 1 - y), (1 - x, 1 - y)]


def allgather8(v, *, name):
    m_per, n = v.shape

    def body(x_ref, out_ref, send_sems, recv_sems, local_sem):
        x, y, c = _position()
        me, sibling = (x, y, c), (x, y, 1 - c)
        chips = _other_chips(x, y)

        def rows(px, py, pc):
            return out_ref.at[pl.ds((4 * px + 2 * py + pc) * m_per, m_per), :]

        def copy(k, block, to, src=None):
            return pltpu.make_async_remote_copy(
                src_ref=rows(*block) if src is None else src, dst_ref=rows(*block),
                send_sem=send_sems.at[k], recv_sem=recv_sems.at[k], device_id=to, device_id_type=MESH)

        mine = pltpu.make_async_copy(x_ref, rows(*me), local_sem)
        mine.start()
        first = [copy(0, me, sibling, src=x_ref)]
        first += [copy(1 + j, me, (*chip, c), src=x_ref) for j, chip in enumerate(chips)]
        for cp in first:
            cp.start()
        passed = [copy(4 + j, (*chip, c), sibling) for j, chip in enumerate(chips)]
        for j, chip in enumerate(chips):
            copy(1 + j, (*chip, c), me).wait_recv()
            passed[j].start()
        copy(0, sibling, me).wait_recv()
        for j, chip in enumerate(chips):
            copy(4 + j, (*chip, 1 - c), me).wait_recv()
        for cp in first + passed:
            cp.wait_send()
        mine.wait()

    return _pcall(
        body, name=name,
        out_shape=jax.ShapeDtypeStruct((N_DEV * m_per, n), v.dtype),
        in_specs=[pl.BlockSpec(memory_space=pltpu.VMEM)],
        out_specs=pl.BlockSpec(memory_space=pltpu.VMEM),
        scratch_shapes=[pltpu.SemaphoreType.DMA((7,)), pltpu.SemaphoreType.DMA((7,)), pltpu.SemaphoreType.DMA],
    )(v)


def gather_weights(shards, *, name):
    n = len(shards)
    half = [s.shape[0] // 2 for s in shards]
    widths = sorted({s.shape[1] for s in shards})
    chunk_rows = [_stage_rows(h, s.shape[1], itemsize=s.dtype.itemsize) for s, h in zip(shards, half)]
    plan = [(w, j, r0) for w in range(n) for j in range(N_CHIPS - 1) for r0 in range(0, half[w], chunk_rows[w])]

    def body(*refs):
        ins, outs = refs[:n], refs[n:2 * n]
        send_sems, recv_sems, local_sems, passed_sems, mine_sems = refs[2 * n:2 * n + 5]
        mine = refs[2 * n + 5:3 * n + 5]
        stage = refs[3 * n + 5:]
        bufs = {wd: stage[4 * i] for i, wd in enumerate(widths)}
        load_sems = {wd: stage[4 * i + 1] for i, wd in enumerate(widths)}
        stage_send = {wd: stage[4 * i + 2] for i, wd in enumerate(widths)}
        x, y, c = _position()
        me_chip = 2 * x + y
        chips = _other_chips(x, y)

        def piece(w, chip, core):
            return outs[w].at[chip, pl.ds(core * half[w], half[w]), :]

        def over_ici(w, j, chip, src=None):
            px, py = chips[j]
            dst = piece(w, chip, c)
            return pltpu.make_async_remote_copy(
                src_ref=dst if src is None else src, dst_ref=dst, send_sem=send_sems.at[3 * w + j],
                recv_sem=recv_sems.at[3 * w + j], device_id=(px, py, c), device_id_type=MESH)

        local = [pltpu.make_async_copy(ins[w], outs[w].at[me_chip], local_sems.at[w]) for w in range(n)]
        for cp in local:
            cp.start()
        loads = [pltpu.make_async_copy(ins[w].at[pl.ds(c * half[w], half[w]), :], mine[w], mine_sems.at[w])
                 for w in range(n)]
        for ld in loads:
            ld.start()
        sent = []
        for w in range(n):
            loads[w].wait()
            for j in range(N_CHIPS - 1):
                cp = over_ici(w, j, me_chip, src=mine[w])
                cp.start()
                sent.append(cp)
        chunks = []
        for idx, (w, j, r0) in enumerate(plan):
            wd, rb = shards[w].shape[1], chunk_rows[w]
            k = 2 * chips[j][0] + chips[j][1]

            def make(staged, slot, idx=idx, w=w, k=k, r0=r0, wd=wd, rb=rb):
                return pltpu.make_async_remote_copy(
                    src_ref=staged, dst_ref=outs[w].at[k, pl.ds(c * half[w] + r0, rb), :],
                    send_sem=stage_send[wd].at[slot], recv_sem=passed_sems.at[idx],
                    device_id=(x, y, 1 - c), device_id_type=MESH), True

            chunk = (wd, outs[w].at[k, pl.ds(c * half[w] + r0, rb), :], [make])
            if r0 == 0:
                chunk += (lambda w=w, j=j, k=k: over_ici(w, j, k).wait_recv(),)
            chunks.append(chunk)
        _staged(chunks, bufs, load_sems)
        for idx, (w, j, r0) in enumerate(plan):
            wd = shards[w].shape[1]
            k = 2 * chips[j][0] + chips[j][1]
            landed = outs[w].at[k, pl.ds((1 - c) * half[w] + r0, chunk_rows[w]), :]
            pltpu.make_async_remote_copy(
                src_ref=landed, dst_ref=landed, send_sem=stage_send[wd].at[0], recv_sem=passed_sems.at[idx],
                device_id=(x, y, 1 - c), device_id_type=MESH).wait_recv()
        for cp in sent:
            cp.wait_send()
        for cp in local:
            cp.wait()

    stage_rows = [(wd, max(r for s, r in zip(shards, chunk_rows) if s.shape[1] == wd)) for wd in widths]
    return _pcall(
        body, name=name,
        out_shape=[jax.ShapeDtypeStruct((N_CHIPS,) + s.shape, s.dtype) for s in shards],
        in_specs=[ANY] * n, out_specs=[ANY] * n,
        scratch_shapes=[pltpu.SemaphoreType.DMA((3 * n,)), pltpu.SemaphoreType.DMA((3 * n,)),
                        pltpu.SemaphoreType.DMA((n,)), pltpu.SemaphoreType.DMA((len(plan),)),
                        pltpu.SemaphoreType.DMA((n,))]
        + [pltpu.VMEM((h, s.shape[1]), s.dtype) for s, h in zip(shards, half)]
        + _stage_scratch(stage_rows, shards[0].dtype),
    )(*shards)


STAGE_BYTES = 2 << 20


def _stage_rows(rows, width, itemsize=4):
    return _pick(rows, tuple(t for t in (1024, 512, 256, 128, 64, 32, 16, 8) if t * width * itemsize <= STAGE_BYTES * 3 // 2))


def _staged(chunks, bufs, load_sems):
    count, pending = {}, {}

    def load(i):
        cls, src = chunks[i][0], chunks[i][1]
        if len(chunks[i]) > 3:
            chunks[i][3]()
        slot = count.get(cls, 0) % 2
        count[cls] = count.get(cls, 0) + 1
        for cp, remote in pending.pop((cls, slot), []):
            if remote:
                cp.wait_send()
            else:
                cp.wait()
        staged = bufs[cls].at[slot, pl.ds(0, src.shape[0]), :]
        ld = pltpu.make_async_copy(src, staged, load_sems[cls].at[slot])
        ld.start()
        return ld, cls, slot, staged

    cur = load(0)
    for i in range(len(chunks)):
        nxt = load(i + 1) if i + 1 < len(chunks) else None
        ld, cls, slot, staged = cur
        ld.wait()
        started = []
        for make in chunks[i][2]:
            cp, remote = make(staged, slot)
            cp.start()
            started.append((cp, remote))
        pending[(cls, slot)] = started
        cur = nxt
    for started in pending.values():
        for cp, remote in started:
            if remote:
                cp.wait_send()
            else:
                cp.wait()


def _stage_scratch(widths_rows, dtype):
    scratch = []
    for width, rows in widths_rows:
        scratch += [pltpu.VMEM((2, rows, width), dtype), pltpu.SemaphoreType.DMA((2,)), pltpu.SemaphoreType.DMA((2,)),
                    pltpu.SemaphoreType.DMA((2,))]
    return scratch


def send_half_to_sibling(grads, *, name):
    n = len(grads)
    half = [g.shape[1] // 2 for g in grads]
    widths = sorted({g.shape[2] for g in grads})
    chunk_rows = [_stage_rows(h, g.shape[2]) for g, h in zip(grads, half)]
    plan = [(w, k, r0) for w in range(n) for k in range(N_CHIPS) for r0 in range(0, half[w], chunk_rows[w])]

    def body(*refs):
        ins, theirs = refs[:n], refs[n:2 * n]
        recv_sems = refs[2 * n]
        stage = refs[2 * n + 1:]
        bufs = {wd: stage[4 * i] for i, wd in enumerate(widths)}
        load_sems = {wd: stage[4 * i + 1] for i, wd in enumerate(widths)}
        send_sems = {wd: stage[4 * i + 2] for i, wd in enumerate(widths)}
        x, y, c = _position()
        chunks = []
        for idx, (w, k, r0) in enumerate(plan):
            wd = grads[w].shape[2]
            rb = chunk_rows[w]

            def make(staged, slot, idx=idx, w=w, k=k, r0=r0, wd=wd, rb=rb):
                return pltpu.make_async_remote_copy(
                    src_ref=staged, dst_ref=theirs[w].at[k, pl.ds(r0, rb), :], send_sem=send_sems[wd].at[slot],
                    recv_sem=recv_sems.at[idx], device_id=(x, y, 1 - c), device_id_type=MESH), True

            chunks.append((wd, ins[w].at[k, pl.ds((1 - c) * half[w] + r0, rb), :], [make]))
        _staged(chunks, bufs, load_sems)
        for idx, (w, k, r0) in enumerate(plan):
            wd = grads[w].shape[2]
            landed = theirs[w].at[k, pl.ds(r0, chunk_rows[w]), :]
            pltpu.make_async_remote_copy(
                src_ref=landed, dst_ref=landed, send_sem=send_sems[wd].at[0], recv_sem=recv_sems.at[idx],
                device_id=(x, y, 1 - c), device_id_type=MESH).wait_recv()

    stage_rows = [(wd, max(r for g, r in zip(grads, chunk_rows) if g.shape[2] == wd)) for wd in widths]
    return _pcall(
        body, name=name,
        out_shape=[jax.ShapeDtypeStruct((N_CHIPS, h, g.shape[2]), g.dtype) for g, h in zip(grads, half)],
        in_specs=[ANY] * n, out_specs=[ANY] * n,
        scratch_shapes=[pltpu.SemaphoreType.DMA((len(plan),))] + _stage_scratch(stage_rows, F32),
    )(*grads)


def _chip_exchange_background(arrays, out_shapes, src_of, dst_of, landed_of, own_of, staged=None):
    n = len(arrays)
    peers = N_CHIPS - 1

    def copies(ins, outs, scr):
        send_sems, recv_sems, local_sems = scr[:3]
        x, y, c = _position()
        me_chip = 2 * x + y
        local, loads, sends, recvs = [], [], [], []
        for w in range(n):
            local.append(pltpu.make_async_copy(*own_of(ins[w], outs[w], w, me_chip), local_sems.at[w]))
            for j, (px, py) in enumerate(_other_chips(x, y)):
                sems = dict(send_sem=send_sems.at[peers * w + j], recv_sem=recv_sems.at[peers * w + j],
                            device_id=(px, py, c), device_id_type=MESH)
                src = src_of(ins[w], w, 2 * px + py, me_chip, c)
                if staged is not None:
                    buf = scr[4 + peers * w + j]
                    loads.append(pltpu.make_async_copy(src, buf, scr[3].at[peers * w + j]))
                    src = buf
                sends.append(pltpu.make_async_remote_copy(src_ref=src, dst_ref=dst_of(outs[w], w, me_chip, c), **sems))
                landed = landed_of(outs[w], w, 2 * px + py, c)
                recvs.append(pltpu.make_async_remote_copy(src_ref=landed, dst_ref=landed, **sems))
        return local, loads, sends, recvs

    def start(ins, outs, scr):
        local, loads, sends, _ = copies(ins, outs, scr)
        for cp in local + loads:
            cp.start()
        for cp in loads:
            cp.wait()
        for cp in sends:
            cp.start()

    def finish(ins, outs, scr):
        local, _, sends, recvs = copies(ins, outs, scr)
        for cp in recvs:
            cp.wait_recv()
        for cp in sends:
            cp.wait_send()
        for cp in local:
            cp.wait()

    scratch = [pltpu.SemaphoreType.DMA((peers * n,)), pltpu.SemaphoreType.DMA((peers * n,)),
               pltpu.SemaphoreType.DMA((n,))]
    if staged is not None:
        scratch.append(pltpu.SemaphoreType.DMA((peers * n,)))
        scratch += [pltpu.VMEM(shape, dtype) for shape, dtype in staged for _ in range(peers)]
    return Background(arrays, out_shapes, scratch, start, finish)


def scatter_background(parts, via_vmem=False):
    return _chip_exchange_background(
        parts, [jax.ShapeDtypeStruct(p.shape, p.dtype) for p in parts],
        src_of=lambda ref, w, peer, me, c: ref.at[peer], dst_of=lambda ref, w, me, c: ref.at[me],
        landed_of=lambda ref, w, peer, c: ref.at[peer], own_of=lambda i, o, w, me: (i.at[me], o.at[me]),
        staged=[(p.shape[1:], p.dtype) for p in parts] if via_vmem else None)


def gather_halves_background(shards):
    half = [s.shape[0] // 2 for s in shards]
    rows = lambda w, c: pl.ds(c * half[w], half[w])
    return _chip_exchange_background(
        shards, [jax.ShapeDtypeStruct((N_CHIPS,) + s.shape, s.dtype) for s in shards],
        src_of=lambda ref, w, peer, me, c: ref.at[rows(w, c), :], dst_of=lambda ref, w, me, c: ref.at[me, rows(w, c), :],
        landed_of=lambda ref, w, peer, c: ref.at[peer, rows(w, c), :], own_of=lambda i, o, w, me: (i, o.at[me]))


def fill_other_half(gathered, *, name):
    n = len(gathered)
    half = [g.shape[1] // 2 for g in gathered]
    widths = sorted({g.shape[2] for g in gathered})
    chunk_rows = [_stage_rows(h, g.shape[2], itemsize=2) for g, h in zip(gathered, half)]
    plan = [(w, j, r0) for w in range(n) for j in range(N_CHIPS - 1) for r0 in range(0, half[w], chunk_rows[w])]

    def body(*refs):
        ins, outs = refs[:n], refs[n:2 * n]
        recv_sems = refs[2 * n]
        stage = refs[2 * n + 1:]
        bufs = {wd: stage[4 * i] for i, wd in enumerate(widths)}
        load_sems = {wd: stage[4 * i + 1] for i, wd in enumerate(widths)}
        send_sems = {wd: stage[4 * i + 2] for i, wd in enumerate(widths)}
        x, y, c = _position()
        chips = _other_chips(x, y)
        chunks = []
        for idx, (w, j, r0) in enumerate(plan):
            wd, rb = gathered[w].shape[2], chunk_rows[w]
            k = 2 * chips[j][0] + chips[j][1]

            def make(staged, slot, idx=idx, w=w, k=k, r0=r0, wd=wd, rb=rb):
                return pltpu.make_async_remote_copy(
                    src_ref=staged, dst_ref=outs[w].at[k, pl.ds(c * half[w] + r0, rb), :],
                    send_sem=send_sems[wd].at[slot], recv_sem=recv_sems.at[idx],
                    device_id=(x, y, 1 - c), device_id_type=MESH), True

            chunks.append((wd, ins[w].at[k, pl.ds(c * half[w] + r0, rb), :], [make]))
        _staged(chunks, bufs, load_sems)
        for idx, (w, j, r0) in enumerate(plan):
            wd = gathered[w].shape[2]
            k = 2 * chips[j][0] + chips[j][1]
            landed = outs[w].at[k, pl.ds((1 - c) * half[w] + r0, chunk_rows[w]), :]
            pltpu.make_async_remote_copy(
                src_ref=landed, dst_ref=landed, send_sem=send_sems[wd].at[0], recv_sem=recv_sems.at[idx],
                device_id=(x, y, 1 - c), device_id_type=MESH).wait_recv()

    stage_rows = [(wd, max(r for g, r in zip(gathered, chunk_rows) if g.shape[2] == wd)) for wd in widths]
    return _pcall(
        body, name=name, out_shape=[jax.ShapeDtypeStruct(g.shape, g.dtype) for g in gathered],
        in_specs=[ANY] * n, out_specs=[ANY] * n, input_output_aliases={w: w for w in range(n)},
        scratch_shapes=[pltpu.SemaphoreType.DMA((len(plan),))] + _stage_scratch(stage_rows, gathered[0].dtype),
    )(*gathered)


def join_with_sibling(halves, *, name):
    n = len(halves)
    widths = sorted({h.shape[1] for h in halves})
    chunk_rows = [_stage_rows(h.shape[0], h.shape[1]) for h in halves]
    plan = [(w, r0) for w in range(n) for r0 in range(0, halves[w].shape[0], chunk_rows[w])]

    def body(*refs):
        ins, outs = refs[:n], refs[n:2 * n]
        recv_sems = refs[2 * n]
        stage = refs[2 * n + 1:]
        bufs = {wd: stage[4 * i] for i, wd in enumerate(widths)}
        load_sems = {wd: stage[4 * i + 1] for i, wd in enumerate(widths)}
        send_sems = {wd: stage[4 * i + 2] for i, wd in enumerate(widths)}
        store_sems = {wd: stage[4 * i + 3] for i, wd in enumerate(widths)}
        x, y, c = _position()
        chunks = []
        for idx, (w, r0) in enumerate(plan):
            h, wd = halves[w].shape
            rb = chunk_rows[w]

            def to_sibling(staged, slot, idx=idx, w=w, r0=r0, h=h, wd=wd, rb=rb):
                return pltpu.make_async_remote_copy(
                    src_ref=staged, dst_ref=outs[w].at[pl.ds(c * h + r0, rb), :], send_sem=send_sems[wd].at[slot],
                    recv_sem=recv_sems.at[idx], device_id=(x, y, 1 - c), device_id_type=MESH), True

            def to_mine(staged, slot, w=w, r0=r0, h=h, wd=wd, rb=rb):
                return pltpu.make_async_copy(staged, outs[w].at[pl.ds(c * h + r0, rb), :], store_sems[wd].at[slot]), False

            chunks.append((wd, ins[w].at[pl.ds(r0, rb), :], [to_sibling, to_mine]))
        _staged(chunks, bufs, load_sems)
        for idx, (w, r0) in enumerate(plan):
            h, wd = halves[w].shape
            landed = outs[w].at[pl.ds((1 - c) * h + r0, chunk_rows[w]), :]
            pltpu.make_async_remote_copy(
                src_ref=landed, dst_ref=landed, send_sem=send_sems[wd].at[0], recv_sem=recv_sems.at[idx],
                device_id=(x, y, 1 - c), device_id_type=MESH).wait_recv()

    stage_rows = [(wd, max(r for h, r in zip(halves, chunk_rows) if h.shape[1] == wd)) for wd in widths]
    return _pcall(
        body, name=name,
        out_shape=[jax.ShapeDtypeStruct((2 * h.shape[0], h.shape[1]), h.dtype) for h in halves],
        in_specs=[ANY] * n, out_specs=[ANY] * n,
        scratch_shapes=[pltpu.SemaphoreType.DMA((len(plan),))] + _stage_scratch(stage_rows, F32),
    )(*halves)


def _row_tile(rows, cols, itemsize=4, budget=2 << 20):
    for t in (1024, 512, 256, 128, 64, 32, 16, 8):
        if rows % t == 0 and t * cols * itemsize <= budget:
            return t
    return rows


def add_half_to_bf16(core, full, theirs, *, name):
    k, r, c = theirs.shape
    tr = _row_tile(r, c)
    nb = r // tr

    def body(core_ref, a_ref, b_ref, o_ref):
        o_ref[...] = (a_ref[...] + b_ref[...]).astype(BF16)

    spec = pl.BlockSpec((None, tr, c), lambda i, j, core_ref: (i, j, 0))
    grid_spec = pltpu.PrefetchScalarGridSpec(
        num_scalar_prefetch=1, grid=(k, nb),
        in_specs=[pl.BlockSpec((None, tr, c), lambda i, j, core_ref: (i, core_ref[0] * nb + j, 0)), spec],
        out_specs=spec)
    return _pcall(body, name=name, grid_spec=grid_spec,
                  out_shape=jax.ShapeDtypeStruct(theirs.shape, BF16))(core, full, theirs)


def sum_blocks(v, *, name):
    k, r, c = v.shape
    tr = _row_tile(r, c * k)

    def body(v_ref, o_ref):
        acc = v_ref[0].astype(F32)
        for j in range(1, k):
            acc = acc + v_ref[j].astype(F32)
        o_ref[...] = acc

    return _pcall(body, name=name, grid=(r // tr,),
                  in_specs=[pl.BlockSpec((k, tr, c), lambda i: (0, i, 0))],
                  out_specs=pl.BlockSpec((tr, c), lambda i: (i, 0)),
                  out_shape=jax.ShapeDtypeStruct((r, c), F32))(v)


def adamw(w, g, m, v, *, name):
    r, c = w.shape
    tr = _row_tile(r, c, budget=1 << 20)
    m_scale = 1.0 / (1.0 - ADAM_B1 ** ADAM_STEP)
    v_scale = 1.0 / (1.0 - ADAM_B2 ** ADAM_STEP)

    def body(w_ref, g_ref, m_ref, v_ref, d_ref, nm_ref, nv_ref):
        gv = g_ref[...]
        nm = ADAM_B1 * m_ref[...] + (1.0 - ADAM_B1) * gv
        nv = ADAM_B2 * v_ref[...] + (1.0 - ADAM_B2) * (gv * gv)
        nm_ref[...] = nm
        nv_ref[...] = nv
        d_ref[...] = -ADAM_LR * ((nm * m_scale) / (jnp.sqrt(nv * v_scale) + ADAM_EPS) + ADAM_WD * w_ref[...])

    spec = pl.BlockSpec((tr, c), lambda i: (i, 0))
    return _pcall(body, name=name, grid=(r // tr,), in_specs=[spec] * 4, out_specs=[spec] * 3,
                  out_shape=[jax.ShapeDtypeStruct((r, c), F32)] * 3)(w, g, m, v)


def ada_fwd(c_all, w_shard, b_shard, *, name):
    bsz, d = c_all.shape
    ncol = w_shard.shape[1]

    def body(c_ref, w_ref, b_ref, o_ref):
        cv = c_ref[...]
        act = (cv * _sigmoid(cv)).astype(BF16)
        o_ref[...] = _dot(act, w_ref[...].astype(BF16)) + b_ref[...]

    tn = _pick(ncol, (512, 256, 128))
    return _pcall(body, name=name, grid=(ncol // tn,),
                  in_specs=[pl.BlockSpec((bsz, d), lambda j: (0, 0)), pl.BlockSpec((d, tn), lambda j: (0, j)),
                            pl.BlockSpec((1, tn), lambda j: (0, j))],
                  out_specs=pl.BlockSpec((bsz, tn), lambda j: (0, j)),
                  out_shape=jax.ShapeDtypeStruct((bsz, ncol), F32))(c_all, w_shard, b_shard)


def ada_bwd(c_all, d_mod_all, d_mod_cols, *, name):
    bsz, d = c_all.shape
    ncol = d_mod_cols.shape[1]
    nall = d_mod_all.shape[1]

    def body(c_ref, da_ref, dc_ref, gw_ref, gb_ref):
        cv = c_ref[...]
        act = (cv * _sigmoid(cv)).astype(BF16)
        gw_ref[...] = _dot_tn(act, dc_ref[...].astype(BF16))
        gb_ref[...] = _colsum(da_ref[...])

    full = lambda s: pl.BlockSpec(s, lambda: (0,) * len(s))
    return _pcall(body, name=name,
                  in_specs=[full((bsz, d)), full((bsz, nall)), full((bsz, ncol))],
                  out_specs=[full((d, ncol)), full((1, nall))],
                  out_shape=[jax.ShapeDtypeStruct((d, ncol), F32), jax.ShapeDtypeStruct((1, nall), F32)],
                  )(c_all, d_mod_all, d_mod_cols)


WEIGHT_NAMES = ['w_ada', 'b_ada', 'pre_norm1', 'post_norm1', 'w_in', 'b_gate', 'lru_conv_w', 'lru_conv_b', 'lru_wa',
                'lru_ba', 'lru_wx', 'lru_bx', 'lru_lambda', 'w_pa', 'ssd_conv_w', 'ssd_conv_b', 'ssd_dt_bias',
                'ssd_a_log', 'ssd_d', 'ssd_norm_w', 'w_pb', 'w_out', 'pre_norm2', 'post_norm2', 'w_ff1', 'w_ff2']
BIG_NAMES = ['w_in', 'w_pa', 'w_pb', 'w_out', 'w_ff1', 'w_ff2']
COLUMN_SHARDED = ('w_in', 'w_ff1')
SMALL_NAMES = [n for n in WEIGHT_NAMES if n not in BIG_NAMES + ['w_ada', 'b_ada']]
SHARDED_SMALL = ('lru_conv_w', 'ssd_conv_w')
PACK_WIDTH = 1024


def _whole(name, gathered):
    if name in COLUMN_SHARDED:
        return jnp.transpose(gathered, (1, 0, 2)).reshape(gathered.shape[1], N_CHIPS * gathered.shape[2])
    return gathered.reshape(N_CHIPS * gathered.shape[1], gathered.shape[2])


def _by_chip(name, g):
    if name in COLUMN_SHARDED:
        return jnp.transpose(g.reshape(g.shape[0], N_CHIPS, g.shape[1] // N_CHIPS), (1, 0, 2))
    return g.reshape(N_CHIPS, g.shape[0] // N_CHIPS, g.shape[1])


class ChipExchange:
    def __init__(self, shards, core):
        self.shards, self.core = shards, core
        self.pending, self.halves = [], {}

    def weights_bg(self):
        return gather_halves_background(list(self.shards.values()))

    def weights(self, arrived):
        swapped = fill_other_half(arrived, name="weights_from_sibling")
        return {n: _whole(n, g) for n, g in zip(self.shards, swapped)}

    def grads_bg(self, grads, short_host=False):
        self.pending = list(grads)
        by_chip = [_by_chip(n, g) for n, g in grads.items()]
        theirs = send_half_to_sibling(by_chip, name="grads_to_sibling_" + self.pending[0])
        sums = [add_half_to_bf16(self.core, a, b, name="add_cores_" + n)
                for n, a, b in zip(self.pending, by_chip, theirs)]
        return scatter_background(sums, via_vmem=short_host)

    def grads_done(self, landed):
        for n, p in zip(self.pending, landed):
            self.halves[n] = sum_blocks(p, name="add_chips_" + n)

    def reduced(self):
        names = list(self.halves)
        return dict(zip(names, join_with_sibling([self.halves[n] for n in names], name="grads_join")))


def _pack(parts):
    flat = jnp.concatenate([p.reshape(-1).astype(F32) for p in parts])
    rows = -(-flat.shape[0] // (PACK_WIDTH * SUBLANES)) * SUBLANES
    return jnp.pad(flat, (0, rows * PACK_WIDTH - flat.shape[0])).reshape(rows, PACK_WIDTH)


def _unpack(packed, shapes):
    flat = packed.reshape(-1)
    out, pos = [], 0
    for s in shapes:
        size = int(np.prod(s))
        out.append(flat[pos:pos + size].reshape(s))
        pos += size
    return out


def kernel(x, c, w_ada, b_ada, pre_norm1, post_norm1, w_in, b_gate, lru_conv_w, lru_conv_b, lru_wa, lru_ba, lru_wx, lru_bx, lru_lambda, w_pa, ssd_conv_w, ssd_conv_b, ssd_dt_bias, ssd_a_log, ssd_d, ssd_norm_w, w_pb, w_out, pre_norm2, post_norm2, w_ff1, w_ff2, loss_target, m_w_ada, m_b_ada, m_pre_norm1, m_post_norm1, m_w_in, m_b_gate, m_lru_conv_w, m_lru_conv_b, m_lru_wa, m_lru_ba, m_lru_wx, m_lru_bx, m_lru_lambda, m_w_pa, m_ssd_conv_w, m_ssd_conv_b, m_ssd_dt_bias, m_ssd_a_log, m_ssd_d, m_ssd_norm_w, m_w_pb, m_w_out, m_pre_norm2, m_post_norm2, m_w_ff1, m_w_ff2, v_w_ada, v_b_ada, v_pre_norm1, v_post_norm1, v_w_in, v_b_gate, v_lru_conv_w, v_lru_conv_b, v_lru_wa, v_lru_ba, v_lru_wx, v_lru_bx, v_lru_lambda, v_w_pa, v_ssd_conv_w, v_ssd_conv_b, v_ssd_dt_bias, v_ssd_a_log, v_ssd_d, v_ssd_norm_w, v_w_pb, v_w_out, v_pre_norm2, v_post_norm2, v_w_ff1, v_w_ff2):
    given = dict(locals())
    bsz, seq, d = x.shape
    my_x, my_y, my_c = lax.axis_index("x"), lax.axis_index("y"), lax.axis_index("c")
    chip = 2 * my_x + my_y
    dev = 2 * chip + my_c
    strip = lambda a: a if a.ndim == 2 else a[0]
    w = {n: strip(given[n]) for n in WEIGHT_NAMES}
    m = {n: strip(given["m_" + n]) for n in WEIGHT_NAMES}
    v = {n: strip(given["v_" + n]) for n in WEIGHT_NAMES}

    first_shapes = [c.shape] + [w[n].shape for n in SHARDED_SMALL]
    first = allgather8(_pack([c] + [w[n] for n in SHARDED_SMALL]), name="gather_c_conv")
    first = first.reshape(N_DEV, -1, PACK_WIDTH)
    per_dev = [_unpack(first[k], first_shapes) for k in range(N_DEV)]
    c_all = jnp.concatenate([p[0] for p in per_dev], axis=0)
    conv_full = {n: jnp.concatenate([per_dev[2 * k][1 + i] for k in range(N_CHIPS)], axis=1)
                 for i, n in enumerate(SHARDED_SMALL)}

    ncol = w["w_ada"].shape[1]
    b_cols = lax.dynamic_slice(b_ada, (0, chip * ncol), (1, ncol))
    mod_cols = ada_fwd(c_all, w["w_ada"], b_cols, name="ada_fwd")
    mod_all = allgather8(mod_cols, name="gather_mod").reshape(N_CHIPS, 2, N_DEV * bsz, ncol)[:, 0]
    mod_all = jnp.transpose(mod_all, (1, 0, 2)).reshape(N_DEV * bsz, N_CHIPS * ncol)
    mod = lax.dynamic_slice(mod_all, (dev * bsz, 0), (bsz, 6 * d)).reshape(bsz, 6, d)
    mod = jnp.pad(mod, ((0, 0), (0, 2), (0, 0)))

    w_in_full = _whole("w_in", gather_weights([w["w_in"].astype(BF16)], name="gather_w_in")[0])
    big = {"w_main": w_in_full[:, :8192],
           "w_dt": jnp.pad(w_in_full[:, 8192:8192 + SSD_HEADS], ((0, 0), (0, LANES - SSD_HEADS))),
           "w_gates": w_in_full[:, 8192 + SSD_HEADS:]}
    small = {n: w[n] for n in SMALL_NAMES}
    small.update(conv_full)
    plan = ChipExchange({n: w[n].astype(BF16) for n in BIG_NAMES if n != "w_in"}, my_c.astype(jnp.int32).reshape(1))

    loss_cols, grad_x, d_mod, small_grads = local_step(x, loss_target, mod, big, small, plan)

    packed = _pack([d_mod, loss_cols] + [small_grads[n] for n in SMALL_NAMES])
    rows = packed.shape[0]
    everyone = allgather8(packed, name="gather_small").reshape(N_DEV, rows, PACK_WIDTH)
    d_mod_all = everyone[:, :bsz * 6].reshape(N_DEV * bsz, 6 * d)
    summed = sum_blocks(everyone, name="sum_small")
    shapes = [d_mod.shape, loss_cols.shape] + [small_grads[n].shape for n in SMALL_NAMES]
    parts = _unpack(summed, shapes)
    loss = jnp.sum(parts[1])
    grads = dict(zip(SMALL_NAMES, parts[2:]))
    for n in SHARDED_SMALL:
        cols = w[n].shape[1]
        grads[n] = lax.dynamic_slice(grads[n], (0, chip * cols), (grads[n].shape[0], cols))
    d_mod_cols = lax.dynamic_slice(d_mod_all, (0, chip * ncol), (N_DEV * bsz, ncol))
    grads["w_ada"], grads["b_ada"] = ada_bwd(c_all, d_mod_all, d_mod_cols, name="ada_bwd")

    grads.update(plan.reduced())

    delta, new_m, new_v = {}, {}, {}
    for n in BIG_NAMES + ["w_ada", "b_ada"]:
        delta[n], new_m[n], new_v[n] = adamw(w[n], grads[n], m[n], v[n], name="adamw_" + n)
    shapes = [w[n].shape for n in SMALL_NAMES]
    pk = lambda src: _pack([src[n] for n in SMALL_NAMES])
    upd = adamw(pk(w), pk(grads), pk(m), pk(v), name="adamw_small")
    for out, packed_out in zip((delta, new_m, new_v), upd):
        out.update(zip(SMALL_NAMES, _unpack(packed_out, shapes)))

    shaped = lambda src: [src[n].reshape(given[n].shape) for n in WEIGHT_NAMES]
    return (loss, grad_x, *shaped(grads), *shaped(delta), *shaped(new_m), *shaped(new_v))
```

```python
import functools
import math

import numpy as np
import jax
import jax.numpy as jnp
from jax import lax
from jax.experimental import pallas as pl
from jax.experimental.pallas import tpu as pltpu

F32 = jnp.float32
BF16 = jnp.bfloat16
HI = lax.Precision.HIGHEST
MESH = pl.DeviceIdType.MESH

D_MODEL = 1024
LRU_HEADS = 16
LRU_HEAD_DIM = 64
LRU_C = 8.0
SSD_INNER = 2048
SSD_HEADS = 32
SSD_HEAD_DIM = 64
SSD_GROUPS = 8
SSD_STATE = 128
SSD_CHUNK = 128
SSD_CONV_DIM = 4096
D_FF = 4096
EPS = 1e-6
N_CHIPS = 4
N_DEV = 8
LANES = 128
SUBLANES = 8

ADAM_LR = 0.001
ADAM_B1 = 0.9
ADAM_B2 = 0.999
ADAM_EPS = 1e-08
ADAM_WD = 0.01
ADAM_STEP = 10


ANY = pl.BlockSpec(memory_space=pl.ANY)


def _pcall(body, **kw):
    return pl.pallas_call(body, **kw)


class Background:
    def __init__(self, inputs, out_shapes, scratch, start, finish):
        self.inputs, self.out_shapes, self.scratch = list(inputs), list(out_shapes), list(scratch)
        self.start, self.finish = start, finish

    def wrap(self, body, kw):
        n_in, n_out = len(kw["in_specs"]), len(kw["out_specs"])
        n_scr = len(kw.get("scratch_shapes", []))
        b_in, b_out = len(self.inputs), len(self.out_shapes)
        grid = kw["grid"]

        def wrapped(*refs):
            ins, b_ins = refs[:n_in], refs[n_in:n_in + b_in]
            o0 = n_in + b_in
            outs, b_outs = refs[o0:o0 + n_out], refs[o0 + n_out:o0 + n_out + b_out]
            s0 = o0 + n_out + b_out
            scr, b_scr = refs[s0:s0 + n_scr], refs[s0 + n_scr:]
            ids = [pl.program_id(a) for a in range(len(grid))]
            first = functools.reduce(jnp.logical_and, [i == 0 for i in ids])
            last = functools.reduce(jnp.logical_and, [i == g - 1 for i, g in zip(ids, grid)])

            @pl.when(first)
            def _():
                self.start(b_ins, b_outs, b_scr)

            body(*ins, *outs, *scr)

            @pl.when(last)
            def _():
                self.finish(b_ins, b_outs, b_scr)

        kw = dict(kw, in_specs=list(kw["in_specs"]) + [ANY] * b_in, out_specs=list(kw["out_specs"]) + [ANY] * b_out,
                  out_shape=list(kw["out_shape"]) + self.out_shapes,
                  scratch_shapes=list(kw.get("scratch_shapes", [])) + self.scratch)
        return wrapped, kw


def _run(body, args, bg, **kw):
    n_out = len(kw["out_shape"])
    if bg is None:
        return list(_pcall(body, **kw)(*args)), []
    body, kw = bg.wrap(body, kw)
    outs = _pcall(body, **kw)(*args, *bg.inputs)
    return list(outs[:n_out]), list(outs[n_out:])


def _sigmoid(v):
    return 1.0 / (1.0 + jnp.exp(-v))


def _log1p(u):
    return jnp.where(u < 1e-3, u * (1.0 - u * (0.5 - u * (1.0 / 3.0))), jnp.log(1.0 + u))


def _softplus(v):
    return jnp.maximum(v, 0.0) + _log1p(jnp.exp(-jnp.abs(v)))


def _neg_expm1(v):
    small = -v * (1.0 + v * (0.5 + v * (1.0 / 6.0 + v * (1.0 / 24.0))))
    return jnp.where(v > -0.05, small, 1.0 - jnp.exp(v))


_GELU_K = math.sqrt(2.0 / math.pi)


def _gelu(v):
    t = jnp.tanh(_GELU_K * (v + 0.044715 * v * v * v))
    return 0.5 * v * (1.0 + t)


def _gelu_grad(v):
    t = jnp.tanh(_GELU_K * (v + 0.044715 * v * v * v))
    return 0.5 * (1.0 + t) + 0.5 * v * (1.0 - t * t) * _GELU_K * (1.0 + 3.0 * 0.044715 * v * v)


def _colsum(v):
    return jnp.sum(v, axis=0, keepdims=True)


def _dot(a, b, precision=None):
    return lax.dot_general(a, b, (((1,), (0,)), ((), ())), preferred_element_type=F32, precision=precision)


def _dot_nt(a, b):
    return lax.dot_general(a, b, (((1,), (1,)), ((), ())), preferred_element_type=F32)


def _dot_tn(a, b):
    return lax.dot_general(a, b, (((0,), (0,)), ((), ())), preferred_element_type=F32)


def _shift_down(xt, prev8, j):
    if j == 0:
        return xt
    n = xt.shape[0]
    r = pltpu.roll(xt, j, 0)
    p = pltpu.roll(prev8, j, 0)
    rows = lax.broadcasted_iota(jnp.int32, (SUBLANES, xt.shape[1]), 0)
    top = jnp.where(rows < j, p, r[0:SUBLANES])
    if n == SUBLANES:
        return top
    return jnp.concatenate([top, r[SUBLANES:]], axis=0)


def _shift_up(xt, next8, j):
    if j == 0:
        return xt
    n = xt.shape[0]
    r = pltpu.roll(xt, n - j, 0)
    p = pltpu.roll(next8, SUBLANES - j, 0)
    rows = lax.broadcasted_iota(jnp.int32, (SUBLANES, xt.shape[1]), 0)
    bot = jnp.where(rows >= SUBLANES - j, p, r[n - SUBLANES:])
    if n == SUBLANES:
        return bot
    return jnp.concatenate([r[:n - SUBLANES], bot], axis=0)


def _conv4(xt, prev8, w, b):
    out = b + w[3:4] * xt
    for k in range(3):
        out = out + w[k:k + 1] * _shift_down(xt, prev8, 3 - k)
    return out


def _conv4_bwd(d_out, next8, xt, w):
    d_x = w[3:4] * d_out
    d_w = []
    for k in range(3):
        up = _shift_up(d_out, next8, 3 - k)
        d_x = d_x + w[k:k + 1] * up
        d_w.append(_colsum(up * xt))
    d_w.append(_colsum(d_out * xt))
    return d_x, d_w, _colsum(d_out)


def _stack_rows(rows, width):
    rows = list(rows) + [jnp.zeros((1, width), F32)] * (SUBLANES - len(rows))
    return jnp.concatenate(rows, axis=0)


def _pick(n, cands):
    for c in cands:
        if n % c == 0:
            return c
    raise ValueError(f"no tile for {n}")


MM_ROWS = 1024
MM_VMEM_BUDGET = 36 << 20
MM_PANEL_COLS = 2048
MM_SUB = 512


def mm_nn(pairs, *, name, out_dtype=F32, a_fn=None, add=None, epi=None, extra=None, bg=None):
    np_ = len(pairs)
    m, n = pairs[0][0].shape[0], pairs[0][1].shape[1]
    pn = n if n <= MM_PANEL_COLS else _pick(n, (MM_PANEL_COLS, 1024, 512, 256, 128))
    ns = _pick(pn, (MM_SUB, 256, 128))
    adds = list(add or ())
    has_extra = extra is not None
    stage0 = a_fn is not None or pairs[0][0].dtype != BF16

    def vmem_bytes(rows):
        tiles = sum(rows * a.shape[1] * a.dtype.itemsize for a, _ in pairs)
        tiles += rows * pn * (4 * len(adds) + (extra.dtype.itemsize if has_extra else 0) + jnp.dtype(out_dtype).itemsize)
        panels = sum(b.shape[0] * pn * b.dtype.itemsize for _, b in pairs)
        return 2 * (tiles + panels) + (rows * pairs[0][0].shape[1] * 2 if stage0 else 0)

    tm = _pick(m, (MM_ROWS, 512, 256, 128, 64, 32, 16, 8))
    if vmem_bytes(tm) > MM_VMEM_BUDGET:
        tm = _pick(m, (512, 256, 128, 64, 32, 16, 8))

    def body(*refs):
        a_refs, b_refs = refs[:np_], refs[np_:2 * np_]
        pos = 2 * np_
        extra_ref = None
        add_refs = refs[pos:pos + len(adds)]
        pos += len(adds)
        if has_extra:
            extra_ref = refs[pos]
            pos += 1
        o_ref = refs[pos]
        lhs = list(a_refs)
        if stage0:
            av = a_refs[0][...]
            if a_fn is not None:
                av = a_fn(av)
            refs[pos + 1][...] = av.astype(BF16)
            lhs[0] = refs[pos + 1]
        for n0 in range(0, pn, ns):
            sl = slice(n0, n0 + ns)
            acc = None
            for a_ref, b_ref in zip(lhs, b_refs):
                part = _dot(a_ref[...].astype(BF16), b_ref[:, sl])
                acc = part if acc is None else acc + part
            for add_ref in add_refs:
                acc = acc + add_ref[:, sl]
            if epi is not None:
                acc = epi(acc, extra_ref[:, sl]) if has_extra else epi(acc)
            o_ref[:, sl] = acc.astype(out_dtype)

    in_specs = [pl.BlockSpec((tm, a.shape[1]), lambda j, i: (i, 0)) for a, _ in pairs]
    in_specs += [pl.BlockSpec((b.shape[0], pn), lambda j, i: (0, j)) for _, b in pairs]
    args = [a for a, _ in pairs] + [b for _, b in pairs]
    tile = pl.BlockSpec((tm, pn), lambda j, i: (i, j))
    for extra_add in adds:
        in_specs.append(tile)
        args.append(extra_add)
    if has_extra:
        in_specs.append(tile)
        args.append(extra)
    outs, bg_outs = _run(
        body, args, bg, name=name, grid=(n // pn, m // tm), in_specs=in_specs, out_specs=[tile],
        out_shape=[jax.ShapeDtypeStruct((m, n), out_dtype)],
        scratch_shapes=[pltpu.VMEM((tm, pairs[0][0].shape[1]), BF16)] if stage0 else [])
    return outs[0] if bg is None else (outs[0], bg_outs)


MM_REDUCE_ROWS = 1024
MM_GRAD_ROWS = 1024
MM_GRAD_COLS = 2048


def mm_tn(a, b, *, name, a_fn=None):
    m, ka = a.shape
    nb = b.shape[1]
    pa = _pick(ka, (MM_GRAD_ROWS, 512, 256, 128))
    pb = nb if nb <= MM_GRAD_COLS else _pick(nb, (MM_GRAD_COLS, 1024, 512, 256, 128))
    ns = _pick(pb, (MM_SUB, 256, 128))
    tmk = _pick(m, (MM_REDUCE_ROWS, 512, 256, 128, 64, 32, 16))

    def body(a_ref, b_ref, o_ref, lhs):
        k = pl.program_id(2)

        @pl.when(k == 0)
        def _():
            o_ref[...] = jnp.zeros_like(o_ref)

        av = a_ref[...]
        if a_fn is not None:
            av = a_fn(av)
        lhs[...] = av.astype(BF16)
        for n0 in range(0, pb, ns):
            o_ref[:, n0:n0 + ns] += _dot_tn(lhs[...], b_ref[:, n0:n0 + ns].astype(BF16))

    return _pcall(
        body, name=name,
        grid=(ka // pa, nb // pb, m // tmk),
        in_specs=[pl.BlockSpec((tmk, pa), lambda i, j, k: (k, i)),
                  pl.BlockSpec((tmk, pb), lambda i, j, k: (k, j))],
        out_specs=pl.BlockSpec((pa, pb), lambda i, j, k: (i, j)),
        out_shape=jax.ShapeDtypeStruct((ka, nb), F32),
        scratch_shapes=[pltpu.VMEM((tmk, pa), BF16)],
    )(a, b)


def _relu_sq(v):
    r = jnp.maximum(v, 0.0)
    return r * r


ROW_TILE = 512


def _row_specs(bsz, seq, width, ts):
    return pl.BlockSpec((None, ts, width), lambda b, i: (b, i, 0))


def _vec_spec(width):
    return pl.BlockSpec((1, width), lambda b, i: (0, 0))


def _mod_spec():
    return pl.BlockSpec((None, SUBLANES, D_MODEL), lambda b, i: (b, 0, 0))


def _rstd(v):
    return lax.rsqrt(jnp.mean(v * v, axis=-1, keepdims=True) + EPS)


def prenorm(x, w, mod, *, name):
    bsz, seq, d = x.shape
    ts = _pick(seq, (ROW_TILE, 256, 128))

    def body(x_ref, w_ref, mod_ref, h_ref):
        xv = x_ref[...]
        m = mod_ref[...]
        xh = xv * _rstd(xv)
        h_ref[...] = ((xh * w_ref[...]) * (1.0 + m[1:2]) + m[0:1]).astype(BF16)

    return _pcall(
        body, name=name, grid=(bsz, seq // ts),
        in_specs=[_row_specs(bsz, seq, d, ts), _vec_spec(d), _mod_spec()],
        out_specs=_row_specs(bsz, seq, d, ts),
        out_shape=jax.ShapeDtypeStruct((bsz, seq, d), BF16),
    )(x, w, mod)


def post1_pre2(x, out1, mod, post1, pre2, *, name):
    bsz, seq, d = x.shape
    ts = _pick(seq, (ROW_TILE, 256, 128))

    def body(x_ref, o_ref, mod_ref, p1_ref, p2_ref, x1_ref, h2_ref):
        m = mod_ref[...]
        ov = o_ref[...]
        x1 = x_ref[...] + m[2:3] * ((ov * _rstd(ov)) * p1_ref[...])
        x1_ref[...] = x1
        xh = x1 * _rstd(x1)
        h2_ref[...] = ((xh * p2_ref[...]) * (1.0 + m[4:5]) + m[3:4]).astype(BF16)

    return _pcall(
        body, name=name, grid=(bsz, seq // ts),
        in_specs=[_row_specs(bsz, seq, d, ts), _row_specs(bsz, seq, d, ts), _mod_spec(), _vec_spec(d), _vec_spec(d)],
        out_specs=[_row_specs(bsz, seq, d, ts), _row_specs(bsz, seq, d, ts)],
        out_shape=[jax.ShapeDtypeStruct((bsz, seq, d), F32), jax.ShapeDtypeStruct((bsz, seq, d), BF16)],
    )(x, out1, mod, post1, pre2)


def _acc_specs(d):
    per_batch = pl.BlockSpec((None, SUBLANES, d), lambda b, i: (b, 0, 0))
    glob = pl.BlockSpec((SUBLANES, d), lambda b, i: (0, 0))
    return per_batch, glob


def _accumulate(pb_ref, gl_ref, pb_rows, gl_rows, width):
    b, i = pl.program_id(0), pl.program_id(1)

    @pl.when(i == 0)
    def _():
        pb_ref[...] = jnp.zeros_like(pb_ref)

    @pl.when((b == 0) & (i == 0))
    def _():
        gl_ref[...] = jnp.zeros_like(gl_ref)

    pb_ref[...] += _stack_rows(pb_rows, width)
    gl_ref[...] += _stack_rows(gl_rows, width)


def _rms_bwd(d_n, n, r):
    return r * (d_n - n * jnp.mean(d_n * n, axis=-1, keepdims=True))


def final_bwd(x1, y2, target, mod, post2, *, name):
    bsz, seq, d = x1.shape
    ts = _pick(seq, (ROW_TILE, 256, 128))

    def body(x1_ref, y_ref, t_ref, mod_ref, p_ref, dx_ref, dy_ref, pb_ref, gl_ref):
        m = mod_ref[...]
        g2 = m[5:6]
        yv = y_ref[...]
        r = _rstd(yv)
        n = yv * r
        o = n * p_ref[...]
        diff = (x1_ref[...] + g2 * o) - t_ref[...]
        dx = diff * (1.0 / d)
        dx_ref[...] = dx
        d_o = dx * g2
        dy_ref[...] = _rms_bwd(d_o * p_ref[...], n, r).astype(BF16)
        _accumulate(pb_ref, gl_ref, [_colsum(dx * o)], [_colsum(d_o * n), _colsum(diff * diff) * (0.5 / d)], d)

    pb, gl = _acc_specs(d)
    rs = _row_specs(bsz, seq, d, ts)
    return _pcall(
        body, name=name, grid=(bsz, seq // ts),
        in_specs=[rs, rs, rs, _mod_spec(), _vec_spec(d)],
        out_specs=[rs, rs, pb, gl],
        out_shape=[jax.ShapeDtypeStruct((bsz, seq, d), F32), jax.ShapeDtypeStruct((bsz, seq, d), BF16),
                   jax.ShapeDtypeStruct((bsz, SUBLANES, d), F32), jax.ShapeDtypeStruct((SUBLANES, d), F32)],
    )(x1, y2, target, mod, post2)


def mid_bwd(d_h2, dx2, x1, out1, mod, pre2, post1, *, name):
    bsz, seq, d = x1.shape
    ts = _pick(seq, (ROW_TILE, 256, 128))

    def body(dh_ref, dx2_ref, x1_ref, o_ref, mod_ref, p2_ref, p1_ref, dx1_ref, do_ref, pb_ref, gl_ref):
        m = mod_ref[...]
        dh = dh_ref[...]
        x1 = x1_ref[...]
        r2 = _rstd(x1)
        xh = x1 * r2
        xw = xh * p2_ref[...]
        d_xw = dh * (1.0 + m[4:5])
        dx1 = dx2_ref[...] + _rms_bwd(d_xw * p2_ref[...], xh, r2)
        dx1_ref[...] = dx1
        ov = o_ref[...]
        r1 = _rstd(ov)
        n1 = ov * r1
        o1 = n1 * p1_ref[...]
        d_o1 = dx1 * m[2:3]
        do_ref[...] = _rms_bwd(d_o1 * p1_ref[...], n1, r1).astype(BF16)
        _accumulate(pb_ref, gl_ref, [_colsum(dh), _colsum(dh * xw), _colsum(dx1 * o1)],
                    [_colsum(d_xw * xh), _colsum(d_o1 * n1)], d)

    pb, gl = _acc_specs(d)
    rs = _row_specs(bsz, seq, d, ts)
    return _pcall(
        body, name=name, grid=(bsz, seq // ts),
        in_specs=[rs, rs, rs, rs, _mod_spec(), _vec_spec(d), _vec_spec(d)],
        out_specs=[rs, rs, pb, gl],
        out_shape=[jax.ShapeDtypeStruct((bsz, seq, d), F32), jax.ShapeDtypeStruct((bsz, seq, d), BF16),
                   jax.ShapeDtypeStruct((bsz, SUBLANES, d), F32), jax.ShapeDtypeStruct((SUBLANES, d), F32)],
    )(d_h2, dx2, x1, out1, mod, pre2, post1)


def first_bwd(d_h1, dx1, x, mod, pre1, *, name):
    bsz, seq, d = x.shape
    ts = _pick(seq, (ROW_TILE, 256, 128))

    def body(dh_ref, dx1_ref, x_ref, mod_ref, p_ref, gx_ref, pb_ref, gl_ref):
        m = mod_ref[...]
        dh = dh_ref[...]
        xv = x_ref[...]
        r = _rstd(xv)
        xh = xv * r
        xw = xh * p_ref[...]
        d_xw = dh * (1.0 + m[1:2])
        gx_ref[...] = dx1_ref[...] + _rms_bwd(d_xw * p_ref[...], xh, r)
        _accumulate(pb_ref, gl_ref, [_colsum(dh), _colsum(dh * xw)], [_colsum(d_xw * xh)], d)

    pb, gl = _acc_specs(d)
    rs = _row_specs(bsz, seq, d, ts)
    return _pcall(
        body, name=name, grid=(bsz, seq // ts),
        in_specs=[rs, rs, rs, _mod_spec(), _vec_spec(d)],
        out_specs=[rs, pb, gl],
        out_shape=[jax.ShapeDtypeStruct((bsz, seq, d), F32),
                   jax.ShapeDtypeStruct((bsz, SUBLANES, d), F32), jax.ShapeDtypeStruct((SUBLANES, d), F32)],
    )(d_h1, dx1, x, mod, pre1)


def merge_bwd(d_merged, ya, yb, gates, b_gate, *, name):
    bsz, seq, d = ya.shape
    ts = _pick(seq, (ROW_TILE, 256, 128))

    def body(dm_ref, ya_ref, yb_ref, g_ref, b_ref, dya_ref, dyb_ref, dg_ref, gl_ref):
        b, i = pl.program_id(0), pl.program_id(1)
        g = _sigmoid(g_ref[...] + b_ref[...])
        dm = dm_ref[...]
        ga, gb = g[:, :d], g[:, d:]
        dya_ref[...] = (dm * ga).astype(BF16)
        dyb_ref[...] = (dm * gb).astype(BF16)
        dg = jnp.concatenate([dm * ya_ref[...] * ga * (1.0 - ga), dm * yb_ref[...] * gb * (1.0 - gb)], axis=1)
        dg_ref[...] = dg.astype(BF16)

        @pl.when((b == 0) & (i == 0))
        def _():
            gl_ref[...] = jnp.zeros_like(gl_ref)

        gl_ref[...] += _stack_rows([_colsum(dg)], 2 * d)

    rs = _row_specs(bsz, seq, d, ts)
    rs2 = _row_specs(bsz, seq, 2 * d, ts)
    return _pcall(
        body, name=name, grid=(bsz, seq // ts),
        in_specs=[rs, rs, rs, rs2, _vec_spec(2 * d)],
        out_specs=[rs, rs, rs2, pl.BlockSpec((SUBLANES, 2 * d), lambda b, i: (0, 0))],
        out_shape=[jax.ShapeDtypeStruct((bsz, seq, d), BF16), jax.ShapeDtypeStruct((bsz, seq, d), BF16),
                   jax.ShapeDtypeStruct((bsz, seq, 2 * d), BF16), jax.ShapeDtypeStruct((SUBLANES, 2 * d), F32)],
    )(d_merged, ya, yb, gates, b_gate)


LRU_TILE = 256
N_LRU_BLOCKS = D_MODEL // LANES


def _block_mm(v, w_ref, transpose=False):
    vb = v.astype(BF16)
    outs = []
    for j in range(N_LRU_BLOCKS):
        blk = vb[:, LANES * j:LANES * (j + 1)]
        outs.append(_dot_nt(blk, w_ref[j]) if transpose else _dot(blk, w_ref[j]))
    return jnp.concatenate(outs, axis=1)


def _lru_gates(xc, wa_ref, ba, wx_ref, bx, sp):
    r = _sigmoid(_block_mm(xc, wa_ref) + ba)
    i = _sigmoid(_block_mm(xc, wx_ref) + bx)
    la = (-LRU_C * r) * sp
    a = jnp.exp(la)
    sq = jnp.sqrt(_neg_expm1(2.0 * la))
    return r, i, a, sq


def _prev8_spec(width, col_block, tile_rows):
    per = tile_rows // SUBLANES
    return pl.BlockSpec((None, SUBLANES, width), lambda b, i: (b, jnp.maximum(i * per - 1, 0), col_block))


def lru_fwd(pm, cw, cb, wa, ba, wx, bx, lam, w_pa, *, name):
    bsz, seq, _ = pm.shape
    d = D_MODEL
    ts = _pick(seq, (LRU_TILE, 128))

    def body(lx_ref, lxp_ref, lg_ref, cw_ref, cb_ref, wa_ref, ba_ref, wx_ref, bx_ref, lam_ref, wpa_ref,
             h_ref, pa_ref, ya_ref, kept_ref, hc, a_s, u_s):
        i = pl.program_id(1)

        @pl.when(i == 0)
        def _():
            hc[...] = jnp.zeros_like(hc)

        lx = lx_ref[...]
        prev8 = jnp.where(i == 0, 0.0, lxp_ref[...])
        xc = _conv4(lx, prev8, cw_ref[...], cb_ref[...])
        sp = _softplus(-lam_ref[...])
        r, ig, a, sq = _lru_gates(xc, wa_ref, ba_ref[...], wx_ref, bx_ref[...], sp)
        for k, kept in enumerate((xc, r, ig, a, sq)):
            kept_ref[:, k * d:(k + 1) * d] = kept
        a_s[...] = a
        u_s[...] = sq * (ig * xc)

        def step(g, h):
            r0 = pl.multiple_of(g * SUBLANES, SUBLANES)
            a8 = a_s[pl.ds(r0, SUBLANES), :]
            u8 = u_s[pl.ds(r0, SUBLANES), :]
            rows = []
            for j in range(SUBLANES):
                h = a8[j:j + 1] * h + u8[j:j + 1]
                rows.append(h)
            h_ref[pl.ds(r0, SUBLANES), :] = jnp.concatenate(rows, axis=0)
            return h

        hc[...] = lax.fori_loop(0, ts // SUBLANES, step, hc[...])
        pa_ref[...] = (h_ref[...] * _gelu(lg_ref[...])).astype(BF16)
        ya_ref[...] = _dot(pa_ref[...], wpa_ref[...])

    vec = _vec_spec(d)
    wspec = pl.BlockSpec((N_LRU_BLOCKS, LANES, LANES), lambda b, i: (0, 0, 0))
    rs = _row_specs(bsz, seq, d, ts)
    return _pcall(
        body, name=name, grid=(bsz, seq // ts),
        in_specs=[pl.BlockSpec((None, ts, d), lambda b, i: (b, i, 0)), _prev8_spec(d, 0, ts),
                  pl.BlockSpec((None, ts, d), lambda b, i: (b, i, 1)),
                  pl.BlockSpec((4, d), lambda b, i: (0, 0)), vec, wspec, vec, wspec, vec, vec,
                  pl.BlockSpec(w_pa.shape, lambda b, i: (0, 0))],
        out_specs=[rs, rs, rs, _row_specs(bsz, seq, 5 * d, ts)],
        out_shape=[jax.ShapeDtypeStruct((bsz, seq, d), F32), jax.ShapeDtypeStruct((bsz, seq, d), BF16),
                   jax.ShapeDtypeStruct((bsz, seq, d), F32), jax.ShapeDtypeStruct((bsz, seq, 5 * d), F32)],
        scratch_shapes=[pltpu.VMEM((1, d), F32), pltpu.VMEM((ts, d), F32), pltpu.VMEM((ts, d), F32)],
    )(pm, pm, pm, cw, cb, wa, ba, wx, bx, lam, w_pa)


def lru_bwd(pm, h, kept, d_ya, cw, wa, wx, lam, wt_pa, wt_lru, *, name, bg=None):
    bsz, seq, _ = pm.shape
    d = D_MODEL
    ts = _pick(seq, (LRU_TILE, 128))
    nt = seq // ts
    per = ts // SUBLANES

    def rev(i):
        return nt - 1 - i

    def body(lx_ref, lg_ref, h_ref, hp_ref, kept_ref, dya_ref, cw_ref, wa_ref, wx_ref,
             lam_ref, wtpa_ref, wtl_ref, dl_ref, dh1_ref, dwa_ref, dwx_ref, rows_ref,
             carry, dxc_next, a_s, dh_s, acc_s):
        b, i = pl.program_id(0), pl.program_id(1)
        t = rev(i)

        @pl.when(i == 0)
        def _():
            carry[...] = jnp.zeros_like(carry)
            dxc_next[...] = jnp.zeros_like(dxc_next)

        @pl.when((b == 0) & (i == 0))
        def _():
            dwa_ref[...] = jnp.zeros_like(dwa_ref)
            dwx_ref[...] = jnp.zeros_like(dwx_ref)
            rows_ref[...] = jnp.zeros_like(rows_ref)

        lx = lx_ref[...]
        lg = lg_ref[...]
        cwv = cw_ref[...]
        lam_v = lam_ref[...]
        sp = _softplus(-lam_v)
        xc, r, ig, a, sq = (kept_ref[:, k * d:(k + 1) * d] for k in range(5))
        hv = h_ref[...]
        d_pa = _dot(dya_ref[...], wtpa_ref[...])
        a_s[...] = a
        dh_s[...] = d_pa * _gelu(lg)

        def step(g, c):
            r0 = pl.multiple_of((per - 1 - g) * SUBLANES, SUBLANES)
            a8 = a_s[pl.ds(r0, SUBLANES), :]
            d8 = dh_s[pl.ds(r0, SUBLANES), :]
            rows = [None] * SUBLANES
            for j in range(SUBLANES - 1, -1, -1):
                acc = d8[j:j + 1] + c
                rows[j] = acc
                c = a8[j:j + 1] * acc
            acc_s[pl.ds(r0, SUBLANES), :] = jnp.concatenate(rows, axis=0)
            return c

        carry[...] = lax.fori_loop(0, per, step, carry[...])
        d_u = acc_s[...]
        hprev8 = jnp.where(t == 0, 0.0, hp_ref[...])
        d_a = d_u * _shift_down(hv, hprev8, 1)
        d_sq = d_u * (ig * xc)
        d_i = d_u * (sq * xc)
        d_xc = d_u * (sq * ig)
        d_la = d_a * a - d_sq * (a * a) / sq
        d_pre_r = (d_la * (-LRU_C * sp)) * (r * (1.0 - r))
        d_pre_i = d_i * (ig * (1.0 - ig))
        d_xc = d_xc + _block_mm(d_pre_r, wa_ref, transpose=True) + _block_mm(d_pre_i, wx_ref, transpose=True)
        xcb = xc.astype(BF16)
        drb = d_pre_r.astype(BF16)
        dib = d_pre_i.astype(BF16)
        for j in range(N_LRU_BLOCKS):
            sl = slice(LANES * j, LANES * (j + 1))
            dwa_ref[j] += _dot_tn(xcb[:, sl], drb[:, sl])
            dwx_ref[j] += _dot_tn(xcb[:, sl], dib[:, sl])
        d_lx, d_cw, d_cb = _conv4_bwd(d_xc, dxc_next[...], lx, cwv)
        dxc_next[...] = d_xc[0:SUBLANES]
        d_lam = _colsum(d_la * (-LRU_C * r)) * (-_sigmoid(-lam_v))
        rows_ref[...] += _stack_rows([_colsum(d_pre_r), _colsum(d_pre_i), d_lam, d_cb] + d_cw, d)
        dl_ref[:, :d] = d_lx.astype(BF16)
        dl_ref[:, d:] = (d_pa * hv * _gelu_grad(lg)).astype(BF16)
        dh1_ref[...] = _dot(dl_ref[...], wtl_ref[...])

    vec = _vec_spec(d)
    wspec = pl.BlockSpec((N_LRU_BLOCKS, LANES, LANES), lambda b, i: (0, 0, 0))
    tile = lambda col: pl.BlockSpec((None, ts, d), lambda b, i: (b, rev(i), col))
    prev8 = lambda col: pl.BlockSpec((None, SUBLANES, d), lambda b, i: (b, jnp.maximum(rev(i) * per - 1, 0), col))
    whole = lambda v: pl.BlockSpec(v.shape, lambda b, i: (0, 0))
    return _run(
        body, (pm, pm, h, h, kept, d_ya, cw, wa, wx, lam, wt_pa, wt_lru), bg, name=name, grid=(bsz, nt),
        in_specs=[tile(0), tile(1), tile(0), prev8(0), pl.BlockSpec((None, ts, 5 * d), lambda b, i: (b, rev(i), 0)),
                  tile(0), pl.BlockSpec((4, d), lambda b, i: (0, 0)), wspec, wspec, vec,
                  whole(wt_pa), whole(wt_lru)],
        out_specs=[pl.BlockSpec((None, ts, 2 * d), lambda b, i: (b, rev(i), 0)), tile(0), wspec, wspec,
                   pl.BlockSpec((SUBLANES, d), lambda b, i: (0, 0))],
        out_shape=[jax.ShapeDtypeStruct((bsz, seq, 2 * d), BF16), jax.ShapeDtypeStruct((bsz, seq, d), F32),
                   jax.ShapeDtypeStruct((N_LRU_BLOCKS, LANES, LANES), F32),
                   jax.ShapeDtypeStruct((N_LRU_BLOCKS, LANES, LANES), F32),
                   jax.ShapeDtypeStruct((SUBLANES, d), F32)],
        scratch_shapes=[pltpu.VMEM((1, d), F32), pltpu.VMEM((SUBLANES, d), F32),
                        pltpu.VMEM((ts, d), F32), pltpu.VMEM((ts, d), F32), pltpu.VMEM((ts, d), F32)])


L = SSD_CHUNK
N_PAIRS = SSD_HEADS // 2


def _ssd_common(conv, dt_raw, dtb, alog):
    sg = _sigmoid(conv)
    xa = conv * sg
    dtv = _softplus(dt_raw + dtb)
    a_neg = -jnp.exp(alog)
    rowi = lax.broadcasted_iota(jnp.int32, (L, L), 0)
    coli = lax.broadcasted_iota(jnp.int32, (L, L), 1)
    tril = (rowi >= coli).astype(F32)
    cs = _dot(tril, dtv * a_neg, precision=HI)
    return conv, sg, xa, dtv, a_neg, cs, rowi, coli


def _head_masks():
    lane = lax.broadcasted_iota(jnp.int32, (L, LANES), 1)
    return lane < SSD_HEAD_DIM


def _spread(v, p, first):
    return jnp.where(first[:v.shape[0]], v[:, 2 * p:2 * p + 1], v[:, 2 * p + 1:2 * p + 2])


def _place_head_sums(acc, z, p, first, lane1):
    rows = z.shape[0]
    s0 = jnp.sum(jnp.where(first[:rows], z, 0.0), axis=1, keepdims=True)
    s1 = jnp.sum(jnp.where(first[:rows], 0.0, z), axis=1, keepdims=True)
    lane = lane1[:rows]
    return acc + jnp.where(lane == 2 * p, s0, 0.0) + jnp.where(lane == 2 * p + 1, s1, 0.0)


def _stack_heads(v, first):
    return jnp.concatenate([jnp.where(first, v, 0.0), jnp.where(first, 0.0, v)], axis=0).astype(BF16)


def ssd_fwd(pm, dtr, cw, cb, dtb, alog, d_lanes, nw, w_pb, ya, gates, b_gate, w_out, *, name, bg=None):
    bsz, seq, _ = pm.shape
    nc = seq // L
    inner, cdim, d = SSD_INNER, SSD_CONV_DIM, D_MODEL

    def body(xbc_ref, xp_ref, z_ref, dt_ref, cw_ref, cb_ref, dtb_ref, alog_ref, dl_ref, nw_ref, wpb_ref,
             ya_ref, g_ref, bg_ref, wout_ref,
             y_ref, yn_ref, st_ref, yb_ref, conv_ref, mg_ref, out_ref, state):
        i = pl.program_id(1)

        @pl.when(i == 0)
        def _():
            state[...] = jnp.zeros_like(state)

        prev8 = jnp.where(i == 0, 0.0, xp_ref[...])
        conv = _conv4(xbc_ref[...], prev8, cw_ref[...], cb_ref[...])
        conv_ref[...] = conv
        _, _, xa, dtv, _, cs, rowi, coli = _ssd_common(conv, dt_ref[...], dtb_ref[...], alog_ref[...])
        cst = cs.T
        causal = rowi >= coli
        first = _head_masks()
        for g in range(SSD_GROUPS):
            bg = xa[:, inner + SSD_STATE * g:inner + SSD_STATE * (g + 1)].astype(BF16)
            cg = xa[:, inner + SSD_GROUPS * SSD_STATE + SSD_STATE * g:
                    inner + SSD_GROUPS * SSD_STATE + SSD_STATE * (g + 1)].astype(BF16)
            cbm = _dot_nt(cg, bg)
            for pp in range(2):
                p = 2 * g + pp
                sl = slice(LANES * p, LANES * (p + 1))
                ms = []
                for hh in (2 * p, 2 * p + 1):
                    seg = cs[:, hh:hh + 1] - cst[hh:hh + 1, :]
                    ms.append((cbm * jnp.exp(jnp.where(causal, seg, -jnp.inf))).astype(BF16))
                xsp = xa[:, sl]
                cs_p = _spread(cs, p, first)
                cs_last = cs_p[L - 1:L]
                xp = xsp * _spread(dtv, p, first)
                y_diag = _dot(jnp.concatenate(ms, axis=1), _stack_heads(xp, first))
                st = state[p]
                st_ref[p] = st
                y_off = _dot(cg, st.astype(BF16)) * jnp.exp(cs_p)
                y_ref[:, sl] = y_diag + y_off + dl_ref[:, sl] * xsp
                state[p] = st * jnp.exp(cs_last) + _dot_tn(bg, (xp * jnp.exp(cs_last - cs_p)).astype(BF16))
        zv = z_ref[...]
        yz = y_ref[...] * (zv * _sigmoid(zv))
        gw = inner // SSD_GROUPS
        for g in range(SSD_GROUPS):
            sl = slice(gw * g, gw * (g + 1))
            seg = yz[:, sl]
            yn_ref[:, sl] = ((seg * _rstd(seg)) * nw_ref[:, sl]).astype(BF16)
        yb = _dot(yn_ref[...], wpb_ref[...])
        yb_ref[...] = yb
        g = _sigmoid(g_ref[...] + bg_ref[...])
        mg_ref[...] = (g[:, :d] * ya_ref[...] + g[:, d:] * yb).astype(BF16)
        out_ref[...] = _dot(mg_ref[...], wout_ref[...])

    cvec = lambda w: pl.BlockSpec((1, w), lambda b, i: (0, 0))
    rows = lambda w: pl.BlockSpec((None, L, w), lambda b, i: (b, i, 0))
    outs, bg_outs = _run(
        body, (pm, pm, pm, dtr, cw, cb, dtb, alog, d_lanes, nw, w_pb, ya, gates, b_gate, w_out), bg, name=name,
        grid=(bsz, nc),
        in_specs=[pl.BlockSpec((None, L, cdim), lambda b, i: (b, i, 1)), _prev8_spec(cdim, 1, L),
                  pl.BlockSpec((None, L, inner), lambda b, i: (b, i, 1)),
                  pl.BlockSpec((None, L, LANES), lambda b, i: (b, i, 0)),
                  pl.BlockSpec((4, cdim), lambda b, i: (0, 0)), cvec(cdim), cvec(LANES), cvec(LANES),
                  cvec(inner), cvec(inner), pl.BlockSpec(w_pb.shape, lambda b, i: (0, 0)),
                  rows(d), rows(2 * d), cvec(2 * d), pl.BlockSpec(w_out.shape, lambda b, i: (0, 0))],
        out_specs=[rows(inner), rows(inner),
                   pl.BlockSpec((None, None, N_PAIRS, SSD_STATE, LANES), lambda b, i: (b, i, 0, 0, 0)),
                   rows(d), rows(cdim), rows(d), rows(d)],
        out_shape=[jax.ShapeDtypeStruct((bsz, seq, inner), F32), jax.ShapeDtypeStruct((bsz, seq, inner), BF16),
                   jax.ShapeDtypeStruct((bsz, nc, N_PAIRS, SSD_STATE, LANES), F32),
                   jax.ShapeDtypeStruct((bsz, seq, d), F32), jax.ShapeDtypeStruct((bsz, seq, cdim), F32),
                   jax.ShapeDtypeStruct((bsz, seq, d), BF16), jax.ShapeDtypeStruct((bsz, seq, d), F32)],
        scratch_shapes=[pltpu.VMEM((N_PAIRS, SSD_STATE, LANES), F32)])
    return outs, bg_outs


def ssd_bwd(pm, conv, dtr, y, states, d_yb, cw, dtb, alog, d_lanes, nw, wt_pb, wt_ssd, *, name):
    bsz, seq, _ = pm.shape
    nc = seq // L
    inner, cdim = SSD_INNER, SSD_CONV_DIM
    per = L // SUBLANES

    def rev(i):
        return nc - 1 - i

    def body(xbc_ref, conv_ref, z_ref, dt_ref, y_ref, st_ref, dyb_ref, cw_ref, dtb_ref, alog_ref,
             dl_ref, nw_ref, wtpb_ref, wts_ref, ds_ref, dh1_ref, ddt_ref, r4_ref, r2_ref, r1_ref,
             dstate, dconv_next, dxs_s, dbc_s):
        b, i = pl.program_id(0), pl.program_id(1)
        t = rev(i)

        @pl.when(i == 0)
        def _():
            dstate[...] = jnp.zeros_like(dstate)
            dconv_next[...] = jnp.zeros_like(dconv_next)

        @pl.when((b == 0) & (i == 0))
        def _():
            r4_ref[...] = jnp.zeros_like(r4_ref)
            r2_ref[...] = jnp.zeros_like(r2_ref)
            r1_ref[...] = jnp.zeros_like(r1_ref)

        xbc = xbc_ref[...]
        cwv = cw_ref[...]
        dt_in = dt_ref[...] + dtb_ref[...]
        conv = conv_ref[...]
        _, sg, xa, dtv, a_neg, cs, rowi, coli = _ssd_common(conv, dt_ref[...], dtb_ref[...], alog_ref[...])
        cst = cs.T
        causal = rowi >= coli
        anti = coli >= rowi
        first = _head_masks()
        lane1 = lax.broadcasted_iota(jnp.int32, (L, LANES), 1)

        yv = y_ref[...]
        zv = z_ref[...]
        sz = _sigmoid(zv)
        zs = zv * sz
        yz = yv * zs
        dyn = _dot(dyb_ref[...], wtpb_ref[...])
        gw = inner // SSD_GROUPS
        d_yz_parts, d_nw_parts = [], []
        for g in range(SSD_GROUPS):
            sl = slice(gw * g, gw * (g + 1))
            seg = yz[:, sl]
            r = _rstd(seg)
            n = seg * r
            d_nw_parts.append(_colsum(dyn[:, sl] * n))
            d_yz_parts.append(_rms_bwd(dyn[:, sl] * nw_ref[:, sl], n, r))
        d_yz = jnp.concatenate(d_yz_parts, axis=1)
        d_y = d_yz * zs
        ds_ref[:, :inner] = (d_yz * yv * (sz * (1.0 + zv * (1.0 - sz)))).astype(BF16)

        a1 = jnp.zeros((L, LANES), F32)
        a2 = jnp.zeros((L, LANES), F32)
        xs_dxt = jnp.zeros((L, LANES), F32)
        c0 = jnp.zeros((1, LANES), F32)
        d_dl = jnp.zeros((1, LANES), F32)
        for g in range(SSD_GROUPS):
            bsl = slice(inner + SSD_STATE * g, inner + SSD_STATE * (g + 1))
            csl = slice(inner + SSD_GROUPS * SSD_STATE + SSD_STATE * g,
                        inner + SSD_GROUPS * SSD_STATE + SSD_STATE * (g + 1))
            bg = xa[:, bsl].astype(BF16)
            cg = xa[:, csl].astype(BF16)
            cbm = _dot_nt(cg, bg)
            cbt = _dot_nt(bg, cg)
            d_cb = jnp.zeros((L, L), F32)
            d_bg = jnp.zeros((L, SSD_STATE), F32)
            d_cg = jnp.zeros((L, SSD_STATE), F32)
            for pp in range(2):
                p = 2 * g + pp
                sl = slice(LANES * p, LANES * (p + 1))
                xsp = xa[:, sl]
                dt_p = _spread(dtv, p, first)
                cs_p = _spread(cs, p, first)
                cs_last = cs_p[L - 1:L]
                e_p = jnp.exp(cs_p)
                w_p = jnp.exp(cs_last - cs_p)
                e_last = jnp.exp(cs_last)
                xp = xsp * dt_p
                xpb = xp.astype(BF16)
                dyp = d_y[:, sl]
                dypb = dyp.astype(BF16)
                dy_heads = (jnp.where(first, dyp, 0.0).astype(BF16), jnp.where(first, 0.0, dyp).astype(BF16))
                x_heads = (jnp.where(first, xp, 0.0).astype(BF16), jnp.where(first, 0.0, xp).astype(BF16))
                mts = []
                for k, hh in enumerate((2 * p, 2 * p + 1)):
                    col = cs[:, hh:hh + 1]
                    row = cst[hh:hh + 1, :]
                    dec = jnp.exp(jnp.where(causal, col - row, -jnp.inf))
                    dec_t = jnp.exp(jnp.where(anti, row - col, -jnp.inf))
                    gd = _dot_nt(dy_heads[k], xpb) * dec
                    d_cb = d_cb + gd
                    mt = cbt * dec_t
                    qd = gd * cbm - _dot_nt(x_heads[k], dypb) * mt
                    a1 = a1 + jnp.where(lane1 == hh, jnp.sum(qd, axis=1, keepdims=True), 0.0)
                    mts.append(mt.astype(BF16))
                dst = dstate[p]
                dstb = dst.astype(BF16)
                st = st_ref[p]
                stb = st.astype(BF16)
                dye = (dyp * e_p).astype(BF16)
                xw = (xp * w_p).astype(BF16)
                dx_off = w_p * _dot(bg, dstb)
                d_xp = _dot(jnp.concatenate(mts, axis=1), jnp.concatenate(dy_heads, axis=0)) + dx_off
                dxs_s[:, sl] = d_xp * dt_p + dyp * dl_ref[:, sl]
                a1 = _place_head_sums(a1, dyp * (_dot(cg, stb) * e_p), p, first, lane1)
                a2 = _place_head_sums(a2, xp * dx_off, p, first, lane1)
                xs_dxt = _place_head_sums(xs_dxt, d_xp * xsp, p, first, lane1)
                c0 = _place_head_sums(c0, _colsum(dst * st) * e_last, p, first, lane1)
                d_dl = _place_head_sums(d_dl, _colsum(dyp * xsp), p, first, lane1)
                d_cg = d_cg + _dot_nt(dye, stb)
                d_bg = d_bg + _dot_nt(xw, dstb)
                dstate[p] = dst * e_last + _dot_tn(cg, dye)
            d_cbb = d_cb.astype(BF16)
            dbc_s[:, SSD_STATE * g:SSD_STATE * (g + 1)] = d_bg + _dot_tn(d_cbb, cg)
            dbc_s[:, SSD_GROUPS * SSD_STATE + SSD_STATE * g:SSD_GROUPS * SSD_STATE + SSD_STATE * (g + 1)] = (
                d_cg + _dot(d_cbb, bg))

        d_da = (_dot(anti.astype(F32), a1, precision=HI) + _dot((rowi > coli).astype(F32), a2, precision=HI) + c0)
        d_dt = d_da * a_neg + xs_dxt
        d_alog = _colsum(d_da * dtv) * a_neg
        d_dtr = jnp.where(lane1 < SSD_HEADS, d_dt * _sigmoid(dt_in), 0.0)
        ddt_ref[...] = d_dtr.astype(BF16)
        d_xa = jnp.concatenate([dxs_s[...], dbc_s[...]], axis=1)
        d_conv = d_xa * (sg * (1.0 + conv * (1.0 - sg)))
        d_xbc, d_cw, d_cbias = _conv4_bwd(d_conv, dconv_next[...], xbc, cwv)
        dconv_next[...] = d_conv[0:SUBLANES]
        ds_ref[:, inner:] = d_xbc.astype(BF16)
        dh1_ref[...] = _dot(ds_ref[...], wts_ref[...])
        r4_ref[...] += _stack_rows([d_cbias] + d_cw, cdim)
        r2_ref[...] += _stack_rows([jnp.concatenate(d_nw_parts, axis=1)], inner)
        r1_ref[...] += _stack_rows([_colsum(d_dtr), d_alog, d_dl], LANES)

    cvec = lambda w: pl.BlockSpec((1, w), lambda b, i: (0, 0))
    return _pcall(
        body, name=name, grid=(bsz, nc),
        in_specs=[pl.BlockSpec((None, L, cdim), lambda b, i: (b, rev(i), 1)),
                  pl.BlockSpec((None, L, cdim), lambda b, i: (b, rev(i), 0)),
                  pl.BlockSpec((None, L, inner), lambda b, i: (b, rev(i), 1)),
                  pl.BlockSpec((None, L, LANES), lambda b, i: (b, rev(i), 0)),
                  pl.BlockSpec((None, L, inner), lambda b, i: (b, rev(i), 0)),
                  pl.BlockSpec((None, None, N_PAIRS, SSD_STATE, LANES), lambda b, i: (b, rev(i), 0, 0, 0)),
                  pl.BlockSpec((None, L, D_MODEL), lambda b, i: (b, rev(i), 0)),
                  pl.BlockSpec((4, cdim), lambda b, i: (0, 0)), cvec(LANES), cvec(LANES),
                  cvec(inner), cvec(inner), pl.BlockSpec(wt_pb.shape, lambda b, i: (0, 0)),
                  pl.BlockSpec(wt_ssd.shape, lambda b, i: (0, 0))],
        out_specs=[pl.BlockSpec((None, L, inner + cdim), lambda b, i: (b, rev(i), 0)),
                   pl.BlockSpec((None, L, D_MODEL), lambda b, i: (b, rev(i), 0)),
                   pl.BlockSpec((None, L, LANES), lambda b, i: (b, rev(i), 0)),
                   pl.BlockSpec((SUBLANES, cdim), lambda b, i: (0, 0)),
                   pl.BlockSpec((SUBLANES, inner), lambda b, i: (0, 0)),
                   pl.BlockSpec((SUBLANES, LANES), lambda b, i: (0, 0))],
        out_shape=[jax.ShapeDtypeStruct((bsz, seq, inner + cdim), BF16),
                   jax.ShapeDtypeStruct((bsz, seq, D_MODEL), F32),
                   jax.ShapeDtypeStruct((bsz, seq, LANES), BF16),
                   jax.ShapeDtypeStruct((SUBLANES, cdim), F32),
                   jax.ShapeDtypeStruct((SUBLANES, inner), F32),
                   jax.ShapeDtypeStruct((SUBLANES, LANES), F32)],
        scratch_shapes=[pltpu.VMEM((N_PAIRS, SSD_STATE, LANES), F32), pltpu.VMEM((SUBLANES, cdim), F32),
                        pltpu.VMEM((L, inner), F32), pltpu.VMEM((L, 2 * SSD_GROUPS * SSD_STATE), F32)],
    )(pm, conv, pm, dtr, y, states, d_yb, cw, dtb, alog, d_lanes, nw, wt_pb, wt_ssd)


def _lru_block_weights(w):
    w = w.reshape(N_LRU_BLOCKS, 2, LRU_HEAD_DIM, LRU_HEAD_DIM)
    z = jnp.zeros((N_LRU_BLOCKS, LRU_HEAD_DIM, LRU_HEAD_DIM), w.dtype)
    top = jnp.concatenate([w[:, 0], z], axis=2)
    bot = jnp.concatenate([z, w[:, 1]], axis=2)
    return jnp.concatenate([top, bot], axis=1).astype(BF16)


def _lru_block_grads(g):
    h = LRU_HEAD_DIM
    return jnp.stack([g[:, :h, :h], g[:, h:, h:]], axis=1).reshape(LRU_HEADS, h, h)


def _pad_lanes(v, width=LANES):
    return jnp.pad(v, ((0, 0), (0, width - v.shape[1])))


class NoExchange:
    def __init__(self, weights):
        self._weights, self.grads = weights, {}

    def weights_bg(self):
        return None

    def weights(self, bg_outs):
        return self._weights

    def grads_bg(self, grads):
        self.grads.update(grads)
        return None

    def grads_done(self, bg_outs):
        pass


def local_step(x, target, mod, big, small, plan):
    bsz, seq, d = x.shape
    t = bsz * seq
    flat = lambda v: v.reshape(t, v.shape[-1])
    unflat = lambda v: v.reshape(bsz, seq, v.shape[-1])

    wa_b = _lru_block_weights(small["lru_wa"])
    wx_b = _lru_block_weights(small["lru_wx"])
    dtb = _pad_lanes(small["ssd_dt_bias"])
    alog = _pad_lanes(small["ssd_a_log"])
    d_lanes = jnp.repeat(small["ssd_d"], SSD_HEAD_DIM, axis=1)

    lru_cols = 2 * D_MODEL
    wt = {"lru": big["w_main"][:, :lru_cols].T, "ssd": big["w_main"][:, lru_cols:].T, "gates": big["w_gates"].T,
          "dt": big["w_dt"].T}

    h1 = prenorm(x, small["pre_norm1"], mod, name="prenorm1")
    h1f = flat(h1)
    arriving = plan.weights_bg()
    if arriving is None:
        pm, arrived = mm_nn([(h1f, big["w_main"])], name="in_proj_main"), []
    else:
        pm, arrived = mm_nn([(h1f, big["w_main"])], name="in_proj_main", bg=arriving)
    pm = unflat(pm)
    big = dict(big, **plan.weights(arrived))
    for n in ("w_pa", "w_pb", "w_out", "w_ff1", "w_ff2"):
        wt[n] = big[n].T
    gates = unflat(mm_nn([(h1f, big["w_gates"])], name="in_proj_gates"))
    dtr = unflat(mm_nn([(h1f, big["w_dt"])], name="in_proj_dt"))
    lru_args = (small["lru_conv_w"], small["lru_conv_b"], wa_b, small["lru_ba"], wx_b, small["lru_bx"],
                small["lru_lambda"])
    h_lru, pa_in, ya, lru_kept = lru_fwd(pm, *lru_args, big["w_pa"], name="lru_fwd")
    ssd_args = (small["ssd_conv_w"], small["ssd_conv_b"], dtb, alog, d_lanes, small["ssd_norm_w"])
    (y_ssd, ynorm, states, yb, conv_ssd, merged, out1), _ = ssd_fwd(
        pm, dtr, *ssd_args, big["w_pb"], ya, gates, small["b_gate"], big["w_out"], name="ssd_fwd")
    x1, h2 = post1_pre2(x, out1, mod, small["post_norm1"], small["pre_norm2"], name="post1_pre2")
    f = mm_nn([(flat(h2), big["w_ff1"])], name="ff1")
    y2 = unflat(mm_nn([(f, big["w_ff2"])], a_fn=_relu_sq, name="ff2"))

    dx2, d_y2, pb_a, gl_a = final_bwd(x1, y2, target, mod, small["post_norm2"], name="final_bwd")
    d_y2f = flat(d_y2)
    d_f = mm_nn([(d_y2f, wt["w_ff2"])], out_dtype=BF16, extra=f,
                epi=lambda r, fv: r * (2.0 * jnp.maximum(fv, 0.0)), name="ff2_dx")
    g_ff2 = mm_tn(f, d_y2f, a_fn=_relu_sq, name="ff2_dw")
    d_h2 = unflat(mm_nn([(d_f, wt["w_ff1"])], name="ff1_dx"))
    g_ff1 = mm_tn(flat(h2), d_f, name="ff1_dw")
    dx1, d_out1, pb_b, gl_b = mid_bwd(d_h2, dx2, x1, out1, mod, small["pre_norm2"], small["post_norm1"],
                                      name="mid_bwd")
    d_out1f = flat(d_out1)
    d_merged = unflat(mm_nn([(d_out1f, wt["w_out"])], name="out_dx"))
    g_out = mm_tn(flat(merged), d_out1f, name="out_dw")
    d_ya, d_yb, d_gates, gl_c = merge_bwd(d_merged, ya, yb, gates, small["b_gate"], name="merge_bwd")
    g_pa = mm_tn(flat(pa_in), flat(d_ya), name="pa_dw")
    g_pb = mm_tn(flat(ynorm), flat(d_yb), name="pb_dw")
    leaving = plan.grads_bg({"w_pa": g_pa, "w_pb": g_pb, "w_out": g_out, "w_ff1": g_ff1, "w_ff2": g_ff2})
    (d_l, dh_lru, g_wa_b, g_wx_b, lru_rows), landed = lru_bwd(
        pm, h_lru, lru_kept, d_ya, small["lru_conv_w"], wa_b, wx_b, small["lru_lambda"], wt["w_pa"], wt["lru"],
        name="lru_bwd", bg=leaving)
    plan.grads_done(landed)
    d_s, dh_ssd, d_dt, r4, r2, r1 = ssd_bwd(pm, conv_ssd, dtr, y_ssd, states, d_yb, small["ssd_conv_w"], dtb, alog,
                                          d_lanes, small["ssd_norm_w"], wt["w_pb"], wt["ssd"], name="ssd_bwd")
    d_lf, d_sf, d_gf, d_dtf = flat(d_l), flat(d_s), flat(d_gates), flat(d_dt)
    g_in = jnp.concatenate([
        mm_tn(h1f, d_lf, name="in_dw_lru"), mm_tn(h1f, d_sf, name="in_dw_ssd"),
        mm_tn(h1f, d_dtf, name="in_dw_dt")[:, :SSD_HEADS], mm_tn(h1f, d_gf, name="in_dw_gates")], axis=1)
    leaving = plan.grads_bg({"w_in": g_in})
    partial = [flat(dh_lru), flat(dh_ssd)]
    if leaving is None:
        d_h1 = mm_nn([(d_gf, wt["gates"]), (d_dtf, wt["dt"])], add=partial, name="in_dx_gates")
    else:
        d_h1, landed = mm_nn([(d_gf, wt["gates"]), (d_dtf, wt["dt"])], add=partial, name="in_dx_gates", bg=leaving)
        plan.grads_done(landed)
    grad_x, pb_c, gl_d = first_bwd(unflat(d_h1), dx1, x, mod, small["pre_norm1"], name="first_bwd")

    d_mod = jnp.stack([pb_c[:, 0], pb_c[:, 1], pb_b[:, 2], pb_b[:, 0], pb_b[:, 1], pb_a[:, 0]], axis=1)
    loss_cols = gl_a[1:2]
    nh = SSD_HEADS
    small_grads = {
        "pre_norm1": gl_d[0:1], "post_norm1": gl_b[1:2], "b_gate": gl_c[0:1],
        "lru_conv_w": lru_rows[4:8], "lru_conv_b": lru_rows[3:4],
        "lru_wa": _lru_block_grads(g_wa_b), "lru_ba": lru_rows[0:1],
        "lru_wx": _lru_block_grads(g_wx_b), "lru_bx": lru_rows[1:2], "lru_lambda": lru_rows[2:3],
        "ssd_conv_w": r4[1:5], "ssd_conv_b": r4[0:1],
        "ssd_dt_bias": r1[0:1, :nh], "ssd_a_log": r1[1:2, :nh], "ssd_d": r1[2:3, :nh],
        "ssd_norm_w": r2[0:1], "pre_norm2": gl_b[0:1], "post_norm2": gl_a[0:1],
    }
    return loss_cols, grad_x, d_mod, small_grads


def _position():
    return lax.axis_index("x"), lax.axis_index("y"), lax.axis_index("c")


def _other_chips(x, y):
    return [(1 - x, y), (x, 1 - y), (1 - x, 1 - y)]


def allgather8(v, *, name):
    m_per, n = v.shape

    def body(x_ref, out_ref, send_sems, recv_sems, local_sem):
        x, y, c = _position()
        me, sibling = (x, y, c), (x, y, 1 - c)
        chips = _other_chips(x, y)

        def rows(px, py, pc):
            return out_ref.at[pl.ds((4 * px + 2 * py + pc) * m_per, m_per), :]

        def copy(k, block, to, src=None):
            return pltpu.make_async_remote_copy(
                src_ref=rows(*block) if src is None else src, dst_ref=rows(*block),
                send_sem=send_sems.at[k], recv_sem=recv_sems.at[k], device_id=to, device_id_type=MESH)

        mine = pltpu.make_async_copy(x_ref, rows(*me), local_sem)
        mine.start()
        first = [copy(0, me, sibling, src=x_ref)]
        first += [copy(1 + j, me, (*chip, c), src=x_ref) for j, chip in enumerate(chips)]
        for cp in first:
            cp.start()
        passed = [copy(4 + j, (*chip, c), sibling) for j, chip in enumerate(chips)]
        for j, chip in enumerate(chips):
            copy(1 + j, (*chip, c), me).wait_recv()
            passed[j].start()
        copy(0, sibling, me).wait_recv()
        for j, chip in enumerate(chips):
            copy(4 + j, (*chip, 1 - c), me).wait_recv()
        for cp in first + passed:
            cp.wait_send()
        mine.wait()

    return _pcall(
        body, name=name,
        out_shape=jax.ShapeDtypeStruct((N_DEV * m_per, n), v.dtype),
        in_specs=[pl.BlockSpec(memory_space=pltpu.VMEM)],
        out_specs=pl.BlockSpec(memory_space=pltpu.VMEM),
        scratch_shapes=[pltpu.SemaphoreType.DMA((7,)), pltpu.SemaphoreType.DMA((7,)), pltpu.SemaphoreType.DMA],
    )(v)


def gather_weights(shards, *, name):
    n = len(shards)
    half = [s.shape[0] // 2 for s in shards]
    widths = sorted({s.shape[1] for s in shards})
    chunk_rows = [_stage_rows(h, s.shape[1], itemsize=s.dtype.itemsize) for s, h in zip(shards, half)]
    plan = [(w, j, r0) for w in range(n) for j in range(N_CHIPS - 1) for r0 in range(0, half[w], chunk_rows[w])]

    def body(*refs):
        ins, outs = refs[:n], refs[n:2 * n]
        send_sems, recv_sems, local_sems, passed_sems = refs[2 * n:2 * n + 4]
        stage = refs[2 * n + 4:]
        bufs = {wd: stage[4 * i] for i, wd in enumerate(widths)}
        load_sems = {wd: stage[4 * i + 1] for i, wd in enumerate(widths)}
        stage_send = {wd: stage[4 * i + 2] for i, wd in enumerate(widths)}
        x, y, c = _position()
        me_chip = 2 * x + y
        chips = _other_chips(x, y)

        def piece(w, chip, core):
            return outs[w].at[chip, pl.ds(core * half[w], half[w]), :]

        def over_ici(w, j, chip, src=None):
            px, py = chips[j]
            dst = piece(w, chip, c)
            return pltpu.make_async_remote_copy(
                src_ref=dst if src is None else src, dst_ref=dst, send_sem=send_sems.at[3 * w + j],
                recv_sem=recv_sems.at[3 * w + j], device_id=(px, py, c), device_id_type=MESH)

        local = [pltpu.make_async_copy(ins[w], outs[w].at[me_chip], local_sems.at[w]) for w in range(n)]
        for cp in local:
            cp.start()
        sent = []
        for w in range(n):
            for j in range(N_CHIPS - 1):
                cp = over_ici(w, j, me_chip, src=ins[w].at[pl.ds(c * half[w], half[w]), :])
                cp.start()
                sent.append(cp)
        chunks = []
        for idx, (w, j, r0) in enumerate(plan):
            wd, rb = shards[w].shape[1], chunk_rows[w]
            k = 2 * chips[j][0] + chips[j][1]

            def make(staged, slot, idx=idx, w=w, k=k, r0=r0, wd=wd, rb=rb):
                return pltpu.make_async_remote_copy(
                    src_ref=staged, dst_ref=outs[w].at[k, pl.ds(c * half[w] + r0, rb), :],
                    send_sem=stage_send[wd].at[slot], recv_sem=passed_sems.at[idx],
                    device_id=(x, y, 1 - c), device_id_type=MESH), True

            chunk = (wd, outs[w].at[k, pl.ds(c * half[w] + r0, rb), :], [make])
            if r0 == 0:
                chunk += (lambda w=w, j=j, k=k: over_ici(w, j, k).wait_recv(),)
            chunks.append(chunk)
        _staged(chunks, bufs, load_sems)
        for idx, (w, j, r0) in enumerate(plan):
            wd = shards[w].shape[1]
            k = 2 * chips[j][0] + chips[j][1]
            landed = outs[w].at[k, pl.ds((1 - c) * half[w] + r0, chunk_rows[w]), :]
            pltpu.make_async_remote_copy(
                src_ref=landed, dst_ref=landed, send_sem=stage_send[wd].at[0], recv_sem=passed_sems.at[idx],
                device_id=(x, y, 1 - c), device_id_type=MESH).wait_recv()
        for cp in sent:
            cp.wait_send()
        for cp in local:
            cp.wait()

    stage_rows = [(wd, max(r for s, r in zip(shards, chunk_rows) if s.shape[1] == wd)) for wd in widths]
    return _pcall(
        body, name=name,
        out_shape=[jax.ShapeDtypeStruct((N_CHIPS,) + s.shape, s.dtype) for s in shards],
        in_specs=[ANY] * n, out_specs=[ANY] * n,
        scratch_shapes=[pltpu.SemaphoreType.DMA((3 * n,)), pltpu.SemaphoreType.DMA((3 * n,)),
                        pltpu.SemaphoreType.DMA((n,)), pltpu.SemaphoreType.DMA((len(plan),))]
        + _stage_scratch(stage_rows, shards[0].dtype),
    )(*shards)


STAGE_BYTES = 2 << 20


def _stage_rows(rows, width, itemsize=4):
    return _pick(rows, tuple(t for t in (1024, 512, 256, 128, 64, 32, 16, 8) if t * width * itemsize <= STAGE_BYTES * 3 // 2))


def _staged(chunks, bufs, load_sems):
    count, pending = {}, {}

    def load(i):
        cls, src = chunks[i][0], chunks[i][1]
        if len(chunks[i]) > 3:
            chunks[i][3]()
        slot = count.get(cls, 0) % 2
        count[cls] = count.get(cls, 0) + 1
        for cp, remote in pending.pop((cls, slot), []):
            if remote:
                cp.wait_send()
            else:
                cp.wait()
        staged = bufs[cls].at[slot, pl.ds(0, src.shape[0]), :]
        ld = pltpu.make_async_copy(src, staged, load_sems[cls].at[slot])
        ld.start()
        return ld, cls, slot, staged

    cur = load(0)
    for i in range(len(chunks)):
        nxt = load(i + 1) if i + 1 < len(chunks) else None
        ld, cls, slot, staged = cur
        ld.wait()
        started = []
        for make in chunks[i][2]:
            cp, remote = make(staged, slot)
            cp.start()
            started.append((cp, remote))
        pending[(cls, slot)] = started
        cur = nxt
    for started in pending.values():
        for cp, remote in started:
            if remote:
                cp.wait_send()
            else:
                cp.wait()


def _stage_scratch(widths_rows, dtype):
    scratch = []
    for width, rows in widths_rows:
        scratch += [pltpu.VMEM((2, rows, width), dtype), pltpu.SemaphoreType.DMA((2,)), pltpu.SemaphoreType.DMA((2,)),
                    pltpu.SemaphoreType.DMA((2,))]
    return scratch


def send_half_to_sibling(grads, *, name):
    n = len(grads)
    half = [g.shape[1] // 2 for g in grads]
    widths = sorted({g.shape[2] for g in grads})
    chunk_rows = [_stage_rows(h, g.shape[2]) for g, h in zip(grads, half)]
    plan = [(w, k, r0) for w in range(n) for k in range(N_CHIPS) for r0 in range(0, half[w], chunk_rows[w])]

    def body(*refs):
        ins, theirs = refs[:n], refs[n:2 * n]
        recv_sems = refs[2 * n]
        stage = refs[2 * n + 1:]
        bufs = {wd: stage[4 * i] for i, wd in enumerate(widths)}
        load_sems = {wd: stage[4 * i + 1] for i, wd in enumerate(widths)}
        send_sems = {wd: stage[4 * i + 2] for i, wd in enumerate(widths)}
        x, y, c = _position()
        chunks = []
        for idx, (w, k, r0) in enumerate(plan):
            wd = grads[w].shape[2]
            rb = chunk_rows[w]

            def make(staged, slot, idx=idx, w=w, k=k, r0=r0, wd=wd, rb=rb):
                return pltpu.make_async_remote_copy(
                    src_ref=staged, dst_ref=theirs[w].at[k, pl.ds(r0, rb), :], send_sem=send_sems[wd].at[slot],
                    recv_sem=recv_sems.at[idx], device_id=(x, y, 1 - c), device_id_type=MESH), True

            chunks.append((wd, ins[w].at[k, pl.ds((1 - c) * half[w] + r0, rb), :], [make]))
        _staged(chunks, bufs, load_sems)
        for idx, (w, k, r0) in enumerate(plan):
            wd = grads[w].shape[2]
            landed = theirs[w].at[k, pl.ds(r0, chunk_rows[w]), :]
            pltpu.make_async_remote_copy(
                src_ref=landed, dst_ref=landed, send_sem=send_sems[wd].at[0], recv_sem=recv_sems.at[idx],
                device_id=(x, y, 1 - c), device_id_type=MESH).wait_recv()

    stage_rows = [(wd, max(r for g, r in zip(grads, chunk_rows) if g.shape[2] == wd)) for wd in widths]
    return _pcall(
        body, name=name,
        out_shape=[jax.ShapeDtypeStruct((N_CHIPS, h, g.shape[2]), g.dtype) for g, h in zip(grads, half)],
        in_specs=[ANY] * n, out_specs=[ANY] * n,
        scratch_shapes=[pltpu.SemaphoreType.DMA((len(plan),))] + _stage_scratch(stage_rows, F32),
    )(*grads)


def _chip_exchange_background(arrays, out_shapes, src_of, dst_of, landed_of, own_of):
    n = len(arrays)

    def copies(ins, outs, scr):
        send_sems, recv_sems, local_sems = scr
        x, y, c = _position()
        me_chip = 2 * x + y
        local, sends, recvs = [], [], []
        for w in range(n):
            local.append(pltpu.make_async_copy(*own_of(ins[w], outs[w], w, me_chip), local_sems.at[w]))
            for j, (px, py) in enumerate(_other_chips(x, y)):
                sems = dict(send_sem=send_sems.at[3 * w + j], recv_sem=recv_sems.at[3 * w + j],
                            device_id=(px, py, c), device_id_type=MESH)
                sends.append(pltpu.make_async_remote_copy(
                    src_ref=src_of(ins[w], w, 2 * px + py, me_chip, c), dst_ref=dst_of(outs[w], w, me_chip, c), **sems))
                landed = landed_of(outs[w], w, 2 * px + py, c)
                recvs.append(pltpu.make_async_remote_copy(src_ref=landed, dst_ref=landed, **sems))
        return local, sends, recvs

    def start(ins, outs, scr):
        local, sends, _ = copies(ins, outs, scr)
        for cp in local + sends:
            cp.start()

    def finish(ins, outs, scr):
        local, sends, recvs = copies(ins, outs, scr)
        for cp in recvs:
            cp.wait_recv()
        for cp in sends:
            cp.wait_send()
        for cp in local:
            cp.wait()

    scratch = [pltpu.SemaphoreType.DMA((3 * n,)), pltpu.SemaphoreType.DMA((3 * n,)), pltpu.SemaphoreType.DMA((n,))]
    return Background(arrays, out_shapes, scratch, start, finish)


def scatter_background(parts):
    return _chip_exchange_background(
        parts, [jax.ShapeDtypeStruct(p.shape, p.dtype) for p in parts],
        src_of=lambda ref, w, peer, me, c: ref.at[peer], dst_of=lambda ref, w, me, c: ref.at[me],
        landed_of=lambda ref, w, peer, c: ref.at[peer], own_of=lambda i, o, w, me: (i.at[me], o.at[me]))


def gather_halves_background(shards):
    half = [s.shape[0] // 2 for s in shards]
    rows = lambda w, c: pl.ds(c * half[w], half[w])
    return _chip_exchange_background(
        shards, [jax.ShapeDtypeStruct((N_CHIPS,) + s.shape, s.dtype) for s in shards],
        src_of=lambda ref, w, peer, me, c: ref.at[rows(w, c), :], dst_of=lambda ref, w, me, c: ref.at[me, rows(w, c), :],
        landed_of=lambda ref, w, peer, c: ref.at[peer, rows(w, c), :], own_of=lambda i, o, w, me: (i, o.at[me]))


def fill_other_half(gathered, *, name):
    n = len(gathered)
    half = [g.shape[1] // 2 for g in gathered]
    widths = sorted({g.shape[2] for g in gathered})
    chunk_rows = [_stage_rows(h, g.shape[2], itemsize=2) for g, h in zip(gathered, half)]
    plan = [(w, j, r0) for w in range(n) for j in range(N_CHIPS - 1) for r0 in range(0, half[w], chunk_rows[w])]

    def body(*refs):
        ins, outs = refs[:n], refs[n:2 * n]
        recv_sems = refs[2 * n]
        stage = refs[2 * n + 1:]
        bufs = {wd: stage[4 * i] for i, wd in enumerate(widths)}
        load_sems = {wd: stage[4 * i + 1] for i, wd in enumerate(widths)}
        send_sems = {wd: stage[4 * i + 2] for i, wd in enumerate(widths)}
        x, y, c = _position()
        chips = _other_chips(x, y)
        chunks = []
        for idx, (w, j, r0) in enumerate(plan):
            wd, rb = gathered[w].shape[2], chunk_rows[w]
            k = 2 * chips[j][0] + chips[j][1]

            def make(staged, slot, idx=idx, w=w, k=k, r0=r0, wd=wd, rb=rb):
                return pltpu.make_async_remote_copy(
                    src_ref=staged, dst_ref=outs[w].at[k, pl.ds(c * half[w] + r0, rb), :],
                    send_sem=send_sems[wd].at[slot], recv_sem=recv_sems.at[idx],
                    device_id=(x, y, 1 - c), device_id_type=MESH), True

            chunks.append((wd, ins[w].at[k, pl.ds(c * half[w] + r0, rb), :], [make]))
        _staged(chunks, bufs, load_sems)
        for idx, (w, j, r0) in enumerate(plan):
            wd = gathered[w].shape[2]
            k = 2 * chips[j][0] + chips[j][1]
            landed = outs[w].at[k, pl.ds((1 - c) * half[w] + r0, chunk_rows[w]), :]
            pltpu.make_async_remote_copy(
                src_ref=landed, dst_ref=landed, send_sem=send_sems[wd].at[0], recv_sem=recv_sems.at[idx],
                device_id=(x, y, 1 - c), device_id_type=MESH).wait_recv()

    stage_rows = [(wd, max(r for g, r in zip(gathered, chunk_rows) if g.shape[2] == wd)) for wd in widths]
    return _pcall(
        body, name=name, out_shape=[jax.ShapeDtypeStruct(g.shape, g.dtype) for g in gathered],
        in_specs=[ANY] * n, out_specs=[ANY] * n, input_output_aliases={w: w for w in range(n)},
        scratch_shapes=[pltpu.SemaphoreType.DMA((len(plan),))] + _stage_scratch(stage_rows, gathered[0].dtype),
    )(*gathered)


def join_with_sibling(halves, *, name):
    n = len(halves)
    widths = sorted({h.shape[1] for h in halves})
    chunk_rows = [_stage_rows(h.shape[0], h.shape[1]) for h in halves]
    plan = [(w, r0) for w in range(n) for r0 in range(0, halves[w].shape[0], chunk_rows[w])]

    def body(*refs):
        ins, outs = refs[:n], refs[n:2 * n]
        recv_sems = refs[2 * n]
        stage = refs[2 * n + 1:]
        bufs = {wd: stage[4 * i] for i, wd in enumerate(widths)}
        load_sems = {wd: stage[4 * i + 1] for i, wd in enumerate(widths)}
        send_sems = {wd: stage[4 * i + 2] for i, wd in enumerate(widths)}
        store_sems = {wd: stage[4 * i + 3] for i, wd in enumerate(widths)}
        x, y, c = _position()
        chunks = []
        for idx, (w, r0) in enumerate(plan):
            h, wd = halves[w].shape
            rb = chunk_rows[w]

            def to_sibling(staged, slot, idx=idx, w=w, r0=r0, h=h, wd=wd, rb=rb):
                return pltpu.make_async_remote_copy(
                    src_ref=staged, dst_ref=outs[w].at[pl.ds(c * h + r0, rb), :], send_sem=send_sems[wd].at[slot],
                    recv_sem=recv_sems.at[idx], device_id=(x, y, 1 - c), device_id_type=MESH), True

            def to_mine(staged, slot, w=w, r0=r0, h=h, wd=wd, rb=rb):
                return pltpu.make_async_copy(staged, outs[w].at[pl.ds(c * h + r0, rb), :], store_sems[wd].at[slot]), False

            chunks.append((wd, ins[w].at[pl.ds(r0, rb), :], [to_sibling, to_mine]))
        _staged(chunks, bufs, load_sems)
        for idx, (w, r0) in enumerate(plan):
            h, wd = halves[w].shape
            landed = outs[w].at[pl.ds((1 - c) * h + r0, chunk_rows[w]), :]
            pltpu.make_async_remote_copy(
                src_ref=landed, dst_ref=landed, send_sem=send_sems[wd].at[0], recv_sem=recv_sems.at[idx],
                device_id=(x, y, 1 - c), device_id_type=MESH).wait_recv()

    stage_rows = [(wd, max(r for h, r in zip(halves, chunk_rows) if h.shape[1] == wd)) for wd in widths]
    return _pcall(
        body, name=name,
        out_shape=[jax.ShapeDtypeStruct((2 * h.shape[0], h.shape[1]), h.dtype) for h in halves],
        in_specs=[ANY] * n, out_specs=[ANY] * n,
        scratch_shapes=[pltpu.SemaphoreType.DMA((len(plan),))] + _stage_scratch(stage_rows, F32),
    )(*halves)


def _row_tile(rows, cols, itemsize=4, budget=2 << 20):
    for t in (1024, 512, 256, 128, 64, 32, 16, 8):
        if rows % t == 0 and t * cols * itemsize <= budget:
            return t
    return rows


def add_half_to_bf16(core, full, theirs, *, name):
    k, r, c = theirs.shape
    tr = _row_tile(r, c)
    nb = r // tr

    def body(core_ref, a_ref, b_ref, o_ref):
        o_ref[...] = (a_ref[...] + b_ref[...]).astype(BF16)

    spec = pl.BlockSpec((None, tr, c), lambda i, j, core_ref: (i, j, 0))
    grid_spec = pltpu.PrefetchScalarGridSpec(
        num_scalar_prefetch=1, grid=(k, nb),
        in_specs=[pl.BlockSpec((None, tr, c), lambda i, j, core_ref: (i, core_ref[0] * nb + j, 0)), spec],
        out_specs=spec)
    return _pcall(body, name=name, grid_spec=grid_spec,
                  out_shape=jax.ShapeDtypeStruct(theirs.shape, BF16))(core, full, theirs)


def sum_blocks(v, *, name):
    k, r, c = v.shape
    tr = _row_tile(r, c * k)

    def body(v_ref, o_ref):
        acc = v_ref[0].astype(F32)
        for j in range(1, k):
            acc = acc + v_ref[j].astype(F32)
        o_ref[...] = acc

    return _pcall(body, name=name, grid=(r // tr,),
                  in_specs=[pl.BlockSpec((k, tr, c), lambda i: (0, i, 0))],
                  out_specs=pl.BlockSpec((tr, c), lambda i: (i, 0)),
                  out_shape=jax.ShapeDtypeStruct((r, c), F32))(v)


def adamw(w, g, m, v, *, name):
    r, c = w.shape
    tr = _row_tile(r, c, budget=1 << 20)
    m_scale = 1.0 / (1.0 - ADAM_B1 ** ADAM_STEP)
    v_scale = 1.0 / (1.0 - ADAM_B2 ** ADAM_STEP)

    def body(w_ref, g_ref, m_ref, v_ref, d_ref, nm_ref, nv_ref):
        gv = g_ref[...]
        nm = ADAM_B1 * m_ref[...] + (1.0 - ADAM_B1) * gv
        nv = ADAM_B2 * v_ref[...] + (1.0 - ADAM_B2) * (gv * gv)
        nm_ref[...] = nm
        nv_ref[...] = nv
        d_ref[...] = -ADAM_LR * ((nm * m_scale) / (jnp.sqrt(nv * v_scale) + ADAM_EPS) + ADAM_WD * w_ref[...])

    spec = pl.BlockSpec((tr, c), lambda i: (i, 0))
    return _pcall(body, name=name, grid=(r // tr,), in_specs=[spec] * 4, out_specs=[spec] * 3,
                  out_shape=[jax.ShapeDtypeStruct((r, c), F32)] * 3)(w, g, m, v)


def ada_fwd(c_all, w_shard, b_shard, *, name):
    bsz, d = c_all.shape
    ncol = w_shard.shape[1]

    def body(c_ref, w_ref, b_ref, o_ref):
        cv = c_ref[...]
        act = (cv * _sigmoid(cv)).astype(BF16)
        o_ref[...] = _dot(act, w_ref[...].astype(BF16)) + b_ref[...]

    tn = _pick(ncol, (512, 256, 128))
    return _pcall(body, name=name, grid=(ncol // tn,),
                  in_specs=[pl.BlockSpec((bsz, d), lambda j: (0, 0)), pl.BlockSpec((d, tn), lambda j: (0, j)),
                            pl.BlockSpec((1, tn), lambda j: (0, j))],
                  out_specs=pl.BlockSpec((bsz, tn), lambda j: (0, j)),
                  out_shape=jax.ShapeDtypeStruct((bsz, ncol), F32))(c_all, w_shard, b_shard)


def ada_bwd(c_all, d_mod_all, d_mod_cols, *, name):
    bsz, d = c_all.shape
    ncol = d_mod_cols.shape[1]
    nall = d_mod_all.shape[1]

    def body(c_ref, da_ref, dc_ref, gw_ref, gb_ref):
        cv = c_ref[...]
        act = (cv * _sigmoid(cv)).astype(BF16)
        gw_ref[...] = _dot_tn(act, dc_ref[...].astype(BF16))
        gb_ref[...] = _colsum(da_ref[...])

    full = lambda s: pl.BlockSpec(s, lambda: (0,) * len(s))
    return _pcall(body, name=name,
                  in_specs=[full((bsz, d)), full((bsz, nall)), full((bsz, ncol))],
                  out_specs=[full((d, ncol)), full((1, nall))],
                  out_shape=[jax.ShapeDtypeStruct((d, ncol), F32), jax.ShapeDtypeStruct((1, nall), F32)],
                  )(c_all, d_mod_all, d_mod_cols)


WEIGHT_NAMES = ['w_ada', 'b_ada', 'pre_norm1', 'post_norm1', 'w_in', 'b_gate', 'lru_conv_w', 'lru_conv_b', 'lru_wa',
                'lru_ba', 'lru_wx', 'lru_bx', 'lru_lambda', 'w_pa', 'ssd_conv_w', 'ssd_conv_b', 'ssd_dt_bias',
                'ssd_a_log', 'ssd_d', 'ssd_norm_w', 'w_pb', 'w_out', 'pre_norm2', 'post_norm2', 'w_ff1', 'w_ff2']
BIG_NAMES = ['w_in', 'w_pa', 'w_pb', 'w_out', 'w_ff1', 'w_ff2']
COLUMN_SHARDED = ('w_in', 'w_ff1')
SMALL_NAMES = [n for n in WEIGHT_NAMES if n not in BIG_NAMES + ['w_ada', 'b_ada']]
SHARDED_SMALL = ('lru_conv_w', 'ssd_conv_w')
PACK_WIDTH = 1024


def _whole(name, gathered):
    if name in COLUMN_SHARDED:
        return jnp.transpose(gathered, (1, 0, 2)).reshape(gathered.shape[1], N_CHIPS * gathered.shape[2])
    return gathered.reshape(N_CHIPS * gathered.shape[1], gathered.shape[2])


def _by_chip(name, g):
    if name in COLUMN_SHARDED:
        return jnp.transpose(g.reshape(g.shape[0], N_CHIPS, g.shape[1] // N_CHIPS), (1, 0, 2))
    return g.reshape(N_CHIPS, g.shape[0] // N_CHIPS, g.shape[1])


class ChipExchange:
    def __init__(self, shards, core):
        self.shards, self.core = shards, core
        self.pending, self.halves = [], {}

    def weights_bg(self):
        return gather_halves_background(list(self.shards.values()))

    def weights(self, arrived):
        swapped = fill_other_half(arrived, name="weights_from_sibling")
        return {n: _whole(n, g) for n, g in zip(self.shards, swapped)}

    def grads_bg(self, grads):
        self.pending = list(grads)
        by_chip = [_by_chip(n, g) for n, g in grads.items()]
        theirs = send_half_to_sibling(by_chip, name="grads_to_sibling_" + self.pending[0])
        sums = [add_half_to_bf16(self.core, a, b, name="add_cores_" + n)
                for n, a, b in zip(self.pending, by_chip, theirs)]
        return scatter_background(sums)

    def grads_done(self, landed):
        for n, p in zip(self.pending, landed):
            self.halves[n] = sum_blocks(p, name="add_chips_" + n)

    def reduced(self):
        names = list(self.halves)
        return dict(zip(names, join_with_sibling([self.halves[n] for n in names], name="grads_join")))


def _pack(parts):
    flat = jnp.concatenate([p.reshape(-1).astype(F32) for p in parts])
    rows = -(-flat.shape[0] // (PACK_WIDTH * SUBLANES)) * SUBLANES
    return jnp.pad(flat, (0, rows * PACK_WIDTH - flat.shape[0])).reshape(rows, PACK_WIDTH)


def _unpack(packed, shapes):
    flat = packed.reshape(-1)
    out, pos = [], 0
    for s in shapes:
        size = int(np.prod(s))
        out.append(flat[pos:pos + size].reshape(s))
        pos += size
    return out


def kernel(x, c, w_ada, b_ada, pre_norm1, post_norm1, w_in, b_gate, lru_conv_w, lru_conv_b, lru_wa, lru_ba, lru_wx, lru_bx, lru_lambda, w_pa, ssd_conv_w, ssd_conv_b, ssd_dt_bias, ssd_a_log, ssd_d, ssd_norm_w, w_pb, w_out, pre_norm2, post_norm2, w_ff1, w_ff2, loss_target, m_w_ada, m_b_ada, m_pre_norm1, m_post_norm1, m_w_in, m_b_gate, m_lru_conv_w, m_lru_conv_b, m_lru_wa, m_lru_ba, m_lru_wx, m_lru_bx, m_lru_lambda, m_w_pa, m_ssd_conv_w, m_ssd_conv_b, m_ssd_dt_bias, m_ssd_a_log, m_ssd_d, m_ssd_norm_w, m_w_pb, m_w_out, m_pre_norm2, m_post_norm2, m_w_ff1, m_w_ff2, v_w_ada, v_b_ada, v_pre_norm1, v_post_norm1, v_w_in, v_b_gate, v_lru_conv_w, v_lru_conv_b, v_lru_wa, v_lru_ba, v_lru_wx, v_lru_bx, v_lru_lambda, v_w_pa, v_ssd_conv_w, v_ssd_conv_b, v_ssd_dt_bias, v_ssd_a_log, v_ssd_d, v_ssd_norm_w, v_w_pb, v_w_out, v_pre_norm2, v_post_norm2, v_w_ff1, v_w_ff2):
    given = dict(locals())
    bsz, seq, d = x.shape
    my_x, my_y, my_c = lax.axis_index("x"), lax.axis_index("y"), lax.axis_index("c")
    chip = 2 * my_x + my_y
    dev = 2 * chip + my_c
    strip = lambda a: a if a.ndim == 2 else a[0]
    w = {n: strip(given[n]) for n in WEIGHT_NAMES}
    m = {n: strip(given["m_" + n]) for n in WEIGHT_NAMES}
    v = {n: strip(given["v_" + n]) for n in WEIGHT_NAMES}

    first_shapes = [c.shape] + [w[n].shape for n in SHARDED_SMALL]
    first = allgather8(_pack([c] + [w[n] for n in SHARDED_SMALL]), name="gather_c_conv")
    first = first.reshape(N_DEV, -1, PACK_WIDTH)
    per_dev = [_unpack(first[k], first_shapes) for k in range(N_DEV)]
    c_all = jnp.concatenate([p[0] for p in per_dev], axis=0)
    conv_full = {n: jnp.concatenate([per_dev[2 * k][1 + i] for k in range(N_CHIPS)], axis=1)
                 for i, n in enumerate(SHARDED_SMALL)}

    ncol = w["w_ada"].shape[1]
    b_cols = lax.dynamic_slice(b_ada, (0, chip * ncol), (1, ncol))
    mod_cols = ada_fwd(c_all, w["w_ada"], b_cols, name="ada_fwd")
    mod_all = allgather8(mod_cols, name="gather_mod").reshape(N_CHIPS, 2, N_DEV * bsz, ncol)[:, 0]
    mod_all = jnp.transpose(mod_all, (1, 0, 2)).reshape(N_DEV * bsz, N_CHIPS * ncol)
    mod = lax.dynamic_slice(mod_all, (dev * bsz, 0), (bsz, 6 * d)).reshape(bsz, 6, d)
    mod = jnp.pad(mod, ((0, 0), (0, 2), (0, 0)))

    w_in_full = _whole("w_in", gather_weights([w["w_in"].astype(BF16)], name="gather_w_in")[0])
    big = {"w_main": w_in_full[:, :8192],
           "w_dt": jnp.pad(w_in_full[:, 8192:8192 + SSD_HEADS], ((0, 0), (0, LANES - SSD_HEADS))),
           "w_gates": w_in_full[:, 8192 + SSD_HEADS:]}
    small = {n: w[n] for n in SMALL_NAMES}
    small.update(conv_full)
    plan = ChipExchange({n: w[n].astype(BF16) for n in BIG_NAMES if n != "w_in"}, my_c.astype(jnp.int32).reshape(1))

    loss_cols, grad_x, d_mod, small_grads = local_step(x, loss_target, mod, big, small, plan)

    packed = _pack([d_mod, loss_cols] + [small_grads[n] for n in SMALL_NAMES])
    rows = packed.shape[0]
    everyone = allgather8(packed, name="gather_small").reshape(N_DEV, rows, PACK_WIDTH)
    d_mod_all = everyone[:, :bsz * 6].reshape(N_DEV * bsz, 6 * d)
    summed = sum_blocks(everyone, name="sum_small")
    shapes = [d_mod.shape, loss_cols.shape] + [small_grads[n].shape for n in SMALL_NAMES]
    parts = _unpack(summed, shapes)
    loss = jnp.sum(parts[1])
    grads = dict(zip(SMALL_NAMES, parts[2:]))
    for n in SHARDED_SMALL:
        cols = w[n].shape[1]
        grads[n] = lax.dynamic_slice(grads[n], (0, chip * cols), (grads[n].shape[0], cols))
    d_mod_cols = lax.dynamic_slice(d_mod_all, (0, chip * ncol), (N_DEV * bsz, ncol))
    grads["w_ada"], grads["b_ada"] = ada_bwd(c_all, d_mod_all, d_mod_cols, name="ada_bwd")

    grads.update(plan.reduced())

    delta, new_m, new_v = {}, {}, {}
    for n in BIG_NAMES + ["w_ada", "b_ada"]:
        delta[n], new_m[n], new_v[n] = adamw(w[n], grads[n], m[n], v[n], name="adamw_" + n)
    shapes = [w[n].shape for n in SMALL_NAMES]
    pk = lambda src: _pack([src[n] for n in SMALL_NAMES])
    upd = adamw(pk(w), pk(grads), pk(m), pk(v), name="adamw_small")
    for out, packed_out in zip((delta, new_m, new_v), upd):
        out.update(zip(SMALL_NAMES, _unpack(packed_out, shapes)))

    shaped = lambda src: [src[n].reshape(given[n].shape) for n in WEIGHT_NAMES]
    return (loss, grad_x, *shaped(grads), *shaped(delta), *shaped(new_m), *shaped(new_v))
```

```python
import functools
import math

import numpy as np
import jax
import jax.numpy as jnp
from jax import lax
from jax.experimental import pallas as pl
from jax.experimental.pallas import tpu as pltpu

F32 = jnp.float32
BF16 = jnp.bfloat16
HI = lax.Precision.HIGHEST
MESH = pl.DeviceIdType.MESH

D_MODEL = 1024
LRU_HEADS = 16
LRU_HEAD_DIM = 64
LRU_C = 8.0
SSD_INNER = 2048
SSD_HEADS = 32
SSD_HEAD_DIM = 64
SSD_GROUPS = 8
SSD_STATE = 128
SSD_CHUNK = 128
SSD_CONV_DIM = 4096
D_FF = 4096
EPS = 1e-6
N_CHIPS = 4
N_DEV = 8
LANES = 128
SUBLANES = 8

ADAM_LR = 0.001
ADAM_B1 = 0.9
ADAM_B2 = 0.999
ADAM_EPS = 1e-08
ADAM_WD = 0.01
ADAM_STEP = 10


ANY = pl.BlockSpec(memory_space=pl.ANY)


def _pcall(body, **kw):
    return pl.pallas_call(body, **kw)


class Background:
    def __init__(self, inputs, out_shapes, scratch, start, finish):
        self.inputs, self.out_shapes, self.scratch = list(inputs), list(out_shapes), list(scratch)
        self.start, self.finish = start, finish

    def wrap(self, body, kw):
        n_in, n_out = len(kw["in_specs"]), len(kw["out_specs"])
        n_scr = len(kw.get("scratch_shapes", []))
        b_in, b_out = len(self.inputs), len(self.out_shapes)
        grid = kw["grid"]

        def wrapped(*refs):
            ins, b_ins = refs[:n_in], refs[n_in:n_in + b_in]
            o0 = n_in + b_in
            outs, b_outs = refs[o0:o0 + n_out], refs[o0 + n_out:o0 + n_out + b_out]
            s0 = o0 + n_out + b_out
            scr, b_scr = refs[s0:s0 + n_scr], refs[s0 + n_scr:]
            ids = [pl.program_id(a) for a in range(len(grid))]
            first = functools.reduce(jnp.logical_and, [i == 0 for i in ids])
            last = functools.reduce(jnp.logical_and, [i == g - 1 for i, g in zip(ids, grid)])

            @pl.when(first)
            def _():
                self.start(b_ins, b_outs, b_scr)

            body(*ins, *outs, *scr)

            @pl.when(last)
            def _():
                self.finish(b_ins, b_outs, b_scr)

        kw = dict(kw, in_specs=list(kw["in_specs"]) + [ANY] * b_in, out_specs=list(kw["out_specs"]) + [ANY] * b_out,
                  out_shape=list(kw["out_shape"]) + self.out_shapes,
                  scratch_shapes=list(kw.get("scratch_shapes", [])) + self.scratch)
        return wrapped, kw


def _run(body, args, bg, **kw):
    n_out = len(kw["out_shape"])
    if bg is None:
        return list(_pcall(body, **kw)(*args)), []
    body, kw = bg.wrap(body, kw)
    outs = _pcall(body, **kw)(*args, *bg.inputs)
    return list(outs[:n_out]), list(outs[n_out:])


def _sigmoid(v):
    return 1.0 / (1.0 + jnp.exp(-v))


def _log1p(u):
    return jnp.where(u < 1e-3, u * (1.0 - u * (0.5 - u * (1.0 / 3.0))), jnp.log(1.0 + u))


def _softplus(v):
    return jnp.maximum(v, 0.0) + _log1p(jnp.exp(-jnp.abs(v)))


def _neg_expm1(v):
    small = -v * (1.0 + v * (0.5 + v * (1.0 / 6.0 + v * (1.0 / 24.0))))
    return jnp.where(v > -0.05, small, 1.0 - jnp.exp(v))


_GELU_K = math.sqrt(2.0 / math.pi)


def _gelu(v):
    t = jnp.tanh(_GELU_K * (v + 0.044715 * v * v * v))
    return 0.5 * v * (1.0 + t)


def _gelu_grad(v):
    t = jnp.tanh(_GELU_K * (v + 0.044715 * v * v * v))
    return 0.5 * (1.0 + t) + 0.5 * v * (1.0 - t * t) * _GELU_K * (1.0 + 3.0 * 0.044715 * v * v)


def _colsum(v):
    return jnp.sum(v, axis=0, keepdims=True)


def _dot(a, b, precision=None):
    return lax.dot_general(a, b, (((1,), (0,)), ((), ())), preferred_element_type=F32, precision=precision)


def _dot_nt(a, b):
    return lax.dot_general(a, b, (((1,), (1,)), ((), ())), preferred_element_type=F32)


def _dot_tn(a, b):
    return lax.dot_general(a, b, (((0,), (0,)), ((), ())), preferred_element_type=F32)


def _shift_down(xt, prev8, j):
    if j == 0:
        return xt
    n = xt.shape[0]
    r = pltpu.roll(xt, j, 0)
    p = pltpu.roll(prev8, j, 0)
    rows = lax.broadcasted_iota(jnp.int32, (SUBLANES, xt.shape[1]), 0)
    top = jnp.where(rows < j, p, r[0:SUBLANES])
    if n == SUBLANES:
        return top
    return jnp.concatenate([top, r[SUBLANES:]], axis=0)


def _shift_up(xt, next8, j):
    if j == 0:
        return xt
    n = xt.shape[0]
    r = pltpu.roll(xt, n - j, 0)
    p = pltpu.roll(next8, SUBLANES - j, 0)
    rows = lax.broadcasted_iota(jnp.int32, (SUBLANES, xt.shape[1]), 0)
    bot = jnp.where(rows >= SUBLANES - j, p, r[n - SUBLANES:])
    if n == SUBLANES:
        return bot
    return jnp.concatenate([r[:n - SUBLANES], bot], axis=0)


def _conv4(xt, prev8, w, b):
    out = b + w[3:4] * xt
    for k in range(3):
        out = out + w[k:k + 1] * _shift_down(xt, prev8, 3 - k)
    return out


def _conv4_bwd(d_out, next8, xt, w):
    d_x = w[3:4] * d_out
    d_w = []
    for k in range(3):
        up = _shift_up(d_out, next8, 3 - k)
        d_x = d_x + w[k:k + 1] * up
        d_w.append(_colsum(up * xt))
    d_w.append(_colsum(d_out * xt))
    return d_x, d_w, _colsum(d_out)


def _stack_rows(rows, width):
    rows = list(rows) + [jnp.zeros((1, width), F32)] * (SUBLANES - len(rows))
    return jnp.concatenate(rows, axis=0)


def _pick(n, cands):
    for c in cands:
        if n % c == 0:
            return c
    raise ValueError(f"no tile for {n}")


MM_ROWS = 1024
MM_VMEM_BUDGET = 36 << 20
MM_PANEL_COLS = 2048
MM_SUB = 512


def mm_nn(pairs, *, name, out_dtype=F32, a_fn=None, add=None, epi=None, extra=None, bg=None):
    np_ = len(pairs)
    m, n = pairs[0][0].shape[0], pairs[0][1].shape[1]
    pn = n if n <= MM_PANEL_COLS else _pick(n, (MM_PANEL_COLS, 1024, 512, 256, 128))
    ns = _pick(pn, (MM_SUB, 256, 128))
    adds = list(add or ())
    has_extra = extra is not None
    stage0 = a_fn is not None or pairs[0][0].dtype != BF16

    def vmem_bytes(rows):
        tiles = sum(rows * a.shape[1] * a.dtype.itemsize for a, _ in pairs)
        tiles += rows * pn * (4 * len(adds) + (extra.dtype.itemsize if has_extra else 0) + jnp.dtype(out_dtype).itemsize)
        panels = sum(b.shape[0] * pn * b.dtype.itemsize for _, b in pairs)
        return 2 * (tiles + panels) + (rows * pairs[0][0].shape[1] * 2 if stage0 else 0)

    tm = _pick(m, (MM_ROWS, 512, 256, 128, 64, 32, 16, 8))
    if vmem_bytes(tm) > MM_VMEM_BUDGET:
        tm = _pick(m, (512, 256, 128, 64, 32, 16, 8))

    def body(*refs):
        a_refs, b_refs = refs[:np_], refs[np_:2 * np_]
        pos = 2 * np_
        extra_ref = None
        add_refs = refs[pos:pos + len(adds)]
        pos += len(adds)
        if has_extra:
            extra_ref = refs[pos]
            pos += 1
        o_ref = refs[pos]
        lhs = list(a_refs)
        if stage0:
            av = a_refs[0][...]
            if a_fn is not None:
                av = a_fn(av)
            refs[pos + 1][...] = av.astype(BF16)
            lhs[0] = refs[pos + 1]
        for n0 in range(0, pn, ns):
            sl = slice(n0, n0 + ns)
            acc = None
            for a_ref, b_ref in zip(lhs, b_refs):
                part = _dot(a_ref[...].astype(BF16), b_ref[:, sl])
                acc = part if acc is None else acc + part
            for add_ref in add_refs:
                acc = acc + add_ref[:, sl]
            if epi is not None:
                acc = epi(acc, extra_ref[:, sl]) if has_extra else epi(acc)
            o_ref[:, sl] = acc.astype(out_dtype)

    in_specs = [pl.BlockSpec((tm, a.shape[1]), lambda j, i: (i, 0)) for a, _ in pairs]
    in_specs += [pl.BlockSpec((b.shape[0], pn), lambda j, i: (0, j)) for _, b in pairs]
    args = [a for a, _ in pairs] + [b for _, b in pairs]
    tile = pl.BlockSpec((tm, pn), lambda j, i: (i, j))
    for extra_add in adds:
        in_specs.append(tile)
        args.append(extra_add)
    if has_extra:
        in_specs.append(tile)
        args.append(extra)
    outs, bg_outs = _run(
        body, args, bg, name=name, grid=(n // pn, m // tm), in_specs=in_specs, out_specs=[tile],
        out_shape=[jax.ShapeDtypeStruct((m, n), out_dtype)],
        scratch_shapes=[pltpu.VMEM((tm, pairs[0][0].shape[1]), BF16)] if stage0 else [])
    return outs[0] if bg is None else (outs[0], bg_outs)


MM_REDUCE_ROWS = 1024
MM_GRAD_ROWS = 1024
MM_GRAD_COLS = 2048


def mm_tn(a, b, *, name, a_fn=None):
    m, ka = a.shape
    nb = b.shape[1]
    pa = _pick(ka, (MM_GRAD_ROWS, 512, 256, 128))
    pb = nb if nb <= MM_GRAD_COLS else _pick(nb, (MM_GRAD_COLS, 1024, 512, 256, 128))
    ns = _pick(pb, (MM_SUB, 256, 128))
    tmk = _pick(m, (MM_REDUCE_ROWS, 512, 256, 128, 64, 32, 16))

    def body(a_ref, b_ref, o_ref, lhs):
        k = pl.program_id(2)

        @pl.when(k == 0)
        def _():
            o_ref[...] = jnp.zeros_like(o_ref)

        av = a_ref[...]
        if a_fn is not None:
            av = a_fn(av)
        lhs[...] = av.astype(BF16)
        for n0 in range(0, pb, ns):
            o_ref[:, n0:n0 + ns] += _dot_tn(lhs[...], b_ref[:, n0:n0 + ns].astype(BF16))

    return _pcall(
        body, name=name,
        grid=(ka // pa, nb // pb, m // tmk),
        in_specs=[pl.BlockSpec((tmk, pa), lambda i, j, k: (k, i)),
                  pl.BlockSpec((tmk, pb), lambda i, j, k: (k, j))],
        out_specs=pl.BlockSpec((pa, pb), lambda i, j, k: (i, j)),
        out_shape=jax.ShapeDtypeStruct((ka, nb), F32),
        scratch_shapes=[pltpu.VMEM((tmk, pa), BF16)],
    )(a, b)


def _relu_sq(v):
    r = jnp.maximum(v, 0.0)
    return r * r


ROW_TILE = 512


def _row_specs(bsz, seq, width, ts):
    return pl.BlockSpec((None, ts, width), lambda b, i: (b, i, 0))


def _vec_spec(width):
    return pl.BlockSpec((1, width), lambda b, i: (0, 0))


def _mod_spec():
    return pl.BlockSpec((None, SUBLANES, D_MODEL), lambda b, i: (b, 0, 0))


def _rstd(v):
    return lax.rsqrt(jnp.mean(v * v, axis=-1, keepdims=True) + EPS)


def prenorm(x, w, mod, *, name):
    bsz, seq, d = x.shape
    ts = _pick(seq, (ROW_TILE, 256, 128))

    def body(x_ref, w_ref, mod_ref, h_ref):
        xv = x_ref[...]
        m = mod_ref[...]
        xh = xv * _rstd(xv)
        h_ref[...] = ((xh * w_ref[...]) * (1.0 + m[1:2]) + m[0:1]).astype(BF16)

    return _pcall(
        body, name=name, grid=(bsz, seq // ts),
        in_specs=[_row_specs(bsz, seq, d, ts), _vec_spec(d), _mod_spec()],
        out_specs=_row_specs(bsz, seq, d, ts),
        out_shape=jax.ShapeDtypeStruct((bsz, seq, d), BF16),
    )(x, w, mod)


def post1_pre2(x, out1, mod, post1, pre2, *, name):
    bsz, seq, d = x.shape
    ts = _pick(seq, (ROW_TILE, 256, 128))

    def body(x_ref, o_ref, mod_ref, p1_ref, p2_ref, x1_ref, h2_ref):
        m = mod_ref[...]
        ov = o_ref[...]
        x1 = x_ref[...] + m[2:3] * ((ov * _rstd(ov)) * p1_ref[...])
        x1_ref[...] = x1
        xh = x1 * _rstd(x1)
        h2_ref[...] = ((xh * p2_ref[...]) * (1.0 + m[4:5]) + m[3:4]).astype(BF16)

    return _pcall(
        body, name=name, grid=(bsz, seq // ts),
        in_specs=[_row_specs(bsz, seq, d, ts), _row_specs(bsz, seq, d, ts), _mod_spec(), _vec_spec(d), _vec_spec(d)],
        out_specs=[_row_specs(bsz, seq, d, ts), _row_specs(bsz, seq, d, ts)],
        out_shape=[jax.ShapeDtypeStruct((bsz, seq, d), F32), jax.ShapeDtypeStruct((bsz, seq, d), BF16)],
    )(x, out1, mod, post1, pre2)


def _acc_specs(d):
    per_batch = pl.BlockSpec((None, SUBLANES, d), lambda b, i: (b, 0, 0))
    glob = pl.BlockSpec((SUBLANES, d), lambda b, i: (0, 0))
    return per_batch, glob


def _accumulate(pb_ref, gl_ref, pb_rows, gl_rows, width):
    b, i = pl.program_id(0), pl.program_id(1)

    @pl.when(i == 0)
    def _():
        pb_ref[...] = jnp.zeros_like(pb_ref)

    @pl.when((b == 0) & (i == 0))
    def _():
        gl_ref[...] = jnp.zeros_like(gl_ref)

    pb_ref[...] += _stack_rows(pb_rows, width)
    gl_ref[...] += _stack_rows(gl_rows, width)


def _rms_bwd(d_n, n, r):
    return r * (d_n - n * jnp.mean(d_n * n, axis=-1, keepdims=True))


def final_bwd(x1, y2, target, mod, post2, *, name):
    bsz, seq, d = x1.shape
    ts = _pick(seq, (ROW_TILE, 256, 128))

    def body(x1_ref, y_ref, t_ref, mod_ref, p_ref, dx_ref, dy_ref, pb_ref, gl_ref):
        m = mod_ref[...]
        g2 = m[5:6]
        yv = y_ref[...]
        r = _rstd(yv)
        n = yv * r
        o = n * p_ref[...]
        diff = (x1_ref[...] + g2 * o) - t_ref[...]
        dx = diff * (1.0 / d)
        dx_ref[...] = dx
        d_o = dx * g2
        dy_ref[...] = _rms_bwd(d_o * p_ref[...], n, r).astype(BF16)
        _accumulate(pb_ref, gl_ref, [_colsum(dx * o)], [_colsum(d_o * n), _colsum(diff * diff) * (0.5 / d)], d)

    pb, gl = _acc_specs(d)
    rs = _row_specs(bsz, seq, d, ts)
    return _pcall(
        body, name=name, grid=(bsz, seq // ts),
        in_specs=[rs, rs, rs, _mod_spec(), _vec_spec(d)],
        out_specs=[rs, rs, pb, gl],
        out_shape=[jax.ShapeDtypeStruct((bsz, seq, d), F32), jax.ShapeDtypeStruct((bsz, seq, d), BF16),
                   jax.ShapeDtypeStruct((bsz, SUBLANES, d), F32), jax.ShapeDtypeStruct((SUBLANES, d), F32)],
    )(x1, y2, target, mod, post2)


def mid_bwd(d_h2, dx2, x1, out1, mod, pre2, post1, *, name):
    bsz, seq, d = x1.shape
    ts = _pick(seq, (ROW_TILE, 256, 128))

    def body(dh_ref, dx2_ref, x1_ref, o_ref, mod_ref, p2_ref, p1_ref, dx1_ref, do_ref, pb_ref, gl_ref):
        m = mod_ref[...]
        dh = dh_ref[...]
        x1 = x1_ref[...]
        r2 = _rstd(x1)
        xh = x1 * r2
        xw = xh * p2_ref[...]
        d_xw = dh * (1.0 + m[4:5])
        dx1 = dx2_ref[...] + _rms_bwd(d_xw * p2_ref[...], xh, r2)
        dx1_ref[...] = dx1
        ov = o_ref[...]
        r1 = _rstd(ov)
        n1 = ov * r1
        o1 = n1 * p1_ref[...]
        d_o1 = dx1 * m[2:3]
        do_ref[...] = _rms_bwd(d_o1 * p1_ref[...], n1, r1).astype(BF16)
        _accumulate(pb_ref, gl_ref, [_colsum(dh), _colsum(dh * xw), _colsum(dx1 * o1)],
                    [_colsum(d_xw * xh), _colsum(d_o1 * n1)], d)

    pb, gl = _acc_specs(d)
    rs = _row_specs(bsz, seq, d, ts)
    return _pcall(
        body, name=name, grid=(bsz, seq // ts),
        in_specs=[rs, rs, rs, rs, _mod_spec(), _vec_spec(d), _vec_spec(d)],
        out_specs=[rs, rs, pb, gl],
        out_shape=[jax.ShapeDtypeStruct((bsz, seq, d), F32), jax.ShapeDtypeStruct((bsz, seq, d), BF16),
                   jax.ShapeDtypeStruct((bsz, SUBLANES, d), F32), jax.ShapeDtypeStruct((SUBLANES, d), F32)],
    )(d_h2, dx2, x1, out1, mod, pre2, post1)


def first_bwd(d_h1, dx1, x, mod, pre1, *, name):
    bsz, seq, d = x.shape
    ts = _pick(seq, (ROW_TILE, 256, 128))

    def body(dh_ref, dx1_ref, x_ref, mod_ref, p_ref, gx_ref, pb_ref, gl_ref):
        m = mod_ref[...]
        dh = dh_ref[...]
        xv = x_ref[...]
        r = _rstd(xv)
        xh = xv * r
        xw = xh * p_ref[...]
        d_xw = dh * (1.0 + m[1:2])
        gx_ref[...] = dx1_ref[...] + _rms_bwd(d_xw * p_ref[...], xh, r)
        _accumulate(pb_ref, gl_ref, [_colsum(dh), _colsum(dh * xw)], [_colsum(d_xw * xh)], d)

    pb, gl = _acc_specs(d)
    rs = _row_specs(bsz, seq, d, ts)
    return _pcall(
        body, name=name, grid=(bsz, seq // ts),
        in_specs=[rs, rs, rs, _mod_spec(), _vec_spec(d)],
        out_specs=[rs, pb, gl],
        out_shape=[jax.ShapeDtypeStruct((bsz, seq, d), F32),
                   jax.ShapeDtypeStruct((bsz, SUBLANES, d), F32), jax.ShapeDtypeStruct((SUBLANES, d), F32)],
    )(d_h1, dx1, x, mod, pre1)


def merge_bwd(d_merged, ya, yb, gates, b_gate, *, name):
    bsz, seq, d = ya.shape
    ts = _pick(seq, (ROW_TILE, 256, 128))

    def body(dm_ref, ya_ref, yb_ref, g_ref, b_ref, dya_ref, dyb_ref, dg_ref, gl_ref):
        b, i = pl.program_id(0), pl.program_id(1)
        g = _sigmoid(g_ref[...] + b_ref[...])
        dm = dm_ref[...]
        ga, gb = g[:, :d], g[:, d:]
        dya_ref[...] = (dm * ga).astype(BF16)
        dyb_ref[...] = (dm * gb).astype(BF16)
        dg = jnp.concatenate([dm * ya_ref[...] * ga * (1.0 - ga), dm * yb_ref[...] * gb * (1.0 - gb)], axis=1)
        dg_ref[...] = dg.astype(BF16)

        @pl.when((b == 0) & (i == 0))
        def _():
            gl_ref[...] = jnp.zeros_like(gl_ref)

        gl_ref[...] += _stack_rows([_colsum(dg)], 2 * d)

    rs = _row_specs(bsz, seq, d, ts)
    rs2 = _row_specs(bsz, seq, 2 * d, ts)
    return _pcall(
        body, name=name, grid=(bsz, seq // ts),
        in_specs=[rs, rs, rs, rs2, _vec_spec(2 * d)],
        out_specs=[rs, rs, rs2, pl.BlockSpec((SUBLANES, 2 * d), lambda b, i: (0, 0))],
        out_shape=[jax.ShapeDtypeStruct((bsz, seq, d), BF16), jax.ShapeDtypeStruct((bsz, seq, d), BF16),
                   jax.ShapeDtypeStruct((bsz, seq, 2 * d), BF16), jax.ShapeDtypeStruct((SUBLANES, 2 * d), F32)],
    )(d_merged, ya, yb, gates, b_gate)


LRU_TILE = 256
N_LRU_BLOCKS = D_MODEL // LANES


def _block_mm(v, w_ref, transpose=False):
    vb = v.astype(BF16)
    outs = []
    for j in range(N_LRU_BLOCKS):
        blk = vb[:, LANES * j:LANES * (j + 1)]
        outs.append(_dot_nt(blk, w_ref[j]) if transpose else _dot(blk, w_ref[j]))
    return jnp.concatenate(outs, axis=1)


def _lru_gates(xc, wa_ref, ba, wx_ref, bx, sp):
    r = _sigmoid(_block_mm(xc, wa_ref) + ba)
    i = _sigmoid(_block_mm(xc, wx_ref) + bx)
    la = (-LRU_C * r) * sp
    a = jnp.exp(la)
    sq = jnp.sqrt(_neg_expm1(2.0 * la))
    return r, i, a, sq


def _group_roll(v, shift):
    rows, width = v.shape
    return pltpu.roll(v.reshape(rows // SUBLANES, SUBLANES, width), shift, 1).reshape(rows, width)


def _group_scan(a, b, reverse=False):
    row = lax.broadcasted_iota(jnp.int32, a.shape, 0) % SUBLANES
    for s in (1, 2, 4):
        take = (row < SUBLANES - s) if reverse else (row >= s)
        shift = SUBLANES - s if reverse else s
        b = jnp.where(take, a * _group_roll(b, shift) + b, b)
        a = jnp.where(take, a * _group_roll(a, shift), a)
    return a, b


def _prev8_spec(width, col_block, tile_rows):
    per = tile_rows // SUBLANES
    return pl.BlockSpec((None, SUBLANES, width), lambda b, i: (b, jnp.maximum(i * per - 1, 0), col_block))


def lru_fwd(pm, cw, cb, wa, ba, wx, bx, lam, w_pa, *, name):
    bsz, seq, _ = pm.shape
    d = D_MODEL
    ts = _pick(seq, (LRU_TILE, 128))

    def body(lx_ref, lxp_ref, lg_ref, cw_ref, cb_ref, wa_ref, ba_ref, wx_ref, bx_ref, lam_ref, wpa_ref,
             h_ref, pa_ref, ya_ref, kept_ref, hc, a_s, u_s):
        i = pl.program_id(1)

        @pl.when(i == 0)
        def _():
            hc[...] = jnp.zeros_like(hc)

        lx = lx_ref[...]
        prev8 = jnp.where(i == 0, 0.0, lxp_ref[...])
        xc = _conv4(lx, prev8, cw_ref[...], cb_ref[...])
        sp = _softplus(-lam_ref[...])
        r, ig, a, sq = _lru_gates(xc, wa_ref, ba_ref[...], wx_ref, bx_ref[...], sp)
        for k, kept in enumerate((xc, r, ig, a, sq)):
            kept_ref[:, k * d:(k + 1) * d] = kept
        a_s[...], u_s[...] = _group_scan(a, sq * (ig * xc))

        def step(g, h):
            r0 = pl.multiple_of(g * SUBLANES, SUBLANES)
            h8 = a_s[pl.ds(r0, SUBLANES), :] * h + u_s[pl.ds(r0, SUBLANES), :]
            h_ref[pl.ds(r0, SUBLANES), :] = h8
            return h8[SUBLANES - 1:SUBLANES]

        hc[...] = lax.fori_loop(0, ts // SUBLANES, step, hc[...])
        pa_ref[...] = (h_ref[...] * _gelu(lg_ref[...])).astype(BF16)
        ya_ref[...] = _dot(pa_ref[...], wpa_ref[...])

    vec = _vec_spec(d)
    wspec = pl.BlockSpec((N_LRU_BLOCKS, LANES, LANES), lambda b, i: (0, 0, 0))
    rs = _row_specs(bsz, seq, d, ts)
    return _pcall(
        body, name=name, grid=(bsz, seq // ts),
        in_specs=[pl.BlockSpec((None, ts, d), lambda b, i: (b, i, 0)), _prev8_spec(d, 0, ts),
                  pl.BlockSpec((None, ts, d), lambda b, i: (b, i, 1)),
                  pl.BlockSpec((4, d), lambda b, i: (0, 0)), vec, wspec, vec, wspec, vec, vec,
                  pl.BlockSpec(w_pa.shape, lambda b, i: (0, 0))],
        out_specs=[rs, rs, rs, _row_specs(bsz, seq, 5 * d, ts)],
        out_shape=[jax.ShapeDtypeStruct((bsz, seq, d), F32), jax.ShapeDtypeStruct((bsz, seq, d), BF16),
                   jax.ShapeDtypeStruct((bsz, seq, d), F32), jax.ShapeDtypeStruct((bsz, seq, 5 * d), F32)],
        scratch_shapes=[pltpu.VMEM((1, d), F32), pltpu.VMEM((ts, d), F32), pltpu.VMEM((ts, d), F32)],
    )(pm, pm, pm, cw, cb, wa, ba, wx, bx, lam, w_pa)


def lru_bwd(pm, h, kept, d_ya, cw, wa, wx, lam, wt_pa, wt_lru, *, name, bg=None):
    bsz, seq, _ = pm.shape
    d = D_MODEL
    ts = _pick(seq, (LRU_TILE, 128))
    nt = seq // ts
    per = ts // SUBLANES

    def rev(i):
        return nt - 1 - i

    def body(lx_ref, lg_ref, h_ref, hp_ref, kept_ref, dya_ref, cw_ref, wa_ref, wx_ref,
             lam_ref, wtpa_ref, wtl_ref, dl_ref, dh1_ref, dwa_ref, dwx_ref, rows_ref,
             carry, dxc_next, a_s, dh_s, acc_s, a0_s):
        b, i = pl.program_id(0), pl.program_id(1)
        t = rev(i)

        @pl.when(i == 0)
        def _():
            carry[...] = jnp.zeros_like(carry)
            dxc_next[...] = jnp.zeros_like(dxc_next)

        @pl.when((b == 0) & (i == 0))
        def _():
            dwa_ref[...] = jnp.zeros_like(dwa_ref)
            dwx_ref[...] = jnp.zeros_like(dwx_ref)
            rows_ref[...] = jnp.zeros_like(rows_ref)

        lx = lx_ref[...]
        lg = lg_ref[...]
        cwv = cw_ref[...]
        lam_v = lam_ref[...]
        sp = _softplus(-lam_v)
        xc, r, ig, a, sq = (kept_ref[:, k * d:(k + 1) * d] for k in range(5))
        hv = h_ref[...]
        d_pa = _dot(dya_ref[...], wtpa_ref[...])
        row = lax.broadcasted_iota(jnp.int32, a.shape, 0) % SUBLANES
        a_next = jnp.where(row < SUBLANES - 1, _group_roll(a, SUBLANES - 1), 1.0)
        a_s[...], dh_s[...] = _group_scan(a_next, d_pa * _gelu(lg), reverse=True)
        a0_s[...] = a

        def step(g, c):
            r0 = pl.multiple_of((per - 1 - g) * SUBLANES, SUBLANES)
            acc8 = a_s[pl.ds(r0, SUBLANES), :] * c + dh_s[pl.ds(r0, SUBLANES), :]
            acc_s[pl.ds(r0, SUBLANES), :] = acc8
            return a0_s[pl.ds(r0, SUBLANES), :][0:1] * acc8[0:1]

        carry[...] = lax.fori_loop(0, per, step, carry[...])
        d_u = acc_s[...]
        hprev8 = jnp.where(t == 0, 0.0, hp_ref[...])
        d_a = d_u * _shift_down(hv, hprev8, 1)
        d_sq = d_u * (ig * xc)
        d_i = d_u * (sq * xc)
        d_xc = d_u * (sq * ig)
        d_la = d_a * a - d_sq * (a * a) / sq
        d_pre_r = (d_la * (-LRU_C * sp)) * (r * (1.0 - r))
        d_pre_i = d_i * (ig * (1.0 - ig))
        d_xc = d_xc + _block_mm(d_pre_r, wa_ref, transpose=True) + _block_mm(d_pre_i, wx_ref, transpose=True)
        xcb = xc.astype(BF16)
        drb = d_pre_r.astype(BF16)
        dib = d_pre_i.astype(BF16)
        for j in range(N_LRU_BLOCKS):
            sl = slice(LANES * j, LANES * (j + 1))
            dwa_ref[j] += _dot_tn(xcb[:, sl], drb[:, sl])
            dwx_ref[j] += _dot_tn(xcb[:, sl], dib[:, sl])
        d_lx, d_cw, d_cb = _conv4_bwd(d_xc, dxc_next[...], lx, cwv)
        dxc_next[...] = d_xc[0:SUBLANES]
        d_lam = _colsum(d_la * (-LRU_C * r)) * (-_sigmoid(-lam_v))
        rows_ref[...] += _stack_rows([_colsum(d_pre_r), _colsum(d_pre_i), d_lam, d_cb] + d_cw, d)
        dl_ref[:, :d] = d_lx.astype(BF16)
        dl_ref[:, d:] = (d_pa * hv * _gelu_grad(lg)).astype(BF16)
        dh1_ref[...] = _dot(dl_ref[...], wtl_ref[...])

    vec = _vec_spec(d)
    wspec = pl.BlockSpec((N_LRU_BLOCKS, LANES, LANES), lambda b, i: (0, 0, 0))
    tile = lambda col: pl.BlockSpec((None, ts, d), lambda b, i: (b, rev(i), col))
    prev8 = lambda col: pl.BlockSpec((None, SUBLANES, d), lambda b, i: (b, jnp.maximum(rev(i) * per - 1, 0), col))
    whole = lambda v: pl.BlockSpec(v.shape, lambda b, i: (0, 0))
    return _run(
        body, (pm, pm, h, h, kept, d_ya, cw, wa, wx, lam, wt_pa, wt_lru), bg, name=name, grid=(bsz, nt),
        in_specs=[tile(0), tile(1), tile(0), prev8(0), pl.BlockSpec((None, ts, 5 * d), lambda b, i: (b, rev(i), 0)),
                  tile(0), pl.BlockSpec((4, d), lambda b, i: (0, 0)), wspec, wspec, vec,
                  whole(wt_pa), whole(wt_lru)],
        out_specs=[pl.BlockSpec((None, ts, 2 * d), lambda b, i: (b, rev(i), 0)), tile(0), wspec, wspec,
                   pl.BlockSpec((SUBLANES, d), lambda b, i: (0, 0))],
        out_shape=[jax.ShapeDtypeStruct((bsz, seq, 2 * d), BF16), jax.ShapeDtypeStruct((bsz, seq, d), F32),
                   jax.ShapeDtypeStruct((N_LRU_BLOCKS, LANES, LANES), F32),
                   jax.ShapeDtypeStruct((N_LRU_BLOCKS, LANES, LANES), F32),
                   jax.ShapeDtypeStruct((SUBLANES, d), F32)],
        scratch_shapes=[pltpu.VMEM((1, d), F32), pltpu.VMEM((SUBLANES, d), F32),
                        pltpu.VMEM((ts, d), F32), pltpu.VMEM((ts, d), F32), pltpu.VMEM((ts, d), F32),
                        pltpu.VMEM((ts, d), F32)])


L = SSD_CHUNK
N_PAIRS = SSD_HEADS // 2


def _ssd_common(conv, dt_raw, dtb, alog):
    sg = _sigmoid(conv)
    xa = conv * sg
    dtv = _softplus(dt_raw + dtb)
    a_neg = -jnp.exp(alog)
    rowi = lax.broadcasted_iota(jnp.int32, (L, L), 0)
    coli = lax.broadcasted_iota(jnp.int32, (L, L), 1)
    tril = (rowi >= coli).astype(F32)
    cs = _dot(tril, dtv * a_neg, precision=HI)
    return conv, sg, xa, dtv, a_neg, cs, rowi, coli


def _head_masks():
    lane = lax.broadcasted_iota(jnp.int32, (L, LANES), 1)
    return lane < SSD_HEAD_DIM


def _spread(v, p, first):
    return jnp.where(first[:v.shape[0]], v[:, 2 * p:2 * p + 1], v[:, 2 * p + 1:2 * p + 2])


def _place_head_sums(acc, z, p, first, lane1):
    rows = z.shape[0]
    s0 = jnp.sum(jnp.where(first[:rows], z, 0.0), axis=1, keepdims=True)
    s1 = jnp.sum(jnp.where(first[:rows], 0.0, z), axis=1, keepdims=True)
    lane = lane1[:rows]
    return acc + jnp.where(lane == 2 * p, s0, 0.0) + jnp.where(lane == 2 * p + 1, s1, 0.0)


def _stack_heads(v, first):
    return jnp.concatenate([jnp.where(first, v, 0.0), jnp.where(first, 0.0, v)], axis=0).astype(BF16)


def ssd_fwd(pm, dtr, cw, cb, dtb, alog, d_lanes, nw, w_pb, ya, gates, b_gate, w_out, *, name, bg=None):
    bsz, seq, _ = pm.shape
    nc = seq // L
    inner, cdim, d = SSD_INNER, SSD_CONV_DIM, D_MODEL

    def body(xbc_ref, xp_ref, z_ref, dt_ref, cw_ref, cb_ref, dtb_ref, alog_ref, dl_ref, nw_ref, wpb_ref,
             ya_ref, g_ref, bg_ref, wout_ref,
             y_ref, yn_ref, st_ref, yb_ref, conv_ref, mg_ref, out_ref, state):
        i = pl.program_id(1)

        @pl.when(i == 0)
        def _():
            state[...] = jnp.zeros_like(state)

        prev8 = jnp.where(i == 0, 0.0, xp_ref[...])
        conv = _conv4(xbc_ref[...], prev8, cw_ref[...], cb_ref[...])
        conv_ref[...] = conv
        _, _, xa, dtv, _, cs, rowi, coli = _ssd_common(conv, dt_ref[...], dtb_ref[...], alog_ref[...])
        cst = cs.T
        causal = rowi >= coli
        first = _head_masks()
        for g in range(SSD_GROUPS):
            bg = xa[:, inner + SSD_STATE * g:inner + SSD_STATE * (g + 1)].astype(BF16)
            cg = xa[:, inner + SSD_GROUPS * SSD_STATE + SSD_STATE * g:
                    inner + SSD_GROUPS * SSD_STATE + SSD_STATE * (g + 1)].astype(BF16)
            cbm = _dot_nt(cg, bg)
            for pp in range(2):
                p = 2 * g + pp
                sl = slice(LANES * p, LANES * (p + 1))
                ms = []
                for hh in (2 * p, 2 * p + 1):
                    seg = cs[:, hh:hh + 1] - cst[hh:hh + 1, :]
                    ms.append((cbm * jnp.exp(jnp.where(causal, seg, -jnp.inf))).astype(BF16))
                xsp = xa[:, sl]
                cs_p = _spread(cs, p, first)
                cs_last = cs_p[L - 1:L]
                xp = xsp * _spread(dtv, p, first)
                y_diag = _dot(jnp.concatenate(ms, axis=1), _stack_heads(xp, first))
                st = state[p]
                st_ref[p] = st
                y_off = _dot(cg, st.astype(BF16)) * jnp.exp(cs_p)
                y_ref[:, sl] = y_diag + y_off + dl_ref[:, sl] * xsp
                state[p] = st * jnp.exp(cs_last) + _dot_tn(bg, (xp * jnp.exp(cs_last - cs_p)).astype(BF16))
        zv = z_ref[...]
        yz = y_ref[...] * (zv * _sigmoid(zv))
        gw = inner // SSD_GROUPS
        for g in range(SSD_GROUPS):
            sl = slice(gw * g, gw * (g + 1))
            seg = yz[:, sl]
            yn_ref[:, sl] = ((seg * _rstd(seg)) * nw_ref[:, sl]).astype(BF16)
        yb = _dot(yn_ref[...], wpb_ref[...])
        yb_ref[...] = yb
        g = _sigmoid(g_ref[...] + bg_ref[...])
        mg_ref[...] = (g[:, :d] * ya_ref[...] + g[:, d:] * yb).astype(BF16)
        out_ref[...] = _dot(mg_ref[...], wout_ref[...])

    cvec = lambda w: pl.BlockSpec((1, w), lambda b, i: (0, 0))
    rows = lambda w: pl.BlockSpec((None, L, w), lambda b, i: (b, i, 0))
    outs, bg_outs = _run(
        body, (pm, pm, pm, dtr, cw, cb, dtb, alog, d_lanes, nw, w_pb, ya, gates, b_gate, w_out), bg, name=name,
        grid=(bsz, nc),
        in_specs=[pl.BlockSpec((None, L, cdim), lambda b, i: (b, i, 1)), _prev8_spec(cdim, 1, L),
                  pl.BlockSpec((None, L, inner), lambda b, i: (b, i, 1)),
                  pl.BlockSpec((None, L, LANES), lambda b, i: (b, i, 0)),
                  pl.BlockSpec((4, cdim), lambda b, i: (0, 0)), cvec(cdim), cvec(LANES), cvec(LANES),
                  cvec(inner), cvec(inner), pl.BlockSpec(w_pb.shape, lambda b, i: (0, 0)),
                  rows(d), rows(2 * d), cvec(2 * d), pl.BlockSpec(w_out.shape, lambda b, i: (0, 0))],
        out_specs=[rows(inner), rows(inner),
                   pl.BlockSpec((None, None, N_PAIRS, SSD_STATE, LANES), lambda b, i: (b, i, 0, 0, 0)),
                   rows(d), rows(cdim), rows(d), rows(d)],
        out_shape=[jax.ShapeDtypeStruct((bsz, seq, inner), F32), jax.ShapeDtypeStruct((bsz, seq, inner), BF16),
                   jax.ShapeDtypeStruct((bsz, nc, N_PAIRS, SSD_STATE, LANES), F32),
                   jax.ShapeDtypeStruct((bsz, seq, d), F32), jax.ShapeDtypeStruct((bsz, seq, cdim), F32),
                   jax.ShapeDtypeStruct((bsz, seq, d), BF16), jax.ShapeDtypeStruct((bsz, seq, d), F32)],
        scratch_shapes=[pltpu.VMEM((N_PAIRS, SSD_STATE, LANES), F32)])
    return outs, bg_outs


def ssd_bwd(pm, conv, dtr, y, states, d_yb, cw, dtb, alog, d_lanes, nw, wt_pb, wt_ssd, *, name):
    bsz, seq, _ = pm.shape
    nc = seq // L
    inner, cdim = SSD_INNER, SSD_CONV_DIM
    per = L // SUBLANES

    def rev(i):
        return nc - 1 - i

    def body(xbc_ref, conv_ref, z_ref, dt_ref, y_ref, st_ref, dyb_ref, cw_ref, dtb_ref, alog_ref,
             dl_ref, nw_ref, wtpb_ref, wts_ref, ds_ref, dh1_ref, ddt_ref, r4_ref, r2_ref, r1_ref,
             dstate, dconv_next, dxs_s, dbc_s):
        b, i = pl.program_id(0), pl.program_id(1)
        t = rev(i)

        @pl.when(i == 0)
        def _():
            dstate[...] = jnp.zeros_like(dstate)
            dconv_next[...] = jnp.zeros_like(dconv_next)

        @pl.when((b == 0) & (i == 0))
        def _():
            r4_ref[...] = jnp.zeros_like(r4_ref)
            r2_ref[...] = jnp.zeros_like(r2_ref)
            r1_ref[...] = jnp.zeros_like(r1_ref)

        xbc = xbc_ref[...]
        cwv = cw_ref[...]
        dt_in = dt_ref[...] + dtb_ref[...]
        conv = conv_ref[...]
        _, sg, xa, dtv, a_neg, cs, rowi, coli = _ssd_common(conv, dt_ref[...], dtb_ref[...], alog_ref[...])
        cst = cs.T
        causal = rowi >= coli
        anti = coli >= rowi
        first = _head_masks()
        lane1 = lax.broadcasted_iota(jnp.int32, (L, LANES), 1)

        yv = y_ref[...]
        zv = z_ref[...]
        sz = _sigmoid(zv)
        zs = zv * sz
        yz = yv * zs
        dyn = _dot(dyb_ref[...], wtpb_ref[...])
        gw = inner // SSD_GROUPS
        d_yz_parts, d_nw_parts = [], []
        for g in range(SSD_GROUPS):
            sl = slice(gw * g, gw * (g + 1))
            seg = yz[:, sl]
            r = _rstd(seg)
            n = seg * r
            d_nw_parts.append(_colsum(dyn[:, sl] * n))
            d_yz_parts.append(_rms_bwd(dyn[:, sl] * nw_ref[:, sl], n, r))
        d_yz = jnp.concatenate(d_yz_parts, axis=1)
        d_y = d_yz * zs
        ds_ref[:, :inner] = (d_yz * yv * (sz * (1.0 + zv * (1.0 - sz)))).astype(BF16)

        a1 = jnp.zeros((L, LANES), F32)
        a2 = jnp.zeros((L, LANES), F32)
        xs_dxt = jnp.zeros((L, LANES), F32)
        c0 = jnp.zeros((1, LANES), F32)
        d_dl = jnp.zeros((1, LANES), F32)
        for g in range(SSD_GROUPS):
            bsl = slice(inner + SSD_STATE * g, inner + SSD_STATE * (g + 1))
            csl = slice(inner + SSD_GROUPS * SSD_STATE + SSD_STATE * g,
                        inner + SSD_GROUPS * SSD_STATE + SSD_STATE * (g + 1))
            bg = xa[:, bsl].astype(BF16)
            cg = xa[:, csl].astype(BF16)
            cbm = _dot_nt(cg, bg)
            cbt = _dot_nt(bg, cg)
            d_cb = jnp.zeros((L, L), F32)
            d_bg = jnp.zeros((L, SSD_STATE), F32)
            d_cg = jnp.zeros((L, SSD_STATE), F32)
            for pp in range(2):
                p = 2 * g + pp
                sl = slice(LANES * p, LANES * (p + 1))
                xsp = xa[:, sl]
                dt_p = _spread(dtv, p, first)
                cs_p = _spread(cs, p, first)
                cs_last = cs_p[L - 1:L]
                e_p = jnp.exp(cs_p)
                w_p = jnp.exp(cs_last - cs_p)
                e_last = jnp.exp(cs_last)
                xp = xsp * dt_p
                xpb = xp.astype(BF16)
                dyp = d_y[:, sl]
                dypb = dyp.astype(BF16)
                dy_heads = (jnp.where(first, dyp, 0.0).astype(BF16), jnp.where(first, 0.0, dyp).astype(BF16))
                x_heads = (jnp.where(first, xp, 0.0).astype(BF16), jnp.where(first, 0.0, xp).astype(BF16))
                mts = []
                for k, hh in enumerate((2 * p, 2 * p + 1)):
                    col = cs[:, hh:hh + 1]
                    row = cst[hh:hh + 1, :]
                    dec = jnp.exp(jnp.where(causal, col - row, -jnp.inf))
                    dec_t = jnp.exp(jnp.where(anti, row - col, -jnp.inf))
                    gd = _dot_nt(dy_heads[k], xpb) * dec
                    d_cb = d_cb + gd
                    mt = cbt * dec_t
                    qd = gd * cbm - _dot_nt(x_heads[k], dypb) * mt
                    a1 = a1 + jnp.where(lane1 == hh, jnp.sum(qd, axis=1, keepdims=True), 0.0)
                    mts.append(mt.astype(BF16))
                dst = dstate[p]
                dstb = dst.astype(BF16)
                st = st_ref[p]
                stb = st.astype(BF16)
                dye = (dyp * e_p).astype(BF16)
                xw = (xp * w_p).astype(BF16)
                dx_off = w_p * _dot(bg, dstb)
                d_xp = _dot(jnp.concatenate(mts, axis=1), jnp.concatenate(dy_heads, axis=0)) + dx_off
                dxs_s[:, sl] = d_xp * dt_p + dyp * dl_ref[:, sl]
                a1 = _place_head_sums(a1, dyp * (_dot(cg, stb) * e_p), p, first, lane1)
                a2 = _place_head_sums(a2, xp * dx_off, p, first, lane1)
                xs_dxt = _place_head_sums(xs_dxt, d_xp * xsp, p, first, lane1)
                c0 = _place_head_sums(c0, _colsum(dst * st) * e_last, p, first, lane1)
                d_dl = _place_head_sums(d_dl, _colsum(dyp * xsp), p, first, lane1)
                d_cg = d_cg + _dot_nt(dye, stb)
                d_bg = d_bg + _dot_nt(xw, dstb)
                dstate[p] = dst * e_last + _dot_tn(cg, dye)
            d_cbb = d_cb.astype(BF16)
            dbc_s[:, SSD_STATE * g:SSD_STATE * (g + 1)] = d_bg + _dot_tn(d_cbb, cg)
            dbc_s[:, SSD_GROUPS * SSD_STATE + SSD_STATE * g:SSD_GROUPS * SSD_STATE + SSD_STATE * (g + 1)] = (
                d_cg + _dot(d_cbb, bg))

        d_da = (_dot(anti.astype(F32), a1, precision=HI) + _dot((rowi > coli).astype(F32), a2, precision=HI) + c0)
        d_dt = d_da * a_neg + xs_dxt
        d_alog = _colsum(d_da * dtv) * a_neg
        d_dtr = jnp.where(lane1 < SSD_HEADS, d_dt * _sigmoid(dt_in), 0.0)
        ddt_ref[...] = d_dtr.astype(BF16)
        d_xa = jnp.concatenate([dxs_s[...], dbc_s[...]], axis=1)
        d_conv = d_xa * (sg * (1.0 + conv * (1.0 - sg)))
        d_xbc, d_cw, d_cbias = _conv4_bwd(d_conv, dconv_next[...], xbc, cwv)
        dconv_next[...] = d_conv[0:SUBLANES]
        ds_ref[:, inner:] = d_xbc.astype(BF16)
        dh1_ref[...] = _dot(ds_ref[...], wts_ref[...])
        r4_ref[...] += _stack_rows([d_cbias] + d_cw, cdim)
        r2_ref[...] += _stack_rows([jnp.concatenate(d_nw_parts, axis=1)], inner)
        r1_ref[...] += _stack_rows([_colsum(d_dtr), d_alog, d_dl], LANES)

    cvec = lambda w: pl.BlockSpec((1, w), lambda b, i: (0, 0))
    return _pcall(
        body, name=name, grid=(bsz, nc),
        in_specs=[pl.BlockSpec((None, L, cdim), lambda b, i: (b, rev(i), 1)),
                  pl.BlockSpec((None, L, cdim), lambda b, i: (b, rev(i), 0)),
                  pl.BlockSpec((None, L, inner), lambda b, i: (b, rev(i), 1)),
                  pl.BlockSpec((None, L, LANES), lambda b, i: (b, rev(i), 0)),
                  pl.BlockSpec((None, L, inner), lambda b, i: (b, rev(i), 0)),
                  pl.BlockSpec((None, None, N_PAIRS, SSD_STATE, LANES), lambda b, i: (b, rev(i), 0, 0, 0)),
                  pl.BlockSpec((None, L, D_MODEL), lambda b, i: (b, rev(i), 0)),
                  pl.BlockSpec((4, cdim), lambda b, i: (0, 0)), cvec(LANES), cvec(LANES),
                  cvec(inner), cvec(inner), pl.BlockSpec(wt_pb.shape, lambda b, i: (0, 0)),
                  pl.BlockSpec(wt_ssd.shape, lambda b, i: (0, 0))],
        out_specs=[pl.BlockSpec((None, L, inner + cdim), lambda b, i: (b, rev(i), 0)),
                   pl.BlockSpec((None, L, D_MODEL), lambda b, i: (b, rev(i), 0)),
                   pl.BlockSpec((None, L, LANES), lambda b, i: (b, rev(i), 0)),
                   pl.BlockSpec((SUBLANES, cdim), lambda b, i: (0, 0)),
                   pl.BlockSpec((SUBLANES, inner), lambda b, i: (0, 0)),
                   pl.BlockSpec((SUBLANES, LANES), lambda b, i: (0, 0))],
        out_shape=[jax.ShapeDtypeStruct((bsz, seq, inner + cdim), BF16),
                   jax.ShapeDtypeStruct((bsz, seq, D_MODEL), F32),
                   jax.ShapeDtypeStruct((bsz, seq, LANES), BF16),
                   jax.ShapeDtypeStruct((SUBLANES, cdim), F32),
                   jax.ShapeDtypeStruct((SUBLANES, inner), F32),
                   jax.ShapeDtypeStruct((SUBLANES, LANES), F32)],
        scratch_shapes=[pltpu.VMEM((N_PAIRS, SSD_STATE, LANES), F32), pltpu.VMEM((SUBLANES, cdim), F32),
                        pltpu.VMEM((L, inner), F32), pltpu.VMEM((L, 2 * SSD_GROUPS * SSD_STATE), F32)],
    )(pm, conv, pm, dtr, y, states, d_yb, cw, dtb, alog, d_lanes, nw, wt_pb, wt_ssd)


def _lru_block_weights(w):
    w = w.reshape(N_LRU_BLOCKS, 2, LRU_HEAD_DIM, LRU_HEAD_DIM)
    z = jnp.zeros((N_LRU_BLOCKS, LRU_HEAD_DIM, LRU_HEAD_DIM), w.dtype)
    top = jnp.concatenate([w[:, 0], z], axis=2)
    bot = jnp.concatenate([z, w[:, 1]], axis=2)
    return jnp.concatenate([top, bot], axis=1).astype(BF16)


def _lru_block_grads(g):
    h = LRU_HEAD_DIM
    return jnp.stack([g[:, :h, :h], g[:, h:, h:]], axis=1).reshape(LRU_HEADS, h, h)


def _pad_lanes(v, width=LANES):
    return jnp.pad(v, ((0, 0), (0, width - v.shape[1])))


class NoExchange:
    def __init__(self, weights):
        self._weights, self.grads = weights, {}

    def weights_bg(self):
        return None

    def weights(self, bg_outs):
        return self._weights

    def grads_bg(self, grads):
        self.grads.update(grads)
        return None

    def grads_done(self, bg_outs):
        pass


def local_step(x, target, mod, big, small, plan):
    bsz, seq, d = x.shape
    t = bsz * seq
    flat = lambda v: v.reshape(t, v.shape[-1])
    unflat = lambda v: v.reshape(bsz, seq, v.shape[-1])

    wa_b = _lru_block_weights(small["lru_wa"])
    wx_b = _lru_block_weights(small["lru_wx"])
    dtb = _pad_lanes(small["ssd_dt_bias"])
    alog = _pad_lanes(small["ssd_a_log"])
    d_lanes = jnp.repeat(small["ssd_d"], SSD_HEAD_DIM, axis=1)

    lru_cols = 2 * D_MODEL
    wt = {"lru": big["w_main"][:, :lru_cols].T, "ssd": big["w_main"][:, lru_cols:].T, "gates": big["w_gates"].T,
          "dt": big["w_dt"].T}

    h1 = prenorm(x, small["pre_norm1"], mod, name="prenorm1")
    h1f = flat(h1)
    arriving = plan.weights_bg()
    if arriving is None:
        pm, arrived = mm_nn([(h1f, big["w_main"])], name="in_proj_main"), []
    else:
        pm, arrived = mm_nn([(h1f, big["w_main"])], name="in_proj_main", bg=arriving)
    pm = unflat(pm)
    big = dict(big, **plan.weights(arrived))
    for n in ("w_pa", "w_pb", "w_out", "w_ff1", "w_ff2"):
        wt[n] = big[n].T
    gates = unflat(mm_nn([(h1f, big["w_gates"])], name="in_proj_gates"))
    dtr = unflat(mm_nn([(h1f, big["w_dt"])], name="in_proj_dt"))
    lru_args = (small["lru_conv_w"], small["lru_conv_b"], wa_b, small["lru_ba"], wx_b, small["lru_bx"],
                small["lru_lambda"])
    h_lru, pa_in, ya, lru_kept = lru_fwd(pm, *lru_args, big["w_pa"], name="lru_fwd")
    ssd_args = (small["ssd_conv_w"], small["ssd_conv_b"], dtb, alog, d_lanes, small["ssd_norm_w"])
    (y_ssd, ynorm, states, yb, conv_ssd, merged, out1), _ = ssd_fwd(
        pm, dtr, *ssd_args, big["w_pb"], ya, gates, small["b_gate"], big["w_out"], name="ssd_fwd")
    x1, h2 = post1_pre2(x, out1, mod, small["post_norm1"], small["pre_norm2"], name="post1_pre2")
    f = mm_nn([(flat(h2), big["w_ff1"])], name="ff1")
    y2 = unflat(mm_nn([(f, big["w_ff2"])], a_fn=_relu_sq, name="ff2"))

    dx2, d_y2, pb_a, gl_a = final_bwd(x1, y2, target, mod, small["post_norm2"], name="final_bwd")
    d_y2f = flat(d_y2)
    d_f = mm_nn([(d_y2f, wt["w_ff2"])], out_dtype=BF16, extra=f,
                epi=lambda r, fv: r * (2.0 * jnp.maximum(fv, 0.0)), name="ff2_dx")
    g_ff2 = mm_tn(f, d_y2f, a_fn=_relu_sq, name="ff2_dw")
    d_h2 = unflat(mm_nn([(d_f, wt["w_ff1"])], name="ff1_dx"))
    g_ff1 = mm_tn(flat(h2), d_f, name="ff1_dw")
    dx1, d_out1, pb_b, gl_b = mid_bwd(d_h2, dx2, x1, out1, mod, small["pre_norm2"], small["post_norm1"],
                                      name="mid_bwd")
    d_out1f = flat(d_out1)
    d_merged = unflat(mm_nn([(d_out1f, wt["w_out"])], name="out_dx"))
    g_out = mm_tn(flat(merged), d_out1f, name="out_dw")
    d_ya, d_yb, d_gates, gl_c = merge_bwd(d_merged, ya, yb, gates, small["b_gate"], name="merge_bwd")
    g_pa = mm_tn(flat(pa_in), flat(d_ya), name="pa_dw")
    g_pb = mm_tn(flat(ynorm), flat(d_yb), name="pb_dw")
    leaving = plan.grads_bg({"w_pa": g_pa, "w_pb": g_pb, "w_out": g_out, "w_ff1": g_ff1, "w_ff2": g_ff2})
    (d_l, dh_lru, g_wa_b, g_wx_b, lru_rows), landed = lru_bwd(
        pm, h_lru, lru_kept, d_ya, small["lru_conv_w"], wa_b, wx_b, small["lru_lambda"], wt["w_pa"], wt["lru"],
        name="lru_bwd", bg=leaving)
    plan.grads_done(landed)
    d_s, dh_ssd, d_dt, r4, r2, r1 = ssd_bwd(pm, conv_ssd, dtr, y_ssd, states, d_yb, small["ssd_conv_w"], dtb, alog,
                                          d_lanes, small["ssd_norm_w"], wt["w_pb"], wt["ssd"], name="ssd_bwd")
    d_lf, d_sf, d_gf, d_dtf = flat(d_l), flat(d_s), flat(d_gates), flat(d_dt)
    g_in = jnp.concatenate([
        mm_tn(h1f, d_lf, name="in_dw_lru"), mm_tn(h1f, d_sf, name="in_dw_ssd"),
        mm_tn(h1f, d_dtf, name="in_dw_dt")[:, :SSD_HEADS], mm_tn(h1f, d_gf, name="in_dw_gates")], axis=1)
    leaving = plan.grads_bg({"w_in": g_in})
    partial = [flat(dh_lru), flat(dh_ssd)]
    if leaving is None:
        d_h1 = mm_nn([(d_gf, wt["gates"]), (d_dtf, wt["dt"])], add=partial, name="in_dx_gates")
    else:
        d_h1, landed = mm_nn([(d_gf, wt["gates"]), (d_dtf, wt["dt"])], add=partial, name="in_dx_gates", bg=leaving)
        plan.grads_done(landed)
    grad_x, pb_c, gl_d = first_bwd(unflat(d_h1), dx1, x, mod, small["pre_norm1"], name="first_bwd")

    d_mod = jnp.stack([pb_c[:, 0], pb_c[:, 1], pb_b[:, 2], pb_b[:, 0], pb_b[:, 1], pb_a[:, 0]], axis=1)
    loss_cols = gl_a[1:2]
    nh = SSD_HEADS
    small_grads = {
        "pre_norm1": gl_d[0:1], "post_norm1": gl_b[1:2], "b_gate": gl_c[0:1],
        "lru_conv_w": lru_rows[4:8], "lru_conv_b": lru_rows[3:4],
        "lru_wa": _lru_block_grads(g_wa_b), "lru_ba": lru_rows[0:1],
        "lru_wx": _lru_block_grads(g_wx_b), "lru_bx": lru_rows[1:2], "lru_lambda": lru_rows[2:3],
        "ssd_conv_w": r4[1:5], "ssd_conv_b": r4[0:1],
        "ssd_dt_bias": r1[0:1, :nh], "ssd_a_log": r1[1:2, :nh], "ssd_d": r1[2:3, :nh],
        "ssd_norm_w": r2[0:1], "pre_norm2": gl_b[0:1], "post_norm2": gl_a[0:1],
    }
    return loss_cols, grad_x, d_mod, small_grads


def _position():
    return lax.axis_index("x"), lax.axis_index("y"), lax.axis_index("c")


def _other_chips(x, y):
    return [(1 - x, y), (x, 1 - y), (1 - x, 1 - y)]


def allgather8(v, *, name):
    m_per, n = v.shape

    def body(x_ref, out_ref, send_sems, recv_sems, local_sem):
        x, y, c = _position()
        me, sibling = (x, y, c), (x, y, 1 - c)
        chips = _other_chips(x, y)

        def rows(px, py, pc):
            return out_ref.at[pl.ds((4 * px + 2 * py + pc) * m_per, m_per), :]

        def copy(k, block, to, src=None):
            return pltpu.make_async_remote_copy(
                src_ref=rows(*block) if src is None else src, dst_ref=rows(*block),
                send_sem=send_sems.at[k], recv_sem=recv_sems.at[k], device_id=to, device_id_type=MESH)

        mine = pltpu.make_async_copy(x_ref, rows(*me), local_sem)
        mine.start()
        first = [copy(0, me, sibling, src=x_ref)]
        first += [copy(1 + j, me, (*chip, c), src=x_ref) for j, chip in enumerate(chips)]
        for cp in first:
            cp.start()
        passed = [copy(4 + j, (*chip, c), sibling) for j, chip in enumerate(chips)]
        for j, chip in enumerate(chips):
            copy(1 + j, (*chip, c), me).wait_recv()
            passed[j].start()
        copy(0, sibling, me).wait_recv()
        for j, chip in enumerate(chips):
            copy(4 + j, (*chip, 1 - c), me).wait_recv()
        for cp in first + passed:
            cp.wait_send()
        mine.wait()

    return _pcall(
        body, name=name,
        out_shape=jax.ShapeDtypeStruct((N_DEV * m_per, n), v.dtype),
        in_specs=[pl.BlockSpec(memory_space=pltpu.VMEM)],
        out_specs=pl.BlockSpec(memory_space=pltpu.VMEM),
        scratch_shapes=[pltpu.SemaphoreType.DMA((7,)), pltpu.SemaphoreType.DMA((7,)), pltpu.SemaphoreType.DMA],
    )(v)


def gather_weights(shards, *, name):
    n = len(shards)
    half = [s.shape[0] // 2 for s in shards]
    widths = sorted({s.shape[1] for s in shards})
    chunk_rows = [_stage_rows(h, s.shape[1], itemsize=s.dtype.itemsize) for s, h in zip(shards, half)]
    plan = [(w, j, r0) for w in range(n) for j in range(N_CHIPS - 1) for r0 in range(0, half[w], chunk_rows[w])]

    def body(*refs):
        ins, outs = refs[:n], refs[n:2 * n]
        send_sems, recv_sems, local_sems, passed_sems = refs[2 * n:2 * n + 4]
        stage = refs[2 * n + 4:]
        bufs = {wd: stage[4 * i] for i, wd in enumerate(widths)}
        load_sems = {wd: stage[4 * i + 1] for i, wd in enumerate(widths)}
        stage_send = {wd: stage[4 * i + 2] for i, wd in enumerate(widths)}
        x, y, c = _position()
        me_chip = 2 * x + y
        chips = _other_chips(x, y)

        def piece(w, chip, core):
            return outs[w].at[chip, pl.ds(core * half[w], half[w]), :]

        def over_ici(w, j, chip, src=None):
            px, py = chips[j]
            dst = piece(w, chip, c)
            return pltpu.make_async_remote_copy(
                src_ref=dst if src is None else src, dst_ref=dst, send_sem=send_sems.at[3 * w + j],
                recv_sem=recv_sems.at[3 * w + j], device_id=(px, py, c), device_id_type=MESH)

        local = [pltpu.make_async_copy(ins[w], outs[w].at[me_chip], local_sems.at[w]) for w in range(n)]
        for cp in local:
            cp.start()
        sent = []
        for w in range(n):
            for j in range(N_CHIPS - 1):
                cp = over_ici(w, j, me_chip, src=ins[w].at[pl.ds(c * half[w], half[w]), :])
                cp.start()
                sent.append(cp)
        chunks = []
        for idx, (w, j, r0) in enumerate(plan):
            wd, rb = shards[w].shape[1], chunk_rows[w]
            k = 2 * chips[j][0] + chips[j][1]

            def make(staged, slot, idx=idx, w=w, k=k, r0=r0, wd=wd, rb=rb):
                return pltpu.make_async_remote_copy(
                    src_ref=staged, dst_ref=outs[w].at[k, pl.ds(c * half[w] + r0, rb), :],
                    send_sem=stage_send[wd].at[slot], recv_sem=passed_sems.at[idx],
                    device_id=(x, y, 1 - c), device_id_type=MESH), True

            chunk = (wd, outs[w].at[k, pl.ds(c * half[w] + r0, rb), :], [make])
            if r0 == 0:
                chunk += (lambda w=w, j=j, k=k: over_ici(w, j, k).wait_recv(),)
            chunks.append(chunk)
        _staged(chunks, bufs, load_sems)
        for idx, (w, j, r0) in enumerate(plan):
            wd = shards[w].shape[1]
            k = 2 * chips[j][0] + chips[j][1]
            landed = outs[w].at[k, pl.ds((1 - c) * half[w] + r0, chunk_rows[w]), :]
            pltpu.make_async_remote_copy(
                src_ref=landed, dst_ref=landed, send_sem=stage_send[wd].at[0], recv_sem=passed_sems.at[idx],
                device_id=(x, y, 1 - c), device_id_type=MESH).wait_recv()
        for cp in sent:
            cp.wait_send()
        for cp in local:
            cp.wait()

    stage_rows = [(wd, max(r for s, r in zip(shards, chunk_rows) if s.shape[1] == wd)) for wd in widths]
    return _pcall(
        body, name=name,
        out_shape=[jax.ShapeDtypeStruct((N_CHIPS,) + s.shape, s.dtype) for s in shards],
        in_specs=[ANY] * n, out_specs=[ANY] * n,
        scratch_shapes=[pltpu.SemaphoreType.DMA((3 * n,)), pltpu.SemaphoreType.DMA((3 * n,)),
                        pltpu.SemaphoreType.DMA((n,)), pltpu.SemaphoreType.DMA((len(plan),))]
        + _stage_scratch(stage_rows, shards[0].dtype),
    )(*shards)


STAGE_BYTES = 2 << 20


def _stage_rows(rows, width, itemsize=4):
    return _pick(rows, tuple(t for t in (1024, 512, 256, 128, 64, 32, 16, 8) if t * width * itemsize <= STAGE_BYTES * 3 // 2))


def _staged(chunks, bufs, load_sems):
    count, pending = {}, {}

    def load(i):
        cls, src = chunks[i][0], chunks[i][1]
        if len(chunks[i]) > 3:
            chunks[i][3]()
        slot = count.get(cls, 0) % 2
        count[cls] = count.get(cls, 0) + 1
        for cp, remote in pending.pop((cls, slot), []):
            if remote:
                cp.wait_send()
            else:
                cp.wait()
        staged = bufs[cls].at[slot, pl.ds(0, src.shape[0]), :]
        ld = pltpu.make_async_copy(src, staged, load_sems[cls].at[slot])
        ld.start()
        return ld, cls, slot, staged

    cur = load(0)
    for i in range(len(chunks)):
        nxt = load(i + 1) if i + 1 < len(chunks) else None
        ld, cls, slot, staged = cur
        ld.wait()
        started = []
        for make in chunks[i][2]:
            cp, remote = make(staged, slot)
            cp.start()
            started.append((cp, remote))
        pending[(cls, slot)] = started
        cur = nxt
    for started in pending.values():
        for cp, remote in started:
            if remote:
                cp.wait_send()
            else:
                cp.wait()


def _stage_scratch(widths_rows, dtype):
    scratch = []
    for width, rows in widths_rows:
        scratch += [pltpu.VMEM((2, rows, width), dtype), pltpu.SemaphoreType.DMA((2,)), pltpu.SemaphoreType.DMA((2,)),
                    pltpu.SemaphoreType.DMA((2,))]
    return scratch


def send_half_to_sibling(grads, *, name):
    n = len(grads)
    half = [g.shape[1] // 2 for g in grads]
    widths = sorted({g.shape[2] for g in grads})
    chunk_rows = [_stage_rows(h, g.shape[2]) for g, h in zip(grads, half)]
    plan = [(w, k, r0) for w in range(n) for k in range(N_CHIPS) for r0 in range(0, half[w], chunk_rows[w])]

    def body(*refs):
        ins, theirs = refs[:n], refs[n:2 * n]
        recv_sems = refs[2 * n]
        stage = refs[2 * n + 1:]
        bufs = {wd: stage[4 * i] for i, wd in enumerate(widths)}
        load_sems = {wd: stage[4 * i + 1] for i, wd in enumerate(widths)}
        send_sems = {wd: stage[4 * i + 2] for i, wd in enumerate(widths)}
        x, y, c = _position()
        chunks = []
        for idx, (w, k, r0) in enumerate(plan):
            wd = grads[w].shape[2]
            rb = chunk_rows[w]

            def make(staged, slot, idx=idx, w=w, k=k, r0=r0, wd=wd, rb=rb):
                return pltpu.make_async_remote_copy(
                    src_ref=staged, dst_ref=theirs[w].at[k, pl.ds(r0, rb), :], send_sem=send_sems[wd].at[slot],
                    recv_sem=recv_sems.at[idx], device_id=(x, y, 1 - c), device_id_type=MESH), True

            chunks.append((wd, ins[w].at[k, pl.ds((1 - c) * half[w] + r0, rb), :], [make]))
        _staged(chunks, bufs, load_sems)
        for idx, (w, k, r0) in enumerate(plan):
            wd = grads[w].shape[2]
            landed = theirs[w].at[k, pl.ds(r0, chunk_rows[w]), :]
            pltpu.make_async_remote_copy(
                src_ref=landed, dst_ref=landed, send_sem=send_sems[wd].at[0], recv_sem=recv_sems.at[idx],
                device_id=(x, y, 1 - c), device_id_type=MESH).wait_recv()

    stage_rows = [(wd, max(r for g, r in zip(grads, chunk_rows) if g.shape[2] == wd)) for wd in widths]
    return _pcall(
        body, name=name,
        out_shape=[jax.ShapeDtypeStruct((N_CHIPS, h, g.shape[2]), g.dtype) for g, h in zip(grads, half)],
        in_specs=[ANY] * n, out_specs=[ANY] * n,
        scratch_shapes=[pltpu.SemaphoreType.DMA((len(plan),))] + _stage_scratch(stage_rows, F32),
    )(*grads)


def _chip_exchange_background(arrays, out_shapes, src_of, dst_of, landed_of, own_of):
    n = len(arrays)

    def copies(ins, outs, scr):
        send_sems, recv_sems, local_sems = scr
        x, y, c = _position()
        me_chip = 2 * x + y
        local, sends, recvs = [], [], []
        for w in range(n):
            local.append(pltpu.make_async_copy(*own_of(ins[w], outs[w], w, me_chip), local_sems.at[w]))
            for j, (px, py) in enumerate(_other_chips(x, y)):
                sems = dict(send_sem=send_sems.at[3 * w + j], recv_sem=recv_sems.at[3 * w + j],
                            device_id=(px, py, c), device_id_type=MESH)
                sends.append(pltpu.make_async_remote_copy(
                    src_ref=src_of(ins[w], w, 2 * px + py, me_chip, c), dst_ref=dst_of(outs[w], w, me_chip, c), **sems))
                landed = landed_of(outs[w], w, 2 * px + py, c)
                recvs.append(pltpu.make_async_remote_copy(src_ref=landed, dst_ref=landed, **sems))
        return local, sends, recvs

    def start(ins, outs, scr):
        local, sends, _ = copies(ins, outs, scr)
        for cp in local + sends:
            cp.start()

    def finish(ins, outs, scr):
        local, sends, recvs = copies(ins, outs, scr)
        for cp in recvs:
            cp.wait_recv()
        for cp in sends:
            cp.wait_send()
        for cp in local:
            cp.wait()

    scratch = [pltpu.SemaphoreType.DMA((3 * n,)), pltpu.SemaphoreType.DMA((3 * n,)), pltpu.SemaphoreType.DMA((n,))]
    return Background(arrays, out_shapes, scratch, start, finish)


def scatter_background(parts):
    return _chip_exchange_background(
        parts, [jax.ShapeDtypeStruct(p.shape, p.dtype) for p in parts],
        src_of=lambda ref, w, peer, me, c: ref.at[peer], dst_of=lambda ref, w, me, c: ref.at[me],
        landed_of=lambda ref, w, peer, c: ref.at[peer], own_of=lambda i, o, w, me: (i.at[me], o.at[me]))


def gather_halves_background(shards):
    half = [s.shape[0] // 2 for s in shards]
    rows = lambda w, c: pl.ds(c * half[w], half[w])
    return _chip_exchange_background(
        shards, [jax.ShapeDtypeStruct((N_CHIPS,) + s.shape, s.dtype) for s in shards],
        src_of=lambda ref, w, peer, me, c: ref.at[rows(w, c), :], dst_of=lambda ref, w, me, c: ref.at[me, rows(w, c), :],
        landed_of=lambda ref, w, peer, c: ref.at[peer, rows(w, c), :], own_of=lambda i, o, w, me: (i, o.at[me]))


def fill_other_half(gathered, *, name):
    n = len(gathered)
    half = [g.shape[1] // 2 for g in gathered]
    widths = sorted({g.shape[2] for g in gathered})
    chunk_rows = [_stage_rows(h, g.shape[2], itemsize=2) for g, h in zip(gathered, half)]
    plan = [(w, j, r0) for w in range(n) for j in range(N_CHIPS - 1) for r0 in range(0, half[w], chunk_rows[w])]

    def body(*refs):
        ins, outs = refs[:n], refs[n:2 * n]
        recv_sems = refs[2 * n]
        stage = refs[2 * n + 1:]
        bufs = {wd: stage[4 * i] for i, wd in enumerate(widths)}
        load_sems = {wd: stage[4 * i + 1] for i, wd in enumerate(widths)}
        send_sems = {wd: stage[4 * i + 2] for i, wd in enumerate(widths)}
        x, y, c = _position()
        chips = _other_chips(x, y)
        chunks = []
        for idx, (w, j, r0) in enumerate(plan):
            wd, rb = gathered[w].shape[2], chunk_rows[w]
            k = 2 * chips[j][0] + chips[j][1]

            def make(staged, slot, idx=idx, w=w, k=k, r0=r0, wd=wd, rb=rb):
                return pltpu.make_async_remote_copy(
                    src_ref=staged, dst_ref=outs[w].at[k, pl.ds(c * half[w] + r0, rb), :],
                    send_sem=send_sems[wd].at[slot], recv_sem=recv_sems.at[idx],
                    device_id=(x, y, 1 - c), device_id_type=MESH), True

            chunks.append((wd, ins[w].at[k, pl.ds(c * half[w] + r0, rb), :], [make]))
        _staged(chunks, bufs, load_sems)
        for idx, (w, j, r0) in enumerate(plan):
            wd = gathered[w].shape[2]
            k = 2 * chips[j][0] + chips[j][1]
            landed = outs[w].at[k, pl.ds((1 - c) * half[w] + r0, chunk_rows[w]), :]
            pltpu.make_async_remote_copy(
                src_ref=landed, dst_ref=landed, send_sem=send_sems[wd].at[0], recv_sem=recv_sems.at[idx],
                device_id=(x, y, 1 - c), device_id_type=MESH).wait_recv()

    stage_rows = [(wd, max(r for g, r in zip(gathered, chunk_rows) if g.shape[2] == wd)) for wd in widths]
    return _pcall(
        body, name=name, out_shape=[jax.ShapeDtypeStruct(g.shape, g.dtype) for g in gathered],
        in_specs=[ANY] * n, out_specs=[ANY] * n, input_output_aliases={w: w for w in range(n)},
        scratch_shapes=[pltpu.SemaphoreType.DMA((len(plan),))] + _stage_scratch(stage_rows, gathered[0].dtype),
    )(*gathered)


def join_with_sibling(halves, *, name):
    n = len(halves)
    widths = sorted({h.shape[1] for h in halves})
    chunk_rows = [_stage_rows(h.shape[0], h.shape[1]) for h in halves]
    plan = [(w, r0) for w in range(n) for r0 in range(0, halves[w].shape[0], chunk_rows[w])]

    def body(*refs):
        ins, outs = refs[:n], refs[n:2 * n]
        recv_sems = refs[2 * n]
        stage = refs[2 * n + 1:]
        bufs = {wd: stage[4 * i] for i, wd in enumerate(widths)}
        load_sems = {wd: stage[4 * i + 1] for i, wd in enumerate(widths)}
        send_sems = {wd: stage[4 * i + 2] for i, wd in enumerate(widths)}
        store_sems = {wd: stage[4 * i + 3] for i, wd in enumerate(widths)}
        x, y, c = _position()
        chunks = []
        for idx, (w, r0) in enumerate(plan):
            h, wd = halves[w].shape
            rb = chunk_rows[w]

            def to_sibling(staged, slot, idx=idx, w=w, r0=r0, h=h, wd=wd, rb=rb):
                return pltpu.make_async_remote_copy(
                    src_ref=staged, dst_ref=outs[w].at[pl.ds(c * h + r0, rb), :], send_sem=send_sems[wd].at[slot],
                    recv_sem=recv_sems.at[idx], device_id=(x, y, 1 - c), device_id_type=MESH), True

            def to_mine(staged, slot, w=w, r0=r0, h=h, wd=wd, rb=rb):
                return pltpu.make_async_copy(staged, outs[w].at[pl.ds(c * h + r0, rb), :], store_sems[wd].at[slot]), False

            chunks.append((wd, ins[w].at[pl.ds(r0, rb), :], [to_sibling, to_mine]))
        _staged(chunks, bufs, load_sems)
        for idx, (w, r0) in enumerate(plan):
            h, wd = halves[w].shape
            landed = outs[w].at[pl.ds((1 - c) * h + r0, chunk_rows[w]), :]
            pltpu.make_async_remote_copy(
                src_ref=landed, dst_ref=landed, send_sem=send_sems[wd].at[0], recv_sem=recv_sems.at[idx],
                device_id=(x, y, 1 - c), device_id_type=MESH).wait_recv()

    stage_rows = [(wd, max(r for h, r in zip(halves, chunk_rows) if h.shape[1] == wd)) for wd in widths]
    return _pcall(
        body, name=name,
        out_shape=[jax.ShapeDtypeStruct((2 * h.shape[0], h.shape[1]), h.dtype) for h in halves],
        in_specs=[ANY] * n, out_specs=[ANY] * n,
        scratch_shapes=[pltpu.SemaphoreType.DMA((len(plan),))] + _stage_scratch(stage_rows, F32),
    )(*halves)


def _row_tile(rows, cols, itemsize=4, budget=2 << 20):
    for t in (1024, 512, 256, 128, 64, 32, 16, 8):
        if rows % t == 0 and t * cols * itemsize <= budget:
            return t
    return rows


def add_half_to_bf16(core, full, theirs, *, name):
    k, r, c = theirs.shape
    tr = _row_tile(r, c)
    nb = r // tr

    def body(core_ref, a_ref, b_ref, o_ref):
        o_ref[...] = (a_ref[...] + b_ref[...]).astype(BF16)

    spec = pl.BlockSpec((None, tr, c), lambda i, j, core_ref: (i, j, 0))
    grid_spec = pltpu.PrefetchScalarGridSpec(
        num_scalar_prefetch=1, grid=(k, nb),
        in_specs=[pl.BlockSpec((None, tr, c), lambda i, j, core_ref: (i, core_ref[0] * nb + j, 0)), spec],
        out_specs=spec)
    return _pcall(body, name=name, grid_spec=grid_spec,
                  out_shape=jax.ShapeDtypeStruct(theirs.shape, BF16))(core, full, theirs)


def sum_blocks(v, *, name):
    k, r, c = v.shape
    tr = _row_tile(r, c * k)

    def body(v_ref, o_ref):
        acc = v_ref[0].astype(F32)
        for j in range(1, k):
            acc = acc + v_ref[j].astype(F32)
        o_ref[...] = acc

    return _pcall(body, name=name, grid=(r // tr,),
                  in_specs=[pl.BlockSpec((k, tr, c), lambda i: (0, i, 0))],
                  out_specs=pl.BlockSpec((tr, c), lambda i: (i, 0)),
                  out_shape=jax.ShapeDtypeStruct((r, c), F32))(v)


def adamw(w, g, m, v, *, name):
    r, c = w.shape
    tr = _row_tile(r, c, budget=1 << 20)
    m_scale = 1.0 / (1.0 - ADAM_B1 ** ADAM_STEP)
    v_scale = 1.0 / (1.0 - ADAM_B2 ** ADAM_STEP)

    def body(w_ref, g_ref, m_ref, v_ref, d_ref, nm_ref, nv_ref):
        gv = g_ref[...]
        nm = ADAM_B1 * m_ref[...] + (1.0 - ADAM_B1) * gv
        nv = ADAM_B2 * v_ref[...] + (1.0 - ADAM_B2) * (gv * gv)
        nm_ref[...] = nm
        nv_ref[...] = nv
        d_ref[...] = -ADAM_LR * ((nm * m_scale) / (jnp.sqrt(nv * v_scale) + ADAM_EPS) + ADAM_WD * w_ref[...])

    spec = pl.BlockSpec((tr, c), lambda i: (i, 0))
    return _pcall(body, name=name, grid=(r // tr,), in_specs=[spec] * 4, out_specs=[spec] * 3,
                  out_shape=[jax.ShapeDtypeStruct((r, c), F32)] * 3)(w, g, m, v)


def ada_fwd(c_all, w_shard, b_shard, *, name):
    bsz, d = c_all.shape
    ncol = w_shard.shape[1]

    def body(c_ref, w_ref, b_ref, o_ref):
        cv = c_ref[...]
        act = (cv * _sigmoid(cv)).astype(BF16)
        o_ref[...] = _dot(act, w_ref[...].astype(BF16)) + b_ref[...]

    tn = _pick(ncol, (512, 256, 128))
    return _pcall(body, name=name, grid=(ncol // tn,),
                  in_specs=[pl.BlockSpec((bsz, d), lambda j: (0, 0)), pl.BlockSpec((d, tn), lambda j: (0, j)),
                            pl.BlockSpec((1, tn), lambda j: (0, j))],
                  out_specs=pl.BlockSpec((bsz, tn), lambda j: (0, j)),
                  out_shape=jax.ShapeDtypeStruct((bsz, ncol), F32))(c_all, w_shard, b_shard)


def ada_bwd(c_all, d_mod_all, d_mod_cols, *, name):
    bsz, d = c_all.shape
    ncol = d_mod_cols.shape[1]
    nall = d_mod_all.shape[1]

    def body(c_ref, da_ref, dc_ref, gw_ref, gb_ref):
        cv = c_ref[...]
        act = (cv * _sigmoid(cv)).astype(BF16)
        gw_ref[...] = _dot_tn(act, dc_ref[...].astype(BF16))
        gb_ref[...] = _colsum(da_ref[...])

    full = lambda s: pl.BlockSpec(s, lambda: (0,) * len(s))
    return _pcall(body, name=name,
                  in_specs=[full((bsz, d)), full((bsz, nall)), full((bsz, ncol))],
                  out_specs=[full((d, ncol)), full((1, nall))],
                  out_shape=[jax.ShapeDtypeStruct((d, ncol), F32), jax.ShapeDtypeStruct((1, nall), F32)],
                  )(c_all, d_mod_all, d_mod_cols)


WEIGHT_NAMES = ['w_ada', 'b_ada', 'pre_norm1', 'post_norm1', 'w_in', 'b_gate', 'lru_conv_w', 'lru_conv_b', 'lru_wa',
                'lru_ba', 'lru_wx', 'lru_bx', 'lru_lambda', 'w_pa', 'ssd_conv_w', 'ssd_conv_b', 'ssd_dt_bias',
                'ssd_a_log', 'ssd_d', 'ssd_norm_w', 'w_pb', 'w_out', 'pre_norm2', 'post_norm2', 'w_ff1', 'w_ff2']
BIG_NAMES = ['w_in', 'w_pa', 'w_pb', 'w_out', 'w_ff1', 'w_ff2']
COLUMN_SHARDED = ('w_in', 'w_ff1')
SMALL_NAMES = [n for n in WEIGHT_NAMES if n not in BIG_NAMES + ['w_ada', 'b_ada']]
SHARDED_SMALL = ('lru_conv_w', 'ssd_conv_w')
PACK_WIDTH = 1024


def _whole(name, gathered):
    if name in COLUMN_SHARDED:
        return jnp.transpose(gathered, (1, 0, 2)).reshape(gathered.shape[1], N_CHIPS * gathered.shape[2])
    return gathered.reshape(N_CHIPS * gathered.shape[1], gathered.shape[2])


def _by_chip(name, g):
    if name in COLUMN_SHARDED:
        return jnp.transpose(g.reshape(g.shape[0], N_CHIPS, g.shape[1] // N_CHIPS), (1, 0, 2))
    return g.reshape(N_CHIPS, g.shape[0] // N_CHIPS, g.shape[1])


class ChipExchange:
    def __init__(self, shards, core):
        self.shards, self.core = shards, core
        self.pending, self.halves = [], {}

    def weights_bg(self):
        return gather_halves_background(list(self.shards.values()))

    def weights(self, arrived):
        swapped = fill_other_half(arrived, name="weights_from_sibling")
        return {n: _whole(n, g) for n, g in zip(self.shards, swapped)}

    def grads_bg(self, grads):
        self.pending = list(grads)
        by_chip = [_by_chip(n, g) for n, g in grads.items()]
        theirs = send_half_to_sibling(by_chip, name="grads_to_sibling_" + self.pending[0])
        sums = [add_half_to_bf16(self.core, a, b, name="add_cores_" + n)
                for n, a, b in zip(self.pending, by_chip, theirs)]
        return scatter_background(sums)

    def grads_done(self, landed):
        for n, p in zip(self.pending, landed):
            self.halves[n] = sum_blocks(p, name="add_chips_" + n)

    def reduced(self):
        names = list(self.halves)
        return dict(zip(names, join_with_sibling([self.halves[n] for n in names], name="grads_join")))


def _pack(parts):
    flat = jnp.concatenate([p.reshape(-1).astype(F32) for p in parts])
    rows = -(-flat.shape[0] // (PACK_WIDTH * SUBLANES)) * SUBLANES
    return jnp.pad(flat, (0, rows * PACK_WIDTH - flat.shape[0])).reshape(rows, PACK_WIDTH)


def _unpack(packed, shapes):
    flat = packed.reshape(-1)
    out, pos = [], 0
    for s in shapes:
        size = int(np.prod(s))
        out.append(flat[pos:pos + size].reshape(s))
        pos += size
    return out


def kernel(x, c, w_ada, b_ada, pre_norm1, post_norm1, w_in, b_gate, lru_conv_w, lru_conv_b, lru_wa, lru_ba, lru_wx, lru_bx, lru_lambda, w_pa, ssd_conv_w, ssd_conv_b, ssd_dt_bias, ssd_a_log, ssd_d, ssd_norm_w, w_pb, w_out, pre_norm2, post_norm2, w_ff1, w_ff2, loss_target, m_w_ada, m_b_ada, m_pre_norm1, m_post_norm1, m_w_in, m_b_gate, m_lru_conv_w, m_lru_conv_b, m_lru_wa, m_lru_ba, m_lru_wx, m_lru_bx, m_lru_lambda, m_w_pa, m_ssd_conv_w, m_ssd_conv_b, m_ssd_dt_bias, m_ssd_a_log, m_ssd_d, m_ssd_norm_w, m_w_pb, m_w_out, m_pre_norm2, m_post_norm2, m_w_ff1, m_w_ff2, v_w_ada, v_b_ada, v_pre_norm1, v_post_norm1, v_w_in, v_b_gate, v_lru_conv_w, v_lru_conv_b, v_lru_wa, v_lru_ba, v_lru_wx, v_lru_bx, v_lru_lambda, v_w_pa, v_ssd_conv_w, v_ssd_conv_b, v_ssd_dt_bias, v_ssd_a_log, v_ssd_d, v_ssd_norm_w, v_w_pb, v_w_out, v_pre_norm2, v_post_norm2, v_w_ff1, v_w_ff2):
    given = dict(locals())
    bsz, seq, d = x.shape
    my_x, my_y, my_c = lax.axis_index("x"), lax.axis_index("y"), lax.axis_index("c")
    chip = 2 * my_x + my_y
    dev = 2 * chip + my_c
    strip = lambda a: a if a.ndim == 2 else a[0]
    w = {n: strip(given[n]) for n in WEIGHT_NAMES}
    m = {n: strip(given["m_" + n]) for n in WEIGHT_NAMES}
    v = {n: strip(given["v_" + n]) for n in WEIGHT_NAMES}

    first_shapes = [c.shape] + [w[n].shape for n in SHARDED_SMALL]
    first = allgather8(_pack([c] + [w[n] for n in SHARDED_SMALL]), name="gather_c_conv")
    first = first.reshape(N_DEV, -1, PACK_WIDTH)
    per_dev = [_unpack(first[k], first_shapes) for k in range(N_DEV)]
    c_all = jnp.concatenate([p[0] for p in per_dev], axis=0)
    conv_full = {n: jnp.concatenate([per_dev[2 * k][1 + i] for k in range(N_CHIPS)], axis=1)
                 for i, n in enumerate(SHARDED_SMALL)}

    ncol = w["w_ada"].shape[1]
    b_cols = lax.dynamic_slice(b_ada, (0, chip * ncol), (1, ncol))
    mod_cols = ada_fwd(c_all, w["w_ada"], b_cols, name="ada_fwd")
    mod_all = allgather8(mod_cols, name="gather_mod").reshape(N_CHIPS, 2, N_DEV * bsz, ncol)[:, 0]
    mod_all = jnp.transpose(mod_all, (1, 0, 2)).reshape(N_DEV * bsz, N_CHIPS * ncol)
    mod = lax.dynamic_slice(mod_all, (dev * bsz, 0), (bsz, 6 * d)).reshape(bsz, 6, d)
    mod = jnp.pad(mod, ((0, 0), (0, 2), (0, 0)))

    w_in_full = _whole("w_in", gather_weights([w["w_in"].astype(BF16)], name="gather_w_in")[0])
    big = {"w_main": w_in_full[:, :8192],
           "w_dt": jnp.pad(w_in_full[:, 8192:8192 + SSD_HEADS], ((0, 0), (0, LANES - SSD_HEADS))),
           "w_gates": w_in_full[:, 8192 + SSD_HEADS:]}
    small = {n: w[n] for n in SMALL_NAMES}
    small.update(conv_full)
    plan = ChipExchange({n: w[n].astype(BF16) for n in BIG_NAMES if n != "w_in"}, my_c.astype(jnp.int32).reshape(1))

    loss_cols, grad_x, d_mod, small_grads = local_step(x, loss_target, mod, big, small, plan)

    packed = _pack([d_mod, loss_cols] + [small_grads[n] for n in SMALL_NAMES])
    rows = packed.shape[0]
    everyone = allgather8(packed, name="gather_small").reshape(N_DEV, rows, PACK_WIDTH)
    d_mod_all = everyone[:, :bsz * 6].reshape(N_DEV * bsz, 6 * d)
    summed = sum_blocks(everyone, name="sum_small")
    shapes = [d_mod.shape, loss_cols.shape] + [small_grads[n].shape for n in SMALL_NAMES]
    parts = _unpack(summed, shapes)
    loss = jnp.sum(parts[1])
    grads = dict(zip(SMALL_NAMES, parts[2:]))
    for n in SHARDED_SMALL:
        cols = w[n].shape[1]
        grads[n] = lax.dynamic_slice(grads[n], (0, chip * cols), (grads[n].shape[0], cols))
    d_mod_cols = lax.dynamic_slice(d_mod_all, (0, chip * ncol), (N_DEV * bsz, ncol))
    grads["w_ada"], grads["b_ada"] = ada_bwd(c_all, d_mod_all, d_mod_cols, name="ada_bwd")

    grads.update(plan.reduced())

    delta, new_m, new_v = {}, {}, {}
    for n in BIG_NAMES + ["w_ada", "b_ada"]:
        delta[n], new_m[n], new_v[n] = adamw(w[n], grads[n], m[n], v[n], name="adamw_" + n)
    shapes = [w[n].shape for n in SMALL_NAMES]
    pk = lambda src: _pack([src[n] for n in SMALL_NAMES])
    upd = adamw(pk(w), pk(grads), pk(m), pk(v), name="adamw_small")
    for out, packed_out in zip((delta, new_m, new_v), upd):
        out.update(zip(SMALL_NAMES, _unpack(packed_out, shapes)))

    shaped = lambda src: [src[n].reshape(given[n].shape) for n in WEIGHT_NAMES]
    return (loss, grad_x, *shaped(grads), *shaped(delta), *shaped(new_m), *shaped(new_v))
```

```python
import functools
import math

import numpy as np
import jax
import jax.numpy as jnp
from jax import lax
from jax.experimental import pallas as pl
from jax.experimental.pallas import tpu as pltpu

F32 = jnp.float32
BF16 = jnp.bfloat16
HI = lax.Precision.HIGHEST
MESH = pl.DeviceIdType.MESH

D_MODEL = 1024
LRU_HEADS = 16
LRU_HEAD_DIM = 64
LRU_C = 8.0
SSD_INNER = 2048
SSD_HEADS = 32
SSD_HEAD_DIM = 64
SSD_GROUPS = 8
SSD_STATE = 128
SSD_CHUNK = 128
SSD_CONV_DIM = 4096
D_FF = 4096
EPS = 1e-6
N_CHIPS = 4
N_DEV = 8
LANES = 128
SUBLANES = 8

ADAM_LR = 0.001
ADAM_B1 = 0.9
ADAM_B2 = 0.999
ADAM_EPS = 1e-08
ADAM_WD = 0.01
ADAM_STEP = 10


ANY = pl.BlockSpec(memory_space=pl.ANY)


def _pcall(body, **kw):
    return pl.pallas_call(body, **kw)


class Background:
    def __init__(self, inputs, out_shapes, scratch, start, finish):
        self.inputs, self.out_shapes, self.scratch = list(inputs), list(out_shapes), list(scratch)
        self.start, self.finish = start, finish

    def wrap(self, body, kw):
        n_in, n_out = len(kw["in_specs"]), len(kw["out_specs"])
        n_scr = len(kw.get("scratch_shapes", []))
        b_in, b_out = len(self.inputs), len(self.out_shapes)
        grid = kw["grid"]

        def wrapped(*refs):
            ins, b_ins = refs[:n_in], refs[n_in:n_in + b_in]
            o0 = n_in + b_in
            outs, b_outs = refs[o0:o0 + n_out], refs[o0 + n_out:o0 + n_out + b_out]
            s0 = o0 + n_out + b_out
            scr, b_scr = refs[s0:s0 + n_scr], refs[s0 + n_scr:]
            ids = [pl.program_id(a) for a in range(len(grid))]
            first = functools.reduce(jnp.logical_and, [i == 0 for i in ids])
            last = functools.reduce(jnp.logical_and, [i == g - 1 for i, g in zip(ids, grid)])

            @pl.when(first)
            def _():
                self.start(b_ins, b_outs, b_scr)

            body(*ins, *outs, *scr)

            @pl.when(last)
            def _():
                self.finish(b_ins, b_outs, b_scr)

        kw = dict(kw, in_specs=list(kw["in_specs"]) + [ANY] * b_in, out_specs=list(kw["out_specs"]) + [ANY] * b_out,
                  out_shape=list(kw["out_shape"]) + self.out_shapes,
                  scratch_shapes=list(kw.get("scratch_shapes", [])) + self.scratch)
        return wrapped, kw


def _run(body, args, bg, **kw):
    n_out = len(kw["out_shape"])
    if bg is None:
        return list(_pcall(body, **kw)(*args)), []
    body, kw = bg.wrap(body, kw)
    outs = _pcall(body, **kw)(*args, *bg.inputs)
    return list(outs[:n_out]), list(outs[n_out:])


def _sigmoid(v):
    return 1.0 / (1.0 + jnp.exp(-v))


def _log1p(u):
    return jnp.where(u < 1e-3, u * (1.0 - u * (0.5 - u * (1.0 / 3.0))), jnp.log(1.0 + u))


def _softplus(v):
    return jnp.maximum(v, 0.0) + _log1p(jnp.exp(-jnp.abs(v)))


def _neg_expm1(v):
    small = -v * (1.0 + v * (0.5 + v * (1.0 / 6.0 + v * (1.0 / 24.0))))
    return jnp.where(v > -0.05, small, 1.0 - jnp.exp(v))


_GELU_K = math.sqrt(2.0 / math.pi)


def _gelu(v):
    t = jnp.tanh(_GELU_K * (v + 0.044715 * v * v * v))
    return 0.5 * v * (1.0 + t)


def _gelu_grad(v):
    t = jnp.tanh(_GELU_K * (v + 0.044715 * v * v * v))
    return 0.5 * (1.0 + t) + 0.5 * v * (1.0 - t * t) * _GELU_K * (1.0 + 3.0 * 0.044715 * v * v)


def _colsum(v):
    return jnp.sum(v, axis=0, keepdims=True)


def _dot(a, b, precision=None):
    return lax.dot_general(a, b, (((1,), (0,)), ((), ())), preferred_element_type=F32, precision=precision)


def _dot_nt(a, b):
    return lax.dot_general(a, b, (((1,), (1,)), ((), ())), preferred_element_type=F32)


def _dot_tn(a, b):
    return lax.dot_general(a, b, (((0,), (0,)), ((), ())), preferred_element_type=F32)


def _shift_down(xt, prev8, j):
    if j == 0:
        return xt
    n = xt.shape[0]
    r = pltpu.roll(xt, j, 0)
    p = pltpu.roll(prev8, j, 0)
    rows = lax.broadcasted_iota(jnp.int32, (SUBLANES, xt.shape[1]), 0)
    top = jnp.where(rows < j, p, r[0:SUBLANES])
    if n == SUBLANES:
        return top
    return jnp.concatenate([top, r[SUBLANES:]], axis=0)


def _shift_up(xt, next8, j):
    if j == 0:
        return xt
    n = xt.shape[0]
    r = pltpu.roll(xt, n - j, 0)
    p = pltpu.roll(next8, SUBLANES - j, 0)
    rows = lax.broadcasted_iota(jnp.int32, (SUBLANES, xt.shape[1]), 0)
    bot = jnp.where(rows >= SUBLANES - j, p, r[n - SUBLANES:])
    if n == SUBLANES:
        return bot
    return jnp.concatenate([r[:n - SUBLANES], bot], axis=0)


def _conv4(xt, prev8, w, b):
    out = b + w[3:4] * xt
    for k in range(3):
        out = out + w[k:k + 1] * _shift_down(xt, prev8, 3 - k)
    return out


def _conv4_bwd(d_out, next8, xt, w):
    d_x = w[3:4] * d_out
    d_w = []
    for k in range(3):
        up = _shift_up(d_out, next8, 3 - k)
        d_x = d_x + w[k:k + 1] * up
        d_w.append(_colsum(up * xt))
    d_w.append(_colsum(d_out * xt))
    return d_x, d_w, _colsum(d_out)


def _stack_rows(rows, width):
    rows = list(rows) + [jnp.zeros((1, width), F32)] * (SUBLANES - len(rows))
    return jnp.concatenate(rows, axis=0)


def _pick(n, cands):
    for c in cands:
        if n % c == 0:
            return c
    raise ValueError(f"no tile for {n}")


MM_ROWS = 1024
MM_VMEM_BUDGET = 36 << 20
MM_PANEL_COLS = 2048
MM_SUB = 512


def mm_nn(pairs, *, name, out_dtype=F32, a_fn=None, add=None, epi=None, extra=None, bg=None):
    np_ = len(pairs)
    m, n = pairs[0][0].shape[0], pairs[0][1].shape[1]
    pn = n if n <= MM_PANEL_COLS else _pick(n, (MM_PANEL_COLS, 1024, 512, 256, 128))
    ns = _pick(pn, (MM_SUB, 256, 128))
    adds = list(add or ())
    has_extra = extra is not None
    stage0 = a_fn is not None or pairs[0][0].dtype != BF16

    def vmem_bytes(rows):
        tiles = sum(rows * a.shape[1] * a.dtype.itemsize for a, _ in pairs)
        tiles += rows * pn * (4 * len(adds) + (extra.dtype.itemsize if has_extra else 0) + jnp.dtype(out_dtype).itemsize)
        panels = sum(b.shape[0] * pn * b.dtype.itemsize for _, b in pairs)
        return 2 * (tiles + panels) + (rows * pairs[0][0].shape[1] * 2 if stage0 else 0)

    tm = _pick(m, (MM_ROWS, 512, 256, 128, 64, 32, 16, 8))
    if vmem_bytes(tm) > MM_VMEM_BUDGET:
        tm = _pick(m, (512, 256, 128, 64, 32, 16, 8))

    def body(*refs):
        a_refs, b_refs = refs[:np_], refs[np_:2 * np_]
        pos = 2 * np_
        extra_ref = None
        add_refs = refs[pos:pos + len(adds)]
        pos += len(adds)
        if has_extra:
            extra_ref = refs[pos]
            pos += 1
        o_ref = refs[pos]
        lhs = list(a_refs)
        if stage0:
            av = a_refs[0][...]
            if a_fn is not None:
                av = a_fn(av)
            refs[pos + 1][...] = av.astype(BF16)
            lhs[0] = refs[pos + 1]
        for n0 in range(0, pn, ns):
            sl = slice(n0, n0 + ns)
            acc = None
            for a_ref, b_ref in zip(lhs, b_refs):
                part = _dot(a_ref[...].astype(BF16), b_ref[:, sl])
                acc = part if acc is None else acc + part
            for add_ref in add_refs:
                acc = acc + add_ref[:, sl]
            if epi is not None:
                acc = epi(acc, extra_ref[:, sl]) if has_extra else epi(acc)
            o_ref[:, sl] = acc.astype(out_dtype)

    in_specs = [pl.BlockSpec((tm, a.shape[1]), lambda j, i: (i, 0)) for a, _ in pairs]
    in_specs += [pl.BlockSpec((b.shape[0], pn), lambda j, i: (0, j)) for _, b in pairs]
    args = [a for a, _ in pairs] + [b for _, b in pairs]
    tile = pl.BlockSpec((tm, pn), lambda j, i: (i, j))
    for extra_add in adds:
        in_specs.append(tile)
        args.append(extra_add)
    if has_extra:
        in_specs.append(tile)
        args.append(extra)
    outs, bg_outs = _run(
        body, args, bg, name=name, grid=(n // pn, m // tm), in_specs=in_specs, out_specs=[tile],
        out_shape=[jax.ShapeDtypeStruct((m, n), out_dtype)],
        scratch_shapes=[pltpu.VMEM((tm, pairs[0][0].shape[1]), BF16)] if stage0 else [])
    return outs[0] if bg is None else (outs[0], bg_outs)


MM_REDUCE_ROWS = 1024
MM_GRAD_ROWS = 1024
MM_GRAD_COLS = 2048


def mm_tn(a, b, *, name, a_fn=None):
    m, ka = a.shape
    nb = b.shape[1]
    pa = _pick(ka, (MM_GRAD_ROWS, 512, 256, 128))
    pb = nb if nb <= MM_GRAD_COLS else _pick(nb, (MM_GRAD_COLS, 1024, 512, 256, 128))
    ns = _pick(pb, (MM_SUB, 256, 128))
    tmk = _pick(m, (MM_REDUCE_ROWS, 512, 256, 128, 64, 32, 16))

    def body(a_ref, b_ref, o_ref, lhs):
        k = pl.program_id(2)

        @pl.when(k == 0)
        def _():
            o_ref[...] = jnp.zeros_like(o_ref)

        av = a_ref[...]
        if a_fn is not None:
            av = a_fn(av)
        lhs[...] = av.astype(BF16)
        for n0 in range(0, pb, ns):
            o_ref[:, n0:n0 + ns] += _dot_tn(lhs[...], b_ref[:, n0:n0 + ns].astype(BF16))

    return _pcall(
        body, name=name,
        grid=(ka // pa, nb // pb, m // tmk),
        in_specs=[pl.BlockSpec((tmk, pa), lambda i, j, k: (k, i)),
                  pl.BlockSpec((tmk, pb), lambda i, j, k: (k, j))],
        out_specs=pl.BlockSpec((pa, pb), lambda i, j, k: (i, j)),
        out_shape=jax.ShapeDtypeStruct((ka, nb), F32),
        scratch_shapes=[pltpu.VMEM((tmk, pa), BF16)],
    )(a, b)


def _relu_sq(v):
    r = jnp.maximum(v, 0.0)
    return r * r


ROW_TILE = 512


def _row_specs(bsz, seq, width, ts):
    return pl.BlockSpec((None, ts, width), lambda b, i: (b, i, 0))


def _vec_spec(width):
    return pl.BlockSpec((1, width), lambda b, i: (0, 0))


def _mod_spec():
    return pl.BlockSpec((None, SUBLANES, D_MODEL), lambda b, i: (b, 0, 0))


def _rstd(v):
    return lax.rsqrt(jnp.mean(v * v, axis=-1, keepdims=True) + EPS)


def prenorm(x, w, mod, *, name):
    bsz, seq, d = x.shape
    ts = _pick(seq, (ROW_TILE, 256, 128))

    def body(x_ref, w_ref, mod_ref, h_ref):
        xv = x_ref[...]
        m = mod_ref[...]
        xh = xv * _rstd(xv)
        h_ref[...] = ((xh * w_ref[...]) * (1.0 + m[1:2]) + m[0:1]).astype(BF16)

    return _pcall(
        body, name=name, grid=(bsz, seq // ts),
        in_specs=[_row_specs(bsz, seq, d, ts), _vec_spec(d), _mod_spec()],
        out_specs=_row_specs(bsz, seq, d, ts),
        out_shape=jax.ShapeDtypeStruct((bsz, seq, d), BF16),
    )(x, w, mod)


def post1_pre2(x, out1, mod, post1, pre2, *, name):
    bsz, seq, d = x.shape
    ts = _pick(seq, (ROW_TILE, 256, 128))

    def body(x_ref, o_ref, mod_ref, p1_ref, p2_ref, x1_ref, h2_ref):
        m = mod_ref[...]
        ov = o_ref[...]
        x1 = x_ref[...] + m[2:3] * ((ov * _rstd(ov)) * p1_ref[...])
        x1_ref[...] = x1
        xh = x1 * _rstd(x1)
        h2_ref[...] = ((xh * p2_ref[...]) * (1.0 + m[4:5]) + m[3:4]).astype(BF16)

    return _pcall(
        body, name=name, grid=(bsz, seq // ts),
        in_specs=[_row_specs(bsz, seq, d, ts), _row_specs(bsz, seq, d, ts), _mod_spec(), _vec_spec(d), _vec_spec(d)],
        out_specs=[_row_specs(bsz, seq, d, ts), _row_specs(bsz, seq, d, ts)],
        out_shape=[jax.ShapeDtypeStruct((bsz, seq, d), F32), jax.ShapeDtypeStruct((bsz, seq, d), BF16)],
    )(x, out1, mod, post1, pre2)


def _acc_specs(d):
    per_batch = pl.BlockSpec((None, SUBLANES, d), lambda b, i: (b, 0, 0))
    glob = pl.BlockSpec((SUBLANES, d), lambda b, i: (0, 0))
    return per_batch, glob


def _accumulate(pb_ref, gl_ref, pb_rows, gl_rows, width):
    b, i = pl.program_id(0), pl.program_id(1)

    @pl.when(i == 0)
    def _():
        pb_ref[...] = jnp.zeros_like(pb_ref)

    @pl.when((b == 0) & (i == 0))
    def _():
        gl_ref[...] = jnp.zeros_like(gl_ref)

    pb_ref[...] += _stack_rows(pb_rows, width)
    gl_ref[...] += _stack_rows(gl_rows, width)


def _rms_bwd(d_n, n, r):
    return r * (d_n - n * jnp.mean(d_n * n, axis=-1, keepdims=True))


def final_bwd(x1, y2, target, mod, post2, *, name):
    bsz, seq, d = x1.shape
    ts = _pick(seq, (ROW_TILE, 256, 128))

    def body(x1_ref, y_ref, t_ref, mod_ref, p_ref, dx_ref, dy_ref, pb_ref, gl_ref):
        m = mod_ref[...]
        g2 = m[5:6]
        yv = y_ref[...]
        r = _rstd(yv)
        n = yv * r
        o = n * p_ref[...]
        diff = (x1_ref[...] + g2 * o) - t_ref[...]
        dx = diff * (1.0 / d)
        dx_ref[...] = dx
        d_o = dx * g2
        dy_ref[...] = _rms_bwd(d_o * p_ref[...], n, r).astype(BF16)
        _accumulate(pb_ref, gl_ref, [_colsum(dx * o)], [_colsum(d_o * n), _colsum(diff * diff) * (0.5 / d)], d)

    pb, gl = _acc_specs(d)
    rs = _row_specs(bsz, seq, d, ts)
    return _pcall(
        body, name=name, grid=(bsz, seq // ts),
        in_specs=[rs, rs, rs, _mod_spec(), _vec_spec(d)],
        out_specs=[rs, rs, pb, gl],
        out_shape=[jax.ShapeDtypeStruct((bsz, seq, d), F32), jax.ShapeDtypeStruct((bsz, seq, d), BF16),
                   jax.ShapeDtypeStruct((bsz, SUBLANES, d), F32), jax.ShapeDtypeStruct((SUBLANES, d), F32)],
    )(x1, y2, target, mod, post2)


def mid_bwd(d_h2, dx2, x1, out1, mod, pre2, post1, *, name):
    bsz, seq, d = x1.shape
    ts = _pick(seq, (ROW_TILE, 256, 128))

    def body(dh_ref, dx2_ref, x1_ref, o_ref, mod_ref, p2_ref, p1_ref, dx1_ref, do_ref, pb_ref, gl_ref):
        m = mod_ref[...]
        dh = dh_ref[...]
        x1 = x1_ref[...]
        r2 = _rstd(x1)
        xh = x1 * r2
        xw = xh * p2_ref[...]
        d_xw = dh * (1.0 + m[4:5])
        dx1 = dx2_ref[...] + _rms_bwd(d_xw * p2_ref[...], xh, r2)
        dx1_ref[...] = dx1
        ov = o_ref[...]
        r1 = _rstd(ov)
        n1 = ov * r1
        o1 = n1 * p1_ref[...]
        d_o1 = dx1 * m[2:3]
        do_ref[...] = _rms_bwd(d_o1 * p1_ref[...], n1, r1).astype(BF16)
        _accumulate(pb_ref, gl_ref, [_colsum(dh), _colsum(dh * xw), _colsum(dx1 * o1)],
                    [_colsum(d_xw * xh), _colsum(d_o1 * n1)], d)

    pb, gl = _acc_specs(d)
    rs = _row_specs(bsz, seq, d, ts)
    return _pcall(
        body, name=name, grid=(bsz, seq // ts),
        in_specs=[rs, rs, rs, rs, _mod_spec(), _vec_spec(d), _vec_spec(d)],
        out_specs=[rs, rs, pb, gl],
        out_shape=[jax.ShapeDtypeStruct((bsz, seq, d), F32), jax.ShapeDtypeStruct((bsz, seq, d), BF16),
                   jax.ShapeDtypeStruct((bsz, SUBLANES, d), F32), jax.ShapeDtypeStruct((SUBLANES, d), F32)],
    )(d_h2, dx2, x1, out1, mod, pre2, post1)


def first_bwd(d_h1, dx1, x, mod, pre1, *, name):
    bsz, seq, d = x.shape
    ts = _pick(seq, (ROW_TILE, 256, 128))

    def body(dh_ref, dx1_ref, x_ref, mod_ref, p_ref, gx_ref, pb_ref, gl_ref):
        m = mod_ref[...]
        dh = dh_ref[...]
        xv = x_ref[...]
        r = _rstd(xv)
        xh = xv * r
        xw = xh * p_ref[...]
        d_xw = dh * (1.0 + m[1:2])
        gx_ref[...] = dx1_ref[...] + _rms_bwd(d_xw * p_ref[...], xh, r)
        _accumulate(pb_ref, gl_ref, [_colsum(dh), _colsum(dh * xw)], [_colsum(d_xw * xh)], d)

    pb, gl = _acc_specs(d)
    rs = _row_specs(bsz, seq, d, ts)
    return _pcall(
        body, name=name, grid=(bsz, seq // ts),
        in_specs=[rs, rs, rs, _mod_spec(), _vec_spec(d)],
        out_specs=[rs, pb, gl],
        out_shape=[jax.ShapeDtypeStruct((bsz, seq, d), F32),
                   jax.ShapeDtypeStruct((bsz, SUBLANES, d), F32), jax.ShapeDtypeStruct((SUBLANES, d), F32)],
    )(d_h1, dx1, x, mod, pre1)


def merge_bwd(d_merged, ya, yb, gates, b_gate, *, name):
    bsz, seq, d = ya.shape
    ts = _pick(seq, (ROW_TILE, 256, 128))

    def body(dm_ref, ya_ref, yb_ref, g_ref, b_ref, dya_ref, dyb_ref, dg_ref, gl_ref):
        b, i = pl.program_id(0), pl.program_id(1)
        g = _sigmoid(g_ref[...] + b_ref[...])
        dm = dm_ref[...]
        ga, gb = g[:, :d], g[:, d:]
        dya_ref[...] = (dm * ga).astype(BF16)
        dyb_ref[...] = (dm * gb).astype(BF16)
        dg = jnp.concatenate([dm * ya_ref[...] * ga * (1.0 - ga), dm * yb_ref[...] * gb * (1.0 - gb)], axis=1)
        dg_ref[...] = dg.astype(BF16)

        @pl.when((b == 0) & (i == 0))
        def _():
            gl_ref[...] = jnp.zeros_like(gl_ref)

        gl_ref[...] += _stack_rows([_colsum(dg)], 2 * d)

    rs = _row_specs(bsz, seq, d, ts)
    rs2 = _row_specs(bsz, seq, 2 * d, ts)
    return _pcall(
        body, name=name, grid=(bsz, seq // ts),
        in_specs=[rs, rs, rs, rs2, _vec_spec(2 * d)],
        out_specs=[rs, rs, rs2, pl.BlockSpec((SUBLANES, 2 * d), lambda b, i: (0, 0))],
        out_shape=[jax.ShapeDtypeStruct((bsz, seq, d), BF16), jax.ShapeDtypeStruct((bsz, seq, d), BF16),
                   jax.ShapeDtypeStruct((bsz, seq, 2 * d), BF16), jax.ShapeDtypeStruct((SUBLANES, 2 * d), F32)],
    )(d_merged, ya, yb, gates, b_gate)


LRU_TILE = 256
N_LRU_BLOCKS = D_MODEL // LANES


def _block_mm(v, w_ref, transpose=False):
    vb = v.astype(BF16)
    outs = []
    for j in range(N_LRU_BLOCKS):
        blk = vb[:, LANES * j:LANES * (j + 1)]
        outs.append(_dot_nt(blk, w_ref[j]) if transpose else _dot(blk, w_ref[j]))
    return jnp.concatenate(outs, axis=1)


def _lru_gates(xc, wa_ref, ba, wx_ref, bx, sp):
    r = _sigmoid(_block_mm(xc, wa_ref) + ba)
    i = _sigmoid(_block_mm(xc, wx_ref) + bx)
    la = (-LRU_C * r) * sp
    a = jnp.exp(la)
    sq = jnp.sqrt(_neg_expm1(2.0 * la))
    return r, i, a, sq


def _group_roll(v, shift):
    rows, width = v.shape
    return pltpu.roll(v.reshape(rows // SUBLANES, SUBLANES, width), shift, 1).reshape(rows, width)


def _group_scan(a, b, reverse=False):
    row = lax.broadcasted_iota(jnp.int32, a.shape, 0) % SUBLANES
    for s in (1, 2, 4):
        take = (row < SUBLANES - s) if reverse else (row >= s)
        shift = SUBLANES - s if reverse else s
        b = jnp.where(take, a * _group_roll(b, shift) + b, b)
        a = jnp.where(take, a * _group_roll(a, shift), a)
    return a, b


def _prev8_spec(width, col_block, tile_rows):
    per = tile_rows // SUBLANES
    return pl.BlockSpec((None, SUBLANES, width), lambda b, i: (b, jnp.maximum(i * per - 1, 0), col_block))


def lru_fwd(pm, cw, cb, wa, ba, wx, bx, lam, w_pa, *, name):
    bsz, seq, _ = pm.shape
    d = D_MODEL
    ts = _pick(seq, (LRU_TILE, 128))

    def body(lx_ref, lxp_ref, lg_ref, cw_ref, cb_ref, wa_ref, ba_ref, wx_ref, bx_ref, lam_ref, wpa_ref,
             h_ref, pa_ref, ya_ref, kept_ref, hc, a_s, u_s):
        i = pl.program_id(1)

        @pl.when(i == 0)
        def _():
            hc[...] = jnp.zeros_like(hc)

        lx = lx_ref[...]
        prev8 = jnp.where(i == 0, 0.0, lxp_ref[...])
        xc = _conv4(lx, prev8, cw_ref[...], cb_ref[...])
        sp = _softplus(-lam_ref[...])
        r, ig, a, sq = _lru_gates(xc, wa_ref, ba_ref[...], wx_ref, bx_ref[...], sp)
        for k, kept in enumerate((xc, r, ig, a, sq)):
            kept_ref[:, k * d:(k + 1) * d] = kept
        a_s[...], u_s[...] = _group_scan(a, sq * (ig * xc))

        def step(g, h):
            r0 = pl.multiple_of(g * SUBLANES, SUBLANES)
            h8 = a_s[pl.ds(r0, SUBLANES), :] * h + u_s[pl.ds(r0, SUBLANES), :]
            h_ref[pl.ds(r0, SUBLANES), :] = h8
            return h8[SUBLANES - 1:SUBLANES]

        hc[...] = lax.fori_loop(0, ts // SUBLANES, step, hc[...])
        pa_ref[...] = (h_ref[...] * _gelu(lg_ref[...])).astype(BF16)
        ya_ref[...] = _dot(pa_ref[...], wpa_ref[...])

    vec = _vec_spec(d)
    wspec = pl.BlockSpec((N_LRU_BLOCKS, LANES, LANES), lambda b, i: (0, 0, 0))
    rs = _row_specs(bsz, seq, d, ts)
    return _pcall(
        body, name=name, grid=(bsz, seq // ts),
        in_specs=[pl.BlockSpec((None, ts, d), lambda b, i: (b, i, 0)), _prev8_spec(d, 0, ts),
                  pl.BlockSpec((None, ts, d), lambda b, i: (b, i, 1)),
                  pl.BlockSpec((4, d), lambda b, i: (0, 0)), vec, wspec, vec, wspec, vec, vec,
                  pl.BlockSpec(w_pa.shape, lambda b, i: (0, 0))],
        out_specs=[rs, rs, rs, _row_specs(bsz, seq, 5 * d, ts)],
        out_shape=[jax.ShapeDtypeStruct((bsz, seq, d), F32), jax.ShapeDtypeStruct((bsz, seq, d), BF16),
                   jax.ShapeDtypeStruct((bsz, seq, d), F32), jax.ShapeDtypeStruct((bsz, seq, 5 * d), F32)],
        scratch_shapes=[pltpu.VMEM((1, d), F32), pltpu.VMEM((ts, d), F32), pltpu.VMEM((ts, d), F32)],
    )(pm, pm, pm, cw, cb, wa, ba, wx, bx, lam, w_pa)


def lru_bwd(pm, h, kept, d_ya, cw, wa, wx, lam, wt_pa, wt_lru, *, name, bg=None):
    bsz, seq, _ = pm.shape
    d = D_MODEL
    ts = _pick(seq, (LRU_TILE, 128))
    nt = seq // ts
    per = ts // SUBLANES

    def rev(i):
        return nt - 1 - i

    def body(lx_ref, lg_ref, h_ref, hp_ref, kept_ref, dya_ref, cw_ref, wa_ref, wx_ref,
             lam_ref, wtpa_ref, wtl_ref, dl_ref, dh1_ref, dwa_ref, dwx_ref, rows_ref,
             carry, dxc_next, a_s, dh_s, acc_s, a0_s):
        b, i = pl.program_id(0), pl.program_id(1)
        t = rev(i)

        @pl.when(i == 0)
        def _():
            carry[...] = jnp.zeros_like(carry)
            dxc_next[...] = jnp.zeros_like(dxc_next)

        @pl.when((b == 0) & (i == 0))
        def _():
            dwa_ref[...] = jnp.zeros_like(dwa_ref)
            dwx_ref[...] = jnp.zeros_like(dwx_ref)
            rows_ref[...] = jnp.zeros_like(rows_ref)

        lx = lx_ref[...]
        lg = lg_ref[...]
        cwv = cw_ref[...]
        lam_v = lam_ref[...]
        sp = _softplus(-lam_v)
        xc, r, ig, a, sq = (kept_ref[:, k * d:(k + 1) * d] for k in range(5))
        hv = h_ref[...]
        d_pa = _dot(dya_ref[...], wtpa_ref[...])
        row = lax.broadcasted_iota(jnp.int32, a.shape, 0) % SUBLANES
        a_next = jnp.where(row < SUBLANES - 1, _group_roll(a, SUBLANES - 1), 1.0)
        a_s[...], dh_s[...] = _group_scan(a_next, d_pa * _gelu(lg), reverse=True)
        a0_s[...] = a

        def step(g, c):
            r0 = pl.multiple_of((per - 1 - g) * SUBLANES, SUBLANES)
            acc8 = a_s[pl.ds(r0, SUBLANES), :] * c + dh_s[pl.ds(r0, SUBLANES), :]
            acc_s[pl.ds(r0, SUBLANES), :] = acc8
            return a0_s[pl.ds(r0, SUBLANES), :][0:1] * acc8[0:1]

        carry[...] = lax.fori_loop(0, per, step, carry[...])
        d_u = acc_s[...]
        hprev8 = jnp.where(t == 0, 0.0, hp_ref[...])
        d_a = d_u * _shift_down(hv, hprev8, 1)
        d_sq = d_u * (ig * xc)
        d_i = d_u * (sq * xc)
        d_xc = d_u * (sq * ig)
        d_la = d_a * a - d_sq * (a * a) / sq
        d_pre_r = (d_la * (-LRU_C * sp)) * (r * (1.0 - r))
        d_pre_i = d_i * (ig * (1.0 - ig))
        d_xc = d_xc + _block_mm(d_pre_r, wa_ref, transpose=True) + _block_mm(d_pre_i, wx_ref, transpose=True)
        xcb = xc.astype(BF16)
        drb = d_pre_r.astype(BF16)
        dib = d_pre_i.astype(BF16)
        for j in range(N_LRU_BLOCKS):
            sl = slice(LANES * j, LANES * (j + 1))
            dwa_ref[j] += _dot_tn(xcb[:, sl], drb[:, sl])
            dwx_ref[j] += _dot_tn(xcb[:, sl], dib[:, sl])
        d_lx, d_cw, d_cb = _conv4_bwd(d_xc, dxc_next[...], lx, cwv)
        dxc_next[...] = d_xc[0:SUBLANES]
        d_lam = _colsum(d_la * (-LRU_C * r)) * (-_sigmoid(-lam_v))
        rows_ref[...] += _stack_rows([_colsum(d_pre_r), _colsum(d_pre_i), d_lam, d_cb] + d_cw, d)
        dl_ref[:, :d] = d_lx.astype(BF16)
        dl_ref[:, d:] = (d_pa * hv * _gelu_grad(lg)).astype(BF16)
        dh1_ref[...] = _dot(dl_ref[...], wtl_ref[...])

    vec = _vec_spec(d)
    wspec = pl.BlockSpec((N_LRU_BLOCKS, LANES, LANES), lambda b, i: (0, 0, 0))
    tile = lambda col: pl.BlockSpec((None, ts, d), lambda b, i: (b, rev(i), col))
    prev8 = lambda col: pl.BlockSpec((None, SUBLANES, d), lambda b, i: (b, jnp.maximum(rev(i) * per - 1, 0), col))
    whole = lambda v: pl.BlockSpec(v.shape, lambda b, i: (0, 0))
    return _run(
        body, (pm, pm, h, h, kept, d_ya, cw, wa, wx, lam, wt_pa, wt_lru), bg, name=name, grid=(bsz, nt),
        in_specs=[tile(0), tile(1), tile(0), prev8(0), pl.BlockSpec((None, ts, 5 * d), lambda b, i: (b, rev(i), 0)),
                  tile(0), pl.BlockSpec((4, d), lambda b, i: (0, 0)), wspec, wspec, vec,
                  whole(wt_pa), whole(wt_lru)],
        out_specs=[pl.BlockSpec((None, ts, 2 * d), lambda b, i: (b, rev(i), 0)), tile(0), wspec, wspec,
                   pl.BlockSpec((SUBLANES, d), lambda b, i: (0, 0))],
        out_shape=[jax.ShapeDtypeStruct((bsz, seq, 2 * d), BF16), jax.ShapeDtypeStruct((bsz, seq, d), F32),
                   jax.ShapeDtypeStruct((N_LRU_BLOCKS, LANES, LANES), F32),
                   jax.ShapeDtypeStruct((N_LRU_BLOCKS, LANES, LANES), F32),
                   jax.ShapeDtypeStruct((SUBLANES, d), F32)],
        scratch_shapes=[pltpu.VMEM((1, d), F32), pltpu.VMEM((SUBLANES, d), F32),
                        pltpu.VMEM((ts, d), F32), pltpu.VMEM((ts, d), F32), pltpu.VMEM((ts, d), F32),
                        pltpu.VMEM((ts, d), F32)])


L = SSD_CHUNK
N_PAIRS = SSD_HEADS // 2


def _ssd_common(conv, dt_raw, dtb, alog):
    sg = _sigmoid(conv)
    xa = conv * sg
    dtv = _softplus(dt_raw + dtb)
    a_neg = -jnp.exp(alog)
    rowi = lax.broadcasted_iota(jnp.int32, (L, L), 0)
    coli = lax.broadcasted_iota(jnp.int32, (L, L), 1)
    tril = (rowi >= coli).astype(F32)
    cs = _dot(tril, dtv * a_neg, precision=HI)
    return conv, sg, xa, dtv, a_neg, cs, rowi, coli


def _head_masks():
    lane = lax.broadcasted_iota(jnp.int32, (L, LANES), 1)
    return lane < SSD_HEAD_DIM


def _spread(v, p, first):
    return jnp.where(first[:v.shape[0]], v[:, 2 * p:2 * p + 1], v[:, 2 * p + 1:2 * p + 2])


def _place_head_sums(acc, z, p, first, lane1):
    rows = z.shape[0]
    s0 = jnp.sum(jnp.where(first[:rows], z, 0.0), axis=1, keepdims=True)
    s1 = jnp.sum(jnp.where(first[:rows], 0.0, z), axis=1, keepdims=True)
    lane = lane1[:rows]
    return jnp.where(lane == 2 * p, s0, jnp.where(lane == 2 * p + 1, s1, acc))


def _stack_heads(v, first):
    return jnp.concatenate([jnp.where(first, v, 0.0), jnp.where(first, 0.0, v)], axis=0).astype(BF16)


def ssd_fwd(pm, dtr, cw, cb, dtb, alog, d_lanes, nw, w_pb, ya, gates, b_gate, w_out, *, name, bg=None):
    bsz, seq, _ = pm.shape
    nc = seq // L
    inner, cdim, d = SSD_INNER, SSD_CONV_DIM, D_MODEL

    def body(xbc_ref, xp_ref, z_ref, dt_ref, cw_ref, cb_ref, dtb_ref, alog_ref, dl_ref, nw_ref, wpb_ref,
             ya_ref, g_ref, bg_ref, wout_ref,
             y_ref, yn_ref, st_ref, yb_ref, conv_ref, mg_ref, out_ref, state):
        i = pl.program_id(1)

        @pl.when(i == 0)
        def _():
            state[...] = jnp.zeros_like(state)

        prev8 = jnp.where(i == 0, 0.0, xp_ref[...])
        conv = _conv4(xbc_ref[...], prev8, cw_ref[...], cb_ref[...])
        conv_ref[...] = conv
        _, _, xa, dtv, _, cs, rowi, coli = _ssd_common(conv, dt_ref[...], dtb_ref[...], alog_ref[...])
        cst = cs.T
        causal = rowi >= coli
        first = _head_masks()
        for g in range(SSD_GROUPS):
            bg = xa[:, inner + SSD_STATE * g:inner + SSD_STATE * (g + 1)].astype(BF16)
            cg = xa[:, inner + SSD_GROUPS * SSD_STATE + SSD_STATE * g:
                    inner + SSD_GROUPS * SSD_STATE + SSD_STATE * (g + 1)].astype(BF16)
            cbm = _dot_nt(cg, bg)
            for pp in range(2):
                p = 2 * g + pp
                sl = slice(LANES * p, LANES * (p + 1))
                ms = []
                for hh in (2 * p, 2 * p + 1):
                    seg = cs[:, hh:hh + 1] - cst[hh:hh + 1, :]
                    ms.append((cbm * jnp.exp(jnp.where(causal, seg, -jnp.inf))).astype(BF16))
                xsp = xa[:, sl]
                cs_p = _spread(cs, p, first)
                cs_last = cs_p[L - 1:L]
                xp = xsp * _spread(dtv, p, first)
                y_diag = _dot(jnp.concatenate(ms, axis=1), _stack_heads(xp, first))
                st = state[p]
                st_ref[p] = st
                y_off = _dot(cg, st.astype(BF16)) * jnp.exp(cs_p)
                y_ref[:, sl] = y_diag + y_off + dl_ref[:, sl] * xsp
                state[p] = st * jnp.exp(cs_last) + _dot_tn(bg, (xp * jnp.exp(cs_last - cs_p)).astype(BF16))
        zv = z_ref[...]
        yz = y_ref[...] * (zv * _sigmoid(zv))
        gw = inner // SSD_GROUPS
        for g in range(SSD_GROUPS):
            sl = slice(gw * g, gw * (g + 1))
            seg = yz[:, sl]
            yn_ref[:, sl] = ((seg * _rstd(seg)) * nw_ref[:, sl]).astype(BF16)
        yb = _dot(yn_ref[...], wpb_ref[...])
        yb_ref[...] = yb
        g = _sigmoid(g_ref[...] + bg_ref[...])
        mg_ref[...] = (g[:, :d] * ya_ref[...] + g[:, d:] * yb).astype(BF16)
        out_ref[...] = _dot(mg_ref[...], wout_ref[...])

    cvec = lambda w: pl.BlockSpec((1, w), lambda b, i: (0, 0))
    rows = lambda w: pl.BlockSpec((None, L, w), lambda b, i: (b, i, 0))
    outs, bg_outs = _run(
        body, (pm, pm, pm, dtr, cw, cb, dtb, alog, d_lanes, nw, w_pb, ya, gates, b_gate, w_out), bg, name=name,
        grid=(bsz, nc),
        in_specs=[pl.BlockSpec((None, L, cdim), lambda b, i: (b, i, 1)), _prev8_spec(cdim, 1, L),
                  pl.BlockSpec((None, L, inner), lambda b, i: (b, i, 1)),
                  pl.BlockSpec((None, L, LANES), lambda b, i: (b, i, 0)),
                  pl.BlockSpec((4, cdim), lambda b, i: (0, 0)), cvec(cdim), cvec(LANES), cvec(LANES),
                  cvec(inner), cvec(inner), pl.BlockSpec(w_pb.shape, lambda b, i: (0, 0)),
                  rows(d), rows(2 * d), cvec(2 * d), pl.BlockSpec(w_out.shape, lambda b, i: (0, 0))],
        out_specs=[rows(inner), rows(inner),
                   pl.BlockSpec((None, None, N_PAIRS, SSD_STATE, LANES), lambda b, i: (b, i, 0, 0, 0)),
                   rows(d), rows(cdim), rows(d), rows(d)],
        out_shape=[jax.ShapeDtypeStruct((bsz, seq, inner), F32), jax.ShapeDtypeStruct((bsz, seq, inner), BF16),
                   jax.ShapeDtypeStruct((bsz, nc, N_PAIRS, SSD_STATE, LANES), F32),
                   jax.ShapeDtypeStruct((bsz, seq, d), F32), jax.ShapeDtypeStruct((bsz, seq, cdim), F32),
                   jax.ShapeDtypeStruct((bsz, seq, d), BF16), jax.ShapeDtypeStruct((bsz, seq, d), F32)],
        scratch_shapes=[pltpu.VMEM((N_PAIRS, SSD_STATE, LANES), F32)])
    return outs, bg_outs


def ssd_bwd(pm, conv, dtr, y, states, d_yb, cw, dtb, alog, d_lanes, nw, wt_pb, wt_ssd, *, name, bg=None):
    bsz, seq, _ = pm.shape
    nc = seq // L
    inner, cdim = SSD_INNER, SSD_CONV_DIM
    per = L // SUBLANES

    def rev(i):
        return nc - 1 - i

    def body(xbc_ref, conv_ref, z_ref, dt_ref, y_ref, st_ref, dyb_ref, cw_ref, dtb_ref, alog_ref,
             dl_ref, nw_ref, wtpb_ref, wts_ref, ds_ref, dh1_ref, ddt_ref, r4_ref, r2_ref, r1_ref,
             dstate, dconv_next, dxs_s, dbc_s):
        b, i = pl.program_id(0), pl.program_id(1)
        t = rev(i)

        @pl.when(i == 0)
        def _():
            dstate[...] = jnp.zeros_like(dstate)
            dconv_next[...] = jnp.zeros_like(dconv_next)

        @pl.when((b == 0) & (i == 0))
        def _():
            r4_ref[...] = jnp.zeros_like(r4_ref)
            r2_ref[...] = jnp.zeros_like(r2_ref)
            r1_ref[...] = jnp.zeros_like(r1_ref)

        xbc = xbc_ref[...]
        cwv = cw_ref[...]
        dt_in = dt_ref[...] + dtb_ref[...]
        conv = conv_ref[...]
        _, sg, xa, dtv, a_neg, cs, rowi, coli = _ssd_common(conv, dt_ref[...], dtb_ref[...], alog_ref[...])
        cst = cs.T
        causal = rowi >= coli
        anti = coli >= rowi
        first = _head_masks()
        lane1 = lax.broadcasted_iota(jnp.int32, (L, LANES), 1)

        yv = y_ref[...]
        zv = z_ref[...]
        sz = _sigmoid(zv)
        zs = zv * sz
        yz = yv * zs
        dyn = _dot(dyb_ref[...], wtpb_ref[...])
        gw = inner // SSD_GROUPS
        d_yz_parts, d_nw_parts = [], []
        for g in range(SSD_GROUPS):
            sl = slice(gw * g, gw * (g + 1))
            seg = yz[:, sl]
            r = _rstd(seg)
            n = seg * r
            d_nw_parts.append(_colsum(dyn[:, sl] * n))
            d_yz_parts.append(_rms_bwd(dyn[:, sl] * nw_ref[:, sl], n, r))
        d_yz = jnp.concatenate(d_yz_parts, axis=1)
        d_y = d_yz * zs
        ds_ref[:, :inner] = (d_yz * yv * (sz * (1.0 + zv * (1.0 - sz)))).astype(BF16)

        a1 = jnp.zeros((L, LANES), F32)
        a1_off = jnp.zeros((L, LANES), F32)
        a2 = jnp.zeros((L, LANES), F32)
        xs_dxt = jnp.zeros((L, LANES), F32)
        c0 = jnp.zeros((1, LANES), F32)
        d_dl = jnp.zeros((1, LANES), F32)
        for g in range(SSD_GROUPS):
            bsl = slice(inner + SSD_STATE * g, inner + SSD_STATE * (g + 1))
            csl = slice(inner + SSD_GROUPS * SSD_STATE + SSD_STATE * g,
                        inner + SSD_GROUPS * SSD_STATE + SSD_STATE * (g + 1))
            bg = xa[:, bsl].astype(BF16)
            cg = xa[:, csl].astype(BF16)
            cbm = _dot_nt(cg, bg)
            cbt = _dot_nt(bg, cg)
            d_cb = jnp.zeros((L, L), F32)
            d_bg = jnp.zeros((L, SSD_STATE), F32)
            d_cg = jnp.zeros((L, SSD_STATE), F32)
            for pp in range(2):
                p = 2 * g + pp
                sl = slice(LANES * p, LANES * (p + 1))
                xsp = xa[:, sl]
                dt_p = _spread(dtv, p, first)
                cs_p = _spread(cs, p, first)
                cs_last = cs_p[L - 1:L]
                e_p = jnp.exp(cs_p)
                w_p = jnp.exp(cs_last - cs_p)
                e_last = jnp.exp(cs_last)
                xp = xsp * dt_p
                xpb = xp.astype(BF16)
                dyp = d_y[:, sl]
                dypb = dyp.astype(BF16)
                dy_heads = (jnp.where(first, dyp, 0.0).astype(BF16), jnp.where(first, 0.0, dyp).astype(BF16))
                x_heads = (jnp.where(first, xp, 0.0).astype(BF16), jnp.where(first, 0.0, xp).astype(BF16))
                mts = []
                for k, hh in enumerate((2 * p, 2 * p + 1)):
                    col = cs[:, hh:hh + 1]
                    row = cst[hh:hh + 1, :]
                    dec = jnp.exp(jnp.where(causal, col - row, -jnp.inf))
                    dec_t = jnp.exp(jnp.where(anti, row - col, -jnp.inf))
                    gd = _dot_nt(dy_heads[k], xpb) * dec
                    d_cb = d_cb + gd
                    mt = cbt * dec_t
                    qd = gd * cbm - _dot_nt(x_heads[k], dypb) * mt
                    a1 = a1 + jnp.where(lane1 == hh, jnp.sum(qd, axis=1, keepdims=True), 0.0)
                    mts.append(mt.astype(BF16))
                dst = dstate[p]
                dstb = dst.astype(BF16)
                st = st_ref[p]
                stb = st.astype(BF16)
                dye = (dyp * e_p).astype(BF16)
                xw = (xp * w_p).astype(BF16)
                dx_off = w_p * _dot(bg, dstb)
                d_xp = _dot(jnp.concatenate(mts, axis=1), jnp.concatenate(dy_heads, axis=0)) + dx_off
                dxs_s[:, sl] = d_xp * dt_p + dyp * dl_ref[:, sl]
                a1_off = _place_head_sums(a1_off, dyp * (_dot(cg, stb) * e_p), p, first, lane1)
                a2 = _place_head_sums(a2, xp * dx_off, p, first, lane1)
                xs_dxt = _place_head_sums(xs_dxt, d_xp * xsp, p, first, lane1)
                c0 = _place_head_sums(c0, _colsum(dst * st) * e_last, p, first, lane1)
                d_dl = _place_head_sums(d_dl, _colsum(dyp * xsp), p, first, lane1)
                d_cg = d_cg + _dot_nt(dye, stb)
                d_bg = d_bg + _dot_nt(xw, dstb)
                dstate[p] = dst * e_last + _dot_tn(cg, dye)
            d_cbb = d_cb.astype(BF16)
            dbc_s[:, SSD_STATE * g:SSD_STATE * (g + 1)] = d_bg + _dot_tn(d_cbb, cg)
            dbc_s[:, SSD_GROUPS * SSD_STATE + SSD_STATE * g:SSD_GROUPS * SSD_STATE + SSD_STATE * (g + 1)] = (
                d_cg + _dot(d_cbb, bg))

        d_da = (_dot(anti.astype(F32), a1 + a1_off, precision=HI)
                + _dot((rowi > coli).astype(F32), a2, precision=HI) + c0)
        d_dt = d_da * a_neg + xs_dxt
        d_alog = _colsum(d_da * dtv) * a_neg
        d_dtr = jnp.where(lane1 < SSD_HEADS, d_dt * _sigmoid(dt_in), 0.0)
        ddt_ref[...] = d_dtr.astype(BF16)
        d_xa = jnp.concatenate([dxs_s[...], dbc_s[...]], axis=1)
        d_conv = d_xa * (sg * (1.0 + conv * (1.0 - sg)))
        d_xbc, d_cw, d_cbias = _conv4_bwd(d_conv, dconv_next[...], xbc, cwv)
        dconv_next[...] = d_conv[0:SUBLANES]
        ds_ref[:, inner:] = d_xbc.astype(BF16)
        dh1_ref[...] = _dot(ds_ref[...], wts_ref[...])
        r4_ref[...] += _stack_rows([d_cbias] + d_cw, cdim)
        r2_ref[...] += _stack_rows([jnp.concatenate(d_nw_parts, axis=1)], inner)
        r1_ref[...] += _stack_rows([_colsum(d_dtr), d_alog, d_dl], LANES)

    cvec = lambda w: pl.BlockSpec((1, w), lambda b, i: (0, 0))
    return _run(
        body, (pm, conv, pm, dtr, y, states, d_yb, cw, dtb, alog, d_lanes, nw, wt_pb, wt_ssd), bg, name=name,
        grid=(bsz, nc),
        in_specs=[pl.BlockSpec((None, L, cdim), lambda b, i: (b, rev(i), 1)),
                  pl.BlockSpec((None, L, cdim), lambda b, i: (b, rev(i), 0)),
                  pl.BlockSpec((None, L, inner), lambda b, i: (b, rev(i), 1)),
                  pl.BlockSpec((None, L, LANES), lambda b, i: (b, rev(i), 0)),
                  pl.BlockSpec((None, L, inner), lambda b, i: (b, rev(i), 0)),
                  pl.BlockSpec((None, None, N_PAIRS, SSD_STATE, LANES), lambda b, i: (b, rev(i), 0, 0, 0)),
                  pl.BlockSpec((None, L, D_MODEL), lambda b, i: (b, rev(i), 0)),
                  pl.BlockSpec((4, cdim), lambda b, i: (0, 0)), cvec(LANES), cvec(LANES),
                  cvec(inner), cvec(inner), pl.BlockSpec(wt_pb.shape, lambda b, i: (0, 0)),
                  pl.BlockSpec(wt_ssd.shape, lambda b, i: (0, 0))],
        out_specs=[pl.BlockSpec((None, L, inner + cdim), lambda b, i: (b, rev(i), 0)),
                   pl.BlockSpec((None, L, D_MODEL), lambda b, i: (b, rev(i), 0)),
                   pl.BlockSpec((None, L, LANES), lambda b, i: (b, rev(i), 0)),
                   pl.BlockSpec((SUBLANES, cdim), lambda b, i: (0, 0)),
                   pl.BlockSpec((SUBLANES, inner), lambda b, i: (0, 0)),
                   pl.BlockSpec((SUBLANES, LANES), lambda b, i: (0, 0))],
        out_shape=[jax.ShapeDtypeStruct((bsz, seq, inner + cdim), BF16),
                   jax.ShapeDtypeStruct((bsz, seq, D_MODEL), F32),
                   jax.ShapeDtypeStruct((bsz, seq, LANES), BF16),
                   jax.ShapeDtypeStruct((SUBLANES, cdim), F32),
                   jax.ShapeDtypeStruct((SUBLANES, inner), F32),
                   jax.ShapeDtypeStruct((SUBLANES, LANES), F32)],
        scratch_shapes=[pltpu.VMEM((N_PAIRS, SSD_STATE, LANES), F32), pltpu.VMEM((SUBLANES, cdim), F32),
                        pltpu.VMEM((L, inner), F32), pltpu.VMEM((L, 2 * SSD_GROUPS * SSD_STATE), F32)])


def _lru_block_weights(w):
    w = w.reshape(N_LRU_BLOCKS, 2, LRU_HEAD_DIM, LRU_HEAD_DIM)
    z = jnp.zeros((N_LRU_BLOCKS, LRU_HEAD_DIM, LRU_HEAD_DIM), w.dtype)
    top = jnp.concatenate([w[:, 0], z], axis=2)
    bot = jnp.concatenate([z, w[:, 1]], axis=2)
    return jnp.concatenate([top, bot], axis=1).astype(BF16)


def _lru_block_grads(g):
    h = LRU_HEAD_DIM
    return jnp.stack([g[:, :h, :h], g[:, h:, h:]], axis=1).reshape(LRU_HEADS, h, h)


def _pad_lanes(v, width=LANES):
    return jnp.pad(v, ((0, 0), (0, width - v.shape[1])))


class NoExchange:
    def __init__(self, weights):
        self._weights, self.grads = weights, {}

    def weights_bg(self):
        return None

    def weights(self, bg_outs):
        return self._weights

    def grads_bg(self, grads):
        self.grads.update(grads)
        return None

    def grads_done(self, bg_outs):
        pass


def local_step(x, target, mod, big, small, plan):
    bsz, seq, d = x.shape
    t = bsz * seq
    flat = lambda v: v.reshape(t, v.shape[-1])
    unflat = lambda v: v.reshape(bsz, seq, v.shape[-1])

    wa_b = _lru_block_weights(small["lru_wa"])
    wx_b = _lru_block_weights(small["lru_wx"])
    dtb = _pad_lanes(small["ssd_dt_bias"])
    alog = _pad_lanes(small["ssd_a_log"])
    d_lanes = jnp.repeat(small["ssd_d"], SSD_HEAD_DIM, axis=1)

    lru_cols = 2 * D_MODEL
    wt = {"lru": big["w_main"][:, :lru_cols].T, "ssd": big["w_main"][:, lru_cols:].T, "gates": big["w_gates"].T,
          "dt": big["w_dt"].T}

    h1 = prenorm(x, small["pre_norm1"], mod, name="prenorm1")
    h1f = flat(h1)
    arriving = plan.weights_bg()
    if arriving is None:
        pm, arrived = mm_nn([(h1f, big["w_main"])], name="in_proj_main"), []
    else:
        pm, arrived = mm_nn([(h1f, big["w_main"])], name="in_proj_main", bg=arriving)
    pm = unflat(pm)
    big = dict(big, **plan.weights(arrived))
    for n in ("w_pa", "w_pb", "w_out", "w_ff1", "w_ff2"):
        wt[n] = big[n].T
    gates = unflat(mm_nn([(h1f, big["w_gates"])], name="in_proj_gates"))
    dtr = unflat(mm_nn([(h1f, big["w_dt"])], name="in_proj_dt"))
    lru_args = (small["lru_conv_w"], small["lru_conv_b"], wa_b, small["lru_ba"], wx_b, small["lru_bx"],
                small["lru_lambda"])
    h_lru, pa_in, ya, lru_kept = lru_fwd(pm, *lru_args, big["w_pa"], name="lru_fwd")
    ssd_args = (small["ssd_conv_w"], small["ssd_conv_b"], dtb, alog, d_lanes, small["ssd_norm_w"])
    (y_ssd, ynorm, states, yb, conv_ssd, merged, out1), _ = ssd_fwd(
        pm, dtr, *ssd_args, big["w_pb"], ya, gates, small["b_gate"], big["w_out"], name="ssd_fwd")
    x1, h2 = post1_pre2(x, out1, mod, small["post_norm1"], small["pre_norm2"], name="post1_pre2")
    f = mm_nn([(flat(h2), big["w_ff1"])], name="ff1")
    y2 = unflat(mm_nn([(f, big["w_ff2"])], a_fn=_relu_sq, name="ff2"))

    dx2, d_y2, pb_a, gl_a = final_bwd(x1, y2, target, mod, small["post_norm2"], name="final_bwd")
    d_y2f = flat(d_y2)
    d_f = mm_nn([(d_y2f, wt["w_ff2"])], out_dtype=BF16, extra=f,
                epi=lambda r, fv: r * (2.0 * jnp.maximum(fv, 0.0)), name="ff2_dx")
    g_ff2 = mm_tn(f, d_y2f, a_fn=_relu_sq, name="ff2_dw")
    d_h2 = unflat(mm_nn([(d_f, wt["w_ff1"])], name="ff1_dx"))
    g_ff1 = mm_tn(flat(h2), d_f, name="ff1_dw")
    dx1, d_out1, pb_b, gl_b = mid_bwd(d_h2, dx2, x1, out1, mod, small["pre_norm2"], small["post_norm1"],
                                      name="mid_bwd")
    d_out1f = flat(d_out1)
    d_merged = unflat(mm_nn([(d_out1f, wt["w_out"])], name="out_dx"))
    g_out = mm_tn(flat(merged), d_out1f, name="out_dw")
    d_ya, d_yb, d_gates, gl_c = merge_bwd(d_merged, ya, yb, gates, small["b_gate"], name="merge_bwd")
    g_pa = mm_tn(flat(pa_in), flat(d_ya), name="pa_dw")
    g_pb = mm_tn(flat(ynorm), flat(d_yb), name="pb_dw")
    leaving = plan.grads_bg({"w_pa": g_pa, "w_pb": g_pb, "w_out": g_out, "w_ff1": g_ff1, "w_ff2": g_ff2})
    (d_l, dh_lru, g_wa_b, g_wx_b, lru_rows), _ = lru_bwd(
        pm, h_lru, lru_kept, d_ya, small["lru_conv_w"], wa_b, wx_b, small["lru_lambda"], wt["w_pa"], wt["lru"],
        name="lru_bwd")
    (d_s, dh_ssd, d_dt, r4, r2, r1), landed = ssd_bwd(
        pm, conv_ssd, dtr, y_ssd, states, d_yb, small["ssd_conv_w"], dtb, alog, d_lanes, small["ssd_norm_w"],
        wt["w_pb"], wt["ssd"], name="ssd_bwd", bg=leaving)
    plan.grads_done(landed)
    d_lf, d_sf, d_gf, d_dtf = flat(d_l), flat(d_s), flat(d_gates), flat(d_dt)
    g_in = jnp.concatenate([
        mm_tn(h1f, d_lf, name="in_dw_lru"), mm_tn(h1f, d_sf, name="in_dw_ssd"),
        mm_tn(h1f, d_dtf, name="in_dw_dt")[:, :SSD_HEADS], mm_tn(h1f, d_gf, name="in_dw_gates")], axis=1)
    leaving = plan.grads_bg({"w_in": g_in})
    partial = [flat(dh_lru), flat(dh_ssd)]
    if leaving is None:
        d_h1 = mm_nn([(d_gf, wt["gates"]), (d_dtf, wt["dt"])], add=partial, name="in_dx_gates")
    else:
        d_h1, landed = mm_nn([(d_gf, wt["gates"]), (d_dtf, wt["dt"])], add=partial, name="in_dx_gates", bg=leaving)
        plan.grads_done(landed)
    grad_x, pb_c, gl_d = first_bwd(unflat(d_h1), dx1, x, mod, small["pre_norm1"], name="first_bwd")

    d_mod = jnp.stack([pb_c[:, 0], pb_c[:, 1], pb_b[:, 2], pb_b[:, 0], pb_b[:, 1], pb_a[:, 0]], axis=1)
    loss_cols = gl_a[1:2]
    nh = SSD_HEADS
    small_grads = {
        "pre_norm1": gl_d[0:1], "post_norm1": gl_b[1:2], "b_gate": gl_c[0:1],
        "lru_conv_w": lru_rows[4:8], "lru_conv_b": lru_rows[3:4],
        "lru_wa": _lru_block_grads(g_wa_b), "lru_ba": lru_rows[0:1],
        "lru_wx": _lru_block_grads(g_wx_b), "lru_bx": lru_rows[1:2], "lru_lambda": lru_rows[2:3],
        "ssd_conv_w": r4[1:5], "ssd_conv_b": r4[0:1],
        "ssd_dt_bias": r1[0:1, :nh], "ssd_a_log": r1[1:2, :nh], "ssd_d": r1[2:3, :nh],
        "ssd_norm_w": r2[0:1], "pre_norm2": gl_b[0:1], "post_norm2": gl_a[0:1],
    }
    return loss_cols, grad_x, d_mod, small_grads


def _position():
    return lax.axis_index("x"), lax.axis_index("y"), lax.axis_index("c")


def _other_chips(x, y):
    return [(1 - x, y), (x, 1 - y), (1 - x, 1 - y)]


def allgather8(v, *, name):
    m_per, n = v.shape

    def body(x_ref, out_ref, send_sems, recv_sems, local_sem):
        x, y, c = _position()
        me, sibling = (x, y, c), (x, y, 1 - c)
        chips = _other_chips(x, y)

        def rows(px, py, pc):
            return out_ref.at[pl.ds((4 * px + 2 * py + pc) * m_per, m_per), :]

        def copy(k, block, to, src=None):
            return pltpu.make_async_remote_copy(
                src_ref=rows(*block) if src is None else src, dst_ref=rows(*block),
                send_sem=send_sems.at[k], recv_sem=recv_sems.at[k], device_id=to, device_id_type=MESH)

        mine = pltpu.make_async_copy(x_ref, rows(*me), local_sem)
        mine.start()
        first = [copy(0, me, sibling, src=x_ref)]
        first += [copy(1 + j, me, (*chip, c), src=x_ref) for j, chip in enumerate(chips)]
        for cp in first:
            cp.start()
        passed = [copy(4 + j, (*chip, c), sibling) for j, chip in enumerate(chips)]
        for j, chip in enumerate(chips):
            copy(1 + j, (*chip, c), me).wait_recv()
            passed[j].start()
        copy(0, sibling, me).wait_recv()
        for j, chip in enumerate(chips):
            copy(4 + j, (*chip, 1 - c), me).wait_recv()
        for cp in first + passed:
            cp.wait_send()
        mine.wait()

    return _pcall(
        body, name=name,
        out_shape=jax.ShapeDtypeStruct((N_DEV * m_per, n), v.dtype),
        in_specs=[pl.BlockSpec(memory_space=pltpu.VMEM)],
        out_specs=pl.BlockSpec(memory_space=pltpu.VMEM),
        scratch_shapes=[pltpu.SemaphoreType.DMA((7,)), pltpu.SemaphoreType.DMA((7,)), pltpu.SemaphoreType.DMA],
    )(v)


def gather_weights(shards, *, name):
    n = len(shards)
    half = [s.shape[0] // 2 for s in shards]
    widths = sorted({s.shape[1] for s in shards})
    chunk_rows = [_stage_rows(h, s.shape[1], itemsize=s.dtype.itemsize) for s, h in zip(shards, half)]
    plan = [(w, j, r0) for w in range(n) for j in range(N_CHIPS - 1) for r0 in range(0, half[w], chunk_rows[w])]

    def body(*refs):
        ins, outs = refs[:n], refs[n:2 * n]
        send_sems, recv_sems, local_sems, passed_sems = refs[2 * n:2 * n + 4]
        stage = refs[2 * n + 4:]
        bufs = {wd: stage[4 * i] for i, wd in enumerate(widths)}
        load_sems = {wd: stage[4 * i + 1] for i, wd in enumerate(widths)}
        stage_send = {wd: stage[4 * i + 2] for i, wd in enumerate(widths)}
        x, y, c = _position()
        me_chip = 2 * x + y
        chips = _other_chips(x, y)

        def piece(w, chip, core):
            return outs[w].at[chip, pl.ds(core * half[w], half[w]), :]

        def over_ici(w, j, chip, src=None):
            px, py = chips[j]
            dst = piece(w, chip, c)
            return pltpu.make_async_remote_copy(
                src_ref=dst if src is None else src, dst_ref=dst, send_sem=send_sems.at[3 * w + j],
                recv_sem=recv_sems.at[3 * w + j], device_id=(px, py, c), device_id_type=MESH)

        local = [pltpu.make_async_copy(ins[w], outs[w].at[me_chip], local_sems.at[w]) for w in range(n)]
        for cp in local:
            cp.start()
        sent = []
        for w in range(n):
            for j in range(N_CHIPS - 1):
                cp = over_ici(w, j, me_chip, src=ins[w].at[pl.ds(c * half[w], half[w]), :])
                cp.start()
                sent.append(cp)
        chunks = []
        for idx, (w, j, r0) in enumerate(plan):
            wd, rb = shards[w].shape[1], chunk_rows[w]
            k = 2 * chips[j][0] + chips[j][1]

            def make(staged, slot, idx=idx, w=w, k=k, r0=r0, wd=wd, rb=rb):
                return pltpu.make_async_remote_copy(
                    src_ref=staged, dst_ref=outs[w].at[k, pl.ds(c * half[w] + r0, rb), :],
                    send_sem=stage_send[wd].at[slot], recv_sem=passed_sems.at[idx],
                    device_id=(x, y, 1 - c), device_id_type=MESH), True

            chunk = (wd, outs[w].at[k, pl.ds(c * half[w] + r0, rb), :], [make])
            if r0 == 0:
                chunk += (lambda w=w, j=j, k=k: over_ici(w, j, k).wait_recv(),)
            chunks.append(chunk)
        _staged(chunks, bufs, load_sems)
        for idx, (w, j, r0) in enumerate(plan):
            wd = shards[w].shape[1]
            k = 2 * chips[j][0] + chips[j][1]
            landed = outs[w].at[k, pl.ds((1 - c) * half[w] + r0, chunk_rows[w]), :]
            pltpu.make_async_remote_copy(
                src_ref=landed, dst_ref=landed, send_sem=stage_send[wd].at[0], recv_sem=passed_sems.at[idx],
                device_id=(x, y, 1 - c), device_id_type=MESH).wait_recv()
        for cp in sent:
            cp.wait_send()
        for cp in local:
            cp.wait()

    stage_rows = [(wd, max(r for s, r in zip(shards, chunk_rows) if s.shape[1] == wd)) for wd in widths]
    return _pcall(
        body, name=name,
        out_shape=[jax.ShapeDtypeStruct((N_CHIPS,) + s.shape, s.dtype) for s in shards],
        in_specs=[ANY] * n, out_specs=[ANY] * n,
        scratch_shapes=[pltpu.SemaphoreType.DMA((3 * n,)), pltpu.SemaphoreType.DMA((3 * n,)),
                        pltpu.SemaphoreType.DMA((n,)), pltpu.SemaphoreType.DMA((len(plan),))]
        + _stage_scratch(stage_rows, shards[0].dtype),
    )(*shards)


STAGE_BYTES = 2 << 20


def _stage_rows(rows, width, itemsize=4):
    return _pick(rows, tuple(t for t in (1024, 512, 256, 128, 64, 32, 16, 8) if t * width * itemsize <= STAGE_BYTES * 3 // 2))


def _staged(chunks, bufs, load_sems):
    count, pending = {}, {}

    def load(i):
        cls, src = chunks[i][0], chunks[i][1]
        if len(chunks[i]) > 3:
            chunks[i][3]()
        slot = count.get(cls, 0) % 2
        count[cls] = count.get(cls, 0) + 1
        for cp, remote in pending.pop((cls, slot), []):
            if remote:
                cp.wait_send()
            else:
                cp.wait()
        staged = bufs[cls].at[slot, pl.ds(0, src.shape[0]), :]
        ld = pltpu.make_async_copy(src, staged, load_sems[cls].at[slot])
        ld.start()
        return ld, cls, slot, staged

    cur = load(0)
    for i in range(len(chunks)):
        nxt = load(i + 1) if i + 1 < len(chunks) else None
        ld, cls, slot, staged = cur
        ld.wait()
        started = []
        for make in chunks[i][2]:
            cp, remote = make(staged, slot)
            cp.start()
            started.append((cp, remote))
        pending[(cls, slot)] = started
        cur = nxt
    for started in pending.values():
        for cp, remote in started:
            if remote:
                cp.wait_send()
            else:
                cp.wait()


def _stage_scratch(widths_rows, dtype):
    scratch = []
    for width, rows in widths_rows:
        scratch += [pltpu.VMEM((2, rows, width), dtype), pltpu.SemaphoreType.DMA((2,)), pltpu.SemaphoreType.DMA((2,)),
                    pltpu.SemaphoreType.DMA((2,))]
    return scratch


def send_half_to_sibling(grads, *, name):
    n = len(grads)
    half = [g.shape[1] // 2 for g in grads]
    widths = sorted({g.shape[2] for g in grads})
    chunk_rows = [_stage_rows(h, g.shape[2]) for g, h in zip(grads, half)]
    plan = [(w, k, r0) for w in range(n) for k in range(N_CHIPS) for r0 in range(0, half[w], chunk_rows[w])]

    def body(*refs):
        ins, theirs = refs[:n], refs[n:2 * n]
        recv_sems = refs[2 * n]
        stage = refs[2 * n + 1:]
        bufs = {wd: stage[4 * i] for i, wd in enumerate(widths)}
        load_sems = {wd: stage[4 * i + 1] for i, wd in enumerate(widths)}
        send_sems = {wd: stage[4 * i + 2] for i, wd in enumerate(widths)}
        x, y, c = _position()
        chunks = []
        for idx, (w, k, r0) in enumerate(plan):
            wd = grads[w].shape[2]
            rb = chunk_rows[w]

            def make(staged, slot, idx=idx, w=w, k=k, r0=r0, wd=wd, rb=rb):
                return pltpu.make_async_remote_copy(
                    src_ref=staged, dst_ref=theirs[w].at[k, pl.ds(r0, rb), :], send_sem=send_sems[wd].at[slot],
                    recv_sem=recv_sems.at[idx], device_id=(x, y, 1 - c), device_id_type=MESH), True

            chunks.append((wd, ins[w].at[k, pl.ds((1 - c) * half[w] + r0, rb), :], [make]))
        _staged(chunks, bufs, load_sems)
        for idx, (w, k, r0) in enumerate(plan):
            wd = grads[w].shape[2]
            landed = theirs[w].at[k, pl.ds(r0, chunk_rows[w]), :]
            pltpu.make_async_remote_copy(
                src_ref=landed, dst_ref=landed, send_sem=send_sems[wd].at[0], recv_sem=recv_sems.at[idx],
                device_id=(x, y, 1 - c), device_id_type=MESH).wait_recv()

    stage_rows = [(wd, max(r for g, r in zip(grads, chunk_rows) if g.shape[2] == wd)) for wd in widths]
    return _pcall(
        body, name=name,
        out_shape=[jax.ShapeDtypeStruct((N_CHIPS, h, g.shape[2]), g.dtype) for g, h in zip(grads, half)],
        in_specs=[ANY] * n, out_specs=[ANY] * n,
        scratch_shapes=[pltpu.SemaphoreType.DMA((len(plan),))] + _stage_scratch(stage_rows, F32),
    )(*grads)


def _chip_exchange_background(arrays, out_shapes, src_of, dst_of, landed_of, own_of):
    n = len(arrays)

    def copies(ins, outs, scr):
        send_sems, recv_sems, local_sems = scr
        x, y, c = _position()
        me_chip = 2 * x + y
        local, sends, recvs = [], [], []
        for w in range(n):
            local.append(pltpu.make_async_copy(*own_of(ins[w], outs[w], w, me_chip), local_sems.at[w]))
            for j, (px, py) in enumerate(_other_chips(x, y)):
                sems = dict(send_sem=send_sems.at[3 * w + j], recv_sem=recv_sems.at[3 * w + j],
                            device_id=(px, py, c), device_id_type=MESH)
                sends.append(pltpu.make_async_remote_copy(
                    src_ref=src_of(ins[w], w, 2 * px + py, me_chip, c), dst_ref=dst_of(outs[w], w, me_chip, c), **sems))
                landed = landed_of(outs[w], w, 2 * px + py, c)
                recvs.append(pltpu.make_async_remote_copy(src_ref=landed, dst_ref=landed, **sems))
        return local, sends, recvs

    def start(ins, outs, scr):
        local, sends, _ = copies(ins, outs, scr)
        for cp in local + sends:
            cp.start()

    def finish(ins, outs, scr):
        local, sends, recvs = copies(ins, outs, scr)
        for cp in recvs:
            cp.wait_recv()
        for cp in sends:
            cp.wait_send()
        for cp in local:
            cp.wait()

    scratch = [pltpu.SemaphoreType.DMA((3 * n,)), pltpu.SemaphoreType.DMA((3 * n,)), pltpu.SemaphoreType.DMA((n,))]
    return Background(arrays, out_shapes, scratch, start, finish)


def scatter_background(parts):
    return _chip_exchange_background(
        parts, [jax.ShapeDtypeStruct(p.shape, p.dtype) for p in parts],
        src_of=lambda ref, w, peer, me, c: ref.at[peer], dst_of=lambda ref, w, me, c: ref.at[me],
        landed_of=lambda ref, w, peer, c: ref.at[peer], own_of=lambda i, o, w, me: (i.at[me], o.at[me]))


def gather_halves_background(shards):
    half = [s.shape[0] // 2 for s in shards]
    rows = lambda w, c: pl.ds(c * half[w], half[w])
    return _chip_exchange_background(
        shards, [jax.ShapeDtypeStruct((N_CHIPS,) + s.shape, s.dtype) for s in shards],
        src_of=lambda ref, w, peer, me, c: ref.at[rows(w, c), :], dst_of=lambda ref, w, me, c: ref.at[me, rows(w, c), :],
        landed_of=lambda ref, w, peer, c: ref.at[peer, rows(w, c), :], own_of=lambda i, o, w, me: (i, o.at[me]))


def fill_other_half(gathered, *, name):
    n = len(gathered)
    half = [g.shape[1] // 2 for g in gathered]
    widths = sorted({g.shape[2] for g in gathered})
    chunk_rows = [_stage_rows(h, g.shape[2], itemsize=2) for g, h in zip(gathered, half)]
    plan = [(w, j, r0) for w in range(n) for j in range(N_CHIPS - 1) for r0 in range(0, half[w], chunk_rows[w])]

    def body(*refs):
        ins, outs = refs[:n], refs[n:2 * n]
        recv_sems = refs[2 * n]
        stage = refs[2 * n + 1:]
        bufs = {wd: stage[4 * i] for i, wd in enumerate(widths)}
        load_sems = {wd: stage[4 * i + 1] for i, wd in enumerate(widths)}
        send_sems = {wd: stage[4 * i + 2] for i, wd in enumerate(widths)}
        x, y, c = _position()
        chips = _other_chips(x, y)
        chunks = []
        for idx, (w, j, r0) in enumerate(plan):
            wd, rb = gathered[w].shape[2], chunk_rows[w]
            k = 2 * chips[j][0] + chips[j][1]

            def make(staged, slot, idx=idx, w=w, k=k, r0=r0, wd=wd, rb=rb):
                return pltpu.make_async_remote_copy(
                    src_ref=staged, dst_ref=outs[w].at[k, pl.ds(c * half[w] + r0, rb), :],
                    send_sem=send_sems[wd].at[slot], recv_sem=recv_sems.at[idx],
                    device_id=(x, y, 1 - c), device_id_type=MESH), True

            chunks.append((wd, ins[w].at[k, pl.ds(c * half[w] + r0, rb), :], [make]))
        _staged(chunks, bufs, load_sems)
        for idx, (w, j, r0) in enumerate(plan):
            wd = gathered[w].shape[2]
            k = 2 * chips[j][0] + chips[j][1]
            landed = outs[w].at[k, pl.ds((1 - c) * half[w] + r0, chunk_rows[w]), :]
            pltpu.make_async_remote_copy(
                src_ref=landed, dst_ref=landed, send_sem=send_sems[wd].at[0], recv_sem=recv_sems.at[idx],
                device_id=(x, y, 1 - c), device_id_type=MESH).wait_recv()

    stage_rows = [(wd, max(r for g, r in zip(gathered, chunk_rows) if g.shape[2] == wd)) for wd in widths]
    return _pcall(
        body, name=name, out_shape=[jax.ShapeDtypeStruct(g.shape, g.dtype) for g in gathered],
        in_specs=[ANY] * n, out_specs=[ANY] * n, input_output_aliases={w: w for w in range(n)},
        scratch_shapes=[pltpu.SemaphoreType.DMA((len(plan),))] + _stage_scratch(stage_rows, gathered[0].dtype),
    )(*gathered)


def join_with_sibling(halves, *, name):
    n = len(halves)
    widths = sorted({h.shape[1] for h in halves})
    chunk_rows = [_stage_rows(h.shape[0], h.shape[1]) for h in halves]
    plan = [(w, r0) for w in range(n) for r0 in range(0, halves[w].shape[0], chunk_rows[w])]

    def body(*refs):
        ins, outs = refs[:n], refs[n:2 * n]
        recv_sems = refs[2 * n]
        stage = refs[2 * n + 1:]
        bufs = {wd: stage[4 * i] for i, wd in enumerate(widths)}
        load_sems = {wd: stage[4 * i + 1] for i, wd in enumerate(widths)}
        send_sems = {wd: stage[4 * i + 2] for i, wd in enumerate(widths)}
        store_sems = {wd: stage[4 * i + 3] for i, wd in enumerate(widths)}
        x, y, c = _position()
        chunks = []
        for idx, (w, r0) in enumerate(plan):
            h, wd = halves[w].shape
            rb = chunk_rows[w]

            def to_sibling(staged, slot, idx=idx, w=w, r0=r0, h=h, wd=wd, rb=rb):
                return pltpu.make_async_remote_copy(
                    src_ref=staged, dst_ref=outs[w].at[pl.ds(c * h + r0, rb), :], send_sem=send_sems[wd].at[slot],
                    recv_sem=recv_sems.at[idx], device_id=(x, y, 1 - c), device_id_type=MESH), True

            def to_mine(staged, slot, w=w, r0=r0, h=h, wd=wd, rb=rb):
                return pltpu.make_async_copy(staged, outs[w].at[pl.ds(c * h + r0, rb), :], store_sems[wd].at[slot]), False

            chunks.append((wd, ins[w].at[pl.ds(r0, rb), :], [to_sibling, to_mine]))
        _staged(chunks, bufs, load_sems)
        for idx, (w, r0) in enumerate(plan):
            h, wd = halves[w].shape
            landed = outs[w].at[pl.ds((1 - c) * h + r0, chunk_rows[w]), :]
            pltpu.make_async_remote_copy(
                src_ref=landed, dst_ref=landed, send_sem=send_sems[wd].at[0], recv_sem=recv_sems.at[idx],
                device_id=(x, y, 1 - c), device_id_type=MESH).wait_recv()

    stage_rows = [(wd, max(r for h, r in zip(halves, chunk_rows) if h.shape[1] == wd)) for wd in widths]
    return _pcall(
        body, name=name,
        out_shape=[jax.ShapeDtypeStruct((2 * h.shape[0], h.shape[1]), h.dtype) for h in halves],
        in_specs=[ANY] * n, out_specs=[ANY] * n,
        scratch_shapes=[pltpu.SemaphoreType.DMA((len(plan),))] + _stage_scratch(stage_rows, F32),
    )(*halves)


def _row_tile(rows, cols, itemsize=4, budget=2 << 20):
    for t in (1024, 512, 256, 128, 64, 32, 16, 8):
        if rows % t == 0 and t * cols * itemsize <= budget:
            return t
    return rows


def add_half_to_bf16(core, full, theirs, *, name):
    k, r, c = theirs.shape
    tr = _row_tile(r, c)
    nb = r // tr

    def body(core_ref, a_ref, b_ref, o_ref):
        o_ref[...] = (a_ref[...] + b_ref[...]).astype(BF16)

    spec = pl.BlockSpec((None, tr, c), lambda i, j, core_ref: (i, j, 0))
    grid_spec = pltpu.PrefetchScalarGridSpec(
        num_scalar_prefetch=1, grid=(k, nb),
        in_specs=[pl.BlockSpec((None, tr, c), lambda i, j, core_ref: (i, core_ref[0] * nb + j, 0)), spec],
        out_specs=spec)
    return _pcall(body, name=name, grid_spec=grid_spec,
                  out_shape=jax.ShapeDtypeStruct(theirs.shape, BF16))(core, full, theirs)


def sum_blocks(v, *, name):
    k, r, c = v.shape
    tr = _row_tile(r, c * k)

    def body(v_ref, o_ref):
        acc = v_ref[0].astype(F32)
        for j in range(1, k):
            acc = acc + v_ref[j].astype(F32)
        o_ref[...] = acc

    return _pcall(body, name=name, grid=(r // tr,),
                  in_specs=[pl.BlockSpec((k, tr, c), lambda i: (0, i, 0))],
                  out_specs=pl.BlockSpec((tr, c), lambda i: (i, 0)),
                  out_shape=jax.ShapeDtypeStruct((r, c), F32))(v)


def adamw(w, g, m, v, *, name):
    r, c = w.shape
    tr = _row_tile(r, c, budget=1 << 20)
    m_scale = 1.0 / (1.0 - ADAM_B1 ** ADAM_STEP)
    v_scale = 1.0 / (1.0 - ADAM_B2 ** ADAM_STEP)

    def body(w_ref, g_ref, m_ref, v_ref, d_ref, nm_ref, nv_ref):
        gv = g_ref[...]
        nm = ADAM_B1 * m_ref[...] + (1.0 - ADAM_B1) * gv
        nv = ADAM_B2 * v_ref[...] + (1.0 - ADAM_B2) * (gv * gv)
        nm_ref[...] = nm
        nv_ref[...] = nv
        d_ref[...] = -ADAM_LR * ((nm * m_scale) / (jnp.sqrt(nv * v_scale) + ADAM_EPS) + ADAM_WD * w_ref[...])

    spec = pl.BlockSpec((tr, c), lambda i: (i, 0))
    return _pcall(body, name=name, grid=(r // tr,), in_specs=[spec] * 4, out_specs=[spec] * 3,
                  out_shape=[jax.ShapeDtypeStruct((r, c), F32)] * 3)(w, g, m, v)


def ada_fwd(c_all, w_shard, b_shard, *, name):
    bsz, d = c_all.shape
    ncol = w_shard.shape[1]

    def body(c_ref, w_ref, b_ref, o_ref):
        cv = c_ref[...]
        act = (cv * _sigmoid(cv)).astype(BF16)
        o_ref[...] = _dot(act, w_ref[...].astype(BF16)) + b_ref[...]

    tn = _pick(ncol, (512, 256, 128))
    return _pcall(body, name=name, grid=(ncol // tn,),
                  in_specs=[pl.BlockSpec((bsz, d), lambda j: (0, 0)), pl.BlockSpec((d, tn), lambda j: (0, j)),
                            pl.BlockSpec((1, tn), lambda j: (0, j))],
                  out_specs=pl.BlockSpec((bsz, tn), lambda j: (0, j)),
                  out_shape=jax.ShapeDtypeStruct((bsz, ncol), F32))(c_all, w_shard, b_shard)


def ada_bwd(c_all, d_mod_all, d_mod_cols, *, name):
    bsz, d = c_all.shape
    ncol = d_mod_cols.shape[1]
    nall = d_mod_all.shape[1]

    def body(c_ref, da_ref, dc_ref, gw_ref, gb_ref):
        cv = c_ref[...]
        act = (cv * _sigmoid(cv)).astype(BF16)
        gw_ref[...] = _dot_tn(act, dc_ref[...].astype(BF16))
        gb_ref[...] = _colsum(da_ref[...])

    full = lambda s: pl.BlockSpec(s, lambda: (0,) * len(s))
    return _pcall(body, name=name,
                  in_specs=[full((bsz, d)), full((bsz, nall)), full((bsz, ncol))],
                  out_specs=[full((d, ncol)), full((1, nall))],
                  out_shape=[jax.ShapeDtypeStruct((d, ncol), F32), jax.ShapeDtypeStruct((1, nall), F32)],
                  )(c_all, d_mod_all, d_mod_cols)


WEIGHT_NAMES = ['w_ada', 'b_ada', 'pre_norm1', 'post_norm1', 'w_in', 'b_gate', 'lru_conv_w', 'lru_conv_b', 'lru_wa',
                'lru_ba', 'lru_wx', 'lru_bx', 'lru_lambda', 'w_pa', 'ssd_conv_w', 'ssd_conv_b', 'ssd_dt_bias',
                'ssd_a_log', 'ssd_d', 'ssd_norm_w', 'w_pb', 'w_out', 'pre_norm2', 'post_norm2', 'w_ff1', 'w_ff2']
BIG_NAMES = ['w_in', 'w_pa', 'w_pb', 'w_out', 'w_ff1', 'w_ff2']
COLUMN_SHARDED = ('w_in', 'w_ff1')
SMALL_NAMES = [n for n in WEIGHT_NAMES if n not in BIG_NAMES + ['w_ada', 'b_ada']]
SHARDED_SMALL = ('lru_conv_w', 'ssd_conv_w')
PACK_WIDTH = 1024


def _whole(name, gathered):
    if name in COLUMN_SHARDED:
        return jnp.transpose(gathered, (1, 0, 2)).reshape(gathered.shape[1], N_CHIPS * gathered.shape[2])
    return gathered.reshape(N_CHIPS * gathered.shape[1], gathered.shape[2])


def _by_chip(name, g):
    if name in COLUMN_SHARDED:
        return jnp.transpose(g.reshape(g.shape[0], N_CHIPS, g.shape[1] // N_CHIPS), (1, 0, 2))
    return g.reshape(N_CHIPS, g.shape[0] // N_CHIPS, g.shape[1])


class ChipExchange:
    def __init__(self, shards, core):
        self.shards, self.core = shards, core
        self.pending, self.halves = [], {}

    def weights_bg(self):
        return gather_halves_background(list(self.shards.values()))

    def weights(self, arrived):
        swapped = fill_other_half(arrived, name="weights_from_sibling")
        return {n: _whole(n, g) for n, g in zip(self.shards, swapped)}

    def grads_bg(self, grads):
        self.pending = list(grads)
        by_chip = [_by_chip(n, g) for n, g in grads.items()]
        theirs = send_half_to_sibling(by_chip, name="grads_to_sibling_" + self.pending[0])
        sums = [add_half_to_bf16(self.core, a, b, name="add_cores_" + n)
                for n, a, b in zip(self.pending, by_chip, theirs)]
        return scatter_background(sums)

    def grads_done(self, landed):
        for n, p in zip(self.pending, landed):
            self.halves[n] = sum_blocks(p, name="add_chips_" + n)

    def reduced(self):
        names = list(self.halves)
        return dict(zip(names, join_with_sibling([self.halves[n] for n in names], name="grads_join")))


def _pack(parts):
    flat = jnp.concatenate([p.reshape(-1).astype(F32) for p in parts])
    rows = -(-flat.shape[0] // (PACK_WIDTH * SUBLANES)) * SUBLANES
    return jnp.pad(flat, (0, rows * PACK_WIDTH - flat.shape[0])).reshape(rows, PACK_WIDTH)


def _unpack(packed, shapes):
    flat = packed.reshape(-1)
    out, pos = [], 0
    for s in shapes:
        size = int(np.prod(s))
        out.append(flat[pos:pos + size].reshape(s))
        pos += size
    return out


def kernel(x, c, w_ada, b_ada, pre_norm1, post_norm1, w_in, b_gate, lru_conv_w, lru_conv_b, lru_wa, lru_ba, lru_wx, lru_bx, lru_lambda, w_pa, ssd_conv_w, ssd_conv_b, ssd_dt_bias, ssd_a_log, ssd_d, ssd_norm_w, w_pb, w_out, pre_norm2, post_norm2, w_ff1, w_ff2, loss_target, m_w_ada, m_b_ada, m_pre_norm1, m_post_norm1, m_w_in, m_b_gate, m_lru_conv_w, m_lru_conv_b, m_lru_wa, m_lru_ba, m_lru_wx, m_lru_bx, m_lru_lambda, m_w_pa, m_ssd_conv_w, m_ssd_conv_b, m_ssd_dt_bias, m_ssd_a_log, m_ssd_d, m_ssd_norm_w, m_w_pb, m_w_out, m_pre_norm2, m_post_norm2, m_w_ff1, m_w_ff2, v_w_ada, v_b_ada, v_pre_norm1, v_post_norm1, v_w_in, v_b_gate, v_lru_conv_w, v_lru_conv_b, v_lru_wa, v_lru_ba, v_lru_wx, v_lru_bx, v_lru_lambda, v_w_pa, v_ssd_conv_w, v_ssd_conv_b, v_ssd_dt_bias, v_ssd_a_log, v_ssd_d, v_ssd_norm_w, v_w_pb, v_w_out, v_pre_norm2, v_post_norm2, v_w_ff1, v_w_ff2):
    given = dict(locals())
    bsz, seq, d = x.shape
    my_x, my_y, my_c = lax.axis_index("x"), lax.axis_index("y"), lax.axis_index("c")
    chip = 2 * my_x + my_y
    dev = 2 * chip + my_c
    strip = lambda a: a if a.ndim == 2 else a[0]
    w = {n: strip(given[n]) for n in WEIGHT_NAMES}
    m = {n: strip(given["m_" + n]) for n in WEIGHT_NAMES}
    v = {n: strip(given["v_" + n]) for n in WEIGHT_NAMES}

    first_shapes = [c.shape] + [w[n].shape for n in SHARDED_SMALL]
    first = allgather8(_pack([c] + [w[n] for n in SHARDED_SMALL]), name="gather_c_conv")
    first = first.reshape(N_DEV, -1, PACK_WIDTH)
    per_dev = [_unpack(first[k], first_shapes) for k in range(N_DEV)]
    c_all = jnp.concatenate([p[0] for p in per_dev], axis=0)
    conv_full = {n: jnp.concatenate([per_dev[2 * k][1 + i] for k in range(N_CHIPS)], axis=1)
                 for i, n in enumerate(SHARDED_SMALL)}

    ncol = w["w_ada"].shape[1]
    b_cols = lax.dynamic_slice(b_ada, (0, chip * ncol), (1, ncol))
    mod_cols = ada_fwd(c_all, w["w_ada"], b_cols, name="ada_fwd")
    mod_all = allgather8(mod_cols, name="gather_mod").reshape(N_CHIPS, 2, N_DEV * bsz, ncol)[:, 0]
    mod_all = jnp.transpose(mod_all, (1, 0, 2)).reshape(N_DEV * bsz, N_CHIPS * ncol)
    mod = lax.dynamic_slice(mod_all, (dev * bsz, 0), (bsz, 6 * d)).reshape(bsz, 6, d)
    mod = jnp.pad(mod, ((0, 0), (0, 2), (0, 0)))

    w_in_full = _whole("w_in", gather_weights([w["w_in"].astype(BF16)], name="gather_w_in")[0])
    big = {"w_main": w_in_full[:, :8192],
           "w_dt": jnp.pad(w_in_full[:, 8192:8192 + SSD_HEADS], ((0, 0), (0, LANES - SSD_HEADS))),
           "w_gates": w_in_full[:, 8192 + SSD_HEADS:]}
    small = {n: w[n] for n in SMALL_NAMES}
    small.update(conv_full)
    plan = ChipExchange({n: w[n].astype(BF16) for n in BIG_NAMES if n != "w_in"}, my_c.astype(jnp.int32).reshape(1))

    loss_cols, grad_x, d_mod, small_grads = local_step(x, loss_target, mod, big, small, plan)

    packed = _pack([d_mod, loss_cols] + [small_grads[n] for n in SMALL_NAMES])
    rows = packed.shape[0]
    everyone = allgather8(packed, name="gather_small").reshape(N_DEV, rows, PACK_WIDTH)
    d_mod_all = everyone[:, :bsz * 6].reshape(N_DEV * bsz, 6 * d)
    summed = sum_blocks(everyone, name="sum_small")
    shapes = [d_mod.shape, loss_cols.shape] + [small_grads[n].shape for n in SMALL_NAMES]
    parts = _unpack(summed, shapes)
    loss = jnp.sum(parts[1])
    grads = dict(zip(SMALL_NAMES, parts[2:]))
    for n in SHARDED_SMALL:
        cols = w[n].shape[1]
        grads[n] = lax.dynamic_slice(grads[n], (0, chip * cols), (grads[n].shape[0], cols))
    d_mod_cols = lax.dynamic_slice(d_mod_all, (0, chip * ncol), (N_DEV * bsz, ncol))
    grads["w_ada"], grads["b_ada"] = ada_bwd(c_all, d_mod_all, d_mod_cols, name="ada_bwd")

    grads.update(plan.reduced())

    delta, new_m, new_v = {}, {}, {}
    for n in BIG_NAMES + ["w_ada", "b_ada"]:
        delta[n], new_m[n], new_v[n] = adamw(w[n], grads[n], m[n], v[n], name="adamw_" + n)
    shapes = [w[n].shape for n in SMALL_NAMES]
    pk = lambda src: _pack([src[n] for n in SMALL_NAMES])
    upd = adamw(pk(w), pk(grads), pk(m), pk(v), name="adamw_small")
    for out, packed_out in zip((delta, new_m, new_v), upd):
        out.update(zip(SMALL_NAMES, _unpack(packed_out, shapes)))

    shaped = lambda src: [src[n].reshape(given[n].shape) for n in WEIGHT_NAMES]
    return (loss, grad_x, *shaped(grads), *shaped(delta), *shaped(new_m), *shaped(new_v))
```

```python
import functools
import math

import numpy as np
import jax
import jax.numpy as jnp
from jax import lax
from jax.experimental import pallas as pl
from jax.experimental.pallas import tpu as pltpu

F32 = jnp.float32
BF16 = jnp.bfloat16
HI = lax.Precision.HIGHEST
MESH = pl.DeviceIdType.MESH

D_MODEL = 1024
LRU_HEADS = 16
LRU_HEAD_DIM = 64
LRU_C = 8.0
SSD_INNER = 2048
SSD_HEADS = 32
SSD_HEAD_DIM = 64
SSD_GROUPS = 8
SSD_STATE = 128
SSD_CHUNK = 128
SSD_CONV_DIM = 4096
D_FF = 4096
EPS = 1e-6
N_CHIPS = 4
N_DEV = 8
LANES = 128
SUBLANES = 8

ADAM_LR = 0.001
ADAM_B1 = 0.9
ADAM_B2 = 0.999
ADAM_EPS = 1e-08
ADAM_WD = 0.01
ADAM_STEP = 10


ANY = pl.BlockSpec(memory_space=pl.ANY)


def _pcall(body, **kw):
    return pl.pallas_call(body, **kw)


class Background:
    def __init__(self, inputs, out_shapes, scratch, start, finish):
        self.inputs, self.out_shapes, self.scratch = list(inputs), list(out_shapes), list(scratch)
        self.start, self.finish = start, finish

    def wrap(self, body, kw):
        n_in, n_out = len(kw["in_specs"]), len(kw["out_specs"])
        n_scr = len(kw.get("scratch_shapes", []))
        b_in, b_out = len(self.inputs), len(self.out_shapes)
        grid = kw["grid"]

        def wrapped(*refs):
            ins, b_ins = refs[:n_in], refs[n_in:n_in + b_in]
            o0 = n_in + b_in
            outs, b_outs = refs[o0:o0 + n_out], refs[o0 + n_out:o0 + n_out + b_out]
            s0 = o0 + n_out + b_out
            scr, b_scr = refs[s0:s0 + n_scr], refs[s0 + n_scr:]
            ids = [pl.program_id(a) for a in range(len(grid))]
            first = functools.reduce(jnp.logical_and, [i == 0 for i in ids])
            last = functools.reduce(jnp.logical_and, [i == g - 1 for i, g in zip(ids, grid)])

            @pl.when(first)
            def _():
                self.start(b_ins, b_outs, b_scr)

            body(*ins, *outs, *scr)

            @pl.when(last)
            def _():
                self.finish(b_ins, b_outs, b_scr)

        kw = dict(kw, in_specs=list(kw["in_specs"]) + [ANY] * b_in, out_specs=list(kw["out_specs"]) + [ANY] * b_out,
                  out_shape=list(kw["out_shape"]) + self.out_shapes,
                  scratch_shapes=list(kw.get("scratch_shapes", [])) + self.scratch)
        return wrapped, kw


def _run(body, args, bg, **kw):
    n_out = len(kw["out_shape"])
    if bg is None:
        return list(_pcall(body, **kw)(*args)), []
    body, kw = bg.wrap(body, kw)
    outs = _pcall(body, **kw)(*args, *bg.inputs)
    return list(outs[:n_out]), list(outs[n_out:])


def _sigmoid(v):
    return 1.0 / (1.0 + jnp.exp(-v))


def _log1p(u):
    return jnp.where(u < 1e-3, u * (1.0 - u * (0.5 - u * (1.0 / 3.0))), jnp.log(1.0 + u))


def _softplus(v):
    return jnp.maximum(v, 0.0) + _log1p(jnp.exp(-jnp.abs(v)))


def _neg_expm1(v):
    small = -v * (1.0 + v * (0.5 + v * (1.0 / 6.0 + v * (1.0 / 24.0))))
    return jnp.where(v > -0.05, small, 1.0 - jnp.exp(v))


_GELU_K = math.sqrt(2.0 / math.pi)


def _gelu(v):
    t = jnp.tanh(_GELU_K * (v + 0.044715 * v * v * v))
    return 0.5 * v * (1.0 + t)


def _gelu_grad(v):
    t = jnp.tanh(_GELU_K * (v + 0.044715 * v * v * v))
    return 0.5 * (1.0 + t) + 0.5 * v * (1.0 - t * t) * _GELU_K * (1.0 + 3.0 * 0.044715 * v * v)


def _colsum(v):
    return jnp.sum(v, axis=0, keepdims=True)


def _dot(a, b, precision=None):
    return lax.dot_general(a, b, (((1,), (0,)), ((), ())), preferred_element_type=F32, precision=precision)


def _dot_nt(a, b):
    return lax.dot_general(a, b, (((1,), (1,)), ((), ())), preferred_element_type=F32)


def _dot_tn(a, b):
    return lax.dot_general(a, b, (((0,), (0,)), ((), ())), preferred_element_type=F32)


def _shift_down(xt, prev8, j):
    if j == 0:
        return xt
    n = xt.shape[0]
    r = pltpu.roll(xt, j, 0)
    p = pltpu.roll(prev8, j, 0)
    rows = lax.broadcasted_iota(jnp.int32, (SUBLANES, xt.shape[1]), 0)
    top = jnp.where(rows < j, p, r[0:SUBLANES])
    if n == SUBLANES:
        return top
    return jnp.concatenate([top, r[SUBLANES:]], axis=0)


def _shift_up(xt, next8, j):
    if j == 0:
        return xt
    n = xt.shape[0]
    r = pltpu.roll(xt, n - j, 0)
    p = pltpu.roll(next8, SUBLANES - j, 0)
    rows = lax.broadcasted_iota(jnp.int32, (SUBLANES, xt.shape[1]), 0)
    bot = jnp.where(rows >= SUBLANES - j, p, r[n - SUBLANES:])
    if n == SUBLANES:
        return bot
    return jnp.concatenate([r[:n - SUBLANES], bot], axis=0)


def _conv4(xt, prev8, w, b):
    out = b + w[3:4] * xt
    for k in range(3):
        out = out + w[k:k + 1] * _shift_down(xt, prev8, 3 - k)
    return out


def _conv4_bwd(d_out, next8, xt, w):
    d_x = w[3:4] * d_out
    d_w = []
    for k in range(3):
        up = _shift_up(d_out, next8, 3 - k)
        d_x = d_x + w[k:k + 1] * up
        d_w.append(_colsum(up * xt))
    d_w.append(_colsum(d_out * xt))
    return d_x, d_w, _colsum(d_out)


def _stack_rows(rows, width):
    rows = list(rows) + [jnp.zeros((1, width), F32)] * (SUBLANES - len(rows))
    return jnp.concatenate(rows, axis=0)


def _pick(n, cands):
    for c in cands:
        if n % c == 0:
            return c
    raise ValueError(f"no tile for {n}")


MM_ROWS = 1024
MM_VMEM_BUDGET = 36 << 20
MM_PANEL_COLS = 2048
MM_SUB = 512


def mm_nn(pairs, *, name, out_dtype=F32, a_fn=None, add=None, epi=None, extra=None, bg=None):
    np_ = len(pairs)
    m, n = pairs[0][0].shape[0], pairs[0][1].shape[1]
    pn = n if n <= MM_PANEL_COLS else _pick(n, (MM_PANEL_COLS, 1024, 512, 256, 128))
    ns = _pick(pn, (MM_SUB, 256, 128))
    adds = list(add or ())
    has_extra = extra is not None
    stage0 = a_fn is not None or pairs[0][0].dtype != BF16

    def vmem_bytes(rows):
        tiles = sum(rows * a.shape[1] * a.dtype.itemsize for a, _ in pairs)
        tiles += rows * pn * (4 * len(adds) + (extra.dtype.itemsize if has_extra else 0) + jnp.dtype(out_dtype).itemsize)
        panels = sum(b.shape[0] * pn * b.dtype.itemsize for _, b in pairs)
        return 2 * (tiles + panels) + (rows * pairs[0][0].shape[1] * 2 if stage0 else 0)

    tm = _pick(m, (MM_ROWS, 512, 256, 128, 64, 32, 16, 8))
    if vmem_bytes(tm) > MM_VMEM_BUDGET:
        tm = _pick(m, (512, 256, 128, 64, 32, 16, 8))

    def body(*refs):
        a_refs, b_refs = refs[:np_], refs[np_:2 * np_]
        pos = 2 * np_
        extra_ref = None
        add_refs = refs[pos:pos + len(adds)]
        pos += len(adds)
        if has_extra:
            extra_ref = refs[pos]
            pos += 1
        o_ref = refs[pos]
        lhs = list(a_refs)
        if stage0:
            av = a_refs[0][...]
            if a_fn is not None:
                av = a_fn(av)
            refs[pos + 1][...] = av.astype(BF16)
            lhs[0] = refs[pos + 1]
        for n0 in range(0, pn, ns):
            sl = slice(n0, n0 + ns)
            acc = None
            for a_ref, b_ref in zip(lhs, b_refs):
                part = _dot(a_ref[...].astype(BF16), b_ref[:, sl])
                acc = part if acc is None else acc + part
            for add_ref in add_refs:
                acc = acc + add_ref[:, sl]
            if epi is not None:
                acc = epi(acc, extra_ref[:, sl]) if has_extra else epi(acc)
            o_ref[:, sl] = acc.astype(out_dtype)

    in_specs = [pl.BlockSpec((tm, a.shape[1]), lambda j, i: (i, 0)) for a, _ in pairs]
    in_specs += [pl.BlockSpec((b.shape[0], pn), lambda j, i: (0, j)) for _, b in pairs]
    args = [a for a, _ in pairs] + [b for _, b in pairs]
    tile = pl.BlockSpec((tm, pn), lambda j, i: (i, j))
    for extra_add in adds:
        in_specs.append(tile)
        args.append(extra_add)
    if has_extra:
        in_specs.append(tile)
        args.append(extra)
    outs, bg_outs = _run(
        body, args, bg, name=name, grid=(n // pn, m // tm), in_specs=in_specs, out_specs=[tile],
        out_shape=[jax.ShapeDtypeStruct((m, n), out_dtype)],
        scratch_shapes=[pltpu.VMEM((tm, pairs[0][0].shape[1]), BF16)] if stage0 else [])
    return outs[0] if bg is None else (outs[0], bg_outs)


MM_REDUCE_ROWS = 1024
MM_GRAD_ROWS = 1024
MM_GRAD_COLS = 2048


def mm_tn(a, b, *, name, a_fn=None):
    m, ka = a.shape
    nb = b.shape[1]
    pa = _pick(ka, (MM_GRAD_ROWS, 512, 256, 128))
    pb = nb if nb <= MM_GRAD_COLS else _pick(nb, (MM_GRAD_COLS, 1024, 512, 256, 128))
    ns = _pick(pb, (MM_SUB, 256, 128))
    tmk = _pick(m, (MM_REDUCE_ROWS, 512, 256, 128, 64, 32, 16))

    def body(a_ref, b_ref, o_ref, lhs):
        k = pl.program_id(2)

        @pl.when(k == 0)
        def _():
            o_ref[...] = jnp.zeros_like(o_ref)

        av = a_ref[...]
        if a_fn is not None:
            av = a_fn(av)
        lhs[...] = av.astype(BF16)
        for n0 in range(0, pb, ns):
            o_ref[:, n0:n0 + ns] += _dot_tn(lhs[...], b_ref[:, n0:n0 + ns].astype(BF16))

    return _pcall(
        body, name=name,
        grid=(ka // pa, nb // pb, m // tmk),
        in_specs=[pl.BlockSpec((tmk, pa), lambda i, j, k: (k, i)),
                  pl.BlockSpec((tmk, pb), lambda i, j, k: (k, j))],
        out_specs=pl.BlockSpec((pa, pb), lambda i, j, k: (i, j)),
        out_shape=jax.ShapeDtypeStruct((ka, nb), F32),
        scratch_shapes=[pltpu.VMEM((tmk, pa), BF16)],
    )(a, b)


def _relu_sq(v):
    r = jnp.maximum(v, 0.0)
    return r * r


ROW_TILE = 512


def _row_specs(bsz, seq, width, ts):
    return pl.BlockSpec((None, ts, width), lambda b, i: (b, i, 0))


def _vec_spec(width):
    return pl.BlockSpec((1, width), lambda b, i: (0, 0))


def _mod_spec():
    return pl.BlockSpec((None, SUBLANES, D_MODEL), lambda b, i: (b, 0, 0))


def _rstd(v):
    return lax.rsqrt(jnp.mean(v * v, axis=-1, keepdims=True) + EPS)


def prenorm(x, w, mod, *, name):
    bsz, seq, d = x.shape
    ts = _pick(seq, (ROW_TILE, 256, 128))

    def body(x_ref, w_ref, mod_ref, h_ref):
        xv = x_ref[...]
        m = mod_ref[...]
        xh = xv * _rstd(xv)
        h_ref[...] = ((xh * w_ref[...]) * (1.0 + m[1:2]) + m[0:1]).astype(BF16)

    return _pcall(
        body, name=name, grid=(bsz, seq // ts),
        in_specs=[_row_specs(bsz, seq, d, ts), _vec_spec(d), _mod_spec()],
        out_specs=_row_specs(bsz, seq, d, ts),
        out_shape=jax.ShapeDtypeStruct((bsz, seq, d), BF16),
    )(x, w, mod)


def post1_pre2(x, out1, mod, post1, pre2, *, name):
    bsz, seq, d = x.shape
    ts = _pick(seq, (ROW_TILE, 256, 128))

    def body(x_ref, o_ref, mod_ref, p1_ref, p2_ref, x1_ref, h2_ref):
        m = mod_ref[...]
        ov = o_ref[...]
        x1 = x_ref[...] + m[2:3] * ((ov * _rstd(ov)) * p1_ref[...])
        x1_ref[...] = x1
        xh = x1 * _rstd(x1)
        h2_ref[...] = ((xh * p2_ref[...]) * (1.0 + m[4:5]) + m[3:4]).astype(BF16)

    return _pcall(
        body, name=name, grid=(bsz, seq // ts),
        in_specs=[_row_specs(bsz, seq, d, ts), _row_specs(bsz, seq, d, ts), _mod_spec(), _vec_spec(d), _vec_spec(d)],
        out_specs=[_row_specs(bsz, seq, d, ts), _row_specs(bsz, seq, d, ts)],
        out_shape=[jax.ShapeDtypeStruct((bsz, seq, d), F32), jax.ShapeDtypeStruct((bsz, seq, d), BF16)],
    )(x, out1, mod, post1, pre2)


def _acc_specs(d):
    per_batch = pl.BlockSpec((None, SUBLANES, d), lambda b, i: (b, 0, 0))
    glob = pl.BlockSpec((SUBLANES, d), lambda b, i: (0, 0))
    return per_batch, glob


def _accumulate(pb_ref, gl_ref, pb_rows, gl_rows, width):
    b, i = pl.program_id(0), pl.program_id(1)

    @pl.when(i == 0)
    def _():
        pb_ref[...] = jnp.zeros_like(pb_ref)

    @pl.when((b == 0) & (i == 0))
    def _():
        gl_ref[...] = jnp.zeros_like(gl_ref)

    pb_ref[...] += _stack_rows(pb_rows, width)
    gl_ref[...] += _stack_rows(gl_rows, width)


def _rms_bwd(d_n, n, r):
    return r * (d_n - n * jnp.mean(d_n * n, axis=-1, keepdims=True))


def final_bwd(x1, y2, target, mod, post2, *, name):
    bsz, seq, d = x1.shape
    ts = _pick(seq, (ROW_TILE, 256, 128))

    def body(x1_ref, y_ref, t_ref, mod_ref, p_ref, dx_ref, dy_ref, pb_ref, gl_ref):
        m = mod_ref[...]
        g2 = m[5:6]
        yv = y_ref[...]
        r = _rstd(yv)
        n = yv * r
        o = n * p_ref[...]
        diff = (x1_ref[...] + g2 * o) - t_ref[...]
        dx = diff * (1.0 / d)
        dx_ref[...] = dx
        d_o = dx * g2
        dy_ref[...] = _rms_bwd(d_o * p_ref[...], n, r).astype(BF16)
        _accumulate(pb_ref, gl_ref, [_colsum(dx * o)], [_colsum(d_o * n), _colsum(diff * diff) * (0.5 / d)], d)

    pb, gl = _acc_specs(d)
    rs = _row_specs(bsz, seq, d, ts)
    return _pcall(
        body, name=name, grid=(bsz, seq // ts),
        in_specs=[rs, rs, rs, _mod_spec(), _vec_spec(d)],
        out_specs=[rs, rs, pb, gl],
        out_shape=[jax.ShapeDtypeStruct((bsz, seq, d), F32), jax.ShapeDtypeStruct((bsz, seq, d), BF16),
                   jax.ShapeDtypeStruct((bsz, SUBLANES, d), F32), jax.ShapeDtypeStruct((SUBLANES, d), F32)],
    )(x1, y2, target, mod, post2)


def mid_bwd(d_h2, dx2, x1, out1, mod, pre2, post1, *, name):
    bsz, seq, d = x1.shape
    ts = _pick(seq, (ROW_TILE, 256, 128))

    def body(dh_ref, dx2_ref, x1_ref, o_ref, mod_ref, p2_ref, p1_ref, dx1_ref, do_ref, pb_ref, gl_ref):
        m = mod_ref[...]
        dh = dh_ref[...]
        x1 = x1_ref[...]
        r2 = _rstd(x1)
        xh = x1 * r2
        xw = xh * p2_ref[...]
        d_xw = dh * (1.0 + m[4:5])
        dx1 = dx2_ref[...] + _rms_bwd(d_xw * p2_ref[...], xh, r2)
        dx1_ref[...] = dx1
        ov = o_ref[...]
        r1 = _rstd(ov)
        n1 = ov * r1
        o1 = n1 * p1_ref[...]
        d_o1 = dx1 * m[2:3]
        do_ref[...] = _rms_bwd(d_o1 * p1_ref[...], n1, r1).astype(BF16)
        _accumulate(pb_ref, gl_ref, [_colsum(dh), _colsum(dh * xw), _colsum(dx1 * o1)],
                    [_colsum(d_xw * xh), _colsum(d_o1 * n1)], d)

    pb, gl = _acc_specs(d)
    rs = _row_specs(bsz, seq, d, ts)
    return _pcall(
        body, name=name, grid=(bsz, seq // ts),
        in_specs=[rs, rs, rs, rs, _mod_spec(), _vec_spec(d), _vec_spec(d)],
        out_specs=[rs, rs, pb, gl],
        out_shape=[jax.ShapeDtypeStruct((bsz, seq, d), F32), jax.ShapeDtypeStruct((bsz, seq, d), BF16),
                   jax.ShapeDtypeStruct((bsz, SUBLANES, d), F32), jax.ShapeDtypeStruct((SUBLANES, d), F32)],
    )(d_h2, dx2, x1, out1, mod, pre2, post1)


def in_dx_first_bwd(d_gates, d_dt, wt_gates, wt_dt, dh_parts, dx1, x, mod, pre1, *, name, bg=None):
    bsz, seq, d = x.shape
    ts = _pick(seq, (ROW_TILE, 256, 128))
    n_parts = len(dh_parts)

    def body(dg_ref, dt_ref, wg_ref, wd_ref, *rest):
        part_refs = rest[:n_parts]
        dx1_ref, x_ref, mod_ref, p_ref, gx_ref, pb_ref, gl_ref = rest[n_parts:]
        dh = _dot(dg_ref[...], wg_ref[...]) + _dot(dt_ref[...], wd_ref[...])
        for part in part_refs:
            dh = dh + part[...]
        m = mod_ref[...]
        xv = x_ref[...]
        r = _rstd(xv)
        xh = xv * r
        xw = xh * p_ref[...]
        d_xw = dh * (1.0 + m[1:2])
        gx_ref[...] = dx1_ref[...] + _rms_bwd(d_xw * p_ref[...], xh, r)
        _accumulate(pb_ref, gl_ref, [_colsum(dh), _colsum(dh * xw)], [_colsum(d_xw * xh)], d)

    pb, gl = _acc_specs(d)
    rs = _row_specs(bsz, seq, d, ts)
    whole = lambda v: pl.BlockSpec(v.shape, lambda b, i: (0, 0))
    return _run(
        body, (d_gates, d_dt, wt_gates, wt_dt, *dh_parts, dx1, x, mod, pre1), bg, name=name, grid=(bsz, seq // ts),
        in_specs=[_row_specs(bsz, seq, d_gates.shape[2], ts), _row_specs(bsz, seq, d_dt.shape[2], ts),
                  whole(wt_gates), whole(wt_dt)] + [rs] * n_parts + [rs, rs, _mod_spec(), _vec_spec(d)],
        out_specs=[rs, pb, gl],
        out_shape=[jax.ShapeDtypeStruct((bsz, seq, d), F32),
                   jax.ShapeDtypeStruct((bsz, SUBLANES, d), F32), jax.ShapeDtypeStruct((SUBLANES, d), F32)])


def merge_bwd(d_merged, ya, yb, gates, b_gate, *, name):
    bsz, seq, d = ya.shape
    ts = _pick(seq, (ROW_TILE, 256, 128))

    def body(dm_ref, ya_ref, yb_ref, g_ref, b_ref, dya_ref, dyb_ref, dg_ref, gl_ref):
        b, i = pl.program_id(0), pl.program_id(1)
        g = _sigmoid(g_ref[...] + b_ref[...])
        dm = dm_ref[...]
        ga, gb = g[:, :d], g[:, d:]
        dya_ref[...] = (dm * ga).astype(BF16)
        dyb_ref[...] = (dm * gb).astype(BF16)
        dg = jnp.concatenate([dm * ya_ref[...] * ga * (1.0 - ga), dm * yb_ref[...] * gb * (1.0 - gb)], axis=1)
        dg_ref[...] = dg.astype(BF16)

        @pl.when((b == 0) & (i == 0))
        def _():
            gl_ref[...] = jnp.zeros_like(gl_ref)

        gl_ref[...] += _stack_rows([_colsum(dg)], 2 * d)

    rs = _row_specs(bsz, seq, d, ts)
    rs2 = _row_specs(bsz, seq, 2 * d, ts)
    return _pcall(
        body, name=name, grid=(bsz, seq // ts),
        in_specs=[rs, rs, rs, rs2, _vec_spec(2 * d)],
        out_specs=[rs, rs, rs2, pl.BlockSpec((SUBLANES, 2 * d), lambda b, i: (0, 0))],
        out_shape=[jax.ShapeDtypeStruct((bsz, seq, d), BF16), jax.ShapeDtypeStruct((bsz, seq, d), BF16),
                   jax.ShapeDtypeStruct((bsz, seq, 2 * d), BF16), jax.ShapeDtypeStruct((SUBLANES, 2 * d), F32)],
    )(d_merged, ya, yb, gates, b_gate)


LRU_TILE = 256
N_LRU_BLOCKS = D_MODEL // LANES


def _block_mm(v, w_ref, transpose=False):
    vb = v.astype(BF16)
    outs = []
    for j in range(N_LRU_BLOCKS):
        blk = vb[:, LANES * j:LANES * (j + 1)]
        outs.append(_dot_nt(blk, w_ref[j]) if transpose else _dot(blk, w_ref[j]))
    return jnp.concatenate(outs, axis=1)


def _lru_gates(xc, wa_ref, ba, wx_ref, bx, sp):
    r = _sigmoid(_block_mm(xc, wa_ref) + ba)
    i = _sigmoid(_block_mm(xc, wx_ref) + bx)
    la = (-LRU_C * r) * sp
    a = jnp.exp(la)
    sq = jnp.sqrt(_neg_expm1(2.0 * la))
    return r, i, a, sq


def _group_roll(v, shift):
    rows, width = v.shape
    return pltpu.roll(v.reshape(rows // SUBLANES, SUBLANES, width), shift, 1).reshape(rows, width)


def _group_scan(a, b, reverse=False):
    row = lax.broadcasted_iota(jnp.int32, a.shape, 0) % SUBLANES
    for s in (1, 2, 4):
        take = (row < SUBLANES - s) if reverse else (row >= s)
        shift = SUBLANES - s if reverse else s
        b = jnp.where(take, a * _group_roll(b, shift) + b, b)
        a = jnp.where(take, a * _group_roll(a, shift), a)
    return a, b


def _prev8_spec(width, col_block, tile_rows):
    per = tile_rows // SUBLANES
    return pl.BlockSpec((None, SUBLANES, width), lambda b, i: (b, jnp.maximum(i * per - 1, 0), col_block))


def lru_fwd(pm, cw, cb, wa, ba, wx, bx, lam, w_pa, *, name):
    bsz, seq, _ = pm.shape
    d = D_MODEL
    ts = _pick(seq, (LRU_TILE, 128))

    def body(lx_ref, lxp_ref, lg_ref, cw_ref, cb_ref, wa_ref, ba_ref, wx_ref, bx_ref, lam_ref, wpa_ref,
             h_ref, pa_ref, ya_ref, kept_ref, hc, a_s, u_s):
        i = pl.program_id(1)

        @pl.when(i == 0)
        def _():
            hc[...] = jnp.zeros_like(hc)

        lx = lx_ref[...]
        prev8 = jnp.where(i == 0, 0.0, lxp_ref[...])
        xc = _conv4(lx, prev8, cw_ref[...], cb_ref[...])
        sp = _softplus(-lam_ref[...])
        r, ig, a, sq = _lru_gates(xc, wa_ref, ba_ref[...], wx_ref, bx_ref[...], sp)
        for k, kept in enumerate((xc, r, ig, a, sq)):
            kept_ref[:, k * d:(k + 1) * d] = kept
        a_s[...], u_s[...] = _group_scan(a, sq * (ig * xc))

        def step(g, h):
            r0 = pl.multiple_of(g * SUBLANES, SUBLANES)
            h8 = a_s[pl.ds(r0, SUBLANES), :] * h + u_s[pl.ds(r0, SUBLANES), :]
            h_ref[pl.ds(r0, SUBLANES), :] = h8
            return h8[SUBLANES - 1:SUBLANES]

        hc[...] = lax.fori_loop(0, ts // SUBLANES, step, hc[...])
        pa_ref[...] = (h_ref[...] * _gelu(lg_ref[...])).astype(BF16)
        ya_ref[...] = _dot(pa_ref[...], wpa_ref[...])

    vec = _vec_spec(d)
    wspec = pl.BlockSpec((N_LRU_BLOCKS, LANES, LANES), lambda b, i: (0, 0, 0))
    rs = _row_specs(bsz, seq, d, ts)
    return _pcall(
        body, name=name, grid=(bsz, seq // ts),
        in_specs=[pl.BlockSpec((None, ts, d), lambda b, i: (b, i, 0)), _prev8_spec(d, 0, ts),
                  pl.BlockSpec((None, ts, d), lambda b, i: (b, i, 1)),
                  pl.BlockSpec((4, d), lambda b, i: (0, 0)), vec, wspec, vec, wspec, vec, vec,
                  pl.BlockSpec(w_pa.shape, lambda b, i: (0, 0))],
        out_specs=[rs, rs, rs, _row_specs(bsz, seq, 5 * d, ts)],
        out_shape=[jax.ShapeDtypeStruct((bsz, seq, d), F32), jax.ShapeDtypeStruct((bsz, seq, d), BF16),
                   jax.ShapeDtypeStruct((bsz, seq, d), F32), jax.ShapeDtypeStruct((bsz, seq, 5 * d), F32)],
        scratch_shapes=[pltpu.VMEM((1, d), F32), pltpu.VMEM((ts, d), F32), pltpu.VMEM((ts, d), F32)],
    )(pm, pm, pm, cw, cb, wa, ba, wx, bx, lam, w_pa)


def lru_bwd(pm, h, kept, d_ya, cw, wa, wx, lam, wt_pa, wt_lru, *, name, bg=None):
    bsz, seq, _ = pm.shape
    d = D_MODEL
    ts = _pick(seq, (LRU_TILE, 128))
    nt = seq // ts
    per = ts // SUBLANES

    def rev(i):
        return nt - 1 - i

    def body(lx_ref, lg_ref, h_ref, hp_ref, kept_ref, dya_ref, cw_ref, wa_ref, wx_ref,
             lam_ref, wtpa_ref, wtl_ref, dl_ref, dh1_ref, dwa_ref, dwx_ref, rows_ref,
             carry, dxc_next, a_s, dh_s, acc_s, a0_s):
        b, i = pl.program_id(0), pl.program_id(1)
        t = rev(i)

        @pl.when(i == 0)
        def _():
            carry[...] = jnp.zeros_like(carry)
            dxc_next[...] = jnp.zeros_like(dxc_next)

        @pl.when((b == 0) & (i == 0))
        def _():
            dwa_ref[...] = jnp.zeros_like(dwa_ref)
            dwx_ref[...] = jnp.zeros_like(dwx_ref)
            rows_ref[...] = jnp.zeros_like(rows_ref)

        lx = lx_ref[...]
        lg = lg_ref[...]
        cwv = cw_ref[...]
        lam_v = lam_ref[...]
        sp = _softplus(-lam_v)
        xc, r, ig, a, sq = (kept_ref[:, k * d:(k + 1) * d] for k in range(5))
        hv = h_ref[...]
        d_pa = _dot(dya_ref[...], wtpa_ref[...])
        row = lax.broadcasted_iota(jnp.int32, a.shape, 0) % SUBLANES
        a_next = jnp.where(row < SUBLANES - 1, _group_roll(a, SUBLANES - 1), 1.0)
        a_s[...], dh_s[...] = _group_scan(a_next, d_pa * _gelu(lg), reverse=True)
        a0_s[...] = a

        def step(g, c):
            r0 = pl.multiple_of((per - 1 - g) * SUBLANES, SUBLANES)
            acc8 = a_s[pl.ds(r0, SUBLANES), :] * c + dh_s[pl.ds(r0, SUBLANES), :]
            acc_s[pl.ds(r0, SUBLANES), :] = acc8
            return a0_s[pl.ds(r0, SUBLANES), :][0:1] * acc8[0:1]

        carry[...] = lax.fori_loop(0, per, step, carry[...])
        d_u = acc_s[...]
        hprev8 = jnp.where(t == 0, 0.0, hp_ref[...])
        d_a = d_u * _shift_down(hv, hprev8, 1)
        d_sq = d_u * (ig * xc)
        d_i = d_u * (sq * xc)
        d_xc = d_u * (sq * ig)
        d_la = d_a * a - d_sq * (a * a) / sq
        d_pre_r = (d_la * (-LRU_C * sp)) * (r * (1.0 - r))
        d_pre_i = d_i * (ig * (1.0 - ig))
        d_xc = d_xc + _block_mm(d_pre_r, wa_ref, transpose=True) + _block_mm(d_pre_i, wx_ref, transpose=True)
        xcb = xc.astype(BF16)
        drb = d_pre_r.astype(BF16)
        dib = d_pre_i.astype(BF16)
        for j in range(N_LRU_BLOCKS):
            sl = slice(LANES * j, LANES * (j + 1))
            dwa_ref[j] += _dot_tn(xcb[:, sl], drb[:, sl])
            dwx_ref[j] += _dot_tn(xcb[:, sl], dib[:, sl])
        d_lx, d_cw, d_cb = _conv4_bwd(d_xc, dxc_next[...], lx, cwv)
        dxc_next[...] = d_xc[0:SUBLANES]
        d_lam = _colsum(d_la * (-LRU_C * r)) * (-_sigmoid(-lam_v))
        rows_ref[...] += _stack_rows([_colsum(d_pre_r), _colsum(d_pre_i), d_lam, d_cb] + d_cw, d)
        dl_ref[:, :d] = d_lx.astype(BF16)
        dl_ref[:, d:] = (d_pa * hv * _gelu_grad(lg)).astype(BF16)
        dh1_ref[...] = _dot(dl_ref[...], wtl_ref[...])

    vec = _vec_spec(d)
    wspec = pl.BlockSpec((N_LRU_BLOCKS, LANES, LANES), lambda b, i: (0, 0, 0))
    tile = lambda col: pl.BlockSpec((None, ts, d), lambda b, i: (b, rev(i), col))
    prev8 = lambda col: pl.BlockSpec((None, SUBLANES, d), lambda b, i: (b, jnp.maximum(rev(i) * per - 1, 0), col))
    whole = lambda v: pl.BlockSpec(v.shape, lambda b, i: (0, 0))
    return _run(
        body, (pm, pm, h, h, kept, d_ya, cw, wa, wx, lam, wt_pa, wt_lru), bg, name=name, grid=(bsz, nt),
        in_specs=[tile(0), tile(1), tile(0), prev8(0), pl.BlockSpec((None, ts, 5 * d), lambda b, i: (b, rev(i), 0)),
                  tile(0), pl.BlockSpec((4, d), lambda b, i: (0, 0)), wspec, wspec, vec,
                  whole(wt_pa), whole(wt_lru)],
        out_specs=[pl.BlockSpec((None, ts, 2 * d), lambda b, i: (b, rev(i), 0)), tile(0), wspec, wspec,
                   pl.BlockSpec((SUBLANES, d), lambda b, i: (0, 0))],
        out_shape=[jax.ShapeDtypeStruct((bsz, seq, 2 * d), BF16), jax.ShapeDtypeStruct((bsz, seq, d), F32),
                   jax.ShapeDtypeStruct((N_LRU_BLOCKS, LANES, LANES), F32),
                   jax.ShapeDtypeStruct((N_LRU_BLOCKS, LANES, LANES), F32),
                   jax.ShapeDtypeStruct((SUBLANES, d), F32)],
        scratch_shapes=[pltpu.VMEM((1, d), F32), pltpu.VMEM((SUBLANES, d), F32),
                        pltpu.VMEM((ts, d), F32), pltpu.VMEM((ts, d), F32), pltpu.VMEM((ts, d), F32),
                        pltpu.VMEM((ts, d), F32)])


L = SSD_CHUNK
N_PAIRS = SSD_HEADS // 2


def _ssd_common(conv, dt_raw, dtb, alog):
    sg = _sigmoid(conv)
    xa = conv * sg
    dtv = _softplus(dt_raw + dtb)
    a_neg = -jnp.exp(alog)
    rowi = lax.broadcasted_iota(jnp.int32, (L, L), 0)
    coli = lax.broadcasted_iota(jnp.int32, (L, L), 1)
    tril = (rowi >= coli).astype(F32)
    cs = _dot(tril, dtv * a_neg, precision=HI)
    return conv, sg, xa, dtv, a_neg, cs, rowi, coli


def _head_masks():
    lane = lax.broadcasted_iota(jnp.int32, (L, LANES), 1)
    return lane < SSD_HEAD_DIM


def _spread(v, p, first):
    return jnp.where(first[:v.shape[0]], v[:, 2 * p:2 * p + 1], v[:, 2 * p + 1:2 * p + 2])


def _place_head_sums(acc, z, p, first, lane1):
    rows = z.shape[0]
    s0 = jnp.sum(jnp.where(first[:rows], z, 0.0), axis=1, keepdims=True)
    s1 = jnp.sum(jnp.where(first[:rows], 0.0, z), axis=1, keepdims=True)
    lane = lane1[:rows]
    return acc + jnp.where(lane == 2 * p, s0, 0.0) + jnp.where(lane == 2 * p + 1, s1, 0.0)


def _stack_heads(v, first):
    return jnp.concatenate([jnp.where(first, v, 0.0), jnp.where(first, 0.0, v)], axis=0).astype(BF16)


def ssd_fwd(pm, dtr, cw, cb, dtb, alog, d_lanes, nw, w_pb, ya, gates, b_gate, w_out, *, name, bg=None):
    bsz, seq, _ = pm.shape
    nc = seq // L
    inner, cdim, d = SSD_INNER, SSD_CONV_DIM, D_MODEL

    def body(xbc_ref, xp_ref, z_ref, dt_ref, cw_ref, cb_ref, dtb_ref, alog_ref, dl_ref, nw_ref, wpb_ref,
             ya_ref, g_ref, bg_ref, wout_ref,
             y_ref, yn_ref, st_ref, yb_ref, conv_ref, mg_ref, out_ref, state):
        i = pl.program_id(1)

        @pl.when(i == 0)
        def _():
            state[...] = jnp.zeros_like(state)

        prev8 = jnp.where(i == 0, 0.0, xp_ref[...])
        conv = _conv4(xbc_ref[...], prev8, cw_ref[...], cb_ref[...])
        conv_ref[...] = conv
        _, _, xa, dtv, _, cs, rowi, coli = _ssd_common(conv, dt_ref[...], dtb_ref[...], alog_ref[...])
        cst = cs.T
        causal = rowi >= coli
        first = _head_masks()
        for g in range(SSD_GROUPS):
            bg = xa[:, inner + SSD_STATE * g:inner + SSD_STATE * (g + 1)].astype(BF16)
            cg = xa[:, inner + SSD_GROUPS * SSD_STATE + SSD_STATE * g:
                    inner + SSD_GROUPS * SSD_STATE + SSD_STATE * (g + 1)].astype(BF16)
            cbm = _dot_nt(cg, bg)
            for pp in range(2):
                p = 2 * g + pp
                sl = slice(LANES * p, LANES * (p + 1))
                ms = []
                for hh in (2 * p, 2 * p + 1):
                    seg = cs[:, hh:hh + 1] - cst[hh:hh + 1, :]
                    ms.append((cbm * jnp.exp(jnp.where(causal, seg, -jnp.inf))).astype(BF16))
                xsp = xa[:, sl]
                cs_p = _spread(cs, p, first)
                cs_last = cs_p[L - 1:L]
                xp = xsp * _spread(dtv, p, first)
                y_diag = _dot(jnp.concatenate(ms, axis=1), _stack_heads(xp, first))
                st = state[p]
                st_ref[p] = st
                y_off = _dot(cg, st.astype(BF16)) * jnp.exp(cs_p)
                y_ref[:, sl] = y_diag + y_off + dl_ref[:, sl] * xsp
                state[p] = st * jnp.exp(cs_last) + _dot_tn(bg, (xp * jnp.exp(cs_last - cs_p)).astype(BF16))
        zv = z_ref[...]
        yz = y_ref[...] * (zv * _sigmoid(zv))
        gw = inner // SSD_GROUPS
        for g in range(SSD_GROUPS):
            sl = slice(gw * g, gw * (g + 1))
            seg = yz[:, sl]
            yn_ref[:, sl] = ((seg * _rstd(seg)) * nw_ref[:, sl]).astype(BF16)
        yb = _dot(yn_ref[...], wpb_ref[...])
        yb_ref[...] = yb
        g = _sigmoid(g_ref[...] + bg_ref[...])
        mg_ref[...] = (g[:, :d] * ya_ref[...] + g[:, d:] * yb).astype(BF16)
        out_ref[...] = _dot(mg_ref[...], wout_ref[...])

    cvec = lambda w: pl.BlockSpec((1, w), lambda b, i: (0, 0))
    rows = lambda w: pl.BlockSpec((None, L, w), lambda b, i: (b, i, 0))
    outs, bg_outs = _run(
        body, (pm, pm, pm, dtr, cw, cb, dtb, alog, d_lanes, nw, w_pb, ya, gates, b_gate, w_out), bg, name=name,
        grid=(bsz, nc),
        in_specs=[pl.BlockSpec((None, L, cdim), lambda b, i: (b, i, 1)), _prev8_spec(cdim, 1, L),
                  pl.BlockSpec((None, L, inner), lambda b, i: (b, i, 1)),
                  pl.BlockSpec((None, L, LANES), lambda b, i: (b, i, 0)),
                  pl.BlockSpec((4, cdim), lambda b, i: (0, 0)), cvec(cdim), cvec(LANES), cvec(LANES),
                  cvec(inner), cvec(inner), pl.BlockSpec(w_pb.shape, lambda b, i: (0, 0)),
                  rows(d), rows(2 * d), cvec(2 * d), pl.BlockSpec(w_out.shape, lambda b, i: (0, 0))],
        out_specs=[rows(inner), rows(inner),
                   pl.BlockSpec((None, None, N_PAIRS, SSD_STATE, LANES), lambda b, i: (b, i, 0, 0, 0)),
                   rows(d), rows(cdim), rows(d), rows(d)],
        out_shape=[jax.ShapeDtypeStruct((bsz, seq, inner), F32), jax.ShapeDtypeStruct((bsz, seq, inner), BF16),
                   jax.ShapeDtypeStruct((bsz, nc, N_PAIRS, SSD_STATE, LANES), F32),
                   jax.ShapeDtypeStruct((bsz, seq, d), F32), jax.ShapeDtypeStruct((bsz, seq, cdim), F32),
                   jax.ShapeDtypeStruct((bsz, seq, d), BF16), jax.ShapeDtypeStruct((bsz, seq, d), F32)],
        scratch_shapes=[pltpu.VMEM((N_PAIRS, SSD_STATE, LANES), F32)])
    return outs, bg_outs


def ssd_bwd(pm, conv, dtr, y, states, d_yb, cw, dtb, alog, d_lanes, nw, wt_pb, wt_ssd, *, name):
    bsz, seq, _ = pm.shape
    nc = seq // L
    inner, cdim = SSD_INNER, SSD_CONV_DIM
    per = L // SUBLANES

    def rev(i):
        return nc - 1 - i

    def body(xbc_ref, conv_ref, z_ref, dt_ref, y_ref, st_ref, dyb_ref, cw_ref, dtb_ref, alog_ref,
             dl_ref, nw_ref, wtpb_ref, wts_ref, ds_ref, dh1_ref, ddt_ref, r4_ref, r2_ref, r1_ref,
             dstate, dconv_next, dxs_s, dbc_s):
        b, i = pl.program_id(0), pl.program_id(1)
        t = rev(i)

        @pl.when(i == 0)
        def _():
            dstate[...] = jnp.zeros_like(dstate)
            dconv_next[...] = jnp.zeros_like(dconv_next)

        @pl.when((b == 0) & (i == 0))
        def _():
            r4_ref[...] = jnp.zeros_like(r4_ref)
            r2_ref[...] = jnp.zeros_like(r2_ref)
            r1_ref[...] = jnp.zeros_like(r1_ref)

        xbc = xbc_ref[...]
        cwv = cw_ref[...]
        dt_in = dt_ref[...] + dtb_ref[...]
        conv = conv_ref[...]
        _, sg, xa, dtv, a_neg, cs, rowi, coli = _ssd_common(conv, dt_ref[...], dtb_ref[...], alog_ref[...])
        cst = cs.T
        causal = rowi >= coli
        anti = coli >= rowi
        first = _head_masks()
        lane1 = lax.broadcasted_iota(jnp.int32, (L, LANES), 1)

        yv = y_ref[...]
        zv = z_ref[...]
        sz = _sigmoid(zv)
        zs = zv * sz
        yz = yv * zs
        dyn = _dot(dyb_ref[...], wtpb_ref[...])
        gw = inner // SSD_GROUPS
        d_yz_parts, d_nw_parts = [], []
        for g in range(SSD_GROUPS):
            sl = slice(gw * g, gw * (g + 1))
            seg = yz[:, sl]
            r = _rstd(seg)
            n = seg * r
            d_nw_parts.append(_colsum(dyn[:, sl] * n))
            d_yz_parts.append(_rms_bwd(dyn[:, sl] * nw_ref[:, sl], n, r))
        d_yz = jnp.concatenate(d_yz_parts, axis=1)
        d_y = d_yz * zs
        ds_ref[:, :inner] = (d_yz * yv * (sz * (1.0 + zv * (1.0 - sz)))).astype(BF16)

        a1 = jnp.zeros((L, LANES), F32)
        a2 = jnp.zeros((L, LANES), F32)
        xs_dxt = jnp.zeros((L, LANES), F32)
        c0 = jnp.zeros((1, LANES), F32)
        d_dl = jnp.zeros((1, LANES), F32)
        for g in range(SSD_GROUPS):
            bsl = slice(inner + SSD_STATE * g, inner + SSD_STATE * (g + 1))
            csl = slice(inner + SSD_GROUPS * SSD_STATE + SSD_STATE * g,
                        inner + SSD_GROUPS * SSD_STATE + SSD_STATE * (g + 1))
            bg = xa[:, bsl].astype(BF16)
            cg = xa[:, csl].astype(BF16)
            cbm = _dot_nt(cg, bg)
            cbt = _dot_nt(bg, cg)
            d_cb = jnp.zeros((L, L), F32)
            d_bg = jnp.zeros((L, SSD_STATE), F32)
            d_cg = jnp.zeros((L, SSD_STATE), F32)
            for pp in range(2):
                p = 2 * g + pp
                sl = slice(LANES * p, LANES * (p + 1))
                xsp = xa[:, sl]
                dt_p = _spread(dtv, p, first)
                cs_p = _spread(cs, p, first)
                cs_last = cs_p[L - 1:L]
                e_p = jnp.exp(cs_p)
                w_p = jnp.exp(cs_last - cs_p)
                e_last = jnp.exp(cs_last)
                xp = xsp * dt_p
                xpb = xp.astype(BF16)
                dyp = d_y[:, sl]
                dypb = dyp.astype(BF16)
                dy_heads = (jnp.where(first, dyp, 0.0).astype(BF16), jnp.where(first, 0.0, dyp).astype(BF16))
                x_heads = (jnp.where(first, xp, 0.0).astype(BF16), jnp.where(first, 0.0, xp).astype(BF16))
                mts = []
                for k, hh in enumerate((2 * p, 2 * p + 1)):
                    col = cs[:, hh:hh + 1]
                    row = cst[hh:hh + 1, :]
                    dec = jnp.exp(jnp.where(causal, col - row, -jnp.inf))
                    dec_t = jnp.exp(jnp.where(anti, row - col, -jnp.inf))
                    gd = _dot_nt(dy_heads[k], xpb) * dec
                    d_cb = d_cb + gd
                    mt = cbt * dec_t
                    qd = gd * cbm - _dot_nt(x_heads[k], dypb) * mt
                    a1 = a1 + jnp.where(lane1 == hh, jnp.sum(qd, axis=1, keepdims=True), 0.0)
                    mts.append(mt.astype(BF16))
                dst = dstate[p]
                dstb = dst.astype(BF16)
                st = st_ref[p]
                stb = st.astype(BF16)
                dye = (dyp * e_p).astype(BF16)
                xw = (xp * w_p).astype(BF16)
                dx_off = w_p * _dot(bg, dstb)
                d_xp = _dot(jnp.concatenate(mts, axis=1), jnp.concatenate(dy_heads, axis=0)) + dx_off
                dxs_s[:, sl] = d_xp * dt_p + dyp * dl_ref[:, sl]
                a1 = _place_head_sums(a1, dyp * (_dot(cg, stb) * e_p), p, first, lane1)
                a2 = _place_head_sums(a2, xp * dx_off, p, first, lane1)
                xs_dxt = _place_head_sums(xs_dxt, d_xp * xsp, p, first, lane1)
                c0 = _place_head_sums(c0, _colsum(dst * st) * e_last, p, first, lane1)
                d_dl = _place_head_sums(d_dl, _colsum(dyp * xsp), p, first, lane1)
                d_cg = d_cg + _dot_nt(dye, stb)
                d_bg = d_bg + _dot_nt(xw, dstb)
                dstate[p] = dst * e_last + _dot_tn(cg, dye)
            d_cbb = d_cb.astype(BF16)
            dbc_s[:, SSD_STATE * g:SSD_STATE * (g + 1)] = d_bg + _dot_tn(d_cbb, cg)
            dbc_s[:, SSD_GROUPS * SSD_STATE + SSD_STATE * g:SSD_GROUPS * SSD_STATE + SSD_STATE * (g + 1)] = (
                d_cg + _dot(d_cbb, bg))

        d_da = (_dot(anti.astype(F32), a1, precision=HI) + _dot((rowi > coli).astype(F32), a2, precision=HI) + c0)
        d_dt = d_da * a_neg + xs_dxt
        d_alog = _colsum(d_da * dtv) * a_neg
        d_dtr = jnp.where(lane1 < SSD_HEADS, d_dt * _sigmoid(dt_in), 0.0)
        ddt_ref[...] = d_dtr.astype(BF16)
        d_xa = jnp.concatenate([dxs_s[...], dbc_s[...]], axis=1)
        d_conv = d_xa * (sg * (1.0 + conv * (1.0 - sg)))
        d_xbc, d_cw, d_cbias = _conv4_bwd(d_conv, dconv_next[...], xbc, cwv)
        dconv_next[...] = d_conv[0:SUBLANES]
        ds_ref[:, inner:] = d_xbc.astype(BF16)
        dh1_ref[...] = _dot(ds_ref[...], wts_ref[...])
        r4_ref[...] += _stack_rows([d_cbias] + d_cw, cdim)
        r2_ref[...] += _stack_rows([jnp.concatenate(d_nw_parts, axis=1)], inner)
        r1_ref[...] += _stack_rows([_colsum(d_dtr), d_alog, d_dl], LANES)

    cvec = lambda w: pl.BlockSpec((1, w), lambda b, i: (0, 0))
    return _pcall(
        body, name=name, grid=(bsz, nc),
        in_specs=[pl.BlockSpec((None, L, cdim), lambda b, i: (b, rev(i), 1)),
                  pl.BlockSpec((None, L, cdim), lambda b, i: (b, rev(i), 0)),
                  pl.BlockSpec((None, L, inner), lambda b, i: (b, rev(i), 1)),
                  pl.BlockSpec((None, L, LANES), lambda b, i: (b, rev(i), 0)),
                  pl.BlockSpec((None, L, inner), lambda b, i: (b, rev(i), 0)),
                  pl.BlockSpec((None, None, N_PAIRS, SSD_STATE, LANES), lambda b, i: (b, rev(i), 0, 0, 0)),
                  pl.BlockSpec((None, L, D_MODEL), lambda b, i: (b, rev(i), 0)),
                  pl.BlockSpec((4, cdim), lambda b, i: (0, 0)), cvec(LANES), cvec(LANES),
                  cvec(inner), cvec(inner), pl.BlockSpec(wt_pb.shape, lambda b, i: (0, 0)),
                  pl.BlockSpec(wt_ssd.shape, lambda b, i: (0, 0))],
        out_specs=[pl.BlockSpec((None, L, inner + cdim), lambda b, i: (b, rev(i), 0)),
                   pl.BlockSpec((None, L, D_MODEL), lambda b, i: (b, rev(i), 0)),
                   pl.BlockSpec((None, L, LANES), lambda b, i: (b, rev(i), 0)),
                   pl.BlockSpec((SUBLANES, cdim), lambda b, i: (0, 0)),
                   pl.BlockSpec((SUBLANES, inner), lambda b, i: (0, 0)),
                   pl.BlockSpec((SUBLANES, LANES), lambda b, i: (0, 0))],
        out_shape=[jax.ShapeDtypeStruct((bsz, seq, inner + cdim), BF16),
                   jax.ShapeDtypeStruct((bsz, seq, D_MODEL), F32),
                   jax.ShapeDtypeStruct((bsz, seq, LANES), BF16),
                   jax.ShapeDtypeStruct((SUBLANES, cdim), F32),
                   jax.ShapeDtypeStruct((SUBLANES, inner), F32),
                   jax.ShapeDtypeStruct((SUBLANES, LANES), F32)],
        scratch_shapes=[pltpu.VMEM((N_PAIRS, SSD_STATE, LANES), F32), pltpu.VMEM((SUBLANES, cdim), F32),
                        pltpu.VMEM((L, inner), F32), pltpu.VMEM((L, 2 * SSD_GROUPS * SSD_STATE), F32)],
    )(pm, conv, pm, dtr, y, states, d_yb, cw, dtb, alog, d_lanes, nw, wt_pb, wt_ssd)


def _lru_block_weights(w):
    w = w.reshape(N_LRU_BLOCKS, 2, LRU_HEAD_DIM, LRU_HEAD_DIM)
    z = jnp.zeros((N_LRU_BLOCKS, LRU_HEAD_DIM, LRU_HEAD_DIM), w.dtype)
    top = jnp.concatenate([w[:, 0], z], axis=2)
    bot = jnp.concatenate([z, w[:, 1]], axis=2)
    return jnp.concatenate([top, bot], axis=1).astype(BF16)


def _lru_block_grads(g):
    h = LRU_HEAD_DIM
    return jnp.stack([g[:, :h, :h], g[:, h:, h:]], axis=1).reshape(LRU_HEADS, h, h)


def _pad_lanes(v, width=LANES):
    return jnp.pad(v, ((0, 0), (0, width - v.shape[1])))


class NoExchange:
    def __init__(self, weights):
        self._weights, self.grads = weights, {}

    def weights_bg(self):
        return None

    def weights(self, bg_outs):
        return self._weights

    def grads_bg(self, grads):
        self.grads.update(grads)
        return None

    def grads_done(self, bg_outs):
        pass


def local_step(x, target, mod, big, small, plan):
    bsz, seq, d = x.shape
    t = bsz * seq
    flat = lambda v: v.reshape(t, v.shape[-1])
    unflat = lambda v: v.reshape(bsz, seq, v.shape[-1])

    wa_b = _lru_block_weights(small["lru_wa"])
    wx_b = _lru_block_weights(small["lru_wx"])
    dtb = _pad_lanes(small["ssd_dt_bias"])
    alog = _pad_lanes(small["ssd_a_log"])
    d_lanes = jnp.repeat(small["ssd_d"], SSD_HEAD_DIM, axis=1)

    lru_cols = 2 * D_MODEL
    wt = {"lru": big["w_main"][:, :lru_cols].T, "ssd": big["w_main"][:, lru_cols:].T, "gates": big["w_gates"].T,
          "dt": big["w_dt"].T}

    h1 = prenorm(x, small["pre_norm1"], mod, name="prenorm1")
    h1f = flat(h1)
    arriving = plan.weights_bg()
    if arriving is None:
        pm, arrived = mm_nn([(h1f, big["w_main"])], name="in_proj_main"), []
    else:
        pm, arrived = mm_nn([(h1f, big["w_main"])], name="in_proj_main", bg=arriving)
    pm = unflat(pm)
    big = dict(big, **plan.weights(arrived))
    for n in ("w_pa", "w_pb", "w_out", "w_ff1", "w_ff2"):
        wt[n] = big[n].T
    gates = unflat(mm_nn([(h1f, big["w_gates"])], name="in_proj_gates"))
    dtr = unflat(mm_nn([(h1f, big["w_dt"])], name="in_proj_dt"))
    lru_args = (small["lru_conv_w"], small["lru_conv_b"], wa_b, small["lru_ba"], wx_b, small["lru_bx"],
                small["lru_lambda"])
    h_lru, pa_in, ya, lru_kept = lru_fwd(pm, *lru_args, big["w_pa"], name="lru_fwd")
    ssd_args = (small["ssd_conv_w"], small["ssd_conv_b"], dtb, alog, d_lanes, small["ssd_norm_w"])
    (y_ssd, ynorm, states, yb, conv_ssd, merged, out1), _ = ssd_fwd(
        pm, dtr, *ssd_args, big["w_pb"], ya, gates, small["b_gate"], big["w_out"], name="ssd_fwd")
    x1, h2 = post1_pre2(x, out1, mod, small["post_norm1"], small["pre_norm2"], name="post1_pre2")
    f = mm_nn([(flat(h2), big["w_ff1"])], name="ff1")
    y2 = unflat(mm_nn([(f, big["w_ff2"])], a_fn=_relu_sq, name="ff2"))

    dx2, d_y2, pb_a, gl_a = final_bwd(x1, y2, target, mod, small["post_norm2"], name="final_bwd")
    d_y2f = flat(d_y2)
    d_f = mm_nn([(d_y2f, wt["w_ff2"])], out_dtype=BF16, extra=f,
                epi=lambda r, fv: r * (2.0 * jnp.maximum(fv, 0.0)), name="ff2_dx")
    g_ff2 = mm_tn(f, d_y2f, a_fn=_relu_sq, name="ff2_dw")
    d_h2 = unflat(mm_nn([(d_f, wt["w_ff1"])], name="ff1_dx"))
    g_ff1 = mm_tn(flat(h2), d_f, name="ff1_dw")
    dx1, d_out1, pb_b, gl_b = mid_bwd(d_h2, dx2, x1, out1, mod, small["pre_norm2"], small["post_norm1"],
                                      name="mid_bwd")
    d_out1f = flat(d_out1)
    d_merged = unflat(mm_nn([(d_out1f, wt["w_out"])], name="out_dx"))
    g_out = mm_tn(flat(merged), d_out1f, name="out_dw")
    d_ya, d_yb, d_gates, gl_c = merge_bwd(d_merged, ya, yb, gates, small["b_gate"], name="merge_bwd")
    g_pa = mm_tn(flat(pa_in), flat(d_ya), name="pa_dw")
    g_pb = mm_tn(flat(ynorm), flat(d_yb), name="pb_dw")
    leaving = plan.grads_bg({"w_pa": g_pa, "w_pb": g_pb, "w_out": g_out, "w_ff1": g_ff1, "w_ff2": g_ff2})
    (d_l, dh_lru, g_wa_b, g_wx_b, lru_rows), landed = lru_bwd(
        pm, h_lru, lru_kept, d_ya, small["lru_conv_w"], wa_b, wx_b, small["lru_lambda"], wt["w_pa"], wt["lru"],
        name="lru_bwd", bg=leaving)
    plan.grads_done(landed)
    d_s, dh_ssd, d_dt, r4, r2, r1 = ssd_bwd(pm, conv_ssd, dtr, y_ssd, states, d_yb, small["ssd_conv_w"], dtb, alog,
                                          d_lanes, small["ssd_norm_w"], wt["w_pb"], wt["ssd"], name="ssd_bwd")
    d_lf, d_sf, d_gf, d_dtf = flat(d_l), flat(d_s), flat(d_gates), flat(d_dt)
    g_in = jnp.concatenate([
        mm_tn(h1f, d_lf, name="in_dw_lru"), mm_tn(h1f, d_sf, name="in_dw_ssd"),
        mm_tn(h1f, d_dtf, name="in_dw_dt")[:, :SSD_HEADS], mm_tn(h1f, d_gf, name="in_dw_gates")], axis=1)
    leaving = plan.grads_bg({"w_in": g_in})
    (grad_x, pb_c, gl_d), landed = in_dx_first_bwd(
        d_gates, d_dt, wt["gates"], wt["dt"], [dh_lru, dh_ssd], dx1, x, mod, small["pre_norm1"],
        name="in_dx_first_bwd", bg=leaving)
    plan.grads_done(landed)

    d_mod = jnp.stack([pb_c[:, 0], pb_c[:, 1], pb_b[:, 2], pb_b[:, 0], pb_b[:, 1], pb_a[:, 0]], axis=1)
    loss_cols = gl_a[1:2]
    nh = SSD_HEADS
    small_grads = {
        "pre_norm1": gl_d[0:1], "post_norm1": gl_b[1:2], "b_gate": gl_c[0:1],
        "lru_conv_w": lru_rows[4:8], "lru_conv_b": lru_rows[3:4],
        "lru_wa": _lru_block_grads(g_wa_b), "lru_ba": lru_rows[0:1],
        "lru_wx": _lru_block_grads(g_wx_b), "lru_bx": lru_rows[1:2], "lru_lambda": lru_rows[2:3],
        "ssd_conv_w": r4[1:5], "ssd_conv_b": r4[0:1],
        "ssd_dt_bias": r1[0:1, :nh], "ssd_a_log": r1[1:2, :nh], "ssd_d": r1[2:3, :nh],
        "ssd_norm_w": r2[0:1], "pre_norm2": gl_b[0:1], "post_norm2": gl_a[0:1],
    }
    return loss_cols, grad_x, d_mod, small_grads


def _position():
    return lax.axis_index("x"), lax.axis_index("y"), lax.axis_index("c")


def _other_chips(x, y):
    return [(1 - x, y), (x, 1 - y), (1 - x, 1 - y)]


def allgather8(v, *, name):
    m_per, n = v.shape

    def body(x_ref, out_ref, send_sems, recv_sems, local_sem):
        x, y, c = _position()
        me, sibling = (x, y, c), (x, y, 1 - c)
        chips = _other_chips(x, y)

        def rows(px, py, pc):
            return out_ref.at[pl.ds((4 * px + 2 * py + pc) * m_per, m_per), :]

        def copy(k, block, to, src=None):
            return pltpu.make_async_remote_copy(
                src_ref=rows(*block) if src is None else src, dst_ref=rows(*block),
                send_sem=send_sems.at[k], recv_sem=recv_sems.at[k], device_id=to, device_id_type=MESH)

        mine = pltpu.make_async_copy(x_ref, rows(*me), local_sem)
        mine.start()
        first = [copy(0, me, sibling, src=x_ref)]
        first += [copy(1 + j, me, (*chip, c), src=x_ref) for j, chip in enumerate(chips)]
        for cp in first:
            cp.start()
        passed = [copy(4 + j, (*chip, c), sibling) for j, chip in enumerate(chips)]
        for j, chip in enumerate(chips):
            copy(1 + j, (*chip, c), me).wait_recv()
            passed[j].start()
        copy(0, sibling, me).wait_recv()
        for j, chip in enumerate(chips):
            copy(4 + j, (*chip, 1 - c), me).wait_recv()
        for cp in first + passed:
            cp.wait_send()
        mine.wait()

    return _pcall(
        body, name=name,
        out_shape=jax.ShapeDtypeStruct((N_DEV * m_per, n), v.dtype),
        in_specs=[pl.BlockSpec(memory_space=pltpu.VMEM)],
        out_specs=pl.BlockSpec(memory_space=pltpu.VMEM),
        scratch_shapes=[pltpu.SemaphoreType.DMA((7,)), pltpu.SemaphoreType.DMA((7,)), pltpu.SemaphoreType.DMA],
    )(v)


def gather_weights(shards, *, name):
    n = len(shards)
    half = [s.shape[0] // 2 for s in shards]
    widths = sorted({s.shape[1] for s in shards})
    chunk_rows = [_stage_rows(h, s.shape[1], itemsize=s.dtype.itemsize) for s, h in zip(shards, half)]
    plan = [(w, j, r0) for w in range(n) for j in range(N_CHIPS - 1) for r0 in range(0, half[w], chunk_rows[w])]

    def body(*refs):
        ins, outs = refs[:n], refs[n:2 * n]
        send_sems, recv_sems, local_sems, passed_sems = refs[2 * n:2 * n + 4]
        stage = refs[2 * n + 4:]
        bufs = {wd: stage[4 * i] for i, wd in enumerate(widths)}
        load_sems = {wd: stage[4 * i + 1] for i, wd in enumerate(widths)}
        stage_send = {wd: stage[4 * i + 2] for i, wd in enumerate(widths)}
        x, y, c = _position()
        me_chip = 2 * x + y
        chips = _other_chips(x, y)

        def piece(w, chip, core):
            return outs[w].at[chip, pl.ds(core * half[w], half[w]), :]

        def over_ici(w, j, chip, src=None):
            px, py = chips[j]
            dst = piece(w, chip, c)
            return pltpu.make_async_remote_copy(
                src_ref=dst if src is None else src, dst_ref=dst, send_sem=send_sems.at[3 * w + j],
                recv_sem=recv_sems.at[3 * w + j], device_id=(px, py, c), device_id_type=MESH)

        local = [pltpu.make_async_copy(ins[w], outs[w].at[me_chip], local_sems.at[w]) for w in range(n)]
        for cp in local:
            cp.start()
        sent = []
        for w in range(n):
            for j in range(N_CHIPS - 1):
                cp = over_ici(w, j, me_chip, src=ins[w].at[pl.ds(c * half[w], half[w]), :])
                cp.start()
                sent.append(cp)
        chunks = []
        for idx, (w, j, r0) in enumerate(plan):
            wd, rb = shards[w].shape[1], chunk_rows[w]
            k = 2 * chips[j][0] + chips[j][1]

            def make(staged, slot, idx=idx, w=w, k=k, r0=r0, wd=wd, rb=rb):
                return pltpu.make_async_remote_copy(
                    src_ref=staged, dst_ref=outs[w].at[k, pl.ds(c * half[w] + r0, rb), :],
                    send_sem=stage_send[wd].at[slot], recv_sem=passed_sems.at[idx],
                    device_id=(x, y, 1 - c), device_id_type=MESH), True

            chunk = (wd, outs[w].at[k, pl.ds(c * half[w] + r0, rb), :], [make])
            if r0 == 0:
                chunk += (lambda w=w, j=j, k=k: over_ici(w, j, k).wait_recv(),)
            chunks.append(chunk)
        _staged(chunks, bufs, load_sems)
        for idx, (w, j, r0) in enumerate(plan):
            wd = shards[w].shape[1]
            k = 2 * chips[j][0] + chips[j][1]
            landed = outs[w].at[k, pl.ds((1 - c) * half[w] + r0, chunk_rows[w]), :]
            pltpu.make_async_remote_copy(
                src_ref=landed, dst_ref=landed, send_sem=stage_send[wd].at[0], recv_sem=passed_sems.at[idx],
                device_id=(x, y, 1 - c), device_id_type=MESH).wait_recv()
        for cp in sent:
            cp.wait_send()
        for cp in local:
            cp.wait()

    stage_rows = [(wd, max(r for s, r in zip(shards, chunk_rows) if s.shape[1] == wd)) for wd in widths]
    return _pcall(
        body, name=name,
        out_shape=[jax.ShapeDtypeStruct((N_CHIPS,) + s.shape, s.dtype) for s in shards],
        in_specs=[ANY] * n, out_specs=[ANY] * n,
        scratch_shapes=[pltpu.SemaphoreType.DMA((3 * n,)), pltpu.SemaphoreType.DMA((3 * n,)),
                        pltpu.SemaphoreType.DMA((n,)), pltpu.SemaphoreType.DMA((len(plan),))]
        + _stage_scratch(stage_rows, shards[0].dtype),
    )(*shards)


STAGE_BYTES = 2 << 20


def _stage_rows(rows, width, itemsize=4):
    return _pick(rows, tuple(t for t in (1024, 512, 256, 128, 64, 32, 16, 8) if t * width * itemsize <= STAGE_BYTES * 3 // 2))


def _staged(chunks, bufs, load_sems):
    count, pending = {}, {}

    def load(i):
        cls, src = chunks[i][0], chunks[i][1]
        if len(chunks[i]) > 3:
            chunks[i][3]()
        slot = count.get(cls, 0) % 2
        count[cls] = count.get(cls, 0) + 1
        for cp, remote in pending.pop((cls, slot), []):
            if remote:
                cp.wait_send()
            else:
                cp.wait()
        staged = bufs[cls].at[slot, pl.ds(0, src.shape[0]), :]
        ld = pltpu.make_async_copy(src, staged, load_sems[cls].at[slot])
        ld.start()
        return ld, cls, slot, staged

    cur = load(0)
    for i in range(len(chunks)):
        nxt = load(i + 1) if i + 1 < len(chunks) else None
        ld, cls, slot, staged = cur
        ld.wait()
        started = []
        for make in chunks[i][2]:
            cp, remote = make(staged, slot)
            cp.start()
            started.append((cp, remote))
        pending[(cls, slot)] = started
        cur = nxt
    for started in pending.values():
        for cp, remote in started:
            if remote:
                cp.wait_send()
            else:
                cp.wait()


def _stage_scratch(widths_rows, dtype):
    scratch = []
    for width, rows in widths_rows:
        scratch += [pltpu.VMEM((2, rows, width), dtype), pltpu.SemaphoreType.DMA((2,)), pltpu.SemaphoreType.DMA((2,)),
                    pltpu.SemaphoreType.DMA((2,))]
    return scratch


def send_half_to_sibling(grads, *, name):
    n = len(grads)
    half = [g.shape[1] // 2 for g in grads]
    widths = sorted({g.shape[2] for g in grads})
    chunk_rows = [_stage_rows(h, g.shape[2]) for g, h in zip(grads, half)]
    plan = [(w, k, r0) for w in range(n) for k in range(N_CHIPS) for r0 in range(0, half[w], chunk_rows[w])]

    def body(*refs):
        ins, theirs = refs[:n], refs[n:2 * n]
        recv_sems = refs[2 * n]
        stage = refs[2 * n + 1:]
        bufs = {wd: stage[4 * i] for i, wd in enumerate(widths)}
        load_sems = {wd: stage[4 * i + 1] for i, wd in enumerate(widths)}
        send_sems = {wd: stage[4 * i + 2] for i, wd in enumerate(widths)}
        x, y, c = _position()
        chunks = []
        for idx, (w, k, r0) in enumerate(plan):
            wd = grads[w].shape[2]
            rb = chunk_rows[w]

            def make(staged, slot, idx=idx, w=w, k=k, r0=r0, wd=wd, rb=rb):
                return pltpu.make_async_remote_copy(
                    src_ref=staged, dst_ref=theirs[w].at[k, pl.ds(r0, rb), :], send_sem=send_sems[wd].at[slot],
                    recv_sem=recv_sems.at[idx], device_id=(x, y, 1 - c), device_id_type=MESH), True

            chunks.append((wd, ins[w].at[k, pl.ds((1 - c) * half[w] + r0, rb), :], [make]))
        _staged(chunks, bufs, load_sems)
        for idx, (w, k, r0) in enumerate(plan):
            wd = grads[w].shape[2]
            landed = theirs[w].at[k, pl.ds(r0, chunk_rows[w]), :]
            pltpu.make_async_remote_copy(
                src_ref=landed, dst_ref=landed, send_sem=send_sems[wd].at[0], recv_sem=recv_sems.at[idx],
                device_id=(x, y, 1 - c), device_id_type=MESH).wait_recv()

    stage_rows = [(wd, max(r for g, r in zip(grads, chunk_rows) if g.shape[2] == wd)) for wd in widths]
    return _pcall(
        body, name=name,
        out_shape=[jax.ShapeDtypeStruct((N_CHIPS, h, g.shape[2]), g.dtype) for g, h in zip(grads, half)],
        in_specs=[ANY] * n, out_specs=[ANY] * n,
        scratch_shapes=[pltpu.SemaphoreType.DMA((len(plan),))] + _stage_scratch(stage_rows, F32),
    )(*grads)


def _chip_exchange_background(arrays, out_shapes, src_of, dst_of, landed_of, own_of):
    n = len(arrays)

    def copies(ins, outs, scr):
        send_sems, recv_sems, local_sems = scr
        x, y, c = _position()
        me_chip = 2 * x + y
        local, sends, recvs = [], [], []
        for w in range(n):
            local.append(pltpu.make_async_copy(*own_of(ins[w], outs[w], w, me_chip), local_sems.at[w]))
            for j, (px, py) in enumerate(_other_chips(x, y)):
                sems = dict(send_sem=send_sems.at[3 * w + j], recv_sem=recv_sems.at[3 * w + j],
                            device_id=(px, py, c), device_id_type=MESH)
                sends.append(pltpu.make_async_remote_copy(
                    src_ref=src_of(ins[w], w, 2 * px + py, me_chip, c), dst_ref=dst_of(outs[w], w, me_chip, c), **sems))
                landed = landed_of(outs[w], w, 2 * px + py, c)
                recvs.append(pltpu.make_async_remote_copy(src_ref=landed, dst_ref=landed, **sems))
        return local, sends, recvs

    def start(ins, outs, scr):
        local, sends, _ = copies(ins, outs, scr)
        for cp in local + sends:
            cp.start()

    def finish(ins, outs, scr):
        local, sends, recvs = copies(ins, outs, scr)
        for cp in recvs:
            cp.wait_recv()
        for cp in sends:
            cp.wait_send()
        for cp in local:
            cp.wait()

    scratch = [pltpu.SemaphoreType.DMA((3 * n,)), pltpu.SemaphoreType.DMA((3 * n,)), pltpu.SemaphoreType.DMA((n,))]
    return Background(arrays, out_shapes, scratch, start, finish)


def scatter_background(parts):
    return _chip_exchange_background(
        parts, [jax.ShapeDtypeStruct(p.shape, p.dtype) for p in parts],
        src_of=lambda ref, w, peer, me, c: ref.at[peer], dst_of=lambda ref, w, me, c: ref.at[me],
        landed_of=lambda ref, w, peer, c: ref.at[peer], own_of=lambda i, o, w, me: (i.at[me], o.at[me]))


def gather_halves_background(shards):
    half = [s.shape[0] // 2 for s in shards]
    rows = lambda w, c: pl.ds(c * half[w], half[w])
    return _chip_exchange_background(
        shards, [jax.ShapeDtypeStruct((N_CHIPS,) + s.shape, s.dtype) for s in shards],
        src_of=lambda ref, w, peer, me, c: ref.at[rows(w, c), :], dst_of=lambda ref, w, me, c: ref.at[me, rows(w, c), :],
        landed_of=lambda ref, w, peer, c: ref.at[peer, rows(w, c), :], own_of=lambda i, o, w, me: (i, o.at[me]))


def fill_other_half(gathered, *, name):
    n = len(gathered)
    half = [g.shape[1] // 2 for g in gathered]
    widths = sorted({g.shape[2] for g in gathered})
    chunk_rows = [_stage_rows(h, g.shape[2], itemsize=2) for g, h in zip(gathered, half)]
    plan = [(w, j, r0) for w in range(n) for j in range(N_CHIPS - 1) for r0 in range(0, half[w], chunk_rows[w])]

    def body(*refs):
        ins, outs = refs[:n], refs[n:2 * n]
        recv_sems = refs[2 * n]
        stage = refs[2 * n + 1:]
        bufs = {wd: stage[4 * i] for i, wd in enumerate(widths)}
        load_sems = {wd: stage[4 * i + 1] for i, wd in enumerate(widths)}
        send_sems = {wd: stage[4 * i + 2] for i, wd in enumerate(widths)}
        x, y, c = _position()
        chips = _other_chips(x, y)
        chunks = []
        for idx, (w, j, r0) in enumerate(plan):
            wd, rb = gathered[w].shape[2], chunk_rows[w]
            k = 2 * chips[j][0] + chips[j][1]

            def make(staged, slot, idx=idx, w=w, k=k, r0=r0, wd=wd, rb=rb):
                return pltpu.make_async_remote_copy(
                    src_ref=staged, dst_ref=outs[w].at[k, pl.ds(c * half[w] + r0, rb), :],
                    send_sem=send_sems[wd].at[slot], recv_sem=recv_sems.at[idx],
                    device_id=(x, y, 1 - c), device_id_type=MESH), True

            chunks.append((wd, ins[w].at[k, pl.ds(c * half[w] + r0, rb), :], [make]))
        _staged(chunks, bufs, load_sems)
        for idx, (w, j, r0) in enumerate(plan):
            wd = gathered[w].shape[2]
            k = 2 * chips[j][0] + chips[j][1]
            landed = outs[w].at[k, pl.ds((1 - c) * half[w] + r0, chunk_rows[w]), :]
            pltpu.make_async_remote_copy(
                src_ref=landed, dst_ref=landed, send_sem=send_sems[wd].at[0], recv_sem=recv_sems.at[idx],
                device_id=(x, y, 1 - c), device_id_type=MESH).wait_recv()

    stage_rows = [(wd, max(r for g, r in zip(gathered, chunk_rows) if g.shape[2] == wd)) for wd in widths]
    return _pcall(
        body, name=name, out_shape=[jax.ShapeDtypeStruct(g.shape, g.dtype) for g in gathered],
        in_specs=[ANY] * n, out_specs=[ANY] * n, input_output_aliases={w: w for w in range(n)},
        scratch_shapes=[pltpu.SemaphoreType.DMA((len(plan),))] + _stage_scratch(stage_rows, gathered[0].dtype),
    )(*gathered)


def join_with_sibling(halves, *, name):
    n = len(halves)
    widths = sorted({h.shape[1] for h in halves})
    chunk_rows = [_stage_rows(h.shape[0], h.shape[1]) for h in halves]
    plan = [(w, r0) for w in range(n) for r0 in range(0, halves[w].shape[0], chunk_rows[w])]

    def body(*refs):
        ins, outs = refs[:n], refs[n:2 * n]
        recv_sems = refs[2 * n]
        stage = refs[2 * n + 1:]
        bufs = {wd: stage[4 * i] for i, wd in enumerate(widths)}
        load_sems = {wd: stage[4 * i + 1] for i, wd in enumerate(widths)}
        send_sems = {wd: stage[4 * i + 2] for i, wd in enumerate(widths)}
        store_sems = {wd: stage[4 * i + 3] for i, wd in enumerate(widths)}
        x, y, c = _position()
        chunks = []
        for idx, (w, r0) in enumerate(plan):
            h, wd = halves[w].shape
            rb = chunk_rows[w]

            def to_sibling(staged, slot, idx=idx, w=w, r0=r0, h=h, wd=wd, rb=rb):
                return pltpu.make_async_remote_copy(
                    src_ref=staged, dst_ref=outs[w].at[pl.ds(c * h + r0, rb), :], send_sem=send_sems[wd].at[slot],
                    recv_sem=recv_sems.at[idx], device_id=(x, y, 1 - c), device_id_type=MESH), True

            def to_mine(staged, slot, w=w, r0=r0, h=h, wd=wd, rb=rb):
                return pltpu.make_async_copy(staged, outs[w].at[pl.ds(c * h + r0, rb), :], store_sems[wd].at[slot]), False

            chunks.append((wd, ins[w].at[pl.ds(r0, rb), :], [to_sibling, to_mine]))
        _staged(chunks, bufs, load_sems)
        for idx, (w, r0) in enumerate(plan):
            h, wd = halves[w].shape
            landed = outs[w].at[pl.ds((1 - c) * h + r0, chunk_rows[w]), :]
            pltpu.make_async_remote_copy(
                src_ref=landed, dst_ref=landed, send_sem=send_sems[wd].at[0], recv_sem=recv_sems.at[idx],
                device_id=(x, y, 1 - c), device_id_type=MESH).wait_recv()

    stage_rows = [(wd, max(r for h, r in zip(halves, chunk_rows) if h.shape[1] == wd)) for wd in widths]
    return _pcall(
        body, name=name,
        out_shape=[jax.ShapeDtypeStruct((2 * h.shape[0], h.shape[1]), h.dtype) for h in halves],
        in_specs=[ANY] * n, out_specs=[ANY] * n,
        scratch_shapes=[pltpu.SemaphoreType.DMA((len(plan),))] + _stage_scratch(stage_rows, F32),
    )(*halves)


def _row_tile(rows, cols, itemsize=4, budget=2 << 20):
    for t in (1024, 512, 256, 128, 64, 32, 16, 8):
        if rows % t == 0 and t * cols * itemsize <= budget:
            return t
    return rows


def add_half_to_bf16(core, full, theirs, *, name):
    k, r, c = theirs.shape
    tr = _row_tile(r, c)
    nb = r // tr

    def body(core_ref, a_ref, b_ref, o_ref):
        o_ref[...] = (a_ref[...] + b_ref[...]).astype(BF16)

    spec = pl.BlockSpec((None, tr, c), lambda i, j, core_ref: (i, j, 0))
    grid_spec = pltpu.PrefetchScalarGridSpec(
        num_scalar_prefetch=1, grid=(k, nb),
        in_specs=[pl.BlockSpec((None, tr, c), lambda i, j, core_ref: (i, core_ref[0] * nb + j, 0)), spec],
        out_specs=spec)
    return _pcall(body, name=name, grid_spec=grid_spec,
                  out_shape=jax.ShapeDtypeStruct(theirs.shape, BF16))(core, full, theirs)


def sum_blocks(v, *, name):
    k, r, c = v.shape
    tr = _row_tile(r, c * k)

    def body(v_ref, o_ref):
        acc = v_ref[0].astype(F32)
        for j in range(1, k):
            acc = acc + v_ref[j].astype(F32)
        o_ref[...] = acc

    return _pcall(body, name=name, grid=(r // tr,),
                  in_specs=[pl.BlockSpec((k, tr, c), lambda i: (0, i, 0))],
                  out_specs=pl.BlockSpec((tr, c), lambda i: (i, 0)),
                  out_shape=jax.ShapeDtypeStruct((r, c), F32))(v)


def adamw(w, g, m, v, *, name):
    r, c = w.shape
    tr = _row_tile(r, c, budget=1 << 20)
    m_scale = 1.0 / (1.0 - ADAM_B1 ** ADAM_STEP)
    v_scale = 1.0 / (1.0 - ADAM_B2 ** ADAM_STEP)

    def body(w_ref, g_ref, m_ref, v_ref, d_ref, nm_ref, nv_ref):
        gv = g_ref[...]
        nm = ADAM_B1 * m_ref[...] + (1.0 - ADAM_B1) * gv
        nv = ADAM_B2 * v_ref[...] + (1.0 - ADAM_B2) * (gv * gv)
        nm_ref[...] = nm
        nv_ref[...] = nv
        d_ref[...] = -ADAM_LR * ((nm * m_scale) / (jnp.sqrt(nv * v_scale) + ADAM_EPS) + ADAM_WD * w_ref[...])

    spec = pl.BlockSpec((tr, c), lambda i: (i, 0))
    return _pcall(body, name=name, grid=(r // tr,), in_specs=[spec] * 4, out_specs=[spec] * 3,
                  out_shape=[jax.ShapeDtypeStruct((r, c), F32)] * 3)(w, g, m, v)


def ada_fwd(c_all, w_shard, b_shard, *, name):
    bsz, d = c_all.shape
    ncol = w_shard.shape[1]

    def body(c_ref, w_ref, b_ref, o_ref):
        cv = c_ref[...]
        act = (cv * _sigmoid(cv)).astype(BF16)
        o_ref[...] = _dot(act, w_ref[...].astype(BF16)) + b_ref[...]

    tn = _pick(ncol, (512, 256, 128))
    return _pcall(body, name=name, grid=(ncol // tn,),
                  in_specs=[pl.BlockSpec((bsz, d), lambda j: (0, 0)), pl.BlockSpec((d, tn), lambda j: (0, j)),
                            pl.BlockSpec((1, tn), lambda j: (0, j))],
                  out_specs=pl.BlockSpec((bsz, tn), lambda j: (0, j)),
                  out_shape=jax.ShapeDtypeStruct((bsz, ncol), F32))(c_all, w_shard, b_shard)


def ada_bwd(c_all, d_mod_all, d_mod_cols, *, name):
    bsz, d = c_all.shape
    ncol = d_mod_cols.shape[1]
    nall = d_mod_all.shape[1]

    def body(c_ref, da_ref, dc_ref, gw_ref, gb_ref):
        cv = c_ref[...]
        act = (cv * _sigmoid(cv)).astype(BF16)
        gw_ref[...] = _dot_tn(act, dc_ref[...].astype(BF16))
        gb_ref[...] = _colsum(da_ref[...])

    full = lambda s: pl.BlockSpec(s, lambda: (0,) * len(s))
    return _pcall(body, name=name,
                  in_specs=[full((bsz, d)), full((bsz, nall)), full((bsz, ncol))],
                  out_specs=[full((d, ncol)), full((1, nall))],
                  out_shape=[jax.ShapeDtypeStruct((d, ncol), F32), jax.ShapeDtypeStruct((1, nall), F32)],
                  )(c_all, d_mod_all, d_mod_cols)


WEIGHT_NAMES = ['w_ada', 'b_ada', 'pre_norm1', 'post_norm1', 'w_in', 'b_gate', 'lru_conv_w', 'lru_conv_b', 'lru_wa',
                'lru_ba', 'lru_wx', 'lru_bx', 'lru_lambda', 'w_pa', 'ssd_conv_w', 'ssd_conv_b', 'ssd_dt_bias',
                'ssd_a_log', 'ssd_d', 'ssd_norm_w', 'w_pb', 'w_out', 'pre_norm2', 'post_norm2', 'w_ff1', 'w_ff2']
BIG_NAMES = ['w_in', 'w_pa', 'w_pb', 'w_out', 'w_ff1', 'w_ff2']
COLUMN_SHARDED = ('w_in', 'w_ff1')
SMALL_NAMES = [n for n in WEIGHT_NAMES if n not in BIG_NAMES + ['w_ada', 'b_ada']]
SHARDED_SMALL = ('lru_conv_w', 'ssd_conv_w')
PACK_WIDTH = 1024


def _whole(name, gathered):
    if name in COLUMN_SHARDED:
        return jnp.transpose(gathered, (1, 0, 2)).reshape(gathered.shape[1], N_CHIPS * gathered.shape[2])
    return gathered.reshape(N_CHIPS * gathered.shape[1], gathered.shape[2])


def _by_chip(name, g):
    if name in COLUMN_SHARDED:
        return jnp.transpose(g.reshape(g.shape[0], N_CHIPS, g.shape[1] // N_CHIPS), (1, 0, 2))
    return g.reshape(N_CHIPS, g.shape[0] // N_CHIPS, g.shape[1])


class ChipExchange:
    def __init__(self, shards, core):
        self.shards, self.core = shards, core
        self.pending, self.halves = [], {}

    def weights_bg(self):
        return gather_halves_background(list(self.shards.values()))

    def weights(self, arrived):
        swapped = fill_other_half(arrived, name="weights_from_sibling")
        return {n: _whole(n, g) for n, g in zip(self.shards, swapped)}

    def grads_bg(self, grads):
        self.pending = list(grads)
        by_chip = [_by_chip(n, g) for n, g in grads.items()]
        theirs = send_half_to_sibling(by_chip, name="grads_to_sibling_" + self.pending[0])
        sums = [add_half_to_bf16(self.core, a, b, name="add_cores_" + n)
                for n, a, b in zip(self.pending, by_chip, theirs)]
        return scatter_background(sums)

    def grads_done(self, landed):
        for n, p in zip(self.pending, landed):
            self.halves[n] = sum_blocks(p, name="add_chips_" + n)

    def reduced(self):
        names = list(self.halves)
        return dict(zip(names, join_with_sibling([self.halves[n] for n in names], name="grads_join")))


def _pack(parts):
    flat = jnp.concatenate([p.reshape(-1).astype(F32) for p in parts])
    rows = -(-flat.shape[0] // (PACK_WIDTH * SUBLANES)) * SUBLANES
    return jnp.pad(flat, (0, rows * PACK_WIDTH - flat.shape[0])).reshape(rows, PACK_WIDTH)


def _unpack(packed, shapes):
    flat = packed.reshape(-1)
    out, pos = [], 0
    for s in shapes:
        size = int(np.prod(s))
        out.append(flat[pos:pos + size].reshape(s))
        pos += size
    return out


def kernel(x, c, w_ada, b_ada, pre_norm1, post_norm1, w_in, b_gate, lru_conv_w, lru_conv_b, lru_wa, lru_ba, lru_wx, lru_bx, lru_lambda, w_pa, ssd_conv_w, ssd_conv_b, ssd_dt_bias, ssd_a_log, ssd_d, ssd_norm_w, w_pb, w_out, pre_norm2, post_norm2, w_ff1, w_ff2, loss_target, m_w_ada, m_b_ada, m_pre_norm1, m_post_norm1, m_w_in, m_b_gate, m_lru_conv_w, m_lru_conv_b, m_lru_wa, m_lru_ba, m_lru_wx, m_lru_bx, m_lru_lambda, m_w_pa, m_ssd_conv_w, m_ssd_conv_b, m_ssd_dt_bias, m_ssd_a_log, m_ssd_d, m_ssd_norm_w, m_w_pb, m_w_out, m_pre_norm2, m_post_norm2, m_w_ff1, m_w_ff2, v_w_ada, v_b_ada, v_pre_norm1, v_post_norm1, v_w_in, v_b_gate, v_lru_conv_w, v_lru_conv_b, v_lru_wa, v_lru_ba, v_lru_wx, v_lru_bx, v_lru_lambda, v_w_pa, v_ssd_conv_w, v_ssd_conv_b, v_ssd_dt_bias, v_ssd_a_log, v_ssd_d, v_ssd_norm_w, v_w_pb, v_w_out, v_pre_norm2, v_post_norm2, v_w_ff1, v_w_ff2):
    given = dict(locals())
    bsz, seq, d = x.shape
    my_x, my_y, my_c = lax.axis_index("x"), lax.axis_index("y"), lax.axis_index("c")
    chip = 2 * my_x + my_y
    dev = 2 * chip + my_c
    strip = lambda a: a if a.ndim == 2 else a[0]
    w = {n: strip(given[n]) for n in WEIGHT_NAMES}
    m = {n: strip(given["m_" + n]) for n in WEIGHT_NAMES}
    v = {n: strip(given["v_" + n]) for n in WEIGHT_NAMES}

    first_shapes = [c.shape] + [w[n].shape for n in SHARDED_SMALL]
    first = allgather8(_pack([c] + [w[n] for n in SHARDED_SMALL]), name="gather_c_conv")
    first = first.reshape(N_DEV, -1, PACK_WIDTH)
    per_dev = [_unpack(first[k], first_shapes) for k in range(N_DEV)]
    c_all = jnp.concatenate([p[0] for p in per_dev], axis=0)
    conv_full = {n: jnp.concatenate([per_dev[2 * k][1 + i] for k in range(N_CHIPS)], axis=1)
                 for i, n in enumerate(SHARDED_SMALL)}

    ncol = w["w_ada"].shape[1]
    b_cols = lax.dynamic_slice(b_ada, (0, chip * ncol), (1, ncol))
    mod_cols = ada_fwd(c_all, w["w_ada"], b_cols, name="ada_fwd")
    mod_all = allgather8(mod_cols, name="gather_mod").reshape(N_CHIPS, 2, N_DEV * bsz, ncol)[:, 0]
    mod_all = jnp.transpose(mod_all, (1, 0, 2)).reshape(N_DEV * bsz, N_CHIPS * ncol)
    mod = lax.dynamic_slice(mod_all, (dev * bsz, 0), (bsz, 6 * d)).reshape(bsz, 6, d)
    mod = jnp.pad(mod, ((0, 0), (0, 2), (0, 0)))

    w_in_full = _whole("w_in", gather_weights([w["w_in"].astype(BF16)], name="gather_w_in")[0])
    big = {"w_main": w_in_full[:, :8192],
           "w_dt": jnp.pad(w_in_full[:, 8192:8192 + SSD_HEADS], ((0, 0), (0, LANES - SSD_HEADS))),
           "w_gates": w_in_full[:, 8192 + SSD_HEADS:]}
    small = {n: w[n] for n in SMALL_NAMES}
    small.update(conv_full)
    plan = ChipExchange({n: w[n].astype(BF16) for n in BIG_NAMES if n != "w_in"}, my_c.astype(jnp.int32).reshape(1))

    loss_cols, grad_x, d_mod, small_grads = local_step(x, loss_target, mod, big, small, plan)

    packed = _pack([d_mod, loss_cols] + [small_grads[n] for n in SMALL_NAMES])
    rows = packed.shape[0]
    everyone = allgather8(packed, name="gather_small").reshape(N_DEV, rows, PACK_WIDTH)
    d_mod_all = everyone[:, :bsz * 6].reshape(N_DEV * bsz, 6 * d)
    summed = sum_blocks(everyone, name="sum_small")
    shapes = [d_mod.shape, loss_cols.shape] + [small_grads[n].shape for n in SMALL_NAMES]
    parts = _unpack(summed, shapes)
    loss = jnp.sum(parts[1])
    grads = dict(zip(SMALL_NAMES, parts[2:]))
    for n in SHARDED_SMALL:
        cols = w[n].shape[1]
        grads[n] = lax.dynamic_slice(grads[n], (0, chip * cols), (grads[n].shape[0], cols))
    d_mod_cols = lax.dynamic_slice(d_mod_all, (0, chip * ncol), (N_DEV * bsz, ncol))
    grads["w_ada"], grads["b_ada"] = ada_bwd(c_all, d_mod_all, d_mod_cols, name="ada_bwd")

    grads.update(plan.reduced())

    delta, new_m, new_v = {}, {}, {}
    for n in BIG_NAMES + ["w_ada", "b_ada"]:
        delta[n], new_m[n], new_v[n] = adamw(w[n], grads[n], m[n], v[n], name="adamw_" + n)
    shapes = [w[n].shape for n in SMALL_NAMES]
    pk = lambda src: _pack([src[n] for n in SMALL_NAMES])
    upd = adamw(pk(w), pk(grads), pk(m), pk(v), name="adamw_small")
    for out, packed_out in zip((delta, new_m, new_v), upd):
        out.update(zip(SMALL_NAMES, _unpack(packed_out, shapes)))

    shaped = lambda src: [src[n].reshape(given[n].shape) for n in WEIGHT_NAMES]
    return (loss, grad_x, *shaped(grads), *shaped(delta), *shaped(new_m), *shaped(new_v))
```

```python
import functools
import math

import numpy as np
import jax
import jax.numpy as jnp
from jax import lax
from jax.experimental import pallas as pl
from jax.experimental.pallas import tpu as pltpu

F32 = jnp.float32
BF16 = jnp.bfloat16
HI = lax.Precision.HIGHEST
MESH = pl.DeviceIdType.MESH

D_MODEL = 1024
LRU_HEADS = 16
LRU_HEAD_DIM = 64
LRU_C = 8.0
SSD_INNER = 2048
SSD_HEADS = 32
SSD_HEAD_DIM = 64
SSD_GROUPS = 8
SSD_STATE = 128
SSD_CHUNK = 128
SSD_CONV_DIM = 4096
D_FF = 4096
EPS = 1e-6
N_CHIPS = 4
N_DEV = 8
LANES = 128
SUBLANES = 8

ADAM_LR = 0.001
ADAM_B1 = 0.9
ADAM_B2 = 0.999
ADAM_EPS = 1e-08
ADAM_WD = 0.01
ADAM_STEP = 10


ANY = pl.BlockSpec(memory_space=pl.ANY)


def _pcall(body, **kw):
    return pl.pallas_call(body, **kw)


class Background:
    def __init__(self, inputs, out_shapes, scratch, start, finish):
        self.inputs, self.out_shapes, self.scratch = list(inputs), list(out_shapes), list(scratch)
        self.start, self.finish = start, finish

    def wrap(self, body, kw):
        n_in, n_out = len(kw["in_specs"]), len(kw["out_specs"])
        n_scr = len(kw.get("scratch_shapes", []))
        b_in, b_out = len(self.inputs), len(self.out_shapes)
        grid = kw["grid"]

        def wrapped(*refs):
            ins, b_ins = refs[:n_in], refs[n_in:n_in + b_in]
            o0 = n_in + b_in
            outs, b_outs = refs[o0:o0 + n_out], refs[o0 + n_out:o0 + n_out + b_out]
            s0 = o0 + n_out + b_out
            scr, b_scr = refs[s0:s0 + n_scr], refs[s0 + n_scr:]
            ids = [pl.program_id(a) for a in range(len(grid))]
            first = functools.reduce(jnp.logical_and, [i == 0 for i in ids])
            last = functools.reduce(jnp.logical_and, [i == g - 1 for i, g in zip(ids, grid)])

            @pl.when(first)
            def _():
                self.start(b_ins, b_outs, b_scr)

            body(*ins, *outs, *scr)

            @pl.when(last)
            def _():
                self.finish(b_ins, b_outs, b_scr)

        kw = dict(kw, in_specs=list(kw["in_specs"]) + [ANY] * b_in, out_specs=list(kw["out_specs"]) + [ANY] * b_out,
                  out_shape=list(kw["out_shape"]) + self.out_shapes,
                  scratch_shapes=list(kw.get("scratch_shapes", [])) + self.scratch)
        return wrapped, kw


def _run(body, args, bg, **kw):
    n_out = len(kw["out_shape"])
    if bg is None:
        return list(_pcall(body, **kw)(*args)), []
    body, kw = bg.wrap(body, kw)
    outs = _pcall(body, **kw)(*args, *bg.inputs)
    return list(outs[:n_out]), list(outs[n_out:])


def _sigmoid(v):
    return 1.0 / (1.0 + jnp.exp(-v))


def _log1p(u):
    return jnp.where(u < 1e-3, u * (1.0 - u * (0.5 - u * (1.0 / 3.0))), jnp.log(1.0 + u))


def _softplus(v):
    return jnp.maximum(v, 0.0) + _log1p(jnp.exp(-jnp.abs(v)))


def _neg_expm1(v):
    small = -v * (1.0 + v * (0.5 + v * (1.0 / 6.0 + v * (1.0 / 24.0))))
    return jnp.where(v > -0.05, small, 1.0 - jnp.exp(v))


_GELU_K = math.sqrt(2.0 / math.pi)


def _gelu(v):
    t = jnp.tanh(_GELU_K * (v + 0.044715 * v * v * v))
    return 0.5 * v * (1.0 + t)


def _gelu_grad(v):
    t = jnp.tanh(_GELU_K * (v + 0.044715 * v * v * v))
    return 0.5 * (1.0 + t) + 0.5 * v * (1.0 - t * t) * _GELU_K * (1.0 + 3.0 * 0.044715 * v * v)


def _colsum(v):
    return jnp.sum(v, axis=0, keepdims=True)


def _dot(a, b, precision=None):
    return lax.dot_general(a, b, (((1,), (0,)), ((), ())), preferred_element_type=F32, precision=precision)


def _dot_nt(a, b):
    return lax.dot_general(a, b, (((1,), (1,)), ((), ())), preferred_element_type=F32)


def _dot_tn(a, b):
    return lax.dot_general(a, b, (((0,), (0,)), ((), ())), preferred_element_type=F32)


def _shift_down(xt, prev8, j):
    if j == 0:
        return xt
    n = xt.shape[0]
    r = pltpu.roll(xt, j, 0)
    p = pltpu.roll(prev8, j, 0)
    rows = lax.broadcasted_iota(jnp.int32, (SUBLANES, xt.shape[1]), 0)
    top = jnp.where(rows < j, p, r[0:SUBLANES])
    if n == SUBLANES:
        return top
    return jnp.concatenate([top, r[SUBLANES:]], axis=0)


def _shift_up(xt, next8, j):
    if j == 0:
        return xt
    n = xt.shape[0]
    r = pltpu.roll(xt, n - j, 0)
    p = pltpu.roll(next8, SUBLANES - j, 0)
    rows = lax.broadcasted_iota(jnp.int32, (SUBLANES, xt.shape[1]), 0)
    bot = jnp.where(rows >= SUBLANES - j, p, r[n - SUBLANES:])
    if n == SUBLANES:
        return bot
    return jnp.concatenate([r[:n - SUBLANES], bot], axis=0)


def _conv4(xt, prev8, w, b):
    out = b + w[3:4] * xt
    for k in range(3):
        out = out + w[k:k + 1] * _shift_down(xt, prev8, 3 - k)
    return out


def _conv4_bwd(d_out, next8, xt, w):
    d_x = w[3:4] * d_out
    d_w = []
    for k in range(3):
        up = _shift_up(d_out, next8, 3 - k)
        d_x = d_x + w[k:k + 1] * up
        d_w.append(_colsum(up * xt))
    d_w.append(_colsum(d_out * xt))
    return d_x, d_w, _colsum(d_out)


def _stack_rows(rows, width):
    rows = list(rows) + [jnp.zeros((1, width), F32)] * (SUBLANES - len(rows))
    return jnp.concatenate(rows, axis=0)


def _pick(n, cands):
    for c in cands:
        if n % c == 0:
            return c
    raise ValueError(f"no tile for {n}")


MM_ROWS = 1024
MM_VMEM_BUDGET = 36 << 20
MM_PANEL_COLS = 2048
MM_SUB = 512


def mm_nn(pairs, *, name, out_dtype=F32, a_fn=None, add=None, epi=None, extra=None, bg=None):
    np_ = len(pairs)
    m, n = pairs[0][0].shape[0], pairs[0][1].shape[1]
    pn = n if n <= MM_PANEL_COLS else _pick(n, (MM_PANEL_COLS, 1024, 512, 256, 128))
    ns = _pick(pn, (MM_SUB, 256, 128))
    adds = list(add or ())
    has_extra = extra is not None
    stage0 = a_fn is not None or pairs[0][0].dtype != BF16

    def vmem_bytes(rows):
        tiles = sum(rows * a.shape[1] * a.dtype.itemsize for a, _ in pairs)
        tiles += rows * pn * (4 * len(adds) + (extra.dtype.itemsize if has_extra else 0) + jnp.dtype(out_dtype).itemsize)
        panels = sum(b.shape[0] * pn * b.dtype.itemsize for _, b in pairs)
        return 2 * (tiles + panels) + (rows * pairs[0][0].shape[1] * 2 if stage0 else 0)

    tm = _pick(m, (MM_ROWS, 512, 256, 128, 64, 32, 16, 8))
    if vmem_bytes(tm) > MM_VMEM_BUDGET:
        tm = _pick(m, (512, 256, 128, 64, 32, 16, 8))

    def body(*refs):
        a_refs, b_refs = refs[:np_], refs[np_:2 * np_]
        pos = 2 * np_
        extra_ref = None
        add_refs = refs[pos:pos + len(adds)]
        pos += len(adds)
        if has_extra:
            extra_ref = refs[pos]
            pos += 1
        o_ref = refs[pos]
        lhs = list(a_refs)
        if stage0:
            av = a_refs[0][...]
            if a_fn is not None:
                av = a_fn(av)
            refs[pos + 1][...] = av.astype(BF16)
            lhs[0] = refs[pos + 1]
        for n0 in range(0, pn, ns):
            sl = slice(n0, n0 + ns)
            acc = None
            for a_ref, b_ref in zip(lhs, b_refs):
                part = _dot(a_ref[...].astype(BF16), b_ref[:, sl])
                acc = part if acc is None else acc + part
            for add_ref in add_refs:
                acc = acc + add_ref[:, sl]
            if epi is not None:
                acc = epi(acc, extra_ref[:, sl]) if has_extra else epi(acc)
            o_ref[:, sl] = acc.astype(out_dtype)

    in_specs = [pl.BlockSpec((tm, a.shape[1]), lambda j, i: (i, 0)) for a, _ in pairs]
    in_specs += [pl.BlockSpec((b.shape[0], pn), lambda j, i: (0, j)) for _, b in pairs]
    args = [a for a, _ in pairs] + [b for _, b in pairs]
    tile = pl.BlockSpec((tm, pn), lambda j, i: (i, j))
    for extra_add in adds:
        in_specs.append(tile)
        args.append(extra_add)
    if has_extra:
        in_specs.append(tile)
        args.append(extra)
    outs, bg_outs = _run(
        body, args, bg, name=name, grid=(n // pn, m // tm), in_specs=in_specs, out_specs=[tile],
        out_shape=[jax.ShapeDtypeStruct((m, n), out_dtype)],
        scratch_shapes=[pltpu.VMEM((tm, pairs[0][0].shape[1]), BF16)] if stage0 else [])
    return outs[0] if bg is None else (outs[0], bg_outs)


MM_REDUCE_ROWS = 1024
MM_GRAD_ROWS = 1024
MM_GRAD_COLS = 2048


def mm_tn(a, b, *, name, a_fn=None):
    m, ka = a.shape
    nb = b.shape[1]
    pa = _pick(ka, (MM_GRAD_ROWS, 512, 256, 128))
    pb = nb if nb <= MM_GRAD_COLS else _pick(nb, (MM_GRAD_COLS, 1024, 512, 256, 128))
    ns = _pick(pb, (MM_SUB, 256, 128))
    tmk = _pick(m, (MM_REDUCE_ROWS, 512, 256, 128, 64, 32, 16))

    def body(a_ref, b_ref, o_ref, lhs):
        k = pl.program_id(2)

        @pl.when(k == 0)
        def _():
            o_ref[...] = jnp.zeros_like(o_ref)

        av = a_ref[...]
        if a_fn is not None:
            av = a_fn(av)
        lhs[...] = av.astype(BF16)
        for n0 in range(0, pb, ns):
            o_ref[:, n0:n0 + ns] += _dot_tn(lhs[...], b_ref[:, n0:n0 + ns].astype(BF16))

    return _pcall(
        body, name=name,
        grid=(ka // pa, nb // pb, m // tmk),
        in_specs=[pl.BlockSpec((tmk, pa), lambda i, j, k: (k, i)),
                  pl.BlockSpec((tmk, pb), lambda i, j, k: (k, j))],
        out_specs=pl.BlockSpec((pa, pb), lambda i, j, k: (i, j)),
        out_shape=jax.ShapeDtypeStruct((ka, nb), F32),
        scratch_shapes=[pltpu.VMEM((tmk, pa), BF16)],
    )(a, b)


def _relu_sq(v):
    r = jnp.maximum(v, 0.0)
    return r * r


ROW_TILE = 512


def _row_specs(bsz, seq, width, ts):
    return pl.BlockSpec((None, ts, width), lambda b, i: (b, i, 0))


def _vec_spec(width):
    return pl.BlockSpec((1, width), lambda b, i: (0, 0))


def _mod_spec():
    return pl.BlockSpec((None, SUBLANES, D_MODEL), lambda b, i: (b, 0, 0))


def _rstd(v):
    return lax.rsqrt(jnp.mean(v * v, axis=-1, keepdims=True) + EPS)


def prenorm(x, w, mod, *, name):
    bsz, seq, d = x.shape
    ts = _pick(seq, (ROW_TILE, 256, 128))

    def body(x_ref, w_ref, mod_ref, h_ref):
        xv = x_ref[...]
        m = mod_ref[...]
        xh = xv * _rstd(xv)
        h_ref[...] = ((xh * w_ref[...]) * (1.0 + m[1:2]) + m[0:1]).astype(BF16)

    return _pcall(
        body, name=name, grid=(bsz, seq // ts),
        in_specs=[_row_specs(bsz, seq, d, ts), _vec_spec(d), _mod_spec()],
        out_specs=_row_specs(bsz, seq, d, ts),
        out_shape=jax.ShapeDtypeStruct((bsz, seq, d), BF16),
    )(x, w, mod)


def post1_pre2(x, out1, mod, post1, pre2, *, name):
    bsz, seq, d = x.shape
    ts = _pick(seq, (ROW_TILE, 256, 128))

    def body(x_ref, o_ref, mod_ref, p1_ref, p2_ref, x1_ref, h2_ref):
        m = mod_ref[...]
        ov = o_ref[...]
        x1 = x_ref[...] + m[2:3] * ((ov * _rstd(ov)) * p1_ref[...])
        x1_ref[...] = x1
        xh = x1 * _rstd(x1)
        h2_ref[...] = ((xh * p2_ref[...]) * (1.0 + m[4:5]) + m[3:4]).astype(BF16)

    return _pcall(
        body, name=name, grid=(bsz, seq // ts),
        in_specs=[_row_specs(bsz, seq, d, ts), _row_specs(bsz, seq, d, ts), _mod_spec(), _vec_spec(d), _vec_spec(d)],
        out_specs=[_row_specs(bsz, seq, d, ts), _row_specs(bsz, seq, d, ts)],
        out_shape=[jax.ShapeDtypeStruct((bsz, seq, d), F32), jax.ShapeDtypeStruct((bsz, seq, d), BF16)],
    )(x, out1, mod, post1, pre2)


def _acc_specs(d):
    per_batch = pl.BlockSpec((None, SUBLANES, d), lambda b, i: (b, 0, 0))
    glob = pl.BlockSpec((SUBLANES, d), lambda b, i: (0, 0))
    return per_batch, glob


def _accumulate(pb_ref, gl_ref, pb_rows, gl_rows, width):
    b, i = pl.program_id(0), pl.program_id(1)

    @pl.when(i == 0)
    def _():
        pb_ref[...] = jnp.zeros_like(pb_ref)

    @pl.when((b == 0) & (i == 0))
    def _():
        gl_ref[...] = jnp.zeros_like(gl_ref)

    pb_ref[...] += _stack_rows(pb_rows, width)
    gl_ref[...] += _stack_rows(gl_rows, width)


def _rms_bwd(d_n, n, r):
    return r * (d_n - n * jnp.mean(d_n * n, axis=-1, keepdims=True))


def final_bwd(x1, y2, target, mod, post2, *, name):
    bsz, seq, d = x1.shape
    ts = _pick(seq, (ROW_TILE, 256, 128))

    def body(x1_ref, y_ref, t_ref, mod_ref, p_ref, dx_ref, dy_ref, pb_ref, gl_ref):
        m = mod_ref[...]
        g2 = m[5:6]
        yv = y_ref[...]
        r = _rstd(yv)
        n = yv * r
        o = n * p_ref[...]
        diff = (x1_ref[...] + g2 * o) - t_ref[...]
        dx = diff * (1.0 / d)
        dx_ref[...] = dx
        d_o = dx * g2
        dy_ref[...] = _rms_bwd(d_o * p_ref[...], n, r).astype(BF16)
        _accumulate(pb_ref, gl_ref, [_colsum(dx * o)], [_colsum(d_o * n), _colsum(diff * diff) * (0.5 / d)], d)

    pb, gl = _acc_specs(d)
    rs = _row_specs(bsz, seq, d, ts)
    return _pcall(
        body, name=name, grid=(bsz, seq // ts),
        in_specs=[rs, rs, rs, _mod_spec(), _vec_spec(d)],
        out_specs=[rs, rs, pb, gl],
        out_shape=[jax.ShapeDtypeStruct((bsz, seq, d), F32), jax.ShapeDtypeStruct((bsz, seq, d), BF16),
                   jax.ShapeDtypeStruct((bsz, SUBLANES, d), F32), jax.ShapeDtypeStruct((SUBLANES, d), F32)],
    )(x1, y2, target, mod, post2)


def mid_bwd(d_h2, dx2, x1, out1, mod, pre2, post1, *, name):
    bsz, seq, d = x1.shape
    ts = _pick(seq, (ROW_TILE, 256, 128))

    def body(dh_ref, dx2_ref, x1_ref, o_ref, mod_ref, p2_ref, p1_ref, dx1_ref, do_ref, pb_ref, gl_ref):
        m = mod_ref[...]
        dh = dh_ref[...]
        x1 = x1_ref[...]
        r2 = _rstd(x1)
        xh = x1 * r2
        xw = xh * p2_ref[...]
        d_xw = dh * (1.0 + m[4:5])
        dx1 = dx2_ref[...] + _rms_bwd(d_xw * p2_ref[...], xh, r2)
        dx1_ref[...] = dx1
        ov = o_ref[...]
        r1 = _rstd(ov)
        n1 = ov * r1
        o1 = n1 * p1_ref[...]
        d_o1 = dx1 * m[2:3]
        do_ref[...] = _rms_bwd(d_o1 * p1_ref[...], n1, r1).astype(BF16)
        _accumulate(pb_ref, gl_ref, [_colsum(dh), _colsum(dh * xw), _colsum(dx1 * o1)],
                    [_colsum(d_xw * xh), _colsum(d_o1 * n1)], d)

    pb, gl = _acc_specs(d)
    rs = _row_specs(bsz, seq, d, ts)
    return _pcall(
        body, name=name, grid=(bsz, seq // ts),
        in_specs=[rs, rs, rs, rs, _mod_spec(), _vec_spec(d), _vec_spec(d)],
        out_specs=[rs, rs, pb, gl],
        out_shape=[jax.ShapeDtypeStruct((bsz, seq, d), F32), jax.ShapeDtypeStruct((bsz, seq, d), BF16),
                   jax.ShapeDtypeStruct((bsz, SUBLANES, d), F32), jax.ShapeDtypeStruct((SUBLANES, d), F32)],
    )(d_h2, dx2, x1, out1, mod, pre2, post1)


def in_dx_first_bwd(d_gates, d_dt, wt_gates, wt_dt, dh_parts, dx1, x, mod, pre1, *, name, bg=None):
    bsz, seq, d = x.shape
    ts = _pick(seq, (ROW_TILE, 256, 128))
    n_parts = len(dh_parts)

    def body(dg_ref, dt_ref, wg_ref, wd_ref, *rest):
        part_refs = rest[:n_parts]
        dx1_ref, x_ref, mod_ref, p_ref, gx_ref, pb_ref, gl_ref = rest[n_parts:]
        dh = _dot(dg_ref[...], wg_ref[...]) + _dot(dt_ref[...], wd_ref[...])
        for part in part_refs:
            dh = dh + part[...]
        m = mod_ref[...]
        xv = x_ref[...]
        r = _rstd(xv)
        xh = xv * r
        xw = xh * p_ref[...]
        d_xw = dh * (1.0 + m[1:2])
        gx_ref[...] = dx1_ref[...] + _rms_bwd(d_xw * p_ref[...], xh, r)
        _accumulate(pb_ref, gl_ref, [_colsum(dh), _colsum(dh * xw)], [_colsum(d_xw * xh)], d)

    pb, gl = _acc_specs(d)
    rs = _row_specs(bsz, seq, d, ts)
    whole = lambda v: pl.BlockSpec(v.shape, lambda b, i: (0, 0))
    return _run(
        body, (d_gates, d_dt, wt_gates, wt_dt, *dh_parts, dx1, x, mod, pre1), bg, name=name, grid=(bsz, seq // ts),
        in_specs=[_row_specs(bsz, seq, d_gates.shape[2], ts), _row_specs(bsz, seq, d_dt.shape[2], ts),
                  whole(wt_gates), whole(wt_dt)] + [rs] * n_parts + [rs, rs, _mod_spec(), _vec_spec(d)],
        out_specs=[rs, pb, gl],
        out_shape=[jax.ShapeDtypeStruct((bsz, seq, d), F32),
                   jax.ShapeDtypeStruct((bsz, SUBLANES, d), F32), jax.ShapeDtypeStruct((SUBLANES, d), F32)])


def merge_bwd(d_merged, ya, yb, gates, b_gate, *, name):
    bsz, seq, d = ya.shape
    ts = _pick(seq, (ROW_TILE, 256, 128))

    def body(dm_ref, ya_ref, yb_ref, g_ref, b_ref, dya_ref, dyb_ref, dg_ref, gl_ref):
        b, i = pl.program_id(0), pl.program_id(1)
        g = _sigmoid(g_ref[...] + b_ref[...])
        dm = dm_ref[...]
        ga, gb = g[:, :d], g[:, d:]
        dya_ref[...] = (dm * ga).astype(BF16)
        dyb_ref[...] = (dm * gb).astype(BF16)
        dg = jnp.concatenate([dm * ya_ref[...] * ga * (1.0 - ga), dm * yb_ref[...] * gb * (1.0 - gb)], axis=1)
        dg_ref[...] = dg.astype(BF16)

        @pl.when((b == 0) & (i == 0))
        def _():
            gl_ref[...] = jnp.zeros_like(gl_ref)

        gl_ref[...] += _stack_rows([_colsum(dg)], 2 * d)

    rs = _row_specs(bsz, seq, d, ts)
    rs2 = _row_specs(bsz, seq, 2 * d, ts)
    return _pcall(
        body, name=name, grid=(bsz, seq // ts),
        in_specs=[rs, rs, rs, rs2, _vec_spec(2 * d)],
        out_specs=[rs, rs, rs2, pl.BlockSpec((SUBLANES, 2 * d), lambda b, i: (0, 0))],
        out_shape=[jax.ShapeDtypeStruct((bsz, seq, d), BF16), jax.ShapeDtypeStruct((bsz, seq, d), BF16),
                   jax.ShapeDtypeStruct((bsz, seq, 2 * d), BF16), jax.ShapeDtypeStruct((SUBLANES, 2 * d), F32)],
    )(d_merged, ya, yb, gates, b_gate)


LRU_TILE = 256
N_LRU_BLOCKS = D_MODEL // LANES


def _block_mm(v, w_ref, transpose=False):
    vb = v.astype(BF16)
    outs = []
    for j in range(N_LRU_BLOCKS):
        blk = vb[:, LANES * j:LANES * (j + 1)]
        outs.append(_dot_nt(blk, w_ref[j]) if transpose else _dot(blk, w_ref[j]))
    return jnp.concatenate(outs, axis=1)


def _lru_gates(xc, wa_ref, ba, wx_ref, bx, sp):
    r = _sigmoid(_block_mm(xc, wa_ref) + ba)
    i = _sigmoid(_block_mm(xc, wx_ref) + bx)
    la = (-LRU_C * r) * sp
    a = jnp.exp(la)
    sq = jnp.sqrt(_neg_expm1(2.0 * la))
    return r, i, a, sq


def _group_roll(v, shift):
    rows, width = v.shape
    return pltpu.roll(v.reshape(rows // SUBLANES, SUBLANES, width), shift, 1).reshape(rows, width)


def _group_scan(a, b, reverse=False):
    row = lax.broadcasted_iota(jnp.int32, a.shape, 0) % SUBLANES
    for s in (1, 2, 4):
        take = (row < SUBLANES - s) if reverse else (row >= s)
        shift = SUBLANES - s if reverse else s
        b = jnp.where(take, a * _group_roll(b, shift) + b, b)
        a = jnp.where(take, a * _group_roll(a, shift), a)
    return a, b


def _prev8_spec(width, col_block, tile_rows):
    per = tile_rows // SUBLANES
    return pl.BlockSpec((None, SUBLANES, width), lambda b, i: (b, jnp.maximum(i * per - 1, 0), col_block))


def lru_fwd(pm, cw, cb, wa, ba, wx, bx, lam, w_pa, *, name):
    bsz, seq, _ = pm.shape
    d = D_MODEL
    ts = _pick(seq, (LRU_TILE, 128))

    def body(lx_ref, lxp_ref, lg_ref, cw_ref, cb_ref, wa_ref, ba_ref, wx_ref, bx_ref, lam_ref, wpa_ref,
             h_ref, pa_ref, ya_ref, kept_ref, hc, a_s, u_s):
        i = pl.program_id(1)

        @pl.when(i == 0)
        def _():
            hc[...] = jnp.zeros_like(hc)

        lx = lx_ref[...]
        prev8 = jnp.where(i == 0, 0.0, lxp_ref[...])
        xc = _conv4(lx, prev8, cw_ref[...], cb_ref[...])
        sp = _softplus(-lam_ref[...])
        r, ig, a, sq = _lru_gates(xc, wa_ref, ba_ref[...], wx_ref, bx_ref[...], sp)
        for k, kept in enumerate((xc, r, ig, a, sq)):
            kept_ref[:, k * d:(k + 1) * d] = kept
        a_s[...], u_s[...] = _group_scan(a, sq * (ig * xc))

        def step(g, h):
            r0 = pl.multiple_of(g * SUBLANES, SUBLANES)
            h8 = a_s[pl.ds(r0, SUBLANES), :] * h + u_s[pl.ds(r0, SUBLANES), :]
            h_ref[pl.ds(r0, SUBLANES), :] = h8
            return h8[SUBLANES - 1:SUBLANES]

        hc[...] = lax.fori_loop(0, ts // SUBLANES, step, hc[...])
        pa_ref[...] = (h_ref[...] * _gelu(lg_ref[...])).astype(BF16)
        ya_ref[...] = _dot(pa_ref[...], wpa_ref[...])

    vec = _vec_spec(d)
    wspec = pl.BlockSpec((N_LRU_BLOCKS, LANES, LANES), lambda b, i: (0, 0, 0))
    rs = _row_specs(bsz, seq, d, ts)
    return _pcall(
        body, name=name, grid=(bsz, seq // ts),
        in_specs=[pl.BlockSpec((None, ts, d), lambda b, i: (b, i, 0)), _prev8_spec(d, 0, ts),
                  pl.BlockSpec((None, ts, d), lambda b, i: (b, i, 1)),
                  pl.BlockSpec((4, d), lambda b, i: (0, 0)), vec, wspec, vec, wspec, vec, vec,
                  pl.BlockSpec(w_pa.shape, lambda b, i: (0, 0))],
        out_specs=[rs, rs, rs, _row_specs(bsz, seq, 5 * d, ts)],
        out_shape=[jax.ShapeDtypeStruct((bsz, seq, d), F32), jax.ShapeDtypeStruct((bsz, seq, d), BF16),
                   jax.ShapeDtypeStruct((bsz, seq, d), F32), jax.ShapeDtypeStruct((bsz, seq, 5 * d), F32)],
        scratch_shapes=[pltpu.VMEM((1, d), F32), pltpu.VMEM((ts, d), F32), pltpu.VMEM((ts, d), F32)],
    )(pm, pm, pm, cw, cb, wa, ba, wx, bx, lam, w_pa)


def lru_bwd(pm, h, kept, d_ya, cw, wa, wx, lam, wt_pa, wt_lru, *, name, bg=None):
    bsz, seq, _ = pm.shape
    d = D_MODEL
    ts = _pick(seq, (LRU_TILE, 128))
    nt = seq // ts
    per = ts // SUBLANES

    def rev(i):
        return nt - 1 - i

    def body(lx_ref, lg_ref, h_ref, hp_ref, kept_ref, dya_ref, cw_ref, wa_ref, wx_ref,
             lam_ref, wtpa_ref, wtl_ref, dl_ref, dh1_ref, dwa_ref, dwx_ref, rows_ref,
             carry, dxc_next, a_s, dh_s, acc_s, a0_s):
        b, i = pl.program_id(0), pl.program_id(1)
        t = rev(i)

        @pl.when(i == 0)
        def _():
            carry[...] = jnp.zeros_like(carry)
            dxc_next[...] = jnp.zeros_like(dxc_next)

        @pl.when((b == 0) & (i == 0))
        def _():
            dwa_ref[...] = jnp.zeros_like(dwa_ref)
            dwx_ref[...] = jnp.zeros_like(dwx_ref)
            rows_ref[...] = jnp.zeros_like(rows_ref)

        lx = lx_ref[...]
        lg = lg_ref[...]
        cwv = cw_ref[...]
        lam_v = lam_ref[...]
        sp = _softplus(-lam_v)
        xc, r, ig, a, sq = (kept_ref[:, k * d:(k + 1) * d] for k in range(5))
        hv = h_ref[...]
        d_pa = _dot(dya_ref[...], wtpa_ref[...])
        row = lax.broadcasted_iota(jnp.int32, a.shape, 0) % SUBLANES
        a_next = jnp.where(row < SUBLANES - 1, _group_roll(a, SUBLANES - 1), 1.0)
        a_s[...], dh_s[...] = _group_scan(a_next, d_pa * _gelu(lg), reverse=True)
        a0_s[...] = a

        def step(g, c):
            r0 = pl.multiple_of((per - 1 - g) * SUBLANES, SUBLANES)
            acc8 = a_s[pl.ds(r0, SUBLANES), :] * c + dh_s[pl.ds(r0, SUBLANES), :]
            acc_s[pl.ds(r0, SUBLANES), :] = acc8
            return a0_s[pl.ds(r0, SUBLANES), :][0:1] * acc8[0:1]

        carry[...] = lax.fori_loop(0, per, step, carry[...])
        d_u = acc_s[...]
        hprev8 = jnp.where(t == 0, 0.0, hp_ref[...])
        d_a = d_u * _shift_down(hv, hprev8, 1)
        d_sq = d_u * (ig * xc)
        d_i = d_u * (sq * xc)
        d_xc = d_u * (sq * ig)
        d_la = d_a * a - d_sq * (a * a) / sq
        d_pre_r = (d_la * (-LRU_C * sp)) * (r * (1.0 - r))
        d_pre_i = d_i * (ig * (1.0 - ig))
        d_xc = d_xc + _block_mm(d_pre_r, wa_ref, transpose=True) + _block_mm(d_pre_i, wx_ref, transpose=True)
        xcb = xc.astype(BF16)
        drb = d_pre_r.astype(BF16)
        dib = d_pre_i.astype(BF16)
        for j in range(N_LRU_BLOCKS):
            sl = slice(LANES * j, LANES * (j + 1))
            dwa_ref[j] += _dot_tn(xcb[:, sl], drb[:, sl])
            dwx_ref[j] += _dot_tn(xcb[:, sl], dib[:, sl])
        d_lx, d_cw, d_cb = _conv4_bwd(d_xc, dxc_next[...], lx, cwv)
        dxc_next[...] = d_xc[0:SUBLANES]
        d_lam = _colsum(d_la * (-LRU_C * r)) * (-_sigmoid(-lam_v))
        rows_ref[...] += _stack_rows([_colsum(d_pre_r), _colsum(d_pre_i), d_lam, d_cb] + d_cw, d)
        dl_ref[:, :d] = d_lx.astype(BF16)
        dl_ref[:, d:] = (d_pa * hv * _gelu_grad(lg)).astype(BF16)
        dh1_ref[...] = _dot(dl_ref[...], wtl_ref[...])

    vec = _vec_spec(d)
    wspec = pl.BlockSpec((N_LRU_BLOCKS, LANES, LANES), lambda b, i: (0, 0, 0))
    tile = lambda col: pl.BlockSpec((None, ts, d), lambda b, i: (b, rev(i), col))
    prev8 = lambda col: pl.BlockSpec((None, SUBLANES, d), lambda b, i: (b, jnp.maximum(rev(i) * per - 1, 0), col))
    whole = lambda v: pl.BlockSpec(v.shape, lambda b, i: (0, 0))
    return _run(
        body, (pm, pm, h, h, kept, d_ya, cw, wa, wx, lam, wt_pa, wt_lru), bg, name=name, grid=(bsz, nt),
        in_specs=[tile(0), tile(1), tile(0), prev8(0), pl.BlockSpec((None, ts, 5 * d), lambda b, i: (b, rev(i), 0)),
                  tile(0), pl.BlockSpec((4, d), lambda b, i: (0, 0)), wspec, wspec, vec,
                  whole(wt_pa), whole(wt_lru)],
        out_specs=[pl.BlockSpec((None, ts, 2 * d), lambda b, i: (b, rev(i), 0)), tile(0), wspec, wspec,
                   pl.BlockSpec((SUBLANES, d), lambda b, i: (0, 0))],
        out_shape=[jax.ShapeDtypeStruct((bsz, seq, 2 * d), BF16), jax.ShapeDtypeStruct((bsz, seq, d), F32),
                   jax.ShapeDtypeStruct((N_LRU_BLOCKS, LANES, LANES), F32),
                   jax.ShapeDtypeStruct((N_LRU_BLOCKS, LANES, LANES), F32),
                   jax.ShapeDtypeStruct((SUBLANES, d), F32)],
        scratch_shapes=[pltpu.VMEM((1, d), F32), pltpu.VMEM((SUBLANES, d), F32),
                        pltpu.VMEM((ts, d), F32), pltpu.VMEM((ts, d), F32), pltpu.VMEM((ts, d), F32),
                        pltpu.VMEM((ts, d), F32)])


L = SSD_CHUNK
N_PAIRS = SSD_HEADS // 2


def _ssd_common(conv, dt_raw, dtb, alog):
    sg = _sigmoid(conv)
    xa = conv * sg
    dtv = _softplus(dt_raw + dtb)
    a_neg = -jnp.exp(alog)
    rowi = lax.broadcasted_iota(jnp.int32, (L, L), 0)
    coli = lax.broadcasted_iota(jnp.int32, (L, L), 1)
    tril = (rowi >= coli).astype(F32)
    cs = _dot(tril, dtv * a_neg, precision=HI)
    return conv, sg, xa, dtv, a_neg, cs, rowi, coli


def _head_masks():
    lane = lax.broadcasted_iota(jnp.int32, (L, LANES), 1)
    return lane < SSD_HEAD_DIM


def _spread(v, p, first):
    return jnp.where(first[:v.shape[0]], v[:, 2 * p:2 * p + 1], v[:, 2 * p + 1:2 * p + 2])


def _place_head_sums(acc, z, p, first, lane1):
    rows = z.shape[0]
    s0 = jnp.sum(jnp.where(first[:rows], z, 0.0), axis=1, keepdims=True)
    s1 = jnp.sum(jnp.where(first[:rows], 0.0, z), axis=1, keepdims=True)
    lane = lane1[:rows]
    return acc + jnp.where(lane == 2 * p, s0, 0.0) + jnp.where(lane == 2 * p + 1, s1, 0.0)


def _stack_heads(v, first):
    return jnp.concatenate([jnp.where(first, v, 0.0), jnp.where(first, 0.0, v)], axis=0).astype(BF16)


def ssd_fwd(pm, dtr, cw, cb, dtb, alog, d_lanes, nw, w_pb, ya, gates, b_gate, w_out, x, mod, post1, pre2, *, name,
            bg=None):
    bsz, seq, _ = pm.shape
    nc = seq // L
    inner, cdim, d = SSD_INNER, SSD_CONV_DIM, D_MODEL

    def body(xbc_ref, xp_ref, z_ref, dt_ref, cw_ref, cb_ref, dtb_ref, alog_ref, dl_ref, nw_ref, wpb_ref,
             ya_ref, g_ref, bg_ref, wout_ref, x_ref, mod_ref, p1_ref, p2_ref,
             y_ref, yn_ref, st_ref, yb_ref, conv_ref, mg_ref, out_ref, x1_ref, h2_ref, state):
        i = pl.program_id(1)

        @pl.when(i == 0)
        def _():
            state[...] = jnp.zeros_like(state)

        prev8 = jnp.where(i == 0, 0.0, xp_ref[...])
        conv = _conv4(xbc_ref[...], prev8, cw_ref[...], cb_ref[...])
        conv_ref[...] = conv
        _, _, xa, dtv, _, cs, rowi, coli = _ssd_common(conv, dt_ref[...], dtb_ref[...], alog_ref[...])
        cst = cs.T
        causal = rowi >= coli
        first = _head_masks()
        for g in range(SSD_GROUPS):
            bg = xa[:, inner + SSD_STATE * g:inner + SSD_STATE * (g + 1)].astype(BF16)
            cg = xa[:, inner + SSD_GROUPS * SSD_STATE + SSD_STATE * g:
                    inner + SSD_GROUPS * SSD_STATE + SSD_STATE * (g + 1)].astype(BF16)
            cbm = _dot_nt(cg, bg)
            for pp in range(2):
                p = 2 * g + pp
                sl = slice(LANES * p, LANES * (p + 1))
                ms = []
                for hh in (2 * p, 2 * p + 1):
                    seg = cs[:, hh:hh + 1] - cst[hh:hh + 1, :]
                    ms.append((cbm * jnp.exp(jnp.where(causal, seg, -jnp.inf))).astype(BF16))
                xsp = xa[:, sl]
                cs_p = _spread(cs, p, first)
                cs_last = cs_p[L - 1:L]
                xp = xsp * _spread(dtv, p, first)
                y_diag = _dot(jnp.concatenate(ms, axis=1), _stack_heads(xp, first))
                st = state[p]
                st_ref[p] = st
                y_off = _dot(cg, st.astype(BF16)) * jnp.exp(cs_p)
                y_ref[:, sl] = y_diag + y_off + dl_ref[:, sl] * xsp
                state[p] = st * jnp.exp(cs_last) + _dot_tn(bg, (xp * jnp.exp(cs_last - cs_p)).astype(BF16))
        zv = z_ref[...]
        yz = y_ref[...] * (zv * _sigmoid(zv))
        gw = inner // SSD_GROUPS
        for g in range(SSD_GROUPS):
            sl = slice(gw * g, gw * (g + 1))
            seg = yz[:, sl]
            yn_ref[:, sl] = ((seg * _rstd(seg)) * nw_ref[:, sl]).astype(BF16)
        yb = _dot(yn_ref[...], wpb_ref[...])
        yb_ref[...] = yb
        g = _sigmoid(g_ref[...] + bg_ref[...])
        mg_ref[...] = (g[:, :d] * ya_ref[...] + g[:, d:] * yb).astype(BF16)
        ov = _dot(mg_ref[...], wout_ref[...])
        out_ref[...] = ov
        m = mod_ref[...]
        x1 = x_ref[...] + m[2:3] * ((ov * _rstd(ov)) * p1_ref[...])
        x1_ref[...] = x1
        h2_ref[...] = (((x1 * _rstd(x1)) * p2_ref[...]) * (1.0 + m[4:5]) + m[3:4]).astype(BF16)

    cvec = lambda w: pl.BlockSpec((1, w), lambda b, i: (0, 0))
    rows = lambda w: pl.BlockSpec((None, L, w), lambda b, i: (b, i, 0))
    outs, bg_outs = _run(
        body, (pm, pm, pm, dtr, cw, cb, dtb, alog, d_lanes, nw, w_pb, ya, gates, b_gate, w_out, x, mod, post1, pre2),
        bg, name=name,
        grid=(bsz, nc),
        in_specs=[pl.BlockSpec((None, L, cdim), lambda b, i: (b, i, 1)), _prev8_spec(cdim, 1, L),
                  pl.BlockSpec((None, L, inner), lambda b, i: (b, i, 1)),
                  pl.BlockSpec((None, L, LANES), lambda b, i: (b, i, 0)),
                  pl.BlockSpec((4, cdim), lambda b, i: (0, 0)), cvec(cdim), cvec(LANES), cvec(LANES),
                  cvec(inner), cvec(inner), pl.BlockSpec(w_pb.shape, lambda b, i: (0, 0)),
                  rows(d), rows(2 * d), cvec(2 * d), pl.BlockSpec(w_out.shape, lambda b, i: (0, 0)),
                  rows(d), _mod_spec(), cvec(d), cvec(d)],
        out_specs=[rows(inner), rows(inner),
                   pl.BlockSpec((None, None, N_PAIRS, SSD_STATE, LANES), lambda b, i: (b, i, 0, 0, 0)),
                   rows(d), rows(cdim), rows(d), rows(d), rows(d), rows(d)],
        out_shape=[jax.ShapeDtypeStruct((bsz, seq, inner), F32), jax.ShapeDtypeStruct((bsz, seq, inner), BF16),
                   jax.ShapeDtypeStruct((bsz, nc, N_PAIRS, SSD_STATE, LANES), F32),
                   jax.ShapeDtypeStruct((bsz, seq, d), F32), jax.ShapeDtypeStruct((bsz, seq, cdim), F32),
                   jax.ShapeDtypeStruct((bsz, seq, d), BF16), jax.ShapeDtypeStruct((bsz, seq, d), F32),
                   jax.ShapeDtypeStruct((bsz, seq, d), F32), jax.ShapeDtypeStruct((bsz, seq, d), BF16)],
        scratch_shapes=[pltpu.VMEM((N_PAIRS, SSD_STATE, LANES), F32)])
    return outs, bg_outs


def ssd_bwd(pm, conv, dtr, y, states, d_yb, cw, dtb, alog, d_lanes, nw, wt_pb, wt_ssd, *, name):
    bsz, seq, _ = pm.shape
    nc = seq // L
    inner, cdim = SSD_INNER, SSD_CONV_DIM
    per = L // SUBLANES

    def rev(i):
        return nc - 1 - i

    def body(xbc_ref, conv_ref, z_ref, dt_ref, y_ref, st_ref, dyb_ref, cw_ref, dtb_ref, alog_ref,
             dl_ref, nw_ref, wtpb_ref, wts_ref, ds_ref, dh1_ref, ddt_ref, r4_ref, r2_ref, r1_ref,
             dstate, dconv_next, dxs_s, dbc_s):
        b, i = pl.program_id(0), pl.program_id(1)
        t = rev(i)

        @pl.when(i == 0)
        def _():
            dstate[...] = jnp.zeros_like(dstate)
            dconv_next[...] = jnp.zeros_like(dconv_next)

        @pl.when((b == 0) & (i == 0))
        def _():
            r4_ref[...] = jnp.zeros_like(r4_ref)
            r2_ref[...] = jnp.zeros_like(r2_ref)
            r1_ref[...] = jnp.zeros_like(r1_ref)

        xbc = xbc_ref[...]
        cwv = cw_ref[...]
        dt_in = dt_ref[...] + dtb_ref[...]
        conv = conv_ref[...]
        _, sg, xa, dtv, a_neg, cs, rowi, coli = _ssd_common(conv, dt_ref[...], dtb_ref[...], alog_ref[...])
        cst = cs.T
        causal = rowi >= coli
        anti = coli >= rowi
        first = _head_masks()
        lane1 = lax.broadcasted_iota(jnp.int32, (L, LANES), 1)

        yv = y_ref[...]
        zv = z_ref[...]
        sz = _sigmoid(zv)
        zs = zv * sz
        yz = yv * zs
        dyn = _dot(dyb_ref[...], wtpb_ref[...])
        gw = inner // SSD_GROUPS
        d_yz_parts, d_nw_parts = [], []
        for g in range(SSD_GROUPS):
            sl = slice(gw * g, gw * (g + 1))
            seg = yz[:, sl]
            r = _rstd(seg)
            n = seg * r
            d_nw_parts.append(_colsum(dyn[:, sl] * n))
            d_yz_parts.append(_rms_bwd(dyn[:, sl] * nw_ref[:, sl], n, r))
        d_yz = jnp.concatenate(d_yz_parts, axis=1)
        d_y = d_yz * zs
        ds_ref[:, :inner] = (d_yz * yv * (sz * (1.0 + zv * (1.0 - sz)))).astype(BF16)

        a1 = jnp.zeros((L, LANES), F32)
        a2 = jnp.zeros((L, LANES), F32)
        xs_dxt = jnp.zeros((L, LANES), F32)
        c0 = jnp.zeros((1, LANES), F32)
        d_dl = jnp.zeros((1, LANES), F32)
        for g in range(SSD_GROUPS):
            bsl = slice(inner + SSD_STATE * g, inner + SSD_STATE * (g + 1))
            csl = slice(inner + SSD_GROUPS * SSD_STATE + SSD_STATE * g,
                        inner + SSD_GROUPS * SSD_STATE + SSD_STATE * (g + 1))
            bg = xa[:, bsl].astype(BF16)
            cg = xa[:, csl].astype(BF16)
            cbm = _dot_nt(cg, bg)
            cbt = _dot_nt(bg, cg)
            d_cb = jnp.zeros((L, L), F32)
            d_bg = jnp.zeros((L, SSD_STATE), F32)
            d_cg = jnp.zeros((L, SSD_STATE), F32)
            for pp in range(2):
                p = 2 * g + pp
                sl = slice(LANES * p, LANES * (p + 1))
                xsp = xa[:, sl]
                dt_p = _spread(dtv, p, first)
                cs_p = _spread(cs, p, first)
                cs_last = cs_p[L - 1:L]
                e_p = jnp.exp(cs_p)
                w_p = jnp.exp(cs_last - cs_p)
                e_last = jnp.exp(cs_last)
                xp = xsp * dt_p
                xpb = xp.astype(BF16)
                dyp = d_y[:, sl]
                dypb = dyp.astype(BF16)
                dy_heads = (jnp.where(first, dyp, 0.0).astype(BF16), jnp.where(first, 0.0, dyp).astype(BF16))
                x_heads = (jnp.where(first, xp, 0.0).astype(BF16), jnp.where(first, 0.0, xp).astype(BF16))
                mts = []
                for k, hh in enumerate((2 * p, 2 * p + 1)):
                    col = cs[:, hh:hh + 1]
                    row = cst[hh:hh + 1, :]
                    dec = jnp.exp(jnp.where(causal, col - row, -jnp.inf))
                    dec_t = jnp.exp(jnp.where(anti, row - col, -jnp.inf))
                    gd = _dot_nt(dy_heads[k], xpb) * dec
                    d_cb = d_cb + gd
                    mt = cbt * dec_t
                    qd = gd * cbm - _dot_nt(x_heads[k], dypb) * mt
                    a1 = a1 + jnp.where(lane1 == hh, jnp.sum(qd, axis=1, keepdims=True), 0.0)
                    mts.append(mt.astype(BF16))
                dst = dstate[p]
                dstb = dst.astype(BF16)
                st = st_ref[p]
                stb = st.astype(BF16)
                dye = (dyp * e_p).astype(BF16)
                xw = (xp * w_p).astype(BF16)
                dx_off = w_p * _dot(bg, dstb)
                d_xp = _dot(jnp.concatenate(mts, axis=1), jnp.concatenate(dy_heads, axis=0)) + dx_off
                dxs_s[:, sl] = d_xp * dt_p + dyp * dl_ref[:, sl]
                a1 = _place_head_sums(a1, dyp * (_dot(cg, stb) * e_p), p, first, lane1)
                a2 = _place_head_sums(a2, xp * dx_off, p, first, lane1)
                xs_dxt = _place_head_sums(xs_dxt, d_xp * xsp, p, first, lane1)
                c0 = _place_head_sums(c0, _colsum(dst * st) * e_last, p, first, lane1)
                d_dl = _place_head_sums(d_dl, _colsum(dyp * xsp), p, first, lane1)
                d_cg = d_cg + _dot_nt(dye, stb)
                d_bg = d_bg + _dot_nt(xw, dstb)
                dstate[p] = dst * e_last + _dot_tn(cg, dye)
            d_cbb = d_cb.astype(BF16)
            dbc_s[:, SSD_STATE * g:SSD_STATE * (g + 1)] = d_bg + _dot_tn(d_cbb, cg)
            dbc_s[:, SSD_GROUPS * SSD_STATE + SSD_STATE * g:SSD_GROUPS * SSD_STATE + SSD_STATE * (g + 1)] = (
                d_cg + _dot(d_cbb, bg))

        d_da = (_dot(anti.astype(F32), a1, precision=HI) + _dot((rowi > coli).astype(F32), a2, precision=HI) + c0)
        d_dt = d_da * a_neg + xs_dxt
        d_alog = _colsum(d_da * dtv) * a_neg
        d_dtr = jnp.where(lane1 < SSD_HEADS, d_dt * _sigmoid(dt_in), 0.0)
        ddt_ref[...] = d_dtr.astype(BF16)
        d_xa = jnp.concatenate([dxs_s[...], dbc_s[...]], axis=1)
        d_conv = d_xa * (sg * (1.0 + conv * (1.0 - sg)))
        d_xbc, d_cw, d_cbias = _conv4_bwd(d_conv, dconv_next[...], xbc, cwv)
        dconv_next[...] = d_conv[0:SUBLANES]
        ds_ref[:, inner:] = d_xbc.astype(BF16)
        dh1_ref[...] = _dot(ds_ref[...], wts_ref[...])
        r4_ref[...] += _stack_rows([d_cbias] + d_cw, cdim)
        r2_ref[...] += _stack_rows([jnp.concatenate(d_nw_parts, axis=1)], inner)
        r1_ref[...] += _stack_rows([_colsum(d_dtr), d_alog, d_dl], LANES)

    cvec = lambda w: pl.BlockSpec((1, w), lambda b, i: (0, 0))
    return _pcall(
        body, name=name, grid=(bsz, nc),
        in_specs=[pl.BlockSpec((None, L, cdim), lambda b, i: (b, rev(i), 1)),
                  pl.BlockSpec((None, L, cdim), lambda b, i: (b, rev(i), 0)),
                  pl.BlockSpec((None, L, inner), lambda b, i: (b, rev(i), 1)),
                  pl.BlockSpec((None, L, LANES), lambda b, i: (b, rev(i), 0)),
                  pl.BlockSpec((None, L, inner), lambda b, i: (b, rev(i), 0)),
                  pl.BlockSpec((None, None, N_PAIRS, SSD_STATE, LANES), lambda b, i: (b, rev(i), 0, 0, 0)),
                  pl.BlockSpec((None, L, D_MODEL), lambda b, i: (b, rev(i), 0)),
                  pl.BlockSpec((4, cdim), lambda b, i: (0, 0)), cvec(LANES), cvec(LANES),
                  cvec(inner), cvec(inner), pl.BlockSpec(wt_pb.shape, lambda b, i: (0, 0)),
                  pl.BlockSpec(wt_ssd.shape, lambda b, i: (0, 0))],
        out_specs=[pl.BlockSpec((None, L, inner + cdim), lambda b, i: (b, rev(i), 0)),
                   pl.BlockSpec((None, L, D_MODEL), lambda b, i: (b, rev(i), 0)),
                   pl.BlockSpec((None, L, LANES), lambda b, i: (b, rev(i), 0)),
                   pl.BlockSpec((SUBLANES, cdim), lambda b, i: (0, 0)),
                   pl.BlockSpec((SUBLANES, inner), lambda b, i: (0, 0)),
                   pl.BlockSpec((SUBLANES, LANES), lambda b, i: (0, 0))],
        out_shape=[jax.ShapeDtypeStruct((bsz, seq, inner + cdim), BF16),
                   jax.ShapeDtypeStruct((bsz, seq, D_MODEL), F32),
                   jax.ShapeDtypeStruct((bsz, seq, LANES), BF16),
                   jax.ShapeDtypeStruct((SUBLANES, cdim), F32),
                   jax.ShapeDtypeStruct((SUBLANES, inner), F32),
                   jax.ShapeDtypeStruct((SUBLANES, LANES), F32)],
        scratch_shapes=[pltpu.VMEM((N_PAIRS, SSD_STATE, LANES), F32), pltpu.VMEM((SUBLANES, cdim), F32),
                        pltpu.VMEM((L, inner), F32), pltpu.VMEM((L, 2 * SSD_GROUPS * SSD_STATE), F32)],
    )(pm, conv, pm, dtr, y, states, d_yb, cw, dtb, alog, d_lanes, nw, wt_pb, wt_ssd)


def _lru_block_weights(w):
    w = w.reshape(N_LRU_BLOCKS, 2, LRU_HEAD_DIM, LRU_HEAD_DIM)
    z = jnp.zeros((N_LRU_BLOCKS, LRU_HEAD_DIM, LRU_HEAD_DIM), w.dtype)
    top = jnp.concatenate([w[:, 0], z], axis=2)
    bot = jnp.concatenate([z, w[:, 1]], axis=2)
    return jnp.concatenate([top, bot], axis=1).astype(BF16)


def _lru_block_grads(g):
    h = LRU_HEAD_DIM
    return jnp.stack([g[:, :h, :h], g[:, h:, h:]], axis=1).reshape(LRU_HEADS, h, h)


def _pad_lanes(v, width=LANES):
    return jnp.pad(v, ((0, 0), (0, width - v.shape[1])))


class NoExchange:
    def __init__(self, weights):
        self._weights, self.grads = weights, {}

    def weights_bg(self):
        return None

    def weights(self, bg_outs):
        return self._weights

    def grads_bg(self, grads):
        self.grads.update(grads)
        return None

    def grads_done(self, bg_outs):
        pass


def local_step(x, target, mod, big, small, plan):
    bsz, seq, d = x.shape
    t = bsz * seq
    flat = lambda v: v.reshape(t, v.shape[-1])
    unflat = lambda v: v.reshape(bsz, seq, v.shape[-1])

    wa_b = _lru_block_weights(small["lru_wa"])
    wx_b = _lru_block_weights(small["lru_wx"])
    dtb = _pad_lanes(small["ssd_dt_bias"])
    alog = _pad_lanes(small["ssd_a_log"])
    d_lanes = jnp.repeat(small["ssd_d"], SSD_HEAD_DIM, axis=1)

    lru_cols = 2 * D_MODEL
    wt = {"lru": big["w_main"][:, :lru_cols].T, "ssd": big["w_main"][:, lru_cols:].T, "gates": big["w_gates"].T,
          "dt": big["w_dt"].T}

    h1 = prenorm(x, small["pre_norm1"], mod, name="prenorm1")
    h1f = flat(h1)
    arriving = plan.weights_bg()
    if arriving is None:
        pm, arrived = mm_nn([(h1f, big["w_main"])], name="in_proj_main"), []
    else:
        pm, arrived = mm_nn([(h1f, big["w_main"])], name="in_proj_main", bg=arriving)
    pm = unflat(pm)
    big = dict(big, **plan.weights(arrived))
    for n in ("w_pa", "w_pb", "w_out", "w_ff1", "w_ff2"):
        wt[n] = big[n].T
    gates = unflat(mm_nn([(h1f, big["w_gates"])], name="in_proj_gates"))
    dtr = unflat(mm_nn([(h1f, big["w_dt"])], name="in_proj_dt"))
    lru_args = (small["lru_conv_w"], small["lru_conv_b"], wa_b, small["lru_ba"], wx_b, small["lru_bx"],
                small["lru_lambda"])
    h_lru, pa_in, ya, lru_kept = lru_fwd(pm, *lru_args, big["w_pa"], name="lru_fwd")
    ssd_args = (small["ssd_conv_w"], small["ssd_conv_b"], dtb, alog, d_lanes, small["ssd_norm_w"])
    (y_ssd, ynorm, states, yb, conv_ssd, merged, out1, x1, h2), _ = ssd_fwd(
        pm, dtr, *ssd_args, big["w_pb"], ya, gates, small["b_gate"], big["w_out"], x, mod, small["post_norm1"],
        small["pre_norm2"], name="ssd_fwd")
    f = mm_nn([(flat(h2), big["w_ff1"])], name="ff1")
    y2 = unflat(mm_nn([(f, big["w_ff2"])], a_fn=_relu_sq, name="ff2"))

    dx2, d_y2, pb_a, gl_a = final_bwd(x1, y2, target, mod, small["post_norm2"], name="final_bwd")
    d_y2f = flat(d_y2)
    d_f = mm_nn([(d_y2f, wt["w_ff2"])], out_dtype=BF16, extra=f,
                epi=lambda r, fv: r * (2.0 * jnp.maximum(fv, 0.0)), name="ff2_dx")
    g_ff2 = mm_tn(f, d_y2f, a_fn=_relu_sq, name="ff2_dw")
    d_h2 = unflat(mm_nn([(d_f, wt["w_ff1"])], name="ff1_dx"))
    g_ff1 = mm_tn(flat(h2), d_f, name="ff1_dw")
    dx1, d_out1, pb_b, gl_b = mid_bwd(d_h2, dx2, x1, out1, mod, small["pre_norm2"], small["post_norm1"],
                                      name="mid_bwd")
    d_out1f = flat(d_out1)
    d_merged = unflat(mm_nn([(d_out1f, wt["w_out"])], name="out_dx"))
    g_out = mm_tn(flat(merged), d_out1f, name="out_dw")
    d_ya, d_yb, d_gates, gl_c = merge_bwd(d_merged, ya, yb, gates, small["b_gate"], name="merge_bwd")
    g_pa = mm_tn(flat(pa_in), flat(d_ya), name="pa_dw")
    g_pb = mm_tn(flat(ynorm), flat(d_yb), name="pb_dw")
    leaving = plan.grads_bg({"w_pa": g_pa, "w_pb": g_pb, "w_out": g_out, "w_ff1": g_ff1, "w_ff2": g_ff2})
    (d_l, dh_lru, g_wa_b, g_wx_b, lru_rows), landed = lru_bwd(
        pm, h_lru, lru_kept, d_ya, small["lru_conv_w"], wa_b, wx_b, small["lru_lambda"], wt["w_pa"], wt["lru"],
        name="lru_bwd", bg=leaving)
    plan.grads_done(landed)
    d_s, dh_ssd, d_dt, r4, r2, r1 = ssd_bwd(pm, conv_ssd, dtr, y_ssd, states, d_yb, small["ssd_conv_w"], dtb, alog,
                                          d_lanes, small["ssd_norm_w"], wt["w_pb"], wt["ssd"], name="ssd_bwd")
    d_lf, d_sf, d_gf, d_dtf = flat(d_l), flat(d_s), flat(d_gates), flat(d_dt)
    g_in = jnp.concatenate([
        mm_tn(h1f, d_lf, name="in_dw_lru"), mm_tn(h1f, d_sf, name="in_dw_ssd"),
        mm_tn(h1f, d_dtf, name="in_dw_dt")[:, :SSD_HEADS], mm_tn(h1f, d_gf, name="in_dw_gates")], axis=1)
    leaving = plan.grads_bg({"w_in": g_in})
    (grad_x, pb_c, gl_d), landed = in_dx_first_bwd(
        d_gates, d_dt, wt["gates"], wt["dt"], [dh_lru, dh_ssd], dx1, x, mod, small["pre_norm1"],
        name="in_dx_first_bwd", bg=leaving)
    plan.grads_done(landed)

    d_mod = jnp.stack([pb_c[:, 0], pb_c[:, 1], pb_b[:, 2], pb_b[:, 0], pb_b[:, 1], pb_a[:, 0]], axis=1)
    loss_cols = gl_a[1:2]
    nh = SSD_HEADS
    small_grads = {
        "pre_norm1": gl_d[0:1], "post_norm1": gl_b[1:2], "b_gate": gl_c[0:1],
        "lru_conv_w": lru_rows[4:8], "lru_conv_b": lru_rows[3:4],
        "lru_wa": _lru_block_grads(g_wa_b), "lru_ba": lru_rows[0:1],
        "lru_wx": _lru_block_grads(g_wx_b), "lru_bx": lru_rows[1:2], "lru_lambda": lru_rows[2:3],
        "ssd_conv_w": r4[1:5], "ssd_conv_b": r4[0:1],
        "ssd_dt_bias": r1[0:1, :nh], "ssd_a_log": r1[1:2, :nh], "ssd_d": r1[2:3, :nh],
        "ssd_norm_w": r2[0:1], "pre_norm2": gl_b[0:1], "post_norm2": gl_a[0:1],
    }
    return loss_cols, grad_x, d_mod, small_grads


def _position():
    return lax.axis_index("x"), lax.axis_index("y"), lax.axis_index("c")


def _other_chips(x, y):
    return [(1 - x, y), (x, 1 - y), (1 - x, 1 - y)]


def allgather8(v, *, name):
    m_per, n = v.shape

    def body(x_ref, out_ref, send_sems, recv_sems, local_sem):
        x, y, c = _position()
        me, sibling = (x, y, c), (x, y, 1 - c)
        chips = _other_chips(x, y)

        def rows(px, py, pc):
            return out_ref.at[pl.ds((4 * px + 2 * py + pc) * m_per, m_per), :]

        def copy(k, block, to, src=None):
            return pltpu.make_async_remote_copy(
                src_ref=rows(*block) if src is None else src, dst_ref=rows(*block),
                send_sem=send_sems.at[k], recv_sem=recv_sems.at[k], device_id=to, device_id_type=MESH)

        mine = pltpu.make_async_copy(x_ref, rows(*me), local_sem)
        mine.start()
        first = [copy(0, me, sibling, src=x_ref)]
        first += [copy(1 + j, me, (*chip, c), src=x_ref) for j, chip in enumerate(chips)]
        for cp in first:
            cp.start()
        passed = [copy(4 + j, (*chip, c), sibling) for j, chip in enumerate(chips)]
        for j, chip in enumerate(chips):
            copy(1 + j, (*chip, c), me).wait_recv()
            passed[j].start()
        copy(0, sibling, me).wait_recv()
        for j, chip in enumerate(chips):
            copy(4 + j, (*chip, 1 - c), me).wait_recv()
        for cp in first + passed:
            cp.wait_send()
        mine.wait()

    return _pcall(
        body, name=name,
        out_shape=jax.ShapeDtypeStruct((N_DEV * m_per, n), v.dtype),
        in_specs=[pl.BlockSpec(memory_space=pltpu.VMEM)],
        out_specs=pl.BlockSpec(memory_space=pltpu.VMEM),
        scratch_shapes=[pltpu.SemaphoreType.DMA((7,)), pltpu.SemaphoreType.DMA((7,)), pltpu.SemaphoreType.DMA],
    )(v)


def gather_weights(shards, *, name):
    n = len(shards)
    half = [s.shape[0] // 2 for s in shards]
    widths = sorted({s.shape[1] for s in shards})
    chunk_rows = [_stage_rows(h, s.shape[1], itemsize=s.dtype.itemsize) for s, h in zip(shards, half)]
    plan = [(w, j, r0) for w in range(n) for j in range(N_CHIPS - 1) for r0 in range(0, half[w], chunk_rows[w])]

    def body(*refs):
        ins, outs = refs[:n], refs[n:2 * n]
        send_sems, recv_sems, local_sems, passed_sems = refs[2 * n:2 * n + 4]
        stage = refs[2 * n + 4:]
        bufs = {wd: stage[4 * i] for i, wd in enumerate(widths)}
        load_sems = {wd: stage[4 * i + 1] for i, wd in enumerate(widths)}
        stage_send = {wd: stage[4 * i + 2] for i, wd in enumerate(widths)}
        x, y, c = _position()
        me_chip = 2 * x + y
        chips = _other_chips(x, y)

        def piece(w, chip, core):
            return outs[w].at[chip, pl.ds(core * half[w], half[w]), :]

        def over_ici(w, j, chip, src=None):
            px, py = chips[j]
            dst = piece(w, chip, c)
            return pltpu.make_async_remote_copy(
                src_ref=dst if src is None else src, dst_ref=dst, send_sem=send_sems.at[3 * w + j],
                recv_sem=recv_sems.at[3 * w + j], device_id=(px, py, c), device_id_type=MESH)

        local = [pltpu.make_async_copy(ins[w], outs[w].at[me_chip], local_sems.at[w]) for w in range(n)]
        for cp in local:
            cp.start()
        sent = []
        for w in range(n):
            for j in range(N_CHIPS - 1):
                cp = over_ici(w, j, me_chip, src=ins[w].at[pl.ds(c * half[w], half[w]), :])
                cp.start()
                sent.append(cp)
        chunks = []
        for idx, (w, j, r0) in enumerate(plan):
            wd, rb = shards[w].shape[1], chunk_rows[w]
            k = 2 * chips[j][0] + chips[j][1]

            def make(staged, slot, idx=idx, w=w, k=k, r0=r0, wd=wd, rb=rb):
                return pltpu.make_async_remote_copy(
                    src_ref=staged, dst_ref=outs[w].at[k, pl.ds(c * half[w] + r0, rb), :],
                    send_sem=stage_send[wd].at[slot], recv_sem=passed_sems.at[idx],
                    device_id=(x, y, 1 - c), device_id_type=MESH), True

            chunk = (wd, outs[w].at[k, pl.ds(c * half[w] + r0, rb), :], [make])
            if r0 == 0:
                chunk += (lambda w=w, j=j, k=k: over_ici(w, j, k).wait_recv(),)
            chunks.append(chunk)
        _staged(chunks, bufs, load_sems)
        for idx, (w, j, r0) in enumerate(plan):
            wd = shards[w].shape[1]
            k = 2 * chips[j][0] + chips[j][1]
            landed = outs[w].at[k, pl.ds((1 - c) * half[w] + r0, chunk_rows[w]), :]
            pltpu.make_async_remote_copy(
                src_ref=landed, dst_ref=landed, send_sem=stage_send[wd].at[0], recv_sem=passed_sems.at[idx],
                device_id=(x, y, 1 - c), device_id_type=MESH).wait_recv()
        for cp in sent:
            cp.wait_send()
        for cp in local:
            cp.wait()

    stage_rows = [(wd, max(r for s, r in zip(shards, chunk_rows) if s.shape[1] == wd)) for wd in widths]
    return _pcall(
        body, name=name,
        out_shape=[jax.ShapeDtypeStruct((N_CHIPS,) + s.shape, s.dtype) for s in shards],
        in_specs=[ANY] * n, out_specs=[ANY] * n,
        scratch_shapes=[pltpu.SemaphoreType.DMA((3 * n,)), pltpu.SemaphoreType.DMA((3 * n,)),
                        pltpu.SemaphoreType.DMA((n,)), pltpu.SemaphoreType.DMA((len(plan),))]
        + _stage_scratch(stage_rows, shards[0].dtype),
    )(*shards)


STAGE_BYTES = 2 << 20


def _stage_rows(rows, width, itemsize=4):
    return _pick(rows, tuple(t for t in (1024, 512, 256, 128, 64, 32, 16, 8) if t * width * itemsize <= STAGE_BYTES * 3 // 2))


def _staged(chunks, bufs, load_sems):
    count, pending = {}, {}

    def load(i):
        cls, src = chunks[i][0], chunks[i][1]
        if len(chunks[i]) > 3:
            chunks[i][3]()
        slot = count.get(cls, 0) % 2
        count[cls] = count.get(cls, 0) + 1
        for cp, remote in pending.pop((cls, slot), []):
            if remote:
                cp.wait_send()
            else:
                cp.wait()
        staged = bufs[cls].at[slot, pl.ds(0, src.shape[0]), :]
        ld = pltpu.make_async_copy(src, staged, load_sems[cls].at[slot])
        ld.start()
        return ld, cls, slot, staged

    cur = load(0)
    for i in range(len(chunks)):
        nxt = load(i + 1) if i + 1 < len(chunks) else None
        ld, cls, slot, staged = cur
        ld.wait()
        started = []
        for make in chunks[i][2]:
            cp, remote = make(staged, slot)
            cp.start()
            started.append((cp, remote))
        pending[(cls, slot)] = started
        cur = nxt
    for started in pending.values():
        for cp, remote in started:
            if remote:
                cp.wait_send()
            else:
                cp.wait()


def _stage_scratch(widths_rows, dtype):
    scratch = []
    for width, rows in widths_rows:
        scratch += [pltpu.VMEM((2, rows, width), dtype), pltpu.SemaphoreType.DMA((2,)), pltpu.SemaphoreType.DMA((2,)),
                    pltpu.SemaphoreType.DMA((2,))]
    return scratch


def send_half_to_sibling(grads, *, name):
    n = len(grads)
    half = [g.shape[1] // 2 for g in grads]
    widths = sorted({g.shape[2] for g in grads})
    chunk_rows = [_stage_rows(h, g.shape[2]) for g, h in zip(grads, half)]
    plan = [(w, k, r0) for w in range(n) for k in range(N_CHIPS) for r0 in range(0, half[w], chunk_rows[w])]

    def body(*refs):
        ins, theirs = refs[:n], refs[n:2 * n]
        recv_sems = refs[2 * n]
        stage = refs[2 * n + 1:]
        bufs = {wd: stage[4 * i] for i, wd in enumerate(widths)}
        load_sems = {wd: stage[4 * i + 1] for i, wd in enumerate(widths)}
        send_sems = {wd: stage[4 * i + 2] for i, wd in enumerate(widths)}
        x, y, c = _position()
        chunks = []
        for idx, (w, k, r0) in enumerate(plan):
            wd = grads[w].shape[2]
            rb = chunk_rows[w]

            def make(staged, slot, idx=idx, w=w, k=k, r0=r0, wd=wd, rb=rb):
                return pltpu.make_async_remote_copy(
                    src_ref=staged, dst_ref=theirs[w].at[k, pl.ds(r0, rb), :], send_sem=send_sems[wd].at[slot],
                    recv_sem=recv_sems.at[idx], device_id=(x, y, 1 - c), device_id_type=MESH), True

            chunks.append((wd, ins[w].at[k, pl.ds((1 - c) * half[w] + r0, rb), :], [make]))
        _staged(chunks, bufs, load_sems)
        for idx, (w, k, r0) in enumerate(plan):
            wd = grads[w].shape[2]
            landed = theirs[w].at[k, pl.ds(r0, chunk_rows[w]), :]
            pltpu.make_async_remote_copy(
                src_ref=landed, dst_ref=landed, send_sem=send_sems[wd].at[0], recv_sem=recv_sems.at[idx],
                device_id=(x, y, 1 - c), device_id_type=MESH).wait_recv()

    stage_rows = [(wd, max(r for g, r in zip(grads, chunk_rows) if g.shape[2] == wd)) for wd in widths]
    return _pcall(
        body, name=name,
        out_shape=[jax.ShapeDtypeStruct((N_CHIPS, h, g.shape[2]), g.dtype) for g, h in zip(grads, half)],
        in_specs=[ANY] * n, out_specs=[ANY] * n,
        scratch_shapes=[pltpu.SemaphoreType.DMA((len(plan),))] + _stage_scratch(stage_rows, F32),
    )(*grads)


def _chip_exchange_background(arrays, out_shapes, src_of, dst_of, landed_of, own_of):
    n = len(arrays)

    def copies(ins, outs, scr):
        send_sems, recv_sems, local_sems = scr
        x, y, c = _position()
        me_chip = 2 * x + y
        local, sends, recvs = [], [], []
        for w in range(n):
            local.append(pltpu.make_async_copy(*own_of(ins[w], outs[w], w, me_chip), local_sems.at[w]))
            for j, (px, py) in enumerate(_other_chips(x, y)):
                sems = dict(send_sem=send_sems.at[3 * w + j], recv_sem=recv_sems.at[3 * w + j],
                            device_id=(px, py, c), device_id_type=MESH)
                sends.append(pltpu.make_async_remote_copy(
                    src_ref=src_of(ins[w], w, 2 * px + py, me_chip, c), dst_ref=dst_of(outs[w], w, me_chip, c), **sems))
                landed = landed_of(outs[w], w, 2 * px + py, c)
                recvs.append(pltpu.make_async_remote_copy(src_ref=landed, dst_ref=landed, **sems))
        return local, sends, recvs

    def start(ins, outs, scr):
        local, sends, _ = copies(ins, outs, scr)
        for cp in local + sends:
            cp.start()

    def finish(ins, outs, scr):
        local, sends, recvs = copies(ins, outs, scr)
        for cp in recvs:
            cp.wait_recv()
        for cp in sends:
            cp.wait_send()
        for cp in local:
            cp.wait()

    scratch = [pltpu.SemaphoreType.DMA((3 * n,)), pltpu.SemaphoreType.DMA((3 * n,)), pltpu.SemaphoreType.DMA((n,))]
    return Background(arrays, out_shapes, scratch, start, finish)


def scatter_background(parts):
    return _chip_exchange_background(
        parts, [jax.ShapeDtypeStruct(p.shape, p.dtype) for p in parts],
        src_of=lambda ref, w, peer, me, c: ref.at[peer], dst_of=lambda ref, w, me, c: ref.at[me],
        landed_of=lambda ref, w, peer, c: ref.at[peer], own_of=lambda i, o, w, me: (i.at[me], o.at[me]))


def gather_halves_background(shards):
    half = [s.shape[0] // 2 for s in shards]
    rows = lambda w, c: pl.ds(c * half[w], half[w])
    return _chip_exchange_background(
        shards, [jax.ShapeDtypeStruct((N_CHIPS,) + s.shape, s.dtype) for s in shards],
        src_of=lambda ref, w, peer, me, c: ref.at[rows(w, c), :], dst_of=lambda ref, w, me, c: ref.at[me, rows(w, c), :],
        landed_of=lambda ref, w, peer, c: ref.at[peer, rows(w, c), :], own_of=lambda i, o, w, me: (i, o.at[me]))


def fill_other_half(gathered, *, name):
    n = len(gathered)
    half = [g.shape[1] // 2 for g in gathered]
    widths = sorted({g.shape[2] for g in gathered})
    chunk_rows = [_stage_rows(h, g.shape[2], itemsize=2) for g, h in zip(gathered, half)]
    plan = [(w, j, r0) for w in range(n) for j in range(N_CHIPS - 1) for r0 in range(0, half[w], chunk_rows[w])]

    def body(*refs):
        ins, outs = refs[:n], refs[n:2 * n]
        recv_sems = refs[2 * n]
        stage = refs[2 * n + 1:]
        bufs = {wd: stage[4 * i] for i, wd in enumerate(widths)}
        load_sems = {wd: stage[4 * i + 1] for i, wd in enumerate(widths)}
        send_sems = {wd: stage[4 * i + 2] for i, wd in enumerate(widths)}
        x, y, c = _position()
        chips = _other_chips(x, y)
        chunks = []
        for idx, (w, j, r0) in enumerate(plan):
            wd, rb = gathered[w].shape[2], chunk_rows[w]
            k = 2 * chips[j][0] + chips[j][1]

            def make(staged, slot, idx=idx, w=w, k=k, r0=r0, wd=wd, rb=rb):
                return pltpu.make_async_remote_copy(
                    src_ref=staged, dst_ref=outs[w].at[k, pl.ds(c * half[w] + r0, rb), :],
                    send_sem=send_sems[wd].at[slot], recv_sem=recv_sems.at[idx],
                    device_id=(x, y, 1 - c), device_id_type=MESH), True

            chunks.append((wd, ins[w].at[k, pl.ds(c * half[w] + r0, rb), :], [make]))
        _staged(chunks, bufs, load_sems)
        for idx, (w, j, r0) in enumerate(plan):
            wd = gathered[w].shape[2]
            k = 2 * chips[j][0] + chips[j][1]
            landed = outs[w].at[k, pl.ds((1 - c) * half[w] + r0, chunk_rows[w]), :]
            pltpu.make_async_remote_copy(
                src_ref=landed, dst_ref=landed, send_sem=send_sems[wd].at[0], recv_sem=recv_sems.at[idx],
                device_id=(x, y, 1 - c), device_id_type=MESH).wait_recv()

    stage_rows = [(wd, max(r for g, r in zip(gathered, chunk_rows) if g.shape[2] == wd)) for wd in widths]
    return _pcall(
        body, name=name, out_shape=[jax.ShapeDtypeStruct(g.shape, g.dtype) for g in gathered],
        in_specs=[ANY] * n, out_specs=[ANY] * n, input_output_aliases={w: w for w in range(n)},
        scratch_shapes=[pltpu.SemaphoreType.DMA((len(plan),))] + _stage_scratch(stage_rows, gathered[0].dtype),
    )(*gathered)


def join_with_sibling(halves, *, name):
    n = len(halves)
    widths = sorted({h.shape[1] for h in halves})
    chunk_rows = [_stage_rows(h.shape[0], h.shape[1]) for h in halves]
    plan = [(w, r0) for w in range(n) for r0 in range(0, halves[w].shape[0], chunk_rows[w])]

    def body(*refs):
        ins, outs = refs[:n], refs[n:2 * n]
        recv_sems = refs[2 * n]
        stage = refs[2 * n + 1:]
        bufs = {wd: stage[4 * i] for i, wd in enumerate(widths)}
        load_sems = {wd: stage[4 * i + 1] for i, wd in enumerate(widths)}
        send_sems = {wd: stage[4 * i + 2] for i, wd in enumerate(widths)}
        store_sems = {wd: stage[4 * i + 3] for i, wd in enumerate(widths)}
        x, y, c = _position()
        chunks = []
        for idx, (w, r0) in enumerate(plan):
            h, wd = halves[w].shape
            rb = chunk_rows[w]

            def to_sibling(staged, slot, idx=idx, w=w, r0=r0, h=h, wd=wd, rb=rb):
                return pltpu.make_async_remote_copy(
                    src_ref=staged, dst_ref=outs[w].at[pl.ds(c * h + r0, rb), :], send_sem=send_sems[wd].at[slot],
                    recv_sem=recv_sems.at[idx], device_id=(x, y, 1 - c), device_id_type=MESH), True

            def to_mine(staged, slot, w=w, r0=r0, h=h, wd=wd, rb=rb):
                return pltpu.make_async_copy(staged, outs[w].at[pl.ds(c * h + r0, rb), :], store_sems[wd].at[slot]), False

            chunks.append((wd, ins[w].at[pl.ds(r0, rb), :], [to_sibling, to_mine]))
        _staged(chunks, bufs, load_sems)
        for idx, (w, r0) in enumerate(plan):
            h, wd = halves[w].shape
            landed = outs[w].at[pl.ds((1 - c) * h + r0, chunk_rows[w]), :]
            pltpu.make_async_remote_copy(
                src_ref=landed, dst_ref=landed, send_sem=send_sems[wd].at[0], recv_sem=recv_sems.at[idx],
                device_id=(x, y, 1 - c), device_id_type=MESH).wait_recv()

    stage_rows = [(wd, max(r for h, r in zip(halves, chunk_rows) if h.shape[1] == wd)) for wd in widths]
    return _pcall(
        body, name=name,
        out_shape=[jax.ShapeDtypeStruct((2 * h.shape[0], h.shape[1]), h.dtype) for h in halves],
        in_specs=[ANY] * n, out_specs=[ANY] * n,
        scratch_shapes=[pltpu.SemaphoreType.DMA((len(plan),))] + _stage_scratch(stage_rows, F32),
    )(*halves)


def _row_tile(rows, cols, itemsize=4, budget=2 << 20):
    for t in (1024, 512, 256, 128, 64, 32, 16, 8):
        if rows % t == 0 and t * cols * itemsize <= budget:
            return t
    return rows


def add_half_to_bf16(core, full, theirs, *, name):
    k, r, c = theirs.shape
    tr = _row_tile(r, c)
    nb = r // tr

    def body(core_ref, a_ref, b_ref, o_ref):
        o_ref[...] = (a_ref[...] + b_ref[...]).astype(BF16)

    spec = pl.BlockSpec((None, tr, c), lambda i, j, core_ref: (i, j, 0))
    grid_spec = pltpu.PrefetchScalarGridSpec(
        num_scalar_prefetch=1, grid=(k, nb),
        in_specs=[pl.BlockSpec((None, tr, c), lambda i, j, core_ref: (i, core_ref[0] * nb + j, 0)), spec],
        out_specs=spec)
    return _pcall(body, name=name, grid_spec=grid_spec,
                  out_shape=jax.ShapeDtypeStruct(theirs.shape, BF16))(core, full, theirs)


def sum_blocks(v, *, name):
    k, r, c = v.shape
    tr = _row_tile(r, c * k)

    def body(v_ref, o_ref):
        acc = v_ref[0].astype(F32)
        for j in range(1, k):
            acc = acc + v_ref[j].astype(F32)
        o_ref[...] = acc

    return _pcall(body, name=name, grid=(r // tr,),
                  in_specs=[pl.BlockSpec((k, tr, c), lambda i: (0, i, 0))],
                  out_specs=pl.BlockSpec((tr, c), lambda i: (i, 0)),
                  out_shape=jax.ShapeDtypeStruct((r, c), F32))(v)


def adamw(w, g, m, v, *, name):
    r, c = w.shape
    tr = _row_tile(r, c, budget=1 << 20)
    m_scale = 1.0 / (1.0 - ADAM_B1 ** ADAM_STEP)
    v_scale = 1.0 / (1.0 - ADAM_B2 ** ADAM_STEP)

    def body(w_ref, g_ref, m_ref, v_ref, d_ref, nm_ref, nv_ref):
        gv = g_ref[...]
        nm = ADAM_B1 * m_ref[...] + (1.0 - ADAM_B1) * gv
        nv = ADAM_B2 * v_ref[...] + (1.0 - ADAM_B2) * (gv * gv)
        nm_ref[...] = nm
        nv_ref[...] = nv
        d_ref[...] = -ADAM_LR * ((nm * m_scale) / (jnp.sqrt(nv * v_scale) + ADAM_EPS) + ADAM_WD * w_ref[...])

    spec = pl.BlockSpec((tr, c), lambda i: (i, 0))
    return _pcall(body, name=name, grid=(r // tr,), in_specs=[spec] * 4, out_specs=[spec] * 3,
                  out_shape=[jax.ShapeDtypeStruct((r, c), F32)] * 3)(w, g, m, v)


def ada_fwd(c_all, w_shard, b_shard, *, name):
    bsz, d = c_all.shape
    ncol = w_shard.shape[1]

    def body(c_ref, w_ref, b_ref, o_ref):
        cv = c_ref[...]
        act = (cv * _sigmoid(cv)).astype(BF16)
        o_ref[...] = _dot(act, w_ref[...].astype(BF16)) + b_ref[...]

    tn = _pick(ncol, (512, 256, 128))
    return _pcall(body, name=name, grid=(ncol // tn,),
                  in_specs=[pl.BlockSpec((bsz, d), lambda j: (0, 0)), pl.BlockSpec((d, tn), lambda j: (0, j)),
                            pl.BlockSpec((1, tn), lambda j: (0, j))],
                  out_specs=pl.BlockSpec((bsz, tn), lambda j: (0, j)),
                  out_shape=jax.ShapeDtypeStruct((bsz, ncol), F32))(c_all, w_shard, b_shard)


def ada_bwd(c_all, d_mod_all, d_mod_cols, *, name):
    bsz, d = c_all.shape
    ncol = d_mod_cols.shape[1]
    nall = d_mod_all.shape[1]

    def body(c_ref, da_ref, dc_ref, gw_ref, gb_ref):
        cv = c_ref[...]
        act = (cv * _sigmoid(cv)).astype(BF16)
        gw_ref[...] = _dot_tn(act, dc_ref[...].astype(BF16))
        gb_ref[...] = _colsum(da_ref[...])

    full = lambda s: pl.BlockSpec(s, lambda: (0,) * len(s))
    return _pcall(body, name=name,
                  in_specs=[full((bsz, d)), full((bsz, nall)), full((bsz, ncol))],
                  out_specs=[full((d, ncol)), full((1, nall))],
                  out_shape=[jax.ShapeDtypeStruct((d, ncol), F32), jax.ShapeDtypeStruct((1, nall), F32)],
                  )(c_all, d_mod_all, d_mod_cols)


WEIGHT_NAMES = ['w_ada', 'b_ada', 'pre_norm1', 'post_norm1', 'w_in', 'b_gate', 'lru_conv_w', 'lru_conv_b', 'lru_wa',
                'lru_ba', 'lru_wx', 'lru_bx', 'lru_lambda', 'w_pa', 'ssd_conv_w', 'ssd_conv_b', 'ssd_dt_bias',
                'ssd_a_log', 'ssd_d', 'ssd_norm_w', 'w_pb', 'w_out', 'pre_norm2', 'post_norm2', 'w_ff1', 'w_ff2']
BIG_NAMES = ['w_in', 'w_pa', 'w_pb', 'w_out', 'w_ff1', 'w_ff2']
COLUMN_SHARDED = ('w_in', 'w_ff1')
SMALL_NAMES = [n for n in WEIGHT_NAMES if n not in BIG_NAMES + ['w_ada', 'b_ada']]
SHARDED_SMALL = ('lru_conv_w', 'ssd_conv_w')
PACK_WIDTH = 1024


def _whole(name, gathered):
    if name in COLUMN_SHARDED:
        return jnp.transpose(gathered, (1, 0, 2)).reshape(gathered.shape[1], N_CHIPS * gathered.shape[2])
    return gathered.reshape(N_CHIPS * gathered.shape[1], gathered.shape[2])


def _by_chip(name, g):
    if name in COLUMN_SHARDED:
        return jnp.transpose(g.reshape(g.shape[0], N_CHIPS, g.shape[1] // N_CHIPS), (1, 0, 2))
    return g.reshape(N_CHIPS, g.shape[0] // N_CHIPS, g.shape[1])


class ChipExchange:
    def __init__(self, shards, core):
        self.shards, self.core = shards, core
        self.pending, self.halves = [], {}

    def weights_bg(self):
        return gather_halves_background(list(self.shards.values()))

    def weights(self, arrived):
        swapped = fill_other_half(arrived, name="weights_from_sibling")
        return {n: _whole(n, g) for n, g in zip(self.shards, swapped)}

    def grads_bg(self, grads):
        self.pending = list(grads)
        by_chip = [_by_chip(n, g) for n, g in grads.items()]
        theirs = send_half_to_sibling(by_chip, name="grads_to_sibling_" + self.pending[0])
        sums = [add_half_to_bf16(self.core, a, b, name="add_cores_" + n)
                for n, a, b in zip(self.pending, by_chip, theirs)]
        return scatter_background(sums)

    def grads_done(self, landed):
        for n, p in zip(self.pending, landed):
            self.halves[n] = sum_blocks(p, name="add_chips_" + n)

    def reduced(self):
        names = list(self.halves)
        return dict(zip(names, join_with_sibling([self.halves[n] for n in names], name="grads_join")))


def _pack(parts):
    flat = jnp.concatenate([p.reshape(-1).astype(F32) for p in parts])
    rows = -(-flat.shape[0] // (PACK_WIDTH * SUBLANES)) * SUBLANES
    return jnp.pad(flat, (0, rows * PACK_WIDTH - flat.shape[0])).reshape(rows, PACK_WIDTH)


def _unpack(packed, shapes):
    flat = packed.reshape(-1)
    out, pos = [], 0
    for s in shapes:
        size = int(np.prod(s))
        out.append(flat[pos:pos + size].reshape(s))
        pos += size
    return out


def kernel(x, c, w_ada, b_ada, pre_norm1, post_norm1, w_in, b_gate, lru_conv_w, lru_conv_b, lru_wa, lru_ba, lru_wx, lru_bx, lru_lambda, w_pa, ssd_conv_w, ssd_conv_b, ssd_dt_bias, ssd_a_log, ssd_d, ssd_norm_w, w_pb, w_out, pre_norm2, post_norm2, w_ff1, w_ff2, loss_target, m_w_ada, m_b_ada, m_pre_norm1, m_post_norm1, m_w_in, m_b_gate, m_lru_conv_w, m_lru_conv_b, m_lru_wa, m_lru_ba, m_lru_wx, m_lru_bx, m_lru_lambda, m_w_pa, m_ssd_conv_w, m_ssd_conv_b, m_ssd_dt_bias, m_ssd_a_log, m_ssd_d, m_ssd_norm_w, m_w_pb, m_w_out, m_pre_norm2, m_post_norm2, m_w_ff1, m_w_ff2, v_w_ada, v_b_ada, v_pre_norm1, v_post_norm1, v_w_in, v_b_gate, v_lru_conv_w, v_lru_conv_b, v_lru_wa, v_lru_ba, v_lru_wx, v_lru_bx, v_lru_lambda, v_w_pa, v_ssd_conv_w, v_ssd_conv_b, v_ssd_dt_bias, v_ssd_a_log, v_ssd_d, v_ssd_norm_w, v_w_pb, v_w_out, v_pre_norm2, v_post_norm2, v_w_ff1, v_w_ff2):
    given = dict(locals())
    bsz, seq, d = x.shape
    my_x, my_y, my_c = lax.axis_index("x"), lax.axis_index("y"), lax.axis_index("c")
    chip = 2 * my_x + my_y
    dev = 2 * chip + my_c
    strip = lambda a: a if a.ndim == 2 else a[0]
    w = {n: strip(given[n]) for n in WEIGHT_NAMES}
    m = {n: strip(given["m_" + n]) for n in WEIGHT_NAMES}
    v = {n: strip(given["v_" + n]) for n in WEIGHT_NAMES}

    first_shapes = [c.shape] + [w[n].shape for n in SHARDED_SMALL]
    first = allgather8(_pack([c] + [w[n] for n in SHARDED_SMALL]), name="gather_c_conv")
    first = first.reshape(N_DEV, -1, PACK_WIDTH)
    per_dev = [_unpack(first[k], first_shapes) for k in range(N_DEV)]
    c_all = jnp.concatenate([p[0] for p in per_dev], axis=0)
    conv_full = {n: jnp.concatenate([per_dev[2 * k][1 + i] for k in range(N_CHIPS)], axis=1)
                 for i, n in enumerate(SHARDED_SMALL)}

    ncol = w["w_ada"].shape[1]
    b_cols = lax.dynamic_slice(b_ada, (0, chip * ncol), (1, ncol))
    mod_cols = ada_fwd(c_all, w["w_ada"], b_cols, name="ada_fwd")
    mod_all = allgather8(mod_cols, name="gather_mod").reshape(N_CHIPS, 2, N_DEV * bsz, ncol)[:, 0]
    mod_all = jnp.transpose(mod_all, (1, 0, 2)).reshape(N_DEV * bsz, N_CHIPS * ncol)
    mod = lax.dynamic_slice(mod_all, (dev * bsz, 0), (bsz, 6 * d)).reshape(bsz, 6, d)
    mod = jnp.pad(mod, ((0, 0), (0, 2), (0, 0)))

    w_in_full = _whole("w_in", gather_weights([w["w_in"].astype(BF16)], name="gather_w_in")[0])
    big = {"w_main": w_in_full[:, :8192],
           "w_dt": jnp.pad(w_in_full[:, 8192:8192 + SSD_HEADS], ((0, 0), (0, LANES - SSD_HEADS))),
           "w_gates": w_in_full[:, 8192 + SSD_HEADS:]}
    small = {n: w[n] for n in SMALL_NAMES}
    small.update(conv_full)
    plan = ChipExchange({n: w[n].astype(BF16) for n in BIG_NAMES if n != "w_in"}, my_c.astype(jnp.int32).reshape(1))

    loss_cols, grad_x, d_mod, small_grads = local_step(x, loss_target, mod, big, small, plan)

    packed = _pack([d_mod, loss_cols] + [small_grads[n] for n in SMALL_NAMES])
    rows = packed.shape[0]
    everyone = allgather8(packed, name="gather_small").reshape(N_DEV, rows, PACK_WIDTH)
    d_mod_all = everyone[:, :bsz * 6].reshape(N_DEV * bsz, 6 * d)
    summed = sum_blocks(everyone, name="sum_small")
    shapes = [d_mod.shape, loss_cols.shape] + [small_grads[n].shape for n in SMALL_NAMES]
    parts = _unpack(summed, shapes)
    loss = jnp.sum(parts[1])
    grads = dict(zip(SMALL_NAMES, parts[2:]))
    for n in SHARDED_SMALL:
        cols = w[n].shape[1]
        grads[n] = lax.dynamic_slice(grads[n], (0, chip * cols), (grads[n].shape[0], cols))
    d_mod_cols = lax.dynamic_slice(d_mod_all, (0, chip * ncol), (N_DEV * bsz, ncol))
    grads["w_ada"], grads["b_ada"] = ada_bwd(c_all, d_mod_all, d_mod_cols, name="ada_bwd")

    grads.update(plan.reduced())

    delta, new_m, new_v = {}, {}, {}
    for n in BIG_NAMES + ["w_ada", "b_ada"]:
        delta[n], new_m[n], new_v[n] = adamw(w[n], grads[n], m[n], v[n], name="adamw_" + n)
    shapes = [w[n].shape for n in SMALL_NAMES]
    pk = lambda src: _pack([src[n] for n in SMALL_NAMES])
    upd = adamw(pk(w), pk(grads), pk(m), pk(v), name="adamw_small")
    for out, packed_out in zip((delta, new_m, new_v), upd):
        out.update(zip(SMALL_NAMES, _unpack(packed_out, shapes)))

    shaped = lambda src: [src[n].reshape(given[n].shape) for n in WEIGHT_NAMES]
    return (loss, grad_x, *shaped(grads), *shaped(delta), *shaped(new_m), *shaped(new_v))
```
